```python
import jax, jax.numpy as jnp
from jax import lax
import numpy as np

D_MODEL = 1024
BATCH = 8
SEQ = 2048
DEPTH = 1

CHUNK = 64
A_WIDTH = D_MODEL // 2
A_HEAD = 64
A_HEADS = A_WIDTH // A_HEAD
D_DECAY_LORA = 64
D_AAA_LORA = 64
D_GATE_LORA = 160
LNX_EPS = 64e-5
B_WIDTH = D_MODEL // 2
B_GROUPS = 4
B_GROUP_CH = B_WIDTH // B_GROUPS
GMLP_BLOCK = 2 * CHUNK
LN_EPS = 1e-5
A_COLS = 3 * A_WIDTH + D_DECAY_LORA + D_AAA_LORA + D_GATE_LORA
B_COLS = 2 * B_WIDTH
GATE_COLS = 2 * D_MODEL
IN_COLS = A_COLS + B_COLS + GATE_COLS
N_GROUPS = 4
EXPERTS_PER_GROUP = 8
N_EXPERTS = N_GROUPS * EXPERTS_PER_GROUP
TOP_K_IN_GROUP = 2
D_EXPERT = 256
NORM_EPS = 1e-6

kernel_name = "hybrid_rwkv7_gmlp_hmoe_block"


def rms_norm(x, g):
    xf = x.astype(jnp.float32)
    y = xf * lax.rsqrt(jnp.mean(xf * xf, axis=-1, keepdims=True) + NORM_EPS)
    return (y * g.astype(jnp.float32)).astype(x.dtype)


def token_shift(p):
    return jnp.pad(p, ((0, 0), (1, 0), (0, 0)))[:, :-1]


def wkv7(r, dec, k, v, a, b):
    bsz, _, nh, n = r.shape

    def step(S, inp):
        r_t, d_t, k_t, v_t, a_t, b_t = inp
        sa = jnp.einsum('bhij,bhj->bhi', S, a_t)
        S = (S * d_t[:, :, None, :]
             + sa[..., :, None] * b_t[..., None, :]
             + v_t[..., :, None] * k_t[..., None, :])
        return S, jnp.einsum('bhij,bhj->bhi', S, r_t)

    xs = (jnp.moveaxis(r, 1, 0), jnp.moveaxis(dec, 1, 0), jnp.moveaxis(k, 1, 0),
          jnp.moveaxis(v, 1, 0), jnp.moveaxis(a, 1, 0), jnp.moveaxis(b, 1, 0))
    S0 = jnp.zeros((bsz, nh, n, n), jnp.float32)
    _, ys = lax.scan(step, S0, xs)
    return jnp.moveaxis(ys, 0, 1)


def rwkv7_branch(p, mu, w0, w2, a0, a2, g2, k_k, k_a, r_k, lnx_g, lnx_b, w_o):
    bsz, t, _ = p.shape
    pf = p.astype(jnp.float32)
    pm = pf + mu * (token_shift(pf) - pf)
    s0 = A_WIDTH
    s1 = 2 * A_WIDTH
    s2 = 3 * A_WIDTH
    s3 = s2 + D_DECAY_LORA
    s4 = s3 + D_AAA_LORA
    r, k, v, xw, xa, xg = jnp.split(pm, [s0, s1, s2, s3, s4], axis=-1)
    w = -jax.nn.softplus(-(w0 + jnp.tanh(xw) @ w2)) - 0.5
    a = jax.nn.sigmoid(a0 + xa @ a2)
    g = jax.nn.sigmoid(xg) @ g2
    hs = (bsz, t, A_HEADS, A_HEAD)
    kk = (k * k_k).reshape(hs)
    kk = kk / jnp.maximum(jnp.sqrt(jnp.sum(kk * kk, axis=-1, keepdims=True)), 1e-12)
    k = k * (1.0 + (a - 1.0) * k_a)
    r_h = r.reshape(hs)
    k_h = k.reshape(hs)
    v_h = v.reshape(hs)
    a_h = a.reshape(hs)
    dec = jnp.exp(-jnp.exp(w)).reshape(hs)
    y = wkv7(r_h, dec, k_h, v_h, -kk, kk * a_h)
    mean = jnp.mean(y, axis=-1, keepdims=True)
    var = jnp.mean(jnp.square(y - mean), axis=-1, keepdims=True)
    y = ((y - mean) * lax.rsqrt(var + LNX_EPS)).reshape(bsz, t, A_WIDTH) * lnx_g + lnx_b
    bonus = jnp.sum(r_h * k_h * r_k, axis=-1, keepdims=True) * v_h
    y = y + bonus.reshape(bsz, t, A_WIDTH)
    return ((y * g).astype(p.dtype)) @ w_o


def gmlp_branch(p, lnv_g, lnv_b, w_s, b_s, w_o):
    bsz, t, _ = p.shape
    z = jax.nn.gelu(p, approximate=False)
    u, v = jnp.split(z, 2, axis=-1)
    vf = v.astype(jnp.float32)
    mean = jnp.mean(vf, axis=-1, keepdims=True)
    var = jnp.mean(jnp.square(vf - mean), axis=-1, keepdims=True)
    vf = (vf - mean) * lax.rsqrt(var + LN_EPS) * lnv_g + lnv_b
    vb = vf.reshape(bsz, t // GMLP_BLOCK, GMLP_BLOCK, B_GROUPS, B_GROUP_CH)
    mask = jnp.tril(jnp.ones((GMLP_BLOCK, GMLP_BLOCK), dtype=bool))
    ws = jnp.where(mask[None], w_s, jnp.zeros_like(w_s)).astype(jnp.float32)
    sv = jnp.einsum('gts,bcsgd->bctgd', ws, vb) + b_s.T.astype(jnp.float32)[:, :, None]
    sv = sv.reshape(bsz, t, B_WIDTH).astype(u.dtype)
    return (u * sv) @ w_o


def hier_moe(h, w_rg, b_rg, w_re, b_re, w_gate, w_up, w_down):
    bsz, t, d = h.shape
    hf = h.reshape(bsz * t, d)
    n_tok = bsz * t
    g_prob = jax.nn.softmax((hf @ w_rg + b_rg).astype(jnp.float32), axis=-1)
    g_p, g_idx = lax.top_k(g_prob, 1)
    e_logits = (jnp.einsum('nd,gde->nge', hf, w_re) + b_re).astype(jnp.float32)
    e_sel = jnp.take_along_axis(e_logits, g_idx[:, :, None], axis=1)[:, 0]
    e_val, e_idx = lax.top_k(e_sel, TOP_K_IN_GROUP)
    e_w = jax.nn.softmax(e_val, axis=-1) * g_p
    w_grp = jnp.sum(jax.nn.one_hot(e_idx, EXPERTS_PER_GROUP, dtype=jnp.float32) * e_w[..., None], axis=1)
    combine = (jax.nn.one_hot(g_idx[:, 0], N_GROUPS, dtype=jnp.float32)[:, :, None]
               * w_grp[:, None, :]).reshape(n_tok, N_EXPERTS).astype(h.dtype)
    y = jnp.zeros_like(hf)
    for e in range(N_EXPERTS):
        act = jax.nn.silu(hf @ w_gate[e]) * (hf @ w_up[e])
        y = y + combine[:, e:e + 1] * (act @ w_down[e])
    return y.reshape(bsz, t, d)


def setup_inputs(seed: int = 0) -> dict:
    key = jax.random.key(seed)
    ks = jax.random.split(key, 32)
    L = DEPTH

    def nrm(k, shape, scale):
        return jax.random.normal(k, shape, jnp.float32) * scale

    return {
        "x": nrm(ks[0], (BATCH, SEQ, D_MODEL), 1.0),
        "norm1_g": 1.0 + nrm(ks[1], (L, D_MODEL), 0.02),
        "w_in": nrm(ks[2], (L, D_MODEL, IN_COLS), D_MODEL ** -0.5),
        "b_gate": nrm(ks[3], (L, GATE_COLS), 0.02),
        "tmix_mu": jax.random.uniform(ks[4], (L, A_COLS), jnp.float32),
        "w0": jax.random.uniform(ks[5], (L, A_WIDTH), jnp.float32, minval=-6.0, maxval=1.0),
        "w2": nrm(ks[6], (L, D_DECAY_LORA, A_WIDTH), 0.1 * D_DECAY_LORA ** -0.5),
        "a0": nrm(ks[7], (L, A_WIDTH), 0.1),
        "a2": nrm(ks[8], (L, D_AAA_LORA, A_WIDTH), 0.1 * D_AAA_LORA ** -0.5),
        "g2": nrm(ks[9], (L, D_GATE_LORA, A_WIDTH), D_GATE_LORA ** -0.5),
        "k_k": 0.85 + nrm(ks[10], (L, A_WIDTH), 0.02),
        "k_a": 1.0 + nrm(ks[11], (L, A_WIDTH), 0.02),
        "r_k": nrm(ks[12], (L, A_HEADS, A_HEAD), 0.1),
        "lnx_g": 1.0 + nrm(ks[13], (L, A_WIDTH), 0.02),
        "lnx_b": nrm(ks[14], (L, A_WIDTH), 0.02),
        "w_oA": nrm(ks[15], (L, A_WIDTH, D_MODEL), A_WIDTH ** -0.5),
        "lnv_g": 1.0 + nrm(ks[16], (L, B_WIDTH), 0.02),
        "lnv_b": nrm(ks[17], (L, B_WIDTH), 0.02),
        "w_s": nrm(ks[18], (L, B_GROUPS, GMLP_BLOCK, GMLP_BLOCK), GMLP_BLOCK ** -0.5),
        "b_s": 1.0 + nrm(ks[19], (L, B_GROUPS, GMLP_BLOCK), 0.02),
        "w_oB": nrm(ks[20], (L, B_WIDTH, D_MODEL), B_WIDTH ** -0.5),
        "w_out": nrm(ks[21], (L, D_MODEL, D_MODEL), D_MODEL ** -0.5),
        "norm2_g": 1.0 + nrm(ks[22], (L, D_MODEL), 0.02),
        "w_rg": nrm(ks[23], (L, D_MODEL, N_GROUPS), D_MODEL ** -0.5),
        "b_rg": nrm(ks[24], (L, N_GROUPS), 0.01),
        "w_re": nrm(ks[25], (L, N_GROUPS, D_MODEL, EXPERTS_PER_GROUP), D_MODEL ** -0.5),
        "b_re": nrm(ks[26], (L, N_GROUPS, EXPERTS_PER_GROUP), 0.01),
        "w_e_gate": nrm(ks[27], (L, N_EXPERTS, D_MODEL, D_EXPERT), D_MODEL ** -0.5),
        "w_e_up": nrm(ks[28], (L, N_EXPERTS, D_MODEL, D_EXPERT), D_MODEL ** -0.5),
        "w_e_down": nrm(ks[29], (L, N_EXPERTS, D_EXPERT, D_MODEL), D_EXPERT ** -0.5),
        "final_g": 1.0 + nrm(ks[30], (D_MODEL,), 0.02),
    }


def reference(x, norm1_g, w_in, b_gate, tmix_mu, w0, w2, a0, a2, g2, k_k, k_a, r_k,
              lnx_g, lnx_b, w_oA, lnv_g, lnv_b, w_s, b_s, w_oB, w_out, norm2_g,
              w_rg, b_rg, w_re, b_re, w_e_gate, w_e_up, w_e_down, final_g):
    for l in range(DEPTH):
        h = rms_norm(x, norm1_g[l])
        p = h @ w_in[l]
        p_a = p[..., :A_COLS]
        p_b = p[..., A_COLS:A_COLS + B_COLS]
        p_g = p[..., A_COLS + B_COLS:]
        y_a = rwkv7_branch(p_a, tmix_mu[l], w0[l], w2[l], a0[l], a2[l], g2[l], k_k[l], k_a[l],
                           r_k[l], lnx_g[l], lnx_b[l], w_oA[l])
        y_b = gmlp_branch(p_b, lnv_g[l], lnv_b[l], w_s[l], b_s[l], w_oB[l])
        gates = jax.nn.sigmoid(p_g + b_gate[l])
        g_a, g_b = jnp.split(gates, 2, axis=-1)
        x = x + (g_a * y_a + g_b * y_b) @ w_out[l]
        x = x + hier_moe(rms_norm(x, norm2_g[l]), w_rg[l], b_rg[l], w_re[l], b_re[l],
                         w_e_gate[l], w_e_up[l], w_e_down[l])
    return rms_norm(x, final_g)
```

```python
import functools

import jax
import jax.numpy as jnp
from jax import lax
from jax.experimental import pallas as pl
from jax.experimental.pallas import tpu as pltpu

F32 = jnp.float32
BF16 = jnp.bfloat16

D_MODEL = 1024
A_WIDTH = 512
A_HEAD = 64
D_DECAY_LORA = 64
D_AAA_LORA = 64
D_GATE_LORA = 160
B_WIDTH = 512
B_GROUPS = 4
B_GROUP_CH = 128
GMLP_BLOCK = 128
N_GROUPS = 4
EXPERTS_PER_GROUP = 8
N_EXPERTS = 32
D_EXPERT = 256
NORM_EPS = 1e-6
LN_EPS = 1e-5
LNX_EPS = 64e-5

LANES = 128
LORA_PAD = 512
XW_OFF, XA_OFF, XG_OFF = 0, 128, 256
A_PROJ = 3 * A_WIDTH + LORA_PAD
WKV_CHUNK = 64
HEAD_PAIR = 2 * A_HEAD
VMEM_LIMIT = 48 * 1024 * 1024


def _rms(x, g):
    return x * lax.rsqrt(jnp.mean(x * x, axis=-1, keepdims=True) + NORM_EPS) * g


def _dot(a, b):
    return jnp.dot(a, b, preferred_element_type=F32)


def _dot_nt(a, b):
    return lax.dot_general(a, b, (((1,), (1,)), ((), ())), preferred_element_type=F32)


def _split2(x):
    hi = x.astype(BF16)
    lo = (x - hi.astype(F32)).astype(BF16)
    return hi, lo


def _split3(x):
    hi = x.astype(BF16)
    r1 = x - hi.astype(F32)
    mid = r1.astype(BF16)
    lo = (r1 - mid.astype(F32)).astype(BF16)
    return hi, mid, lo


def _head_sum(x, ones_bd):
    hi, lo = _split2(x)
    return _dot(hi, ones_bd) + _dot(lo, ones_bd)


def _rwkv_prep_kernel(tiles_per_seq, x_ref, xprev_ref, g1_ref, wa_ref, mu_ref, w0_ref,
                      w2_ref, a0_ref, a2_ref, g2_ref, kk_ref, ka_ref, ones_ref,
                      r_out, k_out, v_out, a_out, b_out, ld_out, g_out):
    i = pl.program_id(0)
    g1 = g1_ref[...]
    wa = wa_ref[...]
    h = _rms(x_ref[...], g1).astype(BF16)
    p = _dot(h, wa)
    hp = _rms(xprev_ref[...], g1).astype(BF16)
    prev = _dot(hp, wa)[7:8, :]
    prev = jnp.where(i % tiles_per_seq == 0, 0.0, prev)
    row = lax.broadcasted_iota(jnp.int32, p.shape, 0)
    shifted = jnp.where(row == 0, prev, pltpu.roll(p, 1, axis=0))
    pm = p + mu_ref[...] * (shifted - p)

    r = pm[:, 0:A_WIDTH]
    k = pm[:, A_WIDTH:2 * A_WIDTH]
    v = pm[:, 2 * A_WIDTH:3 * A_WIDTH]
    lora = pm[:, 3 * A_WIDTH:]
    xw = lora[:, XW_OFF:XW_OFF + LANES]
    xa = lora[:, XA_OFF:XA_OFF + LANES]
    xg = lora[:, XG_OFF:]

    z = -(w0_ref[...] + _dot(jnp.tanh(xw).astype(BF16), w2_ref[...]))
    softplus = jnp.maximum(z, 0.0) + jnp.log(1.0 + jnp.exp(-jnp.abs(z)))
    w = -softplus - 0.5
    a_lr = jax.nn.sigmoid(a0_ref[...] + _dot(xa.astype(BF16), a2_ref[...]))
    g = _dot(jax.nn.sigmoid(xg).astype(BF16), g2_ref[...])

    kk = k * kk_ref[...]
    norm = jnp.sqrt(_head_sum(kk * kk, ones_ref[...]))
    kk = kk / jnp.maximum(norm, 1e-12)

    r_out[...] = r
    k_out[...] = k * (1.0 + (a_lr - 1.0) * ka_ref[...])
    v_out[...] = v
    a_out[...] = -kk
    b_out[...] = kk * a_lr
    ld_out[...] = -jnp.exp(w)
    g_out[...] = g


def _gmlp_kernel(x_ref, g1_ref, wb_ref, wg_ref, bg_ref, lng_ref, lnb_ref, ws_ref, bs_ref,
                 wo_ref, ybg_out, ga_out):
    tm = x_ref.shape[0]
    h = _rms(x_ref[...], g1_ref[...]).astype(BF16)
    pb = _dot(h, wb_ref[...])
    z = 0.5 * pb * (1.0 + lax.erf(pb * (2.0 ** -0.5)))
    u = z[:, :B_WIDTH]
    v = z[:, B_WIDTH:]
    mean = jnp.mean(v, axis=-1, keepdims=True)
    vc = v - mean
    var = jnp.mean(vc * vc, axis=-1, keepdims=True)
    vn = (vc * lax.rsqrt(var + LN_EPS) * lng_ref[...] + lnb_ref[...]).astype(BF16)

    tri = (lax.broadcasted_iota(jnp.int32, (GMLP_BLOCK, GMLP_BLOCK), 0)
           >= lax.broadcasted_iota(jnp.int32, (GMLP_BLOCK, GMLP_BLOCK), 1))
    bs = bs_ref[...]
    rows = []
    for blk in range(tm // GMLP_BLOCK):
        cols = []
        for grp in range(B_GROUPS):
            ws = jnp.where(tri, ws_ref[grp], 0.0).astype(BF16)
            vb = vn[blk * GMLP_BLOCK:(blk + 1) * GMLP_BLOCK,
                    grp * B_GROUP_CH:(grp + 1) * B_GROUP_CH]
            cols.append(_dot(ws, vb))
        rows.append(jnp.concatenate(cols, axis=1) + bs)
    sv = jnp.concatenate(rows, axis=0)
    yb = _dot((u * sv).astype(BF16), wo_ref[...])
    gates = jax.nn.sigmoid(_dot(h, wg_ref[...]) + bg_ref[...])
    ga_out[...] = gates[:, :D_MODEL]
    ybg_out[...] = gates[:, D_MODEL:] * yb


def _wkv_kernel(r_ref, k_ref, v_ref, a_ref, b_ref, ld_ref, g_ref, rk_ref, lng_ref, lnb_ref,
                ones_ref, o_ref, st_ref):
    C = WKV_CHUNK
    n_pairs = A_WIDTH // HEAD_PAIR

    @pl.when(pl.program_id(1) == 0)
    def _():
        st_ref[...] = jnp.zeros_like(st_ref)

    row = lax.broadcasted_iota(jnp.int32, (C, HEAD_PAIR), 0)
    src = lax.broadcasted_iota(jnp.int32, (C, HEAD_PAIR), 1) & (C - 1)
    incl = src <= row
    strict = src < row
    eye_pair = jnp.where(src == row, 1.0, 0.0)
    bd_mask = ((lax.broadcasted_iota(jnp.int32, (HEAD_PAIR, HEAD_PAIR), 0) >= A_HEAD)
               == (lax.broadcasted_iota(jnp.int32, (HEAD_PAIR, HEAD_PAIR), 1) >= A_HEAD))
    tri_c = (lax.broadcasted_iota(jnp.int32, (C, C), 0)
             >= lax.broadcasted_iota(jnp.int32, (C, C), 1)).astype(BF16)
    ones_bd = ones_ref[...]

    def bd(x):
        xb = x.astype(BF16)
        return jnp.where(bd_mask, jnp.concatenate([xb, xb], axis=0), jnp.zeros((), BF16))

    def chunk(c, carry):
        sl = pl.ds(pl.multiple_of(c * C, C), C)
        ld = ld_ref[sl, :]
        hi, mid, lo = _split3(ld)
        cl = _dot(tri_c, hi) + _dot(tri_c, mid) + _dot(tri_c, lo)
        cl_end = cl[C - 1:C, :]
        r = r_ref[sl, :]
        k = k_ref[sl, :]
        v = v_ref[sl, :]
        a = a_ref[sl, :]
        b = b_ref[sl, :]
        d_in = jnp.exp(cl)
        d_inv = jnp.exp(-cl)
        d_tail = jnp.exp(cl_end - cl)
        rt = r * d_in
        kt = k * d_inv
        bt = b * d_inv
        at = a * jnp.exp(cl - ld)
        kd = k * d_tail
        bdk = b * d_tail
        d_end = jnp.exp(cl_end)

        ys = []
        for p in range(n_pairs):
            cs = slice(p * HEAD_PAIR, (p + 1) * HEAD_PAIR)
            rt_p = rt[:, cs].astype(BF16)
            at_p = at[:, cs]
            v_p = v[:, cs]
            lhs = jnp.concatenate([rt_p, at_p.astype(BF16)], axis=0)
            rhs = jnp.concatenate([bd(kt[:, cs]), bd(bt[:, cs])], axis=0)
            amat = _dot_nt(lhs, rhs)
            a_rk = jnp.where(incl, amat[:C, :HEAD_PAIR], 0.0).astype(BF16)
            a_rb = jnp.where(incl, amat[:C, HEAD_PAIR:], 0.0).astype(BF16)
            l_ak = jnp.where(strict, amat[C:, :HEAD_PAIR], 0.0)
            l_ab = jnp.where(strict, amat[C:, HEAD_PAIR:], 0.0)

            t_mat = eye_pair + l_ab
            q = _dot(l_ab.astype(BF16), bd(l_ab))
            n_sq = 1
            while 2 * n_sq < C // 2:
                both = _dot(q.astype(BF16), jnp.concatenate([bd(q), bd(t_mat)], axis=1))
                q = both[:, :HEAD_PAIR]
                t_mat = t_mat + both[:, HEAD_PAIR:]
                n_sq *= 2
            t_mat = t_mat + _dot(q.astype(BF16), bd(t_mat))
            tal = _dot(t_mat.astype(BF16), jnp.concatenate([bd(at_p), bd(l_ak)], axis=1))
            ta = tal[:, :HEAD_PAIR].astype(BF16)
            tl = tal[:, HEAD_PAIR:].astype(BF16)

            st = st_ref[p]
            st_b = st.astype(BF16)
            bd_v = bd(v_p)
            u = _dot(ta, st_b) + _dot(tl, bd_v)
            ys.append(_dot(rt_p, st_b) + _dot(a_rb, bd(u)) + _dot(a_rk, bd_v))
            bk_t = jnp.concatenate([bdk[:, cs], kd[:, cs]], axis=0).T.astype(BF16)
            uv = jnp.concatenate([u, v_p], axis=0).astype(BF16)
            d_col = jnp.broadcast_to(d_end[:, cs], (HEAD_PAIR, HEAD_PAIR)).T
            st_ref[p] = d_col * st + jnp.where(bd_mask, _dot(bk_t, uv), 0.0)

        y = jnp.concatenate(ys, axis=1)
        mean = _head_sum(y, ones_bd) * (1.0 / A_HEAD)
        yc = y - mean
        var = _head_sum(yc * yc, ones_bd) * (1.0 / A_HEAD)
        yn = yc * lax.rsqrt(var + LNX_EPS) * lng_ref[...] + lnb_ref[...]
        bonus = _head_sum(r * k * rk_ref[...], ones_bd) * v
        o_ref[sl, :] = (yn + bonus) * g_ref[sl, :]
        return carry

    lax.fori_loop(0, r_ref.shape[0] // C, chunk, 0)


def _post_kernel(x_ref, ya_ref, ga_ref, ybg_ref, woa_ref, wout_ref, g2_ref, wr_hi_ref,
                 wr_lo_ref, br_ref, x2_out, h2_out, comb_out):
    y_a = _dot(ya_ref[...].astype(BF16), woa_ref[...])
    mix = ga_ref[...] * y_a + ybg_ref[...]
    x2 = x_ref[...] + _dot(mix.astype(BF16), wout_ref[...])
    x2_out[...] = x2
    h2 = _rms(x2, g2_ref[...])
    h2_out[...] = h2.astype(BF16)

    hi, lo = _split2(h2)
    wr_hi = wr_hi_ref[...]
    logits = _dot(hi, wr_hi) + _dot(lo, wr_hi) + _dot(hi, wr_lo_ref[...]) + br_ref[...]
    lane = lax.broadcasted_iota(jnp.int32, logits.shape, 1)
    neg = jnp.float32(-jnp.inf)
    big = jnp.int32(LANES)
    is_grp = (lane >= N_EXPERTS) & (lane < N_EXPERTS + N_GROUPS)
    gl = jnp.where(is_grp, logits, neg)
    gmax = jnp.max(gl, axis=-1, keepdims=True)
    g_p = 1.0 / jnp.sum(jnp.exp(gl - gmax), axis=-1, keepdims=True)
    g_idx = jnp.min(jnp.where(gl == gmax, lane, big), axis=-1, keepdims=True) - N_EXPERTS
    lo_lane = g_idx * EXPERTS_PER_GROUP
    in_grp = (lane >= lo_lane) & (lane < lo_lane + EXPERTS_PER_GROUP)
    el = jnp.where(in_grp, logits, neg)
    e1 = jnp.max(el, axis=-1, keepdims=True)
    i1 = jnp.min(jnp.where(el == e1, lane, big), axis=-1, keepdims=True)
    el2 = jnp.where(lane == i1, neg, el)
    e2 = jnp.max(el2, axis=-1, keepdims=True)
    i2 = jnp.min(jnp.where(el2 == e2, lane, big), axis=-1, keepdims=True)
    t = jnp.exp(e2 - e1)
    w1 = g_p / (1.0 + t)
    w2 = g_p * t / (1.0 + t)
    comb_out[...] = jnp.where(lane == i1, w1, 0.0) + jnp.where(lane == i2, w2, 0.0)


def _moe_kernel(x2_ref, h2_ref, comb_ref, wgu_ref, wd_ref, gf_ref, o_ref, acc_ref):
    e = pl.program_id(1)

    @pl.when(e == 0)
    def _():
        acc_ref[...] = jnp.zeros_like(acc_ref)

    gu = _dot(h2_ref[...], wgu_ref[0])
    gate = gu[:, :D_EXPERT]
    act = gate * jax.nn.sigmoid(gate) * gu[:, D_EXPERT:]
    sel = (lax.broadcasted_iota(jnp.int32, (LANES, D_EXPERT), 0) == e).astype(BF16)
    c_hi, c_lo = _split2(comb_ref[...])
    c_e = _dot(c_hi, sel) + _dot(c_lo, sel)
    acc_ref[...] += _dot((act * c_e).astype(BF16), wd_ref[0])

    @pl.when(e == pl.num_programs(1) - 1)
    def _():
        o_ref[...] = _rms(x2_ref[...] + acc_ref[...], gf_ref[...])


def _full(shape):
    return pl.BlockSpec(shape, lambda *_: (0,) * len(shape))


def _params(sem):
    return pltpu.CompilerParams(dimension_semantics=sem, vmem_limit_bytes=VMEM_LIMIT)


def _place(cols, parts):
    out = jnp.zeros((parts[0][1].shape[0], cols), parts[0][1].dtype)
    for off, arr in parts:
        out = lax.dynamic_update_slice(out, arr, (0, off))
    return out


def kernel(x, norm1_g, w_in, b_gate, tmix_mu, w0, w2, a0, a2, g2, k_k, k_a, r_k, lnx_g, lnx_b,
           w_oA, lnv_g, lnv_b, w_s, b_s, w_oB, w_out, norm2_g, w_rg, b_rg, w_re, b_re,
           w_e_gate, w_e_up, w_e_down, final_g):
    bsz, seq, d = x.shape
    n_tok = bsz * seq
    depth = norm1_g.shape[0]
    assert depth == 1, "the moe kernel fuses the final norm, so it must be the last layer"
    xf = x.reshape(n_tok, d)

    s_rkv = 3 * A_WIDTH
    s_w = s_rkv + D_DECAY_LORA
    s_a = s_w + D_AAA_LORA
    a_cols = s_a + D_GATE_LORA
    b_cols = 2 * B_WIDTH

    ones_bd = (jnp.arange(A_WIDTH)[:, None] // A_HEAD
               == jnp.arange(A_WIDTH)[None, :] // A_HEAD).astype(BF16)

    tm_a = 256
    tm_b = 256
    tm_p = 512
    tm_e = 1024
    tb = 256

    for l in range(depth):
        wl = w_in[l]
        lora_w = _place(LORA_PAD, [(XW_OFF, wl[:, s_rkv:s_w]), (XA_OFF, wl[:, s_w:s_a]),
                                   (XG_OFF, wl[:, s_a:a_cols])])
        w_a = jnp.concatenate([wl[:, :s_rkv], lora_w], axis=1).astype(BF16)
        mu = tmix_mu[l][None, :]
        mu_a = jnp.concatenate(
            [mu[:, :s_rkv], _place(LORA_PAD, [(XW_OFF, mu[:, s_rkv:s_w]), (XA_OFF, mu[:, s_w:s_a]),
                                              (XG_OFF, mu[:, s_a:a_cols])])], axis=1)
        w2p = jnp.pad(w2[l], ((0, LANES - D_DECAY_LORA), (0, 0))).astype(BF16)
        a2p = jnp.pad(a2[l], ((0, LANES - D_AAA_LORA), (0, 0))).astype(BF16)
        g2p = jnp.pad(g2[l], ((0, LORA_PAD - XG_OFF - D_GATE_LORA), (0, 0))).astype(BF16)
        g1 = norm1_g[l][None, :]

        row512 = lambda i: (i, 0)
        tok_a = pl.BlockSpec((tm_a, A_WIDTH), row512)
        vec_a = _full((1, A_WIDTH))
        outs = pl.pallas_call(
            functools.partial(_rwkv_prep_kernel, seq // tm_a),
            grid=(n_tok // tm_a,),
            in_specs=[pl.BlockSpec((tm_a, d), row512),
                      pl.BlockSpec((8, d), lambda i: (jnp.maximum(i * (tm_a // 8) - 1, 0), 0)),
                      _full((1, d)), _full((d, A_PROJ)), _full((1, A_PROJ)), vec_a,
                      _full((LANES, A_WIDTH)), vec_a, _full((LANES, A_WIDTH)),
                      _full((LORA_PAD - XG_OFF, A_WIDTH)), vec_a, vec_a,
                      _full((A_WIDTH, A_WIDTH))],
            out_specs=[tok_a] * 7,
            out_shape=[jax.ShapeDtypeStruct((n_tok, A_WIDTH), F32)] * 7,
            compiler_params=_params(("parallel",)),
            name="rwkv_prep",
        )(xf, xf, g1, w_a, mu_a, w0[l][None, :], w2p, a0[l][None, :], a2p, g2p,
          k_k[l][None, :], k_a[l][None, :], ones_bd)
        r_, k_, v_, a_, b_, ld_, gg_ = outs

        w_b = wl[:, a_cols:a_cols + b_cols].astype(BF16)
        w_g = wl[:, a_cols + b_cols:].astype(BF16)
        bs_full = jnp.repeat(b_s[l].T, B_GROUP_CH, axis=1)
        tok_d = pl.BlockSpec((tm_b, d), row512)
        ybg, ga = pl.pallas_call(
            _gmlp_kernel,
            grid=(n_tok // tm_b,),
            in_specs=[tok_d, _full((1, d)), _full((d, b_cols)), _full((d, 2 * d)),
                      _full((1, 2 * d)), _full((1, B_WIDTH)), _full((1, B_WIDTH)),
                      _full((B_GROUPS, GMLP_BLOCK, GMLP_BLOCK)), _full((GMLP_BLOCK, B_WIDTH)),
                      _full((B_WIDTH, d))],
            out_specs=[tok_d, tok_d],
            out_shape=[jax.ShapeDtypeStruct((n_tok, d), F32)] * 2,
            compiler_params=_params(("parallel",)),
            name="gmlp",
        )(xf, g1, w_b, w_g, b_gate[l][None, :], lnv_g[l][None, :], lnv_b[l][None, :],
          w_s[l], bs_full, w_oB[l].astype(BF16))

        n_tb = seq // tb
        tok_w = pl.BlockSpec((tb, A_WIDTH), lambda bi, ti: (bi * n_tb + ti, 0))
        vec_w = _full((1, A_WIDTH))
        ya_in = pl.pallas_call(
            _wkv_kernel,
            grid=(bsz, n_tb),
            in_specs=[tok_w] * 7 + [vec_w, vec_w, vec_w, _full((A_WIDTH, A_WIDTH))],
            out_specs=tok_w,
            out_shape=jax.ShapeDtypeStruct((n_tok, A_WIDTH), F32),
            scratch_shapes=[pltpu.VMEM((A_WIDTH // HEAD_PAIR, HEAD_PAIR, HEAD_PAIR), F32)],
            compiler_params=_params(("parallel", "arbitrary")),
            name="wkv",
        )(r_, k_, v_, a_, b_, ld_, gg_, r_k[l].reshape(1, A_WIDTH), lnx_g[l][None, :],
          lnx_b[l][None, :], ones_bd)

        w_r = _place(LANES, [(0, jnp.transpose(w_re[l], (1, 0, 2)).reshape(d, N_EXPERTS)),
                             (N_EXPERTS, w_rg[l])])
        b_r = _place(LANES, [(0, b_re[l].reshape(1, N_EXPERTS)), (N_EXPERTS, b_rg[l][None, :])])
        wr_hi = w_r.astype(BF16)
        wr_lo = (w_r - wr_hi.astype(F32)).astype(BF16)
        tok_p = pl.BlockSpec((tm_p, d), row512)
        x2, h2, comb = pl.pallas_call(
            _post_kernel,
            grid=(n_tok // tm_p,),
            in_specs=[tok_p, pl.BlockSpec((tm_p, A_WIDTH), row512), tok_p, tok_p,
                      _full((A_WIDTH, d)), _full((d, d)), _full((1, d)), _full((d, LANES)),
                      _full((d, LANES)), _full((1, LANES))],
            out_specs=[tok_p, tok_p, pl.BlockSpec((tm_p, LANES), row512)],
            out_shape=[jax.ShapeDtypeStruct((n_tok, d), F32),
                       jax.ShapeDtypeStruct((n_tok, d), BF16),
                       jax.ShapeDtypeStruct((n_tok, LANES), F32)],
            compiler_params=_params(("parallel",)),
            name="post",
        )(xf, ya_in, ga, ybg, w_oA[l].astype(BF16), w_out[l].astype(BF16),
          norm2_g[l][None, :], wr_hi, wr_lo, b_r)

        w_gu = jnp.concatenate([w_e_gate[l], w_e_up[l]], axis=2).astype(BF16)
        gf = final_g[None, :]
        tok_e = pl.BlockSpec((tm_e, d), lambda i, e: (i, 0))
        xf = pl.pallas_call(
            _moe_kernel,
            grid=(n_tok // tm_e, N_EXPERTS),
            in_specs=[tok_e, tok_e, pl.BlockSpec((tm_e, LANES), lambda i, e: (i, 0)),
                      pl.BlockSpec((1, d, 2 * D_EXPERT), lambda i, e: (e, 0, 0)),
                      pl.BlockSpec((1, D_EXPERT, d), lambda i, e: (e, 0, 0)),
                      _full((1, d))],
            out_specs=tok_e,
            out_shape=jax.ShapeDtypeStruct((n_tok, d), F32),
            scratch_shapes=[pltpu.VMEM((tm_e, d), F32)],
            compiler_params=_params(("parallel", "arbitrary")),
            name="moe",
        )(x2, h2, comb, w_gu, w_e_down[l].astype(BF16), gf)

    return xf.reshape(bsz, seq, d)
```

```python
import functools

import jax
import jax.numpy as jnp
from jax import lax
from jax.experimental import pallas as pl
from jax.experimental.pallas import tpu as pltpu

F32 = jnp.float32
BF16 = jnp.bfloat16

D_MODEL = 1024
A_WIDTH = 512
A_HEAD = 64
D_DECAY_LORA = 64
D_AAA_LORA = 64
D_GATE_LORA = 160
B_WIDTH = 512
B_GROUPS = 4
B_GROUP_CH = 128
GMLP_BLOCK = 128
N_GROUPS = 4
EXPERTS_PER_GROUP = 8
N_EXPERTS = 32
D_EXPERT = 256
NORM_EPS = 1e-6
LN_EPS = 1e-5
LNX_EPS = 64e-5

LANES = 128
LORA_PAD = 512
XW_OFF, XA_OFF, XG_OFF = 0, 128, 256
A_PROJ = 3 * A_WIDTH + LORA_PAD
WKV_CHUNK = 64
HEAD_PAIR = 2 * A_HEAD
WKV_SEQS = 2
WKV_TOKENS = 256
WKV_PREP_GROUP = 4
VMEM_LIMIT = 48 * 1024 * 1024


def _rms(x, g):
    return x * lax.rsqrt(jnp.mean(x * x, axis=-1, keepdims=True) + NORM_EPS) * g


def _dot(a, b):
    return jnp.dot(a, b, preferred_element_type=F32)


def _dot_nt(a, b):
    return lax.dot_general(a, b, (((1,), (1,)), ((), ())), preferred_element_type=F32)


def _split2(x):
    hi = x.astype(BF16)
    lo = (x - hi.astype(F32)).astype(BF16)
    return hi, lo


def _split3(x):
    hi = x.astype(BF16)
    r1 = x - hi.astype(F32)
    mid = r1.astype(BF16)
    lo = (r1 - mid.astype(F32)).astype(BF16)
    return hi, mid, lo


def _head_sum(x, ones_bd):
    hi, lo = _split2(x)
    return _dot(hi, ones_bd) + _dot(lo, ones_bd)


def _pair_head_sums(xs, ones_pair):
    n_tiles = A_WIDTH // HEAD_PAIR
    rows = xs[0].shape[0]
    tiles = [x[:, t * HEAD_PAIR:(t + 1) * HEAD_PAIR].astype(BF16) for x in xs for t in range(n_tiles)]
    sums = _dot(jnp.concatenate(tiles, axis=0), ones_pair)
    return [jnp.concatenate([sums[(i * n_tiles + t) * rows:(i * n_tiles + t + 1) * rows]
                             for t in range(n_tiles)], axis=1) for i in range(len(xs))]


def _rwkv_prep_kernel(tiles_per_seq, x_ref, xprev_ref, g1_ref, wa_ref, mu_ref, w0_ref,
                      w2_ref, a0_ref, a2_ref, g2_ref, kk_ref, ka_ref, ones_ref,
                      r_out, k_out, v_out, a_out, b_out, ld_out, g_out):
    i = pl.program_id(0)
    g1 = g1_ref[...]
    wa = wa_ref[...]
    h = _rms(x_ref[...], g1).astype(BF16)
    p = _dot(h, wa)
    hp = _rms(xprev_ref[...], g1).astype(BF16)
    prev = _dot(hp, wa)[7:8, :]
    prev = jnp.where(i % tiles_per_seq == 0, 0.0, prev)
    row = lax.broadcasted_iota(jnp.int32, p.shape, 0)
    shifted = jnp.where(row == 0, prev, pltpu.roll(p, 1, axis=0))
    pm = p + mu_ref[...] * (shifted - p)

    r = pm[:, 0:A_WIDTH]
    k = pm[:, A_WIDTH:2 * A_WIDTH]
    v = pm[:, 2 * A_WIDTH:3 * A_WIDTH]
    lora = pm[:, 3 * A_WIDTH:]
    xw = lora[:, XW_OFF:XW_OFF + LANES]
    xa = lora[:, XA_OFF:XA_OFF + LANES]
    xg = lora[:, XG_OFF:]

    z = -(w0_ref[...] + _dot(jnp.tanh(xw).astype(BF16), w2_ref[...]))
    softplus = jnp.maximum(z, 0.0) + jnp.log(1.0 + jnp.exp(-jnp.abs(z)))
    w = -softplus - 0.5
    a_lr = jax.nn.sigmoid(a0_ref[...] + _dot(xa.astype(BF16), a2_ref[...]))
    g = _dot(jax.nn.sigmoid(xg).astype(BF16), g2_ref[...])

    kk = k * kk_ref[...]
    norm = jnp.sqrt(_head_sum(kk * kk, ones_ref[...]))
    kk = kk / jnp.maximum(norm, 1e-12)

    r_out[...] = r
    k_out[...] = k * (1.0 + (a_lr - 1.0) * ka_ref[...])
    v_out[...] = v
    a_out[...] = -kk
    b_out[...] = kk * a_lr
    ld_out[...] = -jnp.exp(w)
    g_out[...] = g


def _gmlp_kernel(x_ref, g1_ref, wb_ref, wg_ref, bg_ref, lng_ref, lnb_ref, ws_ref, bs_ref,
                 wo_ref, ybg_out, ga_out):
    tm = x_ref.shape[0]
    h = _rms(x_ref[...], g1_ref[...]).astype(BF16)
    pb = _dot(h, wb_ref[...])
    z = 0.5 * pb * (1.0 + lax.erf(pb * (2.0 ** -0.5)))
    u = z[:, :B_WIDTH]
    v = z[:, B_WIDTH:]
    mean = jnp.mean(v, axis=-1, keepdims=True)
    vc = v - mean
    var = jnp.mean(vc * vc, axis=-1, keepdims=True)
    vn = (vc * lax.rsqrt(var + LN_EPS) * lng_ref[...] + lnb_ref[...]).astype(BF16)

    tri = (lax.broadcasted_iota(jnp.int32, (GMLP_BLOCK, GMLP_BLOCK), 0)
           >= lax.broadcasted_iota(jnp.int32, (GMLP_BLOCK, GMLP_BLOCK), 1))
    bs = bs_ref[...]
    rows = []
    for blk in range(tm // GMLP_BLOCK):
        cols = []
        for grp in range(B_GROUPS):
            ws = jnp.where(tri, ws_ref[grp], 0.0).astype(BF16)
            vb = vn[blk * GMLP_BLOCK:(blk + 1) * GMLP_BLOCK,
                    grp * B_GROUP_CH:(grp + 1) * B_GROUP_CH]
            cols.append(_dot(ws, vb))
        rows.append(jnp.concatenate(cols, axis=1) + bs)
    sv = jnp.concatenate(rows, axis=0)
    yb = _dot((u * sv).astype(BF16), wo_ref[...])
    gates = jax.nn.sigmoid(_dot(h, wg_ref[...]) + bg_ref[...])
    ga_out[...] = gates[:, :D_MODEL]
    ybg_out[...] = gates[:, D_MODEL:] * yb


def _wkv_kernel(r_ref, k_ref, v_ref, a_ref, b_ref, ld_ref, g_ref, rk_ref, lng_ref, lnb_ref,
                ones_ref, o_ref, st_ref, ta_s, tl_s, arb_s, ark_s, rt_s, bkt_s, dcol_s):
    C = WKV_CHUNK
    bb, tb, _ = r_ref.shape
    n_chunks = tb // C
    n_pairs = A_WIDTH // HEAD_PAIR

    @pl.when(pl.program_id(1) == 0)
    def _():
        st_ref[...] = jnp.zeros_like(st_ref)

    row = lax.broadcasted_iota(jnp.int32, (C, HEAD_PAIR), 0)
    src = lax.broadcasted_iota(jnp.int32, (C, HEAD_PAIR), 1) & (C - 1)
    incl = src <= row
    strict = src < row
    eye_pair = jnp.where(src == row, 1.0, 0.0)
    bd_mask = ((lax.broadcasted_iota(jnp.int32, (HEAD_PAIR, HEAD_PAIR), 0) >= A_HEAD)
               == (lax.broadcasted_iota(jnp.int32, (HEAD_PAIR, HEAD_PAIR), 1) >= A_HEAD))
    tri_c = (lax.broadcasted_iota(jnp.int32, (C, C), 0)
             >= lax.broadcasted_iota(jnp.int32, (C, C), 1)).astype(BF16)
    ones_pair = ones_ref[...]
    pair_cols = [slice(p * HEAD_PAIR, (p + 1) * HEAD_PAIR) for p in range(n_pairs)]

    def bd(x):
        xb = x.astype(BF16)
        return jnp.where(bd_mask, jnp.concatenate([xb, xb], axis=0), jnp.zeros((), BF16))

    def prep(it, carry):
        where, lhs, rhs, ats = [], [], [], []
        for j in range(WKV_PREP_GROUP):
            flat = it * WKV_PREP_GROUP + j
            b = flat // n_chunks
            c = flat % n_chunks
            sl = pl.ds(pl.multiple_of(c * C, C), C)
            ld = ld_ref[b, sl, :]
            hi, mid, lo = _split3(ld)
            cl = _dot(tri_c, hi) + _dot(tri_c, mid) + _dot(tri_c, lo)
            cl_end = cl[C - 1:C, :]
            k = k_ref[b, sl, :]
            bv = b_ref[b, sl, :]
            d_inv = jnp.exp(-cl)
            d_tail = jnp.exp(cl_end - cl)
            rt = (r_ref[b, sl, :] * jnp.exp(cl)).astype(BF16)
            rt_s[b, sl, :] = rt
            kt = k * d_inv
            bt = bv * d_inv
            at = a_ref[b, sl, :] * jnp.exp(cl - ld)
            kd = k * d_tail
            bdk = bv * d_tail
            d_end = jnp.exp(cl_end)
            for p, cs in enumerate(pair_cols):
                bkt_s[b, c, p] = jnp.concatenate([bdk[:, cs], kd[:, cs]], axis=0).T.astype(BF16)
                dcol_s[b, c, p] = jnp.broadcast_to(d_end[:, cs], (HEAD_PAIR, HEAD_PAIR)).T
                where.append((b, sl, cs))
                ats.append(at[:, cs])
                lhs.append(jnp.concatenate([rt[:, cs], at[:, cs].astype(BF16)], axis=0))
                rhs.append(jnp.concatenate([bd(kt[:, cs]), bd(bt[:, cs])], axis=0))
        n = len(where)
        amat = [_dot_nt(lhs[i], rhs[i]) for i in range(n)]
        l_ak, l_ab = [], []
        for i, (b, sl, cs) in enumerate(where):
            ark_s[b, sl, cs] = jnp.where(incl, amat[i][:C, :HEAD_PAIR], 0.0).astype(BF16)
            arb_s[b, sl, cs] = jnp.where(incl, amat[i][:C, HEAD_PAIR:], 0.0).astype(BF16)
            l_ak.append(jnp.where(strict, amat[i][C:, :HEAD_PAIR], 0.0))
            l_ab.append(jnp.where(strict, amat[i][C:, HEAD_PAIR:], 0.0))

        t_mat = [eye_pair + l for l in l_ab]
        q = [_dot(l.astype(BF16), bd(l)) for l in l_ab]
        n_sq = 1
        while 2 * n_sq < C // 2:
            both = [_dot(q[i].astype(BF16), jnp.concatenate([bd(q[i]), bd(t_mat[i])], axis=1))
                    for i in range(n)]
            q = [x[:, :HEAD_PAIR] for x in both]
            t_mat = [t_mat[i] + both[i][:, HEAD_PAIR:] for i in range(n)]
            n_sq *= 2
        t_mat = [t_mat[i] + _dot(q[i].astype(BF16), bd(t_mat[i])) for i in range(n)]
        tal = [_dot(t_mat[i].astype(BF16), jnp.concatenate([bd(ats[i]), bd(l_ak[i])], axis=1))
               for i in range(n)]
        for i, (b, sl, cs) in enumerate(where):
            ta_s[b, sl, cs] = tal[i][:, :HEAD_PAIR].astype(BF16)
            tl_s[b, sl, cs] = tal[i][:, HEAD_PAIR:].astype(BF16)
        return carry

    lax.fori_loop(0, bb * n_chunks // WKV_PREP_GROUP, prep, 0)

    def step(c, carry):
        sl = pl.ds(pl.multiple_of(c * C, C), C)
        chains = [(b, p) for b in range(bb) for p in range(n_pairs)]
        v = [v_ref[b, sl, :] for b in range(bb)]
        st = [st_ref[b, p] for b, p in chains]
        st_b = [x.astype(BF16) for x in st]
        bd_v = [bd(v[b][:, pair_cols[p]]) for b, p in chains]
        on_st = [_dot(jnp.concatenate([ta_s[b, sl, pair_cols[p]], rt_s[b, sl, pair_cols[p]]],
                                      axis=0), st_b[i]) for i, (b, p) in enumerate(chains)]
        on_v = [_dot(jnp.concatenate([tl_s[b, sl, pair_cols[p]], ark_s[b, sl, pair_cols[p]]],
                                     axis=0), bd_v[i]) for i, (b, p) in enumerate(chains)]
        u = [on_st[i][:C] + on_v[i][:C] for i in range(len(chains))]
        for i, (b, p) in enumerate(chains):
            uv = jnp.concatenate([u[i], v[b][:, pair_cols[p]]], axis=0).astype(BF16)
            st_ref[b, p] = (dcol_s[b, c, p] * st[i]
                            + jnp.where(bd_mask, _dot(bkt_s[b, c, p], uv), 0.0))
        ys = [on_st[i][C:] + on_v[i][C:] + _dot(arb_s[b, sl, pair_cols[p]], bd(u[i]))
              for i, (b, p) in enumerate(chains)]
        for b in range(bb):
            y = jnp.concatenate(ys[b * n_pairs:(b + 1) * n_pairs], axis=1)
            rkr = r_ref[b, sl, :] * k_ref[b, sl, :] * rk_ref[...]
            y_sum, rkr_sum = _pair_head_sums([y, rkr], ones_pair)
            yc = y - y_sum * (1.0 / A_HEAD)
            var = _pair_head_sums([yc * yc], ones_pair)[0] * (1.0 / A_HEAD)
            yn = yc * lax.rsqrt(var + LNX_EPS) * lng_ref[...] + lnb_ref[...]
            o_ref[b, sl, :] = (yn + rkr_sum * v[b]) * g_ref[b, sl, :]
        return carry

    lax.fori_loop(0, n_chunks, step, 0)


def _post_kernel(x_ref, ya_ref, ga_ref, ybg_ref, woa_ref, wout_ref, g2_ref, wr_hi_ref,
                 wr_lo_ref, br_ref, x2_out, h2_out, comb_out):
    y_a = _dot(ya_ref[...].astype(BF16), woa_ref[...])
    mix = ga_ref[...] * y_a + ybg_ref[...]
    x2 = x_ref[...] + _dot(mix.astype(BF16), wout_ref[...])
    x2_out[...] = x2
    h2 = _rms(x2, g2_ref[...])
    h2_out[...] = h2.astype(BF16)

    hi, lo = _split2(h2)
    wr_hi = wr_hi_ref[...]
    logits = _dot(hi, wr_hi) + _dot(lo, wr_hi) + _dot(hi, wr_lo_ref[...]) + br_ref[...]
    lane = lax.broadcasted_iota(jnp.int32, logits.shape, 1)
    neg = jnp.float32(-jnp.inf)
    big = jnp.int32(LANES)
    is_grp = (lane >= N_EXPERTS) & (lane < N_EXPERTS + N_GROUPS)
    gl = jnp.where(is_grp, logits, neg)
    gmax = jnp.max(gl, axis=-1, keepdims=True)
    g_p = 1.0 / jnp.sum(jnp.exp(gl - gmax), axis=-1, keepdims=True)
    g_idx = jnp.min(jnp.where(gl == gmax, lane, big), axis=-1, keepdims=True) - N_EXPERTS
    lo_lane = g_idx * EXPERTS_PER_GROUP
    in_grp = (lane >= lo_lane) & (lane < lo_lane + EXPERTS_PER_GROUP)
    el = jnp.where(in_grp, logits, neg)
    e1 = jnp.max(el, axis=-1, keepdims=True)
    i1 = jnp.min(jnp.where(el == e1, lane, big), axis=-1, keepdims=True)
    el2 = jnp.where(lane == i1, neg, el)
    e2 = jnp.max(el2, axis=-1, keepdims=True)
    i2 = jnp.min(jnp.where(el2 == e2, lane, big), axis=-1, keepdims=True)
    t = jnp.exp(e2 - e1)
    w1 = g_p / (1.0 + t)
    w2 = g_p * t / (1.0 + t)
    comb_out[...] = jnp.where(lane == i1, w1, 0.0) + jnp.where(lane == i2, w2, 0.0)


def _moe_kernel(x2_ref, h2_ref, comb_ref, wgu_ref, wd_ref, gf_ref, o_ref, acc_ref):
    e = pl.program_id(1)

    @pl.when(e == 0)
    def _():
        acc_ref[...] = jnp.zeros_like(acc_ref)

    gu = _dot(h2_ref[...], wgu_ref[0])
    gate = gu[:, :D_EXPERT]
    act = gate * jax.nn.sigmoid(gate) * gu[:, D_EXPERT:]
    sel = (lax.broadcasted_iota(jnp.int32, (LANES, D_EXPERT), 0) == e).astype(BF16)
    c_hi, c_lo = _split2(comb_ref[...])
    c_e = _dot(c_hi, sel) + _dot(c_lo, sel)
    acc_ref[...] += _dot((act * c_e).astype(BF16), wd_ref[0])

    @pl.when(e == pl.num_programs(1) - 1)
    def _():
        o_ref[...] = _rms(x2_ref[...] + acc_ref[...], gf_ref[...])


def _full(shape):
    return pl.BlockSpec(shape, lambda *_: (0,) * len(shape))


def _params(sem):
    return pltpu.CompilerParams(dimension_semantics=sem, vmem_limit_bytes=VMEM_LIMIT)


def _place(cols, parts):
    out = jnp.zeros((parts[0][1].shape[0], cols), parts[0][1].dtype)
    for off, arr in parts:
        out = lax.dynamic_update_slice(out, arr, (0, off))
    return out


def kernel(x, norm1_g, w_in, b_gate, tmix_mu, w0, w2, a0, a2, g2, k_k, k_a, r_k, lnx_g, lnx_b,
           w_oA, lnv_g, lnv_b, w_s, b_s, w_oB, w_out, norm2_g, w_rg, b_rg, w_re, b_re,
           w_e_gate, w_e_up, w_e_down, final_g):
    bsz, seq, d = x.shape
    n_tok = bsz * seq
    depth = norm1_g.shape[0]
    assert depth == 1, "the moe kernel fuses the final norm, so it must be the last layer"
    xf = x.reshape(n_tok, d)

    s_rkv = 3 * A_WIDTH
    s_w = s_rkv + D_DECAY_LORA
    s_a = s_w + D_AAA_LORA
    a_cols = s_a + D_GATE_LORA
    b_cols = 2 * B_WIDTH

    ones_bd = (jnp.arange(A_WIDTH)[:, None] // A_HEAD
               == jnp.arange(A_WIDTH)[None, :] // A_HEAD).astype(BF16)

    tm_a = 256
    tm_b = 256
    tm_p = 512
    tm_e = 1024

    for l in range(depth):
        wl = w_in[l]
        lora_w = _place(LORA_PAD, [(XW_OFF, wl[:, s_rkv:s_w]), (XA_OFF, wl[:, s_w:s_a]),
                                   (XG_OFF, wl[:, s_a:a_cols])])
        w_a = jnp.concatenate([wl[:, :s_rkv], lora_w], axis=1).astype(BF16)
        mu = tmix_mu[l][None, :]
        mu_a = jnp.concatenate(
            [mu[:, :s_rkv], _place(LORA_PAD, [(XW_OFF, mu[:, s_rkv:s_w]), (XA_OFF, mu[:, s_w:s_a]),
                                              (XG_OFF, mu[:, s_a:a_cols])])], axis=1)
        w2p = jnp.pad(w2[l], ((0, LANES - D_DECAY_LORA), (0, 0))).astype(BF16)
        a2p = jnp.pad(a2[l], ((0, LANES - D_AAA_LORA), (0, 0))).astype(BF16)
        g2p = jnp.pad(g2[l], ((0, LORA_PAD - XG_OFF - D_GATE_LORA), (0, 0))).astype(BF16)
        g1 = norm1_g[l][None, :]

        row512 = lambda i: (i, 0)
        tok_a = pl.BlockSpec((tm_a, A_WIDTH), row512)
        vec_a = _full((1, A_WIDTH))
        outs = pl.pallas_call(
            functools.partial(_rwkv_prep_kernel, seq // tm_a),
            grid=(n_tok // tm_a,),
            in_specs=[pl.BlockSpec((tm_a, d), row512),
                      pl.BlockSpec((8, d), lambda i: (jnp.maximum(i * (tm_a // 8) - 1, 0), 0)),
                      _full((1, d)), _full((d, A_PROJ)), _full((1, A_PROJ)), vec_a,
                      _full((LANES, A_WIDTH)), vec_a, _full((LANES, A_WIDTH)),
                      _full((LORA_PAD - XG_OFF, A_WIDTH)), vec_a, vec_a,
                      _full((A_WIDTH, A_WIDTH))],
            out_specs=[tok_a] * 7,
            out_shape=[jax.ShapeDtypeStruct((n_tok, A_WIDTH), F32)] * 7,
            compiler_params=_params(("parallel",)),
            name="rwkv_prep",
        )(xf, xf, g1, w_a, mu_a, w0[l][None, :], w2p, a0[l][None, :], a2p, g2p,
          k_k[l][None, :], k_a[l][None, :], ones_bd)
        r_, k_, v_, a_, b_, ld_, gg_ = outs

        w_b = wl[:, a_cols:a_cols + b_cols].astype(BF16)
        w_g = wl[:, a_cols + b_cols:].astype(BF16)
        bs_full = jnp.repeat(b_s[l].T, B_GROUP_CH, axis=1)
        tok_d = pl.BlockSpec((tm_b, d), row512)
        ybg, ga = pl.pallas_call(
            _gmlp_kernel,
            grid=(n_tok // tm_b,),
            in_specs=[tok_d, _full((1, d)), _full((d, b_cols)), _full((d, 2 * d)),
                      _full((1, 2 * d)), _full((1, B_WIDTH)), _full((1, B_WIDTH)),
                      _full((B_GROUPS, GMLP_BLOCK, GMLP_BLOCK)), _full((GMLP_BLOCK, B_WIDTH)),
                      _full((B_WIDTH, d))],
            out_specs=[tok_d, tok_d],
            out_shape=[jax.ShapeDtypeStruct((n_tok, d), F32)] * 2,
            compiler_params=_params(("parallel",)),
            name="gmlp",
        )(xf, g1, w_b, w_g, b_gate[l][None, :], lnv_g[l][None, :], lnv_b[l][None, :],
          w_s[l], bs_full, w_oB[l].astype(BF16))

        n_pairs = A_WIDTH // HEAD_PAIR
        n_chunks = WKV_TOKENS // WKV_CHUNK
        tok_w = pl.BlockSpec((WKV_SEQS, WKV_TOKENS, A_WIDTH), lambda bi, ti: (bi, ti, 0))
        vec_w = _full((1, A_WIDTH))
        seq_major = lambda t: t.reshape(bsz, seq, A_WIDTH)
        tok_scratch = pltpu.VMEM((WKV_SEQS, WKV_TOKENS, A_WIDTH), BF16)
        ya_in = pl.pallas_call(
            _wkv_kernel,
            grid=(bsz // WKV_SEQS, seq // WKV_TOKENS),
            in_specs=[tok_w] * 7 + [vec_w, vec_w, vec_w, _full((HEAD_PAIR, HEAD_PAIR))],
            out_specs=tok_w,
            out_shape=jax.ShapeDtypeStruct((bsz, seq, A_WIDTH), F32),
            scratch_shapes=[pltpu.VMEM((WKV_SEQS, n_pairs, HEAD_PAIR, HEAD_PAIR), F32)]
            + [tok_scratch] * 5
            + [pltpu.VMEM((WKV_SEQS, n_chunks, n_pairs, HEAD_PAIR, HEAD_PAIR), BF16),
               pltpu.VMEM((WKV_SEQS, n_chunks, n_pairs, HEAD_PAIR, HEAD_PAIR), F32)],
            compiler_params=_params(("parallel", "arbitrary")),
            name="wkv",
        )(*[seq_major(t) for t in (r_, k_, v_, a_, b_, ld_, gg_)], r_k[l].reshape(1, A_WIDTH),
          lnx_g[l][None, :], lnx_b[l][None, :],
          ones_bd[:HEAD_PAIR, :HEAD_PAIR]).reshape(n_tok, A_WIDTH)

        w_r = _place(LANES, [(0, jnp.transpose(w_re[l], (1, 0, 2)).reshape(d, N_EXPERTS)),
                             (N_EXPERTS, w_rg[l])])
        b_r = _place(LANES, [(0, b_re[l].reshape(1, N_EXPERTS)), (N_EXPERTS, b_rg[l][None, :])])
        wr_hi = w_r.astype(BF16)
        wr_lo = (w_r - wr_hi.astype(F32)).astype(BF16)
        tok_p = pl.BlockSpec((tm_p, d), row512)
        x2, h2, comb = pl.pallas_call(
            _post_kernel,
            grid=(n_tok // tm_p,),
            in_specs=[tok_p, pl.BlockSpec((tm_p, A_WIDTH), row512), tok_p, tok_p,
                      _full((A_WIDTH, d)), _full((d, d)), _full((1, d)), _full((d, LANES)),
                      _full((d, LANES)), _full((1, LANES))],
            out_specs=[tok_p, tok_p, pl.BlockSpec((tm_p, LANES), row512)],
            out_shape=[jax.ShapeDtypeStruct((n_tok, d), F32),
                       jax.ShapeDtypeStruct((n_tok, d), BF16),
                       jax.ShapeDtypeStruct((n_tok, LANES), F32)],
            compiler_params=_params(("parallel",)),
            name="post",
        )(xf, ya_in, ga, ybg, w_oA[l].astype(BF16), w_out[l].astype(BF16),
          norm2_g[l][None, :], wr_hi, wr_lo, b_r)

        w_gu = jnp.concatenate([w_e_gate[l], w_e_up[l]], axis=2).astype(BF16)
        gf = final_g[None, :]
        tok_e = pl.BlockSpec((tm_e, d), lambda i, e: (i, 0))
        xf = pl.pallas_call(
            _moe_kernel,
            grid=(n_tok // tm_e, N_EXPERTS),
            in_specs=[tok_e, tok_e, pl.BlockSpec((tm_e, LANES), lambda i, e: (i, 0)),
                      pl.BlockSpec((1, d, 2 * D_EXPERT), lambda i, e: (e, 0, 0)),
                      pl.BlockSpec((1, D_EXPERT, d), lambda i, e: (e, 0, 0)),
                      _full((1, d))],
            out_specs=tok_e,
            out_shape=jax.ShapeDtypeStruct((n_tok, d), F32),
            scratch_shapes=[pltpu.VMEM((tm_e, d), F32)],
            compiler_params=_params(("parallel", "arbitrary")),
            name="moe",
        )(x2, h2, comb, w_gu, w_e_down[l].astype(BF16), gf)

    return xf.reshape(bsz, seq, d)
```

```python
import functools

import jax
import jax.numpy as jnp
from jax import lax
from jax.experimental import pallas as pl
from jax.experimental.pallas import tpu as pltpu

F32 = jnp.float32
BF16 = jnp.bfloat16

D_MODEL = 1024
A_WIDTH = 512
A_HEAD = 64
D_DECAY_LORA = 64
D_AAA_LORA = 64
D_GATE_LORA = 160
B_WIDTH = 512
B_GROUPS = 4
B_GROUP_CH = 128
GMLP_BLOCK = 128
N_GROUPS = 4
EXPERTS_PER_GROUP = 8
N_EXPERTS = 32
D_EXPERT = 256
NORM_EPS = 1e-6
LN_EPS = 1e-5
LNX_EPS = 64e-5

LANES = 128
LORA_PAD = 512
XW_OFF, XA_OFF, XG_OFF = 0, 128, 256
A_PROJ = 3 * A_WIDTH + LORA_PAD
WKV_CHUNK = 64
HEAD_PAIR = 2 * A_HEAD
WKV_SEQS = 2
WKV_TOKENS = 256
WKV_PREP_GROUP = 4
MOE_TILE = 256
MOE_MAX_TILES = 2 * 16384 // MOE_TILE + N_EXPERTS
VMEM_LIMIT = 48 * 1024 * 1024


def _rms(x, g):
    return x * lax.rsqrt(jnp.mean(x * x, axis=-1, keepdims=True) + NORM_EPS) * g


def _dot(a, b):
    return jnp.dot(a, b, preferred_element_type=F32)


def _dot_nt(a, b):
    return lax.dot_general(a, b, (((1,), (1,)), ((), ())), preferred_element_type=F32)


def _split2(x):
    hi = x.astype(BF16)
    lo = (x - hi.astype(F32)).astype(BF16)
    return hi, lo


def _split3(x):
    hi = x.astype(BF16)
    r1 = x - hi.astype(F32)
    mid = r1.astype(BF16)
    lo = (r1 - mid.astype(F32)).astype(BF16)
    return hi, mid, lo


def _head_sum(x, ones_bd):
    hi, lo = _split2(x)
    return _dot(hi, ones_bd) + _dot(lo, ones_bd)


def _pair_head_sums(xs, ones_pair):
    n_tiles = A_WIDTH // HEAD_PAIR
    rows = xs[0].shape[0]
    tiles = [x[:, t * HEAD_PAIR:(t + 1) * HEAD_PAIR].astype(BF16) for x in xs for t in range(n_tiles)]
    sums = _dot(jnp.concatenate(tiles, axis=0), ones_pair)
    return [jnp.concatenate([sums[(i * n_tiles + t) * rows:(i * n_tiles + t + 1) * rows]
                             for t in range(n_tiles)], axis=1) for i in range(len(xs))]


def _rwkv_prep_kernel(tiles_per_seq, x_ref, xprev_ref, g1_ref, wa_ref, mu_ref, w0_ref,
                      w2_ref, a0_ref, a2_ref, g2_ref, kk_ref, ka_ref, ones_ref,
                      r_out, k_out, v_out, a_out, b_out, ld_out, g_out):
    i = pl.program_id(0)
    g1 = g1_ref[...]
    wa = wa_ref[...]
    h = _rms(x_ref[...], g1).astype(BF16)
    p = _dot(h, wa)
    hp = _rms(xprev_ref[...], g1).astype(BF16)
    prev = _dot(hp, wa)[7:8, :]
    prev = jnp.where(i % tiles_per_seq == 0, 0.0, prev)
    row = lax.broadcasted_iota(jnp.int32, p.shape, 0)
    shifted = jnp.where(row == 0, prev, pltpu.roll(p, 1, axis=0))
    pm = p + mu_ref[...] * (shifted - p)

    r = pm[:, 0:A_WIDTH]
    k = pm[:, A_WIDTH:2 * A_WIDTH]
    v = pm[:, 2 * A_WIDTH:3 * A_WIDTH]
    lora = pm[:, 3 * A_WIDTH:]
    xw = lora[:, XW_OFF:XW_OFF + LANES]
    xa = lora[:, XA_OFF:XA_OFF + LANES]
    xg = lora[:, XG_OFF:]

    z = -(w0_ref[...] + _dot(jnp.tanh(xw).astype(BF16), w2_ref[...]))
    softplus = jnp.maximum(z, 0.0) + jnp.log(1.0 + jnp.exp(-jnp.abs(z)))
    w = -softplus - 0.5
    a_lr = jax.nn.sigmoid(a0_ref[...] + _dot(xa.astype(BF16), a2_ref[...]))
    g = _dot(jax.nn.sigmoid(xg).astype(BF16), g2_ref[...])

    kk = k * kk_ref[...]
    norm = jnp.sqrt(_head_sum(kk * kk, ones_ref[...]))
    kk = kk / jnp.maximum(norm, 1e-12)

    r_out[...] = r
    k_out[...] = k * (1.0 + (a_lr - 1.0) * ka_ref[...])
    v_out[...] = v
    a_out[...] = -kk
    b_out[...] = kk * a_lr
    ld_out[...] = -jnp.exp(w)
    g_out[...] = g


def _gmlp_kernel(x_ref, g1_ref, wb_ref, wg_ref, bg_ref, lng_ref, lnb_ref, ws_ref, bs_ref,
                 wo_ref, ybg_out, ga_out):
    tm = x_ref.shape[0]
    h = _rms(x_ref[...], g1_ref[...]).astype(BF16)
    pb = _dot(h, wb_ref[...])
    z = 0.5 * pb * (1.0 + lax.erf(pb * (2.0 ** -0.5)))
    u = z[:, :B_WIDTH]
    v = z[:, B_WIDTH:]
    mean = jnp.mean(v, axis=-1, keepdims=True)
    vc = v - mean
    var = jnp.mean(vc * vc, axis=-1, keepdims=True)
    vn = (vc * lax.rsqrt(var + LN_EPS) * lng_ref[...] + lnb_ref[...]).astype(BF16)

    tri = (lax.broadcasted_iota(jnp.int32, (GMLP_BLOCK, GMLP_BLOCK), 0)
           >= lax.broadcasted_iota(jnp.int32, (GMLP_BLOCK, GMLP_BLOCK), 1))
    bs = bs_ref[...]
    rows = []
    for blk in range(tm // GMLP_BLOCK):
        cols = []
        for grp in range(B_GROUPS):
            ws = jnp.where(tri, ws_ref[grp], 0.0).astype(BF16)
            vb = vn[blk * GMLP_BLOCK:(blk + 1) * GMLP_BLOCK,
                    grp * B_GROUP_CH:(grp + 1) * B_GROUP_CH]
            cols.append(_dot(ws, vb))
        rows.append(jnp.concatenate(cols, axis=1) + bs)
    sv = jnp.concatenate(rows, axis=0)
    yb = _dot((u * sv).astype(BF16), wo_ref[...])
    gates = jax.nn.sigmoid(_dot(h, wg_ref[...]) + bg_ref[...])
    ga_out[...] = gates[:, :D_MODEL]
    ybg_out[...] = gates[:, D_MODEL:] * yb


def _wkv_kernel(r_ref, k_ref, v_ref, a_ref, b_ref, ld_ref, g_ref, rk_ref, lng_ref, lnb_ref,
                ones_ref, o_ref, st_ref, ta_s, tl_s, arb_s, ark_s, rt_s, bkt_s, dcol_s):
    C = WKV_CHUNK
    bb, tb, _ = r_ref.shape
    n_chunks = tb // C
    n_pairs = A_WIDTH // HEAD_PAIR

    @pl.when(pl.program_id(1) == 0)
    def _():
        st_ref[...] = jnp.zeros_like(st_ref)

    row = lax.broadcasted_iota(jnp.int32, (C, HEAD_PAIR), 0)
    src = lax.broadcasted_iota(jnp.int32, (C, HEAD_PAIR), 1) & (C - 1)
    incl = src <= row
    strict = src < row
    eye_pair = jnp.where(src == row, 1.0, 0.0)
    bd_mask = ((lax.broadcasted_iota(jnp.int32, (HEAD_PAIR, HEAD_PAIR), 0) >= A_HEAD)
               == (lax.broadcasted_iota(jnp.int32, (HEAD_PAIR, HEAD_PAIR), 1) >= A_HEAD))
    tri_c = (lax.broadcasted_iota(jnp.int32, (C, C), 0)
             >= lax.broadcasted_iota(jnp.int32, (C, C), 1)).astype(BF16)
    ones_pair = ones_ref[...]
    pair_cols = [slice(p * HEAD_PAIR, (p + 1) * HEAD_PAIR) for p in range(n_pairs)]

    def bd(x):
        xb = x.astype(BF16)
        return jnp.where(bd_mask, jnp.concatenate([xb, xb], axis=0), jnp.zeros((), BF16))

    def prep(it, carry):
        where, lhs, rhs, ats = [], [], [], []
        for j in range(WKV_PREP_GROUP):
            flat = it * WKV_PREP_GROUP + j
            b = flat // n_chunks
            c = flat % n_chunks
            sl = pl.ds(pl.multiple_of(c * C, C), C)
            ld = ld_ref[b, sl, :]
            hi, mid, lo = _split3(ld)
            cl = _dot(tri_c, hi) + _dot(tri_c, mid) + _dot(tri_c, lo)
            cl_end = cl[C - 1:C, :]
            k = k_ref[b, sl, :]
            bv = b_ref[b, sl, :]
            d_inv = jnp.exp(-cl)
            d_tail = jnp.exp(cl_end - cl)
            rt = (r_ref[b, sl, :] * jnp.exp(cl)).astype(BF16)
            rt_s[b, sl, :] = rt
            kt = k * d_inv
            bt = bv * d_inv
            at = a_ref[b, sl, :] * jnp.exp(cl - ld)
            kd = k * d_tail
            bdk = bv * d_tail
            d_end = jnp.exp(cl_end)
            for p, cs in enumerate(pair_cols):
                bkt_s[b, c, p] = jnp.concatenate([bdk[:, cs], kd[:, cs]], axis=0).T.astype(BF16)
                dcol_s[b, c, p] = jnp.broadcast_to(d_end[:, cs], (HEAD_PAIR, HEAD_PAIR)).T
                where.append((b, sl, cs))
                ats.append(at[:, cs])
                lhs.append(jnp.concatenate([rt[:, cs], at[:, cs].astype(BF16)], axis=0))
                rhs.append(jnp.concatenate([bd(kt[:, cs]), bd(bt[:, cs])], axis=0))
        n = len(where)
        amat = [_dot_nt(lhs[i], rhs[i]) for i in range(n)]
        l_ak, l_ab = [], []
        for i, (b, sl, cs) in enumerate(where):
            ark_s[b, sl, cs] = jnp.where(incl, amat[i][:C, :HEAD_PAIR], 0.0).astype(BF16)
            arb_s[b, sl, cs] = jnp.where(incl, amat[i][:C, HEAD_PAIR:], 0.0).astype(BF16)
            l_ak.append(jnp.where(strict, amat[i][C:, :HEAD_PAIR], 0.0))
            l_ab.append(jnp.where(strict, amat[i][C:, HEAD_PAIR:], 0.0))

        t_mat = [eye_pair + l for l in l_ab]
        q = [_dot(l.astype(BF16), bd(l)) for l in l_ab]
        n_sq = 1
        while 2 * n_sq < C // 2:
            both = [_dot(q[i].astype(BF16), jnp.concatenate([bd(q[i]), bd(t_mat[i])], axis=1))
                    for i in range(n)]
            q = [x[:, :HEAD_PAIR] for x in both]
            t_mat = [t_mat[i] + both[i][:, HEAD_PAIR:] for i in range(n)]
            n_sq *= 2
        t_mat = [t_mat[i] + _dot(q[i].astype(BF16), bd(t_mat[i])) for i in range(n)]
        tal = [_dot(t_mat[i].astype(BF16), jnp.concatenate([bd(ats[i]), bd(l_ak[i])], axis=1))
               for i in range(n)]
        for i, (b, sl, cs) in enumerate(where):
            ta_s[b, sl, cs] = tal[i][:, :HEAD_PAIR].astype(BF16)
            tl_s[b, sl, cs] = tal[i][:, HEAD_PAIR:].astype(BF16)
        return carry

    lax.fori_loop(0, bb * n_chunks // WKV_PREP_GROUP, prep, 0)

    def step(c, carry):
        sl = pl.ds(pl.multiple_of(c * C, C), C)
        chains = [(b, p) for b in range(bb) for p in range(n_pairs)]
        v = [v_ref[b, sl, :] for b in range(bb)]
        st = [st_ref[b, p] for b, p in chains]
        st_b = [x.astype(BF16) for x in st]
        bd_v = [bd(v[b][:, pair_cols[p]]) for b, p in chains]
        on_st = [_dot(jnp.concatenate([ta_s[b, sl, pair_cols[p]], rt_s[b, sl, pair_cols[p]]],
                                      axis=0), st_b[i]) for i, (b, p) in enumerate(chains)]
        on_v = [_dot(jnp.concatenate([tl_s[b, sl, pair_cols[p]], ark_s[b, sl, pair_cols[p]]],
                                     axis=0), bd_v[i]) for i, (b, p) in enumerate(chains)]
        u = [on_st[i][:C] + on_v[i][:C] for i in range(len(chains))]
        for i, (b, p) in enumerate(chains):
            uv = jnp.concatenate([u[i], v[b][:, pair_cols[p]]], axis=0).astype(BF16)
            st_ref[b, p] = (dcol_s[b, c, p] * st[i]
                            + jnp.where(bd_mask, _dot(bkt_s[b, c, p], uv), 0.0))
        ys = [on_st[i][C:] + on_v[i][C:] + _dot(arb_s[b, sl, pair_cols[p]], bd(u[i]))
              for i, (b, p) in enumerate(chains)]
        for b in range(bb):
            y = jnp.concatenate(ys[b * n_pairs:(b + 1) * n_pairs], axis=1)
            rkr = r_ref[b, sl, :] * k_ref[b, sl, :] * rk_ref[...]
            y_sum, rkr_sum = _pair_head_sums([y, rkr], ones_pair)
            yc = y - y_sum * (1.0 / A_HEAD)
            var = _pair_head_sums([yc * yc], ones_pair)[0] * (1.0 / A_HEAD)
            yn = yc * lax.rsqrt(var + LNX_EPS) * lng_ref[...] + lnb_ref[...]
            o_ref[b, sl, :] = (yn + rkr_sum * v[b]) * g_ref[b, sl, :]
        return carry

    lax.fori_loop(0, n_chunks, step, 0)


def _post_kernel(x_ref, ya_ref, ga_ref, ybg_ref, woa_ref, wout_ref, g2_ref, wr_hi_ref,
                 wr_lo_ref, br_ref, x2_out, h2_out, meta_out, wts_out, cnt_out, cnt_ref):
    @pl.when(pl.program_id(0) == 0)
    def _():
        cnt_ref[...] = jnp.zeros_like(cnt_ref)

    y_a = _dot(ya_ref[...].astype(BF16), woa_ref[...])
    mix = ga_ref[...] * y_a + ybg_ref[...]
    x2 = x_ref[...] + _dot(mix.astype(BF16), wout_ref[...])
    x2_out[...] = x2
    h2 = _rms(x2, g2_ref[...])
    h2_out[...] = h2

    hi, lo = _split2(h2)
    wr_hi = wr_hi_ref[...]
    logits = _dot(hi, wr_hi) + _dot(lo, wr_hi) + _dot(hi, wr_lo_ref[...]) + br_ref[...]
    lane = lax.broadcasted_iota(jnp.int32, logits.shape, 1)
    neg = jnp.float32(-jnp.inf)
    big = jnp.int32(LANES)
    is_grp = (lane >= N_EXPERTS) & (lane < N_EXPERTS + N_GROUPS)
    gl = jnp.where(is_grp, logits, neg)
    gmax = jnp.max(gl, axis=-1, keepdims=True)
    g_p = 1.0 / jnp.sum(jnp.exp(gl - gmax), axis=-1, keepdims=True)
    g_idx = jnp.min(jnp.where(gl == gmax, lane, big), axis=-1, keepdims=True) - N_EXPERTS
    lo_lane = g_idx * EXPERTS_PER_GROUP
    in_grp = (lane >= lo_lane) & (lane < lo_lane + EXPERTS_PER_GROUP)
    el = jnp.where(in_grp, logits, neg)
    e1 = jnp.max(el, axis=-1, keepdims=True)
    i1 = jnp.min(jnp.where(el == e1, lane, big), axis=-1, keepdims=True)
    el2 = jnp.where(lane == i1, neg, el)
    e2 = jnp.max(el2, axis=-1, keepdims=True)
    i2 = jnp.min(jnp.where(el2 == e2, lane, big), axis=-1, keepdims=True)
    t = jnp.exp(e2 - e1)
    w1 = g_p / (1.0 + t)
    w2 = g_p * t / (1.0 + t)
    wts_out[...] = jnp.where(lane == 0, w1, jnp.where(lane == 1, w2, 0.0))

    tm = logits.shape[0]
    onehot = jnp.where((lane == i1) | (lane == i2), 1.0, 0.0)
    tri = (lax.broadcasted_iota(jnp.int32, (tm, tm), 0)
           >= lax.broadcasted_iota(jnp.int32, (tm, tm), 1)).astype(BF16)
    incl = _dot(tri, onehot.astype(BF16))
    before = incl - onehot + cnt_ref[0:1, :]
    rank1 = jnp.sum(jnp.where(lane == i1, before, 0.0), axis=-1, keepdims=True).astype(jnp.int32)
    rank2 = jnp.sum(jnp.where(lane == i2, before, 0.0), axis=-1, keepdims=True).astype(jnp.int32)
    meta_out[...] = jnp.where(lane == 0, i1, jnp.where(lane == 1, i2, jnp.where(
        lane == 2, rank1, jnp.where(lane == 3, rank2, 0))))
    total = cnt_ref[...] + incl[tm - 1:tm, :]
    cnt_ref[...] = total
    cnt_out[...] = total


def _dispatch_kernel(pos1_ref, pos2_ref, zrow_ref, zvalid_ref, nu_ref, h_ref, hs_ref, zbuf, sem):
    i = pl.program_id(0)
    td = h_ref.shape[0]

    def zero_tile(row):
        start = pl.multiple_of(row, MOE_TILE)
        return pltpu.make_async_copy(zbuf, hs_ref.at[pl.ds(start, MOE_TILE)], sem.at[0])

    def for_unused_tiles(fn):
        def body(t, carry):
            fn(zero_tile(t * MOE_TILE))
            return carry
        lax.fori_loop(nu_ref[0], MOE_MAX_TILES, body, 0)

    @pl.when(i == 0)
    def _():
        zbuf[...] = jnp.zeros_like(zbuf)
        for e in range(N_EXPERTS):
            @pl.when(zvalid_ref[e] > 0)
            def _():
                zero_tile(zrow_ref[e]).start()
        for_unused_tiles(lambda cp: cp.start())
        for e in range(N_EXPERTS):
            @pl.when(zvalid_ref[e] > 0)
            def _():
                zero_tile(zrow_ref[e]).wait()
        for_unused_tiles(lambda cp: cp.wait())

    base = i * td

    def body(j, carry):
        src = h_ref.at[pl.ds(j, 1)]
        pltpu.make_async_copy(src, hs_ref.at[pl.ds(pos1_ref[base + j], 1)], sem.at[1]).start()
        pltpu.make_async_copy(src, hs_ref.at[pl.ds(pos2_ref[base + j], 1)], sem.at[1]).start()
        return carry

    lax.fori_loop(0, td, body, 0, unroll=8)
    for _ in range(2):
        pltpu.make_async_copy(h_ref, hs_ref.at[pl.ds(0, td)], sem.at[1]).wait()


def _expert_kernel(te_ref, nu_ref, hs_ref, wgu_ref, wd_ref, ys_ref):
    @pl.when(pl.program_id(0) < nu_ref[0])
    def _():
        gu = _dot(hs_ref[...].astype(BF16), wgu_ref[0])
        gate = gu[:, :D_EXPERT]
        act = gate * jax.nn.sigmoid(gate) * gu[:, D_EXPERT:]
        ys_ref[...] = _dot(act.astype(BF16), wd_ref[0])

    @pl.when(pl.program_id(0) >= nu_ref[0])
    def _():
        ys_ref[...] = jnp.zeros_like(ys_ref)


def _combine_kernel(pos1_ref, pos2_ref, x2_ref, wts_ref, gf_ref, ys_ref, o_ref, ybuf, sem):
    tc = x2_ref.shape[0]
    base = pl.program_id(0) * tc

    def body(j, carry):
        pltpu.make_async_copy(ys_ref.at[pl.ds(pos1_ref[base + j], 1)], ybuf.at[0, pl.ds(j, 1)],
                              sem.at[0]).start()
        pltpu.make_async_copy(ys_ref.at[pl.ds(pos2_ref[base + j], 1)], ybuf.at[1, pl.ds(j, 1)],
                              sem.at[1]).start()
        return carry

    lax.fori_loop(0, tc, body, 0, unroll=8)
    for slot in range(2):
        pltpu.make_async_copy(ys_ref.at[pl.ds(0, tc)], ybuf.at[slot], sem.at[slot]).wait()
    wts = wts_ref[...]
    y = wts[:, 0:1] * ybuf[0] + wts[:, 1:2] * ybuf[1]
    o_ref[...] = _rms(x2_ref[...] + y, gf_ref[...])


def _full(shape):
    return pl.BlockSpec(shape, lambda *_: (0,) * len(shape))


def _params(sem):
    return pltpu.CompilerParams(dimension_semantics=sem, vmem_limit_bytes=VMEM_LIMIT)


def _place(cols, parts):
    out = jnp.zeros((parts[0][1].shape[0], cols), parts[0][1].dtype)
    for off, arr in parts:
        out = lax.dynamic_update_slice(out, arr, (0, off))
    return out


def kernel(x, norm1_g, w_in, b_gate, tmix_mu, w0, w2, a0, a2, g2, k_k, k_a, r_k, lnx_g, lnx_b,
           w_oA, lnv_g, lnv_b, w_s, b_s, w_oB, w_out, norm2_g, w_rg, b_rg, w_re, b_re,
           w_e_gate, w_e_up, w_e_down, final_g):
    bsz, seq, d = x.shape
    n_tok = bsz * seq
    depth = norm1_g.shape[0]
    assert depth == 1, "the moe kernel fuses the final norm, so it must be the last layer"
    xf = x.reshape(n_tok, d)

    s_rkv = 3 * A_WIDTH
    s_w = s_rkv + D_DECAY_LORA
    s_a = s_w + D_AAA_LORA
    a_cols = s_a + D_GATE_LORA
    b_cols = 2 * B_WIDTH

    ones_bd = (jnp.arange(A_WIDTH)[:, None] // A_HEAD
               == jnp.arange(A_WIDTH)[None, :] // A_HEAD).astype(BF16)

    tm_a = 256
    tm_b = 256
    tm_p = 512

    for l in range(depth):
        wl = w_in[l]
        lora_w = _place(LORA_PAD, [(XW_OFF, wl[:, s_rkv:s_w]), (XA_OFF, wl[:, s_w:s_a]),
                                   (XG_OFF, wl[:, s_a:a_cols])])
        w_a = jnp.concatenate([wl[:, :s_rkv], lora_w], axis=1).astype(BF16)
        mu = tmix_mu[l][None, :]
        mu_a = jnp.concatenate(
            [mu[:, :s_rkv], _place(LORA_PAD, [(XW_OFF, mu[:, s_rkv:s_w]), (XA_OFF, mu[:, s_w:s_a]),
                                              (XG_OFF, mu[:, s_a:a_cols])])], axis=1)
        w2p = jnp.pad(w2[l], ((0, LANES - D_DECAY_LORA), (0, 0))).astype(BF16)
        a2p = jnp.pad(a2[l], ((0, LANES - D_AAA_LORA), (0, 0))).astype(BF16)
        g2p = jnp.pad(g2[l], ((0, LORA_PAD - XG_OFF - D_GATE_LORA), (0, 0))).astype(BF16)
        g1 = norm1_g[l][None, :]

        row512 = lambda i: (i, 0)
        tok_a = pl.BlockSpec((tm_a, A_WIDTH), row512)
        vec_a = _full((1, A_WIDTH))
        outs = pl.pallas_call(
            functools.partial(_rwkv_prep_kernel, seq // tm_a),
            grid=(n_tok // tm_a,),
            in_specs=[pl.BlockSpec((tm_a, d), row512),
                      pl.BlockSpec((8, d), lambda i: (jnp.maximum(i * (tm_a // 8) - 1, 0), 0)),
                      _full((1, d)), _full((d, A_PROJ)), _full((1, A_PROJ)), vec_a,
                      _full((LANES, A_WIDTH)), vec_a, _full((LANES, A_WIDTH)),
                      _full((LORA_PAD - XG_OFF, A_WIDTH)), vec_a, vec_a,
                      _full((A_WIDTH, A_WIDTH))],
            out_specs=[tok_a] * 7,
            out_shape=[jax.ShapeDtypeStruct((n_tok, A_WIDTH), F32)] * 7,
            compiler_params=_params(("parallel",)),
            name="rwkv_prep",
        )(xf, xf, g1, w_a, mu_a, w0[l][None, :], w2p, a0[l][None, :], a2p, g2p,
          k_k[l][None, :], k_a[l][None, :], ones_bd)
        r_, k_, v_, a_, b_, ld_, gg_ = outs

        w_b = wl[:, a_cols:a_cols + b_cols].astype(BF16)
        w_g = wl[:, a_cols + b_cols:].astype(BF16)
        bs_full = jnp.repeat(b_s[l].T, B_GROUP_CH, axis=1)
        tok_d = pl.BlockSpec((tm_b, d), row512)
        ybg, ga = pl.pallas_call(
            _gmlp_kernel,
            grid=(n_tok // tm_b,),
            in_specs=[tok_d, _full((1, d)), _full((d, b_cols)), _full((d, 2 * d)),
                      _full((1, 2 * d)), _full((1, B_WIDTH)), _full((1, B_WIDTH)),
                      _full((B_GROUPS, GMLP_BLOCK, GMLP_BLOCK)), _full((GMLP_BLOCK, B_WIDTH)),
                      _full((B_WIDTH, d))],
            out_specs=[tok_d, tok_d],
            out_shape=[jax.ShapeDtypeStruct((n_tok, d), F32)] * 2,
            compiler_params=_params(("parallel",)),
            name="gmlp",
        )(xf, g1, w_b, w_g, b_gate[l][None, :], lnv_g[l][None, :], lnv_b[l][None, :],
          w_s[l], bs_full, w_oB[l].astype(BF16))

        n_pairs = A_WIDTH // HEAD_PAIR
        n_chunks = WKV_TOKENS // WKV_CHUNK
        tok_w = pl.BlockSpec((WKV_SEQS, WKV_TOKENS, A_WIDTH), lambda bi, ti: (bi, ti, 0))
        vec_w = _full((1, A_WIDTH))
        seq_major = lambda t: t.reshape(bsz, seq, A_WIDTH)
        tok_scratch = pltpu.VMEM((WKV_SEQS, WKV_TOKENS, A_WIDTH), BF16)
        ya_in = pl.pallas_call(
            _wkv_kernel,
            grid=(bsz // WKV_SEQS, seq // WKV_TOKENS),
            in_specs=[tok_w] * 7 + [vec_w, vec_w, vec_w, _full((HEAD_PAIR, HEAD_PAIR))],
            out_specs=tok_w,
            out_shape=jax.ShapeDtypeStruct((bsz, seq, A_WIDTH), F32),
            scratch_shapes=[pltpu.VMEM((WKV_SEQS, n_pairs, HEAD_PAIR, HEAD_PAIR), F32)]
            + [tok_scratch] * 5
            + [pltpu.VMEM((WKV_SEQS, n_chunks, n_pairs, HEAD_PAIR, HEAD_PAIR), BF16),
               pltpu.VMEM((WKV_SEQS, n_chunks, n_pairs, HEAD_PAIR, HEAD_PAIR), F32)],
            compiler_params=_params(("parallel", "arbitrary")),
            name="wkv",
        )(*[seq_major(t) for t in (r_, k_, v_, a_, b_, ld_, gg_)], r_k[l].reshape(1, A_WIDTH),
          lnx_g[l][None, :], lnx_b[l][None, :],
          ones_bd[:HEAD_PAIR, :HEAD_PAIR]).reshape(n_tok, A_WIDTH)

        w_r = _place(LANES, [(0, jnp.transpose(w_re[l], (1, 0, 2)).reshape(d, N_EXPERTS)),
                             (N_EXPERTS, w_rg[l])])
        b_r = _place(LANES, [(0, b_re[l].reshape(1, N_EXPERTS)), (N_EXPERTS, b_rg[l][None, :])])
        wr_hi = w_r.astype(BF16)
        wr_lo = (w_r - wr_hi.astype(F32)).astype(BF16)
        tok_p = pl.BlockSpec((tm_p, d), row512)
        lane_p = pl.BlockSpec((tm_p, LANES), row512)
        x2, h2, meta, wts, cnt = pl.pallas_call(
            _post_kernel,
            grid=(n_tok // tm_p,),
            in_specs=[tok_p, pl.BlockSpec((tm_p, A_WIDTH), row512), tok_p, tok_p,
                      _full((A_WIDTH, d)), _full((d, d)), _full((1, d)), _full((d, LANES)),
                      _full((d, LANES)), _full((1, LANES))],
            out_specs=[tok_p, tok_p, lane_p, lane_p, _full((8, LANES))],
            out_shape=[jax.ShapeDtypeStruct((n_tok, d), F32),
                       jax.ShapeDtypeStruct((n_tok, d), F32),
                       jax.ShapeDtypeStruct((n_tok, LANES), jnp.int32),
                       jax.ShapeDtypeStruct((n_tok, LANES), F32),
                       jax.ShapeDtypeStruct((8, LANES), F32)],
            scratch_shapes=[pltpu.VMEM((8, LANES), F32)],
            compiler_params=_params(("arbitrary",)),
            name="post",
        )(xf, ya_in, ga, ybg, w_oA[l].astype(BF16), w_out[l].astype(BF16),
          norm2_g[l][None, :], wr_hi, wr_lo, b_r)

        assert 2 * n_tok // MOE_TILE + N_EXPERTS <= MOE_MAX_TILES
        counts = cnt[0, :N_EXPERTS].astype(jnp.int32)
        tiles_e = (counts + MOE_TILE - 1) // MOE_TILE
        tile_end = jnp.cumsum(tiles_e)
        tile_start = tile_end - tiles_e
        n_used = tile_end[-1:]
        row_off = tile_start * MOE_TILE
        pos1 = jnp.take(row_off, meta[:, 0]) + meta[:, 2]
        pos2 = jnp.take(row_off, meta[:, 1]) + meta[:, 3]
        tile_ids = jnp.minimum(jnp.arange(MOE_MAX_TILES, dtype=jnp.int32), n_used - 1)
        tile_expert = jnp.sum(tile_ids[:, None] >= tile_end[None, :], axis=1).astype(jnp.int32)
        zrow = (tile_end - 1) * MOE_TILE
        zvalid = (tiles_e > 0).astype(jnp.int32)
        n_rows = MOE_MAX_TILES * MOE_TILE

        td = 256
        h_sorted = pl.pallas_call(
            _dispatch_kernel,
            grid_spec=pltpu.PrefetchScalarGridSpec(
                num_scalar_prefetch=5, grid=(n_tok // td,),
                in_specs=[pl.BlockSpec((td, d), lambda i, *_: (i, 0))],
                out_specs=pl.BlockSpec(memory_space=pl.ANY),
                scratch_shapes=[pltpu.VMEM((MOE_TILE, d), F32), pltpu.SemaphoreType.DMA((2,))]),
            out_shape=jax.ShapeDtypeStruct((n_rows, d), F32),
            compiler_params=_params(("arbitrary",)),
            name="moe_dispatch",
        )(pos1, pos2, zrow, zvalid, n_used, h2)

        w_gu = jnp.concatenate([w_e_gate[l], w_e_up[l]], axis=2).astype(BF16)
        used_tile = lambda i, te, nu: (jnp.minimum(i, nu[0] - 1), 0)
        y_sorted = pl.pallas_call(
            _expert_kernel,
            grid_spec=pltpu.PrefetchScalarGridSpec(
                num_scalar_prefetch=2, grid=(MOE_MAX_TILES,),
                in_specs=[pl.BlockSpec((MOE_TILE, d), used_tile),
                          pl.BlockSpec((1, d, 2 * D_EXPERT), lambda i, te, nu: (te[i], 0, 0)),
                          pl.BlockSpec((1, D_EXPERT, d), lambda i, te, nu: (te[i], 0, 0))],
                out_specs=pl.BlockSpec((MOE_TILE, d), lambda i, te, nu: (i, 0))),
            out_shape=jax.ShapeDtypeStruct((n_rows, d), F32),
            compiler_params=_params(("arbitrary",)),
            name="moe_experts",
        )(tile_expert, n_used, h_sorted, w_gu, w_e_down[l].astype(BF16))

        tc = 256
        xf = pl.pallas_call(
            _combine_kernel,
            grid_spec=pltpu.PrefetchScalarGridSpec(
                num_scalar_prefetch=2, grid=(n_tok // tc,),
                in_specs=[pl.BlockSpec((tc, d), lambda i, *_: (i, 0)),
                          pl.BlockSpec((tc, LANES), lambda i, *_: (i, 0)),
                          pl.BlockSpec((1, d), lambda i, *_: (0, 0)),
                          pl.BlockSpec(memory_space=pl.ANY)],
                out_specs=pl.BlockSpec((tc, d), lambda i, *_: (i, 0)),
                scratch_shapes=[pltpu.VMEM((2, tc, d), F32), pltpu.SemaphoreType.DMA((2,))]),
            out_shape=jax.ShapeDtypeStruct((n_tok, d), F32),
            compiler_params=_params(("arbitrary",)),
            name="moe_combine",
        )(pos1, pos2, x2, wts, final_g[None, :], y_sorted)

    return xf.reshape(bsz, seq, d)
```

```python
import functools

import jax
import jax.numpy as jnp
from jax import lax
from jax.experimental import pallas as pl
from jax.experimental.pallas import tpu as pltpu

F32 = jnp.float32
BF16 = jnp.bfloat16

D_MODEL = 1024
A_WIDTH = 512
A_HEAD = 64
D_DECAY_LORA = 64
D_AAA_LORA = 64
D_GATE_LORA = 160
B_WIDTH = 512
B_GROUPS = 4
B_GROUP_CH = 128
GMLP_BLOCK = 128
N_GROUPS = 4
EXPERTS_PER_GROUP = 8
N_EXPERTS = 32
D_EXPERT = 256
NORM_EPS = 1e-6
LN_EPS = 1e-5
LNX_EPS = 64e-5

LANES = 128
LORA_PAD = 512
XW_OFF, XA_OFF, XG_OFF = 0, 128, 256
A_PROJ = 3 * A_WIDTH + LORA_PAD
WKV_CHUNK = 64
HEAD_PAIR = 2 * A_HEAD
WKV_SEQS = 2
WKV_TOKENS = 256
WKV_PREP_GROUP = 4
MOE_TILE = 256
MOE_MAX_TILES = 2 * 16384 // MOE_TILE + N_EXPERTS
VMEM_LIMIT = 48 * 1024 * 1024


def _rms(x, g):
    return x * lax.rsqrt(jnp.mean(x * x, axis=-1, keepdims=True) + NORM_EPS) * g


def _dot(a, b):
    return jnp.dot(a, b, preferred_element_type=F32)


def _dot_nt(a, b):
    return lax.dot_general(a, b, (((1,), (1,)), ((), ())), preferred_element_type=F32)


def _split2(x):
    hi = x.astype(BF16)
    lo = (x - hi.astype(F32)).astype(BF16)
    return hi, lo


def _split3(x):
    hi = x.astype(BF16)
    r1 = x - hi.astype(F32)
    mid = r1.astype(BF16)
    lo = (r1 - mid.astype(F32)).astype(BF16)
    return hi, mid, lo


TOKEN_TILE = D_MODEL // LANES


def _to_token_tiles(ref, x):
    rows = x.shape[0]
    for s in range(TOKEN_TILE):
        ref[pl.ds(s, rows, stride=TOKEN_TILE), :] = x[:, s * LANES:(s + 1) * LANES]


def _from_token_tiles(ref):
    rows = ref.shape[0] // TOKEN_TILE
    return jnp.concatenate([ref[pl.ds(s, rows, stride=TOKEN_TILE), :] for s in range(TOKEN_TILE)],
                           axis=1)


def _token_tile(ref, t):
    return ref.at[pl.ds(pl.multiple_of(t * TOKEN_TILE, TOKEN_TILE), TOKEN_TILE)]


def _head_sum(x, ones_bd):
    hi, lo = _split2(x)
    return _dot(hi, ones_bd) + _dot(lo, ones_bd)


def _pair_head_sums(xs, ones_pair):
    n_tiles = A_WIDTH // HEAD_PAIR
    rows = xs[0].shape[0]
    tiles = [x[:, t * HEAD_PAIR:(t + 1) * HEAD_PAIR].astype(BF16) for x in xs for t in range(n_tiles)]
    sums = _dot(jnp.concatenate(tiles, axis=0), ones_pair)
    return [jnp.concatenate([sums[(i * n_tiles + t) * rows:(i * n_tiles + t + 1) * rows]
                             for t in range(n_tiles)], axis=1) for i in range(len(xs))]


def _rwkv_prep_kernel(tiles_per_seq, x_ref, xprev_ref, g1_ref, wa_ref, mu_ref, w0_ref,
                      w2_ref, a0_ref, a2_ref, g2_ref, kk_ref, ka_ref, ones_ref,
                      r_out, k_out, v_out, a_out, b_out, ld_out, g_out):
    i = pl.program_id(0)
    g1 = g1_ref[...]
    wa = wa_ref[...]
    h = _rms(x_ref[...], g1).astype(BF16)
    p = _dot(h, wa)
    hp = _rms(xprev_ref[...], g1).astype(BF16)
    prev = _dot(hp, wa)[7:8, :]
    prev = jnp.where(i % tiles_per_seq == 0, 0.0, prev)
    row = lax.broadcasted_iota(jnp.int32, p.shape, 0)
    shifted = jnp.where(row == 0, prev, pltpu.roll(p, 1, axis=0))
    pm = p + mu_ref[...] * (shifted - p)

    r = pm[:, 0:A_WIDTH]
    k = pm[:, A_WIDTH:2 * A_WIDTH]
    v = pm[:, 2 * A_WIDTH:3 * A_WIDTH]
    lora = pm[:, 3 * A_WIDTH:]
    xw = lora[:, XW_OFF:XW_OFF + LANES]
    xa = lora[:, XA_OFF:XA_OFF + LANES]
    xg = lora[:, XG_OFF:]

    z = -(w0_ref[...] + _dot(jnp.tanh(xw).astype(BF16), w2_ref[...]))
    softplus = jnp.maximum(z, 0.0) + jnp.log(1.0 + jnp.exp(-jnp.abs(z)))
    w = -softplus - 0.5
    a_lr = jax.nn.sigmoid(a0_ref[...] + _dot(xa.astype(BF16), a2_ref[...]))
    g = _dot(jax.nn.sigmoid(xg).astype(BF16), g2_ref[...])

    kk = k * kk_ref[...]
    norm = jnp.sqrt(_head_sum(kk * kk, ones_ref[...]))
    kk = kk / jnp.maximum(norm, 1e-12)

    r_out[...] = r
    k_out[...] = k * (1.0 + (a_lr - 1.0) * ka_ref[...])
    v_out[...] = v
    a_out[...] = -kk
    b_out[...] = kk * a_lr
    ld_out[...] = -jnp.exp(w)
    g_out[...] = g


def _gmlp_kernel(x_ref, g1_ref, wb_ref, wg_ref, bg_ref, lng_ref, lnb_ref, ws_ref, bs_ref,
                 wo_ref, ybg_out, ga_out):
    tm = x_ref.shape[0]
    h = _rms(x_ref[...], g1_ref[...]).astype(BF16)
    pb = _dot(h, wb_ref[...])
    z = 0.5 * pb * (1.0 + lax.erf(pb * (2.0 ** -0.5)))
    u = z[:, :B_WIDTH]
    v = z[:, B_WIDTH:]
    mean = jnp.mean(v, axis=-1, keepdims=True)
    vc = v - mean
    var = jnp.mean(vc * vc, axis=-1, keepdims=True)
    vn = (vc * lax.rsqrt(var + LN_EPS) * lng_ref[...] + lnb_ref[...]).astype(BF16)

    tri = (lax.broadcasted_iota(jnp.int32, (GMLP_BLOCK, GMLP_BLOCK), 0)
           >= lax.broadcasted_iota(jnp.int32, (GMLP_BLOCK, GMLP_BLOCK), 1))
    bs = bs_ref[...]
    rows = []
    for blk in range(tm // GMLP_BLOCK):
        cols = []
        for grp in range(B_GROUPS):
            ws = jnp.where(tri, ws_ref[grp], 0.0).astype(BF16)
            vb = vn[blk * GMLP_BLOCK:(blk + 1) * GMLP_BLOCK,
                    grp * B_GROUP_CH:(grp + 1) * B_GROUP_CH]
            cols.append(_dot(ws, vb))
        rows.append(jnp.concatenate(cols, axis=1) + bs)
    sv = jnp.concatenate(rows, axis=0)
    yb = _dot((u * sv).astype(BF16), wo_ref[...])
    gates = jax.nn.sigmoid(_dot(h, wg_ref[...]) + bg_ref[...])
    ga_out[...] = gates[:, :D_MODEL]
    ybg_out[...] = gates[:, D_MODEL:] * yb


def _wkv_kernel(r_ref, k_ref, v_ref, a_ref, b_ref, ld_ref, g_ref, rk_ref, lng_ref, lnb_ref,
                ones_ref, o_ref, st_ref, ta_s, tl_s, arb_s, ark_s, rt_s, bkt_s, dcol_s):
    C = WKV_CHUNK
    bb, tb, _ = r_ref.shape
    n_chunks = tb // C
    n_pairs = A_WIDTH // HEAD_PAIR

    @pl.when(pl.program_id(1) == 0)
    def _():
        st_ref[...] = jnp.zeros_like(st_ref)

    row = lax.broadcasted_iota(jnp.int32, (C, HEAD_PAIR), 0)
    src = lax.broadcasted_iota(jnp.int32, (C, HEAD_PAIR), 1) & (C - 1)
    incl = src <= row
    strict = src < row
    eye_pair = jnp.where(src == row, 1.0, 0.0)
    bd_mask = ((lax.broadcasted_iota(jnp.int32, (HEAD_PAIR, HEAD_PAIR), 0) >= A_HEAD)
               == (lax.broadcasted_iota(jnp.int32, (HEAD_PAIR, HEAD_PAIR), 1) >= A_HEAD))
    tri_c = (lax.broadcasted_iota(jnp.int32, (C, C), 0)
             >= lax.broadcasted_iota(jnp.int32, (C, C), 1)).astype(BF16)
    ones_pair = ones_ref[...]
    pair_cols = [slice(p * HEAD_PAIR, (p + 1) * HEAD_PAIR) for p in range(n_pairs)]

    def bd(x):
        xb = x.astype(BF16)
        return jnp.where(bd_mask, jnp.concatenate([xb, xb], axis=0), jnp.zeros((), BF16))

    def prep(it, carry):
        where, lhs, rhs, ats = [], [], [], []
        for j in range(WKV_PREP_GROUP):
            flat = it * WKV_PREP_GROUP + j
            b = flat // n_chunks
            c = flat % n_chunks
            sl = pl.ds(pl.multiple_of(c * C, C), C)
            ld = ld_ref[b, sl, :]
            hi, mid, lo = _split3(ld)
            cl = _dot(tri_c, hi) + _dot(tri_c, mid) + _dot(tri_c, lo)
            cl_end = cl[C - 1:C, :]
            k = k_ref[b, sl, :]
            bv = b_ref[b, sl, :]
            d_inv = jnp.exp(-cl)
            d_tail = jnp.exp(cl_end - cl)
            rt = (r_ref[b, sl, :] * jnp.exp(cl)).astype(BF16)
            rt_s[b, sl, :] = rt
            kt = k * d_inv
            bt = bv * d_inv
            at = a_ref[b, sl, :] * jnp.exp(cl - ld)
            kd = k * d_tail
            bdk = bv * d_tail
            d_end = jnp.exp(cl_end)
            for p, cs in enumerate(pair_cols):
                bkt_s[b, c, p] = jnp.concatenate([bdk[:, cs], kd[:, cs]], axis=0).T.astype(BF16)
                dcol_s[b, c, p] = jnp.broadcast_to(d_end[:, cs], (HEAD_PAIR, HEAD_PAIR)).T
                where.append((b, sl, cs))
                ats.append(at[:, cs])
                lhs.append(jnp.concatenate([rt[:, cs], at[:, cs].astype(BF16)], axis=0))
                rhs.append(jnp.concatenate([bd(kt[:, cs]), bd(bt[:, cs])], axis=0))
        n = len(where)
        amat = [_dot_nt(lhs[i], rhs[i]) for i in range(n)]
        l_ak, l_ab = [], []
        for i, (b, sl, cs) in enumerate(where):
            ark_s[b, sl, cs] = jnp.where(incl, amat[i][:C, :HEAD_PAIR], 0.0).astype(BF16)
            arb_s[b, sl, cs] = jnp.where(incl, amat[i][:C, HEAD_PAIR:], 0.0).astype(BF16)
            l_ak.append(jnp.where(strict, amat[i][C:, :HEAD_PAIR], 0.0))
            l_ab.append(jnp.where(strict, amat[i][C:, HEAD_PAIR:], 0.0))

        t_mat = [eye_pair + l for l in l_ab]
        q = [_dot(l.astype(BF16), bd(l)) for l in l_ab]
        n_sq = 1
        while 2 * n_sq < C // 2:
            both = [_dot(q[i].astype(BF16), jnp.concatenate([bd(q[i]), bd(t_mat[i])], axis=1))
                    for i in range(n)]
            q = [x[:, :HEAD_PAIR] for x in both]
            t_mat = [t_mat[i] + both[i][:, HEAD_PAIR:] for i in range(n)]
            n_sq *= 2
        t_mat = [t_mat[i] + _dot(q[i].astype(BF16), bd(t_mat[i])) for i in range(n)]
        tal = [_dot(t_mat[i].astype(BF16), jnp.concatenate([bd(ats[i]), bd(l_ak[i])], axis=1))
               for i in range(n)]
        for i, (b, sl, cs) in enumerate(where):
            ta_s[b, sl, cs] = tal[i][:, :HEAD_PAIR].astype(BF16)
            tl_s[b, sl, cs] = tal[i][:, HEAD_PAIR:].astype(BF16)
        return carry

    lax.fori_loop(0, bb * n_chunks // WKV_PREP_GROUP, prep, 0)

    def step(c, carry):
        sl = pl.ds(pl.multiple_of(c * C, C), C)
        chains = [(b, p) for b in range(bb) for p in range(n_pairs)]
        v = [v_ref[b, sl, :] for b in range(bb)]
        st = [st_ref[b, p] for b, p in chains]
        st_b = [x.astype(BF16) for x in st]
        bd_v = [bd(v[b][:, pair_cols[p]]) for b, p in chains]
        on_st = [_dot(jnp.concatenate([ta_s[b, sl, pair_cols[p]], rt_s[b, sl, pair_cols[p]]],
                                      axis=0), st_b[i]) for i, (b, p) in enumerate(chains)]
        on_v = [_dot(jnp.concatenate([tl_s[b, sl, pair_cols[p]], ark_s[b, sl, pair_cols[p]]],
                                     axis=0), bd_v[i]) for i, (b, p) in enumerate(chains)]
        u = [on_st[i][:C] + on_v[i][:C] for i in range(len(chains))]
        for i, (b, p) in enumerate(chains):
            uv = jnp.concatenate([u[i], v[b][:, pair_cols[p]]], axis=0).astype(BF16)
            st_ref[b, p] = (dcol_s[b, c, p] * st[i]
                            + jnp.where(bd_mask, _dot(bkt_s[b, c, p], uv), 0.0))
        ys = [on_st[i][C:] + on_v[i][C:] + _dot(arb_s[b, sl, pair_cols[p]], bd(u[i]))
              for i, (b, p) in enumerate(chains)]
        for b in range(bb):
            y = jnp.concatenate(ys[b * n_pairs:(b + 1) * n_pairs], axis=1)
            rkr = r_ref[b, sl, :] * k_ref[b, sl, :] * rk_ref[...]
            y_sum, rkr_sum = _pair_head_sums([y, rkr], ones_pair)
            yc = y - y_sum * (1.0 / A_HEAD)
            var = _pair_head_sums([yc * yc], ones_pair)[0] * (1.0 / A_HEAD)
            yn = yc * lax.rsqrt(var + LNX_EPS) * lng_ref[...] + lnb_ref[...]
            o_ref[b, sl, :] = (yn + rkr_sum * v[b]) * g_ref[b, sl, :]
        return carry

    lax.fori_loop(0, n_chunks, step, 0)


def _post_kernel(x_ref, ya_ref, ga_ref, ybg_ref, woa_ref, wout_ref, g2_ref, wr_hi_ref,
                 wr_lo_ref, br_ref, x2_out, h2_out, meta_out, wts_out, cnt_out, cnt_ref):
    @pl.when(pl.program_id(0) == 0)
    def _():
        cnt_ref[...] = jnp.zeros_like(cnt_ref)

    y_a = _dot(ya_ref[...].astype(BF16), woa_ref[...])
    mix = ga_ref[...] * y_a + ybg_ref[...]
    x2 = x_ref[...] + _dot(mix.astype(BF16), wout_ref[...])
    x2_out[...] = x2
    h2 = _rms(x2, g2_ref[...])
    _to_token_tiles(h2_out, h2)

    hi, lo = _split2(h2)
    wr_hi = wr_hi_ref[...]
    logits = _dot(hi, wr_hi) + _dot(lo, wr_hi) + _dot(hi, wr_lo_ref[...]) + br_ref[...]
    lane = lax.broadcasted_iota(jnp.int32, logits.shape, 1)
    neg = jnp.float32(-jnp.inf)
    big = jnp.int32(LANES)
    is_grp = (lane >= N_EXPERTS) & (lane < N_EXPERTS + N_GROUPS)
    gl = jnp.where(is_grp, logits, neg)
    gmax = jnp.max(gl, axis=-1, keepdims=True)
    g_p = 1.0 / jnp.sum(jnp.exp(gl - gmax), axis=-1, keepdims=True)
    g_idx = jnp.min(jnp.where(gl == gmax, lane, big), axis=-1, keepdims=True) - N_EXPERTS
    lo_lane = g_idx * EXPERTS_PER_GROUP
    in_grp = (lane >= lo_lane) & (lane < lo_lane + EXPERTS_PER_GROUP)
    el = jnp.where(in_grp, logits, neg)
    e1 = jnp.max(el, axis=-1, keepdims=True)
    i1 = jnp.min(jnp.where(el == e1, lane, big), axis=-1, keepdims=True)
    el2 = jnp.where(lane == i1, neg, el)
    e2 = jnp.max(el2, axis=-1, keepdims=True)
    i2 = jnp.min(jnp.where(el2 == e2, lane, big), axis=-1, keepdims=True)
    t = jnp.exp(e2 - e1)
    w1 = g_p / (1.0 + t)
    w2 = g_p * t / (1.0 + t)
    wts_out[...] = jnp.where(lane == 0, w1, jnp.where(lane == 1, w2, 0.0))

    tm = logits.shape[0]
    onehot = jnp.where((lane == i1) | (lane == i2), 1.0, 0.0)
    tri = (lax.broadcasted_iota(jnp.int32, (tm, tm), 0)
           >= lax.broadcasted_iota(jnp.int32, (tm, tm), 1)).astype(BF16)
    incl = _dot(tri, onehot.astype(BF16))
    before = incl - onehot + cnt_ref[0:1, :]
    rank1 = jnp.sum(jnp.where(lane == i1, before, 0.0), axis=-1, keepdims=True).astype(jnp.int32)
    rank2 = jnp.sum(jnp.where(lane == i2, before, 0.0), axis=-1, keepdims=True).astype(jnp.int32)
    meta_out[...] = jnp.where(lane == 0, i1, jnp.where(lane == 1, i2, jnp.where(
        lane == 2, rank1, jnp.where(lane == 3, rank2, 0))))
    total = cnt_ref[...] + incl[tm - 1:tm, :]
    cnt_ref[...] = total
    cnt_out[...] = total


def _dispatch_kernel(pos1_ref, pos2_ref, zrow_ref, zvalid_ref, nu_ref, h_ref, hs_ref, zbuf, sem):
    i = pl.program_id(0)
    td = h_ref.shape[0] // TOKEN_TILE

    def zero_tile(row):
        start = pl.multiple_of(row * TOKEN_TILE, MOE_TILE * TOKEN_TILE)
        return pltpu.make_async_copy(zbuf, hs_ref.at[pl.ds(start, MOE_TILE * TOKEN_TILE)],
                                     sem.at[0])

    def for_unused_tiles(fn):
        def body(t, carry):
            fn(zero_tile(t * MOE_TILE))
            return carry
        lax.fori_loop(nu_ref[0], MOE_MAX_TILES, body, 0)

    @pl.when(i == 0)
    def _():
        zbuf[...] = jnp.zeros_like(zbuf)
        for e in range(N_EXPERTS):
            @pl.when(zvalid_ref[e] > 0)
            def _():
                zero_tile(zrow_ref[e]).start()
        for_unused_tiles(lambda cp: cp.start())
        for e in range(N_EXPERTS):
            @pl.when(zvalid_ref[e] > 0)
            def _():
                zero_tile(zrow_ref[e]).wait()
        for_unused_tiles(lambda cp: cp.wait())

    base = i * td

    def body(j, carry):
        src = _token_tile(h_ref, j)
        pltpu.make_async_copy(src, _token_tile(hs_ref, pos1_ref[base + j]), sem.at[1]).start()
        pltpu.make_async_copy(src, _token_tile(hs_ref, pos2_ref[base + j]), sem.at[1]).start()
        return carry

    lax.fori_loop(0, td, body, 0, unroll=8)
    for _ in range(2):
        pltpu.make_async_copy(h_ref, hs_ref.at[pl.ds(0, td * TOKEN_TILE)], sem.at[1]).wait()


def _expert_kernel(te_ref, nu_ref, hs_ref, wgu_ref, wd_ref, ys_ref):
    @pl.when(pl.program_id(0) < nu_ref[0])
    def _():
        gu = _dot(_from_token_tiles(hs_ref).astype(BF16), wgu_ref[0])
        gate = gu[:, :D_EXPERT]
        act = gate * jax.nn.sigmoid(gate) * gu[:, D_EXPERT:]
        _to_token_tiles(ys_ref, _dot(act.astype(BF16), wd_ref[0]))

    @pl.when(pl.program_id(0) >= nu_ref[0])
    def _():
        ys_ref[...] = jnp.zeros_like(ys_ref)


def _combine_kernel(pos1_ref, pos2_ref, x2_ref, wts_ref, gf_ref, ys_ref, o_ref, ybuf, sem):
    tc = x2_ref.shape[0]
    base = pl.program_id(0) * tc

    def body(j, carry):
        pltpu.make_async_copy(_token_tile(ys_ref, pos1_ref[base + j]),
                              _token_tile(ybuf.at[0], j), sem.at[0]).start()
        pltpu.make_async_copy(_token_tile(ys_ref, pos2_ref[base + j]),
                              _token_tile(ybuf.at[1], j), sem.at[1]).start()
        return carry

    lax.fori_loop(0, tc, body, 0, unroll=8)
    for slot in range(2):
        pltpu.make_async_copy(ys_ref.at[pl.ds(0, tc * TOKEN_TILE)], ybuf.at[slot],
                              sem.at[slot]).wait()
    wts = wts_ref[...]
    y = wts[:, 0:1] * _from_token_tiles(ybuf.at[0]) + wts[:, 1:2] * _from_token_tiles(ybuf.at[1])
    o_ref[...] = _rms(x2_ref[...] + y, gf_ref[...])


def _full(shape):
    return pl.BlockSpec(shape, lambda *_: (0,) * len(shape))


def _params(sem):
    return pltpu.CompilerParams(dimension_semantics=sem, vmem_limit_bytes=VMEM_LIMIT)


def _place(cols, parts):
    out = jnp.zeros((parts[0][1].shape[0], cols), parts[0][1].dtype)
    for off, arr in parts:
        out = lax.dynamic_update_slice(out, arr, (0, off))
    return out


def kernel(x, norm1_g, w_in, b_gate, tmix_mu, w0, w2, a0, a2, g2, k_k, k_a, r_k, lnx_g, lnx_b,
           w_oA, lnv_g, lnv_b, w_s, b_s, w_oB, w_out, norm2_g, w_rg, b_rg, w_re, b_re,
           w_e_gate, w_e_up, w_e_down, final_g):
    bsz, seq, d = x.shape
    n_tok = bsz * seq
    depth = norm1_g.shape[0]
    assert depth == 1, "the moe kernel fuses the final norm, so it must be the last layer"
    xf = x.reshape(n_tok, d)

    s_rkv = 3 * A_WIDTH
    s_w = s_rkv + D_DECAY_LORA
    s_a = s_w + D_AAA_LORA
    a_cols = s_a + D_GATE_LORA
    b_cols = 2 * B_WIDTH

    ones_bd = (jnp.arange(A_WIDTH)[:, None] // A_HEAD
               == jnp.arange(A_WIDTH)[None, :] // A_HEAD).astype(BF16)

    tm_a = 256
    tm_b = 256
    tm_p = 512

    for l in range(depth):
        wl = w_in[l]
        lora_w = _place(LORA_PAD, [(XW_OFF, wl[:, s_rkv:s_w]), (XA_OFF, wl[:, s_w:s_a]),
                                   (XG_OFF, wl[:, s_a:a_cols])])
        w_a = jnp.concatenate([wl[:, :s_rkv], lora_w], axis=1).astype(BF16)
        mu = tmix_mu[l][None, :]
        mu_a = jnp.concatenate(
            [mu[:, :s_rkv], _place(LORA_PAD, [(XW_OFF, mu[:, s_rkv:s_w]), (XA_OFF, mu[:, s_w:s_a]),
                                              (XG_OFF, mu[:, s_a:a_cols])])], axis=1)
        w2p = jnp.pad(w2[l], ((0, LANES - D_DECAY_LORA), (0, 0))).astype(BF16)
        a2p = jnp.pad(a2[l], ((0, LANES - D_AAA_LORA), (0, 0))).astype(BF16)
        g2p = jnp.pad(g2[l], ((0, LORA_PAD - XG_OFF - D_GATE_LORA), (0, 0))).astype(BF16)
        g1 = norm1_g[l][None, :]

        row512 = lambda i: (i, 0)
        tok_a = pl.BlockSpec((tm_a, A_WIDTH), row512)
        vec_a = _full((1, A_WIDTH))
        outs = pl.pallas_call(
            functools.partial(_rwkv_prep_kernel, seq // tm_a),
            grid=(n_tok // tm_a,),
            in_specs=[pl.BlockSpec((tm_a, d), row512),
                      pl.BlockSpec((8, d), lambda i: (jnp.maximum(i * (tm_a // 8) - 1, 0), 0)),
                      _full((1, d)), _full((d, A_PROJ)), _full((1, A_PROJ)), vec_a,
                      _full((LANES, A_WIDTH)), vec_a, _full((LANES, A_WIDTH)),
                      _full((LORA_PAD - XG_OFF, A_WIDTH)), vec_a, vec_a,
                      _full((A_WIDTH, A_WIDTH))],
            out_specs=[tok_a] * 7,
            out_shape=[jax.ShapeDtypeStruct((n_tok, A_WIDTH), F32)] * 7,
            compiler_params=_params(("parallel",)),
            name="rwkv_prep",
        )(xf, xf, g1, w_a, mu_a, w0[l][None, :], w2p, a0[l][None, :], a2p, g2p,
          k_k[l][None, :], k_a[l][None, :], ones_bd)
        r_, k_, v_, a_, b_, ld_, gg_ = outs

        w_b = wl[:, a_cols:a_cols + b_cols].astype(BF16)
        w_g = wl[:, a_cols + b_cols:].astype(BF16)
        bs_full = jnp.repeat(b_s[l].T, B_GROUP_CH, axis=1)
        tok_d = pl.BlockSpec((tm_b, d), row512)
        ybg, ga = pl.pallas_call(
            _gmlp_kernel,
            grid=(n_tok // tm_b,),
            in_specs=[tok_d, _full((1, d)), _full((d, b_cols)), _full((d, 2 * d)),
                      _full((1, 2 * d)), _full((1, B_WIDTH)), _full((1, B_WIDTH)),
                      _full((B_GROUPS, GMLP_BLOCK, GMLP_BLOCK)), _full((GMLP_BLOCK, B_WIDTH)),
                      _full((B_WIDTH, d))],
            out_specs=[tok_d, tok_d],
            out_shape=[jax.ShapeDtypeStruct((n_tok, d), F32)] * 2,
            compiler_params=_params(("parallel",)),
            name="gmlp",
        )(xf, g1, w_b, w_g, b_gate[l][None, :], lnv_g[l][None, :], lnv_b[l][None, :],
          w_s[l], bs_full, w_oB[l].astype(BF16))

        n_pairs = A_WIDTH // HEAD_PAIR
        n_chunks = WKV_TOKENS // WKV_CHUNK
        tok_w = pl.BlockSpec((WKV_SEQS, WKV_TOKENS, A_WIDTH), lambda bi, ti: (bi, ti, 0))
        vec_w = _full((1, A_WIDTH))
        seq_major = lambda t: t.reshape(bsz, seq, A_WIDTH)
        tok_scratch = pltpu.VMEM((WKV_SEQS, WKV_TOKENS, A_WIDTH), BF16)
        ya_in = pl.pallas_call(
            _wkv_kernel,
            grid=(bsz // WKV_SEQS, seq // WKV_TOKENS),
            in_specs=[tok_w] * 7 + [vec_w, vec_w, vec_w, _full((HEAD_PAIR, HEAD_PAIR))],
            out_specs=tok_w,
            out_shape=jax.ShapeDtypeStruct((bsz, seq, A_WIDTH), F32),
            scratch_shapes=[pltpu.VMEM((WKV_SEQS, n_pairs, HEAD_PAIR, HEAD_PAIR), F32)]
            + [tok_scratch] * 5
            + [pltpu.VMEM((WKV_SEQS, n_chunks, n_pairs, HEAD_PAIR, HEAD_PAIR), BF16),
               pltpu.VMEM((WKV_SEQS, n_chunks, n_pairs, HEAD_PAIR, HEAD_PAIR), F32)],
            compiler_params=_params(("parallel", "arbitrary")),
            name="wkv",
        )(*[seq_major(t) for t in (r_, k_, v_, a_, b_, ld_, gg_)], r_k[l].reshape(1, A_WIDTH),
          lnx_g[l][None, :], lnx_b[l][None, :],
          ones_bd[:HEAD_PAIR, :HEAD_PAIR]).reshape(n_tok, A_WIDTH)

        w_r = _place(LANES, [(0, jnp.transpose(w_re[l], (1, 0, 2)).reshape(d, N_EXPERTS)),
                             (N_EXPERTS, w_rg[l])])
        b_r = _place(LANES, [(0, b_re[l].reshape(1, N_EXPERTS)), (N_EXPERTS, b_rg[l][None, :])])
        wr_hi = w_r.astype(BF16)
        wr_lo = (w_r - wr_hi.astype(F32)).astype(BF16)
        tok_p = pl.BlockSpec((tm_p, d), row512)
        lane_p = pl.BlockSpec((tm_p, LANES), row512)
        x2, h2, meta, wts, cnt = pl.pallas_call(
            _post_kernel,
            grid=(n_tok // tm_p,),
            in_specs=[tok_p, pl.BlockSpec((tm_p, A_WIDTH), row512), tok_p, tok_p,
                      _full((A_WIDTH, d)), _full((d, d)), _full((1, d)), _full((d, LANES)),
                      _full((d, LANES)), _full((1, LANES))],
            out_specs=[tok_p, pl.BlockSpec((tm_p * TOKEN_TILE, LANES), row512),
                       lane_p, lane_p, _full((8, LANES))],
            out_shape=[jax.ShapeDtypeStruct((n_tok, d), F32),
                       jax.ShapeDtypeStruct((n_tok * TOKEN_TILE, LANES), F32),
                       jax.ShapeDtypeStruct((n_tok, LANES), jnp.int32),
                       jax.ShapeDtypeStruct((n_tok, LANES), F32),
                       jax.ShapeDtypeStruct((8, LANES), F32)],
            scratch_shapes=[pltpu.VMEM((8, LANES), F32)],
            compiler_params=_params(("arbitrary",)),
            name="post",
        )(xf, ya_in, ga, ybg, w_oA[l].astype(BF16), w_out[l].astype(BF16),
          norm2_g[l][None, :], wr_hi, wr_lo, b_r)

        assert 2 * n_tok // MOE_TILE + N_EXPERTS <= MOE_MAX_TILES
        counts = cnt[0, :N_EXPERTS].astype(jnp.int32)
        tiles_e = (counts + MOE_TILE - 1) // MOE_TILE
        tile_end = jnp.cumsum(tiles_e)
        tile_start = tile_end - tiles_e
        n_used = tile_end[-1:]
        row_off = tile_start * MOE_TILE
        pos1 = jnp.take(row_off, meta[:, 0]) + meta[:, 2]
        pos2 = jnp.take(row_off, meta[:, 1]) + meta[:, 3]
        tile_ids = jnp.minimum(jnp.arange(MOE_MAX_TILES, dtype=jnp.int32), n_used - 1)
        tile_expert = jnp.sum(tile_ids[:, None] >= tile_end[None, :], axis=1).astype(jnp.int32)
        zrow = (tile_end - 1) * MOE_TILE
        zvalid = (tiles_e > 0).astype(jnp.int32)
        n_rows = MOE_MAX_TILES * MOE_TILE

        td = 256
        sorted_tile = (MOE_TILE * TOKEN_TILE, LANES)
        h_sorted = pl.pallas_call(
            _dispatch_kernel,
            grid_spec=pltpu.PrefetchScalarGridSpec(
                num_scalar_prefetch=5, grid=(n_tok // td,),
                in_specs=[pl.BlockSpec((td * TOKEN_TILE, LANES), lambda i, *_: (i, 0))],
                out_specs=pl.BlockSpec(memory_space=pl.ANY),
                scratch_shapes=[pltpu.VMEM((MOE_TILE * TOKEN_TILE, LANES), F32),
                                pltpu.SemaphoreType.DMA((2,))]),
            out_shape=jax.ShapeDtypeStruct((n_rows * TOKEN_TILE, LANES), F32),
            compiler_params=_params(("arbitrary",)),
            name="moe_dispatch",
        )(pos1, pos2, zrow, zvalid, n_used, h2)

        w_gu = jnp.concatenate([w_e_gate[l], w_e_up[l]], axis=2).astype(BF16)
        used_tile = lambda i, te, nu: (jnp.minimum(i, nu[0] - 1), 0)
        y_sorted = pl.pallas_call(
            _expert_kernel,
            grid_spec=pltpu.PrefetchScalarGridSpec(
                num_scalar_prefetch=2, grid=(MOE_MAX_TILES,),
                in_specs=[pl.BlockSpec(sorted_tile, used_tile),
                          pl.BlockSpec((1, d, 2 * D_EXPERT), lambda i, te, nu: (te[i], 0, 0)),
                          pl.BlockSpec((1, D_EXPERT, d), lambda i, te, nu: (te[i], 0, 0))],
                out_specs=pl.BlockSpec(sorted_tile, lambda i, te, nu: (i, 0))),
            out_shape=jax.ShapeDtypeStruct((n_rows * TOKEN_TILE, LANES), F32),
            compiler_params=_params(("arbitrary",)),
            name="moe_experts",
        )(tile_expert, n_used, h_sorted, w_gu, w_e_down[l].astype(BF16))

        tc = 256
        xf = pl.pallas_call(
            _combine_kernel,
            grid_spec=pltpu.PrefetchScalarGridSpec(
                num_scalar_prefetch=2, grid=(n_tok // tc,),
                in_specs=[pl.BlockSpec((tc, d), lambda i, *_: (i, 0)),
                          pl.BlockSpec((tc, LANES), lambda i, *_: (i, 0)),
                          pl.BlockSpec((1, d), lambda i, *_: (0, 0)),
                          pl.BlockSpec(memory_space=pl.ANY)],
                out_specs=pl.BlockSpec((tc, d), lambda i, *_: (i, 0)),
                scratch_shapes=[pltpu.VMEM((2, tc * TOKEN_TILE, LANES), F32),
                                pltpu.SemaphoreType.DMA((2,))]),
            out_shape=jax.ShapeDtypeStruct((n_tok, d), F32),
            compiler_params=_params(("arbitrary",)),
            name="moe_combine",
        )(pos1, pos2, x2, wts, final_g[None, :], y_sorted)

    return xf.reshape(bsz, seq, d)
```

```python
import functools

import jax
import jax.numpy as jnp
from jax import lax
from jax.experimental import pallas as pl
from jax.experimental.pallas import tpu as pltpu

F32 = jnp.float32
BF16 = jnp.bfloat16

D_MODEL = 1024
A_WIDTH = 512
A_HEAD = 64
D_DECAY_LORA = 64
D_AAA_LORA = 64
D_GATE_LORA = 160
B_WIDTH = 512
B_GROUPS = 4
B_GROUP_CH = 128
GMLP_BLOCK = 128
N_GROUPS = 4
EXPERTS_PER_GROUP = 8
N_EXPERTS = 32
D_EXPERT = 256
NORM_EPS = 1e-6
LN_EPS = 1e-5
LNX_EPS = 64e-5

LANES = 128
LORA_PAD = 512
XW_OFF, XA_OFF, XG_OFF = 0, 128, 256
A_PROJ = 3 * A_WIDTH + LORA_PAD
WKV_CHUNK = 64
HEAD_PAIR = 2 * A_HEAD
WKV_SEQS = 2
WKV_TOKENS = 256
WKV_PREP_GROUP = 4
WKV_NORM_GROUP = 2
RANK_SPAN = 1 << 15
MOE_TILE = 256
MOE_MAX_TILES = 2 * 16384 // MOE_TILE + N_EXPERTS
VMEM_LIMIT = 48 * 1024 * 1024


def _rms(x, g):
    return x * lax.rsqrt(jnp.mean(x * x, axis=-1, keepdims=True) + NORM_EPS) * g


def _dot(a, b):
    return jnp.dot(a, b, preferred_element_type=F32)


def _dot_nt(a, b):
    return lax.dot_general(a, b, (((1,), (1,)), ((), ())), preferred_element_type=F32)


def _split2(x):
    hi = x.astype(BF16)
    lo = (x - hi.astype(F32)).astype(BF16)
    return hi, lo


def _split3(x):
    hi = x.astype(BF16)
    r1 = x - hi.astype(F32)
    mid = r1.astype(BF16)
    lo = (r1 - mid.astype(F32)).astype(BF16)
    return hi, mid, lo


def _head_sum(x, ones_bd):
    hi, lo = _split2(x)
    return _dot(hi, ones_bd) + _dot(lo, ones_bd)


def _pair_head_sums(xs, ones_pair):
    n_tiles = A_WIDTH // HEAD_PAIR
    rows = xs[0].shape[0]
    tiles = [x[:, t * HEAD_PAIR:(t + 1) * HEAD_PAIR].astype(BF16) for x in xs for t in range(n_tiles)]
    sums = _dot(jnp.concatenate(tiles, axis=0), ones_pair)
    return [jnp.concatenate([sums[(i * n_tiles + t) * rows:(i * n_tiles + t + 1) * rows]
                             for t in range(n_tiles)], axis=1) for i in range(len(xs))]


def _rwkv_prep_kernel(tiles_per_seq, x_ref, g1_ref, wa_ref, mu_ref, w0_ref,
                      w2_ref, a0_ref, a2_ref, g2_ref, kk_ref, ka_ref, ones_ref,
                      r_out, k_out, v_out, a_out, b_out, ld_out, g_out, tail_ref):
    i = pl.program_id(0)
    h = _rms(x_ref[...], g1_ref[...]).astype(BF16)
    p = _dot(h, wa_ref[...])
    tm = p.shape[0]
    prev = jnp.where(i % tiles_per_seq == 0, 0.0, tail_ref[7:8, :])
    tail_ref[...] = p[tm - 8:, :]
    row = lax.broadcasted_iota(jnp.int32, p.shape, 0)
    shifted = jnp.where(row == 0, prev, pltpu.roll(p, 1, axis=0))
    pm = p + mu_ref[...] * (shifted - p)

    r = pm[:, 0:A_WIDTH]
    k = pm[:, A_WIDTH:2 * A_WIDTH]
    v = pm[:, 2 * A_WIDTH:3 * A_WIDTH]
    lora = pm[:, 3 * A_WIDTH:]
    xw = lora[:, XW_OFF:XW_OFF + LANES]
    xa = lora[:, XA_OFF:XA_OFF + LANES]
    xg = lora[:, XG_OFF:]

    z = -(w0_ref[...] + _dot(jnp.tanh(xw).astype(BF16), w2_ref[...]))
    softplus = jnp.maximum(z, 0.0) + jnp.log(1.0 + jnp.exp(-jnp.abs(z)))
    w = -softplus - 0.5
    a_lr = jax.nn.sigmoid(a0_ref[...] + _dot(xa.astype(BF16), a2_ref[...]))
    g = _dot(jax.nn.sigmoid(xg).astype(BF16), g2_ref[...])

    kk = k * kk_ref[...]
    norm = jnp.sqrt(_head_sum(kk * kk, ones_ref[...]))
    kk = kk / jnp.maximum(norm, 1e-12)

    r_out[...] = r
    k_out[...] = k * (1.0 + (a_lr - 1.0) * ka_ref[...])
    v_out[...] = v
    a_out[...] = -kk
    b_out[...] = kk * a_lr
    ld_out[...] = -jnp.exp(w)
    g_out[...] = g


def _gmlp_kernel(x_ref, g1_ref, wb_ref, wg_ref, bg_ref, lng_ref, lnb_ref, ws_ref, bs_ref,
                 wo_ref, ybg_out, ga_out):
    tm = x_ref.shape[0]
    h = _rms(x_ref[...], g1_ref[...]).astype(BF16)
    pb = _dot(h, wb_ref[...])
    z = 0.5 * pb * (1.0 + lax.erf(pb * (2.0 ** -0.5)))
    u = z[:, :B_WIDTH]
    v = z[:, B_WIDTH:]
    mean = jnp.mean(v, axis=-1, keepdims=True)
    vc = v - mean
    var = jnp.mean(vc * vc, axis=-1, keepdims=True)
    vn = (vc * lax.rsqrt(var + LN_EPS) * lng_ref[...] + lnb_ref[...]).astype(BF16)

    tri = (lax.broadcasted_iota(jnp.int32, (GMLP_BLOCK, GMLP_BLOCK), 0)
           >= lax.broadcasted_iota(jnp.int32, (GMLP_BLOCK, GMLP_BLOCK), 1))
    bs = bs_ref[...]
    rows = []
    for blk in range(tm // GMLP_BLOCK):
        cols = []
        for grp in range(B_GROUPS):
            ws = jnp.where(tri, ws_ref[grp], 0.0).astype(BF16)
            vb = vn[blk * GMLP_BLOCK:(blk + 1) * GMLP_BLOCK,
                    grp * B_GROUP_CH:(grp + 1) * B_GROUP_CH]
            cols.append(_dot(ws, vb))
        rows.append(jnp.concatenate(cols, axis=1) + bs)
    sv = jnp.concatenate(rows, axis=0)
    yb = _dot((u * sv).astype(BF16), wo_ref[...])
    gates = jax.nn.sigmoid(_dot(h, wg_ref[...]) + bg_ref[...])
    ga_out[...] = gates[:, :D_MODEL]
    ybg_out[...] = gates[:, D_MODEL:] * yb


def _wkv_kernel(r_ref, k_ref, v_ref, a_ref, b_ref, ld_ref, g_ref, rk_ref, lng_ref, lnb_ref,
                ones_ref, o_ref, st_ref, ta_s, tl_s, arb_s, ark_s, rt_s, bkt_s, dcol_s):
    C = WKV_CHUNK
    bb, tb, _ = r_ref.shape
    n_chunks = tb // C
    n_pairs = A_WIDTH // HEAD_PAIR

    @pl.when(pl.program_id(1) == 0)
    def _():
        st_ref[...] = jnp.zeros_like(st_ref)

    row = lax.broadcasted_iota(jnp.int32, (C, HEAD_PAIR), 0)
    src = lax.broadcasted_iota(jnp.int32, (C, HEAD_PAIR), 1) & (C - 1)
    incl = src <= row
    strict = src < row
    eye_pair = jnp.where(src == row, 1.0, 0.0)
    bd_mask = ((lax.broadcasted_iota(jnp.int32, (HEAD_PAIR, HEAD_PAIR), 0) >= A_HEAD)
               == (lax.broadcasted_iota(jnp.int32, (HEAD_PAIR, HEAD_PAIR), 1) >= A_HEAD))
    tri_c = (lax.broadcasted_iota(jnp.int32, (C, C), 0)
             >= lax.broadcasted_iota(jnp.int32, (C, C), 1)).astype(BF16)
    ones_pair = ones_ref[...]
    pair_cols = [slice(p * HEAD_PAIR, (p + 1) * HEAD_PAIR) for p in range(n_pairs)]

    def bd(x):
        xb = x.astype(BF16)
        return jnp.where(bd_mask, jnp.concatenate([xb, xb], axis=0), jnp.zeros((), BF16))

    def prep(it, carry):
        where, lhs, rhs, ats = [], [], [], []
        for j in range(WKV_PREP_GROUP):
            flat = it * WKV_PREP_GROUP + j
            b = flat // n_chunks
            c = flat % n_chunks
            sl = pl.ds(pl.multiple_of(c * C, C), C)
            ld = ld_ref[b, sl, :]
            hi, mid, lo = _split3(ld)
            cl = _dot(tri_c, hi) + _dot(tri_c, mid) + _dot(tri_c, lo)
            cl_end = cl[C - 1:C, :]
            k = k_ref[b, sl, :]
            bv = b_ref[b, sl, :]
            d_inv = jnp.exp(-cl)
            d_tail = jnp.exp(cl_end - cl)
            rt = (r_ref[b, sl, :] * jnp.exp(cl)).astype(BF16)
            rt_s[b, sl, :] = rt
            kt = k * d_inv
            bt = bv * d_inv
            at = a_ref[b, sl, :] * jnp.exp(cl - ld)
            kd = k * d_tail
            bdk = bv * d_tail
            d_end = jnp.exp(cl_end)
            for p, cs in enumerate(pair_cols):
                bkt_s[b, c, p] = jnp.concatenate([bdk[:, cs], kd[:, cs]], axis=0).T.astype(BF16)
                dcol_s[b, c, p] = jnp.broadcast_to(d_end[:, cs], (HEAD_PAIR, HEAD_PAIR)).T
                where.append((b, sl, cs))
                ats.append(at[:, cs])
                lhs.append(jnp.concatenate([rt[:, cs], at[:, cs].astype(BF16)], axis=0))
                rhs.append(jnp.concatenate([bd(kt[:, cs]), bd(bt[:, cs])], axis=0))
        n = len(where)
        amat = [_dot_nt(lhs[i], rhs[i]) for i in range(n)]
        l_ak, l_ab = [], []
        for i, (b, sl, cs) in enumerate(where):
            ark_s[b, sl, cs] = jnp.where(incl, amat[i][:C, :HEAD_PAIR], 0.0).astype(BF16)
            arb_s[b, sl, cs] = jnp.where(incl, amat[i][:C, HEAD_PAIR:], 0.0).astype(BF16)
            l_ak.append(jnp.where(strict, amat[i][C:, :HEAD_PAIR], 0.0))
            l_ab.append(jnp.where(strict, amat[i][C:, HEAD_PAIR:], 0.0))

        t_mat = [eye_pair + l for l in l_ab]
        q = [_dot(l.astype(BF16), bd(l)) for l in l_ab]
        n_sq = 1
        while 2 * n_sq < C // 2:
            both = [_dot(q[i].astype(BF16), jnp.concatenate([bd(q[i]), bd(t_mat[i])], axis=1))
                    for i in range(n)]
            q = [x[:, :HEAD_PAIR] for x in both]
            t_mat = [t_mat[i] + both[i][:, HEAD_PAIR:] for i in range(n)]
            n_sq *= 2
        t_mat = [t_mat[i] + _dot(q[i].astype(BF16), bd(t_mat[i])) for i in range(n)]
        tal = [_dot(t_mat[i].astype(BF16), jnp.concatenate([bd(ats[i]), bd(l_ak[i])], axis=1))
               for i in range(n)]
        for i, (b, sl, cs) in enumerate(where):
            ta_s[b, sl, cs] = tal[i][:, :HEAD_PAIR].astype(BF16)
            tl_s[b, sl, cs] = tal[i][:, HEAD_PAIR:].astype(BF16)
        return carry

    lax.fori_loop(0, bb * n_chunks // WKV_PREP_GROUP, prep, 0)

    def step(c, carry):
        sl = pl.ds(pl.multiple_of(c * C, C), C)
        chains = [(b, p) for b in range(bb) for p in range(n_pairs)]
        v = [v_ref[b, sl, :] for b in range(bb)]
        st = [st_ref[b, p] for b, p in chains]
        st_b = [x.astype(BF16) for x in st]
        bd_v = [bd(v[b][:, pair_cols[p]]) for b, p in chains]
        on_st = [_dot(jnp.concatenate([ta_s[b, sl, pair_cols[p]], rt_s[b, sl, pair_cols[p]]],
                                      axis=0), st_b[i]) for i, (b, p) in enumerate(chains)]
        on_v = [_dot(jnp.concatenate([tl_s[b, sl, pair_cols[p]], ark_s[b, sl, pair_cols[p]]],
                                     axis=0), bd_v[i]) for i, (b, p) in enumerate(chains)]
        u = [on_st[i][:C] + on_v[i][:C] for i in range(len(chains))]
        for i, (b, p) in enumerate(chains):
            uv = jnp.concatenate([u[i], v[b][:, pair_cols[p]]], axis=0).astype(BF16)
            st_ref[b, p] = (dcol_s[b, c, p] * st[i]
                            + jnp.where(bd_mask, _dot(bkt_s[b, c, p], uv), 0.0))
        ys = [on_st[i][C:] + on_v[i][C:] + _dot(arb_s[b, sl, pair_cols[p]], bd(u[i]))
              for i, (b, p) in enumerate(chains)]
        for b in range(bb):
            o_ref[b, sl, :] = jnp.concatenate(ys[b * n_pairs:(b + 1) * n_pairs], axis=1)
        return carry

    lax.fori_loop(0, n_chunks, step, 0)

    def finish(it, carry):
        items = [(b, pl.ds(pl.multiple_of((it * WKV_NORM_GROUP + j) * C, C), C))
                 for j in range(WKV_NORM_GROUP) for b in range(bb)]
        y = [o_ref[b, sl, :] for b, sl in items]
        rkr = [r_ref[b, sl, :] * k_ref[b, sl, :] * rk_ref[...] for b, sl in items]
        sums = [_pair_head_sums([y[i], rkr[i]], ones_pair) for i in range(len(items))]
        yc = [y[i] - sums[i][0] * (1.0 / A_HEAD) for i in range(len(items))]
        var = [_pair_head_sums([x * x], ones_pair)[0] * (1.0 / A_HEAD) for x in yc]
        for i, (b, sl) in enumerate(items):
            yn = yc[i] * lax.rsqrt(var[i] + LNX_EPS) * lng_ref[...] + lnb_ref[...]
            o_ref[b, sl, :] = (yn + sums[i][1] * v_ref[b, sl, :]) * g_ref[b, sl, :]
        return carry

    lax.fori_loop(0, n_chunks // WKV_NORM_GROUP, finish, 0)


def _post_kernel(x_ref, ya_ref, ga_ref, ybg_ref, woa_ref, wout_ref, g2_ref, wr_hi_ref,
                 wr_lo_ref, br_ref, x2_out, h2_out, meta_out, wts_out, cnt_out, cnt_ref):
    @pl.when(pl.program_id(0) == 0)
    def _():
        cnt_ref[...] = jnp.zeros_like(cnt_ref)

    y_a = _dot(ya_ref[...].astype(BF16), woa_ref[...])
    mix = ga_ref[...] * y_a + ybg_ref[...]
    x2 = x_ref[...] + _dot(mix.astype(BF16), wout_ref[...])
    x2_out[...] = x2
    h2 = _rms(x2, g2_ref[...])
    h2_out[...] = h2

    hi, lo = _split2(h2)
    wr_hi = wr_hi_ref[...]
    logits = _dot(hi, wr_hi) + _dot(lo, wr_hi) + _dot(hi, wr_lo_ref[...]) + br_ref[...]
    lane = lax.broadcasted_iota(jnp.int32, logits.shape, 1)
    neg = jnp.float32(-jnp.inf)
    big = jnp.int32(LANES)
    is_grp = (lane >= N_EXPERTS) & (lane < N_EXPERTS + N_GROUPS)
    gl = jnp.where(is_grp, logits, neg)
    gmax = jnp.max(gl, axis=-1, keepdims=True)
    g_p = 1.0 / jnp.sum(jnp.exp(gl - gmax), axis=-1, keepdims=True)
    g_idx = jnp.min(jnp.where(gl == gmax, lane, big), axis=-1, keepdims=True) - N_EXPERTS
    lo_lane = g_idx * EXPERTS_PER_GROUP
    in_grp = (lane >= lo_lane) & (lane < lo_lane + EXPERTS_PER_GROUP)
    el = jnp.where(in_grp, logits, neg)
    e1 = jnp.max(el, axis=-1, keepdims=True)
    i1 = jnp.min(jnp.where(el == e1, lane, big), axis=-1, keepdims=True)
    el2 = jnp.where(lane == i1, neg, el)
    e2 = jnp.max(el2, axis=-1, keepdims=True)
    i2 = jnp.min(jnp.where(el2 == e2, lane, big), axis=-1, keepdims=True)
    t = jnp.exp(e2 - e1)
    w1 = g_p / (1.0 + t)
    w2 = g_p * t / (1.0 + t)
    wts_out[...] = jnp.where(lane == 0, w1, jnp.where(lane == 1, w2, 0.0))

    tm = logits.shape[0]
    onehot = jnp.where((lane == i1) | (lane == i2), 1.0, 0.0)
    tri = (lax.broadcasted_iota(jnp.int32, (tm, tm), 0)
           >= lax.broadcasted_iota(jnp.int32, (tm, tm), 1)).astype(BF16)
    incl = _dot(tri, onehot.astype(BF16))
    before = incl - onehot + cnt_ref[0:1, :]
    rank1 = jnp.sum(jnp.where(lane == i1, before, 0.0), axis=-1, keepdims=True).astype(jnp.int32)
    rank2 = jnp.sum(jnp.where(lane == i2, before, 0.0), axis=-1, keepdims=True).astype(jnp.int32)
    codes = jnp.where(lane == 0, i1 * RANK_SPAN + rank1,
                      jnp.where(lane == 1, i2 * RANK_SPAN + rank2, 0))
    meta_out[...] = codes.T[0:8, :]
    total = cnt_ref[...] + incl[tm - 1:tm, :]
    cnt_ref[...] = total
    cnt_out[...] = total


def _slot_row(code, row_off_ref):
    return row_off_ref[code >> 15] + (code & (RANK_SPAN - 1))


def _dispatch_kernel(code1_ref, code2_ref, row_off_ref, zrow_ref, zvalid_ref, nu_ref, h_ref, hs_ref,
                     zbuf, sem):
    i = pl.program_id(0)
    td = h_ref.shape[0]

    def zero_tile(row):
        start = pl.multiple_of(row, MOE_TILE)
        return pltpu.make_async_copy(zbuf, hs_ref.at[pl.ds(start, MOE_TILE)], sem.at[0])

    def for_unused_tiles(fn):
        def body(t, carry):
            fn(zero_tile(t * MOE_TILE))
            return carry
        lax.fori_loop(nu_ref[0], MOE_MAX_TILES, body, 0)

    @pl.when(i == 0)
    def _():
        zbuf[...] = jnp.zeros_like(zbuf)
        for e in range(N_EXPERTS):
            @pl.when(zvalid_ref[e] > 0)
            def _():
                zero_tile(zrow_ref[e]).start()
        for_unused_tiles(lambda cp: cp.start())
        for e in range(N_EXPERTS):
            @pl.when(zvalid_ref[e] > 0)
            def _():
                zero_tile(zrow_ref[e]).wait()
        for_unused_tiles(lambda cp: cp.wait())

    base = i * td

    def body(j, carry):
        src = h_ref.at[pl.ds(j, 1)]
        for code_ref in (code1_ref, code2_ref):
            row = _slot_row(code_ref[base + j], row_off_ref)
            pltpu.make_async_copy(src, hs_ref.at[pl.ds(row, 1)], sem.at[1]).start()
        return carry

    lax.fori_loop(0, td, body, 0, unroll=8)
    for _ in range(2):
        pltpu.make_async_copy(h_ref, hs_ref.at[pl.ds(0, td)], sem.at[1]).wait()


def _expert_kernel(te_ref, nu_ref, hs_ref, wg_ref, wu_ref, wd_ref, ys_ref, wgu_s, wd_s):
    i = pl.program_id(0)

    @pl.when((i == 0) | (te_ref[i] != te_ref[jnp.maximum(i - 1, 0)]))
    def _():
        wgu_s[:, :D_EXPERT] = wg_ref[0].astype(BF16)
        wgu_s[:, D_EXPERT:] = wu_ref[0].astype(BF16)
        wd_s[...] = wd_ref[0].astype(BF16)

    @pl.when(i < nu_ref[0])
    def _():
        gu = _dot(hs_ref[...].astype(BF16), wgu_s[...])
        gate = gu[:, :D_EXPERT]
        act = gate * jax.nn.sigmoid(gate) * gu[:, D_EXPERT:]
        ys_ref[...] = _dot(act.astype(BF16), wd_s[...])

    @pl.when(pl.program_id(0) >= nu_ref[0])
    def _():
        ys_ref[...] = jnp.zeros_like(ys_ref)


def _combine_kernel(code1_ref, code2_ref, row_off_ref, x2_ref, wts_ref, gf_ref, ys_ref, o_ref,
                    ybuf, sem):
    tc = x2_ref.shape[0]
    base = pl.program_id(0) * tc

    def body(j, carry):
        for slot, code_ref in enumerate((code1_ref, code2_ref)):
            row = _slot_row(code_ref[base + j], row_off_ref)
            pltpu.make_async_copy(ys_ref.at[pl.ds(row, 1)], ybuf.at[slot, pl.ds(j, 1)],
                                  sem.at[slot]).start()
        return carry

    lax.fori_loop(0, tc, body, 0, unroll=8)
    for slot in range(2):
        pltpu.make_async_copy(ys_ref.at[pl.ds(0, tc)], ybuf.at[slot], sem.at[slot]).wait()
    wts = wts_ref[...]
    y = wts[:, 0:1] * ybuf[0] + wts[:, 1:2] * ybuf[1]
    o_ref[...] = _rms(x2_ref[...] + y, gf_ref[...])


def _full(shape):
    return pl.BlockSpec(shape, lambda *_: (0,) * len(shape))


def _params(sem):
    return pltpu.CompilerParams(dimension_semantics=sem, vmem_limit_bytes=VMEM_LIMIT)


def _mixer_a_layout(t):
    s_rkv = 3 * A_WIDTH
    s_w = s_rkv + D_DECAY_LORA
    s_a = s_w + D_AAA_LORA
    gap = lambda n: jnp.zeros((t.shape[0], n), t.dtype)
    return jnp.concatenate(
        [t[:, :s_w], gap(XA_OFF - D_DECAY_LORA), t[:, s_w:s_a], gap(XG_OFF - XA_OFF - D_AAA_LORA),
         t[:, s_a:], gap(LORA_PAD - XG_OFF - D_GATE_LORA)], axis=1)


def _place(cols, parts):
    out = jnp.zeros((parts[0][1].shape[0], cols), parts[0][1].dtype)
    for off, arr in parts:
        out = lax.dynamic_update_slice(out, arr, (0, off))
    return out


def kernel(x, norm1_g, w_in, b_gate, tmix_mu, w0, w2, a0, a2, g2, k_k, k_a, r_k, lnx_g, lnx_b,
           w_oA, lnv_g, lnv_b, w_s, b_s, w_oB, w_out, norm2_g, w_rg, b_rg, w_re, b_re,
           w_e_gate, w_e_up, w_e_down, final_g):
    bsz, seq, d = x.shape
    n_tok = bsz * seq
    depth = norm1_g.shape[0]
    assert depth == 1, "the moe kernel fuses the final norm, so it must be the last layer"
    xf = x.reshape(n_tok, d)

    s_rkv = 3 * A_WIDTH
    s_w = s_rkv + D_DECAY_LORA
    s_a = s_w + D_AAA_LORA
    a_cols = s_a + D_GATE_LORA
    b_cols = 2 * B_WIDTH

    ones_bd = (jnp.arange(A_WIDTH)[:, None] // A_HEAD
               == jnp.arange(A_WIDTH)[None, :] // A_HEAD).astype(BF16)

    tm_a = 512
    tm_b = 512
    tm_p = 512

    for l in range(depth):
        wl = w_in[l]
        w_a = _mixer_a_layout(wl[:, :a_cols]).astype(BF16)
        mu_a = _mixer_a_layout(tmix_mu[l][None, :])
        w2p = jnp.pad(w2[l], ((0, LANES - D_DECAY_LORA), (0, 0))).astype(BF16)
        a2p = jnp.pad(a2[l], ((0, LANES - D_AAA_LORA), (0, 0))).astype(BF16)
        g2p = jnp.pad(g2[l], ((0, LORA_PAD - XG_OFF - D_GATE_LORA), (0, 0))).astype(BF16)
        g1 = norm1_g[l][None, :]

        row512 = lambda i: (i, 0)
        tok_a = pl.BlockSpec((tm_a, A_WIDTH), row512)
        vec_a = _full((1, A_WIDTH))
        outs = pl.pallas_call(
            functools.partial(_rwkv_prep_kernel, seq // tm_a),
            grid=(n_tok // tm_a,),
            in_specs=[pl.BlockSpec((tm_a, d), row512),
                      _full((1, d)), _full((d, A_PROJ)), _full((1, A_PROJ)), vec_a,
                      _full((LANES, A_WIDTH)), vec_a, _full((LANES, A_WIDTH)),
                      _full((LORA_PAD - XG_OFF, A_WIDTH)), vec_a, vec_a,
                      _full((A_WIDTH, A_WIDTH))],
            out_specs=[tok_a] * 7,
            out_shape=[jax.ShapeDtypeStruct((n_tok, A_WIDTH), F32)] * 7,
            scratch_shapes=[pltpu.VMEM((8, A_PROJ), F32)],
            compiler_params=_params(("arbitrary",)),
            name="rwkv_prep",
        )(xf, g1, w_a, mu_a, w0[l][None, :], w2p, a0[l][None, :], a2p, g2p,
          k_k[l][None, :], k_a[l][None, :], ones_bd)
        r_, k_, v_, a_, b_, ld_, gg_ = outs

        w_b = wl[:, a_cols:a_cols + b_cols].astype(BF16)
        w_g = wl[:, a_cols + b_cols:].astype(BF16)
        bs_full = jnp.repeat(b_s[l].T, B_GROUP_CH, axis=1)
        tok_d = pl.BlockSpec((tm_b, d), row512)
        ybg, ga = pl.pallas_call(
            _gmlp_kernel,
            grid=(n_tok // tm_b,),
            in_specs=[tok_d, _full((1, d)), _full((d, b_cols)), _full((d, 2 * d)),
                      _full((1, 2 * d)), _full((1, B_WIDTH)), _full((1, B_WIDTH)),
                      _full((B_GROUPS, GMLP_BLOCK, GMLP_BLOCK)), _full((GMLP_BLOCK, B_WIDTH)),
                      _full((B_WIDTH, d))],
            out_specs=[tok_d, tok_d],
            out_shape=[jax.ShapeDtypeStruct((n_tok, d), F32)] * 2,
            compiler_params=_params(("parallel",)),
            name="gmlp",
        )(xf, g1, w_b, w_g, b_gate[l][None, :], lnv_g[l][None, :], lnv_b[l][None, :],
          w_s[l], bs_full, w_oB[l].astype(BF16))

        n_pairs = A_WIDTH // HEAD_PAIR
        n_chunks = WKV_TOKENS // WKV_CHUNK
        tok_w = pl.BlockSpec((WKV_SEQS, WKV_TOKENS, A_WIDTH), lambda bi, ti: (bi, ti, 0))
        vec_w = _full((1, A_WIDTH))
        seq_major = lambda t: t.reshape(bsz, seq, A_WIDTH)
        tok_scratch = pltpu.VMEM((WKV_SEQS, WKV_TOKENS, A_WIDTH), BF16)
        ya_in = pl.pallas_call(
            _wkv_kernel,
            grid=(bsz // WKV_SEQS, seq // WKV_TOKENS),
            in_specs=[tok_w] * 7 + [vec_w, vec_w, vec_w, _full((HEAD_PAIR, HEAD_PAIR))],
            out_specs=tok_w,
            out_shape=jax.ShapeDtypeStruct((bsz, seq, A_WIDTH), F32),
            scratch_shapes=[pltpu.VMEM((WKV_SEQS, n_pairs, HEAD_PAIR, HEAD_PAIR), F32)]
            + [tok_scratch] * 5
            + [pltpu.VMEM((WKV_SEQS, n_chunks, n_pairs, HEAD_PAIR, HEAD_PAIR), BF16),
               pltpu.VMEM((WKV_SEQS, n_chunks, n_pairs, HEAD_PAIR, HEAD_PAIR), F32)],
            compiler_params=_params(("parallel", "arbitrary")),
            name="wkv",
        )(*[seq_major(t) for t in (r_, k_, v_, a_, b_, ld_, gg_)], r_k[l].reshape(1, A_WIDTH),
          lnx_g[l][None, :], lnx_b[l][None, :],
          ones_bd[:HEAD_PAIR, :HEAD_PAIR]).reshape(n_tok, A_WIDTH)

        w_r = _place(LANES, [(0, jnp.transpose(w_re[l], (1, 0, 2)).reshape(d, N_EXPERTS)),
                             (N_EXPERTS, w_rg[l])])
        b_r = _place(LANES, [(0, b_re[l].reshape(1, N_EXPERTS)), (N_EXPERTS, b_rg[l][None, :])])
        wr_hi = w_r.astype(BF16)
        wr_lo = (w_r - wr_hi.astype(F32)).astype(BF16)
        tok_p = pl.BlockSpec((tm_p, d), row512)
        lane_p = pl.BlockSpec((tm_p, LANES), row512)
        x2, h2, meta, wts, cnt = pl.pallas_call(
            _post_kernel,
            grid=(n_tok // tm_p,),
            in_specs=[tok_p, pl.BlockSpec((tm_p, A_WIDTH), row512), tok_p, tok_p,
                      _full((A_WIDTH, d)), _full((d, d)), _full((1, d)), _full((d, LANES)),
                      _full((d, LANES)), _full((1, LANES))],
            out_specs=[tok_p, tok_p, pl.BlockSpec((8, tm_p), lambda i: (0, i)), lane_p,
                       _full((8, LANES))],
            out_shape=[jax.ShapeDtypeStruct((n_tok, d), F32),
                       jax.ShapeDtypeStruct((n_tok, d), F32),
                       jax.ShapeDtypeStruct((8, n_tok), jnp.int32),
                       jax.ShapeDtypeStruct((n_tok, LANES), F32),
                       jax.ShapeDtypeStruct((8, LANES), F32)],
            scratch_shapes=[pltpu.VMEM((8, LANES), F32)],
            compiler_params=_params(("arbitrary",)),
            name="post",
        )(xf, ya_in, ga, ybg, w_oA[l].astype(BF16), w_out[l].astype(BF16),
          norm2_g[l][None, :], wr_hi, wr_lo, b_r)

        assert 2 * n_tok // MOE_TILE + N_EXPERTS <= MOE_MAX_TILES and 2 * n_tok <= RANK_SPAN
        counts = cnt[0, :N_EXPERTS].astype(jnp.int32)
        tiles_e = (counts + MOE_TILE - 1) // MOE_TILE
        tile_end = jnp.cumsum(tiles_e)
        tile_start = tile_end - tiles_e
        n_used = tile_end[-1:]
        row_off = tile_start * MOE_TILE
        code1, code2 = meta[0], meta[1]
        tile_ids = jnp.minimum(jnp.arange(MOE_MAX_TILES, dtype=jnp.int32), n_used - 1)
        tile_expert = jnp.sum(tile_ids[:, None] >= tile_end[None, :], axis=1).astype(jnp.int32)
        zrow = (tile_end - 1) * MOE_TILE
        zvalid = (tiles_e > 0).astype(jnp.int32)
        n_rows = MOE_MAX_TILES * MOE_TILE

        td = 256
        h_sorted = pl.pallas_call(
            _dispatch_kernel,
            grid_spec=pltpu.PrefetchScalarGridSpec(
                num_scalar_prefetch=6, grid=(n_tok // td,),
                in_specs=[pl.BlockSpec((td, d), lambda i, *_: (i, 0))],
                out_specs=pl.BlockSpec(memory_space=pl.ANY),
                scratch_shapes=[pltpu.VMEM((MOE_TILE, d), F32), pltpu.SemaphoreType.DMA((2,))]),
            out_shape=jax.ShapeDtypeStruct((n_rows, d), F32),
            compiler_params=_params(("arbitrary",)),
            name="moe_dispatch",
        )(code1, code2, row_off, zrow, zvalid, n_used, h2)

        used_tile = lambda i, te, nu: (jnp.minimum(i, nu[0] - 1), 0)
        y_sorted = pl.pallas_call(
            _expert_kernel,
            grid_spec=pltpu.PrefetchScalarGridSpec(
                num_scalar_prefetch=2, grid=(MOE_MAX_TILES,),
                in_specs=[pl.BlockSpec((MOE_TILE, d), used_tile),
                          pl.BlockSpec((1, d, D_EXPERT), lambda i, te, nu: (te[i], 0, 0)),
                          pl.BlockSpec((1, d, D_EXPERT), lambda i, te, nu: (te[i], 0, 0)),
                          pl.BlockSpec((1, D_EXPERT, d), lambda i, te, nu: (te[i], 0, 0))],
                out_specs=pl.BlockSpec((MOE_TILE, d), lambda i, te, nu: (i, 0)),
                scratch_shapes=[pltpu.VMEM((d, 2 * D_EXPERT), BF16),
                                pltpu.VMEM((D_EXPERT, d), BF16)]),
            out_shape=jax.ShapeDtypeStruct((n_rows, d), F32),
            compiler_params=_params(("arbitrary",)),
            name="moe_experts",
        )(tile_expert, n_used, h_sorted, w_e_gate[l], w_e_up[l], w_e_down[l])

        tc = 256
        xf = pl.pallas_call(
            _combine_kernel,
            grid_spec=pltpu.PrefetchScalarGridSpec(
                num_scalar_prefetch=3, grid=(n_tok // tc,),
                in_specs=[pl.BlockSpec((tc, d), lambda i, *_: (i, 0)),
                          pl.BlockSpec((tc, LANES), lambda i, *_: (i, 0)),
                          pl.BlockSpec((1, d), lambda i, *_: (0, 0)),
                          pl.BlockSpec(memory_space=pl.ANY)],
                out_specs=pl.BlockSpec((tc, d), lambda i, *_: (i, 0)),
                scratch_shapes=[pltpu.VMEM((2, tc, d), F32), pltpu.SemaphoreType.DMA((2,))]),
            out_shape=jax.ShapeDtypeStruct((n_tok, d), F32),
            compiler_params=_params(("arbitrary",)),
            name="moe_combine",
        )(code1, code2, row_off, x2, wts, final_g[None, :], y_sorted)

    return xf.reshape(bsz, seq, d)
```

```python
import functools

import jax
import jax.numpy as jnp
from jax import lax
from jax.experimental import pallas as pl
from jax.experimental.pallas import tpu as pltpu

F32 = jnp.float32
BF16 = jnp.bfloat16

D_MODEL = 1024
A_WIDTH = 512
A_HEAD = 64
D_DECAY_LORA = 64
D_AAA_LORA = 64
D_GATE_LORA = 160
B_WIDTH = 512
B_GROUPS = 4
B_GROUP_CH = 128
GMLP_BLOCK = 128
N_GROUPS = 4
EXPERTS_PER_GROUP = 8
N_EXPERTS = 32
D_EXPERT = 256
NORM_EPS = 1e-6
LN_EPS = 1e-5
LNX_EPS = 64e-5

LANES = 128
LORA_PAD = 512
XW_OFF, XA_OFF, XG_OFF = 0, 128, 256
A_PROJ = 3 * A_WIDTH + LORA_PAD
WKV_CHUNK = 64
HEAD_PAIR = 2 * A_HEAD
WKV_SEQS = 2
WKV_TOKENS = 256
WKV_PREP_GROUP = 4
WKV_NORM_GROUP = 2
MOE_TILE = 256
MOE_SUB = 256
RUN_ALIGN = 16
RUN_SLOTS = 2 * MOE_SUB + N_EXPERTS * RUN_ALIGN
RUN_CHUNKS = RUN_SLOTS // RUN_ALIGN
MOE_MAX_TILES = ((2 * 16384 + 16384 // MOE_SUB * N_EXPERTS * (RUN_ALIGN - 1)) // MOE_TILE
                 + N_EXPERTS)
VMEM_LIMIT = 48 * 1024 * 1024


def _rms(x, g):
    return x * lax.rsqrt(jnp.mean(x * x, axis=-1, keepdims=True) + NORM_EPS) * g


def _dot(a, b):
    return jnp.dot(a, b, preferred_element_type=F32)


def _dot_nt(a, b):
    return lax.dot_general(a, b, (((1,), (1,)), ((), ())), preferred_element_type=F32)


def _split2(x):
    hi = x.astype(BF16)
    lo = (x - hi.astype(F32)).astype(BF16)
    return hi, lo


def _split3(x):
    hi = x.astype(BF16)
    r1 = x - hi.astype(F32)
    mid = r1.astype(BF16)
    lo = (r1 - mid.astype(F32)).astype(BF16)
    return hi, mid, lo


def _head_sum(x, ones_bd):
    hi, lo = _split2(x)
    return _dot(hi, ones_bd) + _dot(lo, ones_bd)


def _pair_head_sums(xs, ones_pair):
    n_tiles = A_WIDTH // HEAD_PAIR
    rows = xs[0].shape[0]
    tiles = [x[:, t * HEAD_PAIR:(t + 1) * HEAD_PAIR].astype(BF16) for x in xs for t in range(n_tiles)]
    sums = _dot(jnp.concatenate(tiles, axis=0), ones_pair)
    return [jnp.concatenate([sums[(i * n_tiles + t) * rows:(i * n_tiles + t + 1) * rows]
                             for t in range(n_tiles)], axis=1) for i in range(len(xs))]


def _rwkv_prep_kernel(tiles_per_seq, x_ref, g1_ref, wa_ref, mu_ref, w0_ref,
                      w2_ref, a0_ref, a2_ref, g2_ref, kk_ref, ka_ref, ones_ref,
                      r_out, k_out, v_out, a_out, b_out, ld_out, g_out, tail_ref):
    i = pl.program_id(0)
    h = _rms(x_ref[...], g1_ref[...]).astype(BF16)
    p = _dot(h, wa_ref[...])
    tm = p.shape[0]
    prev = jnp.where(i % tiles_per_seq == 0, 0.0, tail_ref[7:8, :])
    tail_ref[...] = p[tm - 8:, :]
    row = lax.broadcasted_iota(jnp.int32, p.shape, 0)
    shifted = jnp.where(row == 0, prev, pltpu.roll(p, 1, axis=0))
    pm = p + mu_ref[...] * (shifted - p)

    r = pm[:, 0:A_WIDTH]
    k = pm[:, A_WIDTH:2 * A_WIDTH]
    v = pm[:, 2 * A_WIDTH:3 * A_WIDTH]
    lora = pm[:, 3 * A_WIDTH:]
    xw = lora[:, XW_OFF:XW_OFF + LANES]
    xa = lora[:, XA_OFF:XA_OFF + LANES]
    xg = lora[:, XG_OFF:]

    z = -(w0_ref[...] + _dot(jnp.tanh(xw).astype(BF16), w2_ref[...]))
    softplus = jnp.maximum(z, 0.0) + jnp.log(1.0 + jnp.exp(-jnp.abs(z)))
    w = -softplus - 0.5
    a_lr = jax.nn.sigmoid(a0_ref[...] + _dot(xa.astype(BF16), a2_ref[...]))
    g = _dot(jax.nn.sigmoid(xg).astype(BF16), g2_ref[...])

    kk = k * kk_ref[...]
    norm = jnp.sqrt(_head_sum(kk * kk, ones_ref[...]))
    kk = kk / jnp.maximum(norm, 1e-12)

    r_out[...] = r
    k_out[...] = k * (1.0 + (a_lr - 1.0) * ka_ref[...])
    v_out[...] = v
    a_out[...] = -kk
    b_out[...] = kk * a_lr
    ld_out[...] = -jnp.exp(w)
    g_out[...] = g


def _gmlp_kernel(x_ref, g1_ref, wb_ref, wg_ref, bg_ref, lng_ref, lnb_ref, ws_ref, bs_ref,
                 wo_ref, ybg_out, ga_out):
    tm = x_ref.shape[0]
    h = _rms(x_ref[...], g1_ref[...]).astype(BF16)
    pb = _dot(h, wb_ref[...])
    z = 0.5 * pb * (1.0 + lax.erf(pb * (2.0 ** -0.5)))
    u = z[:, :B_WIDTH]
    v = z[:, B_WIDTH:]
    mean = jnp.mean(v, axis=-1, keepdims=True)
    vc = v - mean
    var = jnp.mean(vc * vc, axis=-1, keepdims=True)
    vn = (vc * lax.rsqrt(var + LN_EPS) * lng_ref[...] + lnb_ref[...]).astype(BF16)

    tri = (lax.broadcasted_iota(jnp.int32, (GMLP_BLOCK, GMLP_BLOCK), 0)
           >= lax.broadcasted_iota(jnp.int32, (GMLP_BLOCK, GMLP_BLOCK), 1))
    bs = bs_ref[...]
    rows = []
    for blk in range(tm // GMLP_BLOCK):
        cols = []
        for grp in range(B_GROUPS):
            ws = jnp.where(tri, ws_ref[grp], 0.0).astype(BF16)
            vb = vn[blk * GMLP_BLOCK:(blk + 1) * GMLP_BLOCK,
                    grp * B_GROUP_CH:(grp + 1) * B_GROUP_CH]
            cols.append(_dot(ws, vb))
        rows.append(jnp.concatenate(cols, axis=1) + bs)
    sv = jnp.concatenate(rows, axis=0)
    yb = _dot((u * sv).astype(BF16), wo_ref[...])
    gates = jax.nn.sigmoid(_dot(h, wg_ref[...]) + bg_ref[...])
    ga_out[...] = gates[:, :D_MODEL]
    ybg_out[...] = gates[:, D_MODEL:] * yb


def _wkv_kernel(r_ref, k_ref, v_ref, a_ref, b_ref, ld_ref, g_ref, rk_ref, lng_ref, lnb_ref,
                ones_ref, o_ref, st_ref, ta_s, tl_s, arb_s, ark_s, rt_s, bkt_s, dcol_s):
    C = WKV_CHUNK
    bb, tb, _ = r_ref.shape
    n_chunks = tb // C
    n_pairs = A_WIDTH // HEAD_PAIR

    @pl.when(pl.program_id(1) == 0)
    def _():
        st_ref[...] = jnp.zeros_like(st_ref)

    row = lax.broadcasted_iota(jnp.int32, (C, HEAD_PAIR), 0)
    src = lax.broadcasted_iota(jnp.int32, (C, HEAD_PAIR), 1) & (C - 1)
    incl = src <= row
    strict = src < row
    eye_pair = jnp.where(src == row, 1.0, 0.0)
    bd_mask = ((lax.broadcasted_iota(jnp.int32, (HEAD_PAIR, HEAD_PAIR), 0) >= A_HEAD)
               == (lax.broadcasted_iota(jnp.int32, (HEAD_PAIR, HEAD_PAIR), 1) >= A_HEAD))
    tri_c = (lax.broadcasted_iota(jnp.int32, (C, C), 0)
             >= lax.broadcasted_iota(jnp.int32, (C, C), 1)).astype(BF16)
    ones_pair = ones_ref[...]
    pair_cols = [slice(p * HEAD_PAIR, (p + 1) * HEAD_PAIR) for p in range(n_pairs)]

    def bd(x):
        xb = x.astype(BF16)
        return jnp.where(bd_mask, jnp.concatenate([xb, xb], axis=0), jnp.zeros((), BF16))

    def prep(it, carry):
        where, lhs, rhs, ats = [], [], [], []
        for j in range(WKV_PREP_GROUP):
            flat = it * WKV_PREP_GROUP + j
            b = flat // n_chunks
            c = flat % n_chunks
            sl = pl.ds(pl.multiple_of(c * C, C), C)
            ld = ld_ref[b, sl, :]
            hi, mid, lo = _split3(ld)
            cl = _dot(tri_c, hi) + _dot(tri_c, mid) + _dot(tri_c, lo)
            cl_end = cl[C - 1:C, :]
            k = k_ref[b, sl, :]
            bv = b_ref[b, sl, :]
            d_inv = jnp.exp(-cl)
            d_tail = jnp.exp(cl_end - cl)
            rt = (r_ref[b, sl, :] * jnp.exp(cl)).astype(BF16)
            rt_s[b, sl, :] = rt
            kt = k * d_inv
            bt = bv * d_inv
            at = a_ref[b, sl, :] * jnp.exp(cl - ld)
            kd = k * d_tail
            bdk = bv * d_tail
            d_end = jnp.exp(cl_end)
            for p, cs in enumerate(pair_cols):
                bkt_s[b, c, p] = jnp.concatenate([bdk[:, cs], kd[:, cs]], axis=0).T.astype(BF16)
                dcol_s[b, c, p] = jnp.broadcast_to(d_end[:, cs], (HEAD_PAIR, HEAD_PAIR)).T
                where.append((b, sl, cs))
                ats.append(at[:, cs])
                lhs.append(jnp.concatenate([rt[:, cs], at[:, cs].astype(BF16)], axis=0))
                rhs.append(jnp.concatenate([bd(kt[:, cs]), bd(bt[:, cs])], axis=0))
        n = len(where)
        amat = [_dot_nt(lhs[i], rhs[i]) for i in range(n)]
        l_ak, l_ab = [], []
        for i, (b, sl, cs) in enumerate(where):
            ark_s[b, sl, cs] = jnp.where(incl, amat[i][:C, :HEAD_PAIR], 0.0).astype(BF16)
            arb_s[b, sl, cs] = jnp.where(incl, amat[i][:C, HEAD_PAIR:], 0.0).astype(BF16)
            l_ak.append(jnp.where(strict, amat[i][C:, :HEAD_PAIR], 0.0))
            l_ab.append(jnp.where(strict, amat[i][C:, HEAD_PAIR:], 0.0))

        t_mat = [eye_pair + l for l in l_ab]
        q = [_dot(l.astype(BF16), bd(l)) for l in l_ab]
        n_sq = 1
        while 2 * n_sq < C // 2:
            both = [_dot(q[i].astype(BF16), jnp.concatenate([bd(q[i]), bd(t_mat[i])], axis=1))
                    for i in range(n)]
            q = [x[:, :HEAD_PAIR] for x in both]
            t_mat = [t_mat[i] + both[i][:, HEAD_PAIR:] for i in range(n)]
            n_sq *= 2
        t_mat = [t_mat[i] + _dot(q[i].astype(BF16), bd(t_mat[i])) for i in range(n)]
        tal = [_dot(t_mat[i].astype(BF16), jnp.concatenate([bd(ats[i]), bd(l_ak[i])], axis=1))
               for i in range(n)]
        for i, (b, sl, cs) in enumerate(where):
            ta_s[b, sl, cs] = tal[i][:, :HEAD_PAIR].astype(BF16)
            tl_s[b, sl, cs] = tal[i][:, HEAD_PAIR:].astype(BF16)
        return carry

    lax.fori_loop(0, bb * n_chunks // WKV_PREP_GROUP, prep, 0)

    def step(c, carry):
        sl = pl.ds(pl.multiple_of(c * C, C), C)
        chains = [(b, p) for b in range(bb) for p in range(n_pairs)]
        v = [v_ref[b, sl, :] for b in range(bb)]
        st = [st_ref[b, p] for b, p in chains]
        st_b = [x.astype(BF16) for x in st]
        bd_v = [bd(v[b][:, pair_cols[p]]) for b, p in chains]
        on_st = [_dot(jnp.concatenate([ta_s[b, sl, pair_cols[p]], rt_s[b, sl, pair_cols[p]]],
                                      axis=0), st_b[i]) for i, (b, p) in enumerate(chains)]
        on_v = [_dot(jnp.concatenate([tl_s[b, sl, pair_cols[p]], ark_s[b, sl, pair_cols[p]]],
                                     axis=0), bd_v[i]) for i, (b, p) in enumerate(chains)]
        u = [on_st[i][:C] + on_v[i][:C] for i in range(len(chains))]
        for i, (b, p) in enumerate(chains):
            uv = jnp.concatenate([u[i], v[b][:, pair_cols[p]]], axis=0).astype(BF16)
            st_ref[b, p] = (dcol_s[b, c, p] * st[i]
                            + jnp.where(bd_mask, _dot(bkt_s[b, c, p], uv), 0.0))
        ys = [on_st[i][C:] + on_v[i][C:] + _dot(arb_s[b, sl, pair_cols[p]], bd(u[i]))
              for i, (b, p) in enumerate(chains)]
        for b in range(bb):
            o_ref[b, sl, :] = jnp.concatenate(ys[b * n_pairs:(b + 1) * n_pairs], axis=1)
        return carry

    lax.fori_loop(0, n_chunks, step, 0)

    def finish(it, carry):
        items = [(b, pl.ds(pl.multiple_of((it * WKV_NORM_GROUP + j) * C, C), C))
                 for j in range(WKV_NORM_GROUP) for b in range(bb)]
        y = [o_ref[b, sl, :] for b, sl in items]
        rkr = [r_ref[b, sl, :] * k_ref[b, sl, :] * rk_ref[...] for b, sl in items]
        sums = [_pair_head_sums([y[i], rkr[i]], ones_pair) for i in range(len(items))]
        yc = [y[i] - sums[i][0] * (1.0 / A_HEAD) for i in range(len(items))]
        var = [_pair_head_sums([x * x], ones_pair)[0] * (1.0 / A_HEAD) for x in yc]
        for i, (b, sl) in enumerate(items):
            yn = yc[i] * lax.rsqrt(var[i] + LNX_EPS) * lng_ref[...] + lnb_ref[...]
            o_ref[b, sl, :] = (yn + sums[i][1] * v_ref[b, sl, :]) * g_ref[b, sl, :]
        return carry

    lax.fori_loop(0, n_chunks // WKV_NORM_GROUP, finish, 0)


def _post_kernel(x_ref, ya_ref, ga_ref, ybg_ref, woa_ref, wout_ref, g2_ref, wr_hi_ref,
                 wr_lo_ref, br_ref, x2_out, h2_out, route_out, wts_out, cnt_out):
    y_a = _dot(ya_ref[...].astype(BF16), woa_ref[...])
    mix = ga_ref[...] * y_a + ybg_ref[...]
    x2 = x_ref[...] + _dot(mix.astype(BF16), wout_ref[...])
    x2_out[...] = x2
    h2 = _rms(x2, g2_ref[...])
    h2_out[...] = h2.astype(BF16)

    hi, lo = _split2(h2)
    wr_hi = wr_hi_ref[...]
    logits = _dot(hi, wr_hi) + _dot(lo, wr_hi) + _dot(hi, wr_lo_ref[...]) + br_ref[...]
    lane = lax.broadcasted_iota(jnp.int32, logits.shape, 1)
    neg = jnp.float32(-jnp.inf)
    big = jnp.int32(LANES)
    is_grp = (lane >= N_EXPERTS) & (lane < N_EXPERTS + N_GROUPS)
    gl = jnp.where(is_grp, logits, neg)
    gmax = jnp.max(gl, axis=-1, keepdims=True)
    g_p = 1.0 / jnp.sum(jnp.exp(gl - gmax), axis=-1, keepdims=True)
    g_idx = jnp.min(jnp.where(gl == gmax, lane, big), axis=-1, keepdims=True) - N_EXPERTS
    lo_lane = g_idx * EXPERTS_PER_GROUP
    in_grp = (lane >= lo_lane) & (lane < lo_lane + EXPERTS_PER_GROUP)
    el = jnp.where(in_grp, logits, neg)
    e1 = jnp.max(el, axis=-1, keepdims=True)
    i1 = jnp.min(jnp.where(el == e1, lane, big), axis=-1, keepdims=True)
    el2 = jnp.where(lane == i1, neg, el)
    e2 = jnp.max(el2, axis=-1, keepdims=True)
    i2 = jnp.min(jnp.where(el2 == e2, lane, big), axis=-1, keepdims=True)
    t = jnp.exp(e2 - e1)
    w1 = g_p / (1.0 + t)
    w2 = g_p * t / (1.0 + t)
    wts_out[...] = jnp.where(lane == 0, w1, jnp.where(lane == 1, w2, 0.0))

    tm = logits.shape[0]
    onehot = jnp.where((lane == i1) | (lane == i2), 1.0, 0.0)
    tri = (lax.broadcasted_iota(jnp.int32, (tm, tm), 0)
           >= lax.broadcasted_iota(jnp.int32, (tm, tm), 1)).astype(BF16)
    incl = _dot(tri, onehot.astype(BF16))
    row = lax.broadcasted_iota(jnp.int32, logits.shape, 0)
    row8 = lax.broadcasted_iota(jnp.int32, (8, LANES), 0)
    ends = [incl[(g + 1) * MOE_SUB - 1:(g + 1) * MOE_SUB, :] for g in range(tm // MOE_SUB)]
    before = incl - onehot
    counts = jnp.zeros((8, LANES), F32)
    for g, end in enumerate(ends):
        start = ends[g - 1] if g else jnp.zeros_like(end)
        before = before - jnp.where((row >= g * MOE_SUB) & (row < (g + 1) * MOE_SUB), start, 0.0)
        counts = jnp.where(row8 == g, end - start, counts)
    rank1 = jnp.sum(jnp.where(lane == i1, before, 0.0), axis=-1, keepdims=True).astype(jnp.int32)
    rank2 = jnp.sum(jnp.where(lane == i2, before, 0.0), axis=-1, keepdims=True).astype(jnp.int32)
    route_out[...] = jnp.where(lane == 0, i1, jnp.where(lane == 1, i2, jnp.where(
        lane == 2, rank1, jnp.where(lane == 3, rank2, 0))))
    cnt_out[...] = counts


def _group_slots(route_ref, loc_ref):
    route = route_ref[...]
    lane = lax.broadcasted_iota(jnp.int32, route.shape, 1)
    loc = loc_ref[0:1, :].astype(F32)

    def slot(k):
        start = jnp.sum(jnp.where(lane == route[:, k:k + 1], loc, 0.0), axis=-1, keepdims=True)
        return start.astype(jnp.int32) + route[:, 2 + k:3 + k]

    return slot(0), slot(1)


def _run_chunk(ref, row):
    return ref.at[pl.ds(pl.multiple_of(row, RUN_ALIGN), RUN_ALIGN)]


def _dispatch_kernel(nch_ref, dstc_ref, zrow_ref, zvalid_ref, nu_ref, h_ref, route_ref, loc_ref,
                     hs_ref, hl_s, zbuf, sem):
    i = pl.program_id(0)

    def zero_tile(row):
        start = pl.multiple_of(row, MOE_TILE)
        return pltpu.make_async_copy(zbuf, hs_ref.at[pl.ds(start, MOE_TILE)], sem.at[0])

    def for_unused_tiles(fn):
        def body(t, carry):
            fn(zero_tile(t * MOE_TILE))
            return carry
        lax.fori_loop(nu_ref[0], MOE_MAX_TILES, body, 0)

    @pl.when(i == 0)
    def _():
        zbuf[...] = jnp.zeros_like(zbuf)
        for e in range(N_EXPERTS):
            @pl.when(zvalid_ref[e] > 0)
            def _():
                zero_tile(zrow_ref[e]).start()
        for_unused_tiles(lambda cp: cp.start())
        for e in range(N_EXPERTS):
            @pl.when(zvalid_ref[e] > 0)
            def _():
                zero_tile(zrow_ref[e]).wait()
        for_unused_tiles(lambda cp: cp.wait())

    slot1, slot2 = _group_slots(route_ref, loc_ref)
    slots = lax.broadcasted_iota(jnp.int32, (h_ref.shape[0], RUN_SLOTS), 1)
    pick = jnp.where((slots == slot1) | (slots == slot2), 1.0, 0.0)
    hl_s[...] = _dot(pick.T.astype(BF16), h_ref[...]).astype(BF16)

    def chunk_copy(c):
        dst = dstc_ref[i * RUN_CHUNKS + c]
        return pltpu.make_async_copy(_run_chunk(hl_s, c * RUN_ALIGN), _run_chunk(hs_ref, dst),
                                     sem.at[1])

    def start(c, carry):
        chunk_copy(c).start()
        return carry

    def wait(c, carry):
        chunk_copy(c).wait()
        return carry

    lax.fori_loop(0, nch_ref[i], start, 0)
    lax.fori_loop(0, nch_ref[i], wait, 0)


def _expert_kernel(te_ref, nu_ref, hs_ref, wg_ref, wu_ref, wd_ref, ys_ref, wgu_s, wd_s):
    i = pl.program_id(0)

    @pl.when((i == 0) | (te_ref[i] != te_ref[jnp.maximum(i - 1, 0)]))
    def _():
        wgu_s[:, :D_EXPERT] = wg_ref[0].astype(BF16)
        wgu_s[:, D_EXPERT:] = wu_ref[0].astype(BF16)
        wd_s[...] = wd_ref[0].astype(BF16)

    @pl.when(i < nu_ref[0])
    def _():
        gu = _dot(hs_ref[...], wgu_s[...])
        gate = gu[:, :D_EXPERT]
        act = gate * jax.nn.sigmoid(gate) * gu[:, D_EXPERT:]
        ys_ref[...] = _dot(act.astype(BF16), wd_s[...]).astype(BF16)

    @pl.when(pl.program_id(0) >= nu_ref[0])
    def _():
        ys_ref[...] = jnp.zeros_like(ys_ref)


def _combine_kernel(nch_ref, dstc_ref, x2_ref, route_ref, loc_ref, wts_ref, gf_ref, ys_ref,
                    o_ref, ybuf, sem):
    i = pl.program_id(0)

    @pl.when(i == 0)
    def _():
        ybuf[...] = jnp.zeros_like(ybuf)

    def chunk_copy(c):
        src = dstc_ref[i * RUN_CHUNKS + c]
        return pltpu.make_async_copy(_run_chunk(ys_ref, src), _run_chunk(ybuf, c * RUN_ALIGN),
                                     sem.at[0])

    def start(c, carry):
        chunk_copy(c).start()
        return carry

    def wait(c, carry):
        chunk_copy(c).wait()
        return carry

    lax.fori_loop(0, nch_ref[i], start, 0)
    slot1, slot2 = _group_slots(route_ref, loc_ref)
    wts = wts_ref[...]
    slots = lax.broadcasted_iota(jnp.int32, (x2_ref.shape[0], RUN_SLOTS), 1)
    pick = (jnp.where(slots == slot1, wts[:, 0:1], 0.0)
            + jnp.where(slots == slot2, wts[:, 1:2], 0.0))
    lax.fori_loop(0, nch_ref[i], wait, 0)
    y = _dot(pick.astype(BF16), ybuf[...])
    o_ref[...] = _rms(x2_ref[...] + y, gf_ref[...])


def _full(shape):
    return pl.BlockSpec(shape, lambda *_: (0,) * len(shape))


def _params(sem):
    return pltpu.CompilerParams(dimension_semantics=sem, vmem_limit_bytes=VMEM_LIMIT)


def _mixer_a_layout(t):
    s_rkv = 3 * A_WIDTH
    s_w = s_rkv + D_DECAY_LORA
    s_a = s_w + D_AAA_LORA
    gap = lambda n: jnp.zeros((t.shape[0], n), t.dtype)
    return jnp.concatenate(
        [t[:, :s_w], gap(XA_OFF - D_DECAY_LORA), t[:, s_w:s_a], gap(XG_OFF - XA_OFF - D_AAA_LORA),
         t[:, s_a:], gap(LORA_PAD - XG_OFF - D_GATE_LORA)], axis=1)


def _place(cols, parts):
    out = jnp.zeros((parts[0][1].shape[0], cols), parts[0][1].dtype)
    for off, arr in parts:
        out = lax.dynamic_update_slice(out, arr, (0, off))
    return out


def kernel(x, norm1_g, w_in, b_gate, tmix_mu, w0, w2, a0, a2, g2, k_k, k_a, r_k, lnx_g, lnx_b,
           w_oA, lnv_g, lnv_b, w_s, b_s, w_oB, w_out, norm2_g, w_rg, b_rg, w_re, b_re,
           w_e_gate, w_e_up, w_e_down, final_g):
    bsz, seq, d = x.shape
    n_tok = bsz * seq
    depth = norm1_g.shape[0]
    assert depth == 1, "the moe kernel fuses the final norm, so it must be the last layer"
    xf = x.reshape(n_tok, d)

    s_rkv = 3 * A_WIDTH
    s_w = s_rkv + D_DECAY_LORA
    s_a = s_w + D_AAA_LORA
    a_cols = s_a + D_GATE_LORA
    b_cols = 2 * B_WIDTH

    ones_bd = (jnp.arange(A_WIDTH)[:, None] // A_HEAD
               == jnp.arange(A_WIDTH)[None, :] // A_HEAD).astype(BF16)

    tm_a = 512
    tm_b = 512
    tm_p = 512

    for l in range(depth):
        wl = w_in[l]
        w_a = _mixer_a_layout(wl[:, :a_cols]).astype(BF16)
        mu_a = _mixer_a_layout(tmix_mu[l][None, :])
        w2p = jnp.pad(w2[l], ((0, LANES - D_DECAY_LORA), (0, 0))).astype(BF16)
        a2p = jnp.pad(a2[l], ((0, LANES - D_AAA_LORA), (0, 0))).astype(BF16)
        g2p = jnp.pad(g2[l], ((0, LORA_PAD - XG_OFF - D_GATE_LORA), (0, 0))).astype(BF16)
        g1 = norm1_g[l][None, :]

        row512 = lambda i: (i, 0)
        tok_a = pl.BlockSpec((tm_a, A_WIDTH), row512)
        vec_a = _full((1, A_WIDTH))
        outs = pl.pallas_call(
            functools.partial(_rwkv_prep_kernel, seq // tm_a),
            grid=(n_tok // tm_a,),
            in_specs=[pl.BlockSpec((tm_a, d), row512),
                      _full((1, d)), _full((d, A_PROJ)), _full((1, A_PROJ)), vec_a,
                      _full((LANES, A_WIDTH)), vec_a, _full((LANES, A_WIDTH)),
                      _full((LORA_PAD - XG_OFF, A_WIDTH)), vec_a, vec_a,
                      _full((A_WIDTH, A_WIDTH))],
            out_specs=[tok_a] * 7,
            out_shape=[jax.ShapeDtypeStruct((n_tok, A_WIDTH), F32)] * 7,
            scratch_shapes=[pltpu.VMEM((8, A_PROJ), F32)],
            compiler_params=_params(("arbitrary",)),
            name="rwkv_prep",
        )(xf, g1, w_a, mu_a, w0[l][None, :], w2p, a0[l][None, :], a2p, g2p,
          k_k[l][None, :], k_a[l][None, :], ones_bd)
        r_, k_, v_, a_, b_, ld_, gg_ = outs

        w_b = wl[:, a_cols:a_cols + b_cols].astype(BF16)
        w_g = wl[:, a_cols + b_cols:].astype(BF16)
        bs_full = jnp.repeat(b_s[l].T, B_GROUP_CH, axis=1)
        tok_d = pl.BlockSpec((tm_b, d), row512)
        ybg, ga = pl.pallas_call(
            _gmlp_kernel,
            grid=(n_tok // tm_b,),
            in_specs=[tok_d, _full((1, d)), _full((d, b_cols)), _full((d, 2 * d)),
                      _full((1, 2 * d)), _full((1, B_WIDTH)), _full((1, B_WIDTH)),
                      _full((B_GROUPS, GMLP_BLOCK, GMLP_BLOCK)), _full((GMLP_BLOCK, B_WIDTH)),
                      _full((B_WIDTH, d))],
            out_specs=[tok_d, tok_d],
            out_shape=[jax.ShapeDtypeStruct((n_tok, d), F32)] * 2,
            compiler_params=_params(("parallel",)),
            name="gmlp",
        )(xf, g1, w_b, w_g, b_gate[l][None, :], lnv_g[l][None, :], lnv_b[l][None, :],
          w_s[l], bs_full, w_oB[l].astype(BF16))

        n_pairs = A_WIDTH // HEAD_PAIR
        n_chunks = WKV_TOKENS // WKV_CHUNK
        tok_w = pl.BlockSpec((WKV_SEQS, WKV_TOKENS, A_WIDTH), lambda bi, ti: (bi, ti, 0))
        vec_w = _full((1, A_WIDTH))
        seq_major = lambda t: t.reshape(bsz, seq, A_WIDTH)
        tok_scratch = pltpu.VMEM((WKV_SEQS, WKV_TOKENS, A_WIDTH), BF16)
        ya_in = pl.pallas_call(
            _wkv_kernel,
            grid=(bsz // WKV_SEQS, seq // WKV_TOKENS),
            in_specs=[tok_w] * 7 + [vec_w, vec_w, vec_w, _full((HEAD_PAIR, HEAD_PAIR))],
            out_specs=tok_w,
            out_shape=jax.ShapeDtypeStruct((bsz, seq, A_WIDTH), F32),
            scratch_shapes=[pltpu.VMEM((WKV_SEQS, n_pairs, HEAD_PAIR, HEAD_PAIR), F32)]
            + [tok_scratch] * 5
            + [pltpu.VMEM((WKV_SEQS, n_chunks, n_pairs, HEAD_PAIR, HEAD_PAIR), BF16),
               pltpu.VMEM((WKV_SEQS, n_chunks, n_pairs, HEAD_PAIR, HEAD_PAIR), F32)],
            compiler_params=_params(("parallel", "arbitrary")),
            name="wkv",
        )(*[seq_major(t) for t in (r_, k_, v_, a_, b_, ld_, gg_)], r_k[l].reshape(1, A_WIDTH),
          lnx_g[l][None, :], lnx_b[l][None, :],
          ones_bd[:HEAD_PAIR, :HEAD_PAIR]).reshape(n_tok, A_WIDTH)

        w_r = _place(LANES, [(0, jnp.transpose(w_re[l], (1, 0, 2)).reshape(d, N_EXPERTS)),
                             (N_EXPERTS, w_rg[l])])
        b_r = _place(LANES, [(0, b_re[l].reshape(1, N_EXPERTS)), (N_EXPERTS, b_rg[l][None, :])])
        wr_hi = w_r.astype(BF16)
        wr_lo = (w_r - wr_hi.astype(F32)).astype(BF16)
        tok_p = pl.BlockSpec((tm_p, d), row512)
        lane_p = pl.BlockSpec((tm_p, LANES), row512)
        x2, h2, route, wts, cnt = pl.pallas_call(
            _post_kernel,
            grid=(n_tok // tm_p,),
            in_specs=[tok_p, pl.BlockSpec((tm_p, A_WIDTH), row512), tok_p, tok_p,
                      _full((A_WIDTH, d)), _full((d, d)), _full((1, d)), _full((d, LANES)),
                      _full((d, LANES)), _full((1, LANES))],
            out_specs=[tok_p, tok_p, lane_p, lane_p, pl.BlockSpec((8, LANES), row512)],
            out_shape=[jax.ShapeDtypeStruct((n_tok, d), F32),
                       jax.ShapeDtypeStruct((n_tok, d), BF16),
                       jax.ShapeDtypeStruct((n_tok, LANES), jnp.int32),
                       jax.ShapeDtypeStruct((n_tok, LANES), F32),
                       jax.ShapeDtypeStruct((n_tok // tm_p * 8, LANES), F32)],
            compiler_params=_params(("parallel",)),
            name="post",
        )(xf, ya_in, ga, ybg, w_oA[l].astype(BF16), w_out[l].astype(BF16),
          norm2_g[l][None, :], wr_hi, wr_lo, b_r)

        assert ((2 * n_tok + n_tok // MOE_SUB * N_EXPERTS * (RUN_ALIGN - 1)) // MOE_TILE
                + N_EXPERTS <= MOE_MAX_TILES)
        n_sub = n_tok // MOE_SUB
        sub_cnt = cnt.reshape(n_tok // tm_p, 8, LANES)[:, :tm_p // MOE_SUB, :N_EXPERTS]
        sub_cnt = sub_cnt.reshape(n_sub, N_EXPERTS).astype(jnp.int32)
        run_len = (sub_cnt + RUN_ALIGN - 1) // RUN_ALIGN * RUN_ALIGN
        tiles_e = (jnp.sum(run_len, axis=0) + MOE_TILE - 1) // MOE_TILE
        tile_end = jnp.cumsum(tiles_e)
        n_used = tile_end[-1:]
        row_off = (tile_end - tiles_e) * MOE_TILE
        run_dst = row_off[None, :] + jnp.cumsum(run_len, axis=0) - run_len
        loc_end = jnp.cumsum(run_len, axis=1)
        loc_off = loc_end - run_len
        n_chunks_sub = loc_end[:, -1] // RUN_ALIGN
        chunk_row = jnp.arange(RUN_CHUNKS, dtype=jnp.int32) * RUN_ALIGN
        chunk_e = jnp.minimum(jnp.sum(chunk_row[None, :, None] >= loc_end[:, None, :], axis=2),
                              N_EXPERTS - 1)
        chunk_dst = (jnp.take_along_axis(run_dst, chunk_e, axis=1) + chunk_row[None, :]
                     - jnp.take_along_axis(loc_off, chunk_e, axis=1)).reshape(-1)
        loc_rows = jnp.pad(loc_off[:, None, :], ((0, 0), (0, 7), (0, LANES - N_EXPERTS)))
        loc_rows = loc_rows.reshape(n_sub * 8, LANES)
        tile_ids = jnp.minimum(jnp.arange(MOE_MAX_TILES, dtype=jnp.int32), n_used - 1)
        tile_expert = jnp.sum(tile_ids[:, None] >= tile_end[None, :], axis=1).astype(jnp.int32)
        zrow = (tile_end - 1) * MOE_TILE
        zvalid = (tiles_e > 0).astype(jnp.int32)
        n_rows = MOE_MAX_TILES * MOE_TILE

        sub_rows = lambda i, *_: (i, 0)
        h_sorted = pl.pallas_call(
            _dispatch_kernel,
            grid_spec=pltpu.PrefetchScalarGridSpec(
                num_scalar_prefetch=5, grid=(n_sub,),
                in_specs=[pl.BlockSpec((MOE_SUB, d), sub_rows),
                          pl.BlockSpec((MOE_SUB, LANES), sub_rows),
                          pl.BlockSpec((8, LANES), sub_rows)],
                out_specs=pl.BlockSpec(memory_space=pl.ANY),
                scratch_shapes=[pltpu.VMEM((RUN_SLOTS, d), BF16), pltpu.VMEM((MOE_TILE, d), BF16),
                                pltpu.SemaphoreType.DMA((2,))]),
            out_shape=jax.ShapeDtypeStruct((n_rows, d), BF16),
            compiler_params=_params(("arbitrary",)),
            name="moe_dispatch",
        )(n_chunks_sub, chunk_dst, zrow, zvalid, n_used, h2, route, loc_rows)

        used_tile = lambda i, te, nu: (jnp.minimum(i, nu[0] - 1), 0)
        y_sorted = pl.pallas_call(
            _expert_kernel,
            grid_spec=pltpu.PrefetchScalarGridSpec(
                num_scalar_prefetch=2, grid=(MOE_MAX_TILES,),
                in_specs=[pl.BlockSpec((MOE_TILE, d), used_tile),
                          pl.BlockSpec((1, d, D_EXPERT), lambda i, te, nu: (te[i], 0, 0)),
                          pl.BlockSpec((1, d, D_EXPERT), lambda i, te, nu: (te[i], 0, 0)),
                          pl.BlockSpec((1, D_EXPERT, d), lambda i, te, nu: (te[i], 0, 0))],
                out_specs=pl.BlockSpec((MOE_TILE, d), lambda i, te, nu: (i, 0)),
                scratch_shapes=[pltpu.VMEM((d, 2 * D_EXPERT), BF16),
                                pltpu.VMEM((D_EXPERT, d), BF16)]),
            out_shape=jax.ShapeDtypeStruct((n_rows, d), BF16),
            compiler_params=_params(("arbitrary",)),
            name="moe_experts",
        )(tile_expert, n_used, h_sorted, w_e_gate[l], w_e_up[l], w_e_down[l])

        xf = pl.pallas_call(
            _combine_kernel,
            grid_spec=pltpu.PrefetchScalarGridSpec(
                num_scalar_prefetch=2, grid=(n_sub,),
                in_specs=[pl.BlockSpec((MOE_SUB, d), sub_rows),
                          pl.BlockSpec((MOE_SUB, LANES), sub_rows),
                          pl.BlockSpec((8, LANES), sub_rows),
                          pl.BlockSpec((MOE_SUB, LANES), sub_rows),
                          pl.BlockSpec((1, d), lambda i, *_: (0, 0)),
                          pl.BlockSpec(memory_space=pl.ANY)],
                out_specs=pl.BlockSpec((MOE_SUB, d), sub_rows),
                scratch_shapes=[pltpu.VMEM((RUN_SLOTS, d), BF16), pltpu.SemaphoreType.DMA((1,))]),
            out_shape=jax.ShapeDtypeStruct((n_tok, d), F32),
            compiler_params=_params(("arbitrary",)),
            name="moe_combine",
        )(n_chunks_sub, chunk_dst, x2, route, loc_rows, wts, final_g[None, :], y_sorted)

    return xf.reshape(bsz, seq, d)
```

```python
import functools

import jax
import jax.numpy as jnp
from jax import lax
from jax.experimental import pallas as pl
from jax.experimental.pallas import tpu as pltpu

F32 = jnp.float32
BF16 = jnp.bfloat16

D_MODEL = 1024
A_WIDTH = 512
A_HEAD = 64
D_DECAY_LORA = 64
D_AAA_LORA = 64
D_GATE_LORA = 160
B_WIDTH = 512
B_GROUPS = 4
B_GROUP_CH = 128
GMLP_BLOCK = 128
N_GROUPS = 4
EXPERTS_PER_GROUP = 8
N_EXPERTS = 32
D_EXPERT = 256
NORM_EPS = 1e-6
LN_EPS = 1e-5
LNX_EPS = 64e-5

LANES = 128
LORA_PAD = 512
XW_OFF, XA_OFF, XG_OFF = 0, 128, 256
A_PROJ = 3 * A_WIDTH + LORA_PAD
WKV_CHUNK = 64
HEAD_PAIR = 2 * A_HEAD
WKV_SEQS = 2
WKV_TOKENS = 256
WKV_PREP_GROUP = 4
WKV_NORM_GROUP = 2
MOE_TILE = 512
MOE_SUB = 256
RUN_ALIGN = 16
RUN_SLOTS = 2 * MOE_SUB + N_EXPERTS * RUN_ALIGN
RUN_CHUNKS = RUN_SLOTS // RUN_ALIGN
MOE_MAX_TILES = ((2 * 16384 + 16384 // MOE_SUB * N_EXPERTS * (RUN_ALIGN - 1)) // MOE_TILE
                 + N_EXPERTS)
VMEM_LIMIT = 48 * 1024 * 1024


def _rms(x, g):
    return x * lax.rsqrt(jnp.mean(x * x, axis=-1, keepdims=True) + NORM_EPS) * g


def _dot(a, b):
    return jnp.dot(a, b, preferred_element_type=F32)


def _dot_nt(a, b):
    return lax.dot_general(a, b, (((1,), (1,)), ((), ())), preferred_element_type=F32)


def _split2(x):
    hi = x.astype(BF16)
    lo = (x - hi.astype(F32)).astype(BF16)
    return hi, lo


def _split3(x):
    hi = x.astype(BF16)
    r1 = x - hi.astype(F32)
    mid = r1.astype(BF16)
    lo = (r1 - mid.astype(F32)).astype(BF16)
    return hi, mid, lo


def _head_sum(x, ones_bd):
    hi, lo = _split2(x)
    return _dot(hi, ones_bd) + _dot(lo, ones_bd)


def _pair_head_sums(xs, ones_pair):
    n_tiles = A_WIDTH // HEAD_PAIR
    rows = xs[0].shape[0]
    tiles = [x[:, t * HEAD_PAIR:(t + 1) * HEAD_PAIR].astype(BF16) for x in xs for t in range(n_tiles)]
    sums = _dot(jnp.concatenate(tiles, axis=0), ones_pair)
    return [jnp.concatenate([sums[(i * n_tiles + t) * rows:(i * n_tiles + t + 1) * rows]
                             for t in range(n_tiles)], axis=1) for i in range(len(xs))]


def _rwkv_prep_kernel(tiles_per_seq, x_ref, g1_ref, wa_ref, mu_ref, w0_ref,
                      w2_ref, a0_ref, a2_ref, g2_ref, kk_ref, ka_ref, ones_ref,
                      r_out, k_out, v_out, a_out, b_out, ld_out, g_out, tail_ref):
    i = pl.program_id(0)
    h = _rms(x_ref[...], g1_ref[...]).astype(BF16)
    p = _dot(h, wa_ref[...])
    tm = p.shape[0]
    prev = jnp.where(i % tiles_per_seq == 0, 0.0, tail_ref[7:8, :])
    tail_ref[...] = p[tm - 8:, :]
    row = lax.broadcasted_iota(jnp.int32, p.shape, 0)
    shifted = jnp.where(row == 0, prev, pltpu.roll(p, 1, axis=0))
    pm = p + mu_ref[...] * (shifted - p)

    r = pm[:, 0:A_WIDTH]
    k = pm[:, A_WIDTH:2 * A_WIDTH]
    v = pm[:, 2 * A_WIDTH:3 * A_WIDTH]
    lora = pm[:, 3 * A_WIDTH:]
    xw = lora[:, XW_OFF:XW_OFF + LANES]
    xa = lora[:, XA_OFF:XA_OFF + LANES]
    xg = lora[:, XG_OFF:]

    z = -(w0_ref[...] + _dot(jnp.tanh(xw).astype(BF16), w2_ref[...]))
    softplus = jnp.maximum(z, 0.0) + jnp.log(1.0 + jnp.exp(-jnp.abs(z)))
    w = -softplus - 0.5
    a_lr = jax.nn.sigmoid(a0_ref[...] + _dot(xa.astype(BF16), a2_ref[...]))
    g = _dot(jax.nn.sigmoid(xg).astype(BF16), g2_ref[...])

    kk = k * kk_ref[...]
    norm = jnp.sqrt(_head_sum(kk * kk, ones_ref[...]))
    kk = kk / jnp.maximum(norm, 1e-12)

    r_out[...] = r
    k_out[...] = k * (1.0 + (a_lr - 1.0) * ka_ref[...])
    v_out[...] = v
    a_out[...] = -kk
    b_out[...] = kk * a_lr
    ld_out[...] = -jnp.exp(w)
    g_out[...] = g


def _gmlp_kernel(x_ref, g1_ref, wb_ref, wg_ref, bg_ref, lng_ref, lnb_ref, ws_ref, bs_ref,
                 wo_ref, ybg_out, ga_out):
    tm = x_ref.shape[0]
    h = _rms(x_ref[...], g1_ref[...]).astype(BF16)
    pb = _dot(h, wb_ref[...])
    z = 0.5 * pb * (1.0 + lax.erf(pb * (2.0 ** -0.5)))
    u = z[:, :B_WIDTH]
    v = z[:, B_WIDTH:]
    mean = jnp.mean(v, axis=-1, keepdims=True)
    vc = v - mean
    var = jnp.mean(vc * vc, axis=-1, keepdims=True)
    vn = (vc * lax.rsqrt(var + LN_EPS) * lng_ref[...] + lnb_ref[...]).astype(BF16)

    tri = (lax.broadcasted_iota(jnp.int32, (GMLP_BLOCK, GMLP_BLOCK), 0)
           >= lax.broadcasted_iota(jnp.int32, (GMLP_BLOCK, GMLP_BLOCK), 1))
    bs = bs_ref[...]
    rows = []
    for blk in range(tm // GMLP_BLOCK):
        cols = []
        for grp in range(B_GROUPS):
            ws = jnp.where(tri, ws_ref[grp], 0.0).astype(BF16)
            vb = vn[blk * GMLP_BLOCK:(blk + 1) * GMLP_BLOCK,
                    grp * B_GROUP_CH:(grp + 1) * B_GROUP_CH]
            cols.append(_dot(ws, vb))
        rows.append(jnp.concatenate(cols, axis=1) + bs)
    sv = jnp.concatenate(rows, axis=0)
    yb = _dot((u * sv).astype(BF16), wo_ref[...])
    gates = jax.nn.sigmoid(_dot(h, wg_ref[...]) + bg_ref[...])
    ga_out[...] = gates[:, :D_MODEL]
    ybg_out[...] = gates[:, D_MODEL:] * yb


def _wkv_kernel(r_ref, k_ref, v_ref, a_ref, b_ref, ld_ref, g_ref, rk_ref, lng_ref, lnb_ref,
                ones_ref, o_ref, st_ref, ta_s, tl_s, arb_s, ark_s, rt_s, bkt_s, dcol_s):
    C = WKV_CHUNK
    bb, tb, _ = r_ref.shape
    n_chunks = tb // C
    n_pairs = A_WIDTH // HEAD_PAIR

    @pl.when(pl.program_id(1) == 0)
    def _():
        st_ref[...] = jnp.zeros_like(st_ref)

    row = lax.broadcasted_iota(jnp.int32, (C, HEAD_PAIR), 0)
    src = lax.broadcasted_iota(jnp.int32, (C, HEAD_PAIR), 1) & (C - 1)
    incl = src <= row
    strict = src < row
    eye_pair = jnp.where(src == row, 1.0, 0.0)
    bd_mask = ((lax.broadcasted_iota(jnp.int32, (HEAD_PAIR, HEAD_PAIR), 0) >= A_HEAD)
               == (lax.broadcasted_iota(jnp.int32, (HEAD_PAIR, HEAD_PAIR), 1) >= A_HEAD))
    tri_c = (lax.broadcasted_iota(jnp.int32, (C, C), 0)
             >= lax.broadcasted_iota(jnp.int32, (C, C), 1)).astype(BF16)
    ones_pair = ones_ref[...]
    pair_cols = [slice(p * HEAD_PAIR, (p + 1) * HEAD_PAIR) for p in range(n_pairs)]

    def bd(x):
        xb = x.astype(BF16)
        return jnp.where(bd_mask, jnp.concatenate([xb, xb], axis=0), jnp.zeros((), BF16))

    def prep(it, carry):
        where, lhs, rhs, ats = [], [], [], []
        for j in range(WKV_PREP_GROUP):
            flat = it * WKV_PREP_GROUP + j
            b = flat // n_chunks
            c = flat % n_chunks
            sl = pl.ds(pl.multiple_of(c * C, C), C)
            ld = ld_ref[b, sl, :]
            hi, mid, lo = _split3(ld)
            cl = _dot(tri_c, hi) + _dot(tri_c, mid) + _dot(tri_c, lo)
            cl_end = cl[C - 1:C, :]
            k = k_ref[b, sl, :]
            bv = b_ref[b, sl, :]
            d_inv = jnp.exp(-cl)
            d_tail = jnp.exp(cl_end - cl)
            rt = (r_ref[b, sl, :] * jnp.exp(cl)).astype(BF16)
            rt_s[b, sl, :] = rt
            kt = k * d_inv
            bt = bv * d_inv
            at = a_ref[b, sl, :] * jnp.exp(cl - ld)
            kd = k * d_tail
            bdk = bv * d_tail
            d_end = jnp.exp(cl_end)
            for p, cs in enumerate(pair_cols):
                bkt_s[b, c, p] = jnp.concatenate([bdk[:, cs], kd[:, cs]], axis=0).T.astype(BF16)
                dcol_s[b, c, p] = jnp.broadcast_to(d_end[:, cs], (HEAD_PAIR, HEAD_PAIR)).T
                where.append((b, sl, cs))
                ats.append(at[:, cs])
                lhs.append(jnp.concatenate([rt[:, cs], at[:, cs].astype(BF16)], axis=0))
                rhs.append(jnp.concatenate([bd(kt[:, cs]), bd(bt[:, cs])], axis=0))
        n = len(where)
        amat = [_dot_nt(lhs[i], rhs[i]) for i in range(n)]
        l_ak, l_ab = [], []
        for i, (b, sl, cs) in enumerate(where):
            ark_s[b, sl, cs] = jnp.where(incl, amat[i][:C, :HEAD_PAIR], 0.0).astype(BF16)
            arb_s[b, sl, cs] = jnp.where(incl, amat[i][:C, HEAD_PAIR:], 0.0).astype(BF16)
            l_ak.append(jnp.where(strict, amat[i][C:, :HEAD_PAIR], 0.0))
            l_ab.append(jnp.where(strict, amat[i][C:, HEAD_PAIR:], 0.0))

        t_mat = [eye_pair + l for l in l_ab]
        q = [_dot(l.astype(BF16), bd(l)) for l in l_ab]
        n_sq = 1
        while 2 * n_sq < C // 2:
            both = [_dot(q[i].astype(BF16), jnp.concatenate([bd(q[i]), bd(t_mat[i])], axis=1))
                    for i in range(n)]
            q = [x[:, :HEAD_PAIR] for x in both]
            t_mat = [t_mat[i] + both[i][:, HEAD_PAIR:] for i in range(n)]
            n_sq *= 2
        t_mat = [t_mat[i] + _dot(q[i].astype(BF16), bd(t_mat[i])) for i in range(n)]
        tal = [_dot(t_mat[i].astype(BF16), jnp.concatenate([bd(ats[i]), bd(l_ak[i])], axis=1))
               for i in range(n)]
        for i, (b, sl, cs) in enumerate(where):
            ta_s[b, sl, cs] = tal[i][:, :HEAD_PAIR].astype(BF16)
            tl_s[b, sl, cs] = tal[i][:, HEAD_PAIR:].astype(BF16)
        return carry

    lax.fori_loop(0, bb * n_chunks // WKV_PREP_GROUP, prep, 0)

    def step(c, carry):
        sl = pl.ds(pl.multiple_of(c * C, C), C)
        chains = [(b, p) for b in range(bb) for p in range(n_pairs)]
        v = [v_ref[b, sl, :] for b in range(bb)]
        st = [st_ref[b, p] for b, p in chains]
        st_b = [x.astype(BF16) for x in st]
        bd_v = [bd(v[b][:, pair_cols[p]]) for b, p in chains]
        on_st = [_dot(jnp.concatenate([ta_s[b, sl, pair_cols[p]], rt_s[b, sl, pair_cols[p]]],
                                      axis=0), st_b[i]) for i, (b, p) in enumerate(chains)]
        on_v = [_dot(jnp.concatenate([tl_s[b, sl, pair_cols[p]], ark_s[b, sl, pair_cols[p]]],
                                     axis=0), bd_v[i]) for i, (b, p) in enumerate(chains)]
        u = [on_st[i][:C] + on_v[i][:C] for i in range(len(chains))]
        for i, (b, p) in enumerate(chains):
            uv = jnp.concatenate([u[i], v[b][:, pair_cols[p]]], axis=0).astype(BF16)
            st_ref[b, p] = (dcol_s[b, c, p] * st[i]
                            + jnp.where(bd_mask, _dot(bkt_s[b, c, p], uv), 0.0))
        ys = [on_st[i][C:] + on_v[i][C:] + _dot(arb_s[b, sl, pair_cols[p]], bd(u[i]))
              for i, (b, p) in enumerate(chains)]
        for b in range(bb):
            o_ref[b, sl, :] = jnp.concatenate(ys[b * n_pairs:(b + 1) * n_pairs], axis=1)
        return carry

    lax.fori_loop(0, n_chunks, step, 0)

    def finish(it, carry):
        items = [(b, pl.ds(pl.multiple_of((it * WKV_NORM_GROUP + j) * C, C), C))
                 for j in range(WKV_NORM_GROUP) for b in range(bb)]
        y = [o_ref[b, sl, :] for b, sl in items]
        rkr = [r_ref[b, sl, :] * k_ref[b, sl, :] * rk_ref[...] for b, sl in items]
        sums = [_pair_head_sums([y[i], rkr[i]], ones_pair) for i in range(len(items))]
        yc = [y[i] - sums[i][0] * (1.0 / A_HEAD) for i in range(len(items))]
        var = [_pair_head_sums([x * x], ones_pair)[0] * (1.0 / A_HEAD) for x in yc]
        for i, (b, sl) in enumerate(items):
            yn = yc[i] * lax.rsqrt(var[i] + LNX_EPS) * lng_ref[...] + lnb_ref[...]
            o_ref[b, sl, :] = (yn + sums[i][1] * v_ref[b, sl, :]) * g_ref[b, sl, :]
        return carry

    lax.fori_loop(0, n_chunks // WKV_NORM_GROUP, finish, 0)


def _post_kernel(x_ref, ya_ref, ga_ref, ybg_ref, woa_ref, wout_ref, g2_ref, wr_hi_ref,
                 wr_lo_ref, br_ref, x2_out, h2_out, route_out, wts_out, cnt_out):
    y_a = _dot(ya_ref[...].astype(BF16), woa_ref[...])
    mix = ga_ref[...] * y_a + ybg_ref[...]
    x2 = x_ref[...] + _dot(mix.astype(BF16), wout_ref[...])
    x2_out[...] = x2
    h2 = _rms(x2, g2_ref[...])
    h2_out[...] = h2.astype(BF16)

    hi, lo = _split2(h2)
    wr_hi = wr_hi_ref[...]
    logits = _dot(hi, wr_hi) + _dot(lo, wr_hi) + _dot(hi, wr_lo_ref[...]) + br_ref[...]
    lane = lax.broadcasted_iota(jnp.int32, logits.shape, 1)
    neg = jnp.float32(-jnp.inf)
    big = jnp.int32(LANES)
    is_grp = (lane >= N_EXPERTS) & (lane < N_EXPERTS + N_GROUPS)
    gl = jnp.where(is_grp, logits, neg)
    gmax = jnp.max(gl, axis=-1, keepdims=True)
    g_p = 1.0 / jnp.sum(jnp.exp(gl - gmax), axis=-1, keepdims=True)
    g_idx = jnp.min(jnp.where(gl == gmax, lane, big), axis=-1, keepdims=True) - N_EXPERTS
    lo_lane = g_idx * EXPERTS_PER_GROUP
    in_grp = (lane >= lo_lane) & (lane < lo_lane + EXPERTS_PER_GROUP)
    el = jnp.where(in_grp, logits, neg)
    e1 = jnp.max(el, axis=-1, keepdims=True)
    i1 = jnp.min(jnp.where(el == e1, lane, big), axis=-1, keepdims=True)
    el2 = jnp.where(lane == i1, neg, el)
    e2 = jnp.max(el2, axis=-1, keepdims=True)
    i2 = jnp.min(jnp.where(el2 == e2, lane, big), axis=-1, keepdims=True)
    t = jnp.exp(e2 - e1)
    w1 = g_p / (1.0 + t)
    w2 = g_p * t / (1.0 + t)
    wts_out[...] = jnp.where(lane == 0, w1, jnp.where(lane == 1, w2, 0.0))

    tm = logits.shape[0]
    onehot = jnp.where((lane == i1) | (lane == i2), 1.0, 0.0)
    tri = (lax.broadcasted_iota(jnp.int32, (tm, tm), 0)
           >= lax.broadcasted_iota(jnp.int32, (tm, tm), 1)).astype(BF16)
    incl = _dot(tri, onehot.astype(BF16))
    row = lax.broadcasted_iota(jnp.int32, logits.shape, 0)
    row8 = lax.broadcasted_iota(jnp.int32, (8, LANES), 0)
    ends = [incl[(g + 1) * MOE_SUB - 1:(g + 1) * MOE_SUB, :] for g in range(tm // MOE_SUB)]
    before = incl - onehot
    counts = jnp.zeros((8, LANES), F32)
    for g, end in enumerate(ends):
        start = ends[g - 1] if g else jnp.zeros_like(end)
        before = before - jnp.where((row >= g * MOE_SUB) & (row < (g + 1) * MOE_SUB), start, 0.0)
        counts = jnp.where(row8 == g, end - start, counts)
    rank1 = jnp.sum(jnp.where(lane == i1, before, 0.0), axis=-1, keepdims=True).astype(jnp.int32)
    rank2 = jnp.sum(jnp.where(lane == i2, before, 0.0), axis=-1, keepdims=True).astype(jnp.int32)
    route_out[...] = jnp.where(lane == 0, i1, jnp.where(lane == 1, i2, jnp.where(
        lane == 2, rank1, jnp.where(lane == 3, rank2, 0))))
    cnt_out[...] = counts


def _group_slots(route_ref, loc_ref):
    route = route_ref[...]
    lane = lax.broadcasted_iota(jnp.int32, route.shape, 1)
    loc = loc_ref[0:1, :].astype(F32)

    def slot(k):
        start = jnp.sum(jnp.where(lane == route[:, k:k + 1], loc, 0.0), axis=-1, keepdims=True)
        return start.astype(jnp.int32) + route[:, 2 + k:3 + k]

    return slot(0), slot(1)


def _run_chunk(ref, row):
    return ref.at[pl.ds(pl.multiple_of(row, RUN_ALIGN), RUN_ALIGN)]


def _dispatch_kernel(nch_ref, dstc_ref, zrow_ref, zvalid_ref, nu_ref, h_ref, route_ref, loc_ref,
                     hs_ref, hl_s, zbuf, sem):
    i = pl.program_id(0)

    def zero_tile(row):
        start = pl.multiple_of(row, MOE_TILE)
        return pltpu.make_async_copy(zbuf, hs_ref.at[pl.ds(start, MOE_TILE)], sem.at[0])

    def for_unused_tiles(fn):
        def body(t, carry):
            fn(zero_tile(t * MOE_TILE))
            return carry
        lax.fori_loop(nu_ref[0], MOE_MAX_TILES, body, 0)

    @pl.when(i == 0)
    def _():
        zbuf[...] = jnp.zeros_like(zbuf)
        for e in range(N_EXPERTS):
            @pl.when(zvalid_ref[e] > 0)
            def _():
                zero_tile(zrow_ref[e]).start()
        for_unused_tiles(lambda cp: cp.start())
        for e in range(N_EXPERTS):
            @pl.when(zvalid_ref[e] > 0)
            def _():
                zero_tile(zrow_ref[e]).wait()
        for_unused_tiles(lambda cp: cp.wait())

    slot1, slot2 = _group_slots(route_ref, loc_ref)
    slots = lax.broadcasted_iota(jnp.int32, (h_ref.shape[0], RUN_SLOTS), 1)
    pick = jnp.where((slots == slot1) | (slots == slot2), 1.0, 0.0)
    buf = i % 2
    hl_s[buf] = _dot(pick.T.astype(BF16), h_ref[...]).astype(BF16)

    def for_chunks(group, fn):
        def body(c, carry):
            dst = dstc_ref[group * RUN_CHUNKS + c]
            fn(pltpu.make_async_copy(_run_chunk(hl_s.at[group % 2], c * RUN_ALIGN),
                                     _run_chunk(hs_ref, dst), sem.at[1 + group % 2]))
            return carry
        lax.fori_loop(0, nch_ref[group], body, 0)

    for_chunks(i, lambda cp: cp.start())

    @pl.when(i > 0)
    def _():
        for_chunks(i - 1, lambda cp: cp.wait())

    @pl.when(i == pl.num_programs(0) - 1)
    def _():
        for_chunks(i, lambda cp: cp.wait())


def _expert_kernel(te_ref, nu_ref, hs_ref, wg_ref, wu_ref, wd_ref, ys_ref, wgu_s, wd_s):
    i = pl.program_id(0)

    @pl.when((i == 0) | (te_ref[i] != te_ref[jnp.maximum(i - 1, 0)]))
    def _():
        wgu_s[:, :D_EXPERT] = wg_ref[0].astype(BF16)
        wgu_s[:, D_EXPERT:] = wu_ref[0].astype(BF16)
        wd_s[...] = wd_ref[0].astype(BF16)

    @pl.when(i < nu_ref[0])
    def _():
        gu = _dot(hs_ref[...], wgu_s[...])
        gate = gu[:, :D_EXPERT]
        act = gate * jax.nn.sigmoid(gate) * gu[:, D_EXPERT:]
        ys_ref[...] = _dot(act.astype(BF16), wd_s[...]).astype(BF16)

    @pl.when(pl.program_id(0) >= nu_ref[0])
    def _():
        ys_ref[...] = jnp.zeros_like(ys_ref)


def _combine_kernel(nch_ref, dstc_ref, x2_ref, route_ref, loc_ref, wts_ref, gf_ref, ys_ref,
                    o_ref, ybuf, sem):
    i = pl.program_id(0)

    def for_chunks(group, fn):
        def body(c, carry):
            src = dstc_ref[group * RUN_CHUNKS + c]
            fn(pltpu.make_async_copy(_run_chunk(ys_ref, src),
                                     _run_chunk(ybuf.at[group % 2], c * RUN_ALIGN),
                                     sem.at[group % 2]))
            return carry
        lax.fori_loop(0, nch_ref[group], body, 0)

    @pl.when(i == 0)
    def _():
        ybuf[...] = jnp.zeros_like(ybuf)
        for_chunks(0, lambda cp: cp.start())

    @pl.when(i + 1 < pl.num_programs(0))
    def _():
        for_chunks(i + 1, lambda cp: cp.start())

    slot1, slot2 = _group_slots(route_ref, loc_ref)
    wts = wts_ref[...]
    slots = lax.broadcasted_iota(jnp.int32, (x2_ref.shape[0], RUN_SLOTS), 1)
    pick = (jnp.where(slots == slot1, wts[:, 0:1], 0.0)
            + jnp.where(slots == slot2, wts[:, 1:2], 0.0))
    for_chunks(i, lambda cp: cp.wait())
    y = _dot(pick.astype(BF16), ybuf[i % 2])
    o_ref[...] = _rms(x2_ref[...] + y, gf_ref[...])


def _full(shape):
    return pl.BlockSpec(shape, lambda *_: (0,) * len(shape))


def _params(sem):
    return pltpu.CompilerParams(dimension_semantics=sem, vmem_limit_bytes=VMEM_LIMIT)


def _mixer_a_layout(t):
    s_rkv = 3 * A_WIDTH
    s_w = s_rkv + D_DECAY_LORA
    s_a = s_w + D_AAA_LORA
    gap = lambda n: jnp.zeros((t.shape[0], n), t.dtype)
    return jnp.concatenate(
        [t[:, :s_w], gap(XA_OFF - D_DECAY_LORA), t[:, s_w:s_a], gap(XG_OFF - XA_OFF - D_AAA_LORA),
         t[:, s_a:], gap(LORA_PAD - XG_OFF - D_GATE_LORA)], axis=1)


def _place(cols, parts):
    out = jnp.zeros((parts[0][1].shape[0], cols), parts[0][1].dtype)
    for off, arr in parts:
        out = lax.dynamic_update_slice(out, arr, (0, off))
    return out


def kernel(x, norm1_g, w_in, b_gate, tmix_mu, w0, w2, a0, a2, g2, k_k, k_a, r_k, lnx_g, lnx_b,
           w_oA, lnv_g, lnv_b, w_s, b_s, w_oB, w_out, norm2_g, w_rg, b_rg, w_re, b_re,
           w_e_gate, w_e_up, w_e_down, final_g):
    bsz, seq, d = x.shape
    n_tok = bsz * seq
    depth = norm1_g.shape[0]
    assert depth == 1, "the moe kernel fuses the final norm, so it must be the last layer"
    assert bsz % WKV_SEQS == 0 and seq % WKV_TOKENS == 0
    xf = x.reshape(n_tok, d)

    s_rkv = 3 * A_WIDTH
    s_w = s_rkv + D_DECAY_LORA
    s_a = s_w + D_AAA_LORA
    a_cols = s_a + D_GATE_LORA
    b_cols = 2 * B_WIDTH

    ones_bd = (jnp.arange(A_WIDTH)[:, None] // A_HEAD
               == jnp.arange(A_WIDTH)[None, :] // A_HEAD).astype(BF16)

    tm_a = 512
    tm_b = 512
    tm_p = 512

    for l in range(depth):
        wl = w_in[l]
        w_a = _mixer_a_layout(wl[:, :a_cols]).astype(BF16)
        mu_a = _mixer_a_layout(tmix_mu[l][None, :])
        w2p = jnp.pad(w2[l], ((0, LANES - D_DECAY_LORA), (0, 0))).astype(BF16)
        a2p = jnp.pad(a2[l], ((0, LANES - D_AAA_LORA), (0, 0))).astype(BF16)
        g2p = jnp.pad(g2[l], ((0, LORA_PAD - XG_OFF - D_GATE_LORA), (0, 0))).astype(BF16)
        g1 = norm1_g[l][None, :]

        row512 = lambda i: (i, 0)
        tok_a = pl.BlockSpec((tm_a, A_WIDTH), row512)
        vec_a = _full((1, A_WIDTH))
        outs = pl.pallas_call(
            functools.partial(_rwkv_prep_kernel, seq // tm_a),
            grid=(n_tok // tm_a,),
            in_specs=[pl.BlockSpec((tm_a, d), row512),
                      _full((1, d)), _full((d, A_PROJ)), _full((1, A_PROJ)), vec_a,
                      _full((LANES, A_WIDTH)), vec_a, _full((LANES, A_WIDTH)),
                      _full((LORA_PAD - XG_OFF, A_WIDTH)), vec_a, vec_a,
                      _full((A_WIDTH, A_WIDTH))],
            out_specs=[tok_a] * 7,
            out_shape=[jax.ShapeDtypeStruct((n_tok, A_WIDTH), F32)] * 7,
            scratch_shapes=[pltpu.VMEM((8, A_PROJ), F32)],
            compiler_params=_params(("arbitrary",)),
            name="rwkv_prep",
        )(xf, g1, w_a, mu_a, w0[l][None, :], w2p, a0[l][None, :], a2p, g2p,
          k_k[l][None, :], k_a[l][None, :], ones_bd)
        r_, k_, v_, a_, b_, ld_, gg_ = outs

        w_b = wl[:, a_cols:a_cols + b_cols].astype(BF16)
        w_g = wl[:, a_cols + b_cols:].astype(BF16)
        bs_full = jnp.repeat(b_s[l].T, B_GROUP_CH, axis=1)
        tok_d = pl.BlockSpec((tm_b, d), row512)
        ybg, ga = pl.pallas_call(
            _gmlp_kernel,
            grid=(n_tok // tm_b,),
            in_specs=[tok_d, _full((1, d)), _full((d, b_cols)), _full((d, 2 * d)),
                      _full((1, 2 * d)), _full((1, B_WIDTH)), _full((1, B_WIDTH)),
                      _full((B_GROUPS, GMLP_BLOCK, GMLP_BLOCK)), _full((GMLP_BLOCK, B_WIDTH)),
                      _full((B_WIDTH, d))],
            out_specs=[tok_d, tok_d],
            out_shape=[jax.ShapeDtypeStruct((n_tok, d), F32)] * 2,
            compiler_params=_params(("parallel",)),
            name="gmlp",
        )(xf, g1, w_b, w_g, b_gate[l][None, :], lnv_g[l][None, :], lnv_b[l][None, :],
          w_s[l], bs_full, w_oB[l].astype(BF16))

        n_pairs = A_WIDTH // HEAD_PAIR
        n_chunks = WKV_TOKENS // WKV_CHUNK
        tok_w = pl.BlockSpec((WKV_SEQS, WKV_TOKENS, A_WIDTH), lambda bi, ti: (bi, ti, 0))
        vec_w = _full((1, A_WIDTH))
        seq_major = lambda t: t.reshape(bsz, seq, A_WIDTH)
        tok_scratch = pltpu.VMEM((WKV_SEQS, WKV_TOKENS, A_WIDTH), BF16)
        ya_in = pl.pallas_call(
            _wkv_kernel,
            grid=(bsz // WKV_SEQS, seq // WKV_TOKENS),
            in_specs=[tok_w] * 7 + [vec_w, vec_w, vec_w, _full((HEAD_PAIR, HEAD_PAIR))],
            out_specs=tok_w,
            out_shape=jax.ShapeDtypeStruct((bsz, seq, A_WIDTH), F32),
            scratch_shapes=[pltpu.VMEM((WKV_SEQS, n_pairs, HEAD_PAIR, HEAD_PAIR), F32)]
            + [tok_scratch] * 5
            + [pltpu.VMEM((WKV_SEQS, n_chunks, n_pairs, HEAD_PAIR, HEAD_PAIR), BF16),
               pltpu.VMEM((WKV_SEQS, n_chunks, n_pairs, HEAD_PAIR, HEAD_PAIR), F32)],
            compiler_params=_params(("parallel", "arbitrary")),
            name="wkv",
        )(*[seq_major(t) for t in (r_, k_, v_, a_, b_, ld_, gg_)], r_k[l].reshape(1, A_WIDTH),
          lnx_g[l][None, :], lnx_b[l][None, :],
          ones_bd[:HEAD_PAIR, :HEAD_PAIR]).reshape(n_tok, A_WIDTH)

        w_r = _place(LANES, [(0, jnp.transpose(w_re[l], (1, 0, 2)).reshape(d, N_EXPERTS)),
                             (N_EXPERTS, w_rg[l])])
        b_r = _place(LANES, [(0, b_re[l].reshape(1, N_EXPERTS)), (N_EXPERTS, b_rg[l][None, :])])
        wr_hi = w_r.astype(BF16)
        wr_lo = (w_r - wr_hi.astype(F32)).astype(BF16)
        tok_p = pl.BlockSpec((tm_p, d), row512)
        lane_p = pl.BlockSpec((tm_p, LANES), row512)
        x2, h2, route, wts, cnt = pl.pallas_call(
            _post_kernel,
            grid=(n_tok // tm_p,),
            in_specs=[tok_p, pl.BlockSpec((tm_p, A_WIDTH), row512), tok_p, tok_p,
                      _full((A_WIDTH, d)), _full((d, d)), _full((1, d)), _full((d, LANES)),
                      _full((d, LANES)), _full((1, LANES))],
            out_specs=[tok_p, tok_p, lane_p, lane_p, pl.BlockSpec((8, LANES), row512)],
            out_shape=[jax.ShapeDtypeStruct((n_tok, d), F32),
                       jax.ShapeDtypeStruct((n_tok, d), BF16),
                       jax.ShapeDtypeStruct((n_tok, LANES), jnp.int32),
                       jax.ShapeDtypeStruct((n_tok, LANES), F32),
                       jax.ShapeDtypeStruct((n_tok // tm_p * 8, LANES), F32)],
            compiler_params=_params(("parallel",)),
            name="post",
        )(xf, ya_in, ga, ybg, w_oA[l].astype(BF16), w_out[l].astype(BF16),
          norm2_g[l][None, :], wr_hi, wr_lo, b_r)

        assert ((2 * n_tok + n_tok // MOE_SUB * N_EXPERTS * (RUN_ALIGN - 1)) // MOE_TILE
                + N_EXPERTS <= MOE_MAX_TILES)
        n_sub = n_tok // MOE_SUB
        sub_cnt = cnt.reshape(n_tok // tm_p, 8, LANES)[:, :tm_p // MOE_SUB, :N_EXPERTS]
        sub_cnt = sub_cnt.reshape(n_sub, N_EXPERTS).astype(jnp.int32)
        run_len = (sub_cnt + RUN_ALIGN - 1) // RUN_ALIGN * RUN_ALIGN
        tiles_e = (jnp.sum(run_len, axis=0) + MOE_TILE - 1) // MOE_TILE
        tile_end = jnp.cumsum(tiles_e)
        n_used = tile_end[-1:]
        row_off = (tile_end - tiles_e) * MOE_TILE
        run_dst = row_off[None, :] + jnp.cumsum(run_len, axis=0) - run_len
        loc_end = jnp.cumsum(run_len, axis=1)
        loc_off = loc_end - run_len
        n_chunks_sub = loc_end[:, -1] // RUN_ALIGN
        chunk_row = jnp.arange(RUN_CHUNKS, dtype=jnp.int32) * RUN_ALIGN
        chunk_e = jnp.minimum(jnp.sum(chunk_row[None, :, None] >= loc_end[:, None, :], axis=2),
                              N_EXPERTS - 1)
        of_chunk = chunk_e[:, :, None] == jnp.arange(N_EXPERTS, dtype=jnp.int32)
        chunk_dst = (jnp.sum(jnp.where(of_chunk, (run_dst - loc_off)[:, None, :], 0), axis=2)
                     + chunk_row[None, :]).reshape(-1)
        loc_rows = jnp.pad(loc_off[:, None, :], ((0, 0), (0, 7), (0, LANES - N_EXPERTS)))
        loc_rows = loc_rows.reshape(n_sub * 8, LANES)
        tile_ids = jnp.minimum(jnp.arange(MOE_MAX_TILES, dtype=jnp.int32), n_used - 1)
        tile_expert = jnp.sum(tile_ids[:, None] >= tile_end[None, :], axis=1).astype(jnp.int32)
        zrow = (tile_end - 1) * MOE_TILE
        zvalid = (tiles_e > 0).astype(jnp.int32)
        n_rows = MOE_MAX_TILES * MOE_TILE

        sub_rows = lambda i, *_: (i, 0)
        h_sorted = pl.pallas_call(
            _dispatch_kernel,
            grid_spec=pltpu.PrefetchScalarGridSpec(
                num_scalar_prefetch=5, grid=(n_sub,),
                in_specs=[pl.BlockSpec((MOE_SUB, d), sub_rows),
                          pl.BlockSpec((MOE_SUB, LANES), sub_rows),
                          pl.BlockSpec((8, LANES), sub_rows)],
                out_specs=pl.BlockSpec(memory_space=pl.ANY),
                scratch_shapes=[pltpu.VMEM((2, RUN_SLOTS, d), BF16),
                                pltpu.VMEM((MOE_TILE, d), BF16), pltpu.SemaphoreType.DMA((3,))]),
            out_shape=jax.ShapeDtypeStruct((n_rows, d), BF16),
            compiler_params=_params(("arbitrary",)),
            name="moe_dispatch",
        )(n_chunks_sub, chunk_dst, zrow, zvalid, n_used, h2, route, loc_rows)

        used_tile = lambda i, te, nu: (jnp.minimum(i, nu[0] - 1), 0)
        y_sorted = pl.pallas_call(
            _expert_kernel,
            grid_spec=pltpu.PrefetchScalarGridSpec(
                num_scalar_prefetch=2, grid=(MOE_MAX_TILES,),
                in_specs=[pl.BlockSpec((MOE_TILE, d), used_tile),
                          pl.BlockSpec((1, d, D_EXPERT), lambda i, te, nu: (te[i], 0, 0)),
                          pl.BlockSpec((1, d, D_EXPERT), lambda i, te, nu: (te[i], 0, 0)),
                          pl.BlockSpec((1, D_EXPERT, d), lambda i, te, nu: (te[i], 0, 0))],
                out_specs=pl.BlockSpec((MOE_TILE, d), lambda i, te, nu: (i, 0)),
                scratch_shapes=[pltpu.VMEM((d, 2 * D_EXPERT), BF16),
                                pltpu.VMEM((D_EXPERT, d), BF16)]),
            out_shape=jax.ShapeDtypeStruct((n_rows, d), BF16),
            compiler_params=_params(("arbitrary",)),
            name="moe_experts",
        )(tile_expert, n_used, h_sorted, w_e_gate[l], w_e_up[l], w_e_down[l])

        xf = pl.pallas_call(
            _combine_kernel,
            grid_spec=pltpu.PrefetchScalarGridSpec(
                num_scalar_prefetch=2, grid=(n_sub,),
                in_specs=[pl.BlockSpec((MOE_SUB, d), sub_rows),
                          pl.BlockSpec((MOE_SUB, LANES), sub_rows),
                          pl.BlockSpec((8, LANES), sub_rows),
                          pl.BlockSpec((MOE_SUB, LANES), sub_rows),
                          pl.BlockSpec((1, d), lambda i, *_: (0, 0)),
                          pl.BlockSpec(memory_space=pl.ANY)],
                out_specs=pl.BlockSpec((MOE_SUB, d), sub_rows),
                scratch_shapes=[pltpu.VMEM((2, RUN_SLOTS, d), BF16),
                                pltpu.SemaphoreType.DMA((2,))]),
            out_shape=jax.ShapeDtypeStruct((n_tok, d), F32),
            compiler_params=_params(("arbitrary",)),
            name="moe_combine",
        )(n_chunks_sub, chunk_dst, x2, route, loc_rows, wts, final_g[None, :], y_sorted)

    return xf.reshape(bsz, seq, d)
```

```python
import functools

import jax
import jax.numpy as jnp
from jax import lax
from jax.experimental import pallas as pl
from jax.experimental.pallas import tpu as pltpu

F32 = jnp.float32
BF16 = jnp.bfloat16

D_MODEL = 1024
A_WIDTH = 512
A_HEAD = 64
D_DECAY_LORA = 64
D_AAA_LORA = 64
D_GATE_LORA = 160
B_WIDTH = 512
B_GROUPS = 4
B_GROUP_CH = 128
GMLP_BLOCK = 128
N_GROUPS = 4
EXPERTS_PER_GROUP = 8
N_EXPERTS = 32
D_EXPERT = 256
NORM_EPS = 1e-6
LN_EPS = 1e-5
LNX_EPS = 64e-5

LANES = 128
LORA_PAD = 512
XW_OFF, XA_OFF, XG_OFF = 0, 128, 256
A_PROJ = 3 * A_WIDTH + LORA_PAD
WKV_CHUNK = 64
HEAD_PAIR = 2 * A_HEAD
WKV_SEQS = 2
WKV_TOKENS = 256
WKV_PREP_GROUP = 4
WKV_NORM_GROUP = 2
MOE_SUB = 256
RUN_ALIGN = 16
RUN_SLOTS = 2 * MOE_SUB + N_EXPERTS * RUN_ALIGN
EXPERT_ROWS = 2048
EXPERT_TILE = 256
VMEM_LIMIT = 48 * 1024 * 1024


def _rms(x, g):
    return x * lax.rsqrt(jnp.mean(x * x, axis=-1, keepdims=True) + NORM_EPS) * g


def _dot(a, b):
    return jnp.dot(a, b, preferred_element_type=F32)


def _dot_nt(a, b):
    return lax.dot_general(a, b, (((1,), (1,)), ((), ())), preferred_element_type=F32)


def _split2(x):
    hi = x.astype(BF16)
    lo = (x - hi.astype(F32)).astype(BF16)
    return hi, lo


def _split3(x):
    hi = x.astype(BF16)
    r1 = x - hi.astype(F32)
    mid = r1.astype(BF16)
    lo = (r1 - mid.astype(F32)).astype(BF16)
    return hi, mid, lo


def _head_sum(x, ones_bd):
    hi, lo = _split2(x)
    return _dot(hi, ones_bd) + _dot(lo, ones_bd)


def _pair_head_sums(xs, ones_pair):
    n_tiles = A_WIDTH // HEAD_PAIR
    rows = xs[0].shape[0]
    tiles = [x[:, t * HEAD_PAIR:(t + 1) * HEAD_PAIR].astype(BF16) for x in xs for t in range(n_tiles)]
    sums = _dot(jnp.concatenate(tiles, axis=0), ones_pair)
    return [jnp.concatenate([sums[(i * n_tiles + t) * rows:(i * n_tiles + t + 1) * rows]
                             for t in range(n_tiles)], axis=1) for i in range(len(xs))]


def _rwkv_prep_kernel(tiles_per_seq, x_ref, g1_ref, wa_ref, mu_ref, w0_ref,
                      w2_ref, a0_ref, a2_ref, g2_ref, kk_ref, ka_ref, ones_ref,
                      r_out, k_out, v_out, a_out, b_out, ld_out, g_out, tail_ref):
    i = pl.program_id(0)
    h = _rms(x_ref[...], g1_ref[...]).astype(BF16)
    p = _dot(h, wa_ref[...])
    tm = p.shape[0]
    prev = jnp.where(i % tiles_per_seq == 0, 0.0, tail_ref[7:8, :])
    tail_ref[...] = p[tm - 8:, :]
    row = lax.broadcasted_iota(jnp.int32, p.shape, 0)
    shifted = jnp.where(row == 0, prev, pltpu.roll(p, 1, axis=0))
    pm = p + mu_ref[...] * (shifted - p)

    r = pm[:, 0:A_WIDTH]
    k = pm[:, A_WIDTH:2 * A_WIDTH]
    v = pm[:, 2 * A_WIDTH:3 * A_WIDTH]
    lora = pm[:, 3 * A_WIDTH:]
    xw = lora[:, XW_OFF:XW_OFF + LANES]
    xa = lora[:, XA_OFF:XA_OFF + LANES]
    xg = lora[:, XG_OFF:]

    z = -(w0_ref[...] + _dot(jnp.tanh(xw).astype(BF16), w2_ref[...]))
    softplus = jnp.maximum(z, 0.0) + jnp.log(1.0 + jnp.exp(-jnp.abs(z)))
    w = -softplus - 0.5
    a_lr = jax.nn.sigmoid(a0_ref[...] + _dot(xa.astype(BF16), a2_ref[...]))
    g = _dot(jax.nn.sigmoid(xg).astype(BF16), g2_ref[...])

    kk = k * kk_ref[...]
    norm = jnp.sqrt(_head_sum(kk * kk, ones_ref[...]))
    kk = kk / jnp.maximum(norm, 1e-12)

    r_out[...] = r
    k_out[...] = k * (1.0 + (a_lr - 1.0) * ka_ref[...])
    v_out[...] = v
    a_out[...] = -kk
    b_out[...] = kk * a_lr
    ld_out[...] = -jnp.exp(w)
    g_out[...] = g


def _gmlp_kernel(x_ref, g1_ref, wb_ref, wg_ref, bg_ref, lng_ref, lnb_ref, ws_ref, bs_ref,
                 wo_ref, ybg_out, ga_out):
    tm = x_ref.shape[0]
    h = _rms(x_ref[...], g1_ref[...]).astype(BF16)
    pb = _dot(h, wb_ref[...])
    z = 0.5 * pb * (1.0 + lax.erf(pb * (2.0 ** -0.5)))
    u = z[:, :B_WIDTH]
    v = z[:, B_WIDTH:]
    mean = jnp.mean(v, axis=-1, keepdims=True)
    vc = v - mean
    var = jnp.mean(vc * vc, axis=-1, keepdims=True)
    vn = (vc * lax.rsqrt(var + LN_EPS) * lng_ref[...] + lnb_ref[...]).astype(BF16)

    tri = (lax.broadcasted_iota(jnp.int32, (GMLP_BLOCK, GMLP_BLOCK), 0)
           >= lax.broadcasted_iota(jnp.int32, (GMLP_BLOCK, GMLP_BLOCK), 1))
    bs = bs_ref[...]
    rows = []
    for blk in range(tm // GMLP_BLOCK):
        cols = []
        for grp in range(B_GROUPS):
            ws = jnp.where(tri, ws_ref[grp], 0.0).astype(BF16)
            vb = vn[blk * GMLP_BLOCK:(blk + 1) * GMLP_BLOCK,
                    grp * B_GROUP_CH:(grp + 1) * B_GROUP_CH]
            cols.append(_dot(ws, vb))
        rows.append(jnp.concatenate(cols, axis=1) + bs)
    sv = jnp.concatenate(rows, axis=0)
    yb = _dot((u * sv).astype(BF16), wo_ref[...])
    gates = jax.nn.sigmoid(_dot(h, wg_ref[...]) + bg_ref[...])
    ga_out[...] = gates[:, :D_MODEL]
    ybg_out[...] = gates[:, D_MODEL:] * yb


def _wkv_kernel(r_ref, k_ref, v_ref, a_ref, b_ref, ld_ref, g_ref, rk_ref, lng_ref, lnb_ref,
                ones_ref, o_ref, st_ref, ta_s, tl_s, arb_s, ark_s, rt_s, bkt_s, dcol_s):
    C = WKV_CHUNK
    bb, tb, _ = r_ref.shape
    n_chunks = tb // C
    n_pairs = A_WIDTH // HEAD_PAIR

    @pl.when(pl.program_id(1) == 0)
    def _():
        st_ref[...] = jnp.zeros_like(st_ref)

    row = lax.broadcasted_iota(jnp.int32, (C, HEAD_PAIR), 0)
    src = lax.broadcasted_iota(jnp.int32, (C, HEAD_PAIR), 1) & (C - 1)
    incl = src <= row
    strict = src < row
    eye_pair = jnp.where(src == row, 1.0, 0.0)
    bd_mask = ((lax.broadcasted_iota(jnp.int32, (HEAD_PAIR, HEAD_PAIR), 0) >= A_HEAD)
               == (lax.broadcasted_iota(jnp.int32, (HEAD_PAIR, HEAD_PAIR), 1) >= A_HEAD))
    tri_c = (lax.broadcasted_iota(jnp.int32, (C, C), 0)
             >= lax.broadcasted_iota(jnp.int32, (C, C), 1)).astype(BF16)
    ones_pair = ones_ref[...]
    pair_cols = [slice(p * HEAD_PAIR, (p + 1) * HEAD_PAIR) for p in range(n_pairs)]

    def bd(x):
        xb = x.astype(BF16)
        return jnp.where(bd_mask, jnp.concatenate([xb, xb], axis=0), jnp.zeros((), BF16))

    def prep(it, carry):
        where, lhs, rhs, ats = [], [], [], []
        for j in range(WKV_PREP_GROUP):
            flat = it * WKV_PREP_GROUP + j
            b = flat // n_chunks
            c = flat % n_chunks
            sl = pl.ds(pl.multiple_of(c * C, C), C)
            ld = ld_ref[b, sl, :]
            hi, mid, lo = _split3(ld)
            cl = _dot(tri_c, hi) + _dot(tri_c, mid) + _dot(tri_c, lo)
            cl_end = cl[C - 1:C, :]
            k = k_ref[b, sl, :]
            bv = b_ref[b, sl, :]
            d_inv = jnp.exp(-cl)
            d_tail = jnp.exp(cl_end - cl)
            rt = (r_ref[b, sl, :] * jnp.exp(cl)).astype(BF16)
            rt_s[b, sl, :] = rt
            kt = k * d_inv
            bt = bv * d_inv
            at = a_ref[b, sl, :] * jnp.exp(cl - ld)
            kd = k * d_tail
            bdk = bv * d_tail
            d_end = jnp.exp(cl_end)
            for p, cs in enumerate(pair_cols):
                bkt_s[b, c, p] = jnp.concatenate([bdk[:, cs], kd[:, cs]], axis=0).T.astype(BF16)
                dcol_s[b, c, p] = jnp.broadcast_to(d_end[:, cs], (HEAD_PAIR, HEAD_PAIR)).T
                where.append((b, sl, cs))
                ats.append(at[:, cs])
                lhs.append(jnp.concatenate([rt[:, cs], at[:, cs].astype(BF16)], axis=0))
                rhs.append(jnp.concatenate([bd(kt[:, cs]), bd(bt[:, cs])], axis=0))
        n = len(where)
        amat = [_dot_nt(lhs[i], rhs[i]) for i in range(n)]
        l_ak, l_ab = [], []
        for i, (b, sl, cs) in enumerate(where):
            ark_s[b, sl, cs] = jnp.where(incl, amat[i][:C, :HEAD_PAIR], 0.0).astype(BF16)
            arb_s[b, sl, cs] = jnp.where(incl, amat[i][:C, HEAD_PAIR:], 0.0).astype(BF16)
            l_ak.append(jnp.where(strict, amat[i][C:, :HEAD_PAIR], 0.0))
            l_ab.append(jnp.where(strict, amat[i][C:, HEAD_PAIR:], 0.0))

        t_mat = [eye_pair + l for l in l_ab]
        q = [_dot(l.astype(BF16), bd(l)) for l in l_ab]
        n_sq = 1
        while 2 * n_sq < C // 2:
            both = [_dot(q[i].astype(BF16), jnp.concatenate([bd(q[i]), bd(t_mat[i])], axis=1))
                    for i in range(n)]
            q = [x[:, :HEAD_PAIR] for x in both]
            t_mat = [t_mat[i] + both[i][:, HEAD_PAIR:] for i in range(n)]
            n_sq *= 2
        t_mat = [t_mat[i] + _dot(q[i].astype(BF16), bd(t_mat[i])) for i in range(n)]
        tal = [_dot(t_mat[i].astype(BF16), jnp.concatenate([bd(ats[i]), bd(l_ak[i])], axis=1))
               for i in range(n)]
        for i, (b, sl, cs) in enumerate(where):
            ta_s[b, sl, cs] = tal[i][:, :HEAD_PAIR].astype(BF16)
            tl_s[b, sl, cs] = tal[i][:, HEAD_PAIR:].astype(BF16)
        return carry

    lax.fori_loop(0, bb * n_chunks // WKV_PREP_GROUP, prep, 0)

    def step(c, carry):
        sl = pl.ds(pl.multiple_of(c * C, C), C)
        chains = [(b, p) for b in range(bb) for p in range(n_pairs)]
        v = [v_ref[b, sl, :] for b in range(bb)]
        st = [st_ref[b, p] for b, p in chains]
        st_b = [x.astype(BF16) for x in st]
        bd_v = [bd(v[b][:, pair_cols[p]]) for b, p in chains]
        on_st = [_dot(jnp.concatenate([ta_s[b, sl, pair_cols[p]], rt_s[b, sl, pair_cols[p]]],
                                      axis=0), st_b[i]) for i, (b, p) in enumerate(chains)]
        on_v = [_dot(jnp.concatenate([tl_s[b, sl, pair_cols[p]], ark_s[b, sl, pair_cols[p]]],
                                     axis=0), bd_v[i]) for i, (b, p) in enumerate(chains)]
        u = [on_st[i][:C] + on_v[i][:C] for i in range(len(chains))]
        for i, (b, p) in enumerate(chains):
            uv = jnp.concatenate([u[i], v[b][:, pair_cols[p]]], axis=0).astype(BF16)
            st_ref[b, p] = (dcol_s[b, c, p] * st[i]
                            + jnp.where(bd_mask, _dot(bkt_s[b, c, p], uv), 0.0))
        ys = [on_st[i][C:] + on_v[i][C:] + _dot(arb_s[b, sl, pair_cols[p]], bd(u[i]))
              for i, (b, p) in enumerate(chains)]
        for b in range(bb):
            o_ref[b, sl, :] = jnp.concatenate(ys[b * n_pairs:(b + 1) * n_pairs], axis=1)
        return carry

    lax.fori_loop(0, n_chunks, step, 0)

    def finish(it, carry):
        items = [(b, pl.ds(pl.multiple_of((it * WKV_NORM_GROUP + j) * C, C), C))
                 for j in range(WKV_NORM_GROUP) for b in range(bb)]
        y = [o_ref[b, sl, :] for b, sl in items]
        rkr = [r_ref[b, sl, :] * k_ref[b, sl, :] * rk_ref[...] for b, sl in items]
        sums = [_pair_head_sums([y[i], rkr[i]], ones_pair) for i in range(len(items))]
        yc = [y[i] - sums[i][0] * (1.0 / A_HEAD) for i in range(len(items))]
        var = [_pair_head_sums([x * x], ones_pair)[0] * (1.0 / A_HEAD) for x in yc]
        for i, (b, sl) in enumerate(items):
            yn = yc[i] * lax.rsqrt(var[i] + LNX_EPS) * lng_ref[...] + lnb_ref[...]
            o_ref[b, sl, :] = (yn + sums[i][1] * v_ref[b, sl, :]) * g_ref[b, sl, :]
        return carry

    lax.fori_loop(0, n_chunks // WKV_NORM_GROUP, finish, 0)


def _post_kernel(x_ref, ya_ref, ga_ref, ybg_ref, woa_ref, wout_ref, g2_ref, wr_hi_ref,
                 wr_lo_ref, br_ref, x2_out, hl_out, route_out, wts_out, cnt_out):
    y_a = _dot(ya_ref[...].astype(BF16), woa_ref[...])
    mix = ga_ref[...] * y_a + ybg_ref[...]
    x2 = x_ref[...] + _dot(mix.astype(BF16), wout_ref[...])
    x2_out[...] = x2
    h2 = _rms(x2, g2_ref[...])

    hi, lo = _split2(h2)
    wr_hi = wr_hi_ref[...]
    logits = _dot(hi, wr_hi) + _dot(lo, wr_hi) + _dot(hi, wr_lo_ref[...]) + br_ref[...]
    lane = lax.broadcasted_iota(jnp.int32, logits.shape, 1)
    neg = jnp.float32(-jnp.inf)
    big = jnp.int32(LANES)
    is_grp = (lane >= N_EXPERTS) & (lane < N_EXPERTS + N_GROUPS)
    gl = jnp.where(is_grp, logits, neg)
    gmax = jnp.max(gl, axis=-1, keepdims=True)
    g_p = 1.0 / jnp.sum(jnp.exp(gl - gmax), axis=-1, keepdims=True)
    g_idx = jnp.min(jnp.where(gl == gmax, lane, big), axis=-1, keepdims=True) - N_EXPERTS
    lo_lane = g_idx * EXPERTS_PER_GROUP
    in_grp = (lane >= lo_lane) & (lane < lo_lane + EXPERTS_PER_GROUP)
    el = jnp.where(in_grp, logits, neg)
    e1 = jnp.max(el, axis=-1, keepdims=True)
    i1 = jnp.min(jnp.where(el == e1, lane, big), axis=-1, keepdims=True)
    el2 = jnp.where(lane == i1, neg, el)
    e2 = jnp.max(el2, axis=-1, keepdims=True)
    i2 = jnp.min(jnp.where(el2 == e2, lane, big), axis=-1, keepdims=True)
    t = jnp.exp(e2 - e1)
    w1 = g_p / (1.0 + t)
    w2 = g_p * t / (1.0 + t)
    wts_out[...] = jnp.where(lane == 0, w1, jnp.where(lane == 1, w2, 0.0))

    tm = logits.shape[0]
    n_sub = tm // MOE_SUB
    onehot = jnp.where((lane == i1) | (lane == i2), 1.0, 0.0)
    tri = (lax.broadcasted_iota(jnp.int32, (tm, tm), 0)
           >= lax.broadcasted_iota(jnp.int32, (tm, tm), 1)).astype(BF16)
    incl = _dot(tri, onehot.astype(BF16))
    row = lax.broadcasted_iota(jnp.int32, logits.shape, 0)
    row8 = lax.broadcasted_iota(jnp.int32, (8, LANES), 0)
    ends = [incl[(g + 1) * MOE_SUB - 1:(g + 1) * MOE_SUB, :] for g in range(n_sub)]
    before = incl - onehot
    chunks = jnp.zeros((8, LANES), F32)
    for g, end in enumerate(ends):
        start = ends[g - 1] if g else jnp.zeros_like(end)
        before = before - jnp.where((row >= g * MOE_SUB) & (row < (g + 1) * MOE_SUB), start, 0.0)
        chunks = jnp.where(row8 == g, jnp.ceil((end - start) * (1.0 / RUN_ALIGN)), chunks)
    cnt_out[...] = chunks
    upper = (lax.broadcasted_iota(jnp.int32, (LANES, LANES), 0)
             < lax.broadcasted_iota(jnp.int32, (LANES, LANES), 1)).astype(BF16)
    run_start = _dot(chunks.astype(BF16), upper) * RUN_ALIGN
    start_rows = jnp.zeros(logits.shape, F32)
    for g in range(n_sub):
        start_rows = jnp.where((row >= g * MOE_SUB) & (row < (g + 1) * MOE_SUB),
                               run_start[g:g + 1, :], start_rows)
    slot1 = jnp.sum(jnp.where(lane == i1, start_rows + before, 0.0), axis=-1, keepdims=True)
    slot2 = jnp.sum(jnp.where(lane == i2, start_rows + before, 0.0), axis=-1, keepdims=True)
    route = jnp.where(lane == 0, slot1, jnp.where(lane == 1, slot2, 0.0)).astype(jnp.int32)
    route_out[...] = route

    route_t = route.T
    h2b = h2.astype(BF16)
    slot_ids = lax.broadcasted_iota(jnp.int32, (RUN_SLOTS, MOE_SUB), 0)
    for g in range(n_sub):
        cols = slice(g * MOE_SUB, (g + 1) * MOE_SUB)
        pick = jnp.where((slot_ids == route_t[0:1, cols]) | (slot_ids == route_t[1:2, cols]),
                         1.0, 0.0).astype(BF16)
        hl_out[g * RUN_SLOTS:(g + 1) * RUN_SLOTS, :] = _dot(pick, h2b[cols, :]).astype(BF16)


def _expert_kernel(nch_ref, loc_ref, hl_in_ref, wg_ref, wu_ref, wd_ref, hl_ref, xbuf, wgu_s, wd_s,
                   sem):
    del hl_in_ref
    e = pl.program_id(0)
    n_groups = hl_ref.shape[0] // RUN_SLOTS

    @pl.when(e == 0)
    def _():
        xbuf[...] = jnp.zeros_like(xbuf)

    wgu_s[:, :D_EXPERT] = wg_ref[0].astype(BF16)
    wgu_s[:, D_EXPERT:] = wu_ref[0].astype(BF16)
    wd_s[...] = wd_ref[0].astype(BF16)

    def run_rows(g):
        return nch_ref[g * N_EXPERTS + e] * RUN_ALIGN

    def for_chunks(g_lo, g_hi, fn):
        def per_group(g, pos):
            base = g * RUN_SLOTS + loc_ref[g * N_EXPERTS + e]

            def per_chunk(j, carry):
                off = j * RUN_ALIGN
                fn(hl_ref.at[pl.ds(pl.multiple_of(base + off, RUN_ALIGN), RUN_ALIGN)],
                   xbuf.at[pl.ds(pl.multiple_of(pos + off, RUN_ALIGN), RUN_ALIGN)])
                return carry

            lax.fori_loop(0, nch_ref[g * N_EXPERTS + e], per_chunk, 0)
            return pos + run_rows(g)

        return lax.fori_loop(g_lo, g_hi, per_group, 0)

    def one_pass(g_lo):
        g_hi, _ = lax.while_loop(
            lambda st: (st[0] < n_groups) & (st[1] + run_rows(jnp.minimum(st[0], n_groups - 1))
                                             <= EXPERT_ROWS),
            lambda st: (st[0] + 1, st[1] + run_rows(st[0])), (g_lo, 0))
        rows = for_chunks(g_lo, g_hi,
                          lambda hbm, buf: pltpu.make_async_copy(hbm, buf, sem.at[0]).start())
        for_chunks(g_lo, g_hi, lambda hbm, buf: pltpu.make_async_copy(hbm, buf, sem.at[0]).wait())

        def tile(k, carry):
            sl = pl.ds(pl.multiple_of(k * EXPERT_TILE, EXPERT_TILE), EXPERT_TILE)
            gu = _dot(xbuf[sl, :], wgu_s[...])
            gate = gu[:, :D_EXPERT]
            act = gate * jax.nn.sigmoid(gate) * gu[:, D_EXPERT:]
            xbuf[sl, :] = _dot(act.astype(BF16), wd_s[...]).astype(BF16)
            return carry

        lax.fori_loop(0, (rows + EXPERT_TILE - 1) // EXPERT_TILE, tile, 0)
        for_chunks(g_lo, g_hi,
                   lambda hbm, buf: pltpu.make_async_copy(buf, hbm, sem.at[1]).start())
        for_chunks(g_lo, g_hi, lambda hbm, buf: pltpu.make_async_copy(buf, hbm, sem.at[1]).wait())
        return g_hi

    lax.while_loop(lambda g: g < n_groups, one_pass, 0)


def _combine_kernel(x2_ref, route_ref, wts_ref, gf_ref, yl_ref, o_ref):
    route = route_ref[...]
    wts = wts_ref[...]
    slots = lax.broadcasted_iota(jnp.int32, (x2_ref.shape[0], RUN_SLOTS), 1)
    pick = (jnp.where(slots == route[:, 0:1], wts[:, 0:1], 0.0)
            + jnp.where(slots == route[:, 1:2], wts[:, 1:2], 0.0))
    y = _dot(pick.astype(BF16), yl_ref[...])
    o_ref[...] = _rms(x2_ref[...] + y, gf_ref[...])


def _full(shape):
    return pl.BlockSpec(shape, lambda *_: (0,) * len(shape))


def _params(sem):
    return pltpu.CompilerParams(dimension_semantics=sem, vmem_limit_bytes=VMEM_LIMIT)


def _mixer_a_layout(t):
    s_rkv = 3 * A_WIDTH
    s_w = s_rkv + D_DECAY_LORA
    s_a = s_w + D_AAA_LORA
    gap = lambda n: jnp.zeros((t.shape[0], n), t.dtype)
    return jnp.concatenate(
        [t[:, :s_w], gap(XA_OFF - D_DECAY_LORA), t[:, s_w:s_a], gap(XG_OFF - XA_OFF - D_AAA_LORA),
         t[:, s_a:], gap(LORA_PAD - XG_OFF - D_GATE_LORA)], axis=1)


def _place(cols, parts):
    out = jnp.zeros((parts[0][1].shape[0], cols), parts[0][1].dtype)
    for off, arr in parts:
        out = lax.dynamic_update_slice(out, arr, (0, off))
    return out


def kernel(x, norm1_g, w_in, b_gate, tmix_mu, w0, w2, a0, a2, g2, k_k, k_a, r_k, lnx_g, lnx_b,
           w_oA, lnv_g, lnv_b, w_s, b_s, w_oB, w_out, norm2_g, w_rg, b_rg, w_re, b_re,
           w_e_gate, w_e_up, w_e_down, final_g):
    bsz, seq, d = x.shape
    n_tok = bsz * seq
    depth = norm1_g.shape[0]
    assert depth == 1, "the moe kernel fuses the final norm, so it must be the last layer"
    assert bsz % WKV_SEQS == 0 and seq % WKV_TOKENS == 0
    xf = x.reshape(n_tok, d)

    s_rkv = 3 * A_WIDTH
    s_w = s_rkv + D_DECAY_LORA
    s_a = s_w + D_AAA_LORA
    a_cols = s_a + D_GATE_LORA
    b_cols = 2 * B_WIDTH

    ones_bd = (jnp.arange(A_WIDTH)[:, None] // A_HEAD
               == jnp.arange(A_WIDTH)[None, :] // A_HEAD).astype(BF16)

    tm_a = 512
    tm_b = 512
    tm_p = 512

    for l in range(depth):
        wl = w_in[l]
        w_a = _mixer_a_layout(wl[:, :a_cols]).astype(BF16)
        mu_a = _mixer_a_layout(tmix_mu[l][None, :])
        w2p = jnp.pad(w2[l], ((0, LANES - D_DECAY_LORA), (0, 0))).astype(BF16)
        a2p = jnp.pad(a2[l], ((0, LANES - D_AAA_LORA), (0, 0))).astype(BF16)
        g2p = jnp.pad(g2[l], ((0, LORA_PAD - XG_OFF - D_GATE_LORA), (0, 0))).astype(BF16)
        g1 = norm1_g[l][None, :]

        row512 = lambda i: (i, 0)
        tok_a = pl.BlockSpec((tm_a, A_WIDTH), row512)
        vec_a = _full((1, A_WIDTH))
        outs = pl.pallas_call(
            functools.partial(_rwkv_prep_kernel, seq // tm_a),
            grid=(n_tok // tm_a,),
            in_specs=[pl.BlockSpec((tm_a, d), row512),
                      _full((1, d)), _full((d, A_PROJ)), _full((1, A_PROJ)), vec_a,
                      _full((LANES, A_WIDTH)), vec_a, _full((LANES, A_WIDTH)),
                      _full((LORA_PAD - XG_OFF, A_WIDTH)), vec_a, vec_a,
                      _full((A_WIDTH, A_WIDTH))],
            out_specs=[tok_a] * 7,
            out_shape=[jax.ShapeDtypeStruct((n_tok, A_WIDTH), F32)] * 7,
            scratch_shapes=[pltpu.VMEM((8, A_PROJ), F32)],
            compiler_params=_params(("arbitrary",)),
            name="rwkv_prep",
        )(xf, g1, w_a, mu_a, w0[l][None, :], w2p, a0[l][None, :], a2p, g2p,
          k_k[l][None, :], k_a[l][None, :], ones_bd)
        r_, k_, v_, a_, b_, ld_, gg_ = outs

        w_b = wl[:, a_cols:a_cols + b_cols].astype(BF16)
        w_g = wl[:, a_cols + b_cols:].astype(BF16)
        bs_full = jnp.repeat(b_s[l].T, B_GROUP_CH, axis=1)
        tok_d = pl.BlockSpec((tm_b, d), row512)
        ybg, ga = pl.pallas_call(
            _gmlp_kernel,
            grid=(n_tok // tm_b,),
            in_specs=[tok_d, _full((1, d)), _full((d, b_cols)), _full((d, 2 * d)),
                      _full((1, 2 * d)), _full((1, B_WIDTH)), _full((1, B_WIDTH)),
                      _full((B_GROUPS, GMLP_BLOCK, GMLP_BLOCK)), _full((GMLP_BLOCK, B_WIDTH)),
                      _full((B_WIDTH, d))],
            out_specs=[tok_d, tok_d],
            out_shape=[jax.ShapeDtypeStruct((n_tok, d), F32)] * 2,
            compiler_params=_params(("parallel",)),
            name="gmlp",
        )(xf, g1, w_b, w_g, b_gate[l][None, :], lnv_g[l][None, :], lnv_b[l][None, :],
          w_s[l], bs_full, w_oB[l].astype(BF16))

        n_pairs = A_WIDTH // HEAD_PAIR
        n_chunks = WKV_TOKENS // WKV_CHUNK
        tok_w = pl.BlockSpec((WKV_SEQS, WKV_TOKENS, A_WIDTH), lambda bi, ti: (bi, ti, 0))
        vec_w = _full((1, A_WIDTH))
        seq_major = lambda t: t.reshape(bsz, seq, A_WIDTH)
        tok_scratch = pltpu.VMEM((WKV_SEQS, WKV_TOKENS, A_WIDTH), BF16)
        ya_in = pl.pallas_call(
            _wkv_kernel,
            grid=(bsz // WKV_SEQS, seq // WKV_TOKENS),
            in_specs=[tok_w] * 7 + [vec_w, vec_w, vec_w, _full((HEAD_PAIR, HEAD_PAIR))],
            out_specs=tok_w,
            out_shape=jax.ShapeDtypeStruct((bsz, seq, A_WIDTH), F32),
            scratch_shapes=[pltpu.VMEM((WKV_SEQS, n_pairs, HEAD_PAIR, HEAD_PAIR), F32)]
            + [tok_scratch] * 5
            + [pltpu.VMEM((WKV_SEQS, n_chunks, n_pairs, HEAD_PAIR, HEAD_PAIR), BF16),
               pltpu.VMEM((WKV_SEQS, n_chunks, n_pairs, HEAD_PAIR, HEAD_PAIR), F32)],
            compiler_params=_params(("parallel", "arbitrary")),
            name="wkv",
        )(*[seq_major(t) for t in (r_, k_, v_, a_, b_, ld_, gg_)], r_k[l].reshape(1, A_WIDTH),
          lnx_g[l][None, :], lnx_b[l][None, :],
          ones_bd[:HEAD_PAIR, :HEAD_PAIR]).reshape(n_tok, A_WIDTH)

        w_r = _place(LANES, [(0, jnp.transpose(w_re[l], (1, 0, 2)).reshape(d, N_EXPERTS)),
                             (N_EXPERTS, w_rg[l])])
        b_r = _place(LANES, [(0, b_re[l].reshape(1, N_EXPERTS)), (N_EXPERTS, b_rg[l][None, :])])
        wr_hi = w_r.astype(BF16)
        wr_lo = (w_r - wr_hi.astype(F32)).astype(BF16)
        tok_p = pl.BlockSpec((tm_p, d), row512)
        lane_p = pl.BlockSpec((tm_p, LANES), row512)
        n_sub = n_tok // MOE_SUB
        sorted_rows = tm_p // MOE_SUB * RUN_SLOTS
        x2, h_sorted, route, wts, cnt = pl.pallas_call(
            _post_kernel,
            grid=(n_tok // tm_p,),
            in_specs=[tok_p, pl.BlockSpec((tm_p, A_WIDTH), row512), tok_p, tok_p,
                      _full((A_WIDTH, d)), _full((d, d)), _full((1, d)), _full((d, LANES)),
                      _full((d, LANES)), _full((1, LANES))],
            out_specs=[tok_p, pl.BlockSpec((sorted_rows, d), row512), lane_p, lane_p,
                       pl.BlockSpec((8, LANES), row512)],
            out_shape=[jax.ShapeDtypeStruct((n_tok, d), F32),
                       jax.ShapeDtypeStruct((n_sub * RUN_SLOTS, d), BF16),
                       jax.ShapeDtypeStruct((n_tok, LANES), jnp.int32),
                       jax.ShapeDtypeStruct((n_tok, LANES), F32),
                       jax.ShapeDtypeStruct((n_tok // tm_p * 8, LANES), F32)],
            compiler_params=_params(("parallel",)),
            name="post",
        )(xf, ya_in, ga, ybg, w_oA[l].astype(BF16), w_out[l].astype(BF16),
          norm2_g[l][None, :], wr_hi, wr_lo, b_r)

        run_chunks = cnt.reshape(n_tok // tm_p, 8, LANES)[:, :tm_p // MOE_SUB, :N_EXPERTS]
        run_chunks = run_chunks.reshape(n_sub, N_EXPERTS).astype(jnp.int32)
        run_start = (jnp.cumsum(run_chunks, axis=1) - run_chunks) * RUN_ALIGN

        per_expert = lambda e, *_: (e, 0, 0)
        y_sorted = pl.pallas_call(
            _expert_kernel,
            grid_spec=pltpu.PrefetchScalarGridSpec(
                num_scalar_prefetch=2, grid=(N_EXPERTS,),
                in_specs=[pl.BlockSpec(memory_space=pl.ANY),
                          pl.BlockSpec((1, d, D_EXPERT), per_expert),
                          pl.BlockSpec((1, d, D_EXPERT), per_expert),
                          pl.BlockSpec((1, D_EXPERT, d), per_expert)],
                out_specs=pl.BlockSpec(memory_space=pl.ANY),
                scratch_shapes=[pltpu.VMEM((EXPERT_ROWS, d), BF16),
                                pltpu.VMEM((d, 2 * D_EXPERT), BF16),
                                pltpu.VMEM((D_EXPERT, d), BF16),
                                pltpu.SemaphoreType.DMA((2,))]),
            out_shape=jax.ShapeDtypeStruct((n_sub * RUN_SLOTS, d), BF16),
            input_output_aliases={2: 0},
            compiler_params=_params(("arbitrary",)),
            name="moe_experts",
        )(run_chunks.reshape(-1), run_start.reshape(-1), h_sorted, w_e_gate[l], w_e_up[l],
          w_e_down[l])

        sub_rows = lambda i: (i, 0)
        xf = pl.pallas_call(
            _combine_kernel,
            grid=(n_sub,),
            in_specs=[pl.BlockSpec((MOE_SUB, d), sub_rows),
                      pl.BlockSpec((MOE_SUB, LANES), sub_rows),
                      pl.BlockSpec((MOE_SUB, LANES), sub_rows),
                      _full((1, d)),
                      pl.BlockSpec((RUN_SLOTS, d), sub_rows)],
            out_specs=pl.BlockSpec((MOE_SUB, d), sub_rows),
            out_shape=jax.ShapeDtypeStruct((n_tok, d), F32),
            compiler_params=_params(("parallel",)),
            name="moe_combine",
        )(x2, route, wts, final_g[None, :], y_sorted)

    return xf.reshape(bsz, seq, d)
```

```python
import functools

import jax
import jax.numpy as jnp
from jax import lax
from jax.experimental import pallas as pl
from jax.experimental.pallas import tpu as pltpu

F32 = jnp.float32
BF16 = jnp.bfloat16

D_MODEL = 1024
A_WIDTH = 512
A_HEAD = 64
D_DECAY_LORA = 64
D_AAA_LORA = 64
D_GATE_LORA = 160
B_WIDTH = 512
B_GROUPS = 4
B_GROUP_CH = 128
GMLP_BLOCK = 128
N_GROUPS = 4
EXPERTS_PER_GROUP = 8
N_EXPERTS = 32
D_EXPERT = 256
NORM_EPS = 1e-6
LN_EPS = 1e-5
LNX_EPS = 64e-5

LANES = 128
LORA_PAD = 512
XW_OFF, XA_OFF, XG_OFF = 0, 128, 256
A_PROJ = 3 * A_WIDTH + LORA_PAD
WKV_CHUNK = 64
HEAD_PAIR = 2 * A_HEAD
WKV_SEQS = 2
WKV_TOKENS = 256
WKV_PREP_GROUP = 4
WKV_NORM_GROUP = 2
MOE_SUB = 256
RUN_ALIGN = 16
RUN_SLOTS = 2 * MOE_SUB + N_EXPERTS * RUN_ALIGN
EXPERT_ROWS = 2048
EXPERT_TILE = 256
VMEM_LIMIT = 48 * 1024 * 1024


def _rms(x, g):
    return x * lax.rsqrt(jnp.mean(x * x, axis=-1, keepdims=True) + NORM_EPS) * g


def _dot(a, b):
    return jnp.dot(a, b, preferred_element_type=F32)


def _dot_nt(a, b):
    return lax.dot_general(a, b, (((1,), (1,)), ((), ())), preferred_element_type=F32)


def _split2(x):
    hi = x.astype(BF16)
    lo = (x - hi.astype(F32)).astype(BF16)
    return hi, lo


def _split3(x):
    hi = x.astype(BF16)
    r1 = x - hi.astype(F32)
    mid = r1.astype(BF16)
    lo = (r1 - mid.astype(F32)).astype(BF16)
    return hi, mid, lo


def _head_sum(x, ones_bd):
    hi, lo = _split2(x)
    return _dot(hi, ones_bd) + _dot(lo, ones_bd)


def _pair_head_sums(xs, ones_pair):
    n_tiles = A_WIDTH // HEAD_PAIR
    rows = xs[0].shape[0]
    tiles = [x[:, t * HEAD_PAIR:(t + 1) * HEAD_PAIR].astype(BF16) for x in xs for t in range(n_tiles)]
    sums = _dot(jnp.concatenate(tiles, axis=0), ones_pair)
    return [jnp.concatenate([sums[(i * n_tiles + t) * rows:(i * n_tiles + t + 1) * rows]
                             for t in range(n_tiles)], axis=1) for i in range(len(xs))]


def _rwkv_prep_kernel(tiles_per_seq, x_ref, g1_ref, wa_ref, mu_ref, w0_ref,
                      w2_ref, a0_ref, a2_ref, g2_ref, kk_ref, ka_ref, ones_ref,
                      r_out, k_out, v_out, a_out, b_out, ld_out, g_out, tail_ref):
    i = pl.program_id(0)
    h = _rms(x_ref[...], g1_ref[...]).astype(BF16)
    p = _dot(h, wa_ref[...])
    tm = p.shape[0]
    prev = jnp.where(i % tiles_per_seq == 0, 0.0, tail_ref[7:8, :])
    tail_ref[...] = p[tm - 8:, :]
    row = lax.broadcasted_iota(jnp.int32, p.shape, 0)
    shifted = jnp.where(row == 0, prev, pltpu.roll(p, 1, axis=0))
    pm = p + mu_ref[...] * (shifted - p)

    r = pm[:, 0:A_WIDTH]
    k = pm[:, A_WIDTH:2 * A_WIDTH]
    v = pm[:, 2 * A_WIDTH:3 * A_WIDTH]
    lora = pm[:, 3 * A_WIDTH:]
    xw = lora[:, XW_OFF:XW_OFF + LANES]
    xa = lora[:, XA_OFF:XA_OFF + LANES]
    xg = lora[:, XG_OFF:]

    z = -(w0_ref[...] + _dot(jnp.tanh(xw).astype(BF16), w2_ref[...]))
    softplus = jnp.maximum(z, 0.0) + jnp.log(1.0 + jnp.exp(-jnp.abs(z)))
    w = -softplus - 0.5
    a_lr = jax.nn.sigmoid(a0_ref[...] + _dot(xa.astype(BF16), a2_ref[...]))
    g = _dot(jax.nn.sigmoid(xg).astype(BF16), g2_ref[...])

    kk = k * kk_ref[...]
    norm = jnp.sqrt(_head_sum(kk * kk, ones_ref[...]))
    kk = kk / jnp.maximum(norm, 1e-12)

    r_out[...] = r
    k_out[...] = k * (1.0 + (a_lr - 1.0) * ka_ref[...])
    v_out[...] = v
    a_out[...] = -kk
    b_out[...] = kk * a_lr
    ld_out[...] = -jnp.exp(w)
    g_out[...] = g


def _gmlp_kernel(x_ref, g1_ref, wb_ref, wg_ref, bg_ref, lng_ref, lnb_ref, ws_ref, bs_ref,
                 wo_ref, ybg_out, ga_out):
    tm = x_ref.shape[0]
    h = _rms(x_ref[...], g1_ref[...]).astype(BF16)
    pb = _dot(h, wb_ref[...])
    z = 0.5 * pb * (1.0 + lax.erf(pb * (2.0 ** -0.5)))
    u = z[:, :B_WIDTH]
    v = z[:, B_WIDTH:]
    mean = jnp.mean(v, axis=-1, keepdims=True)
    vc = v - mean
    var = jnp.mean(vc * vc, axis=-1, keepdims=True)
    vn = (vc * lax.rsqrt(var + LN_EPS) * lng_ref[...] + lnb_ref[...]).astype(BF16)

    tri = (lax.broadcasted_iota(jnp.int32, (GMLP_BLOCK, GMLP_BLOCK), 0)
           >= lax.broadcasted_iota(jnp.int32, (GMLP_BLOCK, GMLP_BLOCK), 1))
    bs = bs_ref[...]
    rows = []
    for blk in range(tm // GMLP_BLOCK):
        cols = []
        for grp in range(B_GROUPS):
            ws = jnp.where(tri, ws_ref[grp], 0.0).astype(BF16)
            vb = vn[blk * GMLP_BLOCK:(blk + 1) * GMLP_BLOCK,
                    grp * B_GROUP_CH:(grp + 1) * B_GROUP_CH]
            cols.append(_dot(ws, vb))
        rows.append(jnp.concatenate(cols, axis=1) + bs)
    sv = jnp.concatenate(rows, axis=0)
    yb = _dot((u * sv).astype(BF16), wo_ref[...])
    gates = jax.nn.sigmoid(_dot(h, wg_ref[...]) + bg_ref[...])
    ga_out[...] = gates[:, :D_MODEL]
    ybg_out[...] = gates[:, D_MODEL:] * yb


def _wkv_kernel(r_ref, k_ref, v_ref, a_ref, b_ref, ld_ref, g_ref, rk_ref, lng_ref, lnb_ref,
                ones_ref, o_ref, st_ref, ta_s, tl_s, arb_s, ark_s, rt_s, bkt_s, dcol_s):
    C = WKV_CHUNK
    bb, tb, _ = r_ref.shape
    n_chunks = tb // C
    n_pairs = A_WIDTH // HEAD_PAIR

    @pl.when(pl.program_id(1) == 0)
    def _():
        st_ref[...] = jnp.zeros_like(st_ref)

    row = lax.broadcasted_iota(jnp.int32, (C, HEAD_PAIR), 0)
    src = lax.broadcasted_iota(jnp.int32, (C, HEAD_PAIR), 1) & (C - 1)
    incl = src <= row
    strict = src < row
    eye_pair = jnp.where(src == row, 1.0, 0.0)
    bd_mask = ((lax.broadcasted_iota(jnp.int32, (HEAD_PAIR, HEAD_PAIR), 0) >= A_HEAD)
               == (lax.broadcasted_iota(jnp.int32, (HEAD_PAIR, HEAD_PAIR), 1) >= A_HEAD))
    tri_c = (lax.broadcasted_iota(jnp.int32, (C, C), 0)
             >= lax.broadcasted_iota(jnp.int32, (C, C), 1)).astype(BF16)
    ones_pair = ones_ref[...]
    pair_cols = [slice(p * HEAD_PAIR, (p + 1) * HEAD_PAIR) for p in range(n_pairs)]

    def bd(x):
        xb = x.astype(BF16)
        return jnp.where(bd_mask, jnp.concatenate([xb, xb], axis=0), jnp.zeros((), BF16))

    def prep(it, carry):
        where, lhs, rhs, ats = [], [], [], []
        for j in range(WKV_PREP_GROUP):
            flat = it * WKV_PREP_GROUP + j
            b = flat // n_chunks
            c = flat % n_chunks
            sl = pl.ds(pl.multiple_of(c * C, C), C)
            ld = ld_ref[b, sl, :]
            hi, mid, lo = _split3(ld)
            cl = _dot(tri_c, hi) + _dot(tri_c, mid) + _dot(tri_c, lo)
            cl_end = cl[C - 1:C, :]
            k = k_ref[b, sl, :]
            bv = b_ref[b, sl, :]
            d_inv = jnp.exp(-cl)
            d_tail = jnp.exp(cl_end - cl)
            rt = (r_ref[b, sl, :] * jnp.exp(cl)).astype(BF16)
            rt_s[b, sl, :] = rt
            kt = k * d_inv
            bt = bv * d_inv
            at = a_ref[b, sl, :] * jnp.exp(cl - ld)
            kd = k * d_tail
            bdk = bv * d_tail
            d_end = jnp.exp(cl_end)
            for p, cs in enumerate(pair_cols):
                bkt_s[b, c, p] = jnp.concatenate([bdk[:, cs], kd[:, cs]], axis=0).T.astype(BF16)
                dcol_s[b, c, p] = jnp.broadcast_to(d_end[:, cs], (HEAD_PAIR, HEAD_PAIR)).T
                where.append((b, sl, cs))
                ats.append(at[:, cs])
                lhs.append(jnp.concatenate([rt[:, cs], at[:, cs].astype(BF16)], axis=0))
                rhs.append(jnp.concatenate([bd(kt[:, cs]), bd(bt[:, cs])], axis=0))
        n = len(where)
        amat = [_dot_nt(lhs[i], rhs[i]) for i in range(n)]
        l_ak, l_ab = [], []
        for i, (b, sl, cs) in enumerate(where):
            ark_s[b, sl, cs] = jnp.where(incl, amat[i][:C, :HEAD_PAIR], 0.0).astype(BF16)
            arb_s[b, sl, cs] = jnp.where(incl, amat[i][:C, HEAD_PAIR:], 0.0).astype(BF16)
            l_ak.append(jnp.where(strict, amat[i][C:, :HEAD_PAIR], 0.0))
            l_ab.append(jnp.where(strict, amat[i][C:, HEAD_PAIR:], 0.0))

        t_mat = [eye_pair + l for l in l_ab]
        q = [_dot(l.astype(BF16), bd(l)) for l in l_ab]
        n_sq = 1
        while 2 * n_sq < C // 2:
            both = [_dot(q[i].astype(BF16), jnp.concatenate([bd(q[i]), bd(t_mat[i])], axis=1))
                    for i in range(n)]
            q = [x[:, :HEAD_PAIR] for x in both]
            t_mat = [t_mat[i] + both[i][:, HEAD_PAIR:] for i in range(n)]
            n_sq *= 2
        t_mat = [t_mat[i] + _dot(q[i].astype(BF16), bd(t_mat[i])) for i in range(n)]
        tal = [_dot(t_mat[i].astype(BF16), jnp.concatenate([bd(ats[i]), bd(l_ak[i])], axis=1))
               for i in range(n)]
        for i, (b, sl, cs) in enumerate(where):
            ta_s[b, sl, cs] = tal[i][:, :HEAD_PAIR].astype(BF16)
            tl_s[b, sl, cs] = tal[i][:, HEAD_PAIR:].astype(BF16)
        return carry

    lax.fori_loop(0, bb * n_chunks // WKV_PREP_GROUP, prep, 0)

    def step(c, carry):
        sl = pl.ds(pl.multiple_of(c * C, C), C)
        chains = [(b, p) for b in range(bb) for p in range(n_pairs)]
        v = [v_ref[b, sl, :] for b in range(bb)]
        st = [st_ref[b, p] for b, p in chains]
        st_b = [x.astype(BF16) for x in st]
        bd_v = [bd(v[b][:, pair_cols[p]]) for b, p in chains]
        on_st = [_dot(jnp.concatenate([ta_s[b, sl, pair_cols[p]], rt_s[b, sl, pair_cols[p]]],
                                      axis=0), st_b[i]) for i, (b, p) in enumerate(chains)]
        on_v = [_dot(jnp.concatenate([tl_s[b, sl, pair_cols[p]], ark_s[b, sl, pair_cols[p]]],
                                     axis=0), bd_v[i]) for i, (b, p) in enumerate(chains)]
        u = [on_st[i][:C] + on_v[i][:C] for i in range(len(chains))]
        for i, (b, p) in enumerate(chains):
            uv = jnp.concatenate([u[i], v[b][:, pair_cols[p]]], axis=0).astype(BF16)
            st_ref[b, p] = (dcol_s[b, c, p] * st[i]
                            + jnp.where(bd_mask, _dot(bkt_s[b, c, p], uv), 0.0))
        ys = [on_st[i][C:] + on_v[i][C:] + _dot(arb_s[b, sl, pair_cols[p]], bd(u[i]))
              for i, (b, p) in enumerate(chains)]
        for b in range(bb):
            o_ref[b, sl, :] = jnp.concatenate(ys[b * n_pairs:(b + 1) * n_pairs], axis=1)
        return carry

    lax.fori_loop(0, n_chunks, step, 0)

    def finish(it, carry):
        items = [(b, pl.ds(pl.multiple_of((it * WKV_NORM_GROUP + j) * C, C), C))
                 for j in range(WKV_NORM_GROUP) for b in range(bb)]
        y = [o_ref[b, sl, :] for b, sl in items]
        rkr = [r_ref[b, sl, :] * k_ref[b, sl, :] * rk_ref[...] for b, sl in items]
        sums = [_pair_head_sums([y[i], rkr[i]], ones_pair) for i in range(len(items))]
        yc = [y[i] - sums[i][0] * (1.0 / A_HEAD) for i in range(len(items))]
        var = [_pair_head_sums([x * x], ones_pair)[0] * (1.0 / A_HEAD) for x in yc]
        for i, (b, sl) in enumerate(items):
            yn = yc[i] * lax.rsqrt(var[i] + LNX_EPS) * lng_ref[...] + lnb_ref[...]
            o_ref[b, sl, :] = (yn + sums[i][1] * v_ref[b, sl, :]) * g_ref[b, sl, :]
        return carry

    lax.fori_loop(0, n_chunks // WKV_NORM_GROUP, finish, 0)


def _post_kernel(x_ref, ya_ref, ga_ref, ybg_ref, woa_ref, wout_ref, g2_ref, wr_hi_ref,
                 wr_lo_ref, br_ref, x2_out, hl_out, route_out, wts_out, cnt_out):
    y_a = _dot(ya_ref[...].astype(BF16), woa_ref[...])
    mix = ga_ref[...] * y_a + ybg_ref[...]
    x2 = x_ref[...] + _dot(mix.astype(BF16), wout_ref[...])
    x2_out[...] = x2
    h2 = _rms(x2, g2_ref[...])

    hi, lo = _split2(h2)
    wr_hi = wr_hi_ref[...]
    logits = _dot(hi, wr_hi) + _dot(lo, wr_hi) + _dot(hi, wr_lo_ref[...]) + br_ref[...]
    lane = lax.broadcasted_iota(jnp.int32, logits.shape, 1)
    neg = jnp.float32(-jnp.inf)
    big = jnp.int32(LANES)
    is_grp = (lane >= N_EXPERTS) & (lane < N_EXPERTS + N_GROUPS)
    gl = jnp.where(is_grp, logits, neg)
    gmax = jnp.max(gl, axis=-1, keepdims=True)
    g_p = 1.0 / jnp.sum(jnp.exp(gl - gmax), axis=-1, keepdims=True)
    g_idx = jnp.min(jnp.where(gl == gmax, lane, big), axis=-1, keepdims=True) - N_EXPERTS
    lo_lane = g_idx * EXPERTS_PER_GROUP
    in_grp = (lane >= lo_lane) & (lane < lo_lane + EXPERTS_PER_GROUP)
    el = jnp.where(in_grp, logits, neg)
    e1 = jnp.max(el, axis=-1, keepdims=True)
    i1 = jnp.min(jnp.where(el == e1, lane, big), axis=-1, keepdims=True)
    el2 = jnp.where(lane == i1, neg, el)
    e2 = jnp.max(el2, axis=-1, keepdims=True)
    i2 = jnp.min(jnp.where(el2 == e2, lane, big), axis=-1, keepdims=True)
    t = jnp.exp(e2 - e1)
    w1 = g_p / (1.0 + t)
    w2 = g_p * t / (1.0 + t)
    wts_out[...] = jnp.where(lane == 0, w1, jnp.where(lane == 1, w2, 0.0))

    tm = logits.shape[0]
    n_sub = tm // MOE_SUB
    onehot = jnp.where((lane == i1) | (lane == i2), 1.0, 0.0)
    tri = (lax.broadcasted_iota(jnp.int32, (tm, tm), 0)
           >= lax.broadcasted_iota(jnp.int32, (tm, tm), 1)).astype(BF16)
    incl = _dot(tri, onehot.astype(BF16))
    row = lax.broadcasted_iota(jnp.int32, logits.shape, 0)
    row8 = lax.broadcasted_iota(jnp.int32, (8, LANES), 0)
    ends = [incl[(g + 1) * MOE_SUB - 1:(g + 1) * MOE_SUB, :] for g in range(n_sub)]
    before = incl - onehot
    chunks = jnp.zeros((8, LANES), F32)
    for g, end in enumerate(ends):
        start = ends[g - 1] if g else jnp.zeros_like(end)
        before = before - jnp.where((row >= g * MOE_SUB) & (row < (g + 1) * MOE_SUB), start, 0.0)
        chunks = jnp.where(row8 == g, jnp.ceil((end - start) * (1.0 / RUN_ALIGN)), chunks)
    cnt_out[...] = chunks
    upper = (lax.broadcasted_iota(jnp.int32, (LANES, LANES), 0)
             < lax.broadcasted_iota(jnp.int32, (LANES, LANES), 1)).astype(BF16)
    run_start = _dot(chunks.astype(BF16), upper) * RUN_ALIGN
    start_rows = jnp.zeros(logits.shape, F32)
    for g in range(n_sub):
        start_rows = jnp.where((row >= g * MOE_SUB) & (row < (g + 1) * MOE_SUB),
                               run_start[g:g + 1, :], start_rows)
    slot1 = jnp.sum(jnp.where(lane == i1, start_rows + before, 0.0), axis=-1, keepdims=True)
    slot2 = jnp.sum(jnp.where(lane == i2, start_rows + before, 0.0), axis=-1, keepdims=True)
    route = jnp.where(lane == 0, slot1, jnp.where(lane == 1, slot2, 0.0)).astype(jnp.int32)
    route_out[...] = route

    route_t = route.T
    h2b = h2.astype(BF16)
    slot_ids = lax.broadcasted_iota(jnp.int32, (RUN_SLOTS, MOE_SUB), 0)
    for g in range(n_sub):
        cols = slice(g * MOE_SUB, (g + 1) * MOE_SUB)
        pick = jnp.where((slot_ids == route_t[0:1, cols]) | (slot_ids == route_t[1:2, cols]),
                         1.0, 0.0).astype(BF16)
        hl_out[g * RUN_SLOTS:(g + 1) * RUN_SLOTS, :] = _dot(pick, h2b[cols, :]).astype(BF16)


def _expert_kernel(nch_ref, loc_ref, hl_in_ref, wg_ref, wu_ref, wd_ref, hl_ref, xbuf, wgu_s, wd_s,
                   sem, carry_s):
    del hl_in_ref
    e = pl.program_id(0)
    n_groups = hl_ref.shape[0] // RUN_SLOTS
    mine = e % 2
    other = 1 - mine
    gather_sem = lambda buf: sem.at[buf]
    scatter_sem = lambda buf: sem.at[2 + buf]

    def run_rows(ex, g):
        return nch_ref[g * N_EXPERTS + ex] * RUN_ALIGN

    def pass_end(ex, g_lo):
        return lax.while_loop(
            lambda st: (st[0] < n_groups)
            & (st[1] + run_rows(ex, jnp.minimum(st[0], n_groups - 1)) <= EXPERT_ROWS),
            lambda st: (st[0] + 1, st[1] + run_rows(ex, st[0])), (g_lo, 0))[0]

    def for_chunks(ex, buf, g_lo, g_hi, fn):
        def per_group(g, pos):
            base = g * RUN_SLOTS + loc_ref[g * N_EXPERTS + ex]

            def per_chunk(j, carry):
                off = j * RUN_ALIGN
                fn(hl_ref.at[pl.ds(pl.multiple_of(base + off, RUN_ALIGN), RUN_ALIGN)],
                   xbuf.at[buf, pl.ds(pl.multiple_of(pos + off, RUN_ALIGN), RUN_ALIGN)])
                return carry

            lax.fori_loop(0, nch_ref[g * N_EXPERTS + ex], per_chunk, 0)
            return pos + run_rows(ex, g)

        return lax.fori_loop(g_lo, g_hi, per_group, 0)

    def gather(ex, buf, g_lo, g_hi):
        return for_chunks(ex, buf, g_lo, g_hi, lambda hbm, vm: pltpu.make_async_copy(
            hbm, vm, gather_sem(buf)).start())

    def scatter(ex, buf, g_lo, g_hi):
        for_chunks(ex, buf, g_lo, g_hi, lambda hbm, vm: pltpu.make_async_copy(
            vm, hbm, scatter_sem(buf)).start())

    def wait_rows(which_sem, rows):
        chunk = pl.ds(0, RUN_ALIGN)

        def body(c, carry):
            pltpu.make_async_copy(hl_ref.at[chunk], xbuf.at[0, chunk], which_sem).wait()
            return carry

        lax.fori_loop(0, rows // RUN_ALIGN, body, 0)

    def evaluate(buf, rows):
        def tile(k, carry):
            sl = pl.ds(pl.multiple_of(k * EXPERT_TILE, EXPERT_TILE), EXPERT_TILE)
            gu = _dot(xbuf[buf, sl, :], wgu_s[...])
            gate = gu[:, :D_EXPERT]
            act = gate * jax.nn.sigmoid(gate) * gu[:, D_EXPERT:]
            xbuf[buf, sl, :] = _dot(act.astype(BF16), wd_s[...]).astype(BF16)
            return carry

        lax.fori_loop(0, (rows + EXPERT_TILE - 1) // EXPERT_TILE, tile, 0)

    @pl.when(e == 0)
    def _():
        xbuf[...] = jnp.zeros_like(xbuf)
        g_first = pass_end(0, 0)
        carry_s[0] = g_first
        carry_s[1] = gather(0, 0, 0, g_first)
        carry_s[2] = 0

    wgu_s[:, :D_EXPERT] = wg_ref[0].astype(BF16)
    wgu_s[:, D_EXPERT:] = wu_ref[0].astype(BF16)
    wd_s[...] = wd_ref[0].astype(BF16)

    g_first = carry_s[0]
    rows = carry_s[1]
    wait_rows(gather_sem(mine), rows)
    evaluate(mine, rows)
    wait_rows(scatter_sem(other), carry_s[2])
    scatter(e, mine, 0, g_first)

    def later_pass(g_lo):
        g_hi = pass_end(e, g_lo)
        n = gather(e, other, g_lo, g_hi)
        wait_rows(gather_sem(other), n)
        evaluate(other, n)
        scatter(e, other, g_lo, g_hi)
        wait_rows(scatter_sem(other), n)
        return g_hi

    lax.while_loop(lambda g: g < n_groups, later_pass, g_first)
    carry_s[2] = rows

    @pl.when(e + 1 < pl.num_programs(0))
    def _():
        g_next = pass_end(e + 1, 0)
        carry_s[0] = g_next
        carry_s[1] = gather(e + 1, other, 0, g_next)

    @pl.when(e + 1 == pl.num_programs(0))
    def _():
        wait_rows(scatter_sem(mine), rows)


def _combine_kernel(x2_ref, route_ref, wts_ref, gf_ref, yl_ref, o_ref):
    route = route_ref[...]
    wts = wts_ref[...]
    slots = lax.broadcasted_iota(jnp.int32, (x2_ref.shape[0], RUN_SLOTS), 1)
    pick = (jnp.where(slots == route[:, 0:1], wts[:, 0:1], 0.0)
            + jnp.where(slots == route[:, 1:2], wts[:, 1:2], 0.0))
    y = _dot(pick.astype(BF16), yl_ref[...])
    o_ref[...] = _rms(x2_ref[...] + y, gf_ref[...])


def _full(shape):
    return pl.BlockSpec(shape, lambda *_: (0,) * len(shape))


def _params(sem):
    return pltpu.CompilerParams(dimension_semantics=sem, vmem_limit_bytes=VMEM_LIMIT)


def _mixer_a_layout(t):
    s_rkv = 3 * A_WIDTH
    s_w = s_rkv + D_DECAY_LORA
    s_a = s_w + D_AAA_LORA
    gap = lambda n: jnp.zeros((t.shape[0], n), t.dtype)
    return jnp.concatenate(
        [t[:, :s_w], gap(XA_OFF - D_DECAY_LORA), t[:, s_w:s_a], gap(XG_OFF - XA_OFF - D_AAA_LORA),
         t[:, s_a:], gap(LORA_PAD - XG_OFF - D_GATE_LORA)], axis=1)


def _place(cols, parts):
    out = jnp.zeros((parts[0][1].shape[0], cols), parts[0][1].dtype)
    for off, arr in parts:
        out = lax.dynamic_update_slice(out, arr, (0, off))
    return out


def kernel(x, norm1_g, w_in, b_gate, tmix_mu, w0, w2, a0, a2, g2, k_k, k_a, r_k, lnx_g, lnx_b,
           w_oA, lnv_g, lnv_b, w_s, b_s, w_oB, w_out, norm2_g, w_rg, b_rg, w_re, b_re,
           w_e_gate, w_e_up, w_e_down, final_g):
    bsz, seq, d = x.shape
    n_tok = bsz * seq
    depth = norm1_g.shape[0]
    assert depth == 1, "the moe kernel fuses the final norm, so it must be the last layer"
    assert bsz % WKV_SEQS == 0 and seq % WKV_TOKENS == 0
    xf = x.reshape(n_tok, d)

    s_rkv = 3 * A_WIDTH
    s_w = s_rkv + D_DECAY_LORA
    s_a = s_w + D_AAA_LORA
    a_cols = s_a + D_GATE_LORA
    b_cols = 2 * B_WIDTH

    ones_bd = (jnp.arange(A_WIDTH)[:, None] // A_HEAD
               == jnp.arange(A_WIDTH)[None, :] // A_HEAD).astype(BF16)

    tm_a = 512
    tm_b = 512
    tm_p = 512

    for l in range(depth):
        wl = w_in[l]
        w_a = _mixer_a_layout(wl[:, :a_cols]).astype(BF16)
        mu_a = _mixer_a_layout(tmix_mu[l][None, :])
        w2p = jnp.pad(w2[l], ((0, LANES - D_DECAY_LORA), (0, 0))).astype(BF16)
        a2p = jnp.pad(a2[l], ((0, LANES - D_AAA_LORA), (0, 0))).astype(BF16)
        g2p = jnp.pad(g2[l], ((0, LORA_PAD - XG_OFF - D_GATE_LORA), (0, 0))).astype(BF16)
        g1 = norm1_g[l][None, :]

        row512 = lambda i: (i, 0)
        tok_a = pl.BlockSpec((tm_a, A_WIDTH), row512)
        vec_a = _full((1, A_WIDTH))
        outs = pl.pallas_call(
            functools.partial(_rwkv_prep_kernel, seq // tm_a),
            grid=(n_tok // tm_a,),
            in_specs=[pl.BlockSpec((tm_a, d), row512),
                      _full((1, d)), _full((d, A_PROJ)), _full((1, A_PROJ)), vec_a,
                      _full((LANES, A_WIDTH)), vec_a, _full((LANES, A_WIDTH)),
                      _full((LORA_PAD - XG_OFF, A_WIDTH)), vec_a, vec_a,
                      _full((A_WIDTH, A_WIDTH))],
            out_specs=[tok_a] * 7,
            out_shape=[jax.ShapeDtypeStruct((n_tok, A_WIDTH), F32)] * 7,
            scratch_shapes=[pltpu.VMEM((8, A_PROJ), F32)],
            compiler_params=_params(("arbitrary",)),
            name="rwkv_prep",
        )(xf, g1, w_a, mu_a, w0[l][None, :], w2p, a0[l][None, :], a2p, g2p,
          k_k[l][None, :], k_a[l][None, :], ones_bd)
        r_, k_, v_, a_, b_, ld_, gg_ = outs

        w_b = wl[:, a_cols:a_cols + b_cols].astype(BF16)
        w_g = wl[:, a_cols + b_cols:].astype(BF16)
        bs_full = jnp.repeat(b_s[l].T, B_GROUP_CH, axis=1)
        tok_d = pl.BlockSpec((tm_b, d), row512)
        ybg, ga = pl.pallas_call(
            _gmlp_kernel,
            grid=(n_tok // tm_b,),
            in_specs=[tok_d, _full((1, d)), _full((d, b_cols)), _full((d, 2 * d)),
                      _full((1, 2 * d)), _full((1, B_WIDTH)), _full((1, B_WIDTH)),
                      _full((B_GROUPS, GMLP_BLOCK, GMLP_BLOCK)), _full((GMLP_BLOCK, B_WIDTH)),
                      _full((B_WIDTH, d))],
            out_specs=[tok_d, tok_d],
            out_shape=[jax.ShapeDtypeStruct((n_tok, d), F32)] * 2,
            compiler_params=_params(("parallel",)),
            name="gmlp",
        )(xf, g1, w_b, w_g, b_gate[l][None, :], lnv_g[l][None, :], lnv_b[l][None, :],
          w_s[l], bs_full, w_oB[l].astype(BF16))

        n_pairs = A_WIDTH // HEAD_PAIR
        n_chunks = WKV_TOKENS // WKV_CHUNK
        tok_w = pl.BlockSpec((WKV_SEQS, WKV_TOKENS, A_WIDTH), lambda bi, ti: (bi, ti, 0))
        vec_w = _full((1, A_WIDTH))
        seq_major = lambda t: t.reshape(bsz, seq, A_WIDTH)
        tok_scratch = pltpu.VMEM((WKV_SEQS, WKV_TOKENS, A_WIDTH), BF16)
        ya_in = pl.pallas_call(
            _wkv_kernel,
            grid=(bsz // WKV_SEQS, seq // WKV_TOKENS),
            in_specs=[tok_w] * 7 + [vec_w, vec_w, vec_w, _full((HEAD_PAIR, HEAD_PAIR))],
            out_specs=tok_w,
            out_shape=jax.ShapeDtypeStruct((bsz, seq, A_WIDTH), F32),
            scratch_shapes=[pltpu.VMEM((WKV_SEQS, n_pairs, HEAD_PAIR, HEAD_PAIR), F32)]
            + [tok_scratch] * 5
            + [pltpu.VMEM((WKV_SEQS, n_chunks, n_pairs, HEAD_PAIR, HEAD_PAIR), BF16),
               pltpu.VMEM((WKV_SEQS, n_chunks, n_pairs, HEAD_PAIR, HEAD_PAIR), F32)],
            compiler_params=_params(("parallel", "arbitrary")),
            name="wkv",
        )(*[seq_major(t) for t in (r_, k_, v_, a_, b_, ld_, gg_)], r_k[l].reshape(1, A_WIDTH),
          lnx_g[l][None, :], lnx_b[l][None, :],
          ones_bd[:HEAD_PAIR, :HEAD_PAIR]).reshape(n_tok, A_WIDTH)

        w_r = _place(LANES, [(0, jnp.transpose(w_re[l], (1, 0, 2)).reshape(d, N_EXPERTS)),
                             (N_EXPERTS, w_rg[l])])
        b_r = _place(LANES, [(0, b_re[l].reshape(1, N_EXPERTS)), (N_EXPERTS, b_rg[l][None, :])])
        wr_hi = w_r.astype(BF16)
        wr_lo = (w_r - wr_hi.astype(F32)).astype(BF16)
        tok_p = pl.BlockSpec((tm_p, d), row512)
        lane_p = pl.BlockSpec((tm_p, LANES), row512)
        n_sub = n_tok // MOE_SUB
        sorted_rows = tm_p // MOE_SUB * RUN_SLOTS
        x2, h_sorted, route, wts, cnt = pl.pallas_call(
            _post_kernel,
            grid=(n_tok // tm_p,),
            in_specs=[tok_p, pl.BlockSpec((tm_p, A_WIDTH), row512), tok_p, tok_p,
                      _full((A_WIDTH, d)), _full((d, d)), _full((1, d)), _full((d, LANES)),
                      _full((d, LANES)), _full((1, LANES))],
            out_specs=[tok_p, pl.BlockSpec((sorted_rows, d), row512), lane_p, lane_p,
                       pl.BlockSpec((8, LANES), row512)],
            out_shape=[jax.ShapeDtypeStruct((n_tok, d), F32),
                       jax.ShapeDtypeStruct((n_sub * RUN_SLOTS, d), BF16),
                       jax.ShapeDtypeStruct((n_tok, LANES), jnp.int32),
                       jax.ShapeDtypeStruct((n_tok, LANES), F32),
                       jax.ShapeDtypeStruct((n_tok // tm_p * 8, LANES), F32)],
            compiler_params=_params(("parallel",)),
            name="post",
        )(xf, ya_in, ga, ybg, w_oA[l].astype(BF16), w_out[l].astype(BF16),
          norm2_g[l][None, :], wr_hi, wr_lo, b_r)

        run_chunks = cnt.reshape(n_tok // tm_p, 8, LANES)[:, :tm_p // MOE_SUB, :N_EXPERTS]
        run_chunks = run_chunks.reshape(n_sub, N_EXPERTS).astype(jnp.int32)
        run_start = (jnp.cumsum(run_chunks, axis=1) - run_chunks) * RUN_ALIGN

        per_expert = lambda e, *_: (e, 0, 0)
        y_sorted = pl.pallas_call(
            _expert_kernel,
            grid_spec=pltpu.PrefetchScalarGridSpec(
                num_scalar_prefetch=2, grid=(N_EXPERTS,),
                in_specs=[pl.BlockSpec(memory_space=pl.ANY),
                          pl.BlockSpec((1, d, D_EXPERT), per_expert),
                          pl.BlockSpec((1, d, D_EXPERT), per_expert),
                          pl.BlockSpec((1, D_EXPERT, d), per_expert)],
                out_specs=pl.BlockSpec(memory_space=pl.ANY),
                scratch_shapes=[pltpu.VMEM((2, EXPERT_ROWS, d), BF16),
                                pltpu.VMEM((d, 2 * D_EXPERT), BF16),
                                pltpu.VMEM((D_EXPERT, d), BF16),
                                pltpu.SemaphoreType.DMA((4,)),
                                pltpu.SMEM((3,), jnp.int32)]),
            out_shape=jax.ShapeDtypeStruct((n_sub * RUN_SLOTS, d), BF16),
            input_output_aliases={2: 0},
            compiler_params=_params(("arbitrary",)),
            name="moe_experts",
        )(run_chunks.reshape(-1), run_start.reshape(-1), h_sorted, w_e_gate[l], w_e_up[l],
          w_e_down[l])

        sub_rows = lambda i: (i, 0)
        xf = pl.pallas_call(
            _combine_kernel,
            grid=(n_sub,),
            in_specs=[pl.BlockSpec((MOE_SUB, d), sub_rows),
                      pl.BlockSpec((MOE_SUB, LANES), sub_rows),
                      pl.BlockSpec((MOE_SUB, LANES), sub_rows),
                      _full((1, d)),
                      pl.BlockSpec((RUN_SLOTS, d), sub_rows)],
            out_specs=pl.BlockSpec((MOE_SUB, d), sub_rows),
            out_shape=jax.ShapeDtypeStruct((n_tok, d), F32),
            compiler_params=_params(("parallel",)),
            name="moe_combine",
        )(x2, route, wts, final_g[None, :], y_sorted)

    return xf.reshape(bsz, seq, d)
```

```python
import functools

import jax
import jax.numpy as jnp
from jax import lax
from jax.experimental import pallas as pl
from jax.experimental.pallas import tpu as pltpu

F32 = jnp.float32
BF16 = jnp.bfloat16

D_MODEL = 1024
A_WIDTH = 512
A_HEAD = 64
D_DECAY_LORA = 64
D_AAA_LORA = 64
D_GATE_LORA = 160
B_WIDTH = 512
B_GROUPS = 4
B_GROUP_CH = 128
GMLP_BLOCK = 128
N_GROUPS = 4
EXPERTS_PER_GROUP = 8
N_EXPERTS = 32
D_EXPERT = 256
NORM_EPS = 1e-6
LN_EPS = 1e-5
LNX_EPS = 64e-5

LANES = 128
LORA_PAD = 512
XW_OFF, XA_OFF, XG_OFF = 0, 128, 256
A_PROJ = 3 * A_WIDTH + LORA_PAD
WKV_CHUNK = 64
HEAD_PAIR = 2 * A_HEAD
WKV_SEQS = 4
WKV_TOKENS = 128
WKV_PREP_GROUP = 4
WKV_NORM_GROUP = 2
MOE_SUB = 256
RUN_ALIGN = 16
RUN_SLOTS = 2 * MOE_SUB + N_EXPERTS * RUN_ALIGN
EXPERT_ROWS = 2048
EXPERT_TILE = 256
VMEM_LIMIT = 48 * 1024 * 1024


def _rms(x, g):
    return x * lax.rsqrt(jnp.mean(x * x, axis=-1, keepdims=True) + NORM_EPS) * g


def _dot(a, b):
    return jnp.dot(a, b, preferred_element_type=F32)


def _dot_nt(a, b):
    return lax.dot_general(a, b, (((1,), (1,)), ((), ())), preferred_element_type=F32)


def _split2(x):
    hi = x.astype(BF16)
    lo = (x - hi.astype(F32)).astype(BF16)
    return hi, lo


def _split3(x):
    hi = x.astype(BF16)
    r1 = x - hi.astype(F32)
    mid = r1.astype(BF16)
    lo = (r1 - mid.astype(F32)).astype(BF16)
    return hi, mid, lo


def _head_sum(x, ones_bd):
    hi, lo = _split2(x)
    return _dot(hi, ones_bd) + _dot(lo, ones_bd)


def _pair_head_sums(xs, ones_pair):
    n_tiles = A_WIDTH // HEAD_PAIR
    rows = xs[0].shape[0]
    tiles = [x[:, t * HEAD_PAIR:(t + 1) * HEAD_PAIR].astype(BF16) for x in xs for t in range(n_tiles)]
    sums = _dot(jnp.concatenate(tiles, axis=0), ones_pair)
    return [jnp.concatenate([sums[(i * n_tiles + t) * rows:(i * n_tiles + t + 1) * rows]
                             for t in range(n_tiles)], axis=1) for i in range(len(xs))]


def _rwkv_prep_kernel(tiles_per_seq, x_ref, g1_ref, wa_ref, mu_ref, w0_ref,
                      w2_ref, a0_ref, a2_ref, g2_ref, kk_ref, ka_ref, ones_ref,
                      r_out, k_out, v_out, a_out, b_out, ld_out, g_out, tail_ref):
    i = pl.program_id(0)
    h = _rms(x_ref[...], g1_ref[...]).astype(BF16)
    p = _dot(h, wa_ref[...])
    tm = p.shape[0]
    prev = jnp.where(i % tiles_per_seq == 0, 0.0, tail_ref[7:8, :])
    tail_ref[...] = p[tm - 8:, :]
    row = lax.broadcasted_iota(jnp.int32, p.shape, 0)
    shifted = jnp.where(row == 0, prev, pltpu.roll(p, 1, axis=0))
    pm = p + mu_ref[...] * (shifted - p)

    r = pm[:, 0:A_WIDTH]
    k = pm[:, A_WIDTH:2 * A_WIDTH]
    v = pm[:, 2 * A_WIDTH:3 * A_WIDTH]
    lora = pm[:, 3 * A_WIDTH:]
    xw = lora[:, XW_OFF:XW_OFF + LANES]
    xa = lora[:, XA_OFF:XA_OFF + LANES]
    xg = lora[:, XG_OFF:]

    z = -(w0_ref[...] + _dot(jnp.tanh(xw).astype(BF16), w2_ref[...]))
    softplus = jnp.maximum(z, 0.0) + jnp.log(1.0 + jnp.exp(-jnp.abs(z)))
    w = -softplus - 0.5
    a_lr = jax.nn.sigmoid(a0_ref[...] + _dot(xa.astype(BF16), a2_ref[...]))
    g = _dot(jax.nn.sigmoid(xg).astype(BF16), g2_ref[...])

    kk = k * kk_ref[...]
    norm = jnp.sqrt(_head_sum(kk * kk, ones_ref[...]))
    kk = kk / jnp.maximum(norm, 1e-12)

    r_out[...] = r
    k_out[...] = k * (1.0 + (a_lr - 1.0) * ka_ref[...])
    v_out[...] = v
    a_out[...] = -kk
    b_out[...] = kk * a_lr
    ld_out[...] = -jnp.exp(w)
    g_out[...] = g


def _gmlp_kernel(x_ref, g1_ref, wb_ref, wg_ref, bg_ref, lng_ref, lnb_ref, ws_ref, bs_ref,
                 wo_ref, ybg_out, ga_out):
    tm = x_ref.shape[0]
    h = _rms(x_ref[...], g1_ref[...]).astype(BF16)
    pb = _dot(h, wb_ref[...])
    z = 0.5 * pb * (1.0 + lax.erf(pb * (2.0 ** -0.5)))
    u = z[:, :B_WIDTH]
    v = z[:, B_WIDTH:]
    mean = jnp.mean(v, axis=-1, keepdims=True)
    vc = v - mean
    var = jnp.mean(vc * vc, axis=-1, keepdims=True)
    vn = (vc * lax.rsqrt(var + LN_EPS) * lng_ref[...] + lnb_ref[...]).astype(BF16)

    tri = (lax.broadcasted_iota(jnp.int32, (GMLP_BLOCK, GMLP_BLOCK), 0)
           >= lax.broadcasted_iota(jnp.int32, (GMLP_BLOCK, GMLP_BLOCK), 1))
    bs = bs_ref[...]
    rows = []
    for blk in range(tm // GMLP_BLOCK):
        cols = []
        for grp in range(B_GROUPS):
            ws = jnp.where(tri, ws_ref[grp], 0.0).astype(BF16)
            vb = vn[blk * GMLP_BLOCK:(blk + 1) * GMLP_BLOCK,
                    grp * B_GROUP_CH:(grp + 1) * B_GROUP_CH]
            cols.append(_dot(ws, vb))
        rows.append(jnp.concatenate(cols, axis=1) + bs)
    sv = jnp.concatenate(rows, axis=0)
    yb = _dot((u * sv).astype(BF16), wo_ref[...])
    gates = jax.nn.sigmoid(_dot(h, wg_ref[...]) + bg_ref[...])
    ga_out[...] = gates[:, :D_MODEL].astype(BF16)
    ybg_out[...] = (gates[:, D_MODEL:] * yb).astype(BF16)


def _wkv_kernel(r_ref, k_ref, v_ref, a_ref, b_ref, ld_ref, g_ref, rk_ref, lng_ref, lnb_ref,
                ones_ref, o_ref, st_ref, ta_s, tl_s, arb_s, ark_s, rt_s, bkt_s, dcol_s):
    C = WKV_CHUNK
    bb, tb, _ = r_ref.shape
    n_chunks = tb // C
    n_pairs = A_WIDTH // HEAD_PAIR

    @pl.when(pl.program_id(1) == 0)
    def _():
        st_ref[...] = jnp.zeros_like(st_ref)

    row = lax.broadcasted_iota(jnp.int32, (C, HEAD_PAIR), 0)
    src = lax.broadcasted_iota(jnp.int32, (C, HEAD_PAIR), 1) & (C - 1)
    incl = src <= row
    strict = src < row
    eye_pair = jnp.where(src == row, 1.0, 0.0)
    bd_mask = ((lax.broadcasted_iota(jnp.int32, (HEAD_PAIR, HEAD_PAIR), 0) >= A_HEAD)
               == (lax.broadcasted_iota(jnp.int32, (HEAD_PAIR, HEAD_PAIR), 1) >= A_HEAD))
    tri_c = (lax.broadcasted_iota(jnp.int32, (C, C), 0)
             >= lax.broadcasted_iota(jnp.int32, (C, C), 1)).astype(BF16)
    ones_pair = ones_ref[...]
    pair_cols = [slice(p * HEAD_PAIR, (p + 1) * HEAD_PAIR) for p in range(n_pairs)]

    def bd(x):
        xb = x.astype(BF16)
        return jnp.where(bd_mask, jnp.concatenate([xb, xb], axis=0), jnp.zeros((), BF16))

    def prep(it, carry):
        where, lhs, rhs, ats = [], [], [], []
        for j in range(WKV_PREP_GROUP):
            flat = it * WKV_PREP_GROUP + j
            b = flat // n_chunks
            c = flat % n_chunks
            sl = pl.ds(pl.multiple_of(c * C, C), C)
            ld = ld_ref[b, sl, :]
            hi, mid, lo = _split3(ld)
            cl = _dot(tri_c, hi) + _dot(tri_c, mid) + _dot(tri_c, lo)
            cl_end = cl[C - 1:C, :]
            k = k_ref[b, sl, :]
            bv = b_ref[b, sl, :]
            d_inv = jnp.exp(-cl)
            d_tail = jnp.exp(cl_end - cl)
            rt = (r_ref[b, sl, :] * jnp.exp(cl)).astype(BF16)
            rt_s[b, sl, :] = rt
            kt = k * d_inv
            bt = bv * d_inv
            at = a_ref[b, sl, :] * jnp.exp(cl - ld)
            kd = k * d_tail
            bdk = bv * d_tail
            d_end = jnp.exp(cl_end)
            for p, cs in enumerate(pair_cols):
                bkt_s[b, c, p] = jnp.concatenate([bdk[:, cs], kd[:, cs]], axis=0).T.astype(BF16)
                dcol_s[b, c, p] = jnp.broadcast_to(d_end[:, cs], (HEAD_PAIR, HEAD_PAIR)).T
                where.append((b, sl, cs))
                ats.append(at[:, cs])
                lhs.append(jnp.concatenate([rt[:, cs], at[:, cs].astype(BF16)], axis=0))
                rhs.append(jnp.concatenate([bd(kt[:, cs]), bd(bt[:, cs])], axis=0))
        n = len(where)
        amat = [_dot_nt(lhs[i], rhs[i]) for i in range(n)]
        l_ak, l_ab = [], []
        for i, (b, sl, cs) in enumerate(where):
            ark_s[b, sl, cs] = jnp.where(incl, amat[i][:C, :HEAD_PAIR], 0.0).astype(BF16)
            arb_s[b, sl, cs] = jnp.where(incl, amat[i][:C, HEAD_PAIR:], 0.0).astype(BF16)
            l_ak.append(jnp.where(strict, amat[i][C:, :HEAD_PAIR], 0.0))
            l_ab.append(jnp.where(strict, amat[i][C:, HEAD_PAIR:], 0.0))

        t_mat = [eye_pair + l for l in l_ab]
        q = [_dot(l.astype(BF16), bd(l)) for l in l_ab]
        n_sq = 1
        while 2 * n_sq < C // 2:
            both = [_dot(q[i].astype(BF16), jnp.concatenate([bd(q[i]), bd(t_mat[i])], axis=1))
                    for i in range(n)]
            q = [x[:, :HEAD_PAIR] for x in both]
            t_mat = [t_mat[i] + both[i][:, HEAD_PAIR:] for i in range(n)]
            n_sq *= 2
        t_mat = [t_mat[i] + _dot(q[i].astype(BF16), bd(t_mat[i])) for i in range(n)]
        tal = [_dot(t_mat[i].astype(BF16), jnp.concatenate([bd(ats[i]), bd(l_ak[i])], axis=1))
               for i in range(n)]
        for i, (b, sl, cs) in enumerate(where):
            ta_s[b, sl, cs] = tal[i][:, :HEAD_PAIR].astype(BF16)
            tl_s[b, sl, cs] = tal[i][:, HEAD_PAIR:].astype(BF16)
        return carry

    lax.fori_loop(0, bb * n_chunks // WKV_PREP_GROUP, prep, 0)

    def step(c, carry):
        sl = pl.ds(pl.multiple_of(c * C, C), C)
        chains = [(b, p) for b in range(bb) for p in range(n_pairs)]
        v = [v_ref[b, sl, :] for b in range(bb)]
        st = [st_ref[b, p] for b, p in chains]
        st_b = [x.astype(BF16) for x in st]
        bd_v = [bd(v[b][:, pair_cols[p]]) for b, p in chains]
        on_st = [_dot(jnp.concatenate([ta_s[b, sl, pair_cols[p]], rt_s[b, sl, pair_cols[p]]],
                                      axis=0), st_b[i]) for i, (b, p) in enumerate(chains)]
        on_v = [_dot(jnp.concatenate([tl_s[b, sl, pair_cols[p]], ark_s[b, sl, pair_cols[p]]],
                                     axis=0), bd_v[i]) for i, (b, p) in enumerate(chains)]
        u = [on_st[i][:C] + on_v[i][:C] for i in range(len(chains))]
        for i, (b, p) in enumerate(chains):
            uv = jnp.concatenate([u[i], v[b][:, pair_cols[p]]], axis=0).astype(BF16)
            st_ref[b, p] = (dcol_s[b, c, p] * st[i]
                            + jnp.where(bd_mask, _dot(bkt_s[b, c, p], uv), 0.0))
        ys = [on_st[i][C:] + on_v[i][C:] + _dot(arb_s[b, sl, pair_cols[p]], bd(u[i]))
              for i, (b, p) in enumerate(chains)]
        for b in range(bb):
            o_ref[b, sl, :] = jnp.concatenate(ys[b * n_pairs:(b + 1) * n_pairs], axis=1)
        return carry

    lax.fori_loop(0, n_chunks, step, 0)

    def finish(it, carry):
        items = [(b, pl.ds(pl.multiple_of((it * WKV_NORM_GROUP + j) * C, C), C))
                 for j in range(WKV_NORM_GROUP) for b in range(bb)]
        y = [o_ref[b, sl, :] for b, sl in items]
        rkr = [r_ref[b, sl, :] * k_ref[b, sl, :] * rk_ref[...] for b, sl in items]
        sums = [_pair_head_sums([y[i], rkr[i]], ones_pair) for i in range(len(items))]
        yc = [y[i] - sums[i][0] * (1.0 / A_HEAD) for i in range(len(items))]
        var = [_pair_head_sums([x * x], ones_pair)[0] * (1.0 / A_HEAD) for x in yc]
        for i, (b, sl) in enumerate(items):
            yn = yc[i] * lax.rsqrt(var[i] + LNX_EPS) * lng_ref[...] + lnb_ref[...]
            o_ref[b, sl, :] = (yn + sums[i][1] * v_ref[b, sl, :]) * g_ref[b, sl, :]
        return carry

    lax.fori_loop(0, n_chunks // WKV_NORM_GROUP, finish, 0)


def _post_kernel(x_ref, ya_ref, ga_ref, ybg_ref, woa_ref, wout_ref, g2_ref, wr_hi_ref,
                 wr_lo_ref, br_ref, x2_out, hl_out, route_out, wts_out, cnt_out):
    y_a = _dot(ya_ref[...].astype(BF16), woa_ref[...])
    mix = ga_ref[...] * y_a + ybg_ref[...]
    x2 = x_ref[...] + _dot(mix.astype(BF16), wout_ref[...])
    x2_out[...] = x2
    h2 = _rms(x2, g2_ref[...])

    hi, lo = _split2(h2)
    wr_hi = wr_hi_ref[...]
    logits = _dot(hi, wr_hi) + _dot(lo, wr_hi) + _dot(hi, wr_lo_ref[...]) + br_ref[...]
    lane = lax.broadcasted_iota(jnp.int32, logits.shape, 1)
    neg = jnp.float32(-jnp.inf)
    big = jnp.int32(LANES)
    is_grp = (lane >= N_EXPERTS) & (lane < N_EXPERTS + N_GROUPS)
    gl = jnp.where(is_grp, logits, neg)
    gmax = jnp.max(gl, axis=-1, keepdims=True)
    g_p = 1.0 / jnp.sum(jnp.exp(gl - gmax), axis=-1, keepdims=True)
    g_idx = jnp.min(jnp.where(gl == gmax, lane, big), axis=-1, keepdims=True) - N_EXPERTS
    lo_lane = g_idx * EXPERTS_PER_GROUP
    in_grp = (lane >= lo_lane) & (lane < lo_lane + EXPERTS_PER_GROUP)
    el = jnp.where(in_grp, logits, neg)
    e1 = jnp.max(el, axis=-1, keepdims=True)
    i1 = jnp.min(jnp.where(el == e1, lane, big), axis=-1, keepdims=True)
    el2 = jnp.where(lane == i1, neg, el)
    e2 = jnp.max(el2, axis=-1, keepdims=True)
    i2 = jnp.min(jnp.where(el2 == e2, lane, big), axis=-1, keepdims=True)
    t = jnp.exp(e2 - e1)
    w1 = g_p / (1.0 + t)
    w2 = g_p * t / (1.0 + t)
    wts_out[...] = jnp.where(lane == 0, w1, jnp.where(lane == 1, w2, 0.0))

    tm = logits.shape[0]
    n_sub = tm // MOE_SUB
    onehot = jnp.where((lane == i1) | (lane == i2), 1.0, 0.0)
    tri = (lax.broadcasted_iota(jnp.int32, (tm, tm), 0)
           >= lax.broadcasted_iota(jnp.int32, (tm, tm), 1)).astype(BF16)
    incl = _dot(tri, onehot.astype(BF16))
    row = lax.broadcasted_iota(jnp.int32, logits.shape, 0)
    row8 = lax.broadcasted_iota(jnp.int32, (8, LANES), 0)
    ends = [incl[(g + 1) * MOE_SUB - 1:(g + 1) * MOE_SUB, :] for g in range(n_sub)]
    before = incl - onehot
    chunks = jnp.zeros((8, LANES), F32)
    for g, end in enumerate(ends):
        start = ends[g - 1] if g else jnp.zeros_like(end)
        before = before - jnp.where((row >= g * MOE_SUB) & (row < (g + 1) * MOE_SUB), start, 0.0)
        chunks = jnp.where(row8 == g, jnp.ceil((end - start) * (1.0 / RUN_ALIGN)), chunks)
    cnt_out[...] = chunks
    upper = (lax.broadcasted_iota(jnp.int32, (LANES, LANES), 0)
             < lax.broadcasted_iota(jnp.int32, (LANES, LANES), 1)).astype(BF16)
    run_start = _dot(chunks.astype(BF16), upper) * RUN_ALIGN
    start_rows = jnp.zeros(logits.shape, F32)
    for g in range(n_sub):
        start_rows = jnp.where((row >= g * MOE_SUB) & (row < (g + 1) * MOE_SUB),
                               run_start[g:g + 1, :], start_rows)
    slot1 = jnp.sum(jnp.where(lane == i1, start_rows + before, 0.0), axis=-1, keepdims=True)
    slot2 = jnp.sum(jnp.where(lane == i2, start_rows + before, 0.0), axis=-1, keepdims=True)
    route = jnp.where(lane == 0, slot1, jnp.where(lane == 1, slot2, 0.0)).astype(jnp.int32)
    route_out[...] = route

    route_t = route.T
    h2b = h2.astype(BF16)
    slot_ids = lax.broadcasted_iota(jnp.int32, (RUN_SLOTS, MOE_SUB), 0)
    for g in range(n_sub):
        cols = slice(g * MOE_SUB, (g + 1) * MOE_SUB)
        pick = jnp.where((slot_ids == route_t[0:1, cols]) | (slot_ids == route_t[1:2, cols]),
                         1.0, 0.0).astype(BF16)
        hl_out[g * RUN_SLOTS:(g + 1) * RUN_SLOTS, :] = _dot(pick, h2b[cols, :]).astype(BF16)


def _expert_kernel(nch_ref, loc_ref, hl_in_ref, wg_ref, wu_ref, wd_ref, hl_ref, xbuf, wgu_s, wd_s,
                   sem, chunk_s, first_s, drain_s):
    del hl_in_ref
    e = pl.program_id(0)
    n_groups = hl_ref.shape[0] // RUN_SLOTS
    pass_chunks = EXPERT_ROWS // RUN_ALIGN
    mine = e % 2
    other = 1 - mine
    gather_sem = lambda buf: sem.at[buf]
    scatter_sem = lambda buf: sem.at[2 + buf]

    def list_chunks():
        def per_expert(ex, n):
            first_s[ex] = n

            def per_group(g, n):
                base = g * RUN_SLOTS + loc_ref[g * N_EXPERTS + ex]

                def per_chunk(j, n):
                    chunk_s[n] = base + j * RUN_ALIGN
                    return n + 1

                return lax.fori_loop(0, nch_ref[g * N_EXPERTS + ex], per_chunk, n)

            return lax.fori_loop(0, n_groups, per_group, n)

        first_s[N_EXPERTS] = lax.fori_loop(0, N_EXPERTS, per_expert, 0)

    def first_pass(ex):
        lo = first_s[ex]
        return lo, jnp.minimum(lo + pass_chunks, first_s[ex + 1])

    def for_chunks(buf, c_lo, c_hi, fn):
        def body(c, carry):
            fn(hl_ref.at[pl.ds(pl.multiple_of(chunk_s[c], RUN_ALIGN), RUN_ALIGN)],
               xbuf.at[buf, pl.ds(pl.multiple_of((c - c_lo) * RUN_ALIGN, RUN_ALIGN), RUN_ALIGN)])
            return carry

        lax.fori_loop(c_lo, c_hi, body, 0)

    def gather(buf, c_lo, c_hi):
        for_chunks(buf, c_lo, c_hi, lambda hbm, vm: pltpu.make_async_copy(
            hbm, vm, gather_sem(buf)).start())

    def scatter(buf, c_lo, c_hi):
        for_chunks(buf, c_lo, c_hi, lambda hbm, vm: pltpu.make_async_copy(
            vm, hbm, scatter_sem(buf)).start())

    def wait_chunks(which_sem, n):
        chunk = pl.ds(0, RUN_ALIGN)

        def body(c, carry):
            pltpu.make_async_copy(hl_ref.at[chunk], xbuf.at[0, chunk], which_sem).wait()
            return carry

        lax.fori_loop(0, n, body, 0)

    def evaluate(buf, n_chunks):
        def tile(k, carry):
            sl = pl.ds(pl.multiple_of(k * EXPERT_TILE, EXPERT_TILE), EXPERT_TILE)
            gu = _dot(xbuf[buf, sl, :], wgu_s[...])
            gate = gu[:, :D_EXPERT]
            act = gate * jax.nn.sigmoid(gate) * gu[:, D_EXPERT:]
            xbuf[buf, sl, :] = _dot(act.astype(BF16), wd_s[...]).astype(BF16)
            return carry

        lax.fori_loop(0, (n_chunks * RUN_ALIGN + EXPERT_TILE - 1) // EXPERT_TILE, tile, 0)

    @pl.when(e == 0)
    def _():
        xbuf[...] = jnp.zeros_like(xbuf)
        list_chunks()
        gather(0, *first_pass(0))
        drain_s[0] = 0

    wgu_s[:, :D_EXPERT] = wg_ref[0].astype(BF16)
    wgu_s[:, D_EXPERT:] = wu_ref[0].astype(BF16)
    wd_s[...] = wd_ref[0].astype(BF16)

    c_lo, c_hi = first_pass(e)
    wait_chunks(gather_sem(mine), c_hi - c_lo)
    evaluate(mine, c_hi - c_lo)
    wait_chunks(scatter_sem(other), drain_s[0])
    scatter(mine, c_lo, c_hi)

    def later_pass(lo):
        hi = jnp.minimum(lo + pass_chunks, first_s[e + 1])
        gather(other, lo, hi)
        wait_chunks(gather_sem(other), hi - lo)
        evaluate(other, hi - lo)
        scatter(other, lo, hi)
        wait_chunks(scatter_sem(other), hi - lo)
        return hi

    lax.while_loop(lambda c: c < first_s[e + 1], later_pass, c_hi)
    drain_s[0] = c_hi - c_lo

    @pl.when(e + 1 < pl.num_programs(0))
    def _():
        gather(other, *first_pass(e + 1))

    @pl.when(e + 1 == pl.num_programs(0))
    def _():
        wait_chunks(scatter_sem(mine), c_hi - c_lo)


def _combine_kernel(x2_ref, route_ref, wts_ref, gf_ref, yl_ref, o_ref):
    route = route_ref[...]
    wts = wts_ref[...]
    slots = lax.broadcasted_iota(jnp.int32, (x2_ref.shape[0], RUN_SLOTS), 1)
    pick = (jnp.where(slots == route[:, 0:1], wts[:, 0:1], 0.0)
            + jnp.where(slots == route[:, 1:2], wts[:, 1:2], 0.0))
    y = _dot(pick.astype(BF16), yl_ref[...])
    o_ref[...] = _rms(x2_ref[...] + y, gf_ref[...])


def _full(shape):
    return pl.BlockSpec(shape, lambda *_: (0,) * len(shape))


def _params(sem):
    return pltpu.CompilerParams(dimension_semantics=sem, vmem_limit_bytes=VMEM_LIMIT)


def _mixer_a_layout(t):
    s_rkv = 3 * A_WIDTH
    s_w = s_rkv + D_DECAY_LORA
    s_a = s_w + D_AAA_LORA
    gap = lambda n: jnp.zeros((t.shape[0], n), t.dtype)
    return jnp.concatenate(
        [t[:, :s_w], gap(XA_OFF - D_DECAY_LORA), t[:, s_w:s_a], gap(XG_OFF - XA_OFF - D_AAA_LORA),
         t[:, s_a:], gap(LORA_PAD - XG_OFF - D_GATE_LORA)], axis=1)


def _place(cols, parts):
    out = jnp.zeros((parts[0][1].shape[0], cols), parts[0][1].dtype)
    for off, arr in parts:
        out = lax.dynamic_update_slice(out, arr, (0, off))
    return out


def kernel(x, norm1_g, w_in, b_gate, tmix_mu, w0, w2, a0, a2, g2, k_k, k_a, r_k, lnx_g, lnx_b,
           w_oA, lnv_g, lnv_b, w_s, b_s, w_oB, w_out, norm2_g, w_rg, b_rg, w_re, b_re,
           w_e_gate, w_e_up, w_e_down, final_g):
    bsz, seq, d = x.shape
    n_tok = bsz * seq
    depth = norm1_g.shape[0]
    assert depth == 1, "the moe kernel fuses the final norm, so it must be the last layer"
    assert bsz % WKV_SEQS == 0 and seq % WKV_TOKENS == 0
    xf = x.reshape(n_tok, d)

    s_rkv = 3 * A_WIDTH
    s_w = s_rkv + D_DECAY_LORA
    s_a = s_w + D_AAA_LORA
    a_cols = s_a + D_GATE_LORA
    b_cols = 2 * B_WIDTH

    ones_bd = (jnp.arange(A_WIDTH)[:, None] // A_HEAD
               == jnp.arange(A_WIDTH)[None, :] // A_HEAD).astype(BF16)

    tm_a = 512
    tm_b = 512
    tm_p = 512

    for l in range(depth):
        wl = w_in[l]
        w_a = _mixer_a_layout(wl[:, :a_cols]).astype(BF16)
        mu_a = _mixer_a_layout(tmix_mu[l][None, :])
        w2p = jnp.pad(w2[l], ((0, LANES - D_DECAY_LORA), (0, 0))).astype(BF16)
        a2p = jnp.pad(a2[l], ((0, LANES - D_AAA_LORA), (0, 0))).astype(BF16)
        g2p = jnp.pad(g2[l], ((0, LORA_PAD - XG_OFF - D_GATE_LORA), (0, 0))).astype(BF16)
        g1 = norm1_g[l][None, :]

        row512 = lambda i: (i, 0)
        tok_a = pl.BlockSpec((tm_a, A_WIDTH), row512)
        vec_a = _full((1, A_WIDTH))
        outs = pl.pallas_call(
            functools.partial(_rwkv_prep_kernel, seq // tm_a),
            grid=(n_tok // tm_a,),
            in_specs=[pl.BlockSpec((tm_a, d), row512),
                      _full((1, d)), _full((d, A_PROJ)), _full((1, A_PROJ)), vec_a,
                      _full((LANES, A_WIDTH)), vec_a, _full((LANES, A_WIDTH)),
                      _full((LORA_PAD - XG_OFF, A_WIDTH)), vec_a, vec_a,
                      _full((A_WIDTH, A_WIDTH))],
            out_specs=[tok_a] * 7,
            out_shape=[jax.ShapeDtypeStruct((n_tok, A_WIDTH), F32)] * 7,
            scratch_shapes=[pltpu.VMEM((8, A_PROJ), F32)],
            compiler_params=_params(("arbitrary",)),
            name="rwkv_prep",
        )(xf, g1, w_a, mu_a, w0[l][None, :], w2p, a0[l][None, :], a2p, g2p,
          k_k[l][None, :], k_a[l][None, :], ones_bd)
        r_, k_, v_, a_, b_, ld_, gg_ = outs

        w_b = wl[:, a_cols:a_cols + b_cols].astype(BF16)
        w_g = wl[:, a_cols + b_cols:].astype(BF16)
        bs_full = jnp.repeat(b_s[l].T, B_GROUP_CH, axis=1)
        tok_d = pl.BlockSpec((tm_b, d), row512)
        ybg, ga = pl.pallas_call(
            _gmlp_kernel,
            grid=(n_tok // tm_b,),
            in_specs=[tok_d, _full((1, d)), _full((d, b_cols)), _full((d, 2 * d)),
                      _full((1, 2 * d)), _full((1, B_WIDTH)), _full((1, B_WIDTH)),
                      _full((B_GROUPS, GMLP_BLOCK, GMLP_BLOCK)), _full((GMLP_BLOCK, B_WIDTH)),
                      _full((B_WIDTH, d))],
            out_specs=[tok_d, tok_d],
            out_shape=[jax.ShapeDtypeStruct((n_tok, d), BF16)] * 2,
            compiler_params=_params(("parallel",)),
            name="gmlp",
        )(xf, g1, w_b, w_g, b_gate[l][None, :], lnv_g[l][None, :], lnv_b[l][None, :],
          w_s[l], bs_full, w_oB[l].astype(BF16))

        n_pairs = A_WIDTH // HEAD_PAIR
        n_chunks = WKV_TOKENS // WKV_CHUNK
        tok_w = pl.BlockSpec((WKV_SEQS, WKV_TOKENS, A_WIDTH), lambda bi, ti: (bi, ti, 0))
        vec_w = _full((1, A_WIDTH))
        seq_major = lambda t: t.reshape(bsz, seq, A_WIDTH)
        tok_scratch = pltpu.VMEM((WKV_SEQS, WKV_TOKENS, A_WIDTH), BF16)
        ya_in = pl.pallas_call(
            _wkv_kernel,
            grid=(bsz // WKV_SEQS, seq // WKV_TOKENS),
            in_specs=[tok_w] * 7 + [vec_w, vec_w, vec_w, _full((HEAD_PAIR, HEAD_PAIR))],
            out_specs=tok_w,
            out_shape=jax.ShapeDtypeStruct((bsz, seq, A_WIDTH), F32),
            scratch_shapes=[pltpu.VMEM((WKV_SEQS, n_pairs, HEAD_PAIR, HEAD_PAIR), F32)]
            + [tok_scratch] * 5
            + [pltpu.VMEM((WKV_SEQS, n_chunks, n_pairs, HEAD_PAIR, HEAD_PAIR), BF16),
               pltpu.VMEM((WKV_SEQS, n_chunks, n_pairs, HEAD_PAIR, HEAD_PAIR), F32)],
            compiler_params=_params(("parallel", "arbitrary")),
            name="wkv",
        )(*[seq_major(t) for t in (r_, k_, v_, a_, b_, ld_, gg_)], r_k[l].reshape(1, A_WIDTH),
          lnx_g[l][None, :], lnx_b[l][None, :],
          ones_bd[:HEAD_PAIR, :HEAD_PAIR]).reshape(n_tok, A_WIDTH)

        w_r = _place(LANES, [(0, jnp.transpose(w_re[l], (1, 0, 2)).reshape(d, N_EXPERTS)),
                             (N_EXPERTS, w_rg[l])])
        b_r = _place(LANES, [(0, b_re[l].reshape(1, N_EXPERTS)), (N_EXPERTS, b_rg[l][None, :])])
        wr_hi = w_r.astype(BF16)
        wr_lo = (w_r - wr_hi.astype(F32)).astype(BF16)
        tok_p = pl.BlockSpec((tm_p, d), row512)
        lane_p = pl.BlockSpec((tm_p, LANES), row512)
        n_sub = n_tok // MOE_SUB
        sorted_rows = tm_p // MOE_SUB * RUN_SLOTS
        x2, h_sorted, route, wts, cnt = pl.pallas_call(
            _post_kernel,
            grid=(n_tok // tm_p,),
            in_specs=[tok_p, pl.BlockSpec((tm_p, A_WIDTH), row512), tok_p, tok_p,
                      _full((A_WIDTH, d)), _full((d, d)), _full((1, d)), _full((d, LANES)),
                      _full((d, LANES)), _full((1, LANES))],
            out_specs=[tok_p, pl.BlockSpec((sorted_rows, d), row512), lane_p, lane_p,
                       pl.BlockSpec((8, LANES), row512)],
            out_shape=[jax.ShapeDtypeStruct((n_tok, d), F32),
                       jax.ShapeDtypeStruct((n_sub * RUN_SLOTS, d), BF16),
                       jax.ShapeDtypeStruct((n_tok, LANES), jnp.int32),
                       jax.ShapeDtypeStruct((n_tok, LANES), F32),
                       jax.ShapeDtypeStruct((n_tok // tm_p * 8, LANES), F32)],
            compiler_params=_params(("parallel",)),
            name="post",
        )(xf, ya_in, ga, ybg, w_oA[l].astype(BF16), w_out[l].astype(BF16),
          norm2_g[l][None, :], wr_hi, wr_lo, b_r)

        run_chunks = cnt.reshape(n_tok // tm_p, 8, LANES)[:, :tm_p // MOE_SUB, :N_EXPERTS]
        run_chunks = run_chunks.reshape(n_sub, N_EXPERTS).astype(jnp.int32)
        run_start = (jnp.cumsum(run_chunks, axis=1) - run_chunks) * RUN_ALIGN

        per_expert = lambda e, *_: (e, 0, 0)
        y_sorted = pl.pallas_call(
            _expert_kernel,
            grid_spec=pltpu.PrefetchScalarGridSpec(
                num_scalar_prefetch=2, grid=(N_EXPERTS,),
                in_specs=[pl.BlockSpec(memory_space=pl.ANY),
                          pl.BlockSpec((1, d, D_EXPERT), per_expert),
                          pl.BlockSpec((1, d, D_EXPERT), per_expert),
                          pl.BlockSpec((1, D_EXPERT, d), per_expert)],
                out_specs=pl.BlockSpec(memory_space=pl.ANY),
                scratch_shapes=[pltpu.VMEM((2, EXPERT_ROWS, d), BF16),
                                pltpu.VMEM((d, 2 * D_EXPERT), BF16),
                                pltpu.VMEM((D_EXPERT, d), BF16),
                                pltpu.SemaphoreType.DMA((4,)),
                                pltpu.SMEM((n_sub * RUN_SLOTS // RUN_ALIGN,), jnp.int32),
                                pltpu.SMEM((N_EXPERTS + 1,), jnp.int32),
                                pltpu.SMEM((1,), jnp.int32)]),
            out_shape=jax.ShapeDtypeStruct((n_sub * RUN_SLOTS, d), BF16),
            input_output_aliases={2: 0},
            compiler_params=_params(("arbitrary",)),
            name="moe_experts",
        )(run_chunks.reshape(-1), run_start.reshape(-1), h_sorted, w_e_gate[l], w_e_up[l],
          w_e_down[l])

        sub_rows = lambda i: (i, 0)
        xf = pl.pallas_call(
            _combine_kernel,
            grid=(n_sub,),
            in_specs=[pl.BlockSpec((MOE_SUB, d), sub_rows),
                      pl.BlockSpec((MOE_SUB, LANES), sub_rows),
                      pl.BlockSpec((MOE_SUB, LANES), sub_rows),
                      _full((1, d)),
                      pl.BlockSpec((RUN_SLOTS, d), sub_rows)],
            out_specs=pl.BlockSpec((MOE_SUB, d), sub_rows),
            out_shape=jax.ShapeDtypeStruct((n_tok, d), F32),
            compiler_params=_params(("parallel",)),
            name="moe_combine",
        )(x2, route, wts, final_g[None, :], y_sorted)

    return xf.reshape(bsz, seq, d)
```

```python
import functools

import jax
import jax.numpy as jnp
from jax import lax
from jax.experimental import pallas as pl
from jax.experimental.pallas import tpu as pltpu

F32 = jnp.float32
BF16 = jnp.bfloat16

D_MODEL = 1024
A_WIDTH = 512
A_HEAD = 64
D_DECAY_LORA = 64
D_AAA_LORA = 64
D_GATE_LORA = 160
B_WIDTH = 512
B_GROUPS = 4
B_GROUP_CH = 128
GMLP_BLOCK = 128
N_GROUPS = 4
EXPERTS_PER_GROUP = 8
N_EXPERTS = 32
D_EXPERT = 256
NORM_EPS = 1e-6
LN_EPS = 1e-5
LNX_EPS = 64e-5

LANES = 128
LORA_PAD = 512
XW_OFF, XA_OFF, XG_OFF = 0, 128, 256
A_PROJ = 3 * A_WIDTH + LORA_PAD
WKV_CHUNK = 64
HEAD_PAIR = 2 * A_HEAD
WKV_SEQS = 4
WKV_TOKENS = 128
WKV_PREP_GROUP = 4
WKV_NORM_GROUP = 2
MOE_SUB = 256
RUN_ALIGN = 16
RUN_SLOTS = 2 * MOE_SUB + N_EXPERTS * RUN_ALIGN
EXPERT_ROWS = 2048
EXPERT_TILE = 512
CHUNK_UNROLL = 4
VMEM_LIMIT = 48 * 1024 * 1024


def _rms(x, g):
    return x * lax.rsqrt(jnp.mean(x * x, axis=-1, keepdims=True) + NORM_EPS) * g


def _dot(a, b):
    return jnp.dot(a, b, preferred_element_type=F32)


def _dot_nt(a, b):
    return lax.dot_general(a, b, (((1,), (1,)), ((), ())), preferred_element_type=F32)


def _split2(x):
    hi = x.astype(BF16)
    lo = (x - hi.astype(F32)).astype(BF16)
    return hi, lo


def _split3(x):
    hi = x.astype(BF16)
    r1 = x - hi.astype(F32)
    mid = r1.astype(BF16)
    lo = (r1 - mid.astype(F32)).astype(BF16)
    return hi, mid, lo


def _pair_head_sums(xs, ones_pair):
    n_tiles = A_WIDTH // HEAD_PAIR
    rows = xs[0].shape[0]
    tiles = [x[:, t * HEAD_PAIR:(t + 1) * HEAD_PAIR].astype(BF16) for x in xs for t in range(n_tiles)]
    sums = _dot(jnp.concatenate(tiles, axis=0), ones_pair)
    return [jnp.concatenate([sums[(i * n_tiles + t) * rows:(i * n_tiles + t + 1) * rows]
                             for t in range(n_tiles)], axis=1) for i in range(len(xs))]


def _rwkv_prep_kernel(tiles_per_seq, x_ref, g1_ref, wa_ref, mu_ref, w0_ref,
                      w2_ref, a0_ref, a2_ref, g2_ref, kk_ref, ka_ref, ones_ref,
                      r_out, k_out, v_out, a_out, b_out, ld_out, g_out, tail_ref):
    i = pl.program_id(0)
    h = _rms(x_ref[...], g1_ref[...]).astype(BF16)
    p = _dot(h, wa_ref[...])
    tm = p.shape[0]
    prev = jnp.where(i % tiles_per_seq == 0, 0.0, tail_ref[7:8, :])
    tail_ref[...] = p[tm - 8:, :]
    row = lax.broadcasted_iota(jnp.int32, p.shape, 0)
    shifted = jnp.where(row == 0, prev, pltpu.roll(p, 1, axis=0))
    pm = p + mu_ref[...] * (shifted - p)

    r = pm[:, 0:A_WIDTH]
    k = pm[:, A_WIDTH:2 * A_WIDTH]
    v = pm[:, 2 * A_WIDTH:3 * A_WIDTH]
    lora = pm[:, 3 * A_WIDTH:]
    xw = lora[:, XW_OFF:XW_OFF + LANES]
    xa = lora[:, XA_OFF:XA_OFF + LANES]
    xg = lora[:, XG_OFF:]

    z = -(w0_ref[...] + _dot(jnp.tanh(xw).astype(BF16), w2_ref[...]))
    softplus = jnp.maximum(z, 0.0) + jnp.log(1.0 + jnp.exp(-jnp.abs(z)))
    w = -softplus - 0.5
    a_lr = jax.nn.sigmoid(a0_ref[...] + _dot(xa.astype(BF16), a2_ref[...]))
    g = _dot(jax.nn.sigmoid(xg).astype(BF16), g2_ref[...])

    kk = k * kk_ref[...]
    norm = jnp.sqrt(_dot((kk * kk).astype(BF16), ones_ref[...]))
    kk = kk / jnp.maximum(norm, 1e-12)

    r_out[...] = r
    k_out[...] = k * (1.0 + (a_lr - 1.0) * ka_ref[...])
    v_out[...] = v
    a_out[...] = -kk
    b_out[...] = kk * a_lr
    ld_out[...] = -jnp.exp(w)
    g_out[...] = g


def _gmlp_kernel(x_ref, g1_ref, wb_ref, wg_ref, bg_ref, lng_ref, lnb_ref, ws_ref, bs_ref,
                 wo_ref, ybg_out, ga_out):
    tm = x_ref.shape[0]
    h = _rms(x_ref[...], g1_ref[...]).astype(BF16)
    pb = _dot(h, wb_ref[...])
    z = 0.5 * pb * (1.0 + lax.erf(pb * (2.0 ** -0.5)))
    u = z[:, :B_WIDTH]
    v = z[:, B_WIDTH:]
    mean = jnp.mean(v, axis=-1, keepdims=True)
    vc = v - mean
    var = jnp.mean(vc * vc, axis=-1, keepdims=True)
    vn = (vc * lax.rsqrt(var + LN_EPS) * lng_ref[...] + lnb_ref[...]).astype(BF16)

    tri = (lax.broadcasted_iota(jnp.int32, (GMLP_BLOCK, GMLP_BLOCK), 0)
           >= lax.broadcasted_iota(jnp.int32, (GMLP_BLOCK, GMLP_BLOCK), 1))
    bs = bs_ref[...]
    rows = []
    for blk in range(tm // GMLP_BLOCK):
        cols = []
        for grp in range(B_GROUPS):
            ws = jnp.where(tri, ws_ref[grp], 0.0).astype(BF16)
            vb = vn[blk * GMLP_BLOCK:(blk + 1) * GMLP_BLOCK,
                    grp * B_GROUP_CH:(grp + 1) * B_GROUP_CH]
            cols.append(_dot(ws, vb))
        rows.append(jnp.concatenate(cols, axis=1) + bs)
    sv = jnp.concatenate(rows, axis=0)
    yb = _dot((u * sv).astype(BF16), wo_ref[...])
    gates = jax.nn.sigmoid(_dot(h, wg_ref[...]) + bg_ref[...])
    ga_out[...] = gates[:, :D_MODEL].astype(BF16)
    ybg_out[...] = (gates[:, D_MODEL:] * yb).astype(BF16)


def _wkv_kernel(r_ref, k_ref, v_ref, a_ref, b_ref, ld_ref, g_ref, rk_ref, lng_ref, lnb_ref,
                ones_ref, o_ref, st_ref, ta_s, tl_s, arb_s, ark_s, rt_s, bkt_s, dcol_s):
    C = WKV_CHUNK
    bb, tb, _ = r_ref.shape
    n_chunks = tb // C
    n_pairs = A_WIDTH // HEAD_PAIR

    @pl.when(pl.program_id(1) == 0)
    def _():
        st_ref[...] = jnp.zeros_like(st_ref)

    row = lax.broadcasted_iota(jnp.int32, (C, HEAD_PAIR), 0)
    src = lax.broadcasted_iota(jnp.int32, (C, HEAD_PAIR), 1) & (C - 1)
    incl = src <= row
    strict = src < row
    eye_pair = jnp.where(src == row, 1.0, 0.0)
    bd_mask = ((lax.broadcasted_iota(jnp.int32, (HEAD_PAIR, HEAD_PAIR), 0) >= A_HEAD)
               == (lax.broadcasted_iota(jnp.int32, (HEAD_PAIR, HEAD_PAIR), 1) >= A_HEAD))
    tri_c = (lax.broadcasted_iota(jnp.int32, (C, C), 0)
             >= lax.broadcasted_iota(jnp.int32, (C, C), 1)).astype(BF16)
    ones_pair = ones_ref[...]
    pair_cols = [slice(p * HEAD_PAIR, (p + 1) * HEAD_PAIR) for p in range(n_pairs)]

    def bd(x):
        xb = x.astype(BF16)
        return jnp.where(bd_mask, jnp.concatenate([xb, xb], axis=0), jnp.zeros((), BF16))

    def prep(it, carry):
        where, lhs, rhs, ats = [], [], [], []
        for j in range(WKV_PREP_GROUP):
            flat = it * WKV_PREP_GROUP + j
            b = flat // n_chunks
            c = flat % n_chunks
            sl = pl.ds(pl.multiple_of(c * C, C), C)
            ld = ld_ref[b, sl, :]
            hi, mid, lo = _split3(ld)
            cl = _dot(tri_c, hi) + _dot(tri_c, mid) + _dot(tri_c, lo)
            cl_end = cl[C - 1:C, :]
            k = k_ref[b, sl, :]
            bv = b_ref[b, sl, :]
            d_inv = jnp.exp(-cl)
            d_tail = jnp.exp(cl_end - cl)
            rt = (r_ref[b, sl, :] * jnp.exp(cl)).astype(BF16)
            rt_s[b, sl, :] = rt
            kt = k * d_inv
            bt = bv * d_inv
            at = a_ref[b, sl, :] * jnp.exp(cl - ld)
            kd = k * d_tail
            bdk = bv * d_tail
            d_end = jnp.exp(cl_end)
            for p, cs in enumerate(pair_cols):
                bkt_s[b, c, p] = jnp.concatenate([bdk[:, cs], kd[:, cs]], axis=0).T.astype(BF16)
                dcol_s[b, c, p] = jnp.broadcast_to(d_end[:, cs], (HEAD_PAIR, HEAD_PAIR)).T
                where.append((b, sl, cs))
                ats.append(at[:, cs])
                lhs.append(jnp.concatenate([rt[:, cs], at[:, cs].astype(BF16)], axis=0))
                rhs.append(jnp.concatenate([bd(kt[:, cs]), bd(bt[:, cs])], axis=0))
        n = len(where)
        amat = [_dot_nt(lhs[i], rhs[i]) for i in range(n)]
        l_ak, l_ab = [], []
        for i, (b, sl, cs) in enumerate(where):
            ark_s[b, sl, cs] = jnp.where(incl, amat[i][:C, :HEAD_PAIR], 0.0).astype(BF16)
            arb_s[b, sl, cs] = jnp.where(incl, amat[i][:C, HEAD_PAIR:], 0.0).astype(BF16)
            l_ak.append(jnp.where(strict, amat[i][C:, :HEAD_PAIR], 0.0))
            l_ab.append(jnp.where(strict, amat[i][C:, HEAD_PAIR:], 0.0))

        t_mat = [eye_pair + l for l in l_ab]
        q = [_dot(l.astype(BF16), bd(l)) for l in l_ab]
        n_sq = 1
        while 2 * n_sq < C // 2:
            both = [_dot(q[i].astype(BF16), jnp.concatenate([bd(q[i]), bd(t_mat[i])], axis=1))
                    for i in range(n)]
            q = [x[:, :HEAD_PAIR] for x in both]
            t_mat = [t_mat[i] + both[i][:, HEAD_PAIR:] for i in range(n)]
            n_sq *= 2
        t_mat = [t_mat[i] + _dot(q[i].astype(BF16), bd(t_mat[i])) for i in range(n)]
        tal = [_dot(t_mat[i].astype(BF16), jnp.concatenate([bd(ats[i]), bd(l_ak[i])], axis=1))
               for i in range(n)]
        for i, (b, sl, cs) in enumerate(where):
            ta_s[b, sl, cs] = tal[i][:, :HEAD_PAIR].astype(BF16)
            tl_s[b, sl, cs] = tal[i][:, HEAD_PAIR:].astype(BF16)
        return carry

    lax.fori_loop(0, bb * n_chunks // WKV_PREP_GROUP, prep, 0)

    def step(c, carry):
        sl = pl.ds(pl.multiple_of(c * C, C), C)
        chains = [(b, p) for b in range(bb) for p in range(n_pairs)]
        v = [v_ref[b, sl, :] for b in range(bb)]
        st = [st_ref[b, p] for b, p in chains]
        st_b = [x.astype(BF16) for x in st]
        bd_v = [bd(v[b][:, pair_cols[p]]) for b, p in chains]
        on_st = [_dot(jnp.concatenate([ta_s[b, sl, pair_cols[p]], rt_s[b, sl, pair_cols[p]]],
                                      axis=0), st_b[i]) for i, (b, p) in enumerate(chains)]
        on_v = [_dot(jnp.concatenate([tl_s[b, sl, pair_cols[p]], ark_s[b, sl, pair_cols[p]]],
                                     axis=0), bd_v[i]) for i, (b, p) in enumerate(chains)]
        u = [on_st[i][:C] + on_v[i][:C] for i in range(len(chains))]
        for i, (b, p) in enumerate(chains):
            uv = jnp.concatenate([u[i], v[b][:, pair_cols[p]]], axis=0).astype(BF16)
            st_ref[b, p] = (dcol_s[b, c, p] * st[i]
                            + jnp.where(bd_mask, _dot(bkt_s[b, c, p], uv), 0.0))
        ys = [on_st[i][C:] + on_v[i][C:] + _dot(arb_s[b, sl, pair_cols[p]], bd(u[i]))
              for i, (b, p) in enumerate(chains)]
        for b in range(bb):
            o_ref[b, sl, :] = jnp.concatenate(ys[b * n_pairs:(b + 1) * n_pairs], axis=1)
        return carry

    lax.fori_loop(0, n_chunks, step, 0)

    def finish(it, carry):
        items = [(b, pl.ds(pl.multiple_of((it * WKV_NORM_GROUP + j) * C, C), C))
                 for j in range(WKV_NORM_GROUP) for b in range(bb)]
        y = [o_ref[b, sl, :] for b, sl in items]
        rkr = [r_ref[b, sl, :] * k_ref[b, sl, :] * rk_ref[...] for b, sl in items]
        sums = [_pair_head_sums([y[i], rkr[i]], ones_pair) for i in range(len(items))]
        yc = [y[i] - sums[i][0] * (1.0 / A_HEAD) for i in range(len(items))]
        var = [_pair_head_sums([x * x], ones_pair)[0] * (1.0 / A_HEAD) for x in yc]
        for i, (b, sl) in enumerate(items):
            yn = yc[i] * lax.rsqrt(var[i] + LNX_EPS) * lng_ref[...] + lnb_ref[...]
            o_ref[b, sl, :] = (yn + sums[i][1] * v_ref[b, sl, :]) * g_ref[b, sl, :]
        return carry

    lax.fori_loop(0, n_chunks // WKV_NORM_GROUP, finish, 0)


def _post_kernel(x_ref, ya_ref, ga_ref, ybg_ref, woa_ref, wout_ref, g2_ref, wr_ref,
                 wr_hi_ref, br_ref, x2_out, hl_out, route_out, wts_out, cnt_out):
    y_a = _dot(ya_ref[...].astype(BF16), woa_ref[...])
    mix = ga_ref[...] * y_a + ybg_ref[...]
    x2 = x_ref[...] + _dot(mix.astype(BF16), wout_ref[...])
    x2_out[...] = x2
    h2 = _rms(x2, g2_ref[...])

    hi, lo = _split2(h2)
    hi_w = _dot(hi, wr_ref[...])
    logits = hi_w[:, :LANES] + hi_w[:, LANES:] + _dot(lo, wr_hi_ref[...]) + br_ref[...]
    lane = lax.broadcasted_iota(jnp.int32, logits.shape, 1)
    neg = jnp.float32(-jnp.inf)
    big = jnp.int32(LANES)
    is_grp = (lane >= N_EXPERTS) & (lane < N_EXPERTS + N_GROUPS)
    gl = jnp.where(is_grp, logits, neg)
    gmax = jnp.max(gl, axis=-1, keepdims=True)
    g_p = 1.0 / jnp.sum(jnp.exp(gl - gmax), axis=-1, keepdims=True)
    g_idx = jnp.min(jnp.where(gl == gmax, lane, big), axis=-1, keepdims=True) - N_EXPERTS
    lo_lane = g_idx * EXPERTS_PER_GROUP
    in_grp = (lane >= lo_lane) & (lane < lo_lane + EXPERTS_PER_GROUP)
    el = jnp.where(in_grp, logits, neg)
    e1 = jnp.max(el, axis=-1, keepdims=True)
    i1 = jnp.min(jnp.where(el == e1, lane, big), axis=-1, keepdims=True)
    el2 = jnp.where(lane == i1, neg, el)
    e2 = jnp.max(el2, axis=-1, keepdims=True)
    i2 = jnp.min(jnp.where(el2 == e2, lane, big), axis=-1, keepdims=True)
    t = jnp.exp(e2 - e1)
    w1 = g_p / (1.0 + t)
    w2 = g_p * t / (1.0 + t)
    wts_out[...] = jnp.where(lane == 0, w1, jnp.where(lane == 1, w2, 0.0))

    tm = logits.shape[0]
    n_sub = tm // MOE_SUB
    onehot = jnp.where((lane == i1) | (lane == i2), 1.0, 0.0)
    tri = (lax.broadcasted_iota(jnp.int32, (tm, tm), 0)
           >= lax.broadcasted_iota(jnp.int32, (tm, tm), 1)).astype(BF16)
    incl = _dot(tri, onehot.astype(BF16))
    row = lax.broadcasted_iota(jnp.int32, logits.shape, 0)
    row8 = lax.broadcasted_iota(jnp.int32, (8, LANES), 0)
    ends = [incl[(g + 1) * MOE_SUB - 1:(g + 1) * MOE_SUB, :] for g in range(n_sub)]
    before = incl - onehot
    chunks = jnp.zeros((8, LANES), F32)
    for g, end in enumerate(ends):
        start = ends[g - 1] if g else jnp.zeros_like(end)
        before = before - jnp.where((row >= g * MOE_SUB) & (row < (g + 1) * MOE_SUB), start, 0.0)
        chunks = jnp.where(row8 == g, jnp.ceil((end - start) * (1.0 / RUN_ALIGN)), chunks)
    cnt_out[...] = chunks
    upper = (lax.broadcasted_iota(jnp.int32, (LANES, LANES), 0)
             < lax.broadcasted_iota(jnp.int32, (LANES, LANES), 1)).astype(BF16)
    run_start = _dot(chunks.astype(BF16), upper) * RUN_ALIGN
    start_rows = jnp.zeros(logits.shape, F32)
    for g in range(n_sub):
        start_rows = jnp.where((row >= g * MOE_SUB) & (row < (g + 1) * MOE_SUB),
                               run_start[g:g + 1, :], start_rows)
    slot1 = jnp.sum(jnp.where(lane == i1, start_rows + before, 0.0), axis=-1, keepdims=True)
    slot2 = jnp.sum(jnp.where(lane == i2, start_rows + before, 0.0), axis=-1, keepdims=True)
    route = jnp.where(lane == 0, slot1, jnp.where(lane == 1, slot2, 0.0)).astype(jnp.int32)
    route_out[...] = route

    route_t = route.T
    h2b = h2.astype(BF16)
    slot_ids = lax.broadcasted_iota(jnp.int32, (RUN_SLOTS, MOE_SUB), 0)
    for g in range(n_sub):
        cols = slice(g * MOE_SUB, (g + 1) * MOE_SUB)
        pick = jnp.where((slot_ids == route_t[0:1, cols]) | (slot_ids == route_t[1:2, cols]),
                         1.0, 0.0).astype(BF16)
        hl_out[g * RUN_SLOTS:(g + 1) * RUN_SLOTS, :] = _dot(pick, h2b[cols, :]).astype(BF16)


def _expert_kernel(chunk_s, first_s, hl_in_ref, wg_ref, wu_ref, wd_ref, hl_ref, xbuf, wgu_s, wd_s,
                   sem, drain_s):
    del hl_in_ref
    e = pl.program_id(0)
    pass_chunks = EXPERT_ROWS // RUN_ALIGN
    mine = e % 2
    other = 1 - mine
    gather_sem = lambda buf: sem.at[buf]
    scatter_sem = lambda buf: sem.at[2 + buf]

    def first_pass(ex):
        lo = first_s[ex]
        return lo, jnp.minimum(lo + pass_chunks, first_s[ex + 1])

    def for_chunks(buf, c_lo, c_hi, fn):
        def one(c):
            fn(hl_ref.at[pl.ds(pl.multiple_of(chunk_s[c], RUN_ALIGN), RUN_ALIGN)],
               xbuf.at[buf, pl.ds(pl.multiple_of((c - c_lo) * RUN_ALIGN, RUN_ALIGN), RUN_ALIGN)])

        def block(i, carry):
            for u in range(CHUNK_UNROLL):
                one(c_lo + i * CHUNK_UNROLL + u)
            return carry

        def tail(c, carry):
            one(c)
            return carry

        n_blocks = (c_hi - c_lo) // CHUNK_UNROLL
        lax.fori_loop(0, n_blocks, block, 0)
        lax.fori_loop(c_lo + n_blocks * CHUNK_UNROLL, c_hi, tail, 0)

    def gather(buf, c_lo, c_hi):
        for_chunks(buf, c_lo, c_hi, lambda hbm, vm: pltpu.make_async_copy(
            hbm, vm, gather_sem(buf)).start())

    def scatter(buf, c_lo, c_hi):
        for_chunks(buf, c_lo, c_hi, lambda hbm, vm: pltpu.make_async_copy(
            vm, hbm, scatter_sem(buf)).start())

    def wait_chunks(which_sem, n):
        chunk = pl.ds(0, RUN_ALIGN)

        def body(c, carry):
            pltpu.make_async_copy(hl_ref.at[chunk], xbuf.at[0, chunk], which_sem).wait()
            return carry

        lax.fori_loop(0, n, body, 0)

    def evaluate(buf, n_chunks):
        def tile(k, carry):
            sl = pl.ds(pl.multiple_of(k * EXPERT_TILE, EXPERT_TILE), EXPERT_TILE)
            gu = _dot(xbuf[buf, sl, :], wgu_s[...])
            gate = gu[:, :D_EXPERT]
            act = gate * jax.nn.sigmoid(gate) * gu[:, D_EXPERT:]
            xbuf[buf, sl, :] = _dot(act.astype(BF16), wd_s[...]).astype(BF16)
            return carry

        lax.fori_loop(0, (n_chunks * RUN_ALIGN + EXPERT_TILE - 1) // EXPERT_TILE, tile, 0)

    @pl.when(e == 0)
    def _():
        xbuf[...] = jnp.zeros_like(xbuf)
        gather(0, *first_pass(0))
        drain_s[0] = 0

    wgu_s[:, :D_EXPERT] = wg_ref[0].astype(BF16)
    wgu_s[:, D_EXPERT:] = wu_ref[0].astype(BF16)
    wd_s[...] = wd_ref[0].astype(BF16)

    c_lo, c_hi = first_pass(e)
    wait_chunks(gather_sem(mine), c_hi - c_lo)
    evaluate(mine, c_hi - c_lo)
    wait_chunks(scatter_sem(other), drain_s[0])
    scatter(mine, c_lo, c_hi)

    def later_pass(lo):
        hi = jnp.minimum(lo + pass_chunks, first_s[e + 1])
        gather(other, lo, hi)
        wait_chunks(gather_sem(other), hi - lo)
        evaluate(other, hi - lo)
        scatter(other, lo, hi)
        wait_chunks(scatter_sem(other), hi - lo)
        return hi

    lax.while_loop(lambda c: c < first_s[e + 1], later_pass, c_hi)
    drain_s[0] = c_hi - c_lo

    @pl.when(e + 1 < pl.num_programs(0))
    def _():
        gather(other, *first_pass(e + 1))

    @pl.when(e + 1 == pl.num_programs(0))
    def _():
        wait_chunks(scatter_sem(mine), c_hi - c_lo)


def _combine_kernel(x2_ref, route_ref, wts_ref, gf_ref, yl_ref, o_ref):
    route = route_ref[...]
    wts = wts_ref[...]
    slots = lax.broadcasted_iota(jnp.int32, (x2_ref.shape[0], RUN_SLOTS), 1)
    pick = (jnp.where(slots == route[:, 0:1], wts[:, 0:1], 0.0)
            + jnp.where(slots == route[:, 1:2], wts[:, 1:2], 0.0))
    y = _dot(pick.astype(BF16), yl_ref[...])
    o_ref[...] = _rms(x2_ref[...] + y, gf_ref[...])


def _full(shape):
    return pl.BlockSpec(shape, lambda *_: (0,) * len(shape))


def _params(sem):
    return pltpu.CompilerParams(dimension_semantics=sem, vmem_limit_bytes=VMEM_LIMIT)


def _mixer_a_layout(t):
    s_rkv = 3 * A_WIDTH
    s_w = s_rkv + D_DECAY_LORA
    s_a = s_w + D_AAA_LORA
    gap = lambda n: jnp.zeros((t.shape[0], n), t.dtype)
    return jnp.concatenate(
        [t[:, :s_w], gap(XA_OFF - D_DECAY_LORA), t[:, s_w:s_a], gap(XG_OFF - XA_OFF - D_AAA_LORA),
         t[:, s_a:], gap(LORA_PAD - XG_OFF - D_GATE_LORA)], axis=1)


def _place(cols, parts):
    out = jnp.zeros((parts[0][1].shape[0], cols), parts[0][1].dtype)
    for off, arr in parts:
        out = lax.dynamic_update_slice(out, arr, (0, off))
    return out


def kernel(x, norm1_g, w_in, b_gate, tmix_mu, w0, w2, a0, a2, g2, k_k, k_a, r_k, lnx_g, lnx_b,
           w_oA, lnv_g, lnv_b, w_s, b_s, w_oB, w_out, norm2_g, w_rg, b_rg, w_re, b_re,
           w_e_gate, w_e_up, w_e_down, final_g):
    bsz, seq, d = x.shape
    n_tok = bsz * seq
    depth = norm1_g.shape[0]
    assert depth == 1, "the moe kernel fuses the final norm, so it must be the last layer"
    assert bsz % WKV_SEQS == 0 and seq % WKV_TOKENS == 0
    xf = x.reshape(n_tok, d)

    s_rkv = 3 * A_WIDTH
    s_w = s_rkv + D_DECAY_LORA
    s_a = s_w + D_AAA_LORA
    a_cols = s_a + D_GATE_LORA
    b_cols = 2 * B_WIDTH

    ones_bd = (jnp.arange(A_WIDTH)[:, None] // A_HEAD
               == jnp.arange(A_WIDTH)[None, :] // A_HEAD).astype(BF16)

    tm_a = 512
    tm_b = 512
    tm_p = 512

    for l in range(depth):
        wl = w_in[l]
        w_a = _mixer_a_layout(wl[:, :a_cols]).astype(BF16)
        mu_a = _mixer_a_layout(tmix_mu[l][None, :])
        w2p = jnp.pad(w2[l], ((0, LANES - D_DECAY_LORA), (0, 0))).astype(BF16)
        a2p = jnp.pad(a2[l], ((0, LANES - D_AAA_LORA), (0, 0))).astype(BF16)
        g2p = jnp.pad(g2[l], ((0, LORA_PAD - XG_OFF - D_GATE_LORA), (0, 0))).astype(BF16)
        g1 = norm1_g[l][None, :]

        row512 = lambda i: (i, 0)
        tok_a = pl.BlockSpec((tm_a, A_WIDTH), row512)
        vec_a = _full((1, A_WIDTH))
        outs = pl.pallas_call(
            functools.partial(_rwkv_prep_kernel, seq // tm_a),
            grid=(n_tok // tm_a,),
            in_specs=[pl.BlockSpec((tm_a, d), row512),
                      _full((1, d)), _full((d, A_PROJ)), _full((1, A_PROJ)), vec_a,
                      _full((LANES, A_WIDTH)), vec_a, _full((LANES, A_WIDTH)),
                      _full((LORA_PAD - XG_OFF, A_WIDTH)), vec_a, vec_a,
                      _full((A_WIDTH, A_WIDTH))],
            out_specs=[tok_a] * 7,
            out_shape=[jax.ShapeDtypeStruct((n_tok, A_WIDTH), F32)] * 7,
            scratch_shapes=[pltpu.VMEM((8, A_PROJ), F32)],
            compiler_params=_params(("arbitrary",)),
            name="rwkv_prep",
        )(xf, g1, w_a, mu_a, w0[l][None, :], w2p, a0[l][None, :], a2p, g2p,
          k_k[l][None, :], k_a[l][None, :], ones_bd)
        r_, k_, v_, a_, b_, ld_, gg_ = outs

        w_b = wl[:, a_cols:a_cols + b_cols].astype(BF16)
        w_g = wl[:, a_cols + b_cols:].astype(BF16)
        bs_full = jnp.repeat(b_s[l].T, B_GROUP_CH, axis=1)
        tok_d = pl.BlockSpec((tm_b, d), row512)
        ybg, ga = pl.pallas_call(
            _gmlp_kernel,
            grid=(n_tok // tm_b,),
            in_specs=[tok_d, _full((1, d)), _full((d, b_cols)), _full((d, 2 * d)),
                      _full((1, 2 * d)), _full((1, B_WIDTH)), _full((1, B_WIDTH)),
                      _full((B_GROUPS, GMLP_BLOCK, GMLP_BLOCK)), _full((GMLP_BLOCK, B_WIDTH)),
                      _full((B_WIDTH, d))],
            out_specs=[tok_d, tok_d],
            out_shape=[jax.ShapeDtypeStruct((n_tok, d), BF16)] * 2,
            compiler_params=_params(("parallel",)),
            name="gmlp",
        )(xf, g1, w_b, w_g, b_gate[l][None, :], lnv_g[l][None, :], lnv_b[l][None, :],
          w_s[l], bs_full, w_oB[l].astype(BF16))

        n_pairs = A_WIDTH // HEAD_PAIR
        n_chunks = WKV_TOKENS // WKV_CHUNK
        tok_w = pl.BlockSpec((WKV_SEQS, WKV_TOKENS, A_WIDTH), lambda bi, ti: (bi, ti, 0))
        vec_w = _full((1, A_WIDTH))
        seq_major = lambda t: t.reshape(bsz, seq, A_WIDTH)
        tok_scratch = pltpu.VMEM((WKV_SEQS, WKV_TOKENS, A_WIDTH), BF16)
        ya_in = pl.pallas_call(
            _wkv_kernel,
            grid=(bsz // WKV_SEQS, seq // WKV_TOKENS),
            in_specs=[tok_w] * 7 + [vec_w, vec_w, vec_w, _full((HEAD_PAIR, HEAD_PAIR))],
            out_specs=tok_w,
            out_shape=jax.ShapeDtypeStruct((bsz, seq, A_WIDTH), F32),
            scratch_shapes=[pltpu.VMEM((WKV_SEQS, n_pairs, HEAD_PAIR, HEAD_PAIR), F32)]
            + [tok_scratch] * 5
            + [pltpu.VMEM((WKV_SEQS, n_chunks, n_pairs, HEAD_PAIR, HEAD_PAIR), BF16),
               pltpu.VMEM((WKV_SEQS, n_chunks, n_pairs, HEAD_PAIR, HEAD_PAIR), F32)],
            compiler_params=_params(("parallel", "arbitrary")),
            name="wkv",
        )(*[seq_major(t) for t in (r_, k_, v_, a_, b_, ld_, gg_)], r_k[l].reshape(1, A_WIDTH),
          lnx_g[l][None, :], lnx_b[l][None, :],
          ones_bd[:HEAD_PAIR, :HEAD_PAIR]).reshape(n_tok, A_WIDTH)

        w_r = _place(LANES, [(0, jnp.transpose(w_re[l], (1, 0, 2)).reshape(d, N_EXPERTS)),
                             (N_EXPERTS, w_rg[l])])
        b_r = _place(LANES, [(0, b_re[l].reshape(1, N_EXPERTS)), (N_EXPERTS, b_rg[l][None, :])])
        wr_hi = w_r.astype(BF16)
        wr_lo = (w_r - wr_hi.astype(F32)).astype(BF16)
        tok_p = pl.BlockSpec((tm_p, d), row512)
        lane_p = pl.BlockSpec((tm_p, LANES), row512)
        n_sub = n_tok // MOE_SUB
        sorted_rows = tm_p // MOE_SUB * RUN_SLOTS
        x2, h_sorted, route, wts, cnt = pl.pallas_call(
            _post_kernel,
            grid=(n_tok // tm_p,),
            in_specs=[tok_p, pl.BlockSpec((tm_p, A_WIDTH), row512), tok_p, tok_p,
                      _full((A_WIDTH, d)), _full((d, d)), _full((1, d)), _full((d, 2 * LANES)),
                      _full((d, LANES)), _full((1, LANES))],
            out_specs=[tok_p, pl.BlockSpec((sorted_rows, d), row512), lane_p, lane_p,
                       pl.BlockSpec((8, LANES), row512)],
            out_shape=[jax.ShapeDtypeStruct((n_tok, d), F32),
                       jax.ShapeDtypeStruct((n_sub * RUN_SLOTS, d), BF16),
                       jax.ShapeDtypeStruct((n_tok, LANES), jnp.int32),
                       jax.ShapeDtypeStruct((n_tok, LANES), F32),
                       jax.ShapeDtypeStruct((n_tok // tm_p * 8, LANES), F32)],
            compiler_params=_params(("parallel",)),
            name="post",
        )(xf, ya_in, ga, ybg, w_oA[l].astype(BF16), w_out[l].astype(BF16),
          norm2_g[l][None, :], jnp.concatenate([wr_hi, wr_lo], axis=1), wr_hi, b_r)

        run_chunks = cnt.reshape(n_tok // tm_p, 8, LANES)[:, :tm_p // MOE_SUB, :N_EXPERTS]
        run_chunks = run_chunks.reshape(n_sub, N_EXPERTS).astype(jnp.int32)
        run_start = (jnp.cumsum(run_chunks, axis=1) - run_chunks) * RUN_ALIGN
        run_row = (jnp.arange(n_sub, dtype=jnp.int32)[:, None] * RUN_SLOTS + run_start).T.reshape(-1)
        run_n = run_chunks.T.reshape(-1)
        run_end = jnp.cumsum(run_n)
        first_chunk = jnp.concatenate([jnp.zeros((1,), jnp.int32), run_end[n_sub - 1::n_sub]])
        pos = jnp.arange(n_sub * RUN_SLOTS // RUN_ALIGN, dtype=jnp.int32)
        run_of = jnp.sum(pos[:, None] >= run_end[None, :], axis=1)
        in_run = run_of[:, None] == jnp.arange(run_n.shape[0], dtype=jnp.int32)[None, :]
        pick = lambda t: jnp.sum(jnp.where(in_run, t[None, :], 0), axis=1)
        chunk_rows = pick(run_row) + (pos - pick(run_end - run_n)) * RUN_ALIGN

        per_expert = lambda e, *_: (e, 0, 0)
        y_sorted = pl.pallas_call(
            _expert_kernel,
            grid_spec=pltpu.PrefetchScalarGridSpec(
                num_scalar_prefetch=2, grid=(N_EXPERTS,),
                in_specs=[pl.BlockSpec(memory_space=pl.ANY),
                          pl.BlockSpec((1, d, D_EXPERT), per_expert),
                          pl.BlockSpec((1, d, D_EXPERT), per_expert),
                          pl.BlockSpec((1, D_EXPERT, d), per_expert)],
                out_specs=pl.BlockSpec(memory_space=pl.ANY),
                scratch_shapes=[pltpu.VMEM((2, EXPERT_ROWS, d), BF16),
                                pltpu.VMEM((d, 2 * D_EXPERT), BF16),
                                pltpu.VMEM((D_EXPERT, d), BF16),
                                pltpu.SemaphoreType.DMA((4,)),
                                pltpu.SMEM((1,), jnp.int32)]),
            out_shape=jax.ShapeDtypeStruct((n_sub * RUN_SLOTS, d), BF16),
            input_output_aliases={2: 0},
            compiler_params=_params(("arbitrary",)),
            name="moe_experts",
        )(chunk_rows, first_chunk, h_sorted, w_e_gate[l], w_e_up[l], w_e_down[l])

        sub_rows = lambda i: (i, 0)
        xf = pl.pallas_call(
            _combine_kernel,
            grid=(n_sub,),
            in_specs=[pl.BlockSpec((MOE_SUB, d), sub_rows),
                      pl.BlockSpec((MOE_SUB, LANES), sub_rows),
                      pl.BlockSpec((MOE_SUB, LANES), sub_rows),
                      _full((1, d)),
                      pl.BlockSpec((RUN_SLOTS, d), sub_rows)],
            out_specs=pl.BlockSpec((MOE_SUB, d), sub_rows),
            out_shape=jax.ShapeDtypeStruct((n_tok, d), F32),
            compiler_params=_params(("parallel",)),
            name="moe_combine",
        )(x2, route, wts, final_g[None, :], y_sorted)

    return xf.reshape(bsz, seq, d)
```

```python
import functools

import jax
import jax.numpy as jnp
from jax import lax
from jax.experimental import pallas as pl
from jax.experimental.pallas import tpu as pltpu

F32 = jnp.float32
BF16 = jnp.bfloat16

D_MODEL = 1024
A_WIDTH = 512
A_HEAD = 64
D_DECAY_LORA = 64
D_AAA_LORA = 64
D_GATE_LORA = 160
B_WIDTH = 512
B_GROUPS = 4
B_GROUP_CH = 128
GMLP_BLOCK = 128
N_GROUPS = 4
EXPERTS_PER_GROUP = 8
N_EXPERTS = 32
D_EXPERT = 256
NORM_EPS = 1e-6
LN_EPS = 1e-5
LNX_EPS = 64e-5

LANES = 128
LORA_PAD = 512
XW_OFF, XA_OFF, XG_OFF = 0, 128, 256
A_PROJ = 3 * A_WIDTH + LORA_PAD
WKV_CHUNK = 64
HEAD_PAIR = 2 * A_HEAD
WKV_SEQS = 4
WKV_TOKENS = 128
WKV_PREP_GROUP = 4
WKV_NORM_GROUP = 2
MOE_SUB = 256
RUN_ALIGN = 16
RUN_SLOTS = 2 * MOE_SUB + N_EXPERTS * RUN_ALIGN
EXPERT_ROWS = 2048
EXPERT_TILE = 512
DENSE_SPLIT = 256
CHUNK_UNROLL = 4
VMEM_LIMIT = 48 * 1024 * 1024


def _rms(x, g):
    return x * lax.rsqrt(jnp.mean(x * x, axis=-1, keepdims=True) + NORM_EPS) * g


def _dot(a, b):
    return jnp.dot(a, b, preferred_element_type=F32)


def _dot_nt(a, b):
    return lax.dot_general(a, b, (((1,), (1,)), ((), ())), preferred_element_type=F32)


def _split2(x):
    hi = x.astype(BF16)
    lo = (x - hi.astype(F32)).astype(BF16)
    return hi, lo


def _split3(x):
    hi = x.astype(BF16)
    r1 = x - hi.astype(F32)
    mid = r1.astype(BF16)
    lo = (r1 - mid.astype(F32)).astype(BF16)
    return hi, mid, lo


def _pair_head_sums(xs, ones_pair):
    n_tiles = A_WIDTH // HEAD_PAIR
    rows = xs[0].shape[0]
    tiles = [x[:, t * HEAD_PAIR:(t + 1) * HEAD_PAIR].astype(BF16) for x in xs for t in range(n_tiles)]
    sums = _dot(jnp.concatenate(tiles, axis=0), ones_pair)
    return [jnp.concatenate([sums[(i * n_tiles + t) * rows:(i * n_tiles + t + 1) * rows]
                             for t in range(n_tiles)], axis=1) for i in range(len(xs))]


def _rwkv_prep_kernel(tiles_per_seq, x_ref, g1_ref, wa_ref, mu_ref, w0_ref,
                      w2_ref, a0_ref, a2_ref, g2_ref, kk_ref, ka_ref, ones_ref,
                      r_out, k_out, v_out, a_out, b_out, ld_out, g_out, tail_ref):
    i = pl.program_id(0)
    tm = x_ref.shape[0]
    parts = [slice(j * DENSE_SPLIT, (j + 1) * DENSE_SPLIT) for j in range(tm // DENSE_SPLIT)]
    g1 = g1_ref[...]
    h = [_rms(x_ref[p, :], g1).astype(BF16) for p in parts]
    proj = [_dot(hh, wa_ref[...]) for hh in h]
    prev = [jnp.where(i % tiles_per_seq == 0, 0.0, tail_ref[7:8, :])]
    prev += [p[DENSE_SPLIT - 1:, :] for p in proj[:-1]]
    tail_ref[...] = proj[-1][DENSE_SPLIT - 8:, :]
    row = lax.broadcasted_iota(jnp.int32, proj[0].shape, 0)
    pm = []
    for p, pv in zip(proj, prev):
        shifted = jnp.where(row == 0, pv, pltpu.roll(p, 1, axis=0))
        pm.append(p + mu_ref[...] * (shifted - p))
    lora = [x[:, 3 * A_WIDTH:] for x in pm]
    decay = [_dot(jnp.tanh(x[:, XW_OFF:XW_OFF + LANES]).astype(BF16), w2_ref[...]) for x in lora]
    rate = [_dot(x[:, XA_OFF:XA_OFF + LANES].astype(BF16), a2_ref[...]) for x in lora]
    gate = [_dot(jax.nn.sigmoid(x[:, XG_OFF:]).astype(BF16), g2_ref[...]) for x in lora]
    kks = [x[:, A_WIDTH:2 * A_WIDTH] * kk_ref[...] for x in pm]
    sq = [_dot((kk * kk).astype(BF16), ones_ref[...]) for kk in kks]
    for j, p in enumerate(parts):
        k = pm[j][:, A_WIDTH:2 * A_WIDTH]
        z = -(w0_ref[...] + decay[j])
        softplus = jnp.maximum(z, 0.0) + jnp.log(1.0 + jnp.exp(-jnp.abs(z)))
        w = -softplus - 0.5
        a_lr = jax.nn.sigmoid(a0_ref[...] + rate[j])
        kk = kks[j] / jnp.maximum(jnp.sqrt(sq[j]), 1e-12)
        r_out[p, :] = pm[j][:, 0:A_WIDTH]
        k_out[p, :] = k * (1.0 + (a_lr - 1.0) * ka_ref[...])
        v_out[p, :] = pm[j][:, 2 * A_WIDTH:3 * A_WIDTH]
        a_out[p, :] = -kk
        b_out[p, :] = kk * a_lr
        ld_out[p, :] = -jnp.exp(w)
        g_out[p, :] = gate[j]


def _gmlp_kernel(x_ref, g1_ref, wb_ref, wg_ref, bg_ref, lng_ref, lnb_ref, ws_ref, bs_ref,
                 wo_ref, ybg_out, ga_out):
    tm = x_ref.shape[0]
    n_split = tm // DENSE_SPLIT
    parts = [slice(i * DENSE_SPLIT, (i + 1) * DENSE_SPLIT) for i in range(n_split)]
    g1 = g1_ref[...]
    h = [_rms(x_ref[p, :], g1).astype(BF16) for p in parts]
    pb = [_dot(hh, wb_ref[...]) for hh in h]
    tri = (lax.broadcasted_iota(jnp.int32, (GMLP_BLOCK, GMLP_BLOCK), 0)
           >= lax.broadcasted_iota(jnp.int32, (GMLP_BLOCK, GMLP_BLOCK), 1))
    ws = [jnp.where(tri, ws_ref[grp], 0.0).astype(BF16) for grp in range(B_GROUPS)]
    bs = bs_ref[...]
    us, vns = [], []
    for x in pb:
        z = 0.5 * x * (1.0 + lax.erf(x * (2.0 ** -0.5)))
        us.append(z[:, :B_WIDTH])
        v = z[:, B_WIDTH:]
        mean = jnp.mean(v, axis=-1, keepdims=True)
        vc = v - mean
        var = jnp.mean(vc * vc, axis=-1, keepdims=True)
        vns.append((vc * lax.rsqrt(var + LN_EPS) * lng_ref[...] + lnb_ref[...]).astype(BF16))
    svs = []
    for vn in vns:
        rows = []
        for blk in range(DENSE_SPLIT // GMLP_BLOCK):
            cols = [_dot(ws[grp], vn[blk * GMLP_BLOCK:(blk + 1) * GMLP_BLOCK,
                                     grp * B_GROUP_CH:(grp + 1) * B_GROUP_CH])
                    for grp in range(B_GROUPS)]
            rows.append(jnp.concatenate(cols, axis=1) + bs)
        svs.append(jnp.concatenate(rows, axis=0))
    yb = [_dot((u * sv).astype(BF16), wo_ref[...]) for u, sv in zip(us, svs)]
    gates = [jax.nn.sigmoid(_dot(hh, wg_ref[...]) + bg_ref[...]) for hh in h]
    for p, g, y in zip(parts, gates, yb):
        ga_out[p, :] = g[:, :D_MODEL].astype(BF16)
        ybg_out[p, :] = (g[:, D_MODEL:] * y).astype(BF16)


def _wkv_kernel(r_ref, k_ref, v_ref, a_ref, b_ref, ld_ref, g_ref, rk_ref, lng_ref, lnb_ref,
                ones_ref, o_ref, st_ref, ta_s, tl_s, arb_s, ark_s, rt_s, bkt_s, dcol_s):
    C = WKV_CHUNK
    bb, tb, _ = r_ref.shape
    n_chunks = tb // C
    n_pairs = A_WIDTH // HEAD_PAIR

    @pl.when(pl.program_id(1) == 0)
    def _():
        st_ref[...] = jnp.zeros_like(st_ref)

    row = lax.broadcasted_iota(jnp.int32, (C, HEAD_PAIR), 0)
    src = lax.broadcasted_iota(jnp.int32, (C, HEAD_PAIR), 1) & (C - 1)
    incl = src <= row
    strict = src < row
    eye_pair = jnp.where(src == row, 1.0, 0.0)
    bd_mask = ((lax.broadcasted_iota(jnp.int32, (HEAD_PAIR, HEAD_PAIR), 0) >= A_HEAD)
               == (lax.broadcasted_iota(jnp.int32, (HEAD_PAIR, HEAD_PAIR), 1) >= A_HEAD))
    tri_c = (lax.broadcasted_iota(jnp.int32, (C, C), 0)
             >= lax.broadcasted_iota(jnp.int32, (C, C), 1)).astype(BF16)
    ones_pair = ones_ref[...]
    pair_cols = [slice(p * HEAD_PAIR, (p + 1) * HEAD_PAIR) for p in range(n_pairs)]

    def bd(x):
        xb = x.astype(BF16)
        return jnp.where(bd_mask, jnp.concatenate([xb, xb], axis=0), jnp.zeros((), BF16))

    def prep(it, carry):
        where, lhs, rhs, ats = [], [], [], []
        for j in range(WKV_PREP_GROUP):
            flat = it * WKV_PREP_GROUP + j
            b = flat // n_chunks
            c = flat % n_chunks
            sl = pl.ds(pl.multiple_of(c * C, C), C)
            ld = ld_ref[b, sl, :]
            hi, mid, lo = _split3(ld)
            cl = _dot(tri_c, hi) + _dot(tri_c, mid) + _dot(tri_c, lo)
            cl_end = cl[C - 1:C, :]
            k = k_ref[b, sl, :]
            bv = b_ref[b, sl, :]
            d_inv = jnp.exp(-cl)
            d_tail = jnp.exp(cl_end - cl)
            rt = (r_ref[b, sl, :] * jnp.exp(cl)).astype(BF16)
            rt_s[b, sl, :] = rt
            kt = k * d_inv
            bt = bv * d_inv
            at = a_ref[b, sl, :] * jnp.exp(cl - ld)
            kd = k * d_tail
            bdk = bv * d_tail
            d_end = jnp.exp(cl_end)
            for p, cs in enumerate(pair_cols):
                bkt_s[b, c, p] = jnp.concatenate([bdk[:, cs], kd[:, cs]], axis=0).T.astype(BF16)
                dcol_s[b, c, p] = jnp.broadcast_to(d_end[:, cs], (HEAD_PAIR, HEAD_PAIR)).T
                where.append((b, sl, cs))
                ats.append(at[:, cs])
                lhs.append(jnp.concatenate([rt[:, cs], at[:, cs].astype(BF16)], axis=0))
                rhs.append(jnp.concatenate([bd(kt[:, cs]), bd(bt[:, cs])], axis=0))
        n = len(where)
        amat = [_dot_nt(lhs[i], rhs[i]) for i in range(n)]
        l_ak, l_ab = [], []
        for i, (b, sl, cs) in enumerate(where):
            ark_s[b, sl, cs] = jnp.where(incl, amat[i][:C, :HEAD_PAIR], 0.0).astype(BF16)
            arb_s[b, sl, cs] = jnp.where(incl, amat[i][:C, HEAD_PAIR:], 0.0).astype(BF16)
            l_ak.append(jnp.where(strict, amat[i][C:, :HEAD_PAIR], 0.0))
            l_ab.append(jnp.where(strict, amat[i][C:, HEAD_PAIR:], 0.0))

        t_mat = [eye_pair + l for l in l_ab]
        q = [_dot(l.astype(BF16), bd(l)) for l in l_ab]
        n_sq = 1
        while 2 * n_sq < C // 2:
            both = [_dot(q[i].astype(BF16), jnp.concatenate([bd(q[i]), bd(t_mat[i])], axis=1))
                    for i in range(n)]
            q = [x[:, :HEAD_PAIR] for x in both]
            t_mat = [t_mat[i] + both[i][:, HEAD_PAIR:] for i in range(n)]
            n_sq *= 2
        t_mat = [t_mat[i] + _dot(q[i].astype(BF16), bd(t_mat[i])) for i in range(n)]
        tal = [_dot(t_mat[i].astype(BF16), jnp.concatenate([bd(ats[i]), bd(l_ak[i])], axis=1))
               for i in range(n)]
        for i, (b, sl, cs) in enumerate(where):
            ta_s[b, sl, cs] = tal[i][:, :HEAD_PAIR].astype(BF16)
            tl_s[b, sl, cs] = tal[i][:, HEAD_PAIR:].astype(BF16)
        return carry

    lax.fori_loop(0, bb * n_chunks // WKV_PREP_GROUP, prep, 0)

    def step(c, carry):
        sl = pl.ds(pl.multiple_of(c * C, C), C)
        chains = [(b, p) for b in range(bb) for p in range(n_pairs)]
        v = [v_ref[b, sl, :] for b in range(bb)]
        st = [st_ref[b, p] for b, p in chains]
        st_b = [x.astype(BF16) for x in st]
        bd_v = [bd(v[b][:, pair_cols[p]]) for b, p in chains]
        on_st = [_dot(jnp.concatenate([ta_s[b, sl, pair_cols[p]], rt_s[b, sl, pair_cols[p]]],
                                      axis=0), st_b[i]) for i, (b, p) in enumerate(chains)]
        on_v = [_dot(jnp.concatenate([tl_s[b, sl, pair_cols[p]], ark_s[b, sl, pair_cols[p]]],
                                     axis=0), bd_v[i]) for i, (b, p) in enumerate(chains)]
        u = [on_st[i][:C] + on_v[i][:C] for i in range(len(chains))]
        for i, (b, p) in enumerate(chains):
            uv = jnp.concatenate([u[i], v[b][:, pair_cols[p]]], axis=0).astype(BF16)
            st_ref[b, p] = (dcol_s[b, c, p] * st[i]
                            + jnp.where(bd_mask, _dot(bkt_s[b, c, p], uv), 0.0))
        ys = [on_st[i][C:] + on_v[i][C:] + _dot(arb_s[b, sl, pair_cols[p]], bd(u[i]))
              for i, (b, p) in enumerate(chains)]
        for b in range(bb):
            o_ref[b, sl, :] = jnp.concatenate(ys[b * n_pairs:(b + 1) * n_pairs], axis=1)
        return carry

    lax.fori_loop(0, n_chunks, step, 0)

    def finish(it, carry):
        items = [(b, pl.ds(pl.multiple_of((it * WKV_NORM_GROUP + j) * C, C), C))
                 for j in range(WKV_NORM_GROUP) for b in range(bb)]
        y = [o_ref[b, sl, :] for b, sl in items]
        rkr = [r_ref[b, sl, :] * k_ref[b, sl, :] * rk_ref[...] for b, sl in items]
        sums = [_pair_head_sums([y[i], rkr[i]], ones_pair) for i in range(len(items))]
        yc = [y[i] - sums[i][0] * (1.0 / A_HEAD) for i in range(len(items))]
        var = [_pair_head_sums([x * x], ones_pair)[0] * (1.0 / A_HEAD) for x in yc]
        for i, (b, sl) in enumerate(items):
            yn = yc[i] * lax.rsqrt(var[i] + LNX_EPS) * lng_ref[...] + lnb_ref[...]
            o_ref[b, sl, :] = (yn + sums[i][1] * v_ref[b, sl, :]) * g_ref[b, sl, :]
        return carry

    lax.fori_loop(0, n_chunks // WKV_NORM_GROUP, finish, 0)


def _post_kernel(x_ref, ya_ref, ga_ref, ybg_ref, woa_ref, wout_ref, g2_ref, wr_ref,
                 wr_hi_ref, br_ref, x2_out, hl_out, route_out, wts_out, cnt_out):
    n_sub = x_ref.shape[0] // MOE_SUB
    groups = [slice(g * MOE_SUB, (g + 1) * MOE_SUB) for g in range(n_sub)]
    y_a = [_dot(ya_ref[p, :].astype(BF16), woa_ref[...]) for p in groups]
    x2 = [x_ref[p, :] + _dot((ga_ref[p, :] * y + ybg_ref[p, :]).astype(BF16), wout_ref[...])
          for p, y in zip(groups, y_a)]
    h2 = [_rms(x, g2_ref[...]) for x in x2]
    split = [_split2(h) for h in h2]
    hi_w = [_dot(hi, wr_ref[...]) for hi, _ in split]
    lo_w = [_dot(lo, wr_hi_ref[...]) for _, lo in split]
    lane = lax.broadcasted_iota(jnp.int32, (MOE_SUB, LANES), 1)
    neg = jnp.float32(-jnp.inf)
    big = jnp.int32(LANES)
    is_grp = (lane >= N_EXPERTS) & (lane < N_EXPERTS + N_GROUPS)
    tri = (lax.broadcasted_iota(jnp.int32, (MOE_SUB, MOE_SUB), 0)
           >= lax.broadcasted_iota(jnp.int32, (MOE_SUB, MOE_SUB), 1)).astype(BF16)
    picks, onehots = [], []
    for g, p in enumerate(groups):
        x2_out[p, :] = x2[g]
        logits = hi_w[g][:, :LANES] + hi_w[g][:, LANES:] + lo_w[g] + br_ref[...]
        gl = jnp.where(is_grp, logits, neg)
        gmax = jnp.max(gl, axis=-1, keepdims=True)
        g_p = 1.0 / jnp.sum(jnp.exp(gl - gmax), axis=-1, keepdims=True)
        g_idx = jnp.min(jnp.where(gl == gmax, lane, big), axis=-1, keepdims=True) - N_EXPERTS
        lo_lane = g_idx * EXPERTS_PER_GROUP
        in_grp = (lane >= lo_lane) & (lane < lo_lane + EXPERTS_PER_GROUP)
        el = jnp.where(in_grp, logits, neg)
        e1 = jnp.max(el, axis=-1, keepdims=True)
        i1 = jnp.min(jnp.where(el == e1, lane, big), axis=-1, keepdims=True)
        el2 = jnp.where(lane == i1, neg, el)
        e2 = jnp.max(el2, axis=-1, keepdims=True)
        i2 = jnp.min(jnp.where(el2 == e2, lane, big), axis=-1, keepdims=True)
        t = jnp.exp(e2 - e1)
        wts_out[p, :] = jnp.where(lane == 0, g_p / (1.0 + t),
                                  jnp.where(lane == 1, g_p * t / (1.0 + t), 0.0))
        picks.append((i1, i2))
        onehots.append(jnp.where((lane == i1) | (lane == i2), 1.0, 0.0))

    incl = [_dot(tri, oh.astype(BF16)) for oh in onehots]
    row8 = lax.broadcasted_iota(jnp.int32, (8, LANES), 0)
    chunks = jnp.zeros((8, LANES), F32)
    for g in range(n_sub):
        chunks = jnp.where(row8 == g, jnp.ceil(incl[g][MOE_SUB - 1:, :] * (1.0 / RUN_ALIGN)),
                           chunks)
    cnt_out[...] = chunks
    upper = (lax.broadcasted_iota(jnp.int32, (LANES, LANES), 0)
             < lax.broadcasted_iota(jnp.int32, (LANES, LANES), 1)).astype(BF16)
    run_start = _dot(chunks.astype(BF16), upper) * RUN_ALIGN
    routes = []
    for g, p in enumerate(groups):
        slot_of = run_start[g:g + 1, :] + incl[g] - onehots[g]
        i1, i2 = picks[g]
        slot1 = jnp.sum(jnp.where(lane == i1, slot_of, 0.0), axis=-1, keepdims=True)
        slot2 = jnp.sum(jnp.where(lane == i2, slot_of, 0.0), axis=-1, keepdims=True)
        route = jnp.where(lane == 0, slot1, jnp.where(lane == 1, slot2, 0.0)).astype(jnp.int32)
        route_out[p, :] = route
        routes.append(route.T)

    slot_ids = lax.broadcasted_iota(jnp.int32, (RUN_SLOTS, MOE_SUB), 0)
    for g in range(n_sub):
        pick = jnp.where((slot_ids == routes[g][0:1, :]) | (slot_ids == routes[g][1:2, :]),
                         1.0, 0.0).astype(BF16)
        hl_out[g * RUN_SLOTS:(g + 1) * RUN_SLOTS, :] = _dot(pick, h2[g].astype(BF16)).astype(BF16)


def _expert_kernel(chunk_s, first_s, hl_in_ref, wg_ref, wu_ref, wd_ref, hl_ref, xbuf, wgu_s, wd_s,
                   sem, drain_s):
    del hl_in_ref
    e = pl.program_id(0)
    pass_chunks = EXPERT_ROWS // RUN_ALIGN
    mine = e % 2
    other = 1 - mine
    gather_sem = lambda buf: sem.at[buf]
    scatter_sem = lambda buf: sem.at[2 + buf]

    def first_pass(ex):
        lo = first_s[ex]
        return lo, jnp.minimum(lo + pass_chunks, first_s[ex + 1])

    def for_chunks(buf, c_lo, c_hi, fn):
        def one(c):
            fn(hl_ref.at[pl.ds(pl.multiple_of(chunk_s[c], RUN_ALIGN), RUN_ALIGN)],
               xbuf.at[buf, pl.ds(pl.multiple_of((c - c_lo) * RUN_ALIGN, RUN_ALIGN), RUN_ALIGN)])

        def block(i, carry):
            for u in range(CHUNK_UNROLL):
                one(c_lo + i * CHUNK_UNROLL + u)
            return carry

        def tail(c, carry):
            one(c)
            return carry

        n_blocks = (c_hi - c_lo) // CHUNK_UNROLL
        lax.fori_loop(0, n_blocks, block, 0)
        lax.fori_loop(c_lo + n_blocks * CHUNK_UNROLL, c_hi, tail, 0)

    def gather(buf, c_lo, c_hi):
        for_chunks(buf, c_lo, c_hi, lambda hbm, vm: pltpu.make_async_copy(
            hbm, vm, gather_sem(buf)).start())

    def scatter(buf, c_lo, c_hi):
        for_chunks(buf, c_lo, c_hi, lambda hbm, vm: pltpu.make_async_copy(
            vm, hbm, scatter_sem(buf)).start())

    def wait_chunks(which_sem, n):
        def wait_rows(rows):
            def body(c, carry):
                pltpu.make_async_copy(hl_ref.at[pl.ds(0, rows)], xbuf.at[0, pl.ds(0, rows)],
                                      which_sem).wait()
                return carry
            return body

        lax.fori_loop(0, n // CHUNK_UNROLL, wait_rows(CHUNK_UNROLL * RUN_ALIGN), 0)
        lax.fori_loop(0, n % CHUNK_UNROLL, wait_rows(RUN_ALIGN), 0)

    def evaluate(buf, n_chunks):
        def tile(k, carry):
            sl = pl.ds(pl.multiple_of(k * EXPERT_TILE, EXPERT_TILE), EXPERT_TILE)
            gu = _dot(xbuf[buf, sl, :], wgu_s[...])
            gate = gu[:, :D_EXPERT]
            act = gate * jax.nn.sigmoid(gate) * gu[:, D_EXPERT:]
            xbuf[buf, sl, :] = _dot(act.astype(BF16), wd_s[...]).astype(BF16)
            return carry

        lax.fori_loop(0, (n_chunks * RUN_ALIGN + EXPERT_TILE - 1) // EXPERT_TILE, tile, 0)

    @pl.when(e == 0)
    def _():
        xbuf[...] = jnp.zeros_like(xbuf)
        gather(0, *first_pass(0))
        drain_s[0] = 0

    wgu_s[:, :D_EXPERT] = wg_ref[0].astype(BF16)
    wgu_s[:, D_EXPERT:] = wu_ref[0].astype(BF16)
    wd_s[...] = wd_ref[0].astype(BF16)

    c_lo, c_hi = first_pass(e)
    wait_chunks(gather_sem(mine), c_hi - c_lo)
    evaluate(mine, c_hi - c_lo)
    wait_chunks(scatter_sem(other), drain_s[0])
    scatter(mine, c_lo, c_hi)

    def later_pass(lo):
        hi = jnp.minimum(lo + pass_chunks, first_s[e + 1])
        gather(other, lo, hi)
        wait_chunks(gather_sem(other), hi - lo)
        evaluate(other, hi - lo)
        scatter(other, lo, hi)
        wait_chunks(scatter_sem(other), hi - lo)
        return hi

    lax.while_loop(lambda c: c < first_s[e + 1], later_pass, c_hi)
    drain_s[0] = c_hi - c_lo

    @pl.when(e + 1 < pl.num_programs(0))
    def _():
        gather(other, *first_pass(e + 1))

    @pl.when(e + 1 == pl.num_programs(0))
    def _():
        wait_chunks(scatter_sem(mine), c_hi - c_lo)


def _combine_kernel(x2_ref, route_ref, wts_ref, gf_ref, yl_ref, o_ref):
    route = route_ref[...]
    wts = wts_ref[...]
    slots = lax.broadcasted_iota(jnp.int32, (x2_ref.shape[0], RUN_SLOTS), 1)
    pick = (jnp.where(slots == route[:, 0:1], wts[:, 0:1], 0.0)
            + jnp.where(slots == route[:, 1:2], wts[:, 1:2], 0.0))
    y = _dot(pick.astype(BF16), yl_ref[...])
    o_ref[...] = _rms(x2_ref[...] + y, gf_ref[...])


def _full(shape):
    return pl.BlockSpec(shape, lambda *_: (0,) * len(shape))


def _params(sem):
    return pltpu.CompilerParams(dimension_semantics=sem, vmem_limit_bytes=VMEM_LIMIT)


def _mixer_a_layout(t):
    s_rkv = 3 * A_WIDTH
    s_w = s_rkv + D_DECAY_LORA
    s_a = s_w + D_AAA_LORA
    gap = lambda n: jnp.zeros((t.shape[0], n), t.dtype)
    return jnp.concatenate(
        [t[:, :s_w], gap(XA_OFF - D_DECAY_LORA), t[:, s_w:s_a], gap(XG_OFF - XA_OFF - D_AAA_LORA),
         t[:, s_a:], gap(LORA_PAD - XG_OFF - D_GATE_LORA)], axis=1)


def _place(cols, parts):
    out = jnp.zeros((parts[0][1].shape[0], cols), parts[0][1].dtype)
    for off, arr in parts:
        out = lax.dynamic_update_slice(out, arr, (0, off))
    return out


def kernel(x, norm1_g, w_in, b_gate, tmix_mu, w0, w2, a0, a2, g2, k_k, k_a, r_k, lnx_g, lnx_b,
           w_oA, lnv_g, lnv_b, w_s, b_s, w_oB, w_out, norm2_g, w_rg, b_rg, w_re, b_re,
           w_e_gate, w_e_up, w_e_down, final_g):
    bsz, seq, d = x.shape
    n_tok = bsz * seq
    depth = norm1_g.shape[0]
    assert depth == 1, "the moe kernel fuses the final norm, so it must be the last layer"
    assert bsz % WKV_SEQS == 0 and seq % WKV_TOKENS == 0
    xf = x.reshape(n_tok, d)

    s_rkv = 3 * A_WIDTH
    s_w = s_rkv + D_DECAY_LORA
    s_a = s_w + D_AAA_LORA
    a_cols = s_a + D_GATE_LORA
    b_cols = 2 * B_WIDTH

    ones_bd = (jnp.arange(A_WIDTH)[:, None] // A_HEAD
               == jnp.arange(A_WIDTH)[None, :] // A_HEAD).astype(BF16)

    tm_a = 512
    tm_b = 512
    tm_p = 512

    for l in range(depth):
        wl = w_in[l]
        w_a = _mixer_a_layout(wl[:, :a_cols]).astype(BF16)
        mu_a = _mixer_a_layout(tmix_mu[l][None, :])
        w2p = jnp.pad(w2[l], ((0, LANES - D_DECAY_LORA), (0, 0))).astype(BF16)
        a2p = jnp.pad(a2[l], ((0, LANES - D_AAA_LORA), (0, 0))).astype(BF16)
        g2p = jnp.pad(g2[l], ((0, LORA_PAD - XG_OFF - D_GATE_LORA), (0, 0))).astype(BF16)
        g1 = norm1_g[l][None, :]

        row512 = lambda i: (i, 0)
        tok_a = pl.BlockSpec((tm_a, A_WIDTH), row512)
        vec_a = _full((1, A_WIDTH))
        outs = pl.pallas_call(
            functools.partial(_rwkv_prep_kernel, seq // tm_a),
            grid=(n_tok // tm_a,),
            in_specs=[pl.BlockSpec((tm_a, d), row512),
                      _full((1, d)), _full((d, A_PROJ)), _full((1, A_PROJ)), vec_a,
                      _full((LANES, A_WIDTH)), vec_a, _full((LANES, A_WIDTH)),
                      _full((LORA_PAD - XG_OFF, A_WIDTH)), vec_a, vec_a,
                      _full((A_WIDTH, A_WIDTH))],
            out_specs=[tok_a] * 7,
            out_shape=[jax.ShapeDtypeStruct((n_tok, A_WIDTH), F32)] * 7,
            scratch_shapes=[pltpu.VMEM((8, A_PROJ), F32)],
            compiler_params=_params(("arbitrary",)),
            name="rwkv_prep",
        )(xf, g1, w_a, mu_a, w0[l][None, :], w2p, a0[l][None, :], a2p, g2p,
          k_k[l][None, :], k_a[l][None, :], ones_bd)
        r_, k_, v_, a_, b_, ld_, gg_ = outs

        w_b = wl[:, a_cols:a_cols + b_cols].astype(BF16)
        w_g = wl[:, a_cols + b_cols:].astype(BF16)
        bs_full = jnp.repeat(b_s[l].T, B_GROUP_CH, axis=1)
        tok_d = pl.BlockSpec((tm_b, d), row512)
        ybg, ga = pl.pallas_call(
            _gmlp_kernel,
            grid=(n_tok // tm_b,),
            in_specs=[tok_d, _full((1, d)), _full((d, b_cols)), _full((d, 2 * d)),
                      _full((1, 2 * d)), _full((1, B_WIDTH)), _full((1, B_WIDTH)),
                      _full((B_GROUPS, GMLP_BLOCK, GMLP_BLOCK)), _full((GMLP_BLOCK, B_WIDTH)),
                      _full((B_WIDTH, d))],
            out_specs=[tok_d, tok_d],
            out_shape=[jax.ShapeDtypeStruct((n_tok, d), BF16)] * 2,
            compiler_params=_params(("parallel",)),
            name="gmlp",
        )(xf, g1, w_b, w_g, b_gate[l][None, :], lnv_g[l][None, :], lnv_b[l][None, :],
          w_s[l], bs_full, w_oB[l].astype(BF16))

        n_pairs = A_WIDTH // HEAD_PAIR
        n_chunks = WKV_TOKENS // WKV_CHUNK
        tok_w = pl.BlockSpec((WKV_SEQS, WKV_TOKENS, A_WIDTH), lambda bi, ti: (bi, ti, 0))
        vec_w = _full((1, A_WIDTH))
        seq_major = lambda t: t.reshape(bsz, seq, A_WIDTH)
        tok_scratch = pltpu.VMEM((WKV_SEQS, WKV_TOKENS, A_WIDTH), BF16)
        ya_in = pl.pallas_call(
            _wkv_kernel,
            grid=(bsz // WKV_SEQS, seq // WKV_TOKENS),
            in_specs=[tok_w] * 7 + [vec_w, vec_w, vec_w, _full((HEAD_PAIR, HEAD_PAIR))],
            out_specs=tok_w,
            out_shape=jax.ShapeDtypeStruct((bsz, seq, A_WIDTH), F32),
            scratch_shapes=[pltpu.VMEM((WKV_SEQS, n_pairs, HEAD_PAIR, HEAD_PAIR), F32)]
            + [tok_scratch] * 5
            + [pltpu.VMEM((WKV_SEQS, n_chunks, n_pairs, HEAD_PAIR, HEAD_PAIR), BF16),
               pltpu.VMEM((WKV_SEQS, n_chunks, n_pairs, HEAD_PAIR, HEAD_PAIR), F32)],
            compiler_params=_params(("parallel", "arbitrary")),
            name="wkv",
        )(*[seq_major(t) for t in (r_, k_, v_, a_, b_, ld_, gg_)], r_k[l].reshape(1, A_WIDTH),
          lnx_g[l][None, :], lnx_b[l][None, :],
          ones_bd[:HEAD_PAIR, :HEAD_PAIR]).reshape(n_tok, A_WIDTH)

        w_r = _place(LANES, [(0, jnp.transpose(w_re[l], (1, 0, 2)).reshape(d, N_EXPERTS)),
                             (N_EXPERTS, w_rg[l])])
        b_r = _place(LANES, [(0, b_re[l].reshape(1, N_EXPERTS)), (N_EXPERTS, b_rg[l][None, :])])
        wr_hi = w_r.astype(BF16)
        wr_lo = (w_r - wr_hi.astype(F32)).astype(BF16)
        tok_p = pl.BlockSpec((tm_p, d), row512)
        lane_p = pl.BlockSpec((tm_p, LANES), row512)
        n_sub = n_tok // MOE_SUB
        sorted_rows = tm_p // MOE_SUB * RUN_SLOTS
        x2, h_sorted, route, wts, cnt = pl.pallas_call(
            _post_kernel,
            grid=(n_tok // tm_p,),
            in_specs=[tok_p, pl.BlockSpec((tm_p, A_WIDTH), row512), tok_p, tok_p,
                      _full((A_WIDTH, d)), _full((d, d)), _full((1, d)), _full((d, 2 * LANES)),
                      _full((d, LANES)), _full((1, LANES))],
            out_specs=[tok_p, pl.BlockSpec((sorted_rows, d), row512), lane_p, lane_p,
                       pl.BlockSpec((8, LANES), row512)],
            out_shape=[jax.ShapeDtypeStruct((n_tok, d), F32),
                       jax.ShapeDtypeStruct((n_sub * RUN_SLOTS, d), BF16),
                       jax.ShapeDtypeStruct((n_tok, LANES), jnp.int32),
                       jax.ShapeDtypeStruct((n_tok, LANES), F32),
                       jax.ShapeDtypeStruct((n_tok // tm_p * 8, LANES), F32)],
            compiler_params=_params(("parallel",)),
            name="post",
        )(xf, ya_in, ga, ybg, w_oA[l].astype(BF16), w_out[l].astype(BF16),
          norm2_g[l][None, :], jnp.concatenate([wr_hi, wr_lo], axis=1), wr_hi, b_r)

        run_chunks = cnt.reshape(n_tok // tm_p, 8, LANES)[:, :tm_p // MOE_SUB, :N_EXPERTS]
        run_chunks = run_chunks.reshape(n_sub, N_EXPERTS).astype(jnp.int32)
        run_start = (jnp.cumsum(run_chunks, axis=1) - run_chunks) * RUN_ALIGN
        run_row = (jnp.arange(n_sub, dtype=jnp.int32)[:, None] * RUN_SLOTS + run_start).T.reshape(-1)
        run_n = run_chunks.T.reshape(-1)
        run_end = jnp.cumsum(run_n)
        first_chunk = jnp.concatenate([jnp.zeros((1,), jnp.int32), run_end[n_sub - 1::n_sub]])
        pos = jnp.arange(n_sub * RUN_SLOTS // RUN_ALIGN, dtype=jnp.int32)
        run_of = jnp.sum(pos[:, None] >= run_end[None, :], axis=1)
        in_run = run_of[:, None] == jnp.arange(run_n.shape[0], dtype=jnp.int32)[None, :]
        pick = lambda t: jnp.sum(jnp.where(in_run, t[None, :], 0), axis=1)
        chunk_rows = pick(run_row) + (pos - pick(run_end - run_n)) * RUN_ALIGN

        per_expert = lambda e, *_: (e, 0, 0)
        y_sorted = pl.pallas_call(
            _expert_kernel,
            grid_spec=pltpu.PrefetchScalarGridSpec(
                num_scalar_prefetch=2, grid=(N_EXPERTS,),
                in_specs=[pl.BlockSpec(memory_space=pl.ANY),
                          pl.BlockSpec((1, d, D_EXPERT), per_expert),
                          pl.BlockSpec((1, d, D_EXPERT), per_expert),
                          pl.BlockSpec((1, D_EXPERT, d), per_expert)],
                out_specs=pl.BlockSpec(memory_space=pl.ANY),
                scratch_shapes=[pltpu.VMEM((2, EXPERT_ROWS, d), BF16),
                                pltpu.VMEM((d, 2 * D_EXPERT), BF16),
                                pltpu.VMEM((D_EXPERT, d), BF16),
                                pltpu.SemaphoreType.DMA((4,)),
                                pltpu.SMEM((1,), jnp.int32)]),
            out_shape=jax.ShapeDtypeStruct((n_sub * RUN_SLOTS, d), BF16),
            input_output_aliases={2: 0},
            compiler_params=_params(("arbitrary",)),
            name="moe_experts",
        )(chunk_rows, first_chunk, h_sorted, w_e_gate[l], w_e_up[l], w_e_down[l])

        sub_rows = lambda i: (i, 0)
        xf = pl.pallas_call(
            _combine_kernel,
            grid=(n_sub,),
            in_specs=[pl.BlockSpec((MOE_SUB, d), sub_rows),
                      pl.BlockSpec((MOE_SUB, LANES), sub_rows),
                      pl.BlockSpec((MOE_SUB, LANES), sub_rows),
                      _full((1, d)),
                      pl.BlockSpec((RUN_SLOTS, d), sub_rows)],
            out_specs=pl.BlockSpec((MOE_SUB, d), sub_rows),
            out_shape=jax.ShapeDtypeStruct((n_tok, d), F32),
            compiler_params=_params(("parallel",)),
            name="moe_combine",
        )(x2, route, wts, final_g[None, :], y_sorted)

    return xf.reshape(bsz, seq, d)
```

```python
import functools

import jax
import jax.numpy as jnp
from jax import lax
from jax.experimental import pallas as pl
from jax.experimental.pallas import tpu as pltpu

F32 = jnp.float32
BF16 = jnp.bfloat16

D_MODEL = 1024
A_WIDTH = 512
A_HEAD = 64
D_DECAY_LORA = 64
D_AAA_LORA = 64
D_GATE_LORA = 160
B_WIDTH = 512
B_GROUPS = 4
B_GROUP_CH = 128
GMLP_BLOCK = 128
N_GROUPS = 4
EXPERTS_PER_GROUP = 8
N_EXPERTS = 32
D_EXPERT = 256
NORM_EPS = 1e-6
LN_EPS = 1e-5
LNX_EPS = 64e-5

LANES = 128
LORA_PAD = 512
XW_OFF, XA_OFF, XG_OFF = 0, 128, 256
A_PROJ = 3 * A_WIDTH + LORA_PAD
WKV_CHUNK = 64
HEAD_PAIR = 2 * A_HEAD
WKV_SEQS = 4
WKV_TOKENS = 256
WKV_PREP_GROUP = 4
WKV_NORM_GROUP = 2
MOE_SUB = 256
RUN_ALIGN = 16
RUN_SLOTS = 2 * MOE_SUB + N_EXPERTS * RUN_ALIGN
EXPERT_ROWS = 2048
EXPERT_TILE = 512
DENSE_SPLIT = 256
CHUNK_UNROLL = 4
VMEM_LIMIT = 48 * 1024 * 1024


def _rms(x, g):
    return x * lax.rsqrt(jnp.mean(x * x, axis=-1, keepdims=True) + NORM_EPS) * g


def _dot(a, b):
    return jnp.dot(a, b, preferred_element_type=F32)


def _dot_nt(a, b):
    return lax.dot_general(a, b, (((1,), (1,)), ((), ())), preferred_element_type=F32)


def _split2(x):
    hi = x.astype(BF16)
    lo = (x - hi.astype(F32)).astype(BF16)
    return hi, lo


def _split3(x):
    hi = x.astype(BF16)
    r1 = x - hi.astype(F32)
    mid = r1.astype(BF16)
    lo = (r1 - mid.astype(F32)).astype(BF16)
    return hi, mid, lo


def _pair_head_sums(xs, ones_pair):
    n_tiles = A_WIDTH // HEAD_PAIR
    rows = xs[0].shape[0]
    tiles = [x[:, t * HEAD_PAIR:(t + 1) * HEAD_PAIR].astype(BF16) for x in xs for t in range(n_tiles)]
    sums = _dot(jnp.concatenate(tiles, axis=0), ones_pair)
    return [jnp.concatenate([sums[(i * n_tiles + t) * rows:(i * n_tiles + t + 1) * rows]
                             for t in range(n_tiles)], axis=1) for i in range(len(xs))]


def _rwkv_prep_kernel(tiles_per_seq, x_ref, g1_ref, wa_ref, mu_ref, w0_ref,
                      w2_ref, a0_ref, a2_ref, g2_ref, kk_ref, ka_ref, ones_ref,
                      r_out, k_out, v_out, a_out, b_out, ld_out, g_out, tail_ref):
    i = pl.program_id(0)
    tm = x_ref.shape[0]
    parts = [slice(j * DENSE_SPLIT, (j + 1) * DENSE_SPLIT) for j in range(tm // DENSE_SPLIT)]
    g1 = g1_ref[...]
    h = [_rms(x_ref[p, :], g1).astype(BF16) for p in parts]
    proj = [_dot(hh, wa_ref[...]) for hh in h]
    prev = [jnp.where(i % tiles_per_seq == 0, 0.0, tail_ref[7:8, :])]
    prev += [p[DENSE_SPLIT - 1:, :] for p in proj[:-1]]
    tail_ref[...] = proj[-1][DENSE_SPLIT - 8:, :]
    row = lax.broadcasted_iota(jnp.int32, proj[0].shape, 0)
    pm = []
    for p, pv in zip(proj, prev):
        shifted = jnp.where(row == 0, pv, pltpu.roll(p, 1, axis=0))
        pm.append(p + mu_ref[...] * (shifted - p))
    lora = [x[:, 3 * A_WIDTH:] for x in pm]
    decay = [_dot(jnp.tanh(x[:, XW_OFF:XW_OFF + LANES]).astype(BF16), w2_ref[...]) for x in lora]
    rate = [_dot(x[:, XA_OFF:XA_OFF + LANES].astype(BF16), a2_ref[...]) for x in lora]
    gate = [_dot(jax.nn.sigmoid(x[:, XG_OFF:]).astype(BF16), g2_ref[...]) for x in lora]
    kks = [x[:, A_WIDTH:2 * A_WIDTH] * kk_ref[...] for x in pm]
    sq = [_dot((kk * kk).astype(BF16), ones_ref[...]) for kk in kks]
    for j, p in enumerate(parts):
        k = pm[j][:, A_WIDTH:2 * A_WIDTH]
        z = -(w0_ref[...] + decay[j])
        softplus = jnp.maximum(z, 0.0) + jnp.log(1.0 + jnp.exp(-jnp.abs(z)))
        w = -softplus - 0.5
        a_lr = jax.nn.sigmoid(a0_ref[...] + rate[j])
        kk = kks[j] / jnp.maximum(jnp.sqrt(sq[j]), 1e-12)
        r_out[p, :] = pm[j][:, 0:A_WIDTH].astype(BF16)
        k_out[p, :] = (k * (1.0 + (a_lr - 1.0) * ka_ref[...])).astype(BF16)
        v_out[p, :] = pm[j][:, 2 * A_WIDTH:3 * A_WIDTH].astype(BF16)
        a_out[p, :] = (-kk).astype(BF16)
        b_out[p, :] = (kk * a_lr).astype(BF16)
        ld_out[p, :] = -jnp.exp(w)
        g_out[p, :] = gate[j].astype(BF16)


def _gmlp_kernel(x_ref, g1_ref, wb_ref, wg_ref, bg_ref, lng_ref, lnb_ref, ws_ref, bs_ref,
                 wo_ref, ybg_out, ga_out):
    tm = x_ref.shape[0]
    n_split = tm // DENSE_SPLIT
    parts = [slice(i * DENSE_SPLIT, (i + 1) * DENSE_SPLIT) for i in range(n_split)]
    g1 = g1_ref[...]
    h = [_rms(x_ref[p, :], g1).astype(BF16) for p in parts]
    pb = [_dot(hh, wb_ref[...]) for hh in h]
    tri = (lax.broadcasted_iota(jnp.int32, (GMLP_BLOCK, GMLP_BLOCK), 0)
           >= lax.broadcasted_iota(jnp.int32, (GMLP_BLOCK, GMLP_BLOCK), 1))
    ws = [jnp.where(tri, ws_ref[grp], 0.0).astype(BF16) for grp in range(B_GROUPS)]
    bs = bs_ref[...]
    us, vns = [], []
    for x in pb:
        z = 0.5 * x * (1.0 + lax.erf(x * (2.0 ** -0.5)))
        us.append(z[:, :B_WIDTH])
        v = z[:, B_WIDTH:]
        mean = jnp.mean(v, axis=-1, keepdims=True)
        vc = v - mean
        var = jnp.mean(vc * vc, axis=-1, keepdims=True)
        vns.append((vc * lax.rsqrt(var + LN_EPS) * lng_ref[...] + lnb_ref[...]).astype(BF16))
    svs = []
    for vn in vns:
        rows = []
        for blk in range(DENSE_SPLIT // GMLP_BLOCK):
            cols = [_dot(ws[grp], vn[blk * GMLP_BLOCK:(blk + 1) * GMLP_BLOCK,
                                     grp * B_GROUP_CH:(grp + 1) * B_GROUP_CH])
                    for grp in range(B_GROUPS)]
            rows.append(jnp.concatenate(cols, axis=1) + bs)
        svs.append(jnp.concatenate(rows, axis=0))
    yb = [_dot((u * sv).astype(BF16), wo_ref[...]) for u, sv in zip(us, svs)]
    gates = [jax.nn.sigmoid(_dot(hh, wg_ref[...]) + bg_ref[...]) for hh in h]
    for p, g, y in zip(parts, gates, yb):
        ga_out[p, :] = g[:, :D_MODEL].astype(BF16)
        ybg_out[p, :] = (g[:, D_MODEL:] * y).astype(BF16)


def _wkv_kernel(r_ref, k_ref, v_ref, a_ref, b_ref, ld_ref, g_ref, rk_ref, lng_ref, lnb_ref,
                ones_ref, o_ref, st_ref, ta_s, tl_s, arb_s, ark_s, rt_s, bkt_s, dcol_s):
    C = WKV_CHUNK
    bb, tb, _ = r_ref.shape
    n_chunks = tb // C
    n_pairs = A_WIDTH // HEAD_PAIR

    @pl.when(pl.program_id(1) == 0)
    def _():
        st_ref[...] = jnp.zeros_like(st_ref)

    row = lax.broadcasted_iota(jnp.int32, (C, HEAD_PAIR), 0)
    src = lax.broadcasted_iota(jnp.int32, (C, HEAD_PAIR), 1) & (C - 1)
    incl = src <= row
    strict = src < row
    eye_pair = jnp.where(src == row, 1.0, 0.0)
    bd_mask = ((lax.broadcasted_iota(jnp.int32, (HEAD_PAIR, HEAD_PAIR), 0) >= A_HEAD)
               == (lax.broadcasted_iota(jnp.int32, (HEAD_PAIR, HEAD_PAIR), 1) >= A_HEAD))
    tri_c = (lax.broadcasted_iota(jnp.int32, (C, C), 0)
             >= lax.broadcasted_iota(jnp.int32, (C, C), 1)).astype(BF16)
    ones_pair = ones_ref[...]
    pair_cols = [slice(p * HEAD_PAIR, (p + 1) * HEAD_PAIR) for p in range(n_pairs)]

    def bd(x):
        xb = x.astype(BF16)
        return jnp.where(bd_mask, jnp.concatenate([xb, xb], axis=0), jnp.zeros((), BF16))

    def prep(it, carry):
        where, lhs, rhs, ats = [], [], [], []
        for j in range(WKV_PREP_GROUP):
            flat = it * WKV_PREP_GROUP + j
            b = flat // n_chunks
            c = flat % n_chunks
            sl = pl.ds(pl.multiple_of(c * C, C), C)
            ld = ld_ref[b, sl, :]
            hi, mid, lo = _split3(ld)
            cl = _dot(tri_c, hi) + _dot(tri_c, mid) + _dot(tri_c, lo)
            cl_end = cl[C - 1:C, :]
            k = k_ref[b, sl, :].astype(F32)
            bv = b_ref[b, sl, :].astype(F32)
            d_inv = jnp.exp(-cl)
            d_tail = jnp.exp(cl_end - cl)
            rt = (r_ref[b, sl, :].astype(F32) * jnp.exp(cl)).astype(BF16)
            rt_s[b, sl, :] = rt
            kt = k * d_inv
            bt = bv * d_inv
            at = a_ref[b, sl, :].astype(F32) * jnp.exp(cl - ld)
            kd = k * d_tail
            bdk = bv * d_tail
            d_end = jnp.exp(cl_end)
            for p, cs in enumerate(pair_cols):
                bkt_s[b, c, p] = jnp.concatenate([bdk[:, cs], kd[:, cs]], axis=0).T.astype(BF16)
                dcol_s[b, c, p] = jnp.broadcast_to(d_end[:, cs], (HEAD_PAIR, HEAD_PAIR)).T
                where.append((b, sl, cs))
                ats.append(at[:, cs])
                lhs.append(jnp.concatenate([rt[:, cs], at[:, cs].astype(BF16)], axis=0))
                rhs.append(jnp.concatenate([bd(kt[:, cs]), bd(bt[:, cs])], axis=0))
        n = len(where)
        amat = [_dot_nt(lhs[i], rhs[i]) for i in range(n)]
        l_ak, l_ab = [], []
        for i, (b, sl, cs) in enumerate(where):
            ark_s[b, sl, cs] = jnp.where(incl, amat[i][:C, :HEAD_PAIR], 0.0).astype(BF16)
            arb_s[b, sl, cs] = jnp.where(incl, amat[i][:C, HEAD_PAIR:], 0.0).astype(BF16)
            l_ak.append(jnp.where(strict, amat[i][C:, :HEAD_PAIR], 0.0))
            l_ab.append(jnp.where(strict, amat[i][C:, HEAD_PAIR:], 0.0))

        t_mat = [eye_pair + l for l in l_ab]
        q = [_dot(l.astype(BF16), bd(l)) for l in l_ab]
        n_sq = 1
        while 2 * n_sq < C // 2:
            both = [_dot(q[i].astype(BF16), jnp.concatenate([bd(q[i]), bd(t_mat[i])], axis=1))
                    for i in range(n)]
            q = [x[:, :HEAD_PAIR] for x in both]
            t_mat = [t_mat[i] + both[i][:, HEAD_PAIR:] for i in range(n)]
            n_sq *= 2
        t_mat = [t_mat[i] + _dot(q[i].astype(BF16), bd(t_mat[i])) for i in range(n)]
        tal = [_dot(t_mat[i].astype(BF16), jnp.concatenate([bd(ats[i]), bd(l_ak[i])], axis=1))
               for i in range(n)]
        for i, (b, sl, cs) in enumerate(where):
            ta_s[b, sl, cs] = tal[i][:, :HEAD_PAIR].astype(BF16)
            tl_s[b, sl, cs] = tal[i][:, HEAD_PAIR:].astype(BF16)
        return carry

    lax.fori_loop(0, bb * n_chunks // WKV_PREP_GROUP, prep, 0)

    def step(c, carry):
        sl = pl.ds(pl.multiple_of(c * C, C), C)
        chains = [(b, p) for b in range(bb) for p in range(n_pairs)]
        v = [v_ref[b, sl, :].astype(F32) for b in range(bb)]
        st = [st_ref[b, p] for b, p in chains]
        st_b = [x.astype(BF16) for x in st]
        bd_v = [bd(v[b][:, pair_cols[p]]) for b, p in chains]
        on_st = [_dot(jnp.concatenate([ta_s[b, sl, pair_cols[p]], rt_s[b, sl, pair_cols[p]]],
                                      axis=0), st_b[i]) for i, (b, p) in enumerate(chains)]
        on_v = [_dot(jnp.concatenate([tl_s[b, sl, pair_cols[p]], ark_s[b, sl, pair_cols[p]]],
                                     axis=0), bd_v[i]) for i, (b, p) in enumerate(chains)]
        u = [on_st[i][:C] + on_v[i][:C] for i in range(len(chains))]
        for i, (b, p) in enumerate(chains):
            uv = jnp.concatenate([u[i], v[b][:, pair_cols[p]]], axis=0).astype(BF16)
            st_ref[b, p] = (dcol_s[b, c, p] * st[i]
                            + jnp.where(bd_mask, _dot(bkt_s[b, c, p], uv), 0.0))
        ys = [on_st[i][C:] + on_v[i][C:] + _dot(arb_s[b, sl, pair_cols[p]], bd(u[i]))
              for i, (b, p) in enumerate(chains)]
        for b in range(bb):
            o_ref[b, sl, :] = jnp.concatenate(ys[b * n_pairs:(b + 1) * n_pairs], axis=1)
        return carry

    lax.fori_loop(0, n_chunks, step, 0)

    def finish(it, carry):
        items = [(b, pl.ds(pl.multiple_of((it * WKV_NORM_GROUP + j) * C, C), C))
                 for j in range(WKV_NORM_GROUP) for b in range(bb)]
        y = [o_ref[b, sl, :] for b, sl in items]
        rkr = [r_ref[b, sl, :].astype(F32) * k_ref[b, sl, :].astype(F32) * rk_ref[...]
               for b, sl in items]
        sums = [_pair_head_sums([y[i], rkr[i]], ones_pair) for i in range(len(items))]
        yc = [y[i] - sums[i][0] * (1.0 / A_HEAD) for i in range(len(items))]
        var = [_pair_head_sums([x * x], ones_pair)[0] * (1.0 / A_HEAD) for x in yc]
        for i, (b, sl) in enumerate(items):
            yn = yc[i] * lax.rsqrt(var[i] + LNX_EPS) * lng_ref[...] + lnb_ref[...]
            o_ref[b, sl, :] = ((yn + sums[i][1] * v_ref[b, sl, :].astype(F32))
                               * g_ref[b, sl, :].astype(F32))
        return carry

    lax.fori_loop(0, n_chunks // WKV_NORM_GROUP, finish, 0)


def _post_kernel(x_ref, ya_ref, ga_ref, ybg_ref, woa_ref, wout_ref, g2_ref, wr_ref,
                 wr_hi_ref, br_ref, x2_out, hl_out, route_out, wts_out, cnt_out):
    n_sub = x_ref.shape[0] // MOE_SUB
    groups = [slice(g * MOE_SUB, (g + 1) * MOE_SUB) for g in range(n_sub)]
    y_a = [_dot(ya_ref[p, :].astype(BF16), woa_ref[...]) for p in groups]
    x2 = [x_ref[p, :] + _dot((ga_ref[p, :] * y + ybg_ref[p, :]).astype(BF16), wout_ref[...])
          for p, y in zip(groups, y_a)]
    h2 = [_rms(x, g2_ref[...]) for x in x2]
    split = [_split2(h) for h in h2]
    hi_w = [_dot(hi, wr_ref[...]) for hi, _ in split]
    lo_w = [_dot(lo, wr_hi_ref[...]) for _, lo in split]
    lane = lax.broadcasted_iota(jnp.int32, (MOE_SUB, LANES), 1)
    neg = jnp.float32(-jnp.inf)
    big = jnp.int32(LANES)
    is_grp = (lane >= N_EXPERTS) & (lane < N_EXPERTS + N_GROUPS)
    tri = (lax.broadcasted_iota(jnp.int32, (MOE_SUB, MOE_SUB), 0)
           >= lax.broadcasted_iota(jnp.int32, (MOE_SUB, MOE_SUB), 1)).astype(BF16)
    picks, onehots = [], []
    for g, p in enumerate(groups):
        x2_out[p, :] = x2[g]
        logits = hi_w[g][:, :LANES] + hi_w[g][:, LANES:] + lo_w[g] + br_ref[...]
        gl = jnp.where(is_grp, logits, neg)
        gmax = jnp.max(gl, axis=-1, keepdims=True)
        g_p = 1.0 / jnp.sum(jnp.exp(gl - gmax), axis=-1, keepdims=True)
        g_idx = jnp.min(jnp.where(gl == gmax, lane, big), axis=-1, keepdims=True) - N_EXPERTS
        lo_lane = g_idx * EXPERTS_PER_GROUP
        in_grp = (lane >= lo_lane) & (lane < lo_lane + EXPERTS_PER_GROUP)
        el = jnp.where(in_grp, logits, neg)
        e1 = jnp.max(el, axis=-1, keepdims=True)
        i1 = jnp.min(jnp.where(el == e1, lane, big), axis=-1, keepdims=True)
        el2 = jnp.where(lane == i1, neg, el)
        e2 = jnp.max(el2, axis=-1, keepdims=True)
        i2 = jnp.min(jnp.where(el2 == e2, lane, big), axis=-1, keepdims=True)
        t = jnp.exp(e2 - e1)
        wts_out[p, :] = jnp.where(lane == 0, g_p / (1.0 + t),
                                  jnp.where(lane == 1, g_p * t / (1.0 + t), 0.0))
        picks.append((i1, i2))
        onehots.append(jnp.where((lane == i1) | (lane == i2), 1.0, 0.0))

    incl = [_dot(tri, oh.astype(BF16)) for oh in onehots]
    row8 = lax.broadcasted_iota(jnp.int32, (8, LANES), 0)
    chunks = jnp.zeros((8, LANES), F32)
    for g in range(n_sub):
        chunks = jnp.where(row8 == g, jnp.ceil(incl[g][MOE_SUB - 1:, :] * (1.0 / RUN_ALIGN)),
                           chunks)
    cnt_out[...] = chunks
    upper = (lax.broadcasted_iota(jnp.int32, (LANES, LANES), 0)
             < lax.broadcasted_iota(jnp.int32, (LANES, LANES), 1)).astype(BF16)
    run_start = _dot(chunks.astype(BF16), upper) * RUN_ALIGN
    routes = []
    for g, p in enumerate(groups):
        slot_of = run_start[g:g + 1, :] + incl[g] - onehots[g]
        i1, i2 = picks[g]
        slot1 = jnp.sum(jnp.where(lane == i1, slot_of, 0.0), axis=-1, keepdims=True)
        slot2 = jnp.sum(jnp.where(lane == i2, slot_of, 0.0), axis=-1, keepdims=True)
        route = jnp.where(lane == 0, slot1, jnp.where(lane == 1, slot2, 0.0)).astype(jnp.int32)
        route_out[p, :] = route
        routes.append(route.T)

    slot_ids = lax.broadcasted_iota(jnp.int32, (RUN_SLOTS, MOE_SUB), 0)
    for g in range(n_sub):
        pick = jnp.where((slot_ids == routes[g][0:1, :]) | (slot_ids == routes[g][1:2, :]),
                         1.0, 0.0).astype(BF16)
        hl_out[g * RUN_SLOTS:(g + 1) * RUN_SLOTS, :] = _dot(pick, h2[g].astype(BF16)).astype(BF16)


def _expert_kernel(chunk_s, first_s, hl_in_ref, wg_ref, wu_ref, wd_ref, hl_ref, xbuf, wgu_s, wd_s,
                   sem, drain_s):
    del hl_in_ref
    e = pl.program_id(0)
    pass_chunks = EXPERT_ROWS // RUN_ALIGN
    mine = e % 2
    other = 1 - mine
    gather_sem = lambda buf: sem.at[buf]
    scatter_sem = lambda buf: sem.at[2 + buf]

    def first_pass(ex):
        lo = first_s[ex]
        return lo, jnp.minimum(lo + pass_chunks, first_s[ex + 1])

    def for_chunks(buf, c_lo, c_hi, fn):
        def one(c):
            fn(hl_ref.at[pl.ds(pl.multiple_of(chunk_s[c], RUN_ALIGN), RUN_ALIGN)],
               xbuf.at[buf, pl.ds(pl.multiple_of((c - c_lo) * RUN_ALIGN, RUN_ALIGN), RUN_ALIGN)])

        def block(i, carry):
            for u in range(CHUNK_UNROLL):
                one(c_lo + i * CHUNK_UNROLL + u)
            return carry

        def tail(c, carry):
            one(c)
            return carry

        n_blocks = (c_hi - c_lo) // CHUNK_UNROLL
        lax.fori_loop(0, n_blocks, block, 0)
        lax.fori_loop(c_lo + n_blocks * CHUNK_UNROLL, c_hi, tail, 0)

    def gather(buf, c_lo, c_hi):
        for_chunks(buf, c_lo, c_hi, lambda hbm, vm: pltpu.make_async_copy(
            hbm, vm, gather_sem(buf)).start())

    def scatter(buf, c_lo, c_hi):
        for_chunks(buf, c_lo, c_hi, lambda hbm, vm: pltpu.make_async_copy(
            vm, hbm, scatter_sem(buf)).start())

    def wait_chunks(which_sem, n):
        def wait_rows(rows):
            def body(c, carry):
                pltpu.make_async_copy(hl_ref.at[pl.ds(0, rows)], xbuf.at[0, pl.ds(0, rows)],
                                      which_sem).wait()
                return carry
            return body

        lax.fori_loop(0, n // CHUNK_UNROLL, wait_rows(CHUNK_UNROLL * RUN_ALIGN), 0)
        lax.fori_loop(0, n % CHUNK_UNROLL, wait_rows(RUN_ALIGN), 0)

    def evaluate(buf, n_chunks):
        def tile(k, carry):
            sl = pl.ds(pl.multiple_of(k * EXPERT_TILE, EXPERT_TILE), EXPERT_TILE)
            gu = _dot(xbuf[buf, sl, :], wgu_s[...])
            gate = gu[:, :D_EXPERT]
            act = gate * jax.nn.sigmoid(gate) * gu[:, D_EXPERT:]
            xbuf[buf, sl, :] = _dot(act.astype(BF16), wd_s[...]).astype(BF16)
            return carry

        lax.fori_loop(0, (n_chunks * RUN_ALIGN + EXPERT_TILE - 1) // EXPERT_TILE, tile, 0)

    @pl.when(e == 0)
    def _():
        xbuf[...] = jnp.zeros_like(xbuf)
        gather(0, *first_pass(0))
        drain_s[0] = 0

    wgu_s[:, :D_EXPERT] = wg_ref[0].astype(BF16)
    wgu_s[:, D_EXPERT:] = wu_ref[0].astype(BF16)
    wd_s[...] = wd_ref[0].astype(BF16)

    c_lo, c_hi = first_pass(e)
    wait_chunks(gather_sem(mine), c_hi - c_lo)
    evaluate(mine, c_hi - c_lo)
    wait_chunks(scatter_sem(other), drain_s[0])
    scatter(mine, c_lo, c_hi)

    def later_pass(lo):
        hi = jnp.minimum(lo + pass_chunks, first_s[e + 1])
        gather(other, lo, hi)
        wait_chunks(gather_sem(other), hi - lo)
        evaluate(other, hi - lo)
        scatter(other, lo, hi)
        wait_chunks(scatter_sem(other), hi - lo)
        return hi

    lax.while_loop(lambda c: c < first_s[e + 1], later_pass, c_hi)
    drain_s[0] = c_hi - c_lo

    @pl.when(e + 1 < pl.num_programs(0))
    def _():
        gather(other, *first_pass(e + 1))

    @pl.when(e + 1 == pl.num_programs(0))
    def _():
        wait_chunks(scatter_sem(mine), c_hi - c_lo)


def _combine_kernel(x2_ref, route_ref, wts_ref, gf_ref, yl_ref, o_ref):
    slots = lax.broadcasted_iota(jnp.int32, (MOE_SUB, RUN_SLOTS), 1)
    for g in range(x2_ref.shape[0] // MOE_SUB):
        p = slice(g * MOE_SUB, (g + 1) * MOE_SUB)
        route = route_ref[p, :]
        wts = wts_ref[p, :]
        pick = (jnp.where(slots == route[:, 0:1], wts[:, 0:1], 0.0)
                + jnp.where(slots == route[:, 1:2], wts[:, 1:2], 0.0))
        y = _dot(pick.astype(BF16), yl_ref[g * RUN_SLOTS:(g + 1) * RUN_SLOTS, :])
        o_ref[p, :] = _rms(x2_ref[p, :] + y, gf_ref[...])


def _full(shape):
    return pl.BlockSpec(shape, lambda *_: (0,) * len(shape))


def _params(sem):
    return pltpu.CompilerParams(dimension_semantics=sem, vmem_limit_bytes=VMEM_LIMIT)


def _mixer_a_layout(t):
    s_rkv = 3 * A_WIDTH
    s_w = s_rkv + D_DECAY_LORA
    s_a = s_w + D_AAA_LORA
    gap = lambda n: jnp.zeros((t.shape[0], n), t.dtype)
    return jnp.concatenate(
        [t[:, :s_w], gap(XA_OFF - D_DECAY_LORA), t[:, s_w:s_a], gap(XG_OFF - XA_OFF - D_AAA_LORA),
         t[:, s_a:], gap(LORA_PAD - XG_OFF - D_GATE_LORA)], axis=1)


def _place(cols, parts):
    out = jnp.zeros((parts[0][1].shape[0], cols), parts[0][1].dtype)
    for off, arr in parts:
        out = lax.dynamic_update_slice(out, arr, (0, off))
    return out


def kernel(x, norm1_g, w_in, b_gate, tmix_mu, w0, w2, a0, a2, g2, k_k, k_a, r_k, lnx_g, lnx_b,
           w_oA, lnv_g, lnv_b, w_s, b_s, w_oB, w_out, norm2_g, w_rg, b_rg, w_re, b_re,
           w_e_gate, w_e_up, w_e_down, final_g):
    bsz, seq, d = x.shape
    n_tok = bsz * seq
    depth = norm1_g.shape[0]
    assert depth == 1, "the moe kernel fuses the final norm, so it must be the last layer"
    assert bsz % WKV_SEQS == 0 and seq % WKV_TOKENS == 0
    xf = x.reshape(n_tok, d)

    s_rkv = 3 * A_WIDTH
    s_w = s_rkv + D_DECAY_LORA
    s_a = s_w + D_AAA_LORA
    a_cols = s_a + D_GATE_LORA
    b_cols = 2 * B_WIDTH

    ones_bd = (jnp.arange(A_WIDTH)[:, None] // A_HEAD
               == jnp.arange(A_WIDTH)[None, :] // A_HEAD).astype(BF16)

    tm_a = 512
    tm_b = 512
    tm_p = 512

    for l in range(depth):
        wl = w_in[l]
        w_a = _mixer_a_layout(wl[:, :a_cols]).astype(BF16)
        mu_a = _mixer_a_layout(tmix_mu[l][None, :])
        w2p = jnp.pad(w2[l], ((0, LANES - D_DECAY_LORA), (0, 0))).astype(BF16)
        a2p = jnp.pad(a2[l], ((0, LANES - D_AAA_LORA), (0, 0))).astype(BF16)
        g2p = jnp.pad(g2[l], ((0, LORA_PAD - XG_OFF - D_GATE_LORA), (0, 0))).astype(BF16)
        g1 = norm1_g[l][None, :]

        row512 = lambda i: (i, 0)
        tok_a = pl.BlockSpec((tm_a, A_WIDTH), row512)
        vec_a = _full((1, A_WIDTH))
        outs = pl.pallas_call(
            functools.partial(_rwkv_prep_kernel, seq // tm_a),
            grid=(n_tok // tm_a,),
            in_specs=[pl.BlockSpec((tm_a, d), row512),
                      _full((1, d)), _full((d, A_PROJ)), _full((1, A_PROJ)), vec_a,
                      _full((LANES, A_WIDTH)), vec_a, _full((LANES, A_WIDTH)),
                      _full((LORA_PAD - XG_OFF, A_WIDTH)), vec_a, vec_a,
                      _full((A_WIDTH, A_WIDTH))],
            out_specs=[tok_a] * 7,
            out_shape=[jax.ShapeDtypeStruct((n_tok, A_WIDTH), t)
                       for t in (BF16, BF16, BF16, BF16, BF16, F32, BF16)],
            scratch_shapes=[pltpu.VMEM((8, A_PROJ), F32)],
            compiler_params=_params(("arbitrary",)),
            name="rwkv_prep",
        )(xf, g1, w_a, mu_a, w0[l][None, :], w2p, a0[l][None, :], a2p, g2p,
          k_k[l][None, :], k_a[l][None, :], ones_bd)
        r_, k_, v_, a_, b_, ld_, gg_ = outs

        w_b = wl[:, a_cols:a_cols + b_cols].astype(BF16)
        w_g = wl[:, a_cols + b_cols:].astype(BF16)
        bs_full = jnp.repeat(b_s[l].T, B_GROUP_CH, axis=1)
        tok_d = pl.BlockSpec((tm_b, d), row512)
        ybg, ga = pl.pallas_call(
            _gmlp_kernel,
            grid=(n_tok // tm_b,),
            in_specs=[tok_d, _full((1, d)), _full((d, b_cols)), _full((d, 2 * d)),
                      _full((1, 2 * d)), _full((1, B_WIDTH)), _full((1, B_WIDTH)),
                      _full((B_GROUPS, GMLP_BLOCK, GMLP_BLOCK)), _full((GMLP_BLOCK, B_WIDTH)),
                      _full((B_WIDTH, d))],
            out_specs=[tok_d, tok_d],
            out_shape=[jax.ShapeDtypeStruct((n_tok, d), BF16)] * 2,
            compiler_params=_params(("parallel",)),
            name="gmlp",
        )(xf, g1, w_b, w_g, b_gate[l][None, :], lnv_g[l][None, :], lnv_b[l][None, :],
          w_s[l], bs_full, w_oB[l].astype(BF16))

        n_pairs = A_WIDTH // HEAD_PAIR
        n_chunks = WKV_TOKENS // WKV_CHUNK
        tok_w = pl.BlockSpec((WKV_SEQS, WKV_TOKENS, A_WIDTH), lambda bi, ti: (bi, ti, 0))
        vec_w = _full((1, A_WIDTH))
        seq_major = lambda t: t.reshape(bsz, seq, A_WIDTH)
        tok_scratch = pltpu.VMEM((WKV_SEQS, WKV_TOKENS, A_WIDTH), BF16)
        ya_in = pl.pallas_call(
            _wkv_kernel,
            grid=(bsz // WKV_SEQS, seq // WKV_TOKENS),
            in_specs=[tok_w] * 7 + [vec_w, vec_w, vec_w, _full((HEAD_PAIR, HEAD_PAIR))],
            out_specs=tok_w,
            out_shape=jax.ShapeDtypeStruct((bsz, seq, A_WIDTH), F32),
            scratch_shapes=[pltpu.VMEM((WKV_SEQS, n_pairs, HEAD_PAIR, HEAD_PAIR), F32)]
            + [tok_scratch] * 5
            + [pltpu.VMEM((WKV_SEQS, n_chunks, n_pairs, HEAD_PAIR, HEAD_PAIR), BF16),
               pltpu.VMEM((WKV_SEQS, n_chunks, n_pairs, HEAD_PAIR, HEAD_PAIR), F32)],
            compiler_params=_params(("parallel", "arbitrary")),
            name="wkv",
        )(*[seq_major(t) for t in (r_, k_, v_, a_, b_, ld_, gg_)], r_k[l].reshape(1, A_WIDTH),
          lnx_g[l][None, :], lnx_b[l][None, :],
          ones_bd[:HEAD_PAIR, :HEAD_PAIR]).reshape(n_tok, A_WIDTH)

        w_r = _place(LANES, [(0, jnp.transpose(w_re[l], (1, 0, 2)).reshape(d, N_EXPERTS)),
                             (N_EXPERTS, w_rg[l])])
        b_r = _place(LANES, [(0, b_re[l].reshape(1, N_EXPERTS)), (N_EXPERTS, b_rg[l][None, :])])
        wr_hi = w_r.astype(BF16)
        wr_lo = (w_r - wr_hi.astype(F32)).astype(BF16)
        tok_p = pl.BlockSpec((tm_p, d), row512)
        lane_p = pl.BlockSpec((tm_p, LANES), row512)
        n_sub = n_tok // MOE_SUB
        sorted_rows = tm_p // MOE_SUB * RUN_SLOTS
        x2, h_sorted, route, wts, cnt = pl.pallas_call(
            _post_kernel,
            grid=(n_tok // tm_p,),
            in_specs=[tok_p, pl.BlockSpec((tm_p, A_WIDTH), row512), tok_p, tok_p,
                      _full((A_WIDTH, d)), _full((d, d)), _full((1, d)), _full((d, 2 * LANES)),
                      _full((d, LANES)), _full((1, LANES))],
            out_specs=[tok_p, pl.BlockSpec((sorted_rows, d), row512), lane_p, lane_p,
                       pl.BlockSpec((8, LANES), row512)],
            out_shape=[jax.ShapeDtypeStruct((n_tok, d), F32),
                       jax.ShapeDtypeStruct((n_sub * RUN_SLOTS, d), BF16),
                       jax.ShapeDtypeStruct((n_tok, LANES), jnp.int32),
                       jax.ShapeDtypeStruct((n_tok, LANES), F32),
                       jax.ShapeDtypeStruct((n_tok // tm_p * 8, LANES), F32)],
            compiler_params=_params(("parallel",)),
            name="post",
        )(xf, ya_in, ga, ybg, w_oA[l].astype(BF16), w_out[l].astype(BF16),
          norm2_g[l][None, :], jnp.concatenate([wr_hi, wr_lo], axis=1), wr_hi, b_r)

        run_chunks = cnt.reshape(n_tok // tm_p, 8, LANES)[:, :tm_p // MOE_SUB, :N_EXPERTS]
        run_chunks = run_chunks.reshape(n_sub, N_EXPERTS).astype(jnp.int32)
        run_start = (jnp.cumsum(run_chunks, axis=1) - run_chunks) * RUN_ALIGN
        run_row = (jnp.arange(n_sub, dtype=jnp.int32)[:, None] * RUN_SLOTS + run_start).T.reshape(-1)
        run_n = run_chunks.T.reshape(-1)
        run_end = jnp.cumsum(run_n)
        first_chunk = jnp.concatenate([jnp.zeros((1,), jnp.int32), run_end[n_sub - 1::n_sub]])
        pos = jnp.arange(n_sub * RUN_SLOTS // RUN_ALIGN, dtype=jnp.int32)
        run_of = jnp.sum(pos[:, None] >= run_end[None, :], axis=1)
        in_run = run_of[:, None] == jnp.arange(run_n.shape[0], dtype=jnp.int32)[None, :]
        pick = lambda t: jnp.sum(jnp.where(in_run, t[None, :], 0), axis=1)
        chunk_rows = pick(run_row) + (pos - pick(run_end - run_n)) * RUN_ALIGN

        per_expert = lambda e, *_: (e, 0, 0)
        y_sorted = pl.pallas_call(
            _expert_kernel,
            grid_spec=pltpu.PrefetchScalarGridSpec(
                num_scalar_prefetch=2, grid=(N_EXPERTS,),
                in_specs=[pl.BlockSpec(memory_space=pl.ANY),
                          pl.BlockSpec((1, d, D_EXPERT), per_expert),
                          pl.BlockSpec((1, d, D_EXPERT), per_expert),
                          pl.BlockSpec((1, D_EXPERT, d), per_expert)],
                out_specs=pl.BlockSpec(memory_space=pl.ANY),
                scratch_shapes=[pltpu.VMEM((2, EXPERT_ROWS, d), BF16),
                                pltpu.VMEM((d, 2 * D_EXPERT), BF16),
                                pltpu.VMEM((D_EXPERT, d), BF16),
                                pltpu.SemaphoreType.DMA((4,)),
                                pltpu.SMEM((1,), jnp.int32)]),
            out_shape=jax.ShapeDtypeStruct((n_sub * RUN_SLOTS, d), BF16),
            input_output_aliases={2: 0},
            compiler_params=_params(("arbitrary",)),
            name="moe_experts",
        )(chunk_rows, first_chunk, h_sorted, w_e_gate[l], w_e_up[l], w_e_down[l])

        xf = pl.pallas_call(
            _combine_kernel,
            grid=(n_tok // tm_p,),
            in_specs=[tok_p, lane_p, lane_p, _full((1, d)),
                      pl.BlockSpec((sorted_rows, d), row512)],
            out_specs=tok_p,
            out_shape=jax.ShapeDtypeStruct((n_tok, d), F32),
            compiler_params=_params(("parallel",)),
            name="moe_combine",
        )(x2, route, wts, final_g[None, :], y_sorted)

    return xf.reshape(bsz, seq, d)
```

```python
import functools

import jax
import jax.numpy as jnp
from jax import lax
from jax.experimental import pallas as pl
from jax.experimental.pallas import tpu as pltpu

F32 = jnp.float32
BF16 = jnp.bfloat16

D_MODEL = 1024
A_WIDTH = 512
A_HEAD = 64
D_DECAY_LORA = 64
D_AAA_LORA = 64
D_GATE_LORA = 160
B_WIDTH = 512
B_GROUPS = 4
B_GROUP_CH = 128
GMLP_BLOCK = 128
N_GROUPS = 4
EXPERTS_PER_GROUP = 8
N_EXPERTS = 32
D_EXPERT = 256
NORM_EPS = 1e-6
LN_EPS = 1e-5
LNX_EPS = 64e-5

LANES = 128
LORA_PAD = 512
XW_OFF, XA_OFF, XG_OFF = 0, 128, 256
A_PROJ = 3 * A_WIDTH + LORA_PAD
WKV_CHUNK = 64
HEAD_PAIR = 2 * A_HEAD
WKV_SEQS = 4
WKV_TOKENS = 128
WKV_PREP_GROUP = 4
WKV_NORM_GROUP = 2
MOE_SUB = 256
RUN_ALIGN = 16
RUN_SLOTS = 2 * MOE_SUB + N_EXPERTS * RUN_ALIGN
EXPERT_ROWS = 2048
EXPERT_TILE = 512
DENSE_SPLIT = 256
CHUNK_UNROLL = 4
VMEM_LIMIT = 48 * 1024 * 1024


def _rms(x, g):
    return x * lax.rsqrt(jnp.mean(x * x, axis=-1, keepdims=True) + NORM_EPS) * g


def _dot(a, b):
    return jnp.dot(a, b, preferred_element_type=F32)


def _dot_nt(a, b):
    return lax.dot_general(a, b, (((1,), (1,)), ((), ())), preferred_element_type=F32)


def _split2(x):
    hi = x.astype(BF16)
    lo = (x - hi.astype(F32)).astype(BF16)
    return hi, lo


def _split3(x):
    hi = x.astype(BF16)
    r1 = x - hi.astype(F32)
    mid = r1.astype(BF16)
    lo = (r1 - mid.astype(F32)).astype(BF16)
    return hi, mid, lo


def _pair_head_sums(xs, ones_pair):
    n_tiles = A_WIDTH // HEAD_PAIR
    rows = xs[0].shape[0]
    tiles = [x[:, t * HEAD_PAIR:(t + 1) * HEAD_PAIR].astype(BF16) for x in xs for t in range(n_tiles)]
    sums = _dot(jnp.concatenate(tiles, axis=0), ones_pair)
    return [jnp.concatenate([sums[(i * n_tiles + t) * rows:(i * n_tiles + t + 1) * rows]
                             for t in range(n_tiles)], axis=1) for i in range(len(xs))]


def _w_in_kernel(w_ref, wa_out, wb_out, wg_out):
    s_rkv = 3 * A_WIDTH
    s_w = s_rkv + D_DECAY_LORA
    s_a = s_w + D_AAA_LORA
    a_cols = s_a + D_GATE_LORA
    b_end = a_cols + 2 * B_WIDTH
    wa_out[...] = jnp.zeros_like(wa_out)
    wa_out[:, :s_w] = w_ref[:, :s_w].astype(BF16)
    wa_out[:, s_rkv + XA_OFF:s_rkv + XA_OFF + D_AAA_LORA] = w_ref[:, s_w:s_a].astype(BF16)
    wa_out[:, s_rkv + XG_OFF:s_rkv + XG_OFF + D_GATE_LORA] = w_ref[:, s_a:a_cols].astype(BF16)
    wb_out[...] = w_ref[:, a_cols:b_end].astype(BF16)
    wg_out[...] = w_ref[:, b_end:].astype(BF16)


def _rwkv_prep_kernel(tiles_per_seq, x_ref, g1_ref, wa_ref, mu_ref, w0_ref,
                      w2_ref, a0_ref, a2_ref, g2_ref, kk_ref, ka_ref, ones_ref,
                      r_out, k_out, v_out, a_out, b_out, ld_out, g_out, tail_ref):
    i = pl.program_id(0)
    tm = x_ref.shape[0]
    parts = [slice(j * DENSE_SPLIT, (j + 1) * DENSE_SPLIT) for j in range(tm // DENSE_SPLIT)]
    g1 = g1_ref[...]
    h = [_rms(x_ref[p, :], g1).astype(BF16) for p in parts]
    proj = [_dot(hh, wa_ref[...]) for hh in h]
    prev = [jnp.where(i % tiles_per_seq == 0, 0.0, tail_ref[7:8, :])]
    prev += [p[DENSE_SPLIT - 1:, :] for p in proj[:-1]]
    tail_ref[...] = proj[-1][DENSE_SPLIT - 8:, :]
    row = lax.broadcasted_iota(jnp.int32, proj[0].shape, 0)
    pm = []
    for p, pv in zip(proj, prev):
        shifted = jnp.where(row == 0, pv, pltpu.roll(p, 1, axis=0))
        pm.append(p + mu_ref[...] * (shifted - p))
    lora = [x[:, 3 * A_WIDTH:] for x in pm]
    decay = [_dot(jnp.tanh(x[:, XW_OFF:XW_OFF + LANES]).astype(BF16), w2_ref[...]) for x in lora]
    rate = [_dot(x[:, XA_OFF:XA_OFF + LANES].astype(BF16), a2_ref[...]) for x in lora]
    gate = [_dot(jax.nn.sigmoid(x[:, XG_OFF:]).astype(BF16), g2_ref[...]) for x in lora]
    kks = [x[:, A_WIDTH:2 * A_WIDTH] * kk_ref[...] for x in pm]
    sq = [_dot((kk * kk).astype(BF16), ones_ref[...]) for kk in kks]
    for j, p in enumerate(parts):
        k = pm[j][:, A_WIDTH:2 * A_WIDTH]
        z = -(w0_ref[...] + decay[j])
        softplus = jnp.maximum(z, 0.0) + jnp.log(1.0 + jnp.exp(-jnp.abs(z)))
        w = -softplus - 0.5
        a_lr = jax.nn.sigmoid(a0_ref[...] + rate[j])
        kk = kks[j] / jnp.maximum(jnp.sqrt(sq[j]), 1e-12)
        r_out[p, :] = pm[j][:, 0:A_WIDTH]
        k_out[p, :] = k * (1.0 + (a_lr - 1.0) * ka_ref[...])
        v_out[p, :] = pm[j][:, 2 * A_WIDTH:3 * A_WIDTH]
        a_out[p, :] = -kk
        b_out[p, :] = kk * a_lr
        ld_out[p, :] = -jnp.exp(w)
        g_out[p, :] = gate[j]


def _gmlp_kernel(x_ref, g1_ref, wb_ref, wg_ref, bg_ref, lng_ref, lnb_ref, ws_ref, bs_ref,
                 wo_ref, ybg_out, ga_out):
    tm = x_ref.shape[0]
    n_split = tm // DENSE_SPLIT
    parts = [slice(i * DENSE_SPLIT, (i + 1) * DENSE_SPLIT) for i in range(n_split)]
    g1 = g1_ref[...]
    h = [_rms(x_ref[p, :], g1).astype(BF16) for p in parts]
    pb = [_dot(hh, wb_ref[...]) for hh in h]
    tri = (lax.broadcasted_iota(jnp.int32, (GMLP_BLOCK, GMLP_BLOCK), 0)
           >= lax.broadcasted_iota(jnp.int32, (GMLP_BLOCK, GMLP_BLOCK), 1))
    ws = [jnp.where(tri, ws_ref[grp], 0.0).astype(BF16) for grp in range(B_GROUPS)]
    bs = bs_ref[...]
    us, vns = [], []
    for x in pb:
        z = 0.5 * x * (1.0 + lax.erf(x * (2.0 ** -0.5)))
        us.append(z[:, :B_WIDTH])
        v = z[:, B_WIDTH:]
        mean = jnp.mean(v, axis=-1, keepdims=True)
        vc = v - mean
        var = jnp.mean(vc * vc, axis=-1, keepdims=True)
        vns.append((vc * lax.rsqrt(var + LN_EPS) * lng_ref[...] + lnb_ref[...]).astype(BF16))
    svs = []
    for vn in vns:
        rows = []
        for blk in range(DENSE_SPLIT // GMLP_BLOCK):
            cols = [_dot(ws[grp], vn[blk * GMLP_BLOCK:(blk + 1) * GMLP_BLOCK,
                                     grp * B_GROUP_CH:(grp + 1) * B_GROUP_CH])
                    for grp in range(B_GROUPS)]
            rows.append(jnp.concatenate(cols, axis=1) + bs)
        svs.append(jnp.concatenate(rows, axis=0))
    yb = [_dot((u * sv).astype(BF16), wo_ref[...]) for u, sv in zip(us, svs)]
    gates = [jax.nn.sigmoid(_dot(hh, wg_ref[...]) + bg_ref[...]) for hh in h]
    for p, g, y in zip(parts, gates, yb):
        ga_out[p, :] = g[:, :D_MODEL].astype(BF16)
        ybg_out[p, :] = (g[:, D_MODEL:] * y).astype(BF16)


def _wkv_kernel(r_ref, k_ref, v_ref, a_ref, b_ref, ld_ref, g_ref, rk_ref, lng_ref, lnb_ref,
                ones_ref, o_ref, st_ref, ta_s, tl_s, arb_s, ark_s, rt_s, bkt_s, dcol_s):
    C = WKV_CHUNK
    bb, tb, _ = r_ref.shape
    n_chunks = tb // C
    n_pairs = A_WIDTH // HEAD_PAIR

    @pl.when(pl.program_id(1) == 0)
    def _():
        st_ref[...] = jnp.zeros_like(st_ref)

    row = lax.broadcasted_iota(jnp.int32, (C, HEAD_PAIR), 0)
    src = lax.broadcasted_iota(jnp.int32, (C, HEAD_PAIR), 1) & (C - 1)
    incl = src <= row
    strict = src < row
    eye_pair = jnp.where(src == row, 1.0, 0.0)
    bd_mask = ((lax.broadcasted_iota(jnp.int32, (HEAD_PAIR, HEAD_PAIR), 0) >= A_HEAD)
               == (lax.broadcasted_iota(jnp.int32, (HEAD_PAIR, HEAD_PAIR), 1) >= A_HEAD))
    tri_c = (lax.broadcasted_iota(jnp.int32, (C, C), 0)
             >= lax.broadcasted_iota(jnp.int32, (C, C), 1)).astype(BF16)
    ones_pair = ones_ref[...]
    pair_cols = [slice(p * HEAD_PAIR, (p + 1) * HEAD_PAIR) for p in range(n_pairs)]

    def bd(x):
        xb = x.astype(BF16)
        return jnp.where(bd_mask, jnp.concatenate([xb, xb], axis=0), jnp.zeros((), BF16))

    def prep(it, carry):
        where, lhs, rhs, ats = [], [], [], []
        for j in range(WKV_PREP_GROUP):
            flat = it * WKV_PREP_GROUP + j
            b = flat // n_chunks
            c = flat % n_chunks
            sl = pl.ds(pl.multiple_of(c * C, C), C)
            ld = ld_ref[b, sl, :]
            hi, mid, lo = _split3(ld)
            cl = _dot(tri_c, hi) + _dot(tri_c, mid) + _dot(tri_c, lo)
            cl_end = cl[C - 1:C, :]
            k = k_ref[b, sl, :]
            bv = b_ref[b, sl, :]
            d_inv = jnp.exp(-cl)
            d_tail = jnp.exp(cl_end - cl)
            rt = (r_ref[b, sl, :] * jnp.exp(cl)).astype(BF16)
            rt_s[b, sl, :] = rt
            kt = k * d_inv
            bt = bv * d_inv
            at = a_ref[b, sl, :] * jnp.exp(cl - ld)
            kd = k * d_tail
            bdk = bv * d_tail
            d_end = jnp.exp(cl_end)
            for p, cs in enumerate(pair_cols):
                bkt_s[b, c, p] = jnp.concatenate([bdk[:, cs], kd[:, cs]], axis=0).T.astype(BF16)
                dcol_s[b, c, p] = jnp.broadcast_to(d_end[:, cs], (HEAD_PAIR, HEAD_PAIR)).T
                where.append((b, sl, cs))
                ats.append(at[:, cs])
                lhs.append(jnp.concatenate([rt[:, cs], at[:, cs].astype(BF16)], axis=0))
                rhs.append(jnp.concatenate([bd(kt[:, cs]), bd(bt[:, cs])], axis=0))
        n = len(where)
        amat = [_dot_nt(lhs[i], rhs[i]) for i in range(n)]
        l_ak, l_ab = [], []
        for i, (b, sl, cs) in enumerate(where):
            ark_s[b, sl, cs] = jnp.where(incl, amat[i][:C, :HEAD_PAIR], 0.0).astype(BF16)
            arb_s[b, sl, cs] = jnp.where(incl, amat[i][:C, HEAD_PAIR:], 0.0).astype(BF16)
            l_ak.append(jnp.where(strict, amat[i][C:, :HEAD_PAIR], 0.0))
            l_ab.append(jnp.where(strict, amat[i][C:, HEAD_PAIR:], 0.0))

        t_mat = [eye_pair + l for l in l_ab]
        q = [_dot(l.astype(BF16), bd(l)) for l in l_ab]
        n_sq = 1
        while 2 * n_sq < C // 2:
            both = [_dot(q[i].astype(BF16), jnp.concatenate([bd(q[i]), bd(t_mat[i])], axis=1))
                    for i in range(n)]
            q = [x[:, :HEAD_PAIR] for x in both]
            t_mat = [t_mat[i] + both[i][:, HEAD_PAIR:] for i in range(n)]
            n_sq *= 2
        t_mat = [t_mat[i] + _dot(q[i].astype(BF16), bd(t_mat[i])) for i in range(n)]
        tal = [_dot(t_mat[i].astype(BF16), jnp.concatenate([bd(ats[i]), bd(l_ak[i])], axis=1))
               for i in range(n)]
        for i, (b, sl, cs) in enumerate(where):
            ta_s[b, sl, cs] = tal[i][:, :HEAD_PAIR].astype(BF16)
            tl_s[b, sl, cs] = tal[i][:, HEAD_PAIR:].astype(BF16)
        return carry

    lax.fori_loop(0, bb * n_chunks // WKV_PREP_GROUP, prep, 0)

    def step(c, carry):
        sl = pl.ds(pl.multiple_of(c * C, C), C)
        chains = [(b, p) for b in range(bb) for p in range(n_pairs)]
        v = [v_ref[b, sl, :] for b in range(bb)]
        st = [st_ref[b, p] for b, p in chains]
        st_b = [x.astype(BF16) for x in st]
        bd_v = [bd(v[b][:, pair_cols[p]]) for b, p in chains]
        on_st = [_dot(jnp.concatenate([ta_s[b, sl, pair_cols[p]], rt_s[b, sl, pair_cols[p]]],
                                      axis=0), st_b[i]) for i, (b, p) in enumerate(chains)]
        on_v = [_dot(jnp.concatenate([tl_s[b, sl, pair_cols[p]], ark_s[b, sl, pair_cols[p]]],
                                     axis=0), bd_v[i]) for i, (b, p) in enumerate(chains)]
        u = [on_st[i][:C] + on_v[i][:C] for i in range(len(chains))]
        for i, (b, p) in enumerate(chains):
            uv = jnp.concatenate([u[i], v[b][:, pair_cols[p]]], axis=0).astype(BF16)
            st_ref[b, p] = (dcol_s[b, c, p] * st[i]
                            + jnp.where(bd_mask, _dot(bkt_s[b, c, p], uv), 0.0))
        ys = [on_st[i][C:] + on_v[i][C:] + _dot(arb_s[b, sl, pair_cols[p]], bd(u[i]))
              for i, (b, p) in enumerate(chains)]
        for b in range(bb):
            o_ref[b, sl, :] = jnp.concatenate(ys[b * n_pairs:(b + 1) * n_pairs], axis=1)
        return carry

    lax.fori_loop(0, n_chunks, step, 0)

    def finish(it, carry):
        items = [(b, pl.ds(pl.multiple_of((it * WKV_NORM_GROUP + j) * C, C), C))
                 for j in range(WKV_NORM_GROUP) for b in range(bb)]
        y = [o_ref[b, sl, :] for b, sl in items]
        rkr = [r_ref[b, sl, :] * k_ref[b, sl, :] * rk_ref[...] for b, sl in items]
        sums = [_pair_head_sums([y[i], rkr[i]], ones_pair) for i in range(len(items))]
        yc = [y[i] - sums[i][0] * (1.0 / A_HEAD) for i in range(len(items))]
        var = [_pair_head_sums([x * x], ones_pair)[0] * (1.0 / A_HEAD) for x in yc]
        for i, (b, sl) in enumerate(items):
            yn = yc[i] * lax.rsqrt(var[i] + LNX_EPS) * lng_ref[...] + lnb_ref[...]
            o_ref[b, sl, :] = (yn + sums[i][1] * v_ref[b, sl, :]) * g_ref[b, sl, :]
        return carry

    lax.fori_loop(0, n_chunks // WKV_NORM_GROUP, finish, 0)


def _post_kernel(x_ref, ya_ref, ga_ref, ybg_ref, woa_ref, wout_ref, g2_ref, wr_ref,
                 wr_hi_ref, br_ref, x2_out, hl_out, route_out, wts_out, cnt_out):
    n_sub = x_ref.shape[0] // MOE_SUB
    groups = [slice(g * MOE_SUB, (g + 1) * MOE_SUB) for g in range(n_sub)]
    y_a = [_dot(ya_ref[p, :].astype(BF16), woa_ref[...]) for p in groups]
    x2 = [x_ref[p, :] + _dot((ga_ref[p, :] * y + ybg_ref[p, :]).astype(BF16), wout_ref[...])
          for p, y in zip(groups, y_a)]
    h2 = [_rms(x, g2_ref[...]) for x in x2]
    split = [_split2(h) for h in h2]
    hi_w = [_dot(hi, wr_ref[...]) for hi, _ in split]
    lo_w = [_dot(lo, wr_hi_ref[...]) for _, lo in split]
    lane = lax.broadcasted_iota(jnp.int32, (MOE_SUB, LANES), 1)
    neg = jnp.float32(-jnp.inf)
    big = jnp.int32(LANES)
    is_grp = (lane >= N_EXPERTS) & (lane < N_EXPERTS + N_GROUPS)
    tri = (lax.broadcasted_iota(jnp.int32, (MOE_SUB, MOE_SUB), 0)
           >= lax.broadcasted_iota(jnp.int32, (MOE_SUB, MOE_SUB), 1)).astype(BF16)
    picks, onehots = [], []
    for g, p in enumerate(groups):
        x2_out[p, :] = x2[g]
        logits = hi_w[g][:, :LANES] + hi_w[g][:, LANES:] + lo_w[g] + br_ref[...]
        gl = jnp.where(is_grp, logits, neg)
        gmax = jnp.max(gl, axis=-1, keepdims=True)
        g_p = 1.0 / jnp.sum(jnp.exp(gl - gmax), axis=-1, keepdims=True)
        g_idx = jnp.min(jnp.where(gl == gmax, lane, big), axis=-1, keepdims=True) - N_EXPERTS
        lo_lane = g_idx * EXPERTS_PER_GROUP
        in_grp = (lane >= lo_lane) & (lane < lo_lane + EXPERTS_PER_GROUP)
        el = jnp.where(in_grp, logits, neg)
        e1 = jnp.max(el, axis=-1, keepdims=True)
        i1 = jnp.min(jnp.where(el == e1, lane, big), axis=-1, keepdims=True)
        el2 = jnp.where(lane == i1, neg, el)
        e2 = jnp.max(el2, axis=-1, keepdims=True)
        i2 = jnp.min(jnp.where(el2 == e2, lane, big), axis=-1, keepdims=True)
        t = jnp.exp(e2 - e1)
        wts_out[p, :] = jnp.where(lane == 0, g_p / (1.0 + t),
                                  jnp.where(lane == 1, g_p * t / (1.0 + t), 0.0))
        picks.append((i1, i2))
        onehots.append(jnp.where((lane == i1) | (lane == i2), 1.0, 0.0))

    incl = [_dot(tri, oh.astype(BF16)) for oh in onehots]
    row8 = lax.broadcasted_iota(jnp.int32, (8, LANES), 0)
    chunks = jnp.zeros((8, LANES), F32)
    for g in range(n_sub):
        chunks = jnp.where(row8 == g, jnp.ceil(incl[g][MOE_SUB - 1:, :] * (1.0 / RUN_ALIGN)),
                           chunks)
    cnt_out[...] = chunks
    upper = (lax.broadcasted_iota(jnp.int32, (LANES, LANES), 0)
             < lax.broadcasted_iota(jnp.int32, (LANES, LANES), 1)).astype(BF16)
    run_start = _dot(chunks.astype(BF16), upper) * RUN_ALIGN
    routes = []
    for g, p in enumerate(groups):
        slot_of = run_start[g:g + 1, :] + incl[g] - onehots[g]
        i1, i2 = picks[g]
        slot1 = jnp.sum(jnp.where(lane == i1, slot_of, 0.0), axis=-1, keepdims=True)
        slot2 = jnp.sum(jnp.where(lane == i2, slot_of, 0.0), axis=-1, keepdims=True)
        route = jnp.where(lane == 0, slot1, jnp.where(lane == 1, slot2, 0.0)).astype(jnp.int32)
        route_out[p, :] = route
        routes.append(route.T)

    slot_ids = lax.broadcasted_iota(jnp.int32, (RUN_SLOTS, MOE_SUB), 0)
    for g in range(n_sub):
        pick = jnp.where((slot_ids == routes[g][0:1, :]) | (slot_ids == routes[g][1:2, :]),
                         1.0, 0.0).astype(BF16)
        hl_out[g * RUN_SLOTS:(g + 1) * RUN_SLOTS, :] = _dot(pick, h2[g].astype(BF16)).astype(BF16)


def _expert_kernel(chunk_s, first_s, hl_in_ref, wg_ref, wu_ref, wd_ref, hl_ref, xbuf, wgu_s, wd_s,
                   sem, drain_s):
    del hl_in_ref
    e = pl.program_id(0)
    pass_chunks = EXPERT_ROWS // RUN_ALIGN
    mine = e % 2
    other = 1 - mine
    gather_sem = lambda buf: sem.at[buf]
    scatter_sem = lambda buf: sem.at[2 + buf]

    def first_pass(ex):
        lo = first_s[ex]
        return lo, jnp.minimum(lo + pass_chunks, first_s[ex + 1])

    def for_chunks(buf, c_lo, c_hi, fn):
        def one(c):
            fn(hl_ref.at[pl.ds(pl.multiple_of(chunk_s[c], RUN_ALIGN), RUN_ALIGN)],
               xbuf.at[buf, pl.ds(pl.multiple_of((c - c_lo) * RUN_ALIGN, RUN_ALIGN), RUN_ALIGN)])

        def block(i, carry):
            for u in range(CHUNK_UNROLL):
                one(c_lo + i * CHUNK_UNROLL + u)
            return carry

        def tail(c, carry):
            one(c)
            return carry

        n_blocks = (c_hi - c_lo) // CHUNK_UNROLL
        lax.fori_loop(0, n_blocks, block, 0)
        lax.fori_loop(c_lo + n_blocks * CHUNK_UNROLL, c_hi, tail, 0)

    def gather(buf, c_lo, c_hi):
        for_chunks(buf, c_lo, c_hi, lambda hbm, vm: pltpu.make_async_copy(
            hbm, vm, gather_sem(buf)).start())

    def scatter(buf, c_lo, c_hi):
        for_chunks(buf, c_lo, c_hi, lambda hbm, vm: pltpu.make_async_copy(
            vm, hbm, scatter_sem(buf)).start())

    def wait_chunks(which_sem, n):
        def wait_rows(rows):
            def body(c, carry):
                pltpu.make_async_copy(hl_ref.at[pl.ds(0, rows)], xbuf.at[0, pl.ds(0, rows)],
                                      which_sem).wait()
                return carry
            return body

        lax.fori_loop(0, n // CHUNK_UNROLL, wait_rows(CHUNK_UNROLL * RUN_ALIGN), 0)
        lax.fori_loop(0, n % CHUNK_UNROLL, wait_rows(RUN_ALIGN), 0)

    def evaluate(buf, n_chunks):
        def tile(k, carry):
            sl = pl.ds(pl.multiple_of(k * EXPERT_TILE, EXPERT_TILE), EXPERT_TILE)
            gu = _dot(xbuf[buf, sl, :], wgu_s[...])
            gate = gu[:, :D_EXPERT]
            act = gate * jax.nn.sigmoid(gate) * gu[:, D_EXPERT:]
            xbuf[buf, sl, :] = _dot(act.astype(BF16), wd_s[...]).astype(BF16)
            return carry

        lax.fori_loop(0, (n_chunks * RUN_ALIGN + EXPERT_TILE - 1) // EXPERT_TILE, tile, 0)

    @pl.when(e == 0)
    def _():
        xbuf[...] = jnp.zeros_like(xbuf)
        gather(0, *first_pass(0))
        drain_s[0] = 0

    wgu_s[:, :D_EXPERT] = wg_ref[0].astype(BF16)
    wgu_s[:, D_EXPERT:] = wu_ref[0].astype(BF16)
    wd_s[...] = wd_ref[0].astype(BF16)

    c_lo, c_hi = first_pass(e)
    wait_chunks(gather_sem(mine), c_hi - c_lo)
    evaluate(mine, c_hi - c_lo)
    wait_chunks(scatter_sem(other), drain_s[0])
    scatter(mine, c_lo, c_hi)

    def later_pass(lo):
        hi = jnp.minimum(lo + pass_chunks, first_s[e + 1])
        gather(other, lo, hi)
        wait_chunks(gather_sem(other), hi - lo)
        evaluate(other, hi - lo)
        scatter(other, lo, hi)
        wait_chunks(scatter_sem(other), hi - lo)
        return hi

    lax.while_loop(lambda c: c < first_s[e + 1], later_pass, c_hi)
    drain_s[0] = c_hi - c_lo

    @pl.when(e + 1 < pl.num_programs(0))
    def _():
        gather(other, *first_pass(e + 1))

    @pl.when(e + 1 == pl.num_programs(0))
    def _():
        wait_chunks(scatter_sem(mine), c_hi - c_lo)


def _combine_kernel(x2_ref, route_ref, wts_ref, gf_ref, yl_ref, o_ref):
    slots = lax.broadcasted_iota(jnp.int32, (MOE_SUB, RUN_SLOTS), 1)
    for g in range(x2_ref.shape[0] // MOE_SUB):
        p = slice(g * MOE_SUB, (g + 1) * MOE_SUB)
        route = route_ref[p, :]
        wts = wts_ref[p, :]
        pick = (jnp.where(slots == route[:, 0:1], wts[:, 0:1], 0.0)
                + jnp.where(slots == route[:, 1:2], wts[:, 1:2], 0.0))
        y = _dot(pick.astype(BF16), yl_ref[g * RUN_SLOTS:(g + 1) * RUN_SLOTS, :])
        o_ref[p, :] = _rms(x2_ref[p, :] + y, gf_ref[...])


def _full(shape):
    return pl.BlockSpec(shape, lambda *_: (0,) * len(shape))


def _params(sem):
    return pltpu.CompilerParams(dimension_semantics=sem, vmem_limit_bytes=VMEM_LIMIT)


def _mixer_a_layout(t):
    s_rkv = 3 * A_WIDTH
    s_w = s_rkv + D_DECAY_LORA
    s_a = s_w + D_AAA_LORA
    gap = lambda n: jnp.zeros((t.shape[0], n), t.dtype)
    return jnp.concatenate(
        [t[:, :s_w], gap(XA_OFF - D_DECAY_LORA), t[:, s_w:s_a], gap(XG_OFF - XA_OFF - D_AAA_LORA),
         t[:, s_a:], gap(LORA_PAD - XG_OFF - D_GATE_LORA)], axis=1)


def _place(cols, parts):
    out = jnp.zeros((parts[0][1].shape[0], cols), parts[0][1].dtype)
    for off, arr in parts:
        out = lax.dynamic_update_slice(out, arr, (0, off))
    return out


def kernel(x, norm1_g, w_in, b_gate, tmix_mu, w0, w2, a0, a2, g2, k_k, k_a, r_k, lnx_g, lnx_b,
           w_oA, lnv_g, lnv_b, w_s, b_s, w_oB, w_out, norm2_g, w_rg, b_rg, w_re, b_re,
           w_e_gate, w_e_up, w_e_down, final_g):
    bsz, seq, d = x.shape
    n_tok = bsz * seq
    depth = norm1_g.shape[0]
    assert depth == 1, "the moe kernel fuses the final norm, so it must be the last layer"
    assert bsz % WKV_SEQS == 0 and seq % WKV_TOKENS == 0
    xf = x.reshape(n_tok, d)

    s_rkv = 3 * A_WIDTH
    s_w = s_rkv + D_DECAY_LORA
    s_a = s_w + D_AAA_LORA
    a_cols = s_a + D_GATE_LORA
    b_cols = 2 * B_WIDTH

    ones_bd = (jnp.arange(A_WIDTH)[:, None] // A_HEAD
               == jnp.arange(A_WIDTH)[None, :] // A_HEAD).astype(BF16)

    tm_a = 512
    tm_b = 512
    tm_p = 512

    for l in range(depth):
        wl = w_in[l]
        rows_w = 256
        w_a, w_b, w_g = pl.pallas_call(
            _w_in_kernel,
            grid=(d // rows_w,),
            in_specs=[pl.BlockSpec((rows_w, wl.shape[1]), lambda i: (i, 0))],
            out_specs=[pl.BlockSpec((rows_w, A_PROJ), lambda i: (i, 0)),
                       pl.BlockSpec((rows_w, b_cols), lambda i: (i, 0)),
                       pl.BlockSpec((rows_w, 2 * d), lambda i: (i, 0))],
            out_shape=[jax.ShapeDtypeStruct((d, A_PROJ), BF16),
                       jax.ShapeDtypeStruct((d, b_cols), BF16),
                       jax.ShapeDtypeStruct((d, 2 * d), BF16)],
            compiler_params=_params(("parallel",)),
            name="w_in_layout",
        )(wl)
        mu_a = _mixer_a_layout(tmix_mu[l][None, :])
        w2p = jnp.pad(w2[l], ((0, LANES - D_DECAY_LORA), (0, 0))).astype(BF16)
        a2p = jnp.pad(a2[l], ((0, LANES - D_AAA_LORA), (0, 0))).astype(BF16)
        g2p = jnp.pad(g2[l], ((0, LORA_PAD - XG_OFF - D_GATE_LORA), (0, 0))).astype(BF16)
        g1 = norm1_g[l][None, :]

        row512 = lambda i: (i, 0)
        tok_a = pl.BlockSpec((tm_a, A_WIDTH), row512)
        vec_a = _full((1, A_WIDTH))
        outs = pl.pallas_call(
            functools.partial(_rwkv_prep_kernel, seq // tm_a),
            grid=(n_tok // tm_a,),
            in_specs=[pl.BlockSpec((tm_a, d), row512),
                      _full((1, d)), _full((d, A_PROJ)), _full((1, A_PROJ)), vec_a,
                      _full((LANES, A_WIDTH)), vec_a, _full((LANES, A_WIDTH)),
                      _full((LORA_PAD - XG_OFF, A_WIDTH)), vec_a, vec_a,
                      _full((A_WIDTH, A_WIDTH))],
            out_specs=[tok_a] * 7,
            out_shape=[jax.ShapeDtypeStruct((n_tok, A_WIDTH), F32)] * 7,
            scratch_shapes=[pltpu.VMEM((8, A_PROJ), F32)],
            compiler_params=_params(("arbitrary",)),
            name="rwkv_prep",
        )(xf, g1, w_a, mu_a, w0[l][None, :], w2p, a0[l][None, :], a2p, g2p,
          k_k[l][None, :], k_a[l][None, :], ones_bd)
        r_, k_, v_, a_, b_, ld_, gg_ = outs

        bs_full = jnp.repeat(b_s[l].T, B_GROUP_CH, axis=1)
        tok_d = pl.BlockSpec((tm_b, d), row512)
        ybg, ga = pl.pallas_call(
            _gmlp_kernel,
            grid=(n_tok // tm_b,),
            in_specs=[tok_d, _full((1, d)), _full((d, b_cols)), _full((d, 2 * d)),
                      _full((1, 2 * d)), _full((1, B_WIDTH)), _full((1, B_WIDTH)),
                      _full((B_GROUPS, GMLP_BLOCK, GMLP_BLOCK)), _full((GMLP_BLOCK, B_WIDTH)),
                      _full((B_WIDTH, d))],
            out_specs=[tok_d, tok_d],
            out_shape=[jax.ShapeDtypeStruct((n_tok, d), BF16)] * 2,
            compiler_params=_params(("parallel",)),
            name="gmlp",
        )(xf, g1, w_b, w_g, b_gate[l][None, :], lnv_g[l][None, :], lnv_b[l][None, :],
          w_s[l], bs_full, w_oB[l].astype(BF16))

        n_pairs = A_WIDTH // HEAD_PAIR
        n_chunks = WKV_TOKENS // WKV_CHUNK
        tok_w = pl.BlockSpec((WKV_SEQS, WKV_TOKENS, A_WIDTH), lambda bi, ti: (bi, ti, 0))
        vec_w = _full((1, A_WIDTH))
        seq_major = lambda t: t.reshape(bsz, seq, A_WIDTH)
        tok_scratch = pltpu.VMEM((WKV_SEQS, WKV_TOKENS, A_WIDTH), BF16)
        ya_in = pl.pallas_call(
            _wkv_kernel,
            grid=(bsz // WKV_SEQS, seq // WKV_TOKENS),
            in_specs=[tok_w] * 7 + [vec_w, vec_w, vec_w, _full((HEAD_PAIR, HEAD_PAIR))],
            out_specs=tok_w,
            out_shape=jax.ShapeDtypeStruct((bsz, seq, A_WIDTH), F32),
            scratch_shapes=[pltpu.VMEM((WKV_SEQS, n_pairs, HEAD_PAIR, HEAD_PAIR), F32)]
            + [tok_scratch] * 5
            + [pltpu.VMEM((WKV_SEQS, n_chunks, n_pairs, HEAD_PAIR, HEAD_PAIR), BF16),
               pltpu.VMEM((WKV_SEQS, n_chunks, n_pairs, HEAD_PAIR, HEAD_PAIR), F32)],
            compiler_params=_params(("parallel", "arbitrary")),
            name="wkv",
        )(*[seq_major(t) for t in (r_, k_, v_, a_, b_, ld_, gg_)], r_k[l].reshape(1, A_WIDTH),
          lnx_g[l][None, :], lnx_b[l][None, :],
          ones_bd[:HEAD_PAIR, :HEAD_PAIR]).reshape(n_tok, A_WIDTH)

        w_r = _place(LANES, [(0, jnp.transpose(w_re[l], (1, 0, 2)).reshape(d, N_EXPERTS)),
                             (N_EXPERTS, w_rg[l])])
        b_r = _place(LANES, [(0, b_re[l].reshape(1, N_EXPERTS)), (N_EXPERTS, b_rg[l][None, :])])
        wr_hi = w_r.astype(BF16)
        wr_lo = (w_r - wr_hi.astype(F32)).astype(BF16)
        tok_p = pl.BlockSpec((tm_p, d), row512)
        lane_p = pl.BlockSpec((tm_p, LANES), row512)
        n_sub = n_tok // MOE_SUB
        sorted_rows = tm_p // MOE_SUB * RUN_SLOTS
        x2, h_sorted, route, wts, cnt = pl.pallas_call(
            _post_kernel,
            grid=(n_tok // tm_p,),
            in_specs=[tok_p, pl.BlockSpec((tm_p, A_WIDTH), row512), tok_p, tok_p,
                      _full((A_WIDTH, d)), _full((d, d)), _full((1, d)), _full((d, 2 * LANES)),
                      _full((d, LANES)), _full((1, LANES))],
            out_specs=[tok_p, pl.BlockSpec((sorted_rows, d), row512), lane_p, lane_p,
                       pl.BlockSpec((8, LANES), row512)],
            out_shape=[jax.ShapeDtypeStruct((n_tok, d), F32),
                       jax.ShapeDtypeStruct((n_sub * RUN_SLOTS, d), BF16),
                       jax.ShapeDtypeStruct((n_tok, LANES), jnp.int32),
                       jax.ShapeDtypeStruct((n_tok, LANES), F32),
                       jax.ShapeDtypeStruct((n_tok // tm_p * 8, LANES), F32)],
            compiler_params=_params(("parallel",)),
            name="post",
        )(xf, ya_in, ga, ybg, w_oA[l].astype(BF16), w_out[l].astype(BF16),
          norm2_g[l][None, :], jnp.concatenate([wr_hi, wr_lo], axis=1), wr_hi, b_r)

        run_chunks = cnt.reshape(n_tok // tm_p, 8, LANES)[:, :tm_p // MOE_SUB, :N_EXPERTS]
        run_chunks = run_chunks.reshape(n_sub, N_EXPERTS).astype(jnp.int32)
        run_start = (jnp.cumsum(run_chunks, axis=1) - run_chunks) * RUN_ALIGN
        run_row = (jnp.arange(n_sub, dtype=jnp.int32)[:, None] * RUN_SLOTS + run_start).T.reshape(-1)
        run_n = run_chunks.T.reshape(-1)
        run_end = jnp.cumsum(run_n)
        first_chunk = jnp.concatenate([jnp.zeros((1,), jnp.int32), run_end[n_sub - 1::n_sub]])
        pos = jnp.arange(n_sub * RUN_SLOTS // RUN_ALIGN, dtype=jnp.int32)
        run_of = jnp.sum(pos[:, None] >= run_end[None, :], axis=1)
        in_run = run_of[:, None] == jnp.arange(run_n.shape[0], dtype=jnp.int32)[None, :]
        pick = lambda t: jnp.sum(jnp.where(in_run, t[None, :], 0), axis=1)
        chunk_rows = pick(run_row) + (pos - pick(run_end - run_n)) * RUN_ALIGN

        per_expert = lambda e, *_: (e, 0, 0)
        y_sorted = pl.pallas_call(
            _expert_kernel,
            grid_spec=pltpu.PrefetchScalarGridSpec(
                num_scalar_prefetch=2, grid=(N_EXPERTS,),
                in_specs=[pl.BlockSpec(memory_space=pl.ANY),
                          pl.BlockSpec((1, d, D_EXPERT), per_expert),
                          pl.BlockSpec((1, d, D_EXPERT), per_expert),
                          pl.BlockSpec((1, D_EXPERT, d), per_expert)],
                out_specs=pl.BlockSpec(memory_space=pl.ANY),
                scratch_shapes=[pltpu.VMEM((2, EXPERT_ROWS, d), BF16),
                                pltpu.VMEM((d, 2 * D_EXPERT), BF16),
                                pltpu.VMEM((D_EXPERT, d), BF16),
                                pltpu.SemaphoreType.DMA((4,)),
                                pltpu.SMEM((1,), jnp.int32)]),
            out_shape=jax.ShapeDtypeStruct((n_sub * RUN_SLOTS, d), BF16),
            input_output_aliases={2: 0},
            compiler_params=_params(("arbitrary",)),
            name="moe_experts",
        )(chunk_rows, first_chunk, h_sorted, w_e_gate[l], w_e_up[l], w_e_down[l])

        xf = pl.pallas_call(
            _combine_kernel,
            grid=(n_tok // tm_p,),
            in_specs=[tok_p, lane_p, lane_p, _full((1, d)),
                      pl.BlockSpec((sorted_rows, d), row512)],
            out_specs=tok_p,
            out_shape=jax.ShapeDtypeStruct((n_tok, d), F32),
            compiler_params=_params(("parallel",)),
            name="moe_combine",
        )(x2, route, wts, final_g[None, :], y_sorted)

    return xf.reshape(bsz, seq, d)
```

```python
import functools

import jax
import jax.numpy as jnp
from jax import lax
from jax.experimental import pallas as pl
from jax.experimental.pallas import tpu as pltpu

F32 = jnp.float32
BF16 = jnp.bfloat16

D_MODEL = 1024
A_WIDTH = 512
A_HEAD = 64
D_DECAY_LORA = 64
D_AAA_LORA = 64
D_GATE_LORA = 160
B_WIDTH = 512
B_GROUPS = 4
B_GROUP_CH = 128
GMLP_BLOCK = 128
N_GROUPS = 4
EXPERTS_PER_GROUP = 8
N_EXPERTS = 32
D_EXPERT = 256
NORM_EPS = 1e-6
LN_EPS = 1e-5
LNX_EPS = 64e-5

LANES = 128
LORA_PAD = 512
XW_OFF, XA_OFF, XG_OFF = 0, 128, 256
A_PROJ = 3 * A_WIDTH + LORA_PAD
WKV_CHUNK = 64
HEAD_PAIR = 2 * A_HEAD
WKV_SEQS = 4
WKV_TOKENS = 128
WKV_PREP_GROUP = 4
WKV_NORM_GROUP = 2
MOE_SUB = 256
RUN_ALIGN = 16
RUN_SLOTS = 2 * MOE_SUB + N_EXPERTS * RUN_ALIGN
EXPERT_ROWS = 2048
EXPERT_TILE = 512
DENSE_SPLIT = 256
CHUNK_UNROLL = 4
VMEM_LIMIT = 48 * 1024 * 1024


def _rms(x, g):
    return x * lax.rsqrt(jnp.mean(x * x, axis=-1, keepdims=True) + NORM_EPS) * g


def _dot(a, b):
    return jnp.dot(a, b, preferred_element_type=F32)


def _dot_nt(a, b):
    return lax.dot_general(a, b, (((1,), (1,)), ((), ())), preferred_element_type=F32)


def _split2(x):
    hi = x.astype(BF16)
    lo = (x - hi.astype(F32)).astype(BF16)
    return hi, lo


def _split3(x):
    hi = x.astype(BF16)
    r1 = x - hi.astype(F32)
    mid = r1.astype(BF16)
    lo = (r1 - mid.astype(F32)).astype(BF16)
    return hi, mid, lo


def _pair_head_sums(xs, ones_pair):
    n_tiles = A_WIDTH // HEAD_PAIR
    rows = xs[0].shape[0]
    tiles = [x[:, t * HEAD_PAIR:(t + 1) * HEAD_PAIR].astype(BF16) for x in xs for t in range(n_tiles)]
    sums = _dot(jnp.concatenate(tiles, axis=0), ones_pair)
    return [jnp.concatenate([sums[(i * n_tiles + t) * rows:(i * n_tiles + t + 1) * rows]
                             for t in range(n_tiles)], axis=1) for i in range(len(xs))]


def _w_in_kernel(w_ref, wa_out, wb_out, wg_out):
    s_rkv = 3 * A_WIDTH
    s_w = s_rkv + D_DECAY_LORA
    s_a = s_w + D_AAA_LORA
    a_cols = s_a + D_GATE_LORA
    b_end = a_cols + 2 * B_WIDTH
    wa_out[...] = jnp.zeros_like(wa_out)
    wa_out[:s_w, :] = w_ref[:s_w, :].astype(BF16)
    wa_out[s_rkv + XA_OFF:s_rkv + XA_OFF + D_AAA_LORA, :] = w_ref[s_w:s_a, :].astype(BF16)
    wa_out[s_rkv + XG_OFF:s_rkv + XG_OFF + D_GATE_LORA, :] = w_ref[s_a:a_cols, :].astype(BF16)
    wb_out[...] = w_ref[a_cols:b_end, :].astype(BF16)
    wg_out[...] = w_ref[b_end:, :].astype(BF16)


def _rwkv_prep_kernel(tiles_per_seq, x_ref, g1_ref, wa_ref, mu_ref, w0_ref,
                      w2_ref, a0_ref, a2_ref, g2_ref, kk_ref, ka_ref, ones_ref,
                      r_out, k_out, v_out, a_out, b_out, ld_out, g_out, tail_ref):
    i = pl.program_id(0)
    tm = x_ref.shape[0]
    parts = [slice(j * DENSE_SPLIT, (j + 1) * DENSE_SPLIT) for j in range(tm // DENSE_SPLIT)]
    g1 = g1_ref[...]
    h = [_rms(x_ref[p, :], g1).astype(BF16) for p in parts]
    proj = [_dot_nt(hh, wa_ref[...]) for hh in h]
    prev = [jnp.where(i % tiles_per_seq == 0, 0.0, tail_ref[7:8, :])]
    prev += [p[DENSE_SPLIT - 1:, :] for p in proj[:-1]]
    tail_ref[...] = proj[-1][DENSE_SPLIT - 8:, :]
    row = lax.broadcasted_iota(jnp.int32, proj[0].shape, 0)
    pm = []
    for p, pv in zip(proj, prev):
        shifted = jnp.where(row == 0, pv, pltpu.roll(p, 1, axis=0))
        pm.append(p + mu_ref[...] * (shifted - p))
    lora = [x[:, 3 * A_WIDTH:] for x in pm]
    decay = [_dot(jnp.tanh(x[:, XW_OFF:XW_OFF + LANES]).astype(BF16), w2_ref[...]) for x in lora]
    rate = [_dot(x[:, XA_OFF:XA_OFF + LANES].astype(BF16), a2_ref[...]) for x in lora]
    gate = [_dot(jax.nn.sigmoid(x[:, XG_OFF:]).astype(BF16), g2_ref[...]) for x in lora]
    kks = [x[:, A_WIDTH:2 * A_WIDTH] * kk_ref[...] for x in pm]
    sq = [_dot((kk * kk).astype(BF16), ones_ref[...]) for kk in kks]
    for j, p in enumerate(parts):
        k = pm[j][:, A_WIDTH:2 * A_WIDTH]
        z = -(w0_ref[...] + decay[j])
        softplus = jnp.maximum(z, 0.0) + jnp.log(1.0 + jnp.exp(-jnp.abs(z)))
        w = -softplus - 0.5
        a_lr = jax.nn.sigmoid(a0_ref[...] + rate[j])
        kk = kks[j] / jnp.maximum(jnp.sqrt(sq[j]), 1e-12)
        r_out[p, :] = pm[j][:, 0:A_WIDTH]
        k_out[p, :] = k * (1.0 + (a_lr - 1.0) * ka_ref[...])
        v_out[p, :] = pm[j][:, 2 * A_WIDTH:3 * A_WIDTH]
        a_out[p, :] = -kk
        b_out[p, :] = kk * a_lr
        ld_out[p, :] = -jnp.exp(w)
        g_out[p, :] = gate[j]


def _gmlp_kernel(x_ref, g1_ref, wb_ref, wg_ref, bg_ref, lng_ref, lnb_ref, ws_ref, bs_ref,
                 wo_ref, ybg_out, ga_out):
    tm = x_ref.shape[0]
    n_split = tm // DENSE_SPLIT
    parts = [slice(i * DENSE_SPLIT, (i + 1) * DENSE_SPLIT) for i in range(n_split)]
    g1 = g1_ref[...]
    h = [_rms(x_ref[p, :], g1).astype(BF16) for p in parts]
    pb = [_dot_nt(hh, wb_ref[...]) for hh in h]
    tri = (lax.broadcasted_iota(jnp.int32, (GMLP_BLOCK, GMLP_BLOCK), 0)
           >= lax.broadcasted_iota(jnp.int32, (GMLP_BLOCK, GMLP_BLOCK), 1))
    ws = [jnp.where(tri, ws_ref[grp], 0.0).astype(BF16) for grp in range(B_GROUPS)]
    bs = bs_ref[...]
    us, vns = [], []
    for x in pb:
        z = 0.5 * x * (1.0 + lax.erf(x * (2.0 ** -0.5)))
        us.append(z[:, :B_WIDTH])
        v = z[:, B_WIDTH:]
        mean = jnp.mean(v, axis=-1, keepdims=True)
        vc = v - mean
        var = jnp.mean(vc * vc, axis=-1, keepdims=True)
        vns.append((vc * lax.rsqrt(var + LN_EPS) * lng_ref[...] + lnb_ref[...]).astype(BF16))
    svs = []
    for vn in vns:
        rows = []
        for blk in range(DENSE_SPLIT // GMLP_BLOCK):
            cols = [_dot(ws[grp], vn[blk * GMLP_BLOCK:(blk + 1) * GMLP_BLOCK,
                                     grp * B_GROUP_CH:(grp + 1) * B_GROUP_CH])
                    for grp in range(B_GROUPS)]
            rows.append(jnp.concatenate(cols, axis=1) + bs)
        svs.append(jnp.concatenate(rows, axis=0))
    yb = [_dot((u * sv).astype(BF16), wo_ref[...]) for u, sv in zip(us, svs)]
    gates = [jax.nn.sigmoid(_dot_nt(hh, wg_ref[...]) + bg_ref[...]) for hh in h]
    for p, g, y in zip(parts, gates, yb):
        ga_out[p, :] = g[:, :D_MODEL].astype(BF16)
        ybg_out[p, :] = (g[:, D_MODEL:] * y).astype(BF16)


def _wkv_kernel(r_ref, k_ref, v_ref, a_ref, b_ref, ld_ref, g_ref, rk_ref, lng_ref, lnb_ref,
                ones_ref, o_ref, st_ref, ta_s, tl_s, arb_s, ark_s, rt_s, bkt_s, dcol_s):
    C = WKV_CHUNK
    bb, tb, _ = r_ref.shape
    n_chunks = tb // C
    n_pairs = A_WIDTH // HEAD_PAIR

    @pl.when(pl.program_id(1) == 0)
    def _():
        st_ref[...] = jnp.zeros_like(st_ref)

    row = lax.broadcasted_iota(jnp.int32, (C, HEAD_PAIR), 0)
    src = lax.broadcasted_iota(jnp.int32, (C, HEAD_PAIR), 1) & (C - 1)
    incl = src <= row
    strict = src < row
    eye_pair = jnp.where(src == row, 1.0, 0.0)
    bd_mask = ((lax.broadcasted_iota(jnp.int32, (HEAD_PAIR, HEAD_PAIR), 0) >= A_HEAD)
               == (lax.broadcasted_iota(jnp.int32, (HEAD_PAIR, HEAD_PAIR), 1) >= A_HEAD))
    tri_c = (lax.broadcasted_iota(jnp.int32, (C, C), 0)
             >= lax.broadcasted_iota(jnp.int32, (C, C), 1)).astype(BF16)
    ones_pair = ones_ref[...]
    pair_cols = [slice(p * HEAD_PAIR, (p + 1) * HEAD_PAIR) for p in range(n_pairs)]

    def bd(x):
        xb = x.astype(BF16)
        return jnp.where(bd_mask, jnp.concatenate([xb, xb], axis=0), jnp.zeros((), BF16))

    def prep(it, carry):
        where, lhs, rhs, ats = [], [], [], []
        for j in range(WKV_PREP_GROUP):
            flat = it * WKV_PREP_GROUP + j
            b = flat // n_chunks
            c = flat % n_chunks
            sl = pl.ds(pl.multiple_of(c * C, C), C)
            ld = ld_ref[b, sl, :]
            hi, mid, lo = _split3(ld)
            cl = _dot(tri_c, hi) + _dot(tri_c, mid) + _dot(tri_c, lo)
            cl_end = cl[C - 1:C, :]
            k = k_ref[b, sl, :]
            bv = b_ref[b, sl, :]
            d_inv = jnp.exp(-cl)
            d_tail = jnp.exp(cl_end - cl)
            rt = (r_ref[b, sl, :] * jnp.exp(cl)).astype(BF16)
            rt_s[b, sl, :] = rt
            kt = k * d_inv
            bt = bv * d_inv
            at = a_ref[b, sl, :] * jnp.exp(cl - ld)
            kd = k * d_tail
            bdk = bv * d_tail
            d_end = jnp.exp(cl_end)
            for p, cs in enumerate(pair_cols):
                bkt_s[b, c, p] = jnp.concatenate([bdk[:, cs], kd[:, cs]], axis=0).T.astype(BF16)
                dcol_s[b, c, p] = jnp.broadcast_to(d_end[:, cs], (HEAD_PAIR, HEAD_PAIR)).T
                where.append((b, sl, cs))
                ats.append(at[:, cs])
                lhs.append(jnp.concatenate([rt[:, cs], at[:, cs].astype(BF16)], axis=0))
                rhs.append(jnp.concatenate([bd(kt[:, cs]), bd(bt[:, cs])], axis=0))
        n = len(where)
        amat = [_dot_nt(lhs[i], rhs[i]) for i in range(n)]
        l_ak, l_ab = [], []
        for i, (b, sl, cs) in enumerate(where):
            ark_s[b, sl, cs] = jnp.where(incl, amat[i][:C, :HEAD_PAIR], 0.0).astype(BF16)
            arb_s[b, sl, cs] = jnp.where(incl, amat[i][:C, HEAD_PAIR:], 0.0).astype(BF16)
            l_ak.append(jnp.where(strict, amat[i][C:, :HEAD_PAIR], 0.0))
            l_ab.append(jnp.where(strict, amat[i][C:, HEAD_PAIR:], 0.0))

        t_mat = [eye_pair + l for l in l_ab]
        q = [_dot(l.astype(BF16), bd(l)) for l in l_ab]
        n_sq = 1
        while 2 * n_sq < C // 2:
            both = [_dot(q[i].astype(BF16), jnp.concatenate([bd(q[i]), bd(t_mat[i])], axis=1))
                    for i in range(n)]
            q = [x[:, :HEAD_PAIR] for x in both]
            t_mat = [t_mat[i] + both[i][:, HEAD_PAIR:] for i in range(n)]
            n_sq *= 2
        t_mat = [t_mat[i] + _dot(q[i].astype(BF16), bd(t_mat[i])) for i in range(n)]
        tal = [_dot(t_mat[i].astype(BF16), jnp.concatenate([bd(ats[i]), bd(l_ak[i])], axis=1))
               for i in range(n)]
        for i, (b, sl, cs) in enumerate(where):
            ta_s[b, sl, cs] = tal[i][:, :HEAD_PAIR].astype(BF16)
            tl_s[b, sl, cs] = tal[i][:, HEAD_PAIR:].astype(BF16)
        return carry

    lax.fori_loop(0, bb * n_chunks // WKV_PREP_GROUP, prep, 0)

    def step(c, carry):
        sl = pl.ds(pl.multiple_of(c * C, C), C)
        chains = [(b, p) for b in range(bb) for p in range(n_pairs)]
        v = [v_ref[b, sl, :] for b in range(bb)]
        st = [st_ref[b, p] for b, p in chains]
        st_b = [x.astype(BF16) for x in st]
        bd_v = [bd(v[b][:, pair_cols[p]]) for b, p in chains]
        on_st = [_dot(jnp.concatenate([ta_s[b, sl, pair_cols[p]], rt_s[b, sl, pair_cols[p]]],
                                      axis=0), st_b[i]) for i, (b, p) in enumerate(chains)]
        on_v = [_dot(jnp.concatenate([tl_s[b, sl, pair_cols[p]], ark_s[b, sl, pair_cols[p]]],
                                     axis=0), bd_v[i]) for i, (b, p) in enumerate(chains)]
        u = [on_st[i][:C] + on_v[i][:C] for i in range(len(chains))]
        for i, (b, p) in enumerate(chains):
            uv = jnp.concatenate([u[i], v[b][:, pair_cols[p]]], axis=0).astype(BF16)
            st_ref[b, p] = (dcol_s[b, c, p] * st[i]
                            + jnp.where(bd_mask, _dot(bkt_s[b, c, p], uv), 0.0))
        ys = [on_st[i][C:] + on_v[i][C:] + _dot(arb_s[b, sl, pair_cols[p]], bd(u[i]))
              for i, (b, p) in enumerate(chains)]
        for b in range(bb):
            o_ref[b, sl, :] = jnp.concatenate(ys[b * n_pairs:(b + 1) * n_pairs], axis=1)
        return carry

    lax.fori_loop(0, n_chunks, step, 0)

    def finish(it, carry):
        items = [(b, pl.ds(pl.multiple_of((it * WKV_NORM_GROUP + j) * C, C), C))
                 for j in range(WKV_NORM_GROUP) for b in range(bb)]
        y = [o_ref[b, sl, :] for b, sl in items]
        rkr = [r_ref[b, sl, :] * k_ref[b, sl, :] * rk_ref[...] for b, sl in items]
        sums = [_pair_head_sums([y[i], rkr[i]], ones_pair) for i in range(len(items))]
        yc = [y[i] - sums[i][0] * (1.0 / A_HEAD) for i in range(len(items))]
        var = [_pair_head_sums([x * x], ones_pair)[0] * (1.0 / A_HEAD) for x in yc]
        for i, (b, sl) in enumerate(items):
            yn = yc[i] * lax.rsqrt(var[i] + LNX_EPS) * lng_ref[...] + lnb_ref[...]
            o_ref[b, sl, :] = (yn + sums[i][1] * v_ref[b, sl, :]) * g_ref[b, sl, :]
        return carry

    lax.fori_loop(0, n_chunks // WKV_NORM_GROUP, finish, 0)


def _post_kernel(x_ref, ya_ref, ga_ref, ybg_ref, woa_ref, wout_ref, g2_ref, wr_ref,
                 wr_hi_ref, br_ref, x2_out, hl_out, route_out, wts_out, cnt_out):
    n_sub = x_ref.shape[0] // MOE_SUB
    groups = [slice(g * MOE_SUB, (g + 1) * MOE_SUB) for g in range(n_sub)]
    y_a = [_dot(ya_ref[p, :].astype(BF16), woa_ref[...]) for p in groups]
    x2 = [x_ref[p, :] + _dot((ga_ref[p, :] * y + ybg_ref[p, :]).astype(BF16), wout_ref[...])
          for p, y in zip(groups, y_a)]
    h2 = [_rms(x, g2_ref[...]) for x in x2]
    split = [_split2(h) for h in h2]
    hi_w = [_dot(hi, wr_ref[...]) for hi, _ in split]
    lo_w = [_dot(lo, wr_hi_ref[...]) for _, lo in split]
    lane = lax.broadcasted_iota(jnp.int32, (MOE_SUB, LANES), 1)
    neg = jnp.float32(-jnp.inf)
    big = jnp.int32(LANES)
    is_grp = (lane >= N_EXPERTS) & (lane < N_EXPERTS + N_GROUPS)
    tri = (lax.broadcasted_iota(jnp.int32, (MOE_SUB, MOE_SUB), 0)
           >= lax.broadcasted_iota(jnp.int32, (MOE_SUB, MOE_SUB), 1)).astype(BF16)
    picks, onehots = [], []
    for g, p in enumerate(groups):
        x2_out[p, :] = x2[g]
        logits = hi_w[g][:, :LANES] + hi_w[g][:, LANES:] + lo_w[g] + br_ref[...]
        gl = jnp.where(is_grp, logits, neg)
        gmax = jnp.max(gl, axis=-1, keepdims=True)
        g_p = 1.0 / jnp.sum(jnp.exp(gl - gmax), axis=-1, keepdims=True)
        g_idx = jnp.min(jnp.where(gl == gmax, lane, big), axis=-1, keepdims=True) - N_EXPERTS
        lo_lane = g_idx * EXPERTS_PER_GROUP
        in_grp = (lane >= lo_lane) & (lane < lo_lane + EXPERTS_PER_GROUP)
        el = jnp.where(in_grp, logits, neg)
        e1 = jnp.max(el, axis=-1, keepdims=True)
        i1 = jnp.min(jnp.where(el == e1, lane, big), axis=-1, keepdims=True)
        el2 = jnp.where(lane == i1, neg, el)
        e2 = jnp.max(el2, axis=-1, keepdims=True)
        i2 = jnp.min(jnp.where(el2 == e2, lane, big), axis=-1, keepdims=True)
        t = jnp.exp(e2 - e1)
        wts_out[p, :] = jnp.where(lane == 0, g_p / (1.0 + t),
                                  jnp.where(lane == 1, g_p * t / (1.0 + t), 0.0))
        picks.append((i1, i2))
        onehots.append(jnp.where((lane == i1) | (lane == i2), 1.0, 0.0))

    incl = [_dot(tri, oh.astype(BF16)) for oh in onehots]
    row8 = lax.broadcasted_iota(jnp.int32, (8, LANES), 0)
    chunks = jnp.zeros((8, LANES), F32)
    for g in range(n_sub):
        chunks = jnp.where(row8 == g, jnp.ceil(incl[g][MOE_SUB - 1:, :] * (1.0 / RUN_ALIGN)),
                           chunks)
    cnt_out[...] = chunks
    upper = (lax.broadcasted_iota(jnp.int32, (LANES, LANES), 0)
             < lax.broadcasted_iota(jnp.int32, (LANES, LANES), 1)).astype(BF16)
    run_start = _dot(chunks.astype(BF16), upper) * RUN_ALIGN
    routes = []
    for g, p in enumerate(groups):
        slot_of = run_start[g:g + 1, :] + incl[g] - onehots[g]
        i1, i2 = picks[g]
        slot1 = jnp.sum(jnp.where(lane == i1, slot_of, 0.0), axis=-1, keepdims=True)
        slot2 = jnp.sum(jnp.where(lane == i2, slot_of, 0.0), axis=-1, keepdims=True)
        route = jnp.where(lane == 0, slot1, jnp.where(lane == 1, slot2, 0.0)).astype(jnp.int32)
        route_out[p, :] = route
        routes.append(route.T)

    slot_ids = lax.broadcasted_iota(jnp.int32, (RUN_SLOTS, MOE_SUB), 0)
    for g in range(n_sub):
        pick = jnp.where((slot_ids == routes[g][0:1, :]) | (slot_ids == routes[g][1:2, :]),
                         1.0, 0.0).astype(BF16)
        hl_out[g * RUN_SLOTS:(g + 1) * RUN_SLOTS, :] = _dot(pick, h2[g].astype(BF16)).astype(BF16)


def _expert_kernel(chunk_s, first_s, hl_in_ref, wg_ref, wu_ref, wd_ref, hl_ref, xbuf, wgu_s, wd_s,
                   sem, drain_s):
    del hl_in_ref
    e = pl.program_id(0)
    pass_chunks = EXPERT_ROWS // RUN_ALIGN
    mine = e % 2
    other = 1 - mine
    gather_sem = lambda buf: sem.at[buf]
    scatter_sem = lambda buf: sem.at[2 + buf]

    def first_pass(ex):
        lo = first_s[ex]
        return lo, jnp.minimum(lo + pass_chunks, first_s[ex + 1])

    def for_chunks(buf, c_lo, c_hi, fn):
        def one(c):
            fn(hl_ref.at[pl.ds(pl.multiple_of(chunk_s[c], RUN_ALIGN), RUN_ALIGN)],
               xbuf.at[buf, pl.ds(pl.multiple_of((c - c_lo) * RUN_ALIGN, RUN_ALIGN), RUN_ALIGN)])

        def block(i, carry):
            for u in range(CHUNK_UNROLL):
                one(c_lo + i * CHUNK_UNROLL + u)
            return carry

        def tail(c, carry):
            one(c)
            return carry

        n_blocks = (c_hi - c_lo) // CHUNK_UNROLL
        lax.fori_loop(0, n_blocks, block, 0)
        lax.fori_loop(c_lo + n_blocks * CHUNK_UNROLL, c_hi, tail, 0)

    def gather(buf, c_lo, c_hi):
        for_chunks(buf, c_lo, c_hi, lambda hbm, vm: pltpu.make_async_copy(
            hbm, vm, gather_sem(buf)).start())

    def scatter(buf, c_lo, c_hi):
        for_chunks(buf, c_lo, c_hi, lambda hbm, vm: pltpu.make_async_copy(
            vm, hbm, scatter_sem(buf)).start())

    def wait_chunks(which_sem, n):
        def wait_rows(rows):
            def body(c, carry):
                pltpu.make_async_copy(hl_ref.at[pl.ds(0, rows)], xbuf.at[0, pl.ds(0, rows)],
                                      which_sem).wait()
                return carry
            return body

        lax.fori_loop(0, n // CHUNK_UNROLL, wait_rows(CHUNK_UNROLL * RUN_ALIGN), 0)
        lax.fori_loop(0, n % CHUNK_UNROLL, wait_rows(RUN_ALIGN), 0)

    def evaluate(buf, n_chunks):
        def tile(k, carry):
            sl = pl.ds(pl.multiple_of(k * EXPERT_TILE, EXPERT_TILE), EXPERT_TILE)
            gu = _dot(xbuf[buf, sl, :], wgu_s[...])
            gate = gu[:, :D_EXPERT]
            act = gate * jax.nn.sigmoid(gate) * gu[:, D_EXPERT:]
            xbuf[buf, sl, :] = _dot(act.astype(BF16), wd_s[...]).astype(BF16)
            return carry

        lax.fori_loop(0, (n_chunks * RUN_ALIGN + EXPERT_TILE - 1) // EXPERT_TILE, tile, 0)

    @pl.when(e == 0)
    def _():
        xbuf[...] = jnp.zeros_like(xbuf)
        gather(0, *first_pass(0))
        drain_s[0] = 0

    wgu_s[:, :D_EXPERT] = wg_ref[0].astype(BF16)
    wgu_s[:, D_EXPERT:] = wu_ref[0].astype(BF16)
    wd_s[...] = wd_ref[0].astype(BF16)

    c_lo, c_hi = first_pass(e)
    wait_chunks(gather_sem(mine), c_hi - c_lo)
    evaluate(mine, c_hi - c_lo)
    wait_chunks(scatter_sem(other), drain_s[0])
    scatter(mine, c_lo, c_hi)

    def later_pass(lo):
        hi = jnp.minimum(lo + pass_chunks, first_s[e + 1])
        gather(other, lo, hi)
        wait_chunks(gather_sem(other), hi - lo)
        evaluate(other, hi - lo)
        scatter(other, lo, hi)
        wait_chunks(scatter_sem(other), hi - lo)
        return hi

    lax.while_loop(lambda c: c < first_s[e + 1], later_pass, c_hi)
    drain_s[0] = c_hi - c_lo

    @pl.when(e + 1 < pl.num_programs(0))
    def _():
        gather(other, *first_pass(e + 1))

    @pl.when(e + 1 == pl.num_programs(0))
    def _():
        wait_chunks(scatter_sem(mine), c_hi - c_lo)


def _combine_kernel(x2_ref, route_ref, wts_ref, gf_ref, yl_ref, o_ref):
    slots = lax.broadcasted_iota(jnp.int32, (MOE_SUB, RUN_SLOTS), 1)
    for g in range(x2_ref.shape[0] // MOE_SUB):
        p = slice(g * MOE_SUB, (g + 1) * MOE_SUB)
        route = route_ref[p, :]
        wts = wts_ref[p, :]
        pick = (jnp.where(slots == route[:, 0:1], wts[:, 0:1], 0.0)
                + jnp.where(slots == route[:, 1:2], wts[:, 1:2], 0.0))
        y = _dot(pick.astype(BF16), yl_ref[g * RUN_SLOTS:(g + 1) * RUN_SLOTS, :])
        o_ref[p, :] = _rms(x2_ref[p, :] + y, gf_ref[...])


def _full(shape):
    return pl.BlockSpec(shape, lambda *_: (0,) * len(shape))


def _params(sem):
    return pltpu.CompilerParams(dimension_semantics=sem, vmem_limit_bytes=VMEM_LIMIT)


def _mixer_a_layout(t):
    s_rkv = 3 * A_WIDTH
    s_w = s_rkv + D_DECAY_LORA
    s_a = s_w + D_AAA_LORA
    gap = lambda n: jnp.zeros((t.shape[0], n), t.dtype)
    return jnp.concatenate(
        [t[:, :s_w], gap(XA_OFF - D_DECAY_LORA), t[:, s_w:s_a], gap(XG_OFF - XA_OFF - D_AAA_LORA),
         t[:, s_a:], gap(LORA_PAD - XG_OFF - D_GATE_LORA)], axis=1)


def _place(cols, parts):
    out = jnp.zeros((parts[0][1].shape[0], cols), parts[0][1].dtype)
    for off, arr in parts:
        out = lax.dynamic_update_slice(out, arr, (0, off))
    return out


def kernel(x, norm1_g, w_in, b_gate, tmix_mu, w0, w2, a0, a2, g2, k_k, k_a, r_k, lnx_g, lnx_b,
           w_oA, lnv_g, lnv_b, w_s, b_s, w_oB, w_out, norm2_g, w_rg, b_rg, w_re, b_re,
           w_e_gate, w_e_up, w_e_down, final_g):
    bsz, seq, d = x.shape
    n_tok = bsz * seq
    depth = norm1_g.shape[0]
    assert depth == 1, "the moe kernel fuses the final norm, so it must be the last layer"
    assert bsz % WKV_SEQS == 0 and seq % WKV_TOKENS == 0
    xf = x.reshape(n_tok, d)

    s_rkv = 3 * A_WIDTH
    s_w = s_rkv + D_DECAY_LORA
    s_a = s_w + D_AAA_LORA
    a_cols = s_a + D_GATE_LORA
    b_cols = 2 * B_WIDTH

    ones_bd = (jnp.arange(A_WIDTH)[:, None] // A_HEAD
               == jnp.arange(A_WIDTH)[None, :] // A_HEAD).astype(BF16)

    tm_a = 512
    tm_b = 512
    tm_p = 512

    for l in range(depth):
        wl = w_in[l]
        rows_w = 256
        w_a, w_b, w_g = pl.pallas_call(
            _w_in_kernel,
            grid=(d // rows_w,),
            in_specs=[pl.BlockSpec((wl.shape[1], rows_w), lambda i: (0, i))],
            out_specs=[pl.BlockSpec((A_PROJ, rows_w), lambda i: (0, i)),
                       pl.BlockSpec((b_cols, rows_w), lambda i: (0, i)),
                       pl.BlockSpec((2 * d, rows_w), lambda i: (0, i))],
            out_shape=[jax.ShapeDtypeStruct((A_PROJ, d), BF16),
                       jax.ShapeDtypeStruct((b_cols, d), BF16),
                       jax.ShapeDtypeStruct((2 * d, d), BF16)],
            compiler_params=_params(("parallel",)),
            name="w_in_layout",
        )(wl.T)
        mu_a = _mixer_a_layout(tmix_mu[l][None, :])
        w2p = jnp.pad(w2[l], ((0, LANES - D_DECAY_LORA), (0, 0))).astype(BF16)
        a2p = jnp.pad(a2[l], ((0, LANES - D_AAA_LORA), (0, 0))).astype(BF16)
        g2p = jnp.pad(g2[l], ((0, LORA_PAD - XG_OFF - D_GATE_LORA), (0, 0))).astype(BF16)
        g1 = norm1_g[l][None, :]

        row512 = lambda i: (i, 0)
        tok_a = pl.BlockSpec((tm_a, A_WIDTH), row512)
        vec_a = _full((1, A_WIDTH))
        outs = pl.pallas_call(
            functools.partial(_rwkv_prep_kernel, seq // tm_a),
            grid=(n_tok // tm_a,),
            in_specs=[pl.BlockSpec((tm_a, d), row512),
                      _full((1, d)), _full((A_PROJ, d)), _full((1, A_PROJ)), vec_a,
                      _full((LANES, A_WIDTH)), vec_a, _full((LANES, A_WIDTH)),
                      _full((LORA_PAD - XG_OFF, A_WIDTH)), vec_a, vec_a,
                      _full((A_WIDTH, A_WIDTH))],
            out_specs=[tok_a] * 7,
            out_shape=[jax.ShapeDtypeStruct((n_tok, A_WIDTH), F32)] * 7,
            scratch_shapes=[pltpu.VMEM((8, A_PROJ), F32)],
            compiler_params=_params(("arbitrary",)),
            name="rwkv_prep",
        )(xf, g1, w_a, mu_a, w0[l][None, :], w2p, a0[l][None, :], a2p, g2p,
          k_k[l][None, :], k_a[l][None, :], ones_bd)
        r_, k_, v_, a_, b_, ld_, gg_ = outs

        bs_full = jnp.repeat(b_s[l].T, B_GROUP_CH, axis=1)
        tok_d = pl.BlockSpec((tm_b, d), row512)
        ybg, ga = pl.pallas_call(
            _gmlp_kernel,
            grid=(n_tok // tm_b,),
            in_specs=[tok_d, _full((1, d)), _full((b_cols, d)), _full((2 * d, d)),
                      _full((1, 2 * d)), _full((1, B_WIDTH)), _full((1, B_WIDTH)),
                      _full((B_GROUPS, GMLP_BLOCK, GMLP_BLOCK)), _full((GMLP_BLOCK, B_WIDTH)),
                      _full((B_WIDTH, d))],
            out_specs=[tok_d, tok_d],
            out_shape=[jax.ShapeDtypeStruct((n_tok, d), BF16)] * 2,
            compiler_params=_params(("parallel",)),
            name="gmlp",
        )(xf, g1, w_b, w_g, b_gate[l][None, :], lnv_g[l][None, :], lnv_b[l][None, :],
          w_s[l], bs_full, w_oB[l].astype(BF16))

        n_pairs = A_WIDTH // HEAD_PAIR
        n_chunks = WKV_TOKENS // WKV_CHUNK
        tok_w = pl.BlockSpec((WKV_SEQS, WKV_TOKENS, A_WIDTH), lambda bi, ti: (bi, ti, 0))
        vec_w = _full((1, A_WIDTH))
        seq_major = lambda t: t.reshape(bsz, seq, A_WIDTH)
        tok_scratch = pltpu.VMEM((WKV_SEQS, WKV_TOKENS, A_WIDTH), BF16)
        ya_in = pl.pallas_call(
            _wkv_kernel,
            grid=(bsz // WKV_SEQS, seq // WKV_TOKENS),
            in_specs=[tok_w] * 7 + [vec_w, vec_w, vec_w, _full((HEAD_PAIR, HEAD_PAIR))],
            out_specs=tok_w,
            out_shape=jax.ShapeDtypeStruct((bsz, seq, A_WIDTH), F32),
            scratch_shapes=[pltpu.VMEM((WKV_SEQS, n_pairs, HEAD_PAIR, HEAD_PAIR), F32)]
            + [tok_scratch] * 5
            + [pltpu.VMEM((WKV_SEQS, n_chunks, n_pairs, HEAD_PAIR, HEAD_PAIR), BF16),
               pltpu.VMEM((WKV_SEQS, n_chunks, n_pairs, HEAD_PAIR, HEAD_PAIR), F32)],
            compiler_params=_params(("parallel", "arbitrary")),
            name="wkv",
        )(*[seq_major(t) for t in (r_, k_, v_, a_, b_, ld_, gg_)], r_k[l].reshape(1, A_WIDTH),
          lnx_g[l][None, :], lnx_b[l][None, :],
          ones_bd[:HEAD_PAIR, :HEAD_PAIR]).reshape(n_tok, A_WIDTH)

        w_r = _place(LANES, [(0, jnp.transpose(w_re[l], (1, 0, 2)).reshape(d, N_EXPERTS)),
                             (N_EXPERTS, w_rg[l])])
        b_r = _place(LANES, [(0, b_re[l].reshape(1, N_EXPERTS)), (N_EXPERTS, b_rg[l][None, :])])
        wr_hi = w_r.astype(BF16)
        wr_lo = (w_r - wr_hi.astype(F32)).astype(BF16)
        tok_p = pl.BlockSpec((tm_p, d), row512)
        lane_p = pl.BlockSpec((tm_p, LANES), row512)
        n_sub = n_tok // MOE_SUB
        sorted_rows = tm_p // MOE_SUB * RUN_SLOTS
        x2, h_sorted, route, wts, cnt = pl.pallas_call(
            _post_kernel,
            grid=(n_tok // tm_p,),
            in_specs=[tok_p, pl.BlockSpec((tm_p, A_WIDTH), row512), tok_p, tok_p,
                      _full((A_WIDTH, d)), _full((d, d)), _full((1, d)), _full((d, 2 * LANES)),
                      _full((d, LANES)), _full((1, LANES))],
            out_specs=[tok_p, pl.BlockSpec((sorted_rows, d), row512), lane_p, lane_p,
                       pl.BlockSpec((8, LANES), row512)],
            out_shape=[jax.ShapeDtypeStruct((n_tok, d), F32),
                       jax.ShapeDtypeStruct((n_sub * RUN_SLOTS, d), BF16),
                       jax.ShapeDtypeStruct((n_tok, LANES), jnp.int32),
                       jax.ShapeDtypeStruct((n_tok, LANES), F32),
                       jax.ShapeDtypeStruct((n_tok // tm_p * 8, LANES), F32)],
            compiler_params=_params(("parallel",)),
            name="post",
        )(xf, ya_in, ga, ybg, w_oA[l].astype(BF16), w_out[l].astype(BF16),
          norm2_g[l][None, :], jnp.concatenate([wr_hi, wr_lo], axis=1), wr_hi, b_r)

        run_chunks = cnt.reshape(n_tok // tm_p, 8, LANES)[:, :tm_p // MOE_SUB, :N_EXPERTS]
        run_chunks = run_chunks.reshape(n_sub, N_EXPERTS).astype(jnp.int32)
        run_start = (jnp.cumsum(run_chunks, axis=1) - run_chunks) * RUN_ALIGN
        run_row = (jnp.arange(n_sub, dtype=jnp.int32)[:, None] * RUN_SLOTS + run_start).T.reshape(-1)
        run_n = run_chunks.T.reshape(-1)
        run_end = jnp.cumsum(run_n)
        first_chunk = jnp.concatenate([jnp.zeros((1,), jnp.int32), run_end[n_sub - 1::n_sub]])
        pos = jnp.arange(n_sub * RUN_SLOTS // RUN_ALIGN, dtype=jnp.int32)
        run_of = jnp.sum(pos[:, None] >= run_end[None, :], axis=1)
        in_run = run_of[:, None] == jnp.arange(run_n.shape[0], dtype=jnp.int32)[None, :]
        pick = lambda t: jnp.sum(jnp.where(in_run, t[None, :], 0), axis=1)
        chunk_rows = pick(run_row) + (pos - pick(run_end - run_n)) * RUN_ALIGN

        per_expert = lambda e, *_: (e, 0, 0)
        y_sorted = pl.pallas_call(
            _expert_kernel,
            grid_spec=pltpu.PrefetchScalarGridSpec(
                num_scalar_prefetch=2, grid=(N_EXPERTS,),
                in_specs=[pl.BlockSpec(memory_space=pl.ANY),
                          pl.BlockSpec((1, d, D_EXPERT), per_expert),
                          pl.BlockSpec((1, d, D_EXPERT), per_expert),
                          pl.BlockSpec((1, D_EXPERT, d), per_expert)],
                out_specs=pl.BlockSpec(memory_space=pl.ANY),
                scratch_shapes=[pltpu.VMEM((2, EXPERT_ROWS, d), BF16),
                                pltpu.VMEM((d, 2 * D_EXPERT), BF16),
                                pltpu.VMEM((D_EXPERT, d), BF16),
                                pltpu.SemaphoreType.DMA((4,)),
                                pltpu.SMEM((1,), jnp.int32)]),
            out_shape=jax.ShapeDtypeStruct((n_sub * RUN_SLOTS, d), BF16),
            input_output_aliases={2: 0},
            compiler_params=_params(("arbitrary",)),
            name="moe_experts",
        )(chunk_rows, first_chunk, h_sorted, w_e_gate[l], w_e_up[l], w_e_down[l])

        xf = pl.pallas_call(
            _combine_kernel,
            grid=(n_tok // tm_p,),
            in_specs=[tok_p, lane_p, lane_p, _full((1, d)),
                      pl.BlockSpec((sorted_rows, d), row512)],
            out_specs=tok_p,
            out_shape=jax.ShapeDtypeStruct((n_tok, d), F32),
            compiler_params=_params(("parallel",)),
            name="moe_combine",
        )(x2, route, wts, final_g[None, :], y_sorted)

    return xf.reshape(bsz, seq, d)
```

```python
import functools

import jax
import jax.numpy as jnp
from jax import lax
from jax.experimental import pallas as pl
from jax.experimental.pallas import tpu as pltpu

F32 = jnp.float32
BF16 = jnp.bfloat16

D_MODEL = 1024
A_WIDTH = 512
A_HEAD = 64
D_DECAY_LORA = 64
D_AAA_LORA = 64
D_GATE_LORA = 160
B_WIDTH = 512
B_GROUPS = 4
B_GROUP_CH = 128
GMLP_BLOCK = 128
N_GROUPS = 4
EXPERTS_PER_GROUP = 8
N_EXPERTS = 32
D_EXPERT = 256
NORM_EPS = 1e-6
LN_EPS = 1e-5
LNX_EPS = 64e-5

LANES = 128
LORA_PAD = 512
XW_OFF, XA_OFF, XG_OFF = 0, 128, 256
A_PROJ = 3 * A_WIDTH + LORA_PAD
WKV_CHUNK = 64
HEAD_PAIR = 2 * A_HEAD
WKV_SEQS = 4
WKV_TOKENS = 128
WKV_PREP_GROUP = 4
WKV_NORM_GROUP = 2
MOE_SUB = 256
RUN_ALIGN = 16
RUN_SLOTS = 2 * MOE_SUB + N_EXPERTS * RUN_ALIGN
EXPERT_ROWS = 2048
EXPERT_TILE = 512
DENSE_SPLIT = 256
CHUNK_UNROLL = 4
VMEM_LIMIT = 48 * 1024 * 1024


def _rms(x, g):
    return x * lax.rsqrt(jnp.mean(x * x, axis=-1, keepdims=True) + NORM_EPS) * g


def _dot(a, b):
    return jnp.dot(a, b, preferred_element_type=F32)


def _dot_nt(a, b):
    return lax.dot_general(a, b, (((1,), (1,)), ((), ())), preferred_element_type=F32)


def _split2(x):
    hi = x.astype(BF16)
    lo = (x - hi.astype(F32)).astype(BF16)
    return hi, lo


def _split3(x):
    hi = x.astype(BF16)
    r1 = x - hi.astype(F32)
    mid = r1.astype(BF16)
    lo = (r1 - mid.astype(F32)).astype(BF16)
    return hi, mid, lo


def _pair_head_sums(xs, ones_pair):
    n_tiles = A_WIDTH // HEAD_PAIR
    rows = xs[0].shape[0]
    tiles = [x[:, t * HEAD_PAIR:(t + 1) * HEAD_PAIR].astype(BF16) for x in xs for t in range(n_tiles)]
    sums = _dot(jnp.concatenate(tiles, axis=0), ones_pair)
    return [jnp.concatenate([sums[(i * n_tiles + t) * rows:(i * n_tiles + t + 1) * rows]
                             for t in range(n_tiles)], axis=1) for i in range(len(xs))]


def _w_in_kernel(w_ref, wa_out, wb_out, wg_out):
    s_rkv = 3 * A_WIDTH
    s_w = s_rkv + D_DECAY_LORA
    s_a = s_w + D_AAA_LORA
    a_cols = s_a + D_GATE_LORA
    b_end = a_cols + 2 * B_WIDTH
    wa_out[...] = jnp.zeros_like(wa_out)
    wa_out[:s_w, :] = w_ref[:s_w, :].astype(BF16)
    wa_out[s_rkv + XA_OFF:s_rkv + XA_OFF + D_AAA_LORA, :] = w_ref[s_w:s_a, :].astype(BF16)
    wa_out[s_rkv + XG_OFF:s_rkv + XG_OFF + D_GATE_LORA, :] = w_ref[s_a:a_cols, :].astype(BF16)
    wb_out[...] = w_ref[a_cols:b_end, :].astype(BF16)
    wg_out[...] = w_ref[b_end:, :].astype(BF16)


def _inproj_kernel(tiles_per_seq, x_ref, g1_ref, wa_ref, mu_ref, w0_ref, w2_ref, a0_ref, a2_ref,
                   g2_ref, kk_ref, ka_ref, ones_ref, wb_ref, wg_ref, bg_ref, lng_ref, lnb_ref,
                   ws_ref, bs_ref, wo_ref,
                   r_out, k_out, v_out, a_out, b_out, ld_out, g_out, ybg_out, ga_out, tail_ref):
    i = pl.program_id(0)
    tm = x_ref.shape[0]
    parts = [slice(j * DENSE_SPLIT, (j + 1) * DENSE_SPLIT) for j in range(tm // DENSE_SPLIT)]
    g1 = g1_ref[...]
    h = [_rms(x_ref[p, :], g1).astype(BF16) for p in parts]
    proj = [_dot_nt(hh, wa_ref[...]) for hh in h]
    pb = [_dot_nt(hh, wb_ref[...]) for hh in h]

    prev = [jnp.where(i % tiles_per_seq == 0, 0.0, tail_ref[7:8, :])]
    prev += [p[DENSE_SPLIT - 1:, :] for p in proj[:-1]]
    tail_ref[...] = proj[-1][DENSE_SPLIT - 8:, :]
    row = lax.broadcasted_iota(jnp.int32, proj[0].shape, 0)
    pm = []
    for p, pv in zip(proj, prev):
        shifted = jnp.where(row == 0, pv, pltpu.roll(p, 1, axis=0))
        pm.append(p + mu_ref[...] * (shifted - p))
    gates = [jax.nn.sigmoid(_dot_nt(hh, wg_ref[...]) + bg_ref[...]) for hh in h]

    tri = (lax.broadcasted_iota(jnp.int32, (GMLP_BLOCK, GMLP_BLOCK), 0)
           >= lax.broadcasted_iota(jnp.int32, (GMLP_BLOCK, GMLP_BLOCK), 1))
    ws = [jnp.where(tri, ws_ref[grp], 0.0).astype(BF16) for grp in range(B_GROUPS)]
    bs = bs_ref[...]
    us, vns = [], []
    for x in pb:
        z = 0.5 * x * (1.0 + lax.erf(x * (2.0 ** -0.5)))
        us.append(z[:, :B_WIDTH])
        v = z[:, B_WIDTH:]
        mean = jnp.mean(v, axis=-1, keepdims=True)
        vc = v - mean
        var = jnp.mean(vc * vc, axis=-1, keepdims=True)
        vns.append((vc * lax.rsqrt(var + LN_EPS) * lng_ref[...] + lnb_ref[...]).astype(BF16))

    lora = [x[:, 3 * A_WIDTH:] for x in pm]
    decay = [_dot(jnp.tanh(x[:, XW_OFF:XW_OFF + LANES]).astype(BF16), w2_ref[...]) for x in lora]
    rate = [_dot(x[:, XA_OFF:XA_OFF + LANES].astype(BF16), a2_ref[...]) for x in lora]
    gate = [_dot(jax.nn.sigmoid(x[:, XG_OFF:]).astype(BF16), g2_ref[...]) for x in lora]
    kks = [x[:, A_WIDTH:2 * A_WIDTH] * kk_ref[...] for x in pm]
    sq = [_dot((kk * kk).astype(BF16), ones_ref[...]) for kk in kks]

    svs = []
    for vn in vns:
        rows = []
        for blk in range(DENSE_SPLIT // GMLP_BLOCK):
            cols = [_dot(ws[grp], vn[blk * GMLP_BLOCK:(blk + 1) * GMLP_BLOCK,
                                     grp * B_GROUP_CH:(grp + 1) * B_GROUP_CH])
                    for grp in range(B_GROUPS)]
            rows.append(jnp.concatenate(cols, axis=1) + bs)
        svs.append(jnp.concatenate(rows, axis=0))
    yb = [_dot((u * sv).astype(BF16), wo_ref[...]) for u, sv in zip(us, svs)]

    for j, p in enumerate(parts):
        k = pm[j][:, A_WIDTH:2 * A_WIDTH]
        z = -(w0_ref[...] + decay[j])
        softplus = jnp.maximum(z, 0.0) + jnp.log(1.0 + jnp.exp(-jnp.abs(z)))
        w = -softplus - 0.5
        a_lr = jax.nn.sigmoid(a0_ref[...] + rate[j])
        kk = kks[j] / jnp.maximum(jnp.sqrt(sq[j]), 1e-12)
        r_out[p, :] = pm[j][:, 0:A_WIDTH]
        k_out[p, :] = k * (1.0 + (a_lr - 1.0) * ka_ref[...])
        v_out[p, :] = pm[j][:, 2 * A_WIDTH:3 * A_WIDTH]
        a_out[p, :] = -kk
        b_out[p, :] = kk * a_lr
        ld_out[p, :] = -jnp.exp(w)
        g_out[p, :] = gate[j]
        ga_out[p, :] = gates[j][:, :D_MODEL].astype(BF16)
        ybg_out[p, :] = (gates[j][:, D_MODEL:] * yb[j]).astype(BF16)


def _wkv_kernel(r_ref, k_ref, v_ref, a_ref, b_ref, ld_ref, g_ref, rk_ref, lng_ref, lnb_ref,
                ones_ref, o_ref, st_ref, ta_s, tl_s, arb_s, ark_s, rt_s, bkt_s, dcol_s):
    C = WKV_CHUNK
    bb, tb, _ = r_ref.shape
    n_chunks = tb // C
    n_pairs = A_WIDTH // HEAD_PAIR

    @pl.when(pl.program_id(1) == 0)
    def _():
        st_ref[...] = jnp.zeros_like(st_ref)

    row = lax.broadcasted_iota(jnp.int32, (C, HEAD_PAIR), 0)
    src = lax.broadcasted_iota(jnp.int32, (C, HEAD_PAIR), 1) & (C - 1)
    incl = src <= row
    strict = src < row
    eye_pair = jnp.where(src == row, 1.0, 0.0)
    bd_mask = ((lax.broadcasted_iota(jnp.int32, (HEAD_PAIR, HEAD_PAIR), 0) >= A_HEAD)
               == (lax.broadcasted_iota(jnp.int32, (HEAD_PAIR, HEAD_PAIR), 1) >= A_HEAD))
    tri_c = (lax.broadcasted_iota(jnp.int32, (C, C), 0)
             >= lax.broadcasted_iota(jnp.int32, (C, C), 1)).astype(BF16)
    ones_pair = ones_ref[...]
    pair_cols = [slice(p * HEAD_PAIR, (p + 1) * HEAD_PAIR) for p in range(n_pairs)]

    def bd(x):
        xb = x.astype(BF16)
        return jnp.where(bd_mask, jnp.concatenate([xb, xb], axis=0), jnp.zeros((), BF16))

    def prep(it, carry):
        where, lhs, rhs, ats = [], [], [], []
        for j in range(WKV_PREP_GROUP):
            flat = it * WKV_PREP_GROUP + j
            b = flat // n_chunks
            c = flat % n_chunks
            sl = pl.ds(pl.multiple_of(c * C, C), C)
            ld = ld_ref[b, sl, :]
            hi, mid, lo = _split3(ld)
            cl = _dot(tri_c, hi) + _dot(tri_c, mid) + _dot(tri_c, lo)
            cl_end = cl[C - 1:C, :]
            k = k_ref[b, sl, :]
            bv = b_ref[b, sl, :]
            d_inv = jnp.exp(-cl)
            d_tail = jnp.exp(cl_end - cl)
            rt = (r_ref[b, sl, :] * jnp.exp(cl)).astype(BF16)
            rt_s[b, sl, :] = rt
            kt = k * d_inv
            bt = bv * d_inv
            at = a_ref[b, sl, :] * jnp.exp(cl - ld)
            kd = k * d_tail
            bdk = bv * d_tail
            d_end = jnp.exp(cl_end)
            for p, cs in enumerate(pair_cols):
                bkt_s[b, c, p] = jnp.concatenate([bdk[:, cs], kd[:, cs]], axis=0).T.astype(BF16)
                dcol_s[b, c, p] = jnp.broadcast_to(d_end[:, cs], (HEAD_PAIR, HEAD_PAIR)).T
                where.append((b, sl, cs))
                ats.append(at[:, cs])
                lhs.append(jnp.concatenate([rt[:, cs], at[:, cs].astype(BF16)], axis=0))
                rhs.append(jnp.concatenate([bd(kt[:, cs]), bd(bt[:, cs])], axis=0))
        n = len(where)
        amat = [_dot_nt(lhs[i], rhs[i]) for i in range(n)]
        l_ak, l_ab = [], []
        for i, (b, sl, cs) in enumerate(where):
            ark_s[b, sl, cs] = jnp.where(incl, amat[i][:C, :HEAD_PAIR], 0.0).astype(BF16)
            arb_s[b, sl, cs] = jnp.where(incl, amat[i][:C, HEAD_PAIR:], 0.0).astype(BF16)
            l_ak.append(jnp.where(strict, amat[i][C:, :HEAD_PAIR], 0.0))
            l_ab.append(jnp.where(strict, amat[i][C:, HEAD_PAIR:], 0.0))

        t_mat = [eye_pair + l for l in l_ab]
        q = [_dot(l.astype(BF16), bd(l)) for l in l_ab]
        n_sq = 1
        while 2 * n_sq < C // 2:
            both = [_dot(q[i].astype(BF16), jnp.concatenate([bd(q[i]), bd(t_mat[i])], axis=1))
                    for i in range(n)]
            q = [x[:, :HEAD_PAIR] for x in both]
            t_mat = [t_mat[i] + both[i][:, HEAD_PAIR:] for i in range(n)]
            n_sq *= 2
        t_mat = [t_mat[i] + _dot(q[i].astype(BF16), bd(t_mat[i])) for i in range(n)]
        tal = [_dot(t_mat[i].astype(BF16), jnp.concatenate([bd(ats[i]), bd(l_ak[i])], axis=1))
               for i in range(n)]
        for i, (b, sl, cs) in enumerate(where):
            ta_s[b, sl, cs] = tal[i][:, :HEAD_PAIR].astype(BF16)
            tl_s[b, sl, cs] = tal[i][:, HEAD_PAIR:].astype(BF16)
        return carry

    lax.fori_loop(0, bb * n_chunks // WKV_PREP_GROUP, prep, 0)

    def step(c, carry):
        sl = pl.ds(pl.multiple_of(c * C, C), C)
        chains = [(b, p) for b in range(bb) for p in range(n_pairs)]
        v = [v_ref[b, sl, :] for b in range(bb)]
        st = [st_ref[b, p] for b, p in chains]
        st_b = [x.astype(BF16) for x in st]
        bd_v = [bd(v[b][:, pair_cols[p]]) for b, p in chains]
        on_st = [_dot(jnp.concatenate([ta_s[b, sl, pair_cols[p]], rt_s[b, sl, pair_cols[p]]],
                                      axis=0), st_b[i]) for i, (b, p) in enumerate(chains)]
        on_v = [_dot(jnp.concatenate([tl_s[b, sl, pair_cols[p]], ark_s[b, sl, pair_cols[p]]],
                                     axis=0), bd_v[i]) for i, (b, p) in enumerate(chains)]
        u = [on_st[i][:C] + on_v[i][:C] for i in range(len(chains))]
        for i, (b, p) in enumerate(chains):
            uv = jnp.concatenate([u[i], v[b][:, pair_cols[p]]], axis=0).astype(BF16)
            st_ref[b, p] = (dcol_s[b, c, p] * st[i]
                            + jnp.where(bd_mask, _dot(bkt_s[b, c, p], uv), 0.0))
        ys = [on_st[i][C:] + on_v[i][C:] + _dot(arb_s[b, sl, pair_cols[p]], bd(u[i]))
              for i, (b, p) in enumerate(chains)]
        for b in range(bb):
            o_ref[b, sl, :] = jnp.concatenate(ys[b * n_pairs:(b + 1) * n_pairs], axis=1)
        return carry

    lax.fori_loop(0, n_chunks, step, 0)

    def finish(it, carry):
        items = [(b, pl.ds(pl.multiple_of((it * WKV_NORM_GROUP + j) * C, C), C))
                 for j in range(WKV_NORM_GROUP) for b in range(bb)]
        y = [o_ref[b, sl, :] for b, sl in items]
        rkr = [r_ref[b, sl, :] * k_ref[b, sl, :] * rk_ref[...] for b, sl in items]
        sums = [_pair_head_sums([y[i], rkr[i]], ones_pair) for i in range(len(items))]
        yc = [y[i] - sums[i][0] * (1.0 / A_HEAD) for i in range(len(items))]
        var = [_pair_head_sums([x * x], ones_pair)[0] * (1.0 / A_HEAD) for x in yc]
        for i, (b, sl) in enumerate(items):
            yn = yc[i] * lax.rsqrt(var[i] + LNX_EPS) * lng_ref[...] + lnb_ref[...]
            o_ref[b, sl, :] = (yn + sums[i][1] * v_ref[b, sl, :]) * g_ref[b, sl, :]
        return carry

    lax.fori_loop(0, n_chunks // WKV_NORM_GROUP, finish, 0)


def _post_kernel(x_ref, ya_ref, ga_ref, ybg_ref, woa_ref, wout_ref, g2_ref, wr_ref,
                 wr_hi_ref, br_ref, x2_out, hl_out, route_out, wts_out, cnt_out):
    n_sub = x_ref.shape[0] // MOE_SUB
    groups = [slice(g * MOE_SUB, (g + 1) * MOE_SUB) for g in range(n_sub)]
    y_a = [_dot(ya_ref[p, :].astype(BF16), woa_ref[...]) for p in groups]
    x2 = [x_ref[p, :] + _dot((ga_ref[p, :] * y + ybg_ref[p, :]).astype(BF16), wout_ref[...])
          for p, y in zip(groups, y_a)]
    h2 = [_rms(x, g2_ref[...]) for x in x2]
    split = [_split2(h) for h in h2]
    hi_w = [_dot(hi, wr_ref[...]) for hi, _ in split]
    lo_w = [_dot(lo, wr_hi_ref[...]) for _, lo in split]
    lane = lax.broadcasted_iota(jnp.int32, (MOE_SUB, LANES), 1)
    neg = jnp.float32(-jnp.inf)
    big = jnp.int32(LANES)
    is_grp = (lane >= N_EXPERTS) & (lane < N_EXPERTS + N_GROUPS)
    tri = (lax.broadcasted_iota(jnp.int32, (MOE_SUB, MOE_SUB), 0)
           >= lax.broadcasted_iota(jnp.int32, (MOE_SUB, MOE_SUB), 1)).astype(BF16)
    picks, onehots = [], []
    for g, p in enumerate(groups):
        x2_out[p, :] = x2[g]
        logits = hi_w[g][:, :LANES] + hi_w[g][:, LANES:] + lo_w[g] + br_ref[...]
        gl = jnp.where(is_grp, logits, neg)
        gmax = jnp.max(gl, axis=-1, keepdims=True)
        g_p = 1.0 / jnp.sum(jnp.exp(gl - gmax), axis=-1, keepdims=True)
        g_idx = jnp.min(jnp.where(gl == gmax, lane, big), axis=-1, keepdims=True) - N_EXPERTS
        lo_lane = g_idx * EXPERTS_PER_GROUP
        in_grp = (lane >= lo_lane) & (lane < lo_lane + EXPERTS_PER_GROUP)
        el = jnp.where(in_grp, logits, neg)
        e1 = jnp.max(el, axis=-1, keepdims=True)
        i1 = jnp.min(jnp.where(el == e1, lane, big), axis=-1, keepdims=True)
        el2 = jnp.where(lane == i1, neg, el)
        e2 = jnp.max(el2, axis=-1, keepdims=True)
        i2 = jnp.min(jnp.where(el2 == e2, lane, big), axis=-1, keepdims=True)
        t = jnp.exp(e2 - e1)
        wts_out[p, :] = jnp.where(lane == 0, g_p / (1.0 + t),
                                  jnp.where(lane == 1, g_p * t / (1.0 + t), 0.0))
        picks.append((i1, i2))
        onehots.append(jnp.where((lane == i1) | (lane == i2), 1.0, 0.0))

    incl = [_dot(tri, oh.astype(BF16)) for oh in onehots]
    row8 = lax.broadcasted_iota(jnp.int32, (8, LANES), 0)
    chunks = jnp.zeros((8, LANES), F32)
    for g in range(n_sub):
        chunks = jnp.where(row8 == g, jnp.ceil(incl[g][MOE_SUB - 1:, :] * (1.0 / RUN_ALIGN)),
                           chunks)
    cnt_out[...] = chunks
    upper = (lax.broadcasted_iota(jnp.int32, (LANES, LANES), 0)
             < lax.broadcasted_iota(jnp.int32, (LANES, LANES), 1)).astype(BF16)
    run_start = _dot(chunks.astype(BF16), upper) * RUN_ALIGN
    routes = []
    for g, p in enumerate(groups):
        slot_of = run_start[g:g + 1, :] + incl[g] - onehots[g]
        i1, i2 = picks[g]
        slot1 = jnp.sum(jnp.where(lane == i1, slot_of, 0.0), axis=-1, keepdims=True)
        slot2 = jnp.sum(jnp.where(lane == i2, slot_of, 0.0), axis=-1, keepdims=True)
        route = jnp.where(lane == 0, slot1, jnp.where(lane == 1, slot2, 0.0)).astype(jnp.int32)
        route_out[p, :] = route
        routes.append(route.T)

    slot_ids = lax.broadcasted_iota(jnp.int32, (RUN_SLOTS, MOE_SUB), 0)
    for g in range(n_sub):
        pick = jnp.where((slot_ids == routes[g][0:1, :]) | (slot_ids == routes[g][1:2, :]),
                         1.0, 0.0).astype(BF16)
        hl_out[g * RUN_SLOTS:(g + 1) * RUN_SLOTS, :] = _dot(pick, h2[g].astype(BF16)).astype(BF16)


def _expert_kernel(chunk_s, first_s, hl_in_ref, wg_ref, wu_ref, wd_ref, hl_ref, xbuf, wgu_s, wd_s,
                   sem, drain_s):
    del hl_in_ref
    e = pl.program_id(0)
    pass_chunks = EXPERT_ROWS // RUN_ALIGN
    mine = e % 2
    other = 1 - mine
    gather_sem = lambda buf: sem.at[buf]
    scatter_sem = lambda buf: sem.at[2 + buf]

    def first_pass(ex):
        lo = first_s[ex]
        return lo, jnp.minimum(lo + pass_chunks, first_s[ex + 1])

    def for_chunks(buf, c_lo, c_hi, fn):
        def one(c):
            fn(hl_ref.at[pl.ds(pl.multiple_of(chunk_s[c], RUN_ALIGN), RUN_ALIGN)],
               xbuf.at[buf, pl.ds(pl.multiple_of((c - c_lo) * RUN_ALIGN, RUN_ALIGN), RUN_ALIGN)])

        def block(i, carry):
            for u in range(CHUNK_UNROLL):
                one(c_lo + i * CHUNK_UNROLL + u)
            return carry

        def tail(c, carry):
            one(c)
            return carry

        n_blocks = (c_hi - c_lo) // CHUNK_UNROLL
        lax.fori_loop(0, n_blocks, block, 0)
        lax.fori_loop(c_lo + n_blocks * CHUNK_UNROLL, c_hi, tail, 0)

    def gather(buf, c_lo, c_hi):
        for_chunks(buf, c_lo, c_hi, lambda hbm, vm: pltpu.make_async_copy(
            hbm, vm, gather_sem(buf)).start())

    def scatter(buf, c_lo, c_hi):
        for_chunks(buf, c_lo, c_hi, lambda hbm, vm: pltpu.make_async_copy(
            vm, hbm, scatter_sem(buf)).start())

    def wait_chunks(which_sem, n):
        def wait_rows(rows):
            def body(c, carry):
                pltpu.make_async_copy(hl_ref.at[pl.ds(0, rows)], xbuf.at[0, pl.ds(0, rows)],
                                      which_sem).wait()
                return carry
            return body

        lax.fori_loop(0, n // CHUNK_UNROLL, wait_rows(CHUNK_UNROLL * RUN_ALIGN), 0)
        lax.fori_loop(0, n % CHUNK_UNROLL, wait_rows(RUN_ALIGN), 0)

    def evaluate(buf, n_chunks):
        def tile(k, carry):
            sl = pl.ds(pl.multiple_of(k * EXPERT_TILE, EXPERT_TILE), EXPERT_TILE)
            gu = _dot(xbuf[buf, sl, :], wgu_s[...])
            gate = gu[:, :D_EXPERT]
            act = gate * jax.nn.sigmoid(gate) * gu[:, D_EXPERT:]
            xbuf[buf, sl, :] = _dot(act.astype(BF16), wd_s[...]).astype(BF16)
            return carry

        lax.fori_loop(0, (n_chunks * RUN_ALIGN + EXPERT_TILE - 1) // EXPERT_TILE, tile, 0)

    @pl.when(e == 0)
    def _():
        xbuf[...] = jnp.zeros_like(xbuf)
        gather(0, *first_pass(0))
        drain_s[0] = 0

    wgu_s[:, :D_EXPERT] = wg_ref[0].astype(BF16)
    wgu_s[:, D_EXPERT:] = wu_ref[0].astype(BF16)
    wd_s[...] = wd_ref[0].astype(BF16)

    c_lo, c_hi = first_pass(e)
    wait_chunks(gather_sem(mine), c_hi - c_lo)
    evaluate(mine, c_hi - c_lo)
    wait_chunks(scatter_sem(other), drain_s[0])
    scatter(mine, c_lo, c_hi)

    def later_pass(lo):
        hi = jnp.minimum(lo + pass_chunks, first_s[e + 1])
        gather(other, lo, hi)
        wait_chunks(gather_sem(other), hi - lo)
        evaluate(other, hi - lo)
        scatter(other, lo, hi)
        wait_chunks(scatter_sem(other), hi - lo)
        return hi

    lax.while_loop(lambda c: c < first_s[e + 1], later_pass, c_hi)
    drain_s[0] = c_hi - c_lo

    @pl.when(e + 1 < pl.num_programs(0))
    def _():
        gather(other, *first_pass(e + 1))

    @pl.when(e + 1 == pl.num_programs(0))
    def _():
        wait_chunks(scatter_sem(mine), c_hi - c_lo)


def _combine_kernel(x2_ref, route_ref, wts_ref, gf_ref, yl_ref, o_ref):
    slots = lax.broadcasted_iota(jnp.int32, (MOE_SUB, RUN_SLOTS), 1)
    for g in range(x2_ref.shape[0] // MOE_SUB):
        p = slice(g * MOE_SUB, (g + 1) * MOE_SUB)
        route = route_ref[p, :]
        wts = wts_ref[p, :]
        pick = (jnp.where(slots == route[:, 0:1], wts[:, 0:1], 0.0)
                + jnp.where(slots == route[:, 1:2], wts[:, 1:2], 0.0))
        y = _dot(pick.astype(BF16), yl_ref[g * RUN_SLOTS:(g + 1) * RUN_SLOTS, :])
        o_ref[p, :] = _rms(x2_ref[p, :] + y, gf_ref[...])


def _full(shape):
    return pl.BlockSpec(shape, lambda *_: (0,) * len(shape))


def _params(sem):
    return pltpu.CompilerParams(dimension_semantics=sem, vmem_limit_bytes=VMEM_LIMIT)


def _mixer_a_layout(t):
    s_rkv = 3 * A_WIDTH
    s_w = s_rkv + D_DECAY_LORA
    s_a = s_w + D_AAA_LORA
    gap = lambda n: jnp.zeros((t.shape[0], n), t.dtype)
    return jnp.concatenate(
        [t[:, :s_w], gap(XA_OFF - D_DECAY_LORA), t[:, s_w:s_a], gap(XG_OFF - XA_OFF - D_AAA_LORA),
         t[:, s_a:], gap(LORA_PAD - XG_OFF - D_GATE_LORA)], axis=1)


def _place(cols, parts):
    out = jnp.zeros((parts[0][1].shape[0], cols), parts[0][1].dtype)
    for off, arr in parts:
        out = lax.dynamic_update_slice(out, arr, (0, off))
    return out


def kernel(x, norm1_g, w_in, b_gate, tmix_mu, w0, w2, a0, a2, g2, k_k, k_a, r_k, lnx_g, lnx_b,
           w_oA, lnv_g, lnv_b, w_s, b_s, w_oB, w_out, norm2_g, w_rg, b_rg, w_re, b_re,
           w_e_gate, w_e_up, w_e_down, final_g):
    bsz, seq, d = x.shape
    n_tok = bsz * seq
    depth = norm1_g.shape[0]
    assert depth == 1, "the moe kernel fuses the final norm, so it must be the last layer"
    assert bsz % WKV_SEQS == 0 and seq % WKV_TOKENS == 0
    xf = x.reshape(n_tok, d)

    s_rkv = 3 * A_WIDTH
    s_w = s_rkv + D_DECAY_LORA
    s_a = s_w + D_AAA_LORA
    a_cols = s_a + D_GATE_LORA
    b_cols = 2 * B_WIDTH

    ones_bd = (jnp.arange(A_WIDTH)[:, None] // A_HEAD
               == jnp.arange(A_WIDTH)[None, :] // A_HEAD).astype(BF16)

    tm_a = 512
    tm_p = 512

    for l in range(depth):
        wl = w_in[l]
        rows_w = 256
        w_a, w_b, w_g = pl.pallas_call(
            _w_in_kernel,
            grid=(d // rows_w,),
            in_specs=[pl.BlockSpec((wl.shape[1], rows_w), lambda i: (0, i))],
            out_specs=[pl.BlockSpec((A_PROJ, rows_w), lambda i: (0, i)),
                       pl.BlockSpec((b_cols, rows_w), lambda i: (0, i)),
                       pl.BlockSpec((2 * d, rows_w), lambda i: (0, i))],
            out_shape=[jax.ShapeDtypeStruct((A_PROJ, d), BF16),
                       jax.ShapeDtypeStruct((b_cols, d), BF16),
                       jax.ShapeDtypeStruct((2 * d, d), BF16)],
            compiler_params=_params(("parallel",)),
            name="w_in_layout",
        )(wl.T)
        mu_a = _mixer_a_layout(tmix_mu[l][None, :])
        w2p = jnp.pad(w2[l], ((0, LANES - D_DECAY_LORA), (0, 0))).astype(BF16)
        a2p = jnp.pad(a2[l], ((0, LANES - D_AAA_LORA), (0, 0))).astype(BF16)
        g2p = jnp.pad(g2[l], ((0, LORA_PAD - XG_OFF - D_GATE_LORA), (0, 0))).astype(BF16)
        g1 = norm1_g[l][None, :]

        row512 = lambda i: (i, 0)
        tok_a = pl.BlockSpec((tm_a, A_WIDTH), row512)
        tok_d = pl.BlockSpec((tm_a, d), row512)
        vec_a = _full((1, A_WIDTH))
        bs_full = jnp.repeat(b_s[l].T, B_GROUP_CH, axis=1)
        outs = pl.pallas_call(
            functools.partial(_inproj_kernel, seq // tm_a),
            grid=(n_tok // tm_a,),
            in_specs=[tok_d,
                      _full((1, d)), _full((A_PROJ, d)), _full((1, A_PROJ)), vec_a,
                      _full((LANES, A_WIDTH)), vec_a, _full((LANES, A_WIDTH)),
                      _full((LORA_PAD - XG_OFF, A_WIDTH)), vec_a, vec_a,
                      _full((A_WIDTH, A_WIDTH)),
                      _full((b_cols, d)), _full((2 * d, d)), _full((1, 2 * d)),
                      _full((1, B_WIDTH)), _full((1, B_WIDTH)),
                      _full((B_GROUPS, GMLP_BLOCK, GMLP_BLOCK)), _full((GMLP_BLOCK, B_WIDTH)),
                      _full((B_WIDTH, d))],
            out_specs=[tok_a] * 7 + [tok_d, tok_d],
            out_shape=[jax.ShapeDtypeStruct((n_tok, A_WIDTH), F32)] * 7
            + [jax.ShapeDtypeStruct((n_tok, d), BF16)] * 2,
            scratch_shapes=[pltpu.VMEM((8, A_PROJ), F32)],
            compiler_params=_params(("arbitrary",)),
            name="inproj",
        )(xf, g1, w_a, mu_a, w0[l][None, :], w2p, a0[l][None, :], a2p, g2p,
          k_k[l][None, :], k_a[l][None, :], ones_bd, w_b, w_g, b_gate[l][None, :],
          lnv_g[l][None, :], lnv_b[l][None, :], w_s[l], bs_full, w_oB[l].astype(BF16))
        r_, k_, v_, a_, b_, ld_, gg_, ybg, ga = outs

        n_pairs = A_WIDTH // HEAD_PAIR
        n_chunks = WKV_TOKENS // WKV_CHUNK
        tok_w = pl.BlockSpec((WKV_SEQS, WKV_TOKENS, A_WIDTH), lambda bi, ti: (bi, ti, 0))
        vec_w = _full((1, A_WIDTH))
        seq_major = lambda t: t.reshape(bsz, seq, A_WIDTH)
        tok_scratch = pltpu.VMEM((WKV_SEQS, WKV_TOKENS, A_WIDTH), BF16)
        ya_in = pl.pallas_call(
            _wkv_kernel,
            grid=(bsz // WKV_SEQS, seq // WKV_TOKENS),
            in_specs=[tok_w] * 7 + [vec_w, vec_w, vec_w, _full((HEAD_PAIR, HEAD_PAIR))],
            out_specs=tok_w,
            out_shape=jax.ShapeDtypeStruct((bsz, seq, A_WIDTH), F32),
            scratch_shapes=[pltpu.VMEM((WKV_SEQS, n_pairs, HEAD_PAIR, HEAD_PAIR), F32)]
            + [tok_scratch] * 5
            + [pltpu.VMEM((WKV_SEQS, n_chunks, n_pairs, HEAD_PAIR, HEAD_PAIR), BF16),
               pltpu.VMEM((WKV_SEQS, n_chunks, n_pairs, HEAD_PAIR, HEAD_PAIR), F32)],
            compiler_params=_params(("parallel", "arbitrary")),
            name="wkv",
        )(*[seq_major(t) for t in (r_, k_, v_, a_, b_, ld_, gg_)], r_k[l].reshape(1, A_WIDTH),
          lnx_g[l][None, :], lnx_b[l][None, :],
          ones_bd[:HEAD_PAIR, :HEAD_PAIR]).reshape(n_tok, A_WIDTH)

        w_r = _place(LANES, [(0, jnp.transpose(w_re[l], (1, 0, 2)).reshape(d, N_EXPERTS)),
                             (N_EXPERTS, w_rg[l])])
        b_r = _place(LANES, [(0, b_re[l].reshape(1, N_EXPERTS)), (N_EXPERTS, b_rg[l][None, :])])
        wr_hi = w_r.astype(BF16)
        wr_lo = (w_r - wr_hi.astype(F32)).astype(BF16)
        tok_p = pl.BlockSpec((tm_p, d), row512)
        lane_p = pl.BlockSpec((tm_p, LANES), row512)
        n_sub = n_tok // MOE_SUB
        sorted_rows = tm_p // MOE_SUB * RUN_SLOTS
        x2, h_sorted, route, wts, cnt = pl.pallas_call(
            _post_kernel,
            grid=(n_tok // tm_p,),
            in_specs=[tok_p, pl.BlockSpec((tm_p, A_WIDTH), row512), tok_p, tok_p,
                      _full((A_WIDTH, d)), _full((d, d)), _full((1, d)), _full((d, 2 * LANES)),
                      _full((d, LANES)), _full((1, LANES))],
            out_specs=[tok_p, pl.BlockSpec((sorted_rows, d), row512), lane_p, lane_p,
                       pl.BlockSpec((8, LANES), row512)],
            out_shape=[jax.ShapeDtypeStruct((n_tok, d), F32),
                       jax.ShapeDtypeStruct((n_sub * RUN_SLOTS, d), BF16),
                       jax.ShapeDtypeStruct((n_tok, LANES), jnp.int32),
                       jax.ShapeDtypeStruct((n_tok, LANES), F32),
                       jax.ShapeDtypeStruct((n_tok // tm_p * 8, LANES), F32)],
            compiler_params=_params(("parallel",)),
            name="post",
        )(xf, ya_in, ga, ybg, w_oA[l].astype(BF16), w_out[l].astype(BF16),
          norm2_g[l][None, :], jnp.concatenate([wr_hi, wr_lo], axis=1), wr_hi, b_r)

        run_chunks = cnt.reshape(n_tok // tm_p, 8, LANES)[:, :tm_p // MOE_SUB, :N_EXPERTS]
        run_chunks = run_chunks.reshape(n_sub, N_EXPERTS).astype(jnp.int32)
        run_start = (jnp.cumsum(run_chunks, axis=1) - run_chunks) * RUN_ALIGN
        run_row = (jnp.arange(n_sub, dtype=jnp.int32)[:, None] * RUN_SLOTS + run_start).T.reshape(-1)
        run_n = run_chunks.T.reshape(-1)
        run_end = jnp.cumsum(run_n)
        first_chunk = jnp.concatenate([jnp.zeros((1,), jnp.int32), run_end[n_sub - 1::n_sub]])
        pos = jnp.arange(n_sub * RUN_SLOTS // RUN_ALIGN, dtype=jnp.int32)
        run_of = jnp.sum(pos[:, None] >= run_end[None, :], axis=1)
        in_run = run_of[:, None] == jnp.arange(run_n.shape[0], dtype=jnp.int32)[None, :]
        pick = lambda t: jnp.sum(jnp.where(in_run, t[None, :], 0), axis=1)
        chunk_rows = pick(run_row) + (pos - pick(run_end - run_n)) * RUN_ALIGN

        per_expert = lambda e, *_: (e, 0, 0)
        y_sorted = pl.pallas_call(
            _expert_kernel,
            grid_spec=pltpu.PrefetchScalarGridSpec(
                num_scalar_prefetch=2, grid=(N_EXPERTS,),
                in_specs=[pl.BlockSpec(memory_space=pl.ANY),
                          pl.BlockSpec((1, d, D_EXPERT), per_expert),
                          pl.BlockSpec((1, d, D_EXPERT), per_expert),
                          pl.BlockSpec((1, D_EXPERT, d), per_expert)],
                out_specs=pl.BlockSpec(memory_space=pl.ANY),
                scratch_shapes=[pltpu.VMEM((2, EXPERT_ROWS, d), BF16),
                                pltpu.VMEM((d, 2 * D_EXPERT), BF16),
                                pltpu.VMEM((D_EXPERT, d), BF16),
                                pltpu.SemaphoreType.DMA((4,)),
                                pltpu.SMEM((1,), jnp.int32)]),
            out_shape=jax.ShapeDtypeStruct((n_sub * RUN_SLOTS, d), BF16),
            input_output_aliases={2: 0},
            compiler_params=_params(("arbitrary",)),
            name="moe_experts",
        )(chunk_rows, first_chunk, h_sorted, w_e_gate[l], w_e_up[l], w_e_down[l])

        xf = pl.pallas_call(
            _combine_kernel,
            grid=(n_tok // tm_p,),
            in_specs=[tok_p, lane_p, lane_p, _full((1, d)),
                      pl.BlockSpec((sorted_rows, d), row512)],
            out_specs=tok_p,
            out_shape=jax.ShapeDtypeStruct((n_tok, d), F32),
            compiler_params=_params(("parallel",)),
            name="moe_combine",
        )(x2, route, wts, final_g[None, :], y_sorted)

    return xf.reshape(bsz, seq, d)
```

```python
import functools

import jax
import jax.numpy as jnp
from jax import lax
from jax.experimental import pallas as pl
from jax.experimental.pallas import tpu as pltpu

F32 = jnp.float32
BF16 = jnp.bfloat16

D_MODEL = 1024
A_WIDTH = 512
A_HEAD = 64
D_DECAY_LORA = 64
D_AAA_LORA = 64
D_GATE_LORA = 160
B_WIDTH = 512
B_GROUPS = 4
B_GROUP_CH = 128
GMLP_BLOCK = 128
N_GROUPS = 4
EXPERTS_PER_GROUP = 8
N_EXPERTS = 32
D_EXPERT = 256
NORM_EPS = 1e-6
LN_EPS = 1e-5
LNX_EPS = 64e-5

LANES = 128
LORA_PAD = 512
XW_OFF, XA_OFF, XG_OFF = 0, 128, 256
A_PROJ = 3 * A_WIDTH + LORA_PAD
WKV_CHUNK = 64
HEAD_PAIR = 2 * A_HEAD
WKV_SEQS = 4
WKV_TOKENS = 128
WKV_PREP_GROUP = 4
WKV_NORM_GROUP = 2
MOE_SUB = 256
RUN_ALIGN = 16
RUN_SLOTS = 2 * MOE_SUB + N_EXPERTS * RUN_ALIGN
EXPERT_ROWS = 2048
EXPERT_TILE = 512
DENSE_SPLIT = 256
CHUNK_UNROLL = 4
VMEM_LIMIT = 48 * 1024 * 1024


def _rms(x, g):
    return x * lax.rsqrt(jnp.mean(x * x, axis=-1, keepdims=True) + NORM_EPS) * g


def _dot(a, b):
    return jnp.dot(a, b, preferred_element_type=F32)


def _dot_nt(a, b):
    return lax.dot_general(a, b, (((1,), (1,)), ((), ())), preferred_element_type=F32)


def _split2(x):
    hi = x.astype(BF16)
    lo = (x - hi.astype(F32)).astype(BF16)
    return hi, lo


def _split3(x):
    hi = x.astype(BF16)
    r1 = x - hi.astype(F32)
    mid = r1.astype(BF16)
    lo = (r1 - mid.astype(F32)).astype(BF16)
    return hi, mid, lo


def _pair_head_sums(xs, ones_pair):
    n_tiles = A_WIDTH // HEAD_PAIR
    rows = xs[0].shape[0]
    tiles = [x[:, t * HEAD_PAIR:(t + 1) * HEAD_PAIR].astype(BF16) for x in xs for t in range(n_tiles)]
    sums = _dot(jnp.concatenate(tiles, axis=0), ones_pair)
    return [jnp.concatenate([sums[(i * n_tiles + t) * rows:(i * n_tiles + t + 1) * rows]
                             for t in range(n_tiles)], axis=1) for i in range(len(xs))]


def _w_in_kernel(w_ref, wa_out, wb_out, wg_out):
    s_rkv = 3 * A_WIDTH
    s_w = s_rkv + D_DECAY_LORA
    s_a = s_w + D_AAA_LORA
    a_cols = s_a + D_GATE_LORA
    b_end = a_cols + 2 * B_WIDTH
    wa_out[...] = jnp.zeros_like(wa_out)
    wa_out[:s_w, :] = w_ref[:s_w, :].astype(BF16)
    wa_out[s_rkv + XA_OFF:s_rkv + XA_OFF + D_AAA_LORA, :] = w_ref[s_w:s_a, :].astype(BF16)
    wa_out[s_rkv + XG_OFF:s_rkv + XG_OFF + D_GATE_LORA, :] = w_ref[s_a:a_cols, :].astype(BF16)
    wb_out[...] = w_ref[a_cols:b_end, :].astype(BF16)
    wg_out[...] = w_ref[b_end:, :].astype(BF16)


def _inproj_kernel(tiles_per_seq, x_ref, g1_ref, wa_ref, mu_ref, w0_ref, w2_ref, a0_ref, a2_ref,
                   g2_ref, kk_ref, ka_ref, ones_ref, wb_ref, wg_ref, bg_ref, lng_ref, lnb_ref,
                   ws_ref, bs_ref, wo_ref,
                   r_out, k_out, v_out, a_out, b_out, ld_out, g_out, ybg_out, ga_out, tail_ref):
    i = pl.program_id(0)
    tm = x_ref.shape[0]
    parts = [slice(j * DENSE_SPLIT, (j + 1) * DENSE_SPLIT) for j in range(tm // DENSE_SPLIT)]
    g1 = g1_ref[...]
    h = [_rms(x_ref[p, :], g1).astype(BF16) for p in parts]
    proj = [_dot_nt(hh, wa_ref[...]) for hh in h]
    pb = [_dot_nt(hh, wb_ref[...]) for hh in h]

    prev = [jnp.where(i % tiles_per_seq == 0, 0.0, tail_ref[7:8, :])]
    prev += [p[DENSE_SPLIT - 1:, :] for p in proj[:-1]]
    tail_ref[...] = proj[-1][DENSE_SPLIT - 8:, :]
    row = lax.broadcasted_iota(jnp.int32, proj[0].shape, 0)
    pm = []
    for p, pv in zip(proj, prev):
        shifted = jnp.where(row == 0, pv, pltpu.roll(p, 1, axis=0))
        pm.append(p + mu_ref[...] * (shifted - p))
    gates = [jax.nn.sigmoid(_dot_nt(hh, wg_ref[...]) + bg_ref[...]) for hh in h]

    tri = (lax.broadcasted_iota(jnp.int32, (GMLP_BLOCK, GMLP_BLOCK), 0)
           >= lax.broadcasted_iota(jnp.int32, (GMLP_BLOCK, GMLP_BLOCK), 1))
    ws = [jnp.where(tri, ws_ref[grp], 0.0).astype(BF16) for grp in range(B_GROUPS)]
    bs = bs_ref[...]
    us, vns = [], []
    for x in pb:
        z = 0.5 * x * (1.0 + lax.erf(x * (2.0 ** -0.5)))
        us.append(z[:, :B_WIDTH])
        v = z[:, B_WIDTH:]
        mean = jnp.mean(v, axis=-1, keepdims=True)
        vc = v - mean
        var = jnp.mean(vc * vc, axis=-1, keepdims=True)
        vns.append((vc * lax.rsqrt(var + LN_EPS) * lng_ref[...] + lnb_ref[...]).astype(BF16))

    lora = [x[:, 3 * A_WIDTH:] for x in pm]
    decay = [_dot(jnp.tanh(x[:, XW_OFF:XW_OFF + LANES]).astype(BF16), w2_ref[...]) for x in lora]
    rate = [_dot(x[:, XA_OFF:XA_OFF + LANES].astype(BF16), a2_ref[...]) for x in lora]
    gate = [_dot(jax.nn.sigmoid(x[:, XG_OFF:]).astype(BF16), g2_ref[...]) for x in lora]
    kks = [x[:, A_WIDTH:2 * A_WIDTH] * kk_ref[...] for x in pm]
    sq = [_dot((kk * kk).astype(BF16), ones_ref[...]) for kk in kks]

    svs = []
    for vn in vns:
        rows = []
        for blk in range(DENSE_SPLIT // GMLP_BLOCK):
            cols = [_dot(ws[grp], vn[blk * GMLP_BLOCK:(blk + 1) * GMLP_BLOCK,
                                     grp * B_GROUP_CH:(grp + 1) * B_GROUP_CH])
                    for grp in range(B_GROUPS)]
            rows.append(jnp.concatenate(cols, axis=1) + bs)
        svs.append(jnp.concatenate(rows, axis=0))
    yb = [_dot((u * sv).astype(BF16), wo_ref[...]) for u, sv in zip(us, svs)]

    for j, p in enumerate(parts):
        k = pm[j][:, A_WIDTH:2 * A_WIDTH]
        z = -(w0_ref[...] + decay[j])
        softplus = jnp.maximum(z, 0.0) + jnp.log(1.0 + jnp.exp(-jnp.abs(z)))
        w = -softplus - 0.5
        a_lr = jax.nn.sigmoid(a0_ref[...] + rate[j])
        kk = kks[j] / jnp.maximum(jnp.sqrt(sq[j]), 1e-12)
        r_out[p, :] = pm[j][:, 0:A_WIDTH]
        k_out[p, :] = k * (1.0 + (a_lr - 1.0) * ka_ref[...])
        v_out[p, :] = pm[j][:, 2 * A_WIDTH:3 * A_WIDTH]
        a_out[p, :] = -kk
        b_out[p, :] = kk * a_lr
        ld_out[p, :] = -jnp.exp(w)
        g_out[p, :] = gate[j]
        ga_out[p, :] = gates[j][:, :D_MODEL].astype(BF16)
        ybg_out[p, :] = (gates[j][:, D_MODEL:] * yb[j]).astype(BF16)


def _wkv_kernel(r_ref, k_ref, v_ref, a_ref, b_ref, ld_ref, g_ref, rk_ref, lng_ref, lnb_ref,
                ones_ref, o_ref, st_ref, ta_s, tl_s, arb_s, ark_s, rt_s, bkt_s, dcol_s, y_s):
    C = WKV_CHUNK
    bb, tb, _ = r_ref.shape
    n_chunks = tb // C
    n_pairs = A_WIDTH // HEAD_PAIR

    @pl.when(pl.program_id(1) == 0)
    def _():
        st_ref[...] = jnp.zeros_like(st_ref)

    row = lax.broadcasted_iota(jnp.int32, (C, HEAD_PAIR), 0)
    src = lax.broadcasted_iota(jnp.int32, (C, HEAD_PAIR), 1) & (C - 1)
    incl = src <= row
    strict = src < row
    eye_pair = jnp.where(src == row, 1.0, 0.0)
    bd_mask = ((lax.broadcasted_iota(jnp.int32, (HEAD_PAIR, HEAD_PAIR), 0) >= A_HEAD)
               == (lax.broadcasted_iota(jnp.int32, (HEAD_PAIR, HEAD_PAIR), 1) >= A_HEAD))
    tri_c = (lax.broadcasted_iota(jnp.int32, (C, C), 0)
             >= lax.broadcasted_iota(jnp.int32, (C, C), 1)).astype(BF16)
    ones_pair = ones_ref[...]
    pair_cols = [slice(p * HEAD_PAIR, (p + 1) * HEAD_PAIR) for p in range(n_pairs)]

    def bd(x):
        xb = x.astype(BF16)
        return jnp.where(bd_mask, jnp.concatenate([xb, xb], axis=0), jnp.zeros((), BF16))

    def prep(it, carry):
        where, lhs, rhs, ats = [], [], [], []
        for j in range(WKV_PREP_GROUP):
            flat = it * WKV_PREP_GROUP + j
            b = flat // n_chunks
            c = flat % n_chunks
            sl = pl.ds(pl.multiple_of(c * C, C), C)
            ld = ld_ref[b, sl, :]
            hi, mid, lo = _split3(ld)
            cl = _dot(tri_c, hi) + _dot(tri_c, mid) + _dot(tri_c, lo)
            cl_end = cl[C - 1:C, :]
            k = k_ref[b, sl, :]
            bv = b_ref[b, sl, :]
            d_inv = jnp.exp(-cl)
            d_tail = jnp.exp(cl_end - cl)
            rt = (r_ref[b, sl, :] * jnp.exp(cl)).astype(BF16)
            rt_s[b, sl, :] = rt
            kt = k * d_inv
            bt = bv * d_inv
            at = a_ref[b, sl, :] * jnp.exp(cl - ld)
            kd = k * d_tail
            bdk = bv * d_tail
            d_end = jnp.exp(cl_end)
            for p, cs in enumerate(pair_cols):
                bkt_s[b, c, p] = jnp.concatenate([bdk[:, cs], kd[:, cs]], axis=0).T.astype(BF16)
                dcol_s[b, c, p] = jnp.broadcast_to(d_end[:, cs], (HEAD_PAIR, HEAD_PAIR)).T
                where.append((b, sl, cs))
                ats.append(at[:, cs])
                lhs.append(jnp.concatenate([rt[:, cs], at[:, cs].astype(BF16)], axis=0))
                rhs.append(jnp.concatenate([bd(kt[:, cs]), bd(bt[:, cs])], axis=0))
        n = len(where)
        amat = [_dot_nt(lhs[i], rhs[i]) for i in range(n)]
        l_ak, l_ab = [], []
        for i, (b, sl, cs) in enumerate(where):
            ark_s[b, sl, cs] = jnp.where(incl, amat[i][:C, :HEAD_PAIR], 0.0).astype(BF16)
            arb_s[b, sl, cs] = jnp.where(incl, amat[i][:C, HEAD_PAIR:], 0.0).astype(BF16)
            l_ak.append(jnp.where(strict, amat[i][C:, :HEAD_PAIR], 0.0))
            l_ab.append(jnp.where(strict, amat[i][C:, HEAD_PAIR:], 0.0))

        t_mat = [eye_pair + l for l in l_ab]
        q = [_dot(l.astype(BF16), bd(l)) for l in l_ab]
        n_sq = 1
        while 2 * n_sq < C // 2:
            both = [_dot(q[i].astype(BF16), jnp.concatenate([bd(q[i]), bd(t_mat[i])], axis=1))
                    for i in range(n)]
            q = [x[:, :HEAD_PAIR] for x in both]
            t_mat = [t_mat[i] + both[i][:, HEAD_PAIR:] for i in range(n)]
            n_sq *= 2
        t_mat = [t_mat[i] + _dot(q[i].astype(BF16), bd(t_mat[i])) for i in range(n)]
        tal = [_dot(t_mat[i].astype(BF16), jnp.concatenate([bd(ats[i]), bd(l_ak[i])], axis=1))
               for i in range(n)]
        for i, (b, sl, cs) in enumerate(where):
            ta_s[b, sl, cs] = tal[i][:, :HEAD_PAIR].astype(BF16)
            tl_s[b, sl, cs] = tal[i][:, HEAD_PAIR:].astype(BF16)
        return carry

    lax.fori_loop(0, bb * n_chunks // WKV_PREP_GROUP, prep, 0)

    def step(c, carry):
        sl = pl.ds(pl.multiple_of(c * C, C), C)
        chains = [(b, p) for b in range(bb) for p in range(n_pairs)]
        v = [v_ref[b, sl, :] for b in range(bb)]
        st = [st_ref[b, p] for b, p in chains]
        st_b = [x.astype(BF16) for x in st]
        bd_v = [bd(v[b][:, pair_cols[p]]) for b, p in chains]
        on_st = [_dot(jnp.concatenate([ta_s[b, sl, pair_cols[p]], rt_s[b, sl, pair_cols[p]]],
                                      axis=0), st_b[i]) for i, (b, p) in enumerate(chains)]
        on_v = [_dot(jnp.concatenate([tl_s[b, sl, pair_cols[p]], ark_s[b, sl, pair_cols[p]]],
                                     axis=0), bd_v[i]) for i, (b, p) in enumerate(chains)]
        u = [on_st[i][:C] + on_v[i][:C] for i in range(len(chains))]
        for i, (b, p) in enumerate(chains):
            uv = jnp.concatenate([u[i], v[b][:, pair_cols[p]]], axis=0).astype(BF16)
            st_ref[b, p] = (dcol_s[b, c, p] * st[i]
                            + jnp.where(bd_mask, _dot(bkt_s[b, c, p], uv), 0.0))
        ys = [on_st[i][C:] + on_v[i][C:] + _dot(arb_s[b, sl, pair_cols[p]], bd(u[i]))
              for i, (b, p) in enumerate(chains)]
        for b in range(bb):
            y_s[b, sl, :] = jnp.concatenate(ys[b * n_pairs:(b + 1) * n_pairs], axis=1)
        return carry

    lax.fori_loop(0, n_chunks, step, 0)

    def finish(it, carry):
        items = [(b, pl.ds(pl.multiple_of((it * WKV_NORM_GROUP + j) * C, C), C))
                 for j in range(WKV_NORM_GROUP) for b in range(bb)]
        y = [y_s[b, sl, :] for b, sl in items]
        rkr = [r_ref[b, sl, :] * k_ref[b, sl, :] * rk_ref[...] for b, sl in items]
        sums = [_pair_head_sums([y[i], rkr[i]], ones_pair) for i in range(len(items))]
        yc = [y[i] - sums[i][0] * (1.0 / A_HEAD) for i in range(len(items))]
        var = [_pair_head_sums([x * x], ones_pair)[0] * (1.0 / A_HEAD) for x in yc]
        for i, (b, sl) in enumerate(items):
            yn = yc[i] * lax.rsqrt(var[i] + LNX_EPS) * lng_ref[...] + lnb_ref[...]
            out = (yn + sums[i][1] * v_ref[b, sl, :]) * g_ref[b, sl, :]
            o_ref[b, sl, :] = out.astype(o_ref.dtype)
        return carry

    lax.fori_loop(0, n_chunks // WKV_NORM_GROUP, finish, 0)


def _post_kernel(x_ref, ya_ref, ga_ref, ybg_ref, woa_ref, wout_ref, g2_ref, wr_ref,
                 wr_hi_ref, br_ref, x2_out, hl_out, route_out, wts_out, cnt_out):
    n_sub = x_ref.shape[0] // MOE_SUB
    groups = [slice(g * MOE_SUB, (g + 1) * MOE_SUB) for g in range(n_sub)]
    y_a = [_dot(ya_ref[p, :].astype(BF16), woa_ref[...]) for p in groups]
    x2 = [x_ref[p, :] + _dot((ga_ref[p, :] * y + ybg_ref[p, :]).astype(BF16), wout_ref[...])
          for p, y in zip(groups, y_a)]
    h2 = [_rms(x, g2_ref[...]) for x in x2]
    split = [_split2(h) for h in h2]
    hi_w = [_dot(hi, wr_ref[...]) for hi, _ in split]
    lo_w = [_dot(lo, wr_hi_ref[...]) for _, lo in split]
    lane = lax.broadcasted_iota(jnp.int32, (MOE_SUB, LANES), 1)
    neg = jnp.float32(-jnp.inf)
    big = jnp.int32(LANES)
    is_grp = (lane >= N_EXPERTS) & (lane < N_EXPERTS + N_GROUPS)
    tri = (lax.broadcasted_iota(jnp.int32, (MOE_SUB, MOE_SUB), 0)
           >= lax.broadcasted_iota(jnp.int32, (MOE_SUB, MOE_SUB), 1)).astype(BF16)
    picks, onehots = [], []
    for g, p in enumerate(groups):
        x2_out[p, :] = x2[g].astype(x2_out.dtype)
        logits = hi_w[g][:, :LANES] + hi_w[g][:, LANES:] + lo_w[g] + br_ref[...]
        gl = jnp.where(is_grp, logits, neg)
        gmax = jnp.max(gl, axis=-1, keepdims=True)
        g_p = 1.0 / jnp.sum(jnp.exp(gl - gmax), axis=-1, keepdims=True)
        g_idx = jnp.min(jnp.where(gl == gmax, lane, big), axis=-1, keepdims=True) - N_EXPERTS
        lo_lane = g_idx * EXPERTS_PER_GROUP
        in_grp = (lane >= lo_lane) & (lane < lo_lane + EXPERTS_PER_GROUP)
        el = jnp.where(in_grp, logits, neg)
        e1 = jnp.max(el, axis=-1, keepdims=True)
        i1 = jnp.min(jnp.where(el == e1, lane, big), axis=-1, keepdims=True)
        el2 = jnp.where(lane == i1, neg, el)
        e2 = jnp.max(el2, axis=-1, keepdims=True)
        i2 = jnp.min(jnp.where(el2 == e2, lane, big), axis=-1, keepdims=True)
        t = jnp.exp(e2 - e1)
        wts_out[p, :] = jnp.where(lane == 0, g_p / (1.0 + t),
                                  jnp.where(lane == 1, g_p * t / (1.0 + t), 0.0))
        picks.append((i1, i2))
        onehots.append(jnp.where((lane == i1) | (lane == i2), 1.0, 0.0))

    incl = [_dot(tri, oh.astype(BF16)) for oh in onehots]
    row8 = lax.broadcasted_iota(jnp.int32, (8, LANES), 0)
    chunks = jnp.zeros((8, LANES), F32)
    for g in range(n_sub):
        chunks = jnp.where(row8 == g, jnp.ceil(incl[g][MOE_SUB - 1:, :] * (1.0 / RUN_ALIGN)),
                           chunks)
    cnt_out[...] = chunks
    upper = (lax.broadcasted_iota(jnp.int32, (LANES, LANES), 0)
             < lax.broadcasted_iota(jnp.int32, (LANES, LANES), 1)).astype(BF16)
    run_start = _dot(chunks.astype(BF16), upper) * RUN_ALIGN
    routes = []
    for g, p in enumerate(groups):
        slot_of = run_start[g:g + 1, :] + incl[g] - onehots[g]
        i1, i2 = picks[g]
        slot1 = jnp.sum(jnp.where(lane == i1, slot_of, 0.0), axis=-1, keepdims=True)
        slot2 = jnp.sum(jnp.where(lane == i2, slot_of, 0.0), axis=-1, keepdims=True)
        route = jnp.where(lane == 0, slot1, jnp.where(lane == 1, slot2, 0.0)).astype(jnp.int32)
        route_out[p, :] = route
        routes.append(route.T)

    slot_ids = lax.broadcasted_iota(jnp.int32, (RUN_SLOTS, MOE_SUB), 0)
    for g in range(n_sub):
        pick = jnp.where((slot_ids == routes[g][0:1, :]) | (slot_ids == routes[g][1:2, :]),
                         1.0, 0.0).astype(BF16)
        hl_out[g * RUN_SLOTS:(g + 1) * RUN_SLOTS, :] = _dot(pick, h2[g].astype(BF16)).astype(BF16)


def _expert_kernel(chunk_s, first_s, hl_in_ref, wg_ref, wu_ref, wd_ref, hl_ref, xbuf, wgu_s, wd_s,
                   sem, drain_s):
    del hl_in_ref
    e = pl.program_id(0)
    pass_chunks = EXPERT_ROWS // RUN_ALIGN
    mine = e % 2
    other = 1 - mine
    gather_sem = lambda buf: sem.at[buf]
    scatter_sem = lambda buf: sem.at[2 + buf]

    def first_pass(ex):
        lo = first_s[ex]
        return lo, jnp.minimum(lo + pass_chunks, first_s[ex + 1])

    def for_chunks(buf, c_lo, c_hi, fn):
        def one(c):
            fn(hl_ref.at[pl.ds(pl.multiple_of(chunk_s[c], RUN_ALIGN), RUN_ALIGN)],
               xbuf.at[buf, pl.ds(pl.multiple_of((c - c_lo) * RUN_ALIGN, RUN_ALIGN), RUN_ALIGN)])

        def block(i, carry):
            for u in range(CHUNK_UNROLL):
                one(c_lo + i * CHUNK_UNROLL + u)
            return carry

        def tail(c, carry):
            one(c)
            return carry

        n_blocks = (c_hi - c_lo) // CHUNK_UNROLL
        lax.fori_loop(0, n_blocks, block, 0)
        lax.fori_loop(c_lo + n_blocks * CHUNK_UNROLL, c_hi, tail, 0)

    def gather(buf, c_lo, c_hi):
        for_chunks(buf, c_lo, c_hi, lambda hbm, vm: pltpu.make_async_copy(
            hbm, vm, gather_sem(buf)).start())

    def scatter(buf, c_lo, c_hi):
        for_chunks(buf, c_lo, c_hi, lambda hbm, vm: pltpu.make_async_copy(
            vm, hbm, scatter_sem(buf)).start())

    def wait_chunks(which_sem, n):
        def wait_rows(rows):
            def body(c, carry):
                pltpu.make_async_copy(hl_ref.at[pl.ds(0, rows)], xbuf.at[0, pl.ds(0, rows)],
                                      which_sem).wait()
                return carry
            return body

        lax.fori_loop(0, n // CHUNK_UNROLL, wait_rows(CHUNK_UNROLL * RUN_ALIGN), 0)
        lax.fori_loop(0, n % CHUNK_UNROLL, wait_rows(RUN_ALIGN), 0)

    def evaluate(buf, n_chunks):
        def tile(k, carry):
            sl = pl.ds(pl.multiple_of(k * EXPERT_TILE, EXPERT_TILE), EXPERT_TILE)
            gu = _dot(xbuf[buf, sl, :], wgu_s[...])
            gate = gu[:, :D_EXPERT]
            act = gate * jax.nn.sigmoid(gate) * gu[:, D_EXPERT:]
            xbuf[buf, sl, :] = _dot(act.astype(BF16), wd_s[...]).astype(BF16)
            return carry

        lax.fori_loop(0, (n_chunks * RUN_ALIGN + EXPERT_TILE - 1) // EXPERT_TILE, tile, 0)

    @pl.when(e == 0)
    def _():
        xbuf[...] = jnp.zeros_like(xbuf)
        gather(0, *first_pass(0))
        drain_s[0] = 0

    wgu_s[:, :D_EXPERT] = wg_ref[0].astype(BF16)
    wgu_s[:, D_EXPERT:] = wu_ref[0].astype(BF16)
    wd_s[...] = wd_ref[0].astype(BF16)

    c_lo, c_hi = first_pass(e)
    wait_chunks(gather_sem(mine), c_hi - c_lo)
    evaluate(mine, c_hi - c_lo)
    wait_chunks(scatter_sem(other), drain_s[0])
    scatter(mine, c_lo, c_hi)

    def later_pass(lo):
        hi = jnp.minimum(lo + pass_chunks, first_s[e + 1])
        gather(other, lo, hi)
        wait_chunks(gather_sem(other), hi - lo)
        evaluate(other, hi - lo)
        scatter(other, lo, hi)
        wait_chunks(scatter_sem(other), hi - lo)
        return hi

    lax.while_loop(lambda c: c < first_s[e + 1], later_pass, c_hi)
    drain_s[0] = c_hi - c_lo

    @pl.when(e + 1 < pl.num_programs(0))
    def _():
        gather(other, *first_pass(e + 1))

    @pl.when(e + 1 == pl.num_programs(0))
    def _():
        wait_chunks(scatter_sem(mine), c_hi - c_lo)


def _combine_kernel(x2_ref, route_ref, wts_ref, gf_ref, yl_ref, o_ref):
    slots = lax.broadcasted_iota(jnp.int32, (MOE_SUB, RUN_SLOTS), 1)
    for g in range(x2_ref.shape[0] // MOE_SUB):
        p = slice(g * MOE_SUB, (g + 1) * MOE_SUB)
        route = route_ref[p, :]
        wts = wts_ref[p, :]
        pick = (jnp.where(slots == route[:, 0:1], wts[:, 0:1], 0.0)
                + jnp.where(slots == route[:, 1:2], wts[:, 1:2], 0.0))
        y = _dot(pick.astype(BF16), yl_ref[g * RUN_SLOTS:(g + 1) * RUN_SLOTS, :])
        o_ref[p, :] = _rms(x2_ref[p, :] + y, gf_ref[...])


def _full(shape):
    return pl.BlockSpec(shape, lambda *_: (0,) * len(shape))


def _params(sem):
    return pltpu.CompilerParams(dimension_semantics=sem, vmem_limit_bytes=VMEM_LIMIT)


def _mixer_a_layout(t):
    s_rkv = 3 * A_WIDTH
    s_w = s_rkv + D_DECAY_LORA
    s_a = s_w + D_AAA_LORA
    gap = lambda n: jnp.zeros((t.shape[0], n), t.dtype)
    return jnp.concatenate(
        [t[:, :s_w], gap(XA_OFF - D_DECAY_LORA), t[:, s_w:s_a], gap(XG_OFF - XA_OFF - D_AAA_LORA),
         t[:, s_a:], gap(LORA_PAD - XG_OFF - D_GATE_LORA)], axis=1)


def _place(cols, parts):
    out = jnp.zeros((parts[0][1].shape[0], cols), parts[0][1].dtype)
    for off, arr in parts:
        out = lax.dynamic_update_slice(out, arr, (0, off))
    return out


def kernel(x, norm1_g, w_in, b_gate, tmix_mu, w0, w2, a0, a2, g2, k_k, k_a, r_k, lnx_g, lnx_b,
           w_oA, lnv_g, lnv_b, w_s, b_s, w_oB, w_out, norm2_g, w_rg, b_rg, w_re, b_re,
           w_e_gate, w_e_up, w_e_down, final_g):
    bsz, seq, d = x.shape
    n_tok = bsz * seq
    depth = norm1_g.shape[0]
    assert depth == 1, "the moe kernel fuses the final norm, so it must be the last layer"
    assert bsz % WKV_SEQS == 0 and seq % WKV_TOKENS == 0
    xf = x.reshape(n_tok, d)

    s_rkv = 3 * A_WIDTH
    s_w = s_rkv + D_DECAY_LORA
    s_a = s_w + D_AAA_LORA
    a_cols = s_a + D_GATE_LORA
    b_cols = 2 * B_WIDTH

    ones_bd = (jnp.arange(A_WIDTH)[:, None] // A_HEAD
               == jnp.arange(A_WIDTH)[None, :] // A_HEAD).astype(BF16)

    tm_a = 512
    tm_p = 512

    for l in range(depth):
        wl = w_in[l]
        rows_w = 256
        w_a, w_b, w_g = pl.pallas_call(
            _w_in_kernel,
            grid=(d // rows_w,),
            in_specs=[pl.BlockSpec((wl.shape[1], rows_w), lambda i: (0, i))],
            out_specs=[pl.BlockSpec((A_PROJ, rows_w), lambda i: (0, i)),
                       pl.BlockSpec((b_cols, rows_w), lambda i: (0, i)),
                       pl.BlockSpec((2 * d, rows_w), lambda i: (0, i))],
            out_shape=[jax.ShapeDtypeStruct((A_PROJ, d), BF16),
                       jax.ShapeDtypeStruct((b_cols, d), BF16),
                       jax.ShapeDtypeStruct((2 * d, d), BF16)],
            compiler_params=_params(("parallel",)),
            name="w_in_layout",
        )(wl.T)
        mu_a = _mixer_a_layout(tmix_mu[l][None, :])
        w2p = jnp.pad(w2[l], ((0, LANES - D_DECAY_LORA), (0, 0))).astype(BF16)
        a2p = jnp.pad(a2[l], ((0, LANES - D_AAA_LORA), (0, 0))).astype(BF16)
        g2p = jnp.pad(g2[l], ((0, LORA_PAD - XG_OFF - D_GATE_LORA), (0, 0))).astype(BF16)
        g1 = norm1_g[l][None, :]

        row512 = lambda i: (i, 0)
        tok_a = pl.BlockSpec((tm_a, A_WIDTH), row512)
        tok_d = pl.BlockSpec((tm_a, d), row512)
        vec_a = _full((1, A_WIDTH))
        bs_full = jnp.repeat(b_s[l].T, B_GROUP_CH, axis=1)
        outs = pl.pallas_call(
            functools.partial(_inproj_kernel, seq // tm_a),
            grid=(n_tok // tm_a,),
            in_specs=[tok_d,
                      _full((1, d)), _full((A_PROJ, d)), _full((1, A_PROJ)), vec_a,
                      _full((LANES, A_WIDTH)), vec_a, _full((LANES, A_WIDTH)),
                      _full((LORA_PAD - XG_OFF, A_WIDTH)), vec_a, vec_a,
                      _full((A_WIDTH, A_WIDTH)),
                      _full((b_cols, d)), _full((2 * d, d)), _full((1, 2 * d)),
                      _full((1, B_WIDTH)), _full((1, B_WIDTH)),
                      _full((B_GROUPS, GMLP_BLOCK, GMLP_BLOCK)), _full((GMLP_BLOCK, B_WIDTH)),
                      _full((B_WIDTH, d))],
            out_specs=[tok_a] * 7 + [tok_d, tok_d],
            out_shape=[jax.ShapeDtypeStruct((n_tok, A_WIDTH), F32)] * 7
            + [jax.ShapeDtypeStruct((n_tok, d), BF16)] * 2,
            scratch_shapes=[pltpu.VMEM((8, A_PROJ), F32)],
            compiler_params=_params(("arbitrary",)),
            name="inproj",
        )(xf, g1, w_a, mu_a, w0[l][None, :], w2p, a0[l][None, :], a2p, g2p,
          k_k[l][None, :], k_a[l][None, :], ones_bd, w_b, w_g, b_gate[l][None, :],
          lnv_g[l][None, :], lnv_b[l][None, :], w_s[l], bs_full, w_oB[l].astype(BF16))
        r_, k_, v_, a_, b_, ld_, gg_, ybg, ga = outs

        n_pairs = A_WIDTH // HEAD_PAIR
        n_chunks = WKV_TOKENS // WKV_CHUNK
        tok_w = pl.BlockSpec((WKV_SEQS, WKV_TOKENS, A_WIDTH), lambda bi, ti: (bi, ti, 0))
        vec_w = _full((1, A_WIDTH))
        seq_major = lambda t: t.reshape(bsz, seq, A_WIDTH)
        tok_scratch = pltpu.VMEM((WKV_SEQS, WKV_TOKENS, A_WIDTH), BF16)
        ya_in = pl.pallas_call(
            _wkv_kernel,
            grid=(bsz // WKV_SEQS, seq // WKV_TOKENS),
            in_specs=[tok_w] * 7 + [vec_w, vec_w, vec_w, _full((HEAD_PAIR, HEAD_PAIR))],
            out_specs=tok_w,
            out_shape=jax.ShapeDtypeStruct((bsz, seq, A_WIDTH), BF16),
            scratch_shapes=[pltpu.VMEM((WKV_SEQS, n_pairs, HEAD_PAIR, HEAD_PAIR), F32)]
            + [tok_scratch] * 5
            + [pltpu.VMEM((WKV_SEQS, n_chunks, n_pairs, HEAD_PAIR, HEAD_PAIR), BF16),
               pltpu.VMEM((WKV_SEQS, n_chunks, n_pairs, HEAD_PAIR, HEAD_PAIR), F32),
               pltpu.VMEM((WKV_SEQS, WKV_TOKENS, A_WIDTH), F32)],
            compiler_params=_params(("parallel", "arbitrary")),
            name="wkv",
        )(*[seq_major(t) for t in (r_, k_, v_, a_, b_, ld_, gg_)], r_k[l].reshape(1, A_WIDTH),
          lnx_g[l][None, :], lnx_b[l][None, :],
          ones_bd[:HEAD_PAIR, :HEAD_PAIR]).reshape(n_tok, A_WIDTH)

        w_r = _place(LANES, [(0, jnp.transpose(w_re[l], (1, 0, 2)).reshape(d, N_EXPERTS)),
                             (N_EXPERTS, w_rg[l])])
        b_r = _place(LANES, [(0, b_re[l].reshape(1, N_EXPERTS)), (N_EXPERTS, b_rg[l][None, :])])
        wr_hi = w_r.astype(BF16)
        wr_lo = (w_r - wr_hi.astype(F32)).astype(BF16)
        tok_p = pl.BlockSpec((tm_p, d), row512)
        lane_p = pl.BlockSpec((tm_p, LANES), row512)
        n_sub = n_tok // MOE_SUB
        sorted_rows = tm_p // MOE_SUB * RUN_SLOTS
        x2, h_sorted, route, wts, cnt = pl.pallas_call(
            _post_kernel,
            grid=(n_tok // tm_p,),
            in_specs=[tok_p, pl.BlockSpec((tm_p, A_WIDTH), row512), tok_p, tok_p,
                      _full((A_WIDTH, d)), _full((d, d)), _full((1, d)), _full((d, 2 * LANES)),
                      _full((d, LANES)), _full((1, LANES))],
            out_specs=[tok_p, pl.BlockSpec((sorted_rows, d), row512), lane_p, lane_p,
                       pl.BlockSpec((8, LANES), row512)],
            out_shape=[jax.ShapeDtypeStruct((n_tok, d), BF16),
                       jax.ShapeDtypeStruct((n_sub * RUN_SLOTS, d), BF16),
                       jax.ShapeDtypeStruct((n_tok, LANES), jnp.int32),
                       jax.ShapeDtypeStruct((n_tok, LANES), F32),
                       jax.ShapeDtypeStruct((n_tok // tm_p * 8, LANES), F32)],
            compiler_params=_params(("parallel",)),
            name="post",
        )(xf, ya_in, ga, ybg, w_oA[l].astype(BF16), w_out[l].astype(BF16),
          norm2_g[l][None, :], jnp.concatenate([wr_hi, wr_lo], axis=1), wr_hi, b_r)

        run_chunks = cnt.reshape(n_tok // tm_p, 8, LANES)[:, :tm_p // MOE_SUB, :N_EXPERTS]
        run_chunks = run_chunks.reshape(n_sub, N_EXPERTS).astype(jnp.int32)
        run_start = (jnp.cumsum(run_chunks, axis=1) - run_chunks) * RUN_ALIGN
        run_row = (jnp.arange(n_sub, dtype=jnp.int32)[:, None] * RUN_SLOTS + run_start).T.reshape(-1)
        run_n = run_chunks.T.reshape(-1)
        run_end = jnp.cumsum(run_n)
        first_chunk = jnp.concatenate([jnp.zeros((1,), jnp.int32), run_end[n_sub - 1::n_sub]])
        pos = jnp.arange(n_sub * RUN_SLOTS // RUN_ALIGN, dtype=jnp.int32)
        run_of = jnp.sum(pos[:, None] >= run_end[None, :], axis=1)
        in_run = run_of[:, None] == jnp.arange(run_n.shape[0], dtype=jnp.int32)[None, :]
        pick = lambda t: jnp.sum(jnp.where(in_run, t[None, :], 0), axis=1)
        chunk_rows = pick(run_row) + (pos - pick(run_end - run_n)) * RUN_ALIGN

        per_expert = lambda e, *_: (e, 0, 0)
        y_sorted = pl.pallas_call(
            _expert_kernel,
            grid_spec=pltpu.PrefetchScalarGridSpec(
                num_scalar_prefetch=2, grid=(N_EXPERTS,),
                in_specs=[pl.BlockSpec(memory_space=pl.ANY),
                          pl.BlockSpec((1, d, D_EXPERT), per_expert),
                          pl.BlockSpec((1, d, D_EXPERT), per_expert),
                          pl.BlockSpec((1, D_EXPERT, d), per_expert)],
                out_specs=pl.BlockSpec(memory_space=pl.ANY),
                scratch_shapes=[pltpu.VMEM((2, EXPERT_ROWS, d), BF16),
                                pltpu.VMEM((d, 2 * D_EXPERT), BF16),
                                pltpu.VMEM((D_EXPERT, d), BF16),
                                pltpu.SemaphoreType.DMA((4,)),
                                pltpu.SMEM((1,), jnp.int32)]),
            out_shape=jax.ShapeDtypeStruct((n_sub * RUN_SLOTS, d), BF16),
            input_output_aliases={2: 0},
            compiler_params=_params(("arbitrary",)),
            name="moe_experts",
        )(chunk_rows, first_chunk, h_sorted, w_e_gate[l], w_e_up[l], w_e_down[l])

        xf = pl.pallas_call(
            _combine_kernel,
            grid=(n_tok // tm_p,),
            in_specs=[tok_p, lane_p, lane_p, _full((1, d)),
                      pl.BlockSpec((sorted_rows, d), row512)],
            out_specs=tok_p,
            out_shape=jax.ShapeDtypeStruct((n_tok, d), F32),
            compiler_params=_params(("parallel",)),
            name="moe_combine",
        )(x2, route, wts, final_g[None, :], y_sorted)

    return xf.reshape(bsz, seq, d)
```

```python
import functools

import jax
import jax.numpy as jnp
from jax import lax
from jax.experimental import pallas as pl
from jax.experimental.pallas import tpu as pltpu

F32 = jnp.float32
BF16 = jnp.bfloat16

D_MODEL = 1024
A_WIDTH = 512
A_HEAD = 64
D_DECAY_LORA = 64
D_AAA_LORA = 64
D_GATE_LORA = 160
B_WIDTH = 512
B_GROUPS = 4
B_GROUP_CH = 128
GMLP_BLOCK = 128
N_GROUPS = 4
EXPERTS_PER_GROUP = 8
N_EXPERTS = 32
D_EXPERT = 256
NORM_EPS = 1e-6
LN_EPS = 1e-5
LNX_EPS = 64e-5

LANES = 128
LORA_PAD = 512
XW_OFF, XA_OFF, XG_OFF = 0, 128, 256
A_PROJ = 3 * A_WIDTH + LORA_PAD
WKV_CHUNK = 64
HEAD_PAIR = 2 * A_HEAD
WKV_SEQS = 4
WKV_TOKENS = 128
WKV_PREP_GROUP = 4
WKV_NORM_GROUP = 2
MOE_SUB = 256
RUN_ALIGN = 16
RUN_SLOTS = 2 * MOE_SUB + N_EXPERTS * RUN_ALIGN
EXPERT_ROWS = 2048
EXPERT_TILE = 512
DENSE_SPLIT = 256
CHUNK_UNROLL = 4
VMEM_LIMIT = 48 * 1024 * 1024


def _rms(x, g):
    return x * lax.rsqrt(jnp.mean(x * x, axis=-1, keepdims=True) + NORM_EPS) * g


def _dot(a, b):
    return jnp.dot(a, b, preferred_element_type=F32)


def _dot_nt(a, b):
    return lax.dot_general(a, b, (((1,), (1,)), ((), ())), preferred_element_type=F32)


def _split2(x):
    hi = x.astype(BF16)
    lo = (x - hi.astype(F32)).astype(BF16)
    return hi, lo


def _split3(x):
    hi = x.astype(BF16)
    r1 = x - hi.astype(F32)
    mid = r1.astype(BF16)
    lo = (r1 - mid.astype(F32)).astype(BF16)
    return hi, mid, lo


def _pair_head_sums(xs, ones_pair):
    n_tiles = A_WIDTH // HEAD_PAIR
    rows = xs[0].shape[0]
    tiles = [x[:, t * HEAD_PAIR:(t + 1) * HEAD_PAIR].astype(BF16) for x in xs for t in range(n_tiles)]
    sums = _dot(jnp.concatenate(tiles, axis=0), ones_pair)
    return [jnp.concatenate([sums[(i * n_tiles + t) * rows:(i * n_tiles + t + 1) * rows]
                             for t in range(n_tiles)], axis=1) for i in range(len(xs))]


def _w_in_kernel(w_ref, wa_out, wb_out, wg_out):
    s_rkv = 3 * A_WIDTH
    s_w = s_rkv + D_DECAY_LORA
    s_a = s_w + D_AAA_LORA
    a_cols = s_a + D_GATE_LORA
    b_end = a_cols + 2 * B_WIDTH
    wa_out[...] = jnp.zeros_like(wa_out)
    wa_out[:s_w, :] = w_ref[:s_w, :].astype(BF16)
    wa_out[s_rkv + XA_OFF:s_rkv + XA_OFF + D_AAA_LORA, :] = w_ref[s_w:s_a, :].astype(BF16)
    wa_out[s_rkv + XG_OFF:s_rkv + XG_OFF + D_GATE_LORA, :] = w_ref[s_a:a_cols, :].astype(BF16)
    wb_out[...] = w_ref[a_cols:b_end, :].astype(BF16)
    wg_out[...] = w_ref[b_end:, :].astype(BF16)


def _inproj_kernel(tiles_per_seq, x_ref, g1_ref, wa_ref, mu_ref, w0_ref, w2_ref, a0_ref, a2_ref,
                   g2_ref, kk_ref, ka_ref, ones_ref, wb_ref, wg_ref, bg_ref, lng_ref, lnb_ref,
                   ws_ref, bs_ref, wo_ref,
                   r_out, k_out, v_out, a_out, b_out, ld_out, g_out, ybg_out, ga_out, tail_ref):
    i = pl.program_id(0)
    tm = x_ref.shape[0]
    parts = [slice(j * DENSE_SPLIT, (j + 1) * DENSE_SPLIT) for j in range(tm // DENSE_SPLIT)]
    g1 = g1_ref[...]
    h = [_rms(x_ref[p, :], g1).astype(BF16) for p in parts]
    proj = [_dot_nt(hh, wa_ref[...]) for hh in h]
    pb = [_dot_nt(hh, wb_ref[...]) for hh in h]

    prev = [jnp.where(i % tiles_per_seq == 0, 0.0, tail_ref[7:8, :])]
    prev += [p[DENSE_SPLIT - 1:, :] for p in proj[:-1]]
    tail_ref[...] = proj[-1][DENSE_SPLIT - 8:, :]
    row = lax.broadcasted_iota(jnp.int32, proj[0].shape, 0)
    pm = []
    for p, pv in zip(proj, prev):
        shifted = jnp.where(row == 0, pv, pltpu.roll(p, 1, axis=0))
        pm.append(p + mu_ref[...] * (shifted - p))
    gates = [jax.nn.sigmoid(_dot_nt(hh, wg_ref[...]) + bg_ref[...]) for hh in h]

    tri = (lax.broadcasted_iota(jnp.int32, (GMLP_BLOCK, GMLP_BLOCK), 0)
           >= lax.broadcasted_iota(jnp.int32, (GMLP_BLOCK, GMLP_BLOCK), 1))
    ws = [jnp.where(tri, ws_ref[grp], 0.0).astype(BF16) for grp in range(B_GROUPS)]
    bs = bs_ref[...]
    us, vns = [], []
    for x in pb:
        z = 0.5 * x * (1.0 + lax.erf(x * (2.0 ** -0.5)))
        us.append(z[:, :B_WIDTH])
        v = z[:, B_WIDTH:]
        mean = jnp.mean(v, axis=-1, keepdims=True)
        vc = v - mean
        var = jnp.mean(vc * vc, axis=-1, keepdims=True)
        vns.append((vc * lax.rsqrt(var + LN_EPS) * lng_ref[...] + lnb_ref[...]).astype(BF16))

    lora = [x[:, 3 * A_WIDTH:] for x in pm]
    decay = [_dot(jnp.tanh(x[:, XW_OFF:XW_OFF + LANES]).astype(BF16), w2_ref[...]) for x in lora]
    rate = [_dot(x[:, XA_OFF:XA_OFF + LANES].astype(BF16), a2_ref[...]) for x in lora]
    gate = [_dot(jax.nn.sigmoid(x[:, XG_OFF:]).astype(BF16), g2_ref[...]) for x in lora]
    kks = [x[:, A_WIDTH:2 * A_WIDTH] * kk_ref[...] for x in pm]
    sq = [_dot((kk * kk).astype(BF16), ones_ref[...]) for kk in kks]

    svs = []
    for vn in vns:
        rows = []
        for blk in range(DENSE_SPLIT // GMLP_BLOCK):
            cols = [_dot(ws[grp], vn[blk * GMLP_BLOCK:(blk + 1) * GMLP_BLOCK,
                                     grp * B_GROUP_CH:(grp + 1) * B_GROUP_CH])
                    for grp in range(B_GROUPS)]
            rows.append(jnp.concatenate(cols, axis=1) + bs)
        svs.append(jnp.concatenate(rows, axis=0))
    yb = [_dot((u * sv).astype(BF16), wo_ref[...]) for u, sv in zip(us, svs)]

    for j, p in enumerate(parts):
        k = pm[j][:, A_WIDTH:2 * A_WIDTH]
        z = -(w0_ref[...] + decay[j])
        softplus = jnp.maximum(z, 0.0) + jnp.log(1.0 + jnp.exp(-jnp.abs(z)))
        w = -softplus - 0.5
        a_lr = jax.nn.sigmoid(a0_ref[...] + rate[j])
        kk = kks[j] / jnp.maximum(jnp.sqrt(sq[j]), 1e-12)
        r_out[p, :] = pm[j][:, 0:A_WIDTH]
        k_out[p, :] = k * (1.0 + (a_lr - 1.0) * ka_ref[...])
        v_out[p, :] = pm[j][:, 2 * A_WIDTH:3 * A_WIDTH]
        a_out[p, :] = -kk
        b_out[p, :] = kk * a_lr
        ld_out[p, :] = -jnp.exp(w)
        g_out[p, :] = gate[j]
        ga_out[p, :] = gates[j][:, :D_MODEL].astype(BF16)
        ybg_out[p, :] = (gates[j][:, D_MODEL:] * yb[j]).astype(BF16)


def _wkv_kernel(r_ref, k_ref, v_ref, a_ref, b_ref, ld_ref, g_ref, rk_ref, lng_ref, lnb_ref,
                ones_ref, o_ref, st_ref, ta_s, tl_s, arb_s, ark_s, rt_s, bkt_s, dcol_s, y_s):
    C = WKV_CHUNK
    bb, tb, _ = r_ref.shape
    n_chunks = tb // C
    n_pairs = A_WIDTH // HEAD_PAIR

    @pl.when(pl.program_id(1) == 0)
    def _():
        st_ref[...] = jnp.zeros_like(st_ref)

    row = lax.broadcasted_iota(jnp.int32, (C, HEAD_PAIR), 0)
    src = lax.broadcasted_iota(jnp.int32, (C, HEAD_PAIR), 1) & (C - 1)
    incl = src <= row
    strict = src < row
    eye_pair = jnp.where(src == row, 1.0, 0.0)
    bd_mask = ((lax.broadcasted_iota(jnp.int32, (HEAD_PAIR, HEAD_PAIR), 0) >= A_HEAD)
               == (lax.broadcasted_iota(jnp.int32, (HEAD_PAIR, HEAD_PAIR), 1) >= A_HEAD))
    tri_c = (lax.broadcasted_iota(jnp.int32, (C, C), 0)
             >= lax.broadcasted_iota(jnp.int32, (C, C), 1)).astype(BF16)
    ones_pair = ones_ref[...]
    pair_cols = [slice(p * HEAD_PAIR, (p + 1) * HEAD_PAIR) for p in range(n_pairs)]

    def bd(x):
        xb = x.astype(BF16)
        return jnp.where(bd_mask, jnp.concatenate([xb, xb], axis=0), jnp.zeros((), BF16))

    def prep(it, carry):
        where, lhs, rhs, ats = [], [], [], []
        for j in range(WKV_PREP_GROUP):
            flat = it * WKV_PREP_GROUP + j
            b = flat // n_chunks
            c = flat % n_chunks
            sl = pl.ds(pl.multiple_of(c * C, C), C)
            ld = ld_ref[b, sl, :]
            hi, mid, lo = _split3(ld)
            cl = _dot(tri_c, hi) + _dot(tri_c, mid) + _dot(tri_c, lo)
            cl_end = cl[C - 1:C, :]
            k = k_ref[b, sl, :]
            bv = b_ref[b, sl, :]
            d_inv = jnp.exp(-cl)
            d_tail = jnp.exp(cl_end - cl)
            rt = (r_ref[b, sl, :] * jnp.exp(cl)).astype(BF16)
            rt_s[b, sl, :] = rt
            kt = k * d_inv
            bt = bv * d_inv
            at = a_ref[b, sl, :] * jnp.exp(cl - ld)
            kd = k * d_tail
            bdk = bv * d_tail
            d_end = jnp.exp(cl_end)
            for p, cs in enumerate(pair_cols):
                bkt_s[b, c, p] = jnp.concatenate([bdk[:, cs], kd[:, cs]], axis=0).T.astype(BF16)
                dcol_s[b, c, p] = jnp.broadcast_to(d_end[:, cs], (HEAD_PAIR, HEAD_PAIR)).T
                where.append((b, sl, cs))
                ats.append(at[:, cs])
                lhs.append(jnp.concatenate([rt[:, cs], at[:, cs].astype(BF16)], axis=0))
                rhs.append(jnp.concatenate([bd(kt[:, cs]), bd(bt[:, cs])], axis=0))
        n = len(where)
        amat = [_dot_nt(lhs[i], rhs[i]) for i in range(n)]
        l_ak, l_ab = [], []
        for i, (b, sl, cs) in enumerate(where):
            ark_s[b, sl, cs] = jnp.where(incl, amat[i][:C, :HEAD_PAIR], 0.0).astype(BF16)
            arb_s[b, sl, cs] = jnp.where(incl, amat[i][:C, HEAD_PAIR:], 0.0).astype(BF16)
            l_ak.append(jnp.where(strict, amat[i][C:, :HEAD_PAIR], 0.0))
            l_ab.append(jnp.where(strict, amat[i][C:, HEAD_PAIR:], 0.0))

        t_mat = [eye_pair + l for l in l_ab]
        q = [_dot(l.astype(BF16), bd(l)) for l in l_ab]
        n_sq = 1
        while 2 * n_sq < C // 2:
            both = [_dot(q[i].astype(BF16), jnp.concatenate([bd(q[i]), bd(t_mat[i])], axis=1))
                    for i in range(n)]
            q = [x[:, :HEAD_PAIR] for x in both]
            t_mat = [t_mat[i] + both[i][:, HEAD_PAIR:] for i in range(n)]
            n_sq *= 2
        t_mat = [t_mat[i] + _dot(q[i].astype(BF16), bd(t_mat[i])) for i in range(n)]
        tal = [_dot(t_mat[i].astype(BF16), jnp.concatenate([bd(ats[i]), bd(l_ak[i])], axis=1))
               for i in range(n)]
        for i, (b, sl, cs) in enumerate(where):
            ta_s[b, sl, cs] = tal[i][:, :HEAD_PAIR].astype(BF16)
            tl_s[b, sl, cs] = tal[i][:, HEAD_PAIR:].astype(BF16)
        return carry

    lax.fori_loop(0, bb * n_chunks // WKV_PREP_GROUP, prep, 0)

    def step(c, carry):
        sl = pl.ds(pl.multiple_of(c * C, C), C)
        chains = [(b, p) for b in range(bb) for p in range(n_pairs)]
        v = [v_ref[b, sl, :] for b in range(bb)]
        st = [st_ref[b, p] for b, p in chains]
        st_b = [x.astype(BF16) for x in st]
        bd_v = [bd(v[b][:, pair_cols[p]]) for b, p in chains]
        on_st = [_dot(jnp.concatenate([ta_s[b, sl, pair_cols[p]], rt_s[b, sl, pair_cols[p]]],
                                      axis=0), st_b[i]) for i, (b, p) in enumerate(chains)]
        on_v = [_dot(jnp.concatenate([tl_s[b, sl, pair_cols[p]], ark_s[b, sl, pair_cols[p]]],
                                     axis=0), bd_v[i]) for i, (b, p) in enumerate(chains)]
        u = [on_st[i][:C] + on_v[i][:C] for i in range(len(chains))]
        for i, (b, p) in enumerate(chains):
            uv = jnp.concatenate([u[i], v[b][:, pair_cols[p]]], axis=0).astype(BF16)
            st_ref[b, p] = (dcol_s[b, c, p] * st[i]
                            + jnp.where(bd_mask, _dot(bkt_s[b, c, p], uv), 0.0))
        ys = [on_st[i][C:] + on_v[i][C:] + _dot(arb_s[b, sl, pair_cols[p]], bd(u[i]))
              for i, (b, p) in enumerate(chains)]
        for b in range(bb):
            y_s[b, sl, :] = jnp.concatenate(ys[b * n_pairs:(b + 1) * n_pairs], axis=1)
        return carry

    lax.fori_loop(0, n_chunks, step, 0)

    def finish(it, carry):
        items = [(b, pl.ds(pl.multiple_of((it * WKV_NORM_GROUP + j) * C, C), C))
                 for j in range(WKV_NORM_GROUP) for b in range(bb)]
        y = [y_s[b, sl, :] for b, sl in items]
        rkr = [r_ref[b, sl, :] * k_ref[b, sl, :] * rk_ref[...] for b, sl in items]
        sums = [_pair_head_sums([y[i], rkr[i]], ones_pair) for i in range(len(items))]
        yc = [y[i] - sums[i][0] * (1.0 / A_HEAD) for i in range(len(items))]
        var = [_pair_head_sums([x * x], ones_pair)[0] * (1.0 / A_HEAD) for x in yc]
        for i, (b, sl) in enumerate(items):
            yn = yc[i] * lax.rsqrt(var[i] + LNX_EPS) * lng_ref[...] + lnb_ref[...]
            out = (yn + sums[i][1] * v_ref[b, sl, :]) * g_ref[b, sl, :]
            o_ref[b, sl, :] = out.astype(o_ref.dtype)
        return carry

    lax.fori_loop(0, n_chunks // WKV_NORM_GROUP, finish, 0)


def _post_kernel(x_ref, ya_ref, ga_ref, ybg_ref, woa_ref, wout_ref, g2_ref, wr_ref,
                 wr_hi_ref, br_ref, x2_out, hl_out, route_out, wts_out, cnt_out):
    n_sub = x_ref.shape[0] // MOE_SUB
    groups = [slice(g * MOE_SUB, (g + 1) * MOE_SUB) for g in range(n_sub)]
    y_a = [_dot(ya_ref[p, :].astype(BF16), woa_ref[...]) for p in groups]
    x2 = [x_ref[p, :] + _dot((ga_ref[p, :] * y + ybg_ref[p, :]).astype(BF16), wout_ref[...])
          for p, y in zip(groups, y_a)]
    h2 = [_rms(x, g2_ref[...]) for x in x2]
    split = [_split2(h) for h in h2]
    hi_w = [_dot(hi, wr_ref[...]) for hi, _ in split]
    lo_w = [_dot(lo, wr_hi_ref[...]) for _, lo in split]
    lane = lax.broadcasted_iota(jnp.int32, (MOE_SUB, LANES), 1)
    neg = jnp.float32(-jnp.inf)
    big = jnp.int32(LANES)
    is_grp = (lane >= N_EXPERTS) & (lane < N_EXPERTS + N_GROUPS)
    tri = (lax.broadcasted_iota(jnp.int32, (MOE_SUB, MOE_SUB), 0)
           >= lax.broadcasted_iota(jnp.int32, (MOE_SUB, MOE_SUB), 1)).astype(BF16)
    picks, onehots = [], []
    for g, p in enumerate(groups):
        x2_out[p, :] = x2[g].astype(x2_out.dtype)
        logits = hi_w[g][:, :LANES] + hi_w[g][:, LANES:] + lo_w[g] + br_ref[...]
        gl = jnp.where(is_grp, logits, neg)
        gmax = jnp.max(gl, axis=-1, keepdims=True)
        g_p = 1.0 / jnp.sum(jnp.exp(gl - gmax), axis=-1, keepdims=True)
        g_idx = jnp.min(jnp.where(gl == gmax, lane, big), axis=-1, keepdims=True) - N_EXPERTS
        lo_lane = g_idx * EXPERTS_PER_GROUP
        in_grp = (lane >= lo_lane) & (lane < lo_lane + EXPERTS_PER_GROUP)
        el = jnp.where(in_grp, logits, neg)
        e1 = jnp.max(el, axis=-1, keepdims=True)
        i1 = jnp.min(jnp.where(el == e1, lane, big), axis=-1, keepdims=True)
        el2 = jnp.where(lane == i1, neg, el)
        e2 = jnp.max(el2, axis=-1, keepdims=True)
        i2 = jnp.min(jnp.where(el2 == e2, lane, big), axis=-1, keepdims=True)
        t = jnp.exp(e2 - e1)
        wts_out[p, :] = jnp.where(lane == 0, g_p / (1.0 + t),
                                  jnp.where(lane == 1, g_p * t / (1.0 + t), 0.0))
        picks.append((i1, i2))
        onehots.append(jnp.where((lane == i1) | (lane == i2), 1.0, 0.0))

    incl = [_dot(tri, oh.astype(BF16)) for oh in onehots]
    row8 = lax.broadcasted_iota(jnp.int32, (8, LANES), 0)
    chunks = jnp.zeros((8, LANES), F32)
    for g in range(n_sub):
        chunks = jnp.where(row8 == g, jnp.ceil(incl[g][MOE_SUB - 1:, :] * (1.0 / RUN_ALIGN)),
                           chunks)
    cnt_out[...] = chunks
    upper = (lax.broadcasted_iota(jnp.int32, (LANES, LANES), 0)
             < lax.broadcasted_iota(jnp.int32, (LANES, LANES), 1)).astype(BF16)
    run_start = _dot(chunks.astype(BF16), upper) * RUN_ALIGN
    routes = []
    for g, p in enumerate(groups):
        slot_of = run_start[g:g + 1, :] + incl[g] - onehots[g]
        i1, i2 = picks[g]
        slot1 = jnp.sum(jnp.where(lane == i1, slot_of, 0.0), axis=-1, keepdims=True)
        slot2 = jnp.sum(jnp.where(lane == i2, slot_of, 0.0), axis=-1, keepdims=True)
        route = jnp.where(lane == 0, slot1, jnp.where(lane == 1, slot2, 0.0)).astype(jnp.int32)
        route_out[p, :] = route
        routes.append(route.T)

    slot_ids = lax.broadcasted_iota(jnp.int32, (RUN_SLOTS, MOE_SUB), 0)
    for g in range(n_sub):
        pick = jnp.where((slot_ids == routes[g][0:1, :]) | (slot_ids == routes[g][1:2, :]),
                         1.0, 0.0).astype(BF16)
        hl_out[g * RUN_SLOTS:(g + 1) * RUN_SLOTS, :] = _dot(pick, h2[g].astype(BF16)).astype(BF16)


def _expert_kernel(chunk_s, first_s, hl_in_ref, wg_ref, wu_ref, wd_ref, hl_ref, xbuf, wgu_s, wd_s,
                   sem, drain_s):
    del hl_in_ref
    e = pl.program_id(0)
    pass_chunks = EXPERT_ROWS // RUN_ALIGN
    mine = e % 2
    other = 1 - mine
    gather_sem = lambda buf: sem.at[buf]
    scatter_sem = lambda buf: sem.at[2 + buf]

    def first_pass(ex):
        lo = first_s[ex]
        return lo, jnp.minimum(lo + pass_chunks, first_s[ex + 1])

    def for_chunks(buf, c_lo, c_hi, fn):
        def one(c):
            fn(hl_ref.at[pl.ds(pl.multiple_of(chunk_s[c], RUN_ALIGN), RUN_ALIGN)],
               xbuf.at[buf, pl.ds(pl.multiple_of((c - c_lo) * RUN_ALIGN, RUN_ALIGN), RUN_ALIGN)])

        def block(i, carry):
            for u in range(CHUNK_UNROLL):
                one(c_lo + i * CHUNK_UNROLL + u)
            return carry

        def tail(c, carry):
            one(c)
            return carry

        n_blocks = (c_hi - c_lo) // CHUNK_UNROLL
        lax.fori_loop(0, n_blocks, block, 0)
        lax.fori_loop(c_lo + n_blocks * CHUNK_UNROLL, c_hi, tail, 0)

    def gather(buf, c_lo, c_hi):
        for_chunks(buf, c_lo, c_hi, lambda hbm, vm: pltpu.make_async_copy(
            hbm, vm, gather_sem(buf)).start())

    def scatter(buf, c_lo, c_hi):
        for_chunks(buf, c_lo, c_hi, lambda hbm, vm: pltpu.make_async_copy(
            vm, hbm, scatter_sem(buf)).start())

    def wait_chunks(which_sem, n):
        def wait_rows(rows):
            def body(c, carry):
                pltpu.make_async_copy(hl_ref.at[pl.ds(0, rows)], xbuf.at[0, pl.ds(0, rows)],
                                      which_sem).wait()
                return carry
            return body

        lax.fori_loop(0, n // CHUNK_UNROLL, wait_rows(CHUNK_UNROLL * RUN_ALIGN), 0)
        lax.fori_loop(0, n % CHUNK_UNROLL, wait_rows(RUN_ALIGN), 0)

    def evaluate(buf, n_chunks):
        def tile(k, carry):
            sl = pl.ds(pl.multiple_of(k * EXPERT_TILE, EXPERT_TILE), EXPERT_TILE)
            gu = _dot(xbuf[buf, sl, :], wgu_s[...])
            gate = gu[:, :D_EXPERT]
            act = gate * jax.nn.sigmoid(gate) * gu[:, D_EXPERT:]
            xbuf[buf, sl, :] = _dot(act.astype(BF16), wd_s[...]).astype(BF16)
            return carry

        lax.fori_loop(0, (n_chunks * RUN_ALIGN + EXPERT_TILE - 1) // EXPERT_TILE, tile, 0)

    @pl.when(e == 0)
    def _():
        xbuf[...] = jnp.zeros_like(xbuf)
        gather(0, *first_pass(0))
        drain_s[0] = 0

    wgu_s[:, :D_EXPERT] = wg_ref[0].astype(BF16)
    wgu_s[:, D_EXPERT:] = wu_ref[0].astype(BF16)
    wd_s[...] = wd_ref[0].astype(BF16)

    c_lo, c_hi = first_pass(e)
    wait_chunks(gather_sem(mine), c_hi - c_lo)
    wait_chunks(scatter_sem(other), drain_s[0])

    @pl.when(e + 1 < pl.num_programs(0))
    def _():
        gather(other, *first_pass(e + 1))

    evaluate(mine, c_hi - c_lo)
    scatter(mine, c_lo, c_hi)

    def later_pass(state):
        lo, pending = state
        hi = jnp.minimum(lo + pass_chunks, first_s[e + 1])
        wait_chunks(scatter_sem(mine), pending)
        gather(mine, lo, hi)
        wait_chunks(gather_sem(mine), hi - lo)
        evaluate(mine, hi - lo)
        scatter(mine, lo, hi)
        return hi, hi - lo

    _, pending = lax.while_loop(lambda st: st[0] < first_s[e + 1], later_pass,
                                (c_hi, c_hi - c_lo))
    drain_s[0] = pending

    @pl.when(e + 1 == pl.num_programs(0))
    def _():
        wait_chunks(scatter_sem(mine), pending)


def _combine_kernel(x2_ref, route_ref, wts_ref, gf_ref, yl_ref, o_ref):
    slots = lax.broadcasted_iota(jnp.int32, (MOE_SUB, RUN_SLOTS), 1)
    for g in range(x2_ref.shape[0] // MOE_SUB):
        p = slice(g * MOE_SUB, (g + 1) * MOE_SUB)
        route = route_ref[p, :]
        wts = wts_ref[p, :]
        pick = (jnp.where(slots == route[:, 0:1], wts[:, 0:1], 0.0)
                + jnp.where(slots == route[:, 1:2], wts[:, 1:2], 0.0))
        y = _dot(pick.astype(BF16), yl_ref[g * RUN_SLOTS:(g + 1) * RUN_SLOTS, :])
        o_ref[p, :] = _rms(x2_ref[p, :] + y, gf_ref[...])


def _full(shape):
    return pl.BlockSpec(shape, lambda *_: (0,) * len(shape))


def _params(sem):
    return pltpu.CompilerParams(dimension_semantics=sem, vmem_limit_bytes=VMEM_LIMIT)


def _mixer_a_layout(t):
    s_rkv = 3 * A_WIDTH
    s_w = s_rkv + D_DECAY_LORA
    s_a = s_w + D_AAA_LORA
    gap = lambda n: jnp.zeros((t.shape[0], n), t.dtype)
    return jnp.concatenate(
        [t[:, :s_w], gap(XA_OFF - D_DECAY_LORA), t[:, s_w:s_a], gap(XG_OFF - XA_OFF - D_AAA_LORA),
         t[:, s_a:], gap(LORA_PAD - XG_OFF - D_GATE_LORA)], axis=1)


def _place(cols, parts):
    out = jnp.zeros((parts[0][1].shape[0], cols), parts[0][1].dtype)
    for off, arr in parts:
        out = lax.dynamic_update_slice(out, arr, (0, off))
    return out


def kernel(x, norm1_g, w_in, b_gate, tmix_mu, w0, w2, a0, a2, g2, k_k, k_a, r_k, lnx_g, lnx_b,
           w_oA, lnv_g, lnv_b, w_s, b_s, w_oB, w_out, norm2_g, w_rg, b_rg, w_re, b_re,
           w_e_gate, w_e_up, w_e_down, final_g):
    bsz, seq, d = x.shape
    n_tok = bsz * seq
    depth = norm1_g.shape[0]
    assert depth == 1, "the moe kernel fuses the final norm, so it must be the last layer"
    assert bsz % WKV_SEQS == 0 and seq % WKV_TOKENS == 0
    xf = x.reshape(n_tok, d)

    s_rkv = 3 * A_WIDTH
    s_w = s_rkv + D_DECAY_LORA
    s_a = s_w + D_AAA_LORA
    a_cols = s_a + D_GATE_LORA
    b_cols = 2 * B_WIDTH

    ones_bd = (jnp.arange(A_WIDTH)[:, None] // A_HEAD
               == jnp.arange(A_WIDTH)[None, :] // A_HEAD).astype(BF16)

    tm_a = 512
    tm_p = 512

    for l in range(depth):
        wl = w_in[l]
        rows_w = 256
        w_a, w_b, w_g = pl.pallas_call(
            _w_in_kernel,
            grid=(d // rows_w,),
            in_specs=[pl.BlockSpec((wl.shape[1], rows_w), lambda i: (0, i))],
            out_specs=[pl.BlockSpec((A_PROJ, rows_w), lambda i: (0, i)),
                       pl.BlockSpec((b_cols, rows_w), lambda i: (0, i)),
                       pl.BlockSpec((2 * d, rows_w), lambda i: (0, i))],
            out_shape=[jax.ShapeDtypeStruct((A_PROJ, d), BF16),
                       jax.ShapeDtypeStruct((b_cols, d), BF16),
                       jax.ShapeDtypeStruct((2 * d, d), BF16)],
            compiler_params=_params(("parallel",)),
            name="w_in_layout",
        )(wl.T)
        mu_a = _mixer_a_layout(tmix_mu[l][None, :])
        w2p = jnp.pad(w2[l], ((0, LANES - D_DECAY_LORA), (0, 0))).astype(BF16)
        a2p = jnp.pad(a2[l], ((0, LANES - D_AAA_LORA), (0, 0))).astype(BF16)
        g2p = jnp.pad(g2[l], ((0, LORA_PAD - XG_OFF - D_GATE_LORA), (0, 0))).astype(BF16)
        g1 = norm1_g[l][None, :]

        row512 = lambda i: (i, 0)
        tok_a = pl.BlockSpec((tm_a, A_WIDTH), row512)
        tok_d = pl.BlockSpec((tm_a, d), row512)
        vec_a = _full((1, A_WIDTH))
        bs_full = jnp.repeat(b_s[l].T, B_GROUP_CH, axis=1)
        outs = pl.pallas_call(
            functools.partial(_inproj_kernel, seq // tm_a),
            grid=(n_tok // tm_a,),
            in_specs=[tok_d,
                      _full((1, d)), _full((A_PROJ, d)), _full((1, A_PROJ)), vec_a,
                      _full((LANES, A_WIDTH)), vec_a, _full((LANES, A_WIDTH)),
                      _full((LORA_PAD - XG_OFF, A_WIDTH)), vec_a, vec_a,
                      _full((A_WIDTH, A_WIDTH)),
                      _full((b_cols, d)), _full((2 * d, d)), _full((1, 2 * d)),
                      _full((1, B_WIDTH)), _full((1, B_WIDTH)),
                      _full((B_GROUPS, GMLP_BLOCK, GMLP_BLOCK)), _full((GMLP_BLOCK, B_WIDTH)),
                      _full((B_WIDTH, d))],
            out_specs=[tok_a] * 7 + [tok_d, tok_d],
            out_shape=[jax.ShapeDtypeStruct((n_tok, A_WIDTH), F32)] * 7
            + [jax.ShapeDtypeStruct((n_tok, d), BF16)] * 2,
            scratch_shapes=[pltpu.VMEM((8, A_PROJ), F32)],
            compiler_params=_params(("arbitrary",)),
            name="inproj",
        )(xf, g1, w_a, mu_a, w0[l][None, :], w2p, a0[l][None, :], a2p, g2p,
          k_k[l][None, :], k_a[l][None, :], ones_bd, w_b, w_g, b_gate[l][None, :],
          lnv_g[l][None, :], lnv_b[l][None, :], w_s[l], bs_full, w_oB[l].astype(BF16))
        r_, k_, v_, a_, b_, ld_, gg_, ybg, ga = outs

        n_pairs = A_WIDTH // HEAD_PAIR
        n_chunks = WKV_TOKENS // WKV_CHUNK
        tok_w = pl.BlockSpec((WKV_SEQS, WKV_TOKENS, A_WIDTH), lambda bi, ti: (bi, ti, 0))
        vec_w = _full((1, A_WIDTH))
        seq_major = lambda t: t.reshape(bsz, seq, A_WIDTH)
        tok_scratch = pltpu.VMEM((WKV_SEQS, WKV_TOKENS, A_WIDTH), BF16)
        ya_in = pl.pallas_call(
            _wkv_kernel,
            grid=(bsz // WKV_SEQS, seq // WKV_TOKENS),
            in_specs=[tok_w] * 7 + [vec_w, vec_w, vec_w, _full((HEAD_PAIR, HEAD_PAIR))],
            out_specs=tok_w,
            out_shape=jax.ShapeDtypeStruct((bsz, seq, A_WIDTH), BF16),
            scratch_shapes=[pltpu.VMEM((WKV_SEQS, n_pairs, HEAD_PAIR, HEAD_PAIR), F32)]
            + [tok_scratch] * 5
            + [pltpu.VMEM((WKV_SEQS, n_chunks, n_pairs, HEAD_PAIR, HEAD_PAIR), BF16),
               pltpu.VMEM((WKV_SEQS, n_chunks, n_pairs, HEAD_PAIR, HEAD_PAIR), F32),
               pltpu.VMEM((WKV_SEQS, WKV_TOKENS, A_WIDTH), F32)],
            compiler_params=_params(("parallel", "arbitrary")),
            name="wkv",
        )(*[seq_major(t) for t in (r_, k_, v_, a_, b_, ld_, gg_)], r_k[l].reshape(1, A_WIDTH),
          lnx_g[l][None, :], lnx_b[l][None, :],
          ones_bd[:HEAD_PAIR, :HEAD_PAIR]).reshape(n_tok, A_WIDTH)

        w_r = _place(LANES, [(0, jnp.transpose(w_re[l], (1, 0, 2)).reshape(d, N_EXPERTS)),
                             (N_EXPERTS, w_rg[l])])
        b_r = _place(LANES, [(0, b_re[l].reshape(1, N_EXPERTS)), (N_EXPERTS, b_rg[l][None, :])])
        wr_hi = w_r.astype(BF16)
        wr_lo = (w_r - wr_hi.astype(F32)).astype(BF16)
        tok_p = pl.BlockSpec((tm_p, d), row512)
        lane_p = pl.BlockSpec((tm_p, LANES), row512)
        n_sub = n_tok // MOE_SUB
        sorted_rows = tm_p // MOE_SUB * RUN_SLOTS
        x2, h_sorted, route, wts, cnt = pl.pallas_call(
            _post_kernel,
            grid=(n_tok // tm_p,),
            in_specs=[tok_p, pl.BlockSpec((tm_p, A_WIDTH), row512), tok_p, tok_p,
                      _full((A_WIDTH, d)), _full((d, d)), _full((1, d)), _full((d, 2 * LANES)),
                      _full((d, LANES)), _full((1, LANES))],
            out_specs=[tok_p, pl.BlockSpec((sorted_rows, d), row512), lane_p, lane_p,
                       pl.BlockSpec((8, LANES), row512)],
            out_shape=[jax.ShapeDtypeStruct((n_tok, d), F32),
                       jax.ShapeDtypeStruct((n_sub * RUN_SLOTS, d), BF16),
                       jax.ShapeDtypeStruct((n_tok, LANES), jnp.int32),
                       jax.ShapeDtypeStruct((n_tok, LANES), F32),
                       jax.ShapeDtypeStruct((n_tok // tm_p * 8, LANES), F32)],
            compiler_params=_params(("parallel",)),
            name="post",
        )(xf, ya_in, ga, ybg, w_oA[l].astype(BF16), w_out[l].astype(BF16),
          norm2_g[l][None, :], jnp.concatenate([wr_hi, wr_lo], axis=1), wr_hi, b_r)

        run_chunks = cnt.reshape(n_tok // tm_p, 8, LANES)[:, :tm_p // MOE_SUB, :N_EXPERTS]
        run_chunks = run_chunks.reshape(n_sub, N_EXPERTS).astype(jnp.int32)
        run_start = (jnp.cumsum(run_chunks, axis=1) - run_chunks) * RUN_ALIGN
        run_row = (jnp.arange(n_sub, dtype=jnp.int32)[:, None] * RUN_SLOTS + run_start).T.reshape(-1)
        run_n = run_chunks.T.reshape(-1)
        run_end = jnp.cumsum(run_n)
        first_chunk = jnp.concatenate([jnp.zeros((1,), jnp.int32), run_end[n_sub - 1::n_sub]])
        pos = jnp.arange(n_sub * RUN_SLOTS // RUN_ALIGN, dtype=jnp.int32)
        run_of = jnp.sum(pos[:, None] >= run_end[None, :], axis=1)
        in_run = run_of[:, None] == jnp.arange(run_n.shape[0], dtype=jnp.int32)[None, :]
        pick = lambda t: jnp.sum(jnp.where(in_run, t[None, :], 0), axis=1)
        chunk_rows = pick(run_row) + (pos - pick(run_end - run_n)) * RUN_ALIGN

        per_expert = lambda e, *_: (e, 0, 0)
        y_sorted = pl.pallas_call(
            _expert_kernel,
            grid_spec=pltpu.PrefetchScalarGridSpec(
                num_scalar_prefetch=2, grid=(N_EXPERTS,),
                in_specs=[pl.BlockSpec(memory_space=pl.ANY),
                          pl.BlockSpec((1, d, D_EXPERT), per_expert),
                          pl.BlockSpec((1, d, D_EXPERT), per_expert),
                          pl.BlockSpec((1, D_EXPERT, d), per_expert)],
                out_specs=pl.BlockSpec(memory_space=pl.ANY),
                scratch_shapes=[pltpu.VMEM((2, EXPERT_ROWS, d), BF16),
                                pltpu.VMEM((d, 2 * D_EXPERT), BF16),
                                pltpu.VMEM((D_EXPERT, d), BF16),
                                pltpu.SemaphoreType.DMA((4,)),
                                pltpu.SMEM((1,), jnp.int32)]),
            out_shape=jax.ShapeDtypeStruct((n_sub * RUN_SLOTS, d), BF16),
            input_output_aliases={2: 0},
            compiler_params=_params(("arbitrary",)),
            name="moe_experts",
        )(chunk_rows, first_chunk, h_sorted, w_e_gate[l], w_e_up[l], w_e_down[l])

        xf = pl.pallas_call(
            _combine_kernel,
            grid=(n_tok // tm_p,),
            in_specs=[tok_p, lane_p, lane_p, _full((1, d)),
                      pl.BlockSpec((sorted_rows, d), row512)],
            out_specs=tok_p,
            out_shape=jax.ShapeDtypeStruct((n_tok, d), F32),
            compiler_params=_params(("parallel",)),
            name="moe_combine",
        )(x2, route, wts, final_g[None, :], y_sorted)

    return xf.reshape(bsz, seq, d)
```

```python
import functools

import jax
import jax.numpy as jnp
from jax import lax
from jax.experimental import pallas as pl
from jax.experimental.pallas import tpu as pltpu

F32 = jnp.float32
BF16 = jnp.bfloat16

D_MODEL = 1024
A_WIDTH = 512
A_HEAD = 64
D_DECAY_LORA = 64
D_AAA_LORA = 64
D_GATE_LORA = 160
B_WIDTH = 512
B_GROUPS = 4
B_GROUP_CH = 128
GMLP_BLOCK = 128
N_GROUPS = 4
EXPERTS_PER_GROUP = 8
N_EXPERTS = 32
D_EXPERT = 256
NORM_EPS = 1e-6
LN_EPS = 1e-5
LNX_EPS = 64e-5

LANES = 128
LORA_PAD = 512
XW_OFF, XA_OFF, XG_OFF = 0, 128, 256
A_PROJ = 3 * A_WIDTH + LORA_PAD
WKV_CHUNK = 64
HEAD_PAIR = 2 * A_HEAD
WKV_SEQS = 4
WKV_TOKENS = 128
WKV_PREP_GROUP = 4
WKV_NORM_GROUP = 2
MOE_SUB = 256
RUN_ALIGN = 16
RUN_SLOTS = 2 * MOE_SUB + N_EXPERTS * RUN_ALIGN
EXPERT_ROWS = 2048
EXPERT_TILE = 512
DENSE_SPLIT = 256
CHUNK_UNROLL = 4
VMEM_LIMIT = 48 * 1024 * 1024


def _rms(x, g):
    return x * lax.rsqrt(jnp.mean(x * x, axis=-1, keepdims=True) + NORM_EPS) * g


def _dot(a, b):
    return jnp.dot(a, b, preferred_element_type=F32)


def _dot_nt(a, b):
    return lax.dot_general(a, b, (((1,), (1,)), ((), ())), preferred_element_type=F32)


def _split2(x):
    hi = x.astype(BF16)
    lo = (x - hi.astype(F32)).astype(BF16)
    return hi, lo


def _split3(x):
    hi = x.astype(BF16)
    r1 = x - hi.astype(F32)
    mid = r1.astype(BF16)
    lo = (r1 - mid.astype(F32)).astype(BF16)
    return hi, mid, lo


def _pair_head_sums(xs, ones_pair):
    n_tiles = A_WIDTH // HEAD_PAIR
    rows = xs[0].shape[0]
    tiles = [x[:, t * HEAD_PAIR:(t + 1) * HEAD_PAIR].astype(BF16) for x in xs for t in range(n_tiles)]
    sums = _dot(jnp.concatenate(tiles, axis=0), ones_pair)
    return [jnp.concatenate([sums[(i * n_tiles + t) * rows:(i * n_tiles + t + 1) * rows]
                             for t in range(n_tiles)], axis=1) for i in range(len(xs))]


def _w_in_kernel(w_ref, wa_out, wb_out, wg_out):
    s_rkv = 3 * A_WIDTH
    s_w = s_rkv + D_DECAY_LORA
    s_a = s_w + D_AAA_LORA
    a_cols = s_a + D_GATE_LORA
    b_end = a_cols + 2 * B_WIDTH
    wa_out[...] = jnp.zeros_like(wa_out)
    wa_out[:s_w, :] = w_ref[:s_w, :].astype(BF16)
    wa_out[s_rkv + XA_OFF:s_rkv + XA_OFF + D_AAA_LORA, :] = w_ref[s_w:s_a, :].astype(BF16)
    wa_out[s_rkv + XG_OFF:s_rkv + XG_OFF + D_GATE_LORA, :] = w_ref[s_a:a_cols, :].astype(BF16)
    wb_out[...] = w_ref[a_cols:b_end, :].astype(BF16)
    wg_out[...] = w_ref[b_end:, :].astype(BF16)


def _inproj_kernel(tiles_per_seq, x_ref, g1_ref, wa_ref, mu_ref, w0_ref, w2_ref, a0_ref, a2_ref,
                   g2_ref, kk_ref, ka_ref, ones_ref, wb_ref, wg_ref, bg_ref, lng_ref, lnb_ref,
                   ws_ref, bs_ref, wo_ref,
                   r_out, k_out, v_out, a_out, b_out, ld_out, g_out, ybg_out, ga_out, tail_ref):
    i = pl.program_id(0)
    tm = x_ref.shape[0]
    parts = [slice(j * DENSE_SPLIT, (j + 1) * DENSE_SPLIT) for j in range(tm // DENSE_SPLIT)]
    g1 = g1_ref[...]
    h = [_rms(x_ref[p, :], g1).astype(BF16) for p in parts]
    proj = [_dot_nt(hh, wa_ref[...]) for hh in h]
    pb = [_dot_nt(hh, wb_ref[...]) for hh in h]

    prev = [jnp.where(i % tiles_per_seq == 0, 0.0, tail_ref[7:8, :])]
    prev += [p[DENSE_SPLIT - 1:, :] for p in proj[:-1]]
    tail_ref[...] = proj[-1][DENSE_SPLIT - 8:, :]
    row = lax.broadcasted_iota(jnp.int32, proj[0].shape, 0)
    pm = []
    for p, pv in zip(proj, prev):
        shifted = jnp.where(row == 0, pv, pltpu.roll(p, 1, axis=0))
        pm.append(p + mu_ref[...] * (shifted - p))
    gates = [jax.nn.sigmoid(_dot_nt(hh, wg_ref[...]) + bg_ref[...]) for hh in h]

    tri = (lax.broadcasted_iota(jnp.int32, (GMLP_BLOCK, GMLP_BLOCK), 0)
           >= lax.broadcasted_iota(jnp.int32, (GMLP_BLOCK, GMLP_BLOCK), 1))
    ws = [jnp.where(tri, ws_ref[grp], 0.0).astype(BF16) for grp in range(B_GROUPS)]
    bs = bs_ref[...]
    us, vns = [], []
    for x in pb:
        z = 0.5 * x * (1.0 + lax.erf(x * (2.0 ** -0.5)))
        us.append(z[:, :B_WIDTH])
        v = z[:, B_WIDTH:]
        mean = jnp.mean(v, axis=-1, keepdims=True)
        vc = v - mean
        var = jnp.mean(vc * vc, axis=-1, keepdims=True)
        vns.append((vc * lax.rsqrt(var + LN_EPS) * lng_ref[...] + lnb_ref[...]).astype(BF16))

    lora = [x[:, 3 * A_WIDTH:] for x in pm]
    decay = [_dot(jnp.tanh(x[:, XW_OFF:XW_OFF + LANES]).astype(BF16), w2_ref[...]) for x in lora]
    rate = [_dot(x[:, XA_OFF:XA_OFF + LANES].astype(BF16), a2_ref[...]) for x in lora]
    gate = [_dot(jax.nn.sigmoid(x[:, XG_OFF:]).astype(BF16), g2_ref[...]) for x in lora]
    kks = [x[:, A_WIDTH:2 * A_WIDTH] * kk_ref[...] for x in pm]
    sq = [_dot((kk * kk).astype(BF16), ones_ref[...]) for kk in kks]

    svs = []
    for vn in vns:
        rows = []
        for blk in range(DENSE_SPLIT // GMLP_BLOCK):
            cols = [_dot(ws[grp], vn[blk * GMLP_BLOCK:(blk + 1) * GMLP_BLOCK,
                                     grp * B_GROUP_CH:(grp + 1) * B_GROUP_CH])
                    for grp in range(B_GROUPS)]
            rows.append(jnp.concatenate(cols, axis=1) + bs)
        svs.append(jnp.concatenate(rows, axis=0))
    yb = [_dot((u * sv).astype(BF16), wo_ref[...]) for u, sv in zip(us, svs)]

    for j, p in enumerate(parts):
        k = pm[j][:, A_WIDTH:2 * A_WIDTH]
        z = -(w0_ref[...] + decay[j])
        softplus = jnp.maximum(z, 0.0) + jnp.log(1.0 + jnp.exp(-jnp.abs(z)))
        w = -softplus - 0.5
        a_lr = jax.nn.sigmoid(a0_ref[...] + rate[j])
        kk = kks[j] / jnp.maximum(jnp.sqrt(sq[j]), 1e-12)
        r_out[p, :] = pm[j][:, 0:A_WIDTH]
        k_out[p, :] = k * (1.0 + (a_lr - 1.0) * ka_ref[...])
        v_out[p, :] = pm[j][:, 2 * A_WIDTH:3 * A_WIDTH]
        a_out[p, :] = -kk
        b_out[p, :] = kk * a_lr
        ld_out[p, :] = -jnp.exp(w)
        g_out[p, :] = gate[j]
        ga_out[p, :] = gates[j][:, :D_MODEL].astype(BF16)
        ybg_out[p, :] = (gates[j][:, D_MODEL:] * yb[j]).astype(BF16)


def _wkv_kernel(r_ref, k_ref, v_ref, a_ref, b_ref, ld_ref, g_ref, rk_ref, lng_ref, lnb_ref,
                ones_ref, o_ref, st_ref, ta_s, tl_s, arb_s, ark_s, rt_s, bkt_s, dcol_s, y_s):
    C = WKV_CHUNK
    bb, tb, _ = r_ref.shape
    n_chunks = tb // C
    n_pairs = A_WIDTH // HEAD_PAIR

    @pl.when(pl.program_id(1) == 0)
    def _():
        st_ref[...] = jnp.zeros_like(st_ref)

    row = lax.broadcasted_iota(jnp.int32, (C, HEAD_PAIR), 0)
    src = lax.broadcasted_iota(jnp.int32, (C, HEAD_PAIR), 1) & (C - 1)
    incl = src <= row
    strict = src < row
    eye_pair = jnp.where(src == row, 1.0, 0.0)
    bd_mask = ((lax.broadcasted_iota(jnp.int32, (HEAD_PAIR, HEAD_PAIR), 0) >= A_HEAD)
               == (lax.broadcasted_iota(jnp.int32, (HEAD_PAIR, HEAD_PAIR), 1) >= A_HEAD))
    tri_c = (lax.broadcasted_iota(jnp.int32, (C, C), 0)
             >= lax.broadcasted_iota(jnp.int32, (C, C), 1)).astype(BF16)
    ones_pair = ones_ref[...]
    pair_cols = [slice(p * HEAD_PAIR, (p + 1) * HEAD_PAIR) for p in range(n_pairs)]

    def bd(x):
        xb = x.astype(BF16)
        return jnp.where(bd_mask, jnp.concatenate([xb, xb], axis=0), jnp.zeros((), BF16))

    def prep(it, carry):
        where, lhs, rhs, ats = [], [], [], []
        for j in range(WKV_PREP_GROUP):
            flat = it * WKV_PREP_GROUP + j
            b = flat // n_chunks
            c = flat % n_chunks
            sl = pl.ds(pl.multiple_of(c * C, C), C)
            ld = ld_ref[b, sl, :]
            hi, mid, lo = _split3(ld)
            cl = _dot(tri_c, hi) + _dot(tri_c, mid) + _dot(tri_c, lo)
            cl_end = cl[C - 1:C, :]
            k = k_ref[b, sl, :]
            bv = b_ref[b, sl, :]
            d_inv = jnp.exp(-cl)
            d_tail = jnp.exp(cl_end - cl)
            rt = (r_ref[b, sl, :] * jnp.exp(cl)).astype(BF16)
            rt_s[b, sl, :] = rt
            kt = k * d_inv
            bt = bv * d_inv
            at = a_ref[b, sl, :] * jnp.exp(cl - ld)
            kd = k * d_tail
            bdk = bv * d_tail
            d_end = jnp.exp(cl_end)
            for p, cs in enumerate(pair_cols):
                bkt_s[b, c, p] = jnp.concatenate([bdk[:, cs], kd[:, cs]], axis=0).T.astype(BF16)
                dcol_s[b, c, p] = jnp.broadcast_to(d_end[:, cs], (HEAD_PAIR, HEAD_PAIR)).T
                where.append((b, sl, cs))
                ats.append(at[:, cs])
                lhs.append(jnp.concatenate([rt[:, cs], at[:, cs].astype(BF16)], axis=0))
                rhs.append(jnp.concatenate([bd(kt[:, cs]), bd(bt[:, cs])], axis=0))
        n = len(where)
        amat = [_dot_nt(lhs[i], rhs[i]) for i in range(n)]
        l_ak, l_ab = [], []
        for i, (b, sl, cs) in enumerate(where):
            ark_s[b, sl, cs] = jnp.where(incl, amat[i][:C, :HEAD_PAIR], 0.0).astype(BF16)
            arb_s[b, sl, cs] = jnp.where(incl, amat[i][:C, HEAD_PAIR:], 0.0).astype(BF16)
            l_ak.append(jnp.where(strict, amat[i][C:, :HEAD_PAIR], 0.0))
            l_ab.append(jnp.where(strict, amat[i][C:, HEAD_PAIR:], 0.0))

        t_mat = [eye_pair + l for l in l_ab]
        q = [_dot(l.astype(BF16), bd(l)) for l in l_ab]
        n_sq = 1
        while 2 * n_sq < C // 2:
            both = [_dot(q[i].astype(BF16), jnp.concatenate([bd(q[i]), bd(t_mat[i])], axis=1))
                    for i in range(n)]
            q = [x[:, :HEAD_PAIR] for x in both]
            t_mat = [t_mat[i] + both[i][:, HEAD_PAIR:] for i in range(n)]
            n_sq *= 2
        t_mat = [t_mat[i] + _dot(q[i].astype(BF16), bd(t_mat[i])) for i in range(n)]
        tal = [_dot(t_mat[i].astype(BF16), jnp.concatenate([bd(ats[i]), bd(l_ak[i])], axis=1))
               for i in range(n)]
        for i, (b, sl, cs) in enumerate(where):
            ta_s[b, sl, cs] = tal[i][:, :HEAD_PAIR].astype(BF16)
            tl_s[b, sl, cs] = tal[i][:, HEAD_PAIR:].astype(BF16)
        return carry

    lax.fori_loop(0, bb * n_chunks // WKV_PREP_GROUP, prep, 0)

    def step(c, carry):
        sl = pl.ds(pl.multiple_of(c * C, C), C)
        chains = [(b, p) for b in range(bb) for p in range(n_pairs)]
        v = [v_ref[b, sl, :] for b in range(bb)]
        st = [st_ref[b, p] for b, p in chains]
        st_b = [x.astype(BF16) for x in st]
        bd_v = [bd(v[b][:, pair_cols[p]]) for b, p in chains]
        on_st = [_dot(jnp.concatenate([ta_s[b, sl, pair_cols[p]], rt_s[b, sl, pair_cols[p]]],
                                      axis=0), st_b[i]) for i, (b, p) in enumerate(chains)]
        on_v = [_dot(jnp.concatenate([tl_s[b, sl, pair_cols[p]], ark_s[b, sl, pair_cols[p]]],
                                     axis=0), bd_v[i]) for i, (b, p) in enumerate(chains)]
        u = [on_st[i][:C] + on_v[i][:C] for i in range(len(chains))]
        for i, (b, p) in enumerate(chains):
            uv = jnp.concatenate([u[i], v[b][:, pair_cols[p]]], axis=0).astype(BF16)
            st_ref[b, p] = (dcol_s[b, c, p] * st[i]
                            + jnp.where(bd_mask, _dot(bkt_s[b, c, p], uv), 0.0))
        ys = [on_st[i][C:] + on_v[i][C:] + _dot(arb_s[b, sl, pair_cols[p]], bd(u[i]))
              for i, (b, p) in enumerate(chains)]
        for b in range(bb):
            y_s[b, sl, :] = jnp.concatenate(ys[b * n_pairs:(b + 1) * n_pairs], axis=1)
        return carry

    lax.fori_loop(0, n_chunks, step, 0)

    def finish(it, carry):
        items = [(b, pl.ds(pl.multiple_of((it * WKV_NORM_GROUP + j) * C, C), C))
                 for j in range(WKV_NORM_GROUP) for b in range(bb)]
        y = [y_s[b, sl, :] for b, sl in items]
        rkr = [r_ref[b, sl, :] * k_ref[b, sl, :] * rk_ref[...] for b, sl in items]
        sums = [_pair_head_sums([y[i], rkr[i]], ones_pair) for i in range(len(items))]
        yc = [y[i] - sums[i][0] * (1.0 / A_HEAD) for i in range(len(items))]
        var = [_pair_head_sums([x * x], ones_pair)[0] * (1.0 / A_HEAD) for x in yc]
        for i, (b, sl) in enumerate(items):
            yn = yc[i] * lax.rsqrt(var[i] + LNX_EPS) * lng_ref[...] + lnb_ref[...]
            out = (yn + sums[i][1] * v_ref[b, sl, :]) * g_ref[b, sl, :]
            o_ref[b, sl, :] = out.astype(o_ref.dtype)
        return carry

    lax.fori_loop(0, n_chunks // WKV_NORM_GROUP, finish, 0)


def _post_kernel(x_ref, ya_ref, ga_ref, ybg_ref, woa_ref, wout_ref, g2_ref, wrt_ref, brt_ref,
                 x2_out, hl_out, route_out, cnt_out):
    n_sub = x_ref.shape[0] // MOE_SUB
    groups = [slice(g * MOE_SUB, (g + 1) * MOE_SUB) for g in range(n_sub)]
    y_a = [_dot(ya_ref[p, :].astype(BF16), woa_ref[...]) for p in groups]
    x2 = [x_ref[p, :] + _dot((ga_ref[p, :] * y + ybg_ref[p, :]).astype(BF16), wout_ref[...])
          for p, y in zip(groups, y_a)]
    h2 = [_rms(x, g2_ref[...]) for x in x2]
    split = [_split2(h) for h in h2]
    on_hi = [_dot_nt(wrt_ref[...], hi) for hi, _ in split]
    on_lo = [_dot_nt(wrt_ref[0:LANES, :], lo) for _, lo in split]
    row = lax.broadcasted_iota(jnp.int32, (LANES, MOE_SUB), 0)
    neg = jnp.float32(-jnp.inf)
    big = jnp.int32(LANES)
    is_grp = (row >= N_EXPERTS) & (row < N_EXPERTS + N_GROUPS)
    top = lambda v: jnp.max(v, axis=0, keepdims=True)
    first = lambda hit: jnp.min(jnp.where(hit, row, big), axis=0, keepdims=True)
    picks, onehots, weights = [], [], []
    for g, p in enumerate(groups):
        x2_out[p, :] = x2[g].astype(x2_out.dtype)
        logits = on_hi[g][:LANES, :] + on_hi[g][LANES:, :] + on_lo[g] + brt_ref[...]
        gl = jnp.where(is_grp, logits, neg)
        gmax = top(gl)
        g_p = 1.0 / jnp.sum(jnp.exp(gl - gmax), axis=0, keepdims=True)
        lo_row = (first(gl == gmax) - N_EXPERTS) * EXPERTS_PER_GROUP
        el = jnp.where((row >= lo_row) & (row < lo_row + EXPERTS_PER_GROUP), logits, neg)
        e1 = top(el)
        i1 = first(el == e1)
        el2 = jnp.where(row == i1, neg, el)
        e2 = top(el2)
        i2 = first(el2 == e2)
        t = jnp.exp(e2 - e1)
        weights.append((g_p / (1.0 + t), g_p * t / (1.0 + t)))
        picks.append((i1, i2))
        onehots.append(jnp.where((row == i1) | (row == i2), 1.0, 0.0))

    earlier = (lax.broadcasted_iota(jnp.int32, (MOE_SUB, MOE_SUB), 0)
               <= lax.broadcasted_iota(jnp.int32, (MOE_SUB, MOE_SUB), 1)).astype(BF16)
    incl = [_dot(oh.astype(BF16), earlier) for oh in onehots]
    chunks = [jnp.ceil(x[:, MOE_SUB - 1:] * (1.0 / RUN_ALIGN)) for x in incl]
    below = (lax.broadcasted_iota(jnp.int32, (LANES, LANES), 1)
             < lax.broadcasted_iota(jnp.int32, (LANES, LANES), 0)).astype(BF16)
    wide = [jnp.broadcast_to(c, (LANES, LANES)) for c in chunks]
    run_start = [_dot(below, w.astype(BF16))[:, 0:1] * RUN_ALIGN for w in wide]
    row8 = lax.broadcasted_iota(jnp.int32, (8, LANES), 0)
    counts = jnp.zeros((8, LANES), F32)
    slot_ids = lax.broadcasted_iota(jnp.int32, (RUN_SLOTS, MOE_SUB), 0)
    for g, p in enumerate(groups):
        counts = jnp.where(row8 == g, wide[g].T[0:8, :], counts)
        slot_of = run_start[g] + incl[g] - onehots[g]
        i1, i2 = picks[g]
        slot1 = jnp.sum(jnp.where(row == i1, slot_of, 0.0), axis=0, keepdims=True)
        slot2 = jnp.sum(jnp.where(row == i2, slot_of, 0.0), axis=0, keepdims=True)
        w1, w2 = weights[g]
        packed = jnp.where(row == 0, slot1, jnp.where(row == 1, slot2, jnp.where(
            row == 2, w1, jnp.where(row == 3, w2, 0.0))))
        route_out[p, :] = packed.T
        pick = jnp.where((slot_ids == slot1.astype(jnp.int32))
                         | (slot_ids == slot2.astype(jnp.int32)), 1.0, 0.0).astype(BF16)
        hl_out[g * RUN_SLOTS:(g + 1) * RUN_SLOTS, :] = _dot(pick, h2[g].astype(BF16)).astype(BF16)
    cnt_out[...] = counts


def _expert_kernel(chunk_s, first_s, hl_in_ref, wg_ref, wu_ref, wd_ref, hl_ref, xbuf, wgu_s, wd_s,
                   sem, drain_s):
    del hl_in_ref
    e = pl.program_id(0)
    pass_chunks = EXPERT_ROWS // RUN_ALIGN
    mine = e % 2
    other = 1 - mine
    gather_sem = lambda buf: sem.at[buf]
    scatter_sem = lambda buf: sem.at[2 + buf]

    def first_pass(ex):
        lo = first_s[ex]
        return lo, jnp.minimum(lo + pass_chunks, first_s[ex + 1])

    def for_chunks(buf, c_lo, c_hi, fn):
        def one(c):
            fn(hl_ref.at[pl.ds(pl.multiple_of(chunk_s[c], RUN_ALIGN), RUN_ALIGN)],
               xbuf.at[buf, pl.ds(pl.multiple_of((c - c_lo) * RUN_ALIGN, RUN_ALIGN), RUN_ALIGN)])

        def block(i, carry):
            for u in range(CHUNK_UNROLL):
                one(c_lo + i * CHUNK_UNROLL + u)
            return carry

        def tail(c, carry):
            one(c)
            return carry

        n_blocks = (c_hi - c_lo) // CHUNK_UNROLL
        lax.fori_loop(0, n_blocks, block, 0)
        lax.fori_loop(c_lo + n_blocks * CHUNK_UNROLL, c_hi, tail, 0)

    def gather(buf, c_lo, c_hi):
        for_chunks(buf, c_lo, c_hi, lambda hbm, vm: pltpu.make_async_copy(
            hbm, vm, gather_sem(buf)).start())

    def scatter(buf, c_lo, c_hi):
        for_chunks(buf, c_lo, c_hi, lambda hbm, vm: pltpu.make_async_copy(
            vm, hbm, scatter_sem(buf)).start())

    def wait_chunks(which_sem, n):
        def wait_rows(rows):
            def body(c, carry):
                pltpu.make_async_copy(hl_ref.at[pl.ds(0, rows)], xbuf.at[0, pl.ds(0, rows)],
                                      which_sem).wait()
                return carry
            return body

        lax.fori_loop(0, n // CHUNK_UNROLL, wait_rows(CHUNK_UNROLL * RUN_ALIGN), 0)
        lax.fori_loop(0, n % CHUNK_UNROLL, wait_rows(RUN_ALIGN), 0)

    def evaluate(buf, n_chunks):
        def tile(k, carry):
            sl = pl.ds(pl.multiple_of(k * EXPERT_TILE, EXPERT_TILE), EXPERT_TILE)
            gu = _dot(xbuf[buf, sl, :], wgu_s[...])
            gate = gu[:, :D_EXPERT]
            act = gate * jax.nn.sigmoid(gate) * gu[:, D_EXPERT:]
            xbuf[buf, sl, :] = _dot(act.astype(BF16), wd_s[...]).astype(BF16)
            return carry

        lax.fori_loop(0, (n_chunks * RUN_ALIGN + EXPERT_TILE - 1) // EXPERT_TILE, tile, 0)

    @pl.when(e == 0)
    def _():
        xbuf[...] = jnp.zeros_like(xbuf)
        gather(0, *first_pass(0))
        drain_s[0] = 0

    wgu_s[:, :D_EXPERT] = wg_ref[0].astype(BF16)
    wgu_s[:, D_EXPERT:] = wu_ref[0].astype(BF16)
    wd_s[...] = wd_ref[0].astype(BF16)

    c_lo, c_hi = first_pass(e)
    wait_chunks(gather_sem(mine), c_hi - c_lo)
    wait_chunks(scatter_sem(other), drain_s[0])

    @pl.when(e + 1 < pl.num_programs(0))
    def _():
        gather(other, *first_pass(e + 1))

    evaluate(mine, c_hi - c_lo)
    scatter(mine, c_lo, c_hi)

    def later_pass(state):
        lo, pending = state
        hi = jnp.minimum(lo + pass_chunks, first_s[e + 1])
        wait_chunks(scatter_sem(mine), pending)
        gather(mine, lo, hi)
        wait_chunks(gather_sem(mine), hi - lo)
        evaluate(mine, hi - lo)
        scatter(mine, lo, hi)
        return hi, hi - lo

    _, pending = lax.while_loop(lambda st: st[0] < first_s[e + 1], later_pass,
                                (c_hi, c_hi - c_lo))
    drain_s[0] = pending

    @pl.when(e + 1 == pl.num_programs(0))
    def _():
        wait_chunks(scatter_sem(mine), pending)


def _combine_kernel(x2_ref, route_ref, gf_ref, yl_ref, o_ref):
    slots = lax.broadcasted_iota(jnp.int32, (MOE_SUB, RUN_SLOTS), 1).astype(F32)
    for g in range(x2_ref.shape[0] // MOE_SUB):
        p = slice(g * MOE_SUB, (g + 1) * MOE_SUB)
        route = route_ref[p, :]
        pick = (jnp.where(slots == route[:, 0:1], route[:, 2:3], 0.0)
                + jnp.where(slots == route[:, 1:2], route[:, 3:4], 0.0))
        y = _dot(pick.astype(BF16), yl_ref[g * RUN_SLOTS:(g + 1) * RUN_SLOTS, :])
        o_ref[p, :] = _rms(x2_ref[p, :] + y, gf_ref[...])


def _full(shape):
    return pl.BlockSpec(shape, lambda *_: (0,) * len(shape))


def _params(sem):
    return pltpu.CompilerParams(dimension_semantics=sem, vmem_limit_bytes=VMEM_LIMIT)


def _mixer_a_layout(t):
    s_rkv = 3 * A_WIDTH
    s_w = s_rkv + D_DECAY_LORA
    s_a = s_w + D_AAA_LORA
    gap = lambda n: jnp.zeros((t.shape[0], n), t.dtype)
    return jnp.concatenate(
        [t[:, :s_w], gap(XA_OFF - D_DECAY_LORA), t[:, s_w:s_a], gap(XG_OFF - XA_OFF - D_AAA_LORA),
         t[:, s_a:], gap(LORA_PAD - XG_OFF - D_GATE_LORA)], axis=1)


def _place(cols, parts):
    out = jnp.zeros((parts[0][1].shape[0], cols), parts[0][1].dtype)
    for off, arr in parts:
        out = lax.dynamic_update_slice(out, arr, (0, off))
    return out


def kernel(x, norm1_g, w_in, b_gate, tmix_mu, w0, w2, a0, a2, g2, k_k, k_a, r_k, lnx_g, lnx_b,
           w_oA, lnv_g, lnv_b, w_s, b_s, w_oB, w_out, norm2_g, w_rg, b_rg, w_re, b_re,
           w_e_gate, w_e_up, w_e_down, final_g):
    bsz, seq, d = x.shape
    n_tok = bsz * seq
    depth = norm1_g.shape[0]
    assert depth == 1, "the moe kernel fuses the final norm, so it must be the last layer"
    assert bsz % WKV_SEQS == 0 and seq % WKV_TOKENS == 0
    xf = x.reshape(n_tok, d)

    s_rkv = 3 * A_WIDTH
    s_w = s_rkv + D_DECAY_LORA
    s_a = s_w + D_AAA_LORA
    a_cols = s_a + D_GATE_LORA
    b_cols = 2 * B_WIDTH

    ones_bd = (jnp.arange(A_WIDTH)[:, None] // A_HEAD
               == jnp.arange(A_WIDTH)[None, :] // A_HEAD).astype(BF16)

    tm_a = 512
    tm_p = 512

    for l in range(depth):
        wl = w_in[l]
        rows_w = 256
        w_a, w_b, w_g = pl.pallas_call(
            _w_in_kernel,
            grid=(d // rows_w,),
            in_specs=[pl.BlockSpec((wl.shape[1], rows_w), lambda i: (0, i))],
            out_specs=[pl.BlockSpec((A_PROJ, rows_w), lambda i: (0, i)),
                       pl.BlockSpec((b_cols, rows_w), lambda i: (0, i)),
                       pl.BlockSpec((2 * d, rows_w), lambda i: (0, i))],
            out_shape=[jax.ShapeDtypeStruct((A_PROJ, d), BF16),
                       jax.ShapeDtypeStruct((b_cols, d), BF16),
                       jax.ShapeDtypeStruct((2 * d, d), BF16)],
            compiler_params=_params(("parallel",)),
            name="w_in_layout",
        )(wl.T)
        mu_a = _mixer_a_layout(tmix_mu[l][None, :])
        w2p = jnp.pad(w2[l], ((0, LANES - D_DECAY_LORA), (0, 0))).astype(BF16)
        a2p = jnp.pad(a2[l], ((0, LANES - D_AAA_LORA), (0, 0))).astype(BF16)
        g2p = jnp.pad(g2[l], ((0, LORA_PAD - XG_OFF - D_GATE_LORA), (0, 0))).astype(BF16)
        g1 = norm1_g[l][None, :]

        row512 = lambda i: (i, 0)
        tok_a = pl.BlockSpec((tm_a, A_WIDTH), row512)
        tok_d = pl.BlockSpec((tm_a, d), row512)
        vec_a = _full((1, A_WIDTH))
        bs_full = jnp.repeat(b_s[l].T, B_GROUP_CH, axis=1)
        outs = pl.pallas_call(
            functools.partial(_inproj_kernel, seq // tm_a),
            grid=(n_tok // tm_a,),
            in_specs=[tok_d,
                      _full((1, d)), _full((A_PROJ, d)), _full((1, A_PROJ)), vec_a,
                      _full((LANES, A_WIDTH)), vec_a, _full((LANES, A_WIDTH)),
                      _full((LORA_PAD - XG_OFF, A_WIDTH)), vec_a, vec_a,
                      _full((A_WIDTH, A_WIDTH)),
                      _full((b_cols, d)), _full((2 * d, d)), _full((1, 2 * d)),
                      _full((1, B_WIDTH)), _full((1, B_WIDTH)),
                      _full((B_GROUPS, GMLP_BLOCK, GMLP_BLOCK)), _full((GMLP_BLOCK, B_WIDTH)),
                      _full((B_WIDTH, d))],
            out_specs=[tok_a] * 7 + [tok_d, tok_d],
            out_shape=[jax.ShapeDtypeStruct((n_tok, A_WIDTH), F32)] * 7
            + [jax.ShapeDtypeStruct((n_tok, d), BF16)] * 2,
            scratch_shapes=[pltpu.VMEM((8, A_PROJ), F32)],
            compiler_params=_params(("arbitrary",)),
            name="inproj",
        )(xf, g1, w_a, mu_a, w0[l][None, :], w2p, a0[l][None, :], a2p, g2p,
          k_k[l][None, :], k_a[l][None, :], ones_bd, w_b, w_g, b_gate[l][None, :],
          lnv_g[l][None, :], lnv_b[l][None, :], w_s[l], bs_full, w_oB[l].astype(BF16))
        r_, k_, v_, a_, b_, ld_, gg_, ybg, ga = outs

        n_pairs = A_WIDTH // HEAD_PAIR
        n_chunks = WKV_TOKENS // WKV_CHUNK
        tok_w = pl.BlockSpec((WKV_SEQS, WKV_TOKENS, A_WIDTH), lambda bi, ti: (bi, ti, 0))
        vec_w = _full((1, A_WIDTH))
        seq_major = lambda t: t.reshape(bsz, seq, A_WIDTH)
        tok_scratch = pltpu.VMEM((WKV_SEQS, WKV_TOKENS, A_WIDTH), BF16)
        ya_in = pl.pallas_call(
            _wkv_kernel,
            grid=(bsz // WKV_SEQS, seq // WKV_TOKENS),
            in_specs=[tok_w] * 7 + [vec_w, vec_w, vec_w, _full((HEAD_PAIR, HEAD_PAIR))],
            out_specs=tok_w,
            out_shape=jax.ShapeDtypeStruct((bsz, seq, A_WIDTH), BF16),
            scratch_shapes=[pltpu.VMEM((WKV_SEQS, n_pairs, HEAD_PAIR, HEAD_PAIR), F32)]
            + [tok_scratch] * 5
            + [pltpu.VMEM((WKV_SEQS, n_chunks, n_pairs, HEAD_PAIR, HEAD_PAIR), BF16),
               pltpu.VMEM((WKV_SEQS, n_chunks, n_pairs, HEAD_PAIR, HEAD_PAIR), F32),
               pltpu.VMEM((WKV_SEQS, WKV_TOKENS, A_WIDTH), F32)],
            compiler_params=_params(("parallel", "arbitrary")),
            name="wkv",
        )(*[seq_major(t) for t in (r_, k_, v_, a_, b_, ld_, gg_)], r_k[l].reshape(1, A_WIDTH),
          lnx_g[l][None, :], lnx_b[l][None, :],
          ones_bd[:HEAD_PAIR, :HEAD_PAIR]).reshape(n_tok, A_WIDTH)

        w_r = _place(LANES, [(0, jnp.transpose(w_re[l], (1, 0, 2)).reshape(d, N_EXPERTS)),
                             (N_EXPERTS, w_rg[l])])
        b_r = _place(LANES, [(0, b_re[l].reshape(1, N_EXPERTS)), (N_EXPERTS, b_rg[l][None, :])])
        wr_hi = w_r.astype(BF16)
        wr_lo = (w_r - wr_hi.astype(F32)).astype(BF16)
        wr_t = jnp.concatenate([wr_hi.T, wr_lo.T], axis=0)
        tok_p = pl.BlockSpec((tm_p, d), row512)
        lane_p = pl.BlockSpec((tm_p, LANES), row512)
        n_sub = n_tok // MOE_SUB
        sorted_rows = tm_p // MOE_SUB * RUN_SLOTS
        x2, h_sorted, route, cnt = pl.pallas_call(
            _post_kernel,
            grid=(n_tok // tm_p,),
            in_specs=[tok_p, pl.BlockSpec((tm_p, A_WIDTH), row512), tok_p, tok_p,
                      _full((A_WIDTH, d)), _full((d, d)), _full((1, d)), _full((2 * LANES, d)),
                      _full((LANES, 1))],
            out_specs=[tok_p, pl.BlockSpec((sorted_rows, d), row512), lane_p,
                       pl.BlockSpec((8, LANES), row512)],
            out_shape=[jax.ShapeDtypeStruct((n_tok, d), F32),
                       jax.ShapeDtypeStruct((n_sub * RUN_SLOTS, d), BF16),
                       jax.ShapeDtypeStruct((n_tok, LANES), F32),
                       jax.ShapeDtypeStruct((n_tok // tm_p * 8, LANES), F32)],
            compiler_params=_params(("parallel",)),
            name="post",
        )(xf, ya_in, ga, ybg, w_oA[l].astype(BF16), w_out[l].astype(BF16),
          norm2_g[l][None, :], wr_t, b_r.T)

        run_chunks = cnt.reshape(n_tok // tm_p, 8, LANES)[:, :tm_p // MOE_SUB, :N_EXPERTS]
        run_chunks = run_chunks.reshape(n_sub, N_EXPERTS).astype(jnp.int32)
        run_start = (jnp.cumsum(run_chunks, axis=1) - run_chunks) * RUN_ALIGN
        run_row = (jnp.arange(n_sub, dtype=jnp.int32)[:, None] * RUN_SLOTS + run_start).T.reshape(-1)
        run_n = run_chunks.T.reshape(-1)
        run_end = jnp.cumsum(run_n)
        first_chunk = jnp.concatenate([jnp.zeros((1,), jnp.int32), run_end[n_sub - 1::n_sub]])
        pos = jnp.arange(n_sub * RUN_SLOTS // RUN_ALIGN, dtype=jnp.int32)
        run_of = jnp.sum(pos[:, None] >= run_end[None, :], axis=1)
        in_run = run_of[:, None] == jnp.arange(run_n.shape[0], dtype=jnp.int32)[None, :]
        pick = lambda t: jnp.sum(jnp.where(in_run, t[None, :], 0), axis=1)
        chunk_rows = pick(run_row) + (pos - pick(run_end - run_n)) * RUN_ALIGN

        per_expert = lambda e, *_: (e, 0, 0)
        y_sorted = pl.pallas_call(
            _expert_kernel,
            grid_spec=pltpu.PrefetchScalarGridSpec(
                num_scalar_prefetch=2, grid=(N_EXPERTS,),
                in_specs=[pl.BlockSpec(memory_space=pl.ANY),
                          pl.BlockSpec((1, d, D_EXPERT), per_expert),
                          pl.BlockSpec((1, d, D_EXPERT), per_expert),
                          pl.BlockSpec((1, D_EXPERT, d), per_expert)],
                out_specs=pl.BlockSpec(memory_space=pl.ANY),
                scratch_shapes=[pltpu.VMEM((2, EXPERT_ROWS, d), BF16),
                                pltpu.VMEM((d, 2 * D_EXPERT), BF16),
                                pltpu.VMEM((D_EXPERT, d), BF16),
                                pltpu.SemaphoreType.DMA((4,)),
                                pltpu.SMEM((1,), jnp.int32)]),
            out_shape=jax.ShapeDtypeStruct((n_sub * RUN_SLOTS, d), BF16),
            input_output_aliases={2: 0},
            compiler_params=_params(("arbitrary",)),
            name="moe_experts",
        )(chunk_rows, first_chunk, h_sorted, w_e_gate[l], w_e_up[l], w_e_down[l])

        xf = pl.pallas_call(
            _combine_kernel,
            grid=(n_tok // tm_p,),
            in_specs=[tok_p, lane_p, _full((1, d)), pl.BlockSpec((sorted_rows, d), row512)],
            out_specs=tok_p,
            out_shape=jax.ShapeDtypeStruct((n_tok, d), F32),
            compiler_params=_params(("parallel",)),
            name="moe_combine",
        )(x2, route, final_g[None, :], y_sorted)

    return xf.reshape(bsz, seq, d)
```

```python
import functools

import jax
import jax.numpy as jnp
from jax import lax
from jax.experimental import pallas as pl
from jax.experimental.pallas import tpu as pltpu

F32 = jnp.float32
BF16 = jnp.bfloat16

D_MODEL = 1024
A_WIDTH = 512
A_HEAD = 64
D_DECAY_LORA = 64
D_AAA_LORA = 64
D_GATE_LORA = 160
B_WIDTH = 512
B_GROUPS = 4
B_GROUP_CH = 128
GMLP_BLOCK = 128
N_GROUPS = 4
EXPERTS_PER_GROUP = 8
N_EXPERTS = 32
D_EXPERT = 256
NORM_EPS = 1e-6
LN_EPS = 1e-5
LNX_EPS = 64e-5

LANES = 128
LORA_PAD = 512
XW_OFF, XA_OFF, XG_OFF = 0, 128, 256
A_PROJ = 3 * A_WIDTH + LORA_PAD
WKV_CHUNK = 64
HEAD_PAIR = 2 * A_HEAD
WKV_SEQS = 8
WKV_TOKENS = 64
WKV_PREP_GROUP = 4
WKV_NORM_GROUP = 1
MOE_SUB = 256
RUN_ALIGN = 16
RUN_SLOTS = 2 * MOE_SUB + N_EXPERTS * RUN_ALIGN
EXPERT_ROWS = 2048
EXPERT_TILE = 512
DENSE_SPLIT = 256
CHUNK_UNROLL = 4
VMEM_LIMIT = 48 * 1024 * 1024


def _rms(x, g):
    return x * lax.rsqrt(jnp.mean(x * x, axis=-1, keepdims=True) + NORM_EPS) * g


def _dot(a, b):
    return jnp.dot(a, b, preferred_element_type=F32)


def _dot_nt(a, b):
    return lax.dot_general(a, b, (((1,), (1,)), ((), ())), preferred_element_type=F32)


def _split2(x):
    hi = x.astype(BF16)
    lo = (x - hi.astype(F32)).astype(BF16)
    return hi, lo


def _split3(x):
    hi = x.astype(BF16)
    r1 = x - hi.astype(F32)
    mid = r1.astype(BF16)
    lo = (r1 - mid.astype(F32)).astype(BF16)
    return hi, mid, lo


def _pair_head_sums(xs, ones_pair):
    n_tiles = A_WIDTH // HEAD_PAIR
    rows = xs[0].shape[0]
    tiles = [x[:, t * HEAD_PAIR:(t + 1) * HEAD_PAIR].astype(BF16) for x in xs for t in range(n_tiles)]
    sums = _dot(jnp.concatenate(tiles, axis=0), ones_pair)
    return [jnp.concatenate([sums[(i * n_tiles + t) * rows:(i * n_tiles + t + 1) * rows]
                             for t in range(n_tiles)], axis=1) for i in range(len(xs))]


def _w_in_kernel(w_ref, wa_out, wb_out, wg_out):
    s_rkv = 3 * A_WIDTH
    s_w = s_rkv + D_DECAY_LORA
    s_a = s_w + D_AAA_LORA
    a_cols = s_a + D_GATE_LORA
    b_end = a_cols + 2 * B_WIDTH
    wa_out[...] = jnp.zeros_like(wa_out)
    wa_out[:s_w, :] = w_ref[:s_w, :].astype(BF16)
    wa_out[s_rkv + XA_OFF:s_rkv + XA_OFF + D_AAA_LORA, :] = w_ref[s_w:s_a, :].astype(BF16)
    wa_out[s_rkv + XG_OFF:s_rkv + XG_OFF + D_GATE_LORA, :] = w_ref[s_a:a_cols, :].astype(BF16)
    wb_out[...] = w_ref[a_cols:b_end, :].astype(BF16)
    wg_out[...] = w_ref[b_end:, :].astype(BF16)


def _inproj_kernel(tiles_per_seq, x_ref, g1_ref, wa_ref, mu_ref, w0_ref, w2_ref, a0_ref, a2_ref,
                   g2_ref, kk_ref, ka_ref, ones_ref, wb_ref, wg_ref, bg_ref, lng_ref, lnb_ref,
                   ws_ref, bs_ref, wo_ref,
                   r_out, k_out, v_out, a_out, b_out, ld_out, g_out, ybg_out, ga_out, tail_ref):
    i = pl.program_id(0)
    tm = x_ref.shape[0]
    parts = [slice(j * DENSE_SPLIT, (j + 1) * DENSE_SPLIT) for j in range(tm // DENSE_SPLIT)]
    g1 = g1_ref[...]
    h = [_rms(x_ref[p, :], g1).astype(BF16) for p in parts]
    proj = [_dot_nt(hh, wa_ref[...]) for hh in h]
    pb = [_dot_nt(hh, wb_ref[...]) for hh in h]

    prev = [jnp.where(i % tiles_per_seq == 0, 0.0, tail_ref[7:8, :])]
    prev += [p[DENSE_SPLIT - 1:, :] for p in proj[:-1]]
    tail_ref[...] = proj[-1][DENSE_SPLIT - 8:, :]
    row = lax.broadcasted_iota(jnp.int32, proj[0].shape, 0)
    pm = []
    for p, pv in zip(proj, prev):
        shifted = jnp.where(row == 0, pv, pltpu.roll(p, 1, axis=0))
        pm.append(p + mu_ref[...] * (shifted - p))
    gates = [jax.nn.sigmoid(_dot_nt(hh, wg_ref[...]) + bg_ref[...]) for hh in h]

    tri = (lax.broadcasted_iota(jnp.int32, (GMLP_BLOCK, GMLP_BLOCK), 0)
           >= lax.broadcasted_iota(jnp.int32, (GMLP_BLOCK, GMLP_BLOCK), 1))
    ws = [jnp.where(tri, ws_ref[grp], 0.0).astype(BF16) for grp in range(B_GROUPS)]
    bs = bs_ref[...]
    us, vns = [], []
    for x in pb:
        z = 0.5 * x * (1.0 + lax.erf(x * (2.0 ** -0.5)))
        us.append(z[:, :B_WIDTH])
        v = z[:, B_WIDTH:]
        mean = jnp.mean(v, axis=-1, keepdims=True)
        vc = v - mean
        var = jnp.mean(vc * vc, axis=-1, keepdims=True)
        vns.append((vc * lax.rsqrt(var + LN_EPS) * lng_ref[...] + lnb_ref[...]).astype(BF16))

    lora = [x[:, 3 * A_WIDTH:] for x in pm]
    decay = [_dot(jnp.tanh(x[:, XW_OFF:XW_OFF + LANES]).astype(BF16), w2_ref[...]) for x in lora]
    rate = [_dot(x[:, XA_OFF:XA_OFF + LANES].astype(BF16), a2_ref[...]) for x in lora]
    gate = [_dot(jax.nn.sigmoid(x[:, XG_OFF:]).astype(BF16), g2_ref[...]) for x in lora]
    kks = [x[:, A_WIDTH:2 * A_WIDTH] * kk_ref[...] for x in pm]
    sq = [_dot((kk * kk).astype(BF16), ones_ref[...]) for kk in kks]

    svs = []
    for vn in vns:
        rows = []
        for blk in range(DENSE_SPLIT // GMLP_BLOCK):
            cols = [_dot(ws[grp], vn[blk * GMLP_BLOCK:(blk + 1) * GMLP_BLOCK,
                                     grp * B_GROUP_CH:(grp + 1) * B_GROUP_CH])
                    for grp in range(B_GROUPS)]
            rows.append(jnp.concatenate(cols, axis=1) + bs)
        svs.append(jnp.concatenate(rows, axis=0))
    yb = [_dot((u * sv).astype(BF16), wo_ref[...]) for u, sv in zip(us, svs)]

    for j, p in enumerate(parts):
        k = pm[j][:, A_WIDTH:2 * A_WIDTH]
        z = -(w0_ref[...] + decay[j])
        softplus = jnp.maximum(z, 0.0) + jnp.log(1.0 + jnp.exp(-jnp.abs(z)))
        w = -softplus - 0.5
        a_lr = jax.nn.sigmoid(a0_ref[...] + rate[j])
        kk = kks[j] / jnp.maximum(jnp.sqrt(sq[j]), 1e-12)
        r_out[p, :] = pm[j][:, 0:A_WIDTH]
        k_out[p, :] = k * (1.0 + (a_lr - 1.0) * ka_ref[...])
        v_out[p, :] = pm[j][:, 2 * A_WIDTH:3 * A_WIDTH]
        a_out[p, :] = -kk
        b_out[p, :] = kk * a_lr
        ld_out[p, :] = -jnp.exp(w)
        g_out[p, :] = gate[j]
        ga_out[p, :] = gates[j][:, :D_MODEL].astype(BF16)
        ybg_out[p, :] = (gates[j][:, D_MODEL:] * yb[j]).astype(BF16)


def _wkv_kernel(r_ref, k_ref, v_ref, a_ref, b_ref, ld_ref, g_ref, rk_ref, lng_ref, lnb_ref,
                ones_ref, o_ref, st_ref, ta_s, tl_s, arb_s, ark_s, rt_s, bkt_s, dcol_s, y_s):
    C = WKV_CHUNK
    bb, tb, _ = r_ref.shape
    n_chunks = tb // C
    n_pairs = A_WIDTH // HEAD_PAIR

    @pl.when(pl.program_id(1) == 0)
    def _():
        st_ref[...] = jnp.zeros_like(st_ref)

    row = lax.broadcasted_iota(jnp.int32, (C, HEAD_PAIR), 0)
    src = lax.broadcasted_iota(jnp.int32, (C, HEAD_PAIR), 1) & (C - 1)
    incl = src <= row
    strict = src < row
    eye_pair = jnp.where(src == row, 1.0, 0.0)
    bd_mask = ((lax.broadcasted_iota(jnp.int32, (HEAD_PAIR, HEAD_PAIR), 0) >= A_HEAD)
               == (lax.broadcasted_iota(jnp.int32, (HEAD_PAIR, HEAD_PAIR), 1) >= A_HEAD))
    tri_c = (lax.broadcasted_iota(jnp.int32, (C, C), 0)
             >= lax.broadcasted_iota(jnp.int32, (C, C), 1)).astype(BF16)
    ones_pair = ones_ref[...]
    pair_cols = [slice(p * HEAD_PAIR, (p + 1) * HEAD_PAIR) for p in range(n_pairs)]

    def bd(x):
        xb = x.astype(BF16)
        return jnp.where(bd_mask, jnp.concatenate([xb, xb], axis=0), jnp.zeros((), BF16))

    def prep(it, carry):
        where, lhs, rhs, ats = [], [], [], []
        for j in range(WKV_PREP_GROUP):
            flat = it * WKV_PREP_GROUP + j
            b = flat // n_chunks
            c = flat % n_chunks
            sl = pl.ds(pl.multiple_of(c * C, C), C)
            ld = ld_ref[b, sl, :]
            hi, mid, lo = _split3(ld)
            cl = _dot(tri_c, hi) + _dot(tri_c, mid) + _dot(tri_c, lo)
            cl_end = cl[C - 1:C, :]
            k = k_ref[b, sl, :]
            bv = b_ref[b, sl, :]
            d_inv = jnp.exp(-cl)
            d_tail = jnp.exp(cl_end - cl)
            rt = (r_ref[b, sl, :] * jnp.exp(cl)).astype(BF16)
            rt_s[b, sl, :] = rt
            kt = k * d_inv
            bt = bv * d_inv
            at = a_ref[b, sl, :] * jnp.exp(cl - ld)
            kd = k * d_tail
            bdk = bv * d_tail
            d_end = jnp.exp(cl_end)
            for p, cs in enumerate(pair_cols):
                bkt_s[b, c, p] = jnp.concatenate([bdk[:, cs], kd[:, cs]], axis=0).T.astype(BF16)
                dcol_s[b, c, p] = jnp.broadcast_to(d_end[:, cs], (HEAD_PAIR, HEAD_PAIR)).T
                where.append((b, sl, cs))
                ats.append(at[:, cs])
                lhs.append(jnp.concatenate([rt[:, cs], at[:, cs].astype(BF16)], axis=0))
                rhs.append(jnp.concatenate([bd(kt[:, cs]), bd(bt[:, cs])], axis=0))
        n = len(where)
        amat = [_dot_nt(lhs[i], rhs[i]) for i in range(n)]
        l_ak, l_ab = [], []
        for i, (b, sl, cs) in enumerate(where):
            ark_s[b, sl, cs] = jnp.where(incl, amat[i][:C, :HEAD_PAIR], 0.0).astype(BF16)
            arb_s[b, sl, cs] = jnp.where(incl, amat[i][:C, HEAD_PAIR:], 0.0).astype(BF16)
            l_ak.append(jnp.where(strict, amat[i][C:, :HEAD_PAIR], 0.0))
            l_ab.append(jnp.where(strict, amat[i][C:, HEAD_PAIR:], 0.0))

        t_mat = [eye_pair + l for l in l_ab]
        q = [_dot(l.astype(BF16), bd(l)) for l in l_ab]
        n_sq = 1
        while 2 * n_sq < C // 2:
            both = [_dot(q[i].astype(BF16), jnp.concatenate([bd(q[i]), bd(t_mat[i])], axis=1))
                    for i in range(n)]
            q = [x[:, :HEAD_PAIR] for x in both]
            t_mat = [t_mat[i] + both[i][:, HEAD_PAIR:] for i in range(n)]
            n_sq *= 2
        t_mat = [t_mat[i] + _dot(q[i].astype(BF16), bd(t_mat[i])) for i in range(n)]
        tal = [_dot(t_mat[i].astype(BF16), jnp.concatenate([bd(ats[i]), bd(l_ak[i])], axis=1))
               for i in range(n)]
        for i, (b, sl, cs) in enumerate(where):
            ta_s[b, sl, cs] = tal[i][:, :HEAD_PAIR].astype(BF16)
            tl_s[b, sl, cs] = tal[i][:, HEAD_PAIR:].astype(BF16)
        return carry

    lax.fori_loop(0, bb * n_chunks // WKV_PREP_GROUP, prep, 0)

    def step(c, carry):
        sl = pl.ds(pl.multiple_of(c * C, C), C)
        chains = [(b, p) for b in range(bb) for p in range(n_pairs)]
        v = [v_ref[b, sl, :] for b in range(bb)]
        st = [st_ref[b, p] for b, p in chains]
        st_b = [x.astype(BF16) for x in st]
        bd_v = [bd(v[b][:, pair_cols[p]]) for b, p in chains]
        on_st = [_dot(jnp.concatenate([ta_s[b, sl, pair_cols[p]], rt_s[b, sl, pair_cols[p]]],
                                      axis=0), st_b[i]) for i, (b, p) in enumerate(chains)]
        on_v = [_dot(jnp.concatenate([tl_s[b, sl, pair_cols[p]], ark_s[b, sl, pair_cols[p]]],
                                     axis=0), bd_v[i]) for i, (b, p) in enumerate(chains)]
        u = [on_st[i][:C] + on_v[i][:C] for i in range(len(chains))]
        for i, (b, p) in enumerate(chains):
            uv = jnp.concatenate([u[i], v[b][:, pair_cols[p]]], axis=0).astype(BF16)
            st_ref[b, p] = (dcol_s[b, c, p] * st[i]
                            + jnp.where(bd_mask, _dot(bkt_s[b, c, p], uv), 0.0))
        ys = [on_st[i][C:] + on_v[i][C:] + _dot(arb_s[b, sl, pair_cols[p]], bd(u[i]))
              for i, (b, p) in enumerate(chains)]
        for b in range(bb):
            y_s[b, sl, :] = jnp.concatenate(ys[b * n_pairs:(b + 1) * n_pairs], axis=1)
        return carry

    lax.fori_loop(0, n_chunks, step, 0)

    def finish(it, carry):
        items = [(b, pl.ds(pl.multiple_of((it * WKV_NORM_GROUP + j) * C, C), C))
                 for j in range(WKV_NORM_GROUP) for b in range(bb)]
        y = [y_s[b, sl, :] for b, sl in items]
        rkr = [r_ref[b, sl, :] * k_ref[b, sl, :] * rk_ref[...] for b, sl in items]
        sums = [_pair_head_sums([y[i], rkr[i]], ones_pair) for i in range(len(items))]
        yc = [y[i] - sums[i][0] * (1.0 / A_HEAD) for i in range(len(items))]
        var = [_pair_head_sums([x * x], ones_pair)[0] * (1.0 / A_HEAD) for x in yc]
        for i, (b, sl) in enumerate(items):
            yn = yc[i] * lax.rsqrt(var[i] + LNX_EPS) * lng_ref[...] + lnb_ref[...]
            out = (yn + sums[i][1] * v_ref[b, sl, :]) * g_ref[b, sl, :]
            o_ref[b, sl, :] = out.astype(o_ref.dtype)
        return carry

    lax.fori_loop(0, n_chunks // WKV_NORM_GROUP, finish, 0)


def _post_kernel(x_ref, ya_ref, ga_ref, ybg_ref, woa_ref, wout_ref, g2_ref, wrt_ref, brt_ref,
                 x2_out, hl_out, route_out, cnt_out):
    n_sub = x_ref.shape[0] // MOE_SUB
    groups = [slice(g * MOE_SUB, (g + 1) * MOE_SUB) for g in range(n_sub)]
    y_a = [_dot(ya_ref[p, :].astype(BF16), woa_ref[...]) for p in groups]
    x2 = [x_ref[p, :] + _dot((ga_ref[p, :] * y + ybg_ref[p, :]).astype(BF16), wout_ref[...])
          for p, y in zip(groups, y_a)]
    h2 = [_rms(x, g2_ref[...]) for x in x2]
    split = [_split2(h) for h in h2]
    on_hi = [_dot_nt(wrt_ref[...], hi) for hi, _ in split]
    on_lo = [_dot_nt(wrt_ref[0:LANES, :], lo) for _, lo in split]
    row = lax.broadcasted_iota(jnp.int32, (LANES, MOE_SUB), 0)
    neg = jnp.float32(-jnp.inf)
    big = jnp.int32(LANES)
    is_grp = (row >= N_EXPERTS) & (row < N_EXPERTS + N_GROUPS)
    top = lambda v: jnp.max(v, axis=0, keepdims=True)
    first = lambda hit: jnp.min(jnp.where(hit, row, big), axis=0, keepdims=True)
    picks, onehots, weights = [], [], []
    for g, p in enumerate(groups):
        x2_out[p, :] = x2[g].astype(x2_out.dtype)
        logits = on_hi[g][:LANES, :] + on_hi[g][LANES:, :] + on_lo[g] + brt_ref[...]
        gl = jnp.where(is_grp, logits, neg)
        gmax = top(gl)
        g_p = 1.0 / jnp.sum(jnp.exp(gl - gmax), axis=0, keepdims=True)
        lo_row = (first(gl == gmax) - N_EXPERTS) * EXPERTS_PER_GROUP
        el = jnp.where((row >= lo_row) & (row < lo_row + EXPERTS_PER_GROUP), logits, neg)
        e1 = top(el)
        i1 = first(el == e1)
        el2 = jnp.where(row == i1, neg, el)
        e2 = top(el2)
        i2 = first(el2 == e2)
        t = jnp.exp(e2 - e1)
        weights.append((g_p / (1.0 + t), g_p * t / (1.0 + t)))
        picks.append((i1, i2))
        onehots.append(jnp.where((row == i1) | (row == i2), 1.0, 0.0))

    earlier = (lax.broadcasted_iota(jnp.int32, (MOE_SUB, MOE_SUB), 0)
               <= lax.broadcasted_iota(jnp.int32, (MOE_SUB, MOE_SUB), 1)).astype(BF16)
    incl = [_dot(oh.astype(BF16), earlier) for oh in onehots]
    chunks = [jnp.ceil(x[:, MOE_SUB - 1:] * (1.0 / RUN_ALIGN)) for x in incl]
    below = (lax.broadcasted_iota(jnp.int32, (LANES, LANES), 1)
             < lax.broadcasted_iota(jnp.int32, (LANES, LANES), 0)).astype(BF16)
    wide = [jnp.broadcast_to(c, (LANES, LANES)) for c in chunks]
    run_start = [_dot(below, w.astype(BF16))[:, 0:1] * RUN_ALIGN for w in wide]
    row8 = lax.broadcasted_iota(jnp.int32, (8, LANES), 0)
    counts = jnp.zeros((8, LANES), F32)
    slot_ids = lax.broadcasted_iota(jnp.int32, (RUN_SLOTS, MOE_SUB), 0)
    for g, p in enumerate(groups):
        counts = jnp.where(row8 == g, wide[g].T[0:8, :], counts)
        slot_of = run_start[g] + incl[g] - onehots[g]
        i1, i2 = picks[g]
        slot1 = jnp.sum(jnp.where(row == i1, slot_of, 0.0), axis=0, keepdims=True)
        slot2 = jnp.sum(jnp.where(row == i2, slot_of, 0.0), axis=0, keepdims=True)
        w1, w2 = weights[g]
        packed = jnp.where(row == 0, slot1, jnp.where(row == 1, slot2, jnp.where(
            row == 2, w1, jnp.where(row == 3, w2, 0.0))))
        route_out[p, :] = packed.T
        pick = jnp.where((slot_ids == slot1.astype(jnp.int32))
                         | (slot_ids == slot2.astype(jnp.int32)), 1.0, 0.0).astype(BF16)
        hl_out[g * RUN_SLOTS:(g + 1) * RUN_SLOTS, :] = _dot(pick, h2[g].astype(BF16)).astype(BF16)
    cnt_out[...] = counts


def _expert_kernel(chunk_s, first_s, hl_in_ref, wg0_ref, wu0_ref, wd0_ref, wg_ref, wu_ref, wd_ref,
                   hl_ref, xbuf, wgu_s, wd_s, sem, drain_s):
    del hl_in_ref
    e = pl.program_id(0)
    pass_chunks = EXPERT_ROWS // RUN_ALIGN
    mine = e % 2
    other = 1 - mine
    gather_sem = lambda buf: sem.at[buf]
    scatter_sem = lambda buf: sem.at[2 + buf]

    def first_pass(ex):
        lo = first_s[ex]
        return lo, jnp.minimum(lo + pass_chunks, first_s[ex + 1])

    def for_chunks(buf, c_lo, c_hi, fn):
        def one(c):
            fn(hl_ref.at[pl.ds(pl.multiple_of(chunk_s[c], RUN_ALIGN), RUN_ALIGN)],
               xbuf.at[buf, pl.ds(pl.multiple_of((c - c_lo) * RUN_ALIGN, RUN_ALIGN), RUN_ALIGN)])

        def block(i, carry):
            for u in range(CHUNK_UNROLL):
                one(c_lo + i * CHUNK_UNROLL + u)
            return carry

        def tail(c, carry):
            one(c)
            return carry

        n_blocks = (c_hi - c_lo) // CHUNK_UNROLL
        lax.fori_loop(0, n_blocks, block, 0)
        lax.fori_loop(c_lo + n_blocks * CHUNK_UNROLL, c_hi, tail, 0)

    def gather(buf, c_lo, c_hi):
        for_chunks(buf, c_lo, c_hi, lambda hbm, vm: pltpu.make_async_copy(
            hbm, vm, gather_sem(buf)).start())

    def scatter(buf, c_lo, c_hi):
        for_chunks(buf, c_lo, c_hi, lambda hbm, vm: pltpu.make_async_copy(
            vm, hbm, scatter_sem(buf)).start())

    def wait_chunks(which_sem, n):
        def wait_rows(rows):
            def body(c, carry):
                pltpu.make_async_copy(hl_ref.at[pl.ds(0, rows)], xbuf.at[0, pl.ds(0, rows)],
                                      which_sem).wait()
                return carry
            return body

        lax.fori_loop(0, n // CHUNK_UNROLL, wait_rows(CHUNK_UNROLL * RUN_ALIGN), 0)
        lax.fori_loop(0, n % CHUNK_UNROLL, wait_rows(RUN_ALIGN), 0)

    cast_parts = EXPERT_ROWS // EXPERT_TILE

    def cast_part(src, dst, k):
        wg, wu, wd = src
        up = pl.ds(pl.multiple_of(k * (D_MODEL // cast_parts), D_MODEL // cast_parts),
                   D_MODEL // cast_parts)
        down = pl.ds(pl.multiple_of(k * (D_EXPERT // cast_parts), D_EXPERT // cast_parts),
                     D_EXPERT // cast_parts)
        wgu_s[dst, up, :D_EXPERT] = wg[0, up, :].astype(BF16)
        wgu_s[dst, up, D_EXPERT:] = wu[0, up, :].astype(BF16)
        wd_s[dst, down, :] = wd[0, down, :].astype(BF16)

    def evaluate(buf, n_chunks, cast_next):
        n_tiles = (n_chunks * RUN_ALIGN + EXPERT_TILE - 1) // EXPERT_TILE

        def tile(k, carry):
            sl = pl.ds(pl.multiple_of(k * EXPERT_TILE, EXPERT_TILE), EXPERT_TILE)
            gu = _dot(xbuf[buf, sl, :], wgu_s[mine])
            gate = gu[:, :D_EXPERT]
            act = gate * jax.nn.sigmoid(gate) * gu[:, D_EXPERT:]
            xbuf[buf, sl, :] = _dot(act.astype(BF16), wd_s[mine]).astype(BF16)
            if cast_next:
                cast_part((wg_ref, wu_ref, wd_ref), other, k)
            return carry

        lax.fori_loop(0, n_tiles, tile, 0)
        if cast_next:
            lax.fori_loop(n_tiles, cast_parts,
                          lambda k, c: (cast_part((wg_ref, wu_ref, wd_ref), other, k), c)[1], 0)

    @pl.when(e == 0)
    def _():
        xbuf[...] = jnp.zeros_like(xbuf)
        gather(0, *first_pass(0))
        drain_s[0] = 0
        for k in range(cast_parts):
            cast_part((wg0_ref, wu0_ref, wd0_ref), 0, k)

    c_lo, c_hi = first_pass(e)
    wait_chunks(gather_sem(mine), c_hi - c_lo)
    wait_chunks(scatter_sem(other), drain_s[0])

    @pl.when(e + 1 < pl.num_programs(0))
    def _():
        gather(other, *first_pass(e + 1))

    evaluate(mine, c_hi - c_lo, True)
    scatter(mine, c_lo, c_hi)

    def later_pass(state):
        lo, pending = state
        hi = jnp.minimum(lo + pass_chunks, first_s[e + 1])
        wait_chunks(scatter_sem(mine), pending)
        gather(mine, lo, hi)
        wait_chunks(gather_sem(mine), hi - lo)
        evaluate(mine, hi - lo, False)
        scatter(mine, lo, hi)
        return hi, hi - lo

    _, pending = lax.while_loop(lambda st: st[0] < first_s[e + 1], later_pass,
                                (c_hi, c_hi - c_lo))
    drain_s[0] = pending

    @pl.when(e + 1 == pl.num_programs(0))
    def _():
        wait_chunks(scatter_sem(mine), pending)


def _combine_kernel(x2_ref, route_ref, gf_ref, yl_ref, o_ref):
    slots = lax.broadcasted_iota(jnp.int32, (MOE_SUB, RUN_SLOTS), 1).astype(F32)
    for g in range(x2_ref.shape[0] // MOE_SUB):
        p = slice(g * MOE_SUB, (g + 1) * MOE_SUB)
        route = route_ref[p, :]
        pick = (jnp.where(slots == route[:, 0:1], route[:, 2:3], 0.0)
                + jnp.where(slots == route[:, 1:2], route[:, 3:4], 0.0))
        y = _dot(pick.astype(BF16), yl_ref[g * RUN_SLOTS:(g + 1) * RUN_SLOTS, :])
        o_ref[p, :] = _rms(x2_ref[p, :] + y, gf_ref[...])


def _full(shape):
    return pl.BlockSpec(shape, lambda *_: (0,) * len(shape))


def _params(sem):
    return pltpu.CompilerParams(dimension_semantics=sem, vmem_limit_bytes=VMEM_LIMIT)


def _mixer_a_layout(t):
    s_rkv = 3 * A_WIDTH
    s_w = s_rkv + D_DECAY_LORA
    s_a = s_w + D_AAA_LORA
    gap = lambda n: jnp.zeros((t.shape[0], n), t.dtype)
    return jnp.concatenate(
        [t[:, :s_w], gap(XA_OFF - D_DECAY_LORA), t[:, s_w:s_a], gap(XG_OFF - XA_OFF - D_AAA_LORA),
         t[:, s_a:], gap(LORA_PAD - XG_OFF - D_GATE_LORA)], axis=1)


def _place(cols, parts):
    out = jnp.zeros((parts[0][1].shape[0], cols), parts[0][1].dtype)
    for off, arr in parts:
        out = lax.dynamic_update_slice(out, arr, (0, off))
    return out


def kernel(x, norm1_g, w_in, b_gate, tmix_mu, w0, w2, a0, a2, g2, k_k, k_a, r_k, lnx_g, lnx_b,
           w_oA, lnv_g, lnv_b, w_s, b_s, w_oB, w_out, norm2_g, w_rg, b_rg, w_re, b_re,
           w_e_gate, w_e_up, w_e_down, final_g):
    bsz, seq, d = x.shape
    n_tok = bsz * seq
    depth = norm1_g.shape[0]
    assert depth == 1, "the moe kernel fuses the final norm, so it must be the last layer"
    assert bsz % WKV_SEQS == 0 and seq % WKV_TOKENS == 0
    xf = x.reshape(n_tok, d)

    s_rkv = 3 * A_WIDTH
    s_w = s_rkv + D_DECAY_LORA
    s_a = s_w + D_AAA_LORA
    a_cols = s_a + D_GATE_LORA
    b_cols = 2 * B_WIDTH

    ones_bd = (jnp.arange(A_WIDTH)[:, None] // A_HEAD
               == jnp.arange(A_WIDTH)[None, :] // A_HEAD).astype(BF16)

    tm_a = 512
    tm_p = 512

    for l in range(depth):
        wl = w_in[l]
        rows_w = 256
        w_a, w_b, w_g = pl.pallas_call(
            _w_in_kernel,
            grid=(d // rows_w,),
            in_specs=[pl.BlockSpec((wl.shape[1], rows_w), lambda i: (0, i))],
            out_specs=[pl.BlockSpec((A_PROJ, rows_w), lambda i: (0, i)),
                       pl.BlockSpec((b_cols, rows_w), lambda i: (0, i)),
                       pl.BlockSpec((2 * d, rows_w), lambda i: (0, i))],
            out_shape=[jax.ShapeDtypeStruct((A_PROJ, d), BF16),
                       jax.ShapeDtypeStruct((b_cols, d), BF16),
                       jax.ShapeDtypeStruct((2 * d, d), BF16)],
            compiler_params=_params(("parallel",)),
            name="w_in_layout",
        )(wl.T)
        mu_a = _mixer_a_layout(tmix_mu[l][None, :])
        w2p = jnp.pad(w2[l], ((0, LANES - D_DECAY_LORA), (0, 0))).astype(BF16)
        a2p = jnp.pad(a2[l], ((0, LANES - D_AAA_LORA), (0, 0))).astype(BF16)
        g2p = jnp.pad(g2[l], ((0, LORA_PAD - XG_OFF - D_GATE_LORA), (0, 0))).astype(BF16)
        g1 = norm1_g[l][None, :]

        row512 = lambda i: (i, 0)
        tok_a = pl.BlockSpec((tm_a, A_WIDTH), row512)
        tok_d = pl.BlockSpec((tm_a, d), row512)
        vec_a = _full((1, A_WIDTH))
        bs_full = jnp.repeat(b_s[l].T, B_GROUP_CH, axis=1)
        outs = pl.pallas_call(
            functools.partial(_inproj_kernel, seq // tm_a),
            grid=(n_tok // tm_a,),
            in_specs=[tok_d,
                      _full((1, d)), _full((A_PROJ, d)), _full((1, A_PROJ)), vec_a,
                      _full((LANES, A_WIDTH)), vec_a, _full((LANES, A_WIDTH)),
                      _full((LORA_PAD - XG_OFF, A_WIDTH)), vec_a, vec_a,
                      _full((A_WIDTH, A_WIDTH)),
                      _full((b_cols, d)), _full((2 * d, d)), _full((1, 2 * d)),
                      _full((1, B_WIDTH)), _full((1, B_WIDTH)),
                      _full((B_GROUPS, GMLP_BLOCK, GMLP_BLOCK)), _full((GMLP_BLOCK, B_WIDTH)),
                      _full((B_WIDTH, d))],
            out_specs=[tok_a] * 7 + [tok_d, tok_d],
            out_shape=[jax.ShapeDtypeStruct((n_tok, A_WIDTH), F32)] * 7
            + [jax.ShapeDtypeStruct((n_tok, d), BF16)] * 2,
            scratch_shapes=[pltpu.VMEM((8, A_PROJ), F32)],
            compiler_params=_params(("arbitrary",)),
            name="inproj",
        )(xf, g1, w_a, mu_a, w0[l][None, :], w2p, a0[l][None, :], a2p, g2p,
          k_k[l][None, :], k_a[l][None, :], ones_bd, w_b, w_g, b_gate[l][None, :],
          lnv_g[l][None, :], lnv_b[l][None, :], w_s[l], bs_full, w_oB[l].astype(BF16))
        r_, k_, v_, a_, b_, ld_, gg_, ybg, ga = outs

        n_pairs = A_WIDTH // HEAD_PAIR
        n_chunks = WKV_TOKENS // WKV_CHUNK
        tok_w = pl.BlockSpec((WKV_SEQS, WKV_TOKENS, A_WIDTH), lambda bi, ti: (bi, ti, 0))
        vec_w = _full((1, A_WIDTH))
        seq_major = lambda t: t.reshape(bsz, seq, A_WIDTH)
        tok_scratch = pltpu.VMEM((WKV_SEQS, WKV_TOKENS, A_WIDTH), BF16)
        ya_in = pl.pallas_call(
            _wkv_kernel,
            grid=(bsz // WKV_SEQS, seq // WKV_TOKENS),
            in_specs=[tok_w] * 7 + [vec_w, vec_w, vec_w, _full((HEAD_PAIR, HEAD_PAIR))],
            out_specs=tok_w,
            out_shape=jax.ShapeDtypeStruct((bsz, seq, A_WIDTH), BF16),
            scratch_shapes=[pltpu.VMEM((WKV_SEQS, n_pairs, HEAD_PAIR, HEAD_PAIR), F32)]
            + [tok_scratch] * 5
            + [pltpu.VMEM((WKV_SEQS, n_chunks, n_pairs, HEAD_PAIR, HEAD_PAIR), BF16),
               pltpu.VMEM((WKV_SEQS, n_chunks, n_pairs, HEAD_PAIR, HEAD_PAIR), F32),
               pltpu.VMEM((WKV_SEQS, WKV_TOKENS, A_WIDTH), F32)],
            compiler_params=_params(("parallel", "arbitrary")),
            name="wkv",
        )(*[seq_major(t) for t in (r_, k_, v_, a_, b_, ld_, gg_)], r_k[l].reshape(1, A_WIDTH),
          lnx_g[l][None, :], lnx_b[l][None, :],
          ones_bd[:HEAD_PAIR, :HEAD_PAIR]).reshape(n_tok, A_WIDTH)

        w_r = _place(LANES, [(0, jnp.transpose(w_re[l], (1, 0, 2)).reshape(d, N_EXPERTS)),
                             (N_EXPERTS, w_rg[l])])
        b_r = _place(LANES, [(0, b_re[l].reshape(1, N_EXPERTS)), (N_EXPERTS, b_rg[l][None, :])])
        wr_hi = w_r.astype(BF16)
        wr_lo = (w_r - wr_hi.astype(F32)).astype(BF16)
        wr_t = jnp.concatenate([wr_hi.T, wr_lo.T], axis=0)
        tok_p = pl.BlockSpec((tm_p, d), row512)
        lane_p = pl.BlockSpec((tm_p, LANES), row512)
        n_sub = n_tok // MOE_SUB
        sorted_rows = tm_p // MOE_SUB * RUN_SLOTS
        x2, h_sorted, route, cnt = pl.pallas_call(
            _post_kernel,
            grid=(n_tok // tm_p,),
            in_specs=[tok_p, pl.BlockSpec((tm_p, A_WIDTH), row512), tok_p, tok_p,
                      _full((A_WIDTH, d)), _full((d, d)), _full((1, d)), _full((2 * LANES, d)),
                      _full((LANES, 1))],
            out_specs=[tok_p, pl.BlockSpec((sorted_rows, d), row512), lane_p,
                       pl.BlockSpec((8, LANES), row512)],
            out_shape=[jax.ShapeDtypeStruct((n_tok, d), F32),
                       jax.ShapeDtypeStruct((n_sub * RUN_SLOTS, d), BF16),
                       jax.ShapeDtypeStruct((n_tok, LANES), F32),
                       jax.ShapeDtypeStruct((n_tok // tm_p * 8, LANES), F32)],
            compiler_params=_params(("parallel",)),
            name="post",
        )(xf, ya_in, ga, ybg, w_oA[l].astype(BF16), w_out[l].astype(BF16),
          norm2_g[l][None, :], wr_t, b_r.T)

        run_chunks = cnt.reshape(n_tok // tm_p, 8, LANES)[:, :tm_p // MOE_SUB, :N_EXPERTS]
        run_chunks = run_chunks.reshape(n_sub, N_EXPERTS).astype(jnp.int32)
        run_start = (jnp.cumsum(run_chunks, axis=1) - run_chunks) * RUN_ALIGN
        run_row = (jnp.arange(n_sub, dtype=jnp.int32)[:, None] * RUN_SLOTS + run_start).T.reshape(-1)
        run_n = run_chunks.T.reshape(-1)
        run_end = jnp.cumsum(run_n)
        first_chunk = jnp.concatenate([jnp.zeros((1,), jnp.int32), run_end[n_sub - 1::n_sub]])
        pos = jnp.arange(n_sub * RUN_SLOTS // RUN_ALIGN, dtype=jnp.int32)
        run_of = jnp.sum(pos[:, None] >= run_end[None, :], axis=1)
        in_run = run_of[:, None] == jnp.arange(run_n.shape[0], dtype=jnp.int32)[None, :]
        pick = lambda t: jnp.sum(jnp.where(in_run, t[None, :], 0), axis=1)
        chunk_rows = pick(run_row) + (pos - pick(run_end - run_n)) * RUN_ALIGN

        expert0 = lambda e, *_: (0, 0, 0)
        next_expert = lambda e, *_: (jnp.minimum(e + 1, N_EXPERTS - 1), 0, 0)
        y_sorted = pl.pallas_call(
            _expert_kernel,
            grid_spec=pltpu.PrefetchScalarGridSpec(
                num_scalar_prefetch=2, grid=(N_EXPERTS,),
                in_specs=[pl.BlockSpec(memory_space=pl.ANY),
                          pl.BlockSpec((1, d, D_EXPERT), expert0),
                          pl.BlockSpec((1, d, D_EXPERT), expert0),
                          pl.BlockSpec((1, D_EXPERT, d), expert0),
                          pl.BlockSpec((1, d, D_EXPERT), next_expert),
                          pl.BlockSpec((1, d, D_EXPERT), next_expert),
                          pl.BlockSpec((1, D_EXPERT, d), next_expert)],
                out_specs=pl.BlockSpec(memory_space=pl.ANY),
                scratch_shapes=[pltpu.VMEM((2, EXPERT_ROWS, d), BF16),
                                pltpu.VMEM((2, d, 2 * D_EXPERT), BF16),
                                pltpu.VMEM((2, D_EXPERT, d), BF16),
                                pltpu.SemaphoreType.DMA((4,)),
                                pltpu.SMEM((1,), jnp.int32)]),
            out_shape=jax.ShapeDtypeStruct((n_sub * RUN_SLOTS, d), BF16),
            input_output_aliases={2: 0},
            compiler_params=_params(("arbitrary",)),
            name="moe_experts",
        )(chunk_rows, first_chunk, h_sorted, w_e_gate[l], w_e_up[l], w_e_down[l],
          w_e_gate[l], w_e_up[l], w_e_down[l])

        xf = pl.pallas_call(
            _combine_kernel,
            grid=(n_tok // tm_p,),
            in_specs=[tok_p, lane_p, _full((1, d)), pl.BlockSpec((sorted_rows, d), row512)],
            out_specs=tok_p,
            out_shape=jax.ShapeDtypeStruct((n_tok, d), F32),
            compiler_params=_params(("parallel",)),
            name="moe_combine",
        )(x2, route, final_g[None, :], y_sorted)

    return xf.reshape(bsz, seq, d)
```

```python
import functools

import jax
import jax.numpy as jnp
from jax import lax
from jax.experimental import pallas as pl
from jax.experimental.pallas import tpu as pltpu

F32 = jnp.float32
BF16 = jnp.bfloat16

D_MODEL = 1024
A_WIDTH = 512
A_HEAD = 64
D_DECAY_LORA = 64
D_AAA_LORA = 64
D_GATE_LORA = 160
B_WIDTH = 512
B_GROUPS = 4
B_GROUP_CH = 128
GMLP_BLOCK = 128
N_GROUPS = 4
EXPERTS_PER_GROUP = 8
N_EXPERTS = 32
D_EXPERT = 256
NORM_EPS = 1e-6
LN_EPS = 1e-5
LNX_EPS = 64e-5

LANES = 128
LORA_PAD = 512
XW_OFF, XA_OFF, XG_OFF = 0, 128, 256
A_PROJ = 3 * A_WIDTH + LORA_PAD
WKV_CHUNK = 64
HEAD_PAIR = 2 * A_HEAD
WKV_SEQS = 8
WKV_TOKENS = 64
WKV_PREP_GROUP = 4
WKV_NORM_GROUP = 1
MOE_SUB = 256
RUN_ALIGN = 16
RUN_SLOTS = 2 * MOE_SUB + N_EXPERTS * RUN_ALIGN
EXPERT_ROWS = 2048
EXPERT_TILE = 512
DENSE_SPLIT = 256
CHUNK_UNROLL = 4
VMEM_LIMIT = 48 * 1024 * 1024


def _rms(x, g):
    return x * lax.rsqrt(jnp.mean(x * x, axis=-1, keepdims=True) + NORM_EPS) * g


def _dot(a, b):
    return jnp.dot(a, b, preferred_element_type=F32)


def _dot_nt(a, b):
    return lax.dot_general(a, b, (((1,), (1,)), ((), ())), preferred_element_type=F32)


def _split2(x):
    hi = x.astype(BF16)
    lo = (x - hi.astype(F32)).astype(BF16)
    return hi, lo


def _split3(x):
    hi = x.astype(BF16)
    r1 = x - hi.astype(F32)
    mid = r1.astype(BF16)
    lo = (r1 - mid.astype(F32)).astype(BF16)
    return hi, mid, lo


def _pair_head_sums(xs, ones_pair):
    n_tiles = A_WIDTH // HEAD_PAIR
    rows = xs[0].shape[0]
    tiles = [x[:, t * HEAD_PAIR:(t + 1) * HEAD_PAIR].astype(BF16) for x in xs for t in range(n_tiles)]
    sums = _dot(jnp.concatenate(tiles, axis=0), ones_pair)
    return [jnp.concatenate([sums[(i * n_tiles + t) * rows:(i * n_tiles + t + 1) * rows]
                             for t in range(n_tiles)], axis=1) for i in range(len(xs))]


def _w_in_kernel(w_ref, wa_out, wb_out, wg_out):
    s_rkv = 3 * A_WIDTH
    s_w = s_rkv + D_DECAY_LORA
    s_a = s_w + D_AAA_LORA
    a_cols = s_a + D_GATE_LORA
    b_end = a_cols + 2 * B_WIDTH
    wa_out[...] = jnp.zeros_like(wa_out)
    wa_out[:s_w, :] = w_ref[:s_w, :].astype(BF16)
    wa_out[s_rkv + XA_OFF:s_rkv + XA_OFF + D_AAA_LORA, :] = w_ref[s_w:s_a, :].astype(BF16)
    wa_out[s_rkv + XG_OFF:s_rkv + XG_OFF + D_GATE_LORA, :] = w_ref[s_a:a_cols, :].astype(BF16)
    wb_out[...] = w_ref[a_cols:b_end, :].astype(BF16)
    wg_out[...] = w_ref[b_end:, :].astype(BF16)


def _inproj_kernel(tiles_per_seq, x_ref, g1_ref, wa_ref, mu_ref, w0_ref, w2_ref, a0_ref, a2_ref,
                   g2_ref, kk_ref, ka_ref, ones_ref, wb_ref, wg_ref, bg_ref, lng_ref, lnb_ref,
                   ws_ref, bs_ref, wo_ref,
                   r_out, k_out, v_out, a_out, b_out, ld_out, g_out, ybg_out, ga_out, tail_ref):
    i = pl.program_id(0)
    tm = x_ref.shape[0]
    parts = [slice(j * DENSE_SPLIT, (j + 1) * DENSE_SPLIT) for j in range(tm // DENSE_SPLIT)]
    g1 = g1_ref[...]
    h = [_rms(x_ref[p, :], g1).astype(BF16) for p in parts]
    proj = [_dot_nt(hh, wa_ref[...]) for hh in h]
    pb = [_dot_nt(hh, wb_ref[...]) for hh in h]

    prev = [jnp.where(i % tiles_per_seq == 0, 0.0, tail_ref[7:8, :])]
    prev += [p[DENSE_SPLIT - 1:, :] for p in proj[:-1]]
    tail_ref[...] = proj[-1][DENSE_SPLIT - 8:, :]
    row = lax.broadcasted_iota(jnp.int32, proj[0].shape, 0)
    pm = []
    for p, pv in zip(proj, prev):
        shifted = jnp.where(row == 0, pv, pltpu.roll(p, 1, axis=0))
        pm.append(p + mu_ref[...] * (shifted - p))
    gates = [jax.nn.sigmoid(_dot_nt(hh, wg_ref[...]) + bg_ref[...]) for hh in h]

    tri = (lax.broadcasted_iota(jnp.int32, (GMLP_BLOCK, GMLP_BLOCK), 0)
           >= lax.broadcasted_iota(jnp.int32, (GMLP_BLOCK, GMLP_BLOCK), 1))
    ws = [jnp.where(tri, ws_ref[grp], 0.0).astype(BF16) for grp in range(B_GROUPS)]
    bs = bs_ref[...]
    us, vns = [], []
    for x in pb:
        z = 0.5 * x * (1.0 + lax.erf(x * (2.0 ** -0.5)))
        us.append(z[:, :B_WIDTH])
        v = z[:, B_WIDTH:]
        mean = jnp.mean(v, axis=-1, keepdims=True)
        vc = v - mean
        var = jnp.mean(vc * vc, axis=-1, keepdims=True)
        vns.append((vc * lax.rsqrt(var + LN_EPS) * lng_ref[...] + lnb_ref[...]).astype(BF16))

    lora = [x[:, 3 * A_WIDTH:] for x in pm]
    decay = [_dot(jnp.tanh(x[:, XW_OFF:XW_OFF + LANES]).astype(BF16), w2_ref[...]) for x in lora]
    rate = [_dot(x[:, XA_OFF:XA_OFF + LANES].astype(BF16), a2_ref[...]) for x in lora]
    gate = [_dot(jax.nn.sigmoid(x[:, XG_OFF:]).astype(BF16), g2_ref[...]) for x in lora]
    kks = [x[:, A_WIDTH:2 * A_WIDTH] * kk_ref[...] for x in pm]
    sq = [_dot((kk * kk).astype(BF16), ones_ref[...]) for kk in kks]

    svs = []
    for vn in vns:
        rows = []
        for blk in range(DENSE_SPLIT // GMLP_BLOCK):
            cols = [_dot(ws[grp], vn[blk * GMLP_BLOCK:(blk + 1) * GMLP_BLOCK,
                                     grp * B_GROUP_CH:(grp + 1) * B_GROUP_CH])
                    for grp in range(B_GROUPS)]
            rows.append(jnp.concatenate(cols, axis=1) + bs)
        svs.append(jnp.concatenate(rows, axis=0))
    yb = [_dot((u * sv).astype(BF16), wo_ref[...]) for u, sv in zip(us, svs)]

    for j, p in enumerate(parts):
        k = pm[j][:, A_WIDTH:2 * A_WIDTH]
        z = -(w0_ref[...] + decay[j])
        softplus = jnp.maximum(z, 0.0) + jnp.log(1.0 + jnp.exp(-jnp.abs(z)))
        w = -softplus - 0.5
        a_lr = jax.nn.sigmoid(a0_ref[...] + rate[j])
        kk = kks[j] / jnp.maximum(jnp.sqrt(sq[j]), 1e-12)
        r_out[p, :] = pm[j][:, 0:A_WIDTH]
        k_out[p, :] = k * (1.0 + (a_lr - 1.0) * ka_ref[...])
        v_out[p, :] = pm[j][:, 2 * A_WIDTH:3 * A_WIDTH]
        a_out[p, :] = -kk
        b_out[p, :] = kk * a_lr
        ld_out[p, :] = -jnp.exp(w)
        g_out[p, :] = gate[j]
        ga_out[p, :] = gates[j][:, :D_MODEL].astype(BF16)
        ybg_out[p, :] = (gates[j][:, D_MODEL:] * yb[j]).astype(BF16)


def _wkv_kernel(r_ref, k_ref, v_ref, a_ref, b_ref, ld_ref, g_ref, rk_ref, lng_ref, lnb_ref,
                ones_ref, o_ref, st_ref, ta_s, tl_s, arb_s, ark_s, rt_s, bkt_s, dcol_s, y_s):
    C = WKV_CHUNK
    bb, tb, _ = r_ref.shape
    n_chunks = tb // C
    n_pairs = A_WIDTH // HEAD_PAIR

    @pl.when(pl.program_id(1) == 0)
    def _():
        st_ref[...] = jnp.zeros_like(st_ref)

    row = lax.broadcasted_iota(jnp.int32, (C, HEAD_PAIR), 0)
    src = lax.broadcasted_iota(jnp.int32, (C, HEAD_PAIR), 1) & (C - 1)
    incl = src <= row
    strict = src < row
    eye_pair = jnp.where(src == row, 1.0, 0.0)
    bd_mask = ((lax.broadcasted_iota(jnp.int32, (HEAD_PAIR, HEAD_PAIR), 0) >= A_HEAD)
               == (lax.broadcasted_iota(jnp.int32, (HEAD_PAIR, HEAD_PAIR), 1) >= A_HEAD))
    tri_c = (lax.broadcasted_iota(jnp.int32, (C, C), 0)
             >= lax.broadcasted_iota(jnp.int32, (C, C), 1)).astype(BF16)
    ones_pair = ones_ref[...]
    pair_cols = [slice(p * HEAD_PAIR, (p + 1) * HEAD_PAIR) for p in range(n_pairs)]

    def bd(x):
        xb = x.astype(BF16)
        return jnp.where(bd_mask, jnp.concatenate([xb, xb], axis=0), jnp.zeros((), BF16))

    def prep(it, carry):
        where, lhs, rhs, ats = [], [], [], []
        for j in range(WKV_PREP_GROUP):
            flat = it * WKV_PREP_GROUP + j
            b = flat // n_chunks
            c = flat % n_chunks
            sl = pl.ds(pl.multiple_of(c * C, C), C)
            ld = ld_ref[b, sl, :]
            hi, mid, lo = _split3(ld)
            cl = _dot(tri_c, hi) + _dot(tri_c, mid) + _dot(tri_c, lo)
            cl_end = cl[C - 1:C, :]
            k = k_ref[b, sl, :]
            bv = b_ref[b, sl, :]
            d_inv = jnp.exp(-cl)
            d_tail = jnp.exp(cl_end - cl)
            rt = (r_ref[b, sl, :] * jnp.exp(cl)).astype(BF16)
            rt_s[b, sl, :] = rt
            kt = k * d_inv
            bt = bv * d_inv
            at = a_ref[b, sl, :] * jnp.exp(cl - ld)
            kd = k * d_tail
            bdk = bv * d_tail
            d_end = jnp.exp(cl_end)
            for p, cs in enumerate(pair_cols):
                bkt_s[b, c, p] = jnp.concatenate([bdk[:, cs], kd[:, cs]], axis=0).T.astype(BF16)
                dcol_s[b, c, p] = jnp.broadcast_to(d_end[:, cs], (HEAD_PAIR, HEAD_PAIR)).T
                where.append((b, sl, cs))
                ats.append(at[:, cs])
                lhs.append(jnp.concatenate([rt[:, cs], at[:, cs].astype(BF16)], axis=0))
                rhs.append(jnp.concatenate([bd(kt[:, cs]), bd(bt[:, cs])], axis=0))
        n = len(where)
        amat = [_dot_nt(lhs[i], rhs[i]) for i in range(n)]
        l_ak, l_ab = [], []
        for i, (b, sl, cs) in enumerate(where):
            ark_s[b, sl, cs] = jnp.where(incl, amat[i][:C, :HEAD_PAIR], 0.0).astype(BF16)
            arb_s[b, sl, cs] = jnp.where(incl, amat[i][:C, HEAD_PAIR:], 0.0).astype(BF16)
            l_ak.append(jnp.where(strict, amat[i][C:, :HEAD_PAIR], 0.0))
            l_ab.append(jnp.where(strict, amat[i][C:, HEAD_PAIR:], 0.0))

        t_mat = [eye_pair + l for l in l_ab]
        q = [_dot(l.astype(BF16), bd(l)) for l in l_ab]
        n_sq = 1
        while 2 * n_sq < C // 2:
            both = [_dot(q[i].astype(BF16), jnp.concatenate([bd(q[i]), bd(t_mat[i])], axis=1))
                    for i in range(n)]
            q = [x[:, :HEAD_PAIR] for x in both]
            t_mat = [t_mat[i] + both[i][:, HEAD_PAIR:] for i in range(n)]
            n_sq *= 2
        t_mat = [t_mat[i] + _dot(q[i].astype(BF16), bd(t_mat[i])) for i in range(n)]
        tal = [_dot(t_mat[i].astype(BF16), jnp.concatenate([bd(ats[i]), bd(l_ak[i])], axis=1))
               for i in range(n)]
        for i, (b, sl, cs) in enumerate(where):
            ta_s[b, sl, cs] = tal[i][:, :HEAD_PAIR].astype(BF16)
            tl_s[b, sl, cs] = tal[i][:, HEAD_PAIR:].astype(BF16)
        return carry

    lax.fori_loop(0, bb * n_chunks // WKV_PREP_GROUP, prep, 0)

    def step(c, carry):
        sl = pl.ds(pl.multiple_of(c * C, C), C)
        chains = [(b, p) for b in range(bb) for p in range(n_pairs)]
        v = [v_ref[b, sl, :] for b in range(bb)]
        st = [st_ref[b, p] for b, p in chains]
        st_b = [x.astype(BF16) for x in st]
        bd_v = [bd(v[b][:, pair_cols[p]]) for b, p in chains]
        on_st = [_dot(jnp.concatenate([ta_s[b, sl, pair_cols[p]], rt_s[b, sl, pair_cols[p]]],
                                      axis=0), st_b[i]) for i, (b, p) in enumerate(chains)]
        on_v = [_dot(jnp.concatenate([tl_s[b, sl, pair_cols[p]], ark_s[b, sl, pair_cols[p]]],
                                     axis=0), bd_v[i]) for i, (b, p) in enumerate(chains)]
        u = [on_st[i][:C] + on_v[i][:C] for i in range(len(chains))]
        for i, (b, p) in enumerate(chains):
            uv = jnp.concatenate([u[i], v[b][:, pair_cols[p]]], axis=0).astype(BF16)
            st_ref[b, p] = (dcol_s[b, c, p] * st[i]
                            + jnp.where(bd_mask, _dot(bkt_s[b, c, p], uv), 0.0))
        ys = [on_st[i][C:] + on_v[i][C:] + _dot(arb_s[b, sl, pair_cols[p]], bd(u[i]))
              for i, (b, p) in enumerate(chains)]
        for b in range(bb):
            y_s[b, sl, :] = jnp.concatenate(ys[b * n_pairs:(b + 1) * n_pairs], axis=1)
        return carry

    lax.fori_loop(0, n_chunks, step, 0)

    def finish(it, carry):
        items = [(b, pl.ds(pl.multiple_of((it * WKV_NORM_GROUP + j) * C, C), C))
                 for j in range(WKV_NORM_GROUP) for b in range(bb)]
        y = [y_s[b, sl, :] for b, sl in items]
        rkr = [r_ref[b, sl, :] * k_ref[b, sl, :] * rk_ref[...] for b, sl in items]
        sums = [_pair_head_sums([y[i], rkr[i]], ones_pair) for i in range(len(items))]
        yc = [y[i] - sums[i][0] * (1.0 / A_HEAD) for i in range(len(items))]
        var = [_pair_head_sums([x * x], ones_pair)[0] * (1.0 / A_HEAD) for x in yc]
        for i, (b, sl) in enumerate(items):
            yn = yc[i] * lax.rsqrt(var[i] + LNX_EPS) * lng_ref[...] + lnb_ref[...]
            out = (yn + sums[i][1] * v_ref[b, sl, :]) * g_ref[b, sl, :]
            o_ref[b, sl, :] = out.astype(o_ref.dtype)
        return carry

    lax.fori_loop(0, n_chunks // WKV_NORM_GROUP, finish, 0)


def _post_kernel(x_ref, ya_ref, ga_ref, ybg_ref, woa_ref, wout_ref, g2_ref, wrt_ref, brt_ref,
                 x2_out, hl_out, route_out, cnt_out):
    n_sub = x_ref.shape[0] // MOE_SUB
    groups = [slice(g * MOE_SUB, (g + 1) * MOE_SUB) for g in range(n_sub)]
    y_a = [_dot(ya_ref[p, :].astype(BF16), woa_ref[...]) for p in groups]
    x2 = [x_ref[p, :] + _dot((ga_ref[p, :] * y + ybg_ref[p, :]).astype(BF16), wout_ref[...])
          for p, y in zip(groups, y_a)]
    h2 = [_rms(x, g2_ref[...]) for x in x2]
    split = [_split2(h) for h in h2]
    on_hi = [_dot_nt(wrt_ref[...], hi) for hi, _ in split]
    on_lo = [_dot_nt(wrt_ref[0:LANES, :], lo) for _, lo in split]
    row = lax.broadcasted_iota(jnp.int32, (LANES, MOE_SUB), 0)
    neg = jnp.float32(-jnp.inf)
    big = jnp.int32(LANES)
    is_grp = (row >= N_EXPERTS) & (row < N_EXPERTS + N_GROUPS)
    top = lambda v: jnp.max(v, axis=0, keepdims=True)
    first = lambda hit: jnp.min(jnp.where(hit, row, big), axis=0, keepdims=True)
    picks, onehots, weights = [], [], []
    for g, p in enumerate(groups):
        x2_out[p, :] = x2[g].astype(x2_out.dtype)
        logits = on_hi[g][:LANES, :] + on_hi[g][LANES:, :] + on_lo[g] + brt_ref[...]
        gl = jnp.where(is_grp, logits, neg)
        gmax = top(gl)
        g_p = 1.0 / jnp.sum(jnp.exp(gl - gmax), axis=0, keepdims=True)
        lo_row = (first(gl == gmax) - N_EXPERTS) * EXPERTS_PER_GROUP
        el = jnp.where((row >= lo_row) & (row < lo_row + EXPERTS_PER_GROUP), logits, neg)
        e1 = top(el)
        i1 = first(el == e1)
        el2 = jnp.where(row == i1, neg, el)
        e2 = top(el2)
        i2 = first(el2 == e2)
        t = jnp.exp(e2 - e1)
        weights.append((g_p / (1.0 + t), g_p * t / (1.0 + t)))
        picks.append((i1, i2))
        onehots.append(jnp.where((row == i1) | (row == i2), 1.0, 0.0))

    earlier = (lax.broadcasted_iota(jnp.int32, (MOE_SUB, MOE_SUB), 0)
               <= lax.broadcasted_iota(jnp.int32, (MOE_SUB, MOE_SUB), 1)).astype(BF16)
    incl = [_dot(oh.astype(BF16), earlier) for oh in onehots]
    chunks = [jnp.ceil(x[:, MOE_SUB - 1:] * (1.0 / RUN_ALIGN)) for x in incl]
    below = (lax.broadcasted_iota(jnp.int32, (LANES, LANES), 1)
             < lax.broadcasted_iota(jnp.int32, (LANES, LANES), 0)).astype(BF16)
    wide = [jnp.broadcast_to(c, (LANES, LANES)) for c in chunks]
    run_start = [_dot(below, w.astype(BF16))[:, 0:1] * RUN_ALIGN for w in wide]
    row8 = lax.broadcasted_iota(jnp.int32, (8, LANES), 0)
    counts = jnp.zeros((8, LANES), F32)
    slot_ids = lax.broadcasted_iota(jnp.int32, (RUN_SLOTS, MOE_SUB), 0)
    for g, p in enumerate(groups):
        counts = jnp.where(row8 == g, wide[g].T[0:8, :], counts)
        slot_of = run_start[g] + incl[g] - onehots[g]
        i1, i2 = picks[g]
        slot1 = jnp.sum(jnp.where(row == i1, slot_of, 0.0), axis=0, keepdims=True)
        slot2 = jnp.sum(jnp.where(row == i2, slot_of, 0.0), axis=0, keepdims=True)
        w1, w2 = weights[g]
        packed = jnp.where(row == 0, slot1, jnp.where(row == 1, slot2, jnp.where(
            row == 2, w1, jnp.where(row == 3, w2, 0.0))))
        route_out[p, :] = packed.T
        pick = jnp.where((slot_ids == slot1.astype(jnp.int32))
                         | (slot_ids == slot2.astype(jnp.int32)), 1.0, 0.0).astype(BF16)
        hl_out[g * RUN_SLOTS:(g + 1) * RUN_SLOTS, :] = _dot(pick, h2[g].astype(BF16)).astype(BF16)
    cnt_out[...] = counts


def _expert_kernel(chunk_s, first_s, hl_in_ref, wg_ref, wu_ref, wd_ref, hl_ref, xbuf, wgu_s, wd_s,
                   sem, drain_s):
    del hl_in_ref
    e = pl.program_id(0)
    pass_chunks = EXPERT_ROWS // RUN_ALIGN
    mine = e % 2
    other = 1 - mine
    gather_sem = lambda buf: sem.at[buf]
    scatter_sem = lambda buf: sem.at[2 + buf]

    def first_pass(ex):
        lo = first_s[ex]
        return lo, jnp.minimum(lo + pass_chunks, first_s[ex + 1])

    def for_chunks(buf, c_lo, c_hi, fn):
        def one(c):
            fn(hl_ref.at[pl.ds(pl.multiple_of(chunk_s[c], RUN_ALIGN), RUN_ALIGN)],
               xbuf.at[buf, pl.ds(pl.multiple_of((c - c_lo) * RUN_ALIGN, RUN_ALIGN), RUN_ALIGN)])

        def block(i, carry):
            for u in range(CHUNK_UNROLL):
                one(c_lo + i * CHUNK_UNROLL + u)
            return carry

        def tail(c, carry):
            one(c)
            return carry

        n_blocks = (c_hi - c_lo) // CHUNK_UNROLL
        lax.fori_loop(0, n_blocks, block, 0)
        lax.fori_loop(c_lo + n_blocks * CHUNK_UNROLL, c_hi, tail, 0)

    def gather(buf, c_lo, c_hi):
        for_chunks(buf, c_lo, c_hi, lambda hbm, vm: pltpu.make_async_copy(
            hbm, vm, gather_sem(buf)).start())

    def scatter(buf, c_lo, c_hi):
        for_chunks(buf, c_lo, c_hi, lambda hbm, vm: pltpu.make_async_copy(
            vm, hbm, scatter_sem(buf)).start())

    def wait_chunks(which_sem, n):
        def wait_rows(rows):
            def body(c, carry):
                pltpu.make_async_copy(hl_ref.at[pl.ds(0, rows)], xbuf.at[0, pl.ds(0, rows)],
                                      which_sem).wait()
                return carry
            return body

        lax.fori_loop(0, n // CHUNK_UNROLL, wait_rows(CHUNK_UNROLL * RUN_ALIGN), 0)
        lax.fori_loop(0, n % CHUNK_UNROLL, wait_rows(RUN_ALIGN), 0)

    def evaluate(buf, n_chunks):
        def tile(k, carry):
            sl = pl.ds(pl.multiple_of(k * EXPERT_TILE, EXPERT_TILE), EXPERT_TILE)
            gu = _dot(xbuf[buf, sl, :], wgu_s[...])
            gate = gu[:, :D_EXPERT]
            act = gate * jax.nn.sigmoid(gate) * gu[:, D_EXPERT:]
            xbuf[buf, sl, :] = _dot(act.astype(BF16), wd_s[...]).astype(BF16)
            return carry

        lax.fori_loop(0, (n_chunks * RUN_ALIGN + EXPERT_TILE - 1) // EXPERT_TILE, tile, 0)

    @pl.when(e == 0)
    def _():
        xbuf[...] = jnp.zeros_like(xbuf)
        gather(0, *first_pass(0))
        drain_s[0] = 0

    wgu_s[:, :D_EXPERT] = wg_ref[0].astype(BF16)
    wgu_s[:, D_EXPERT:] = wu_ref[0].astype(BF16)
    wd_s[...] = wd_ref[0].astype(BF16)

    c_lo, c_hi = first_pass(e)
    wait_chunks(gather_sem(mine), c_hi - c_lo)
    wait_chunks(scatter_sem(other), drain_s[0])

    @pl.when(e + 1 < pl.num_programs(0))
    def _():
        gather(other, *first_pass(e + 1))

    evaluate(mine, c_hi - c_lo)
    scatter(mine, c_lo, c_hi)

    def later_pass(state):
        lo, pending = state
        hi = jnp.minimum(lo + pass_chunks, first_s[e + 1])
        wait_chunks(scatter_sem(mine), pending)
        gather(mine, lo, hi)
        wait_chunks(gather_sem(mine), hi - lo)
        evaluate(mine, hi - lo)
        scatter(mine, lo, hi)
        return hi, hi - lo

    _, pending = lax.while_loop(lambda st: st[0] < first_s[e + 1], later_pass,
                                (c_hi, c_hi - c_lo))
    drain_s[0] = pending

    @pl.when(e + 1 == pl.num_programs(0))
    def _():
        wait_chunks(scatter_sem(mine), pending)


def _combine_kernel(x2_ref, route_ref, gf_ref, yl_ref, o_ref):
    slots = lax.broadcasted_iota(jnp.int32, (MOE_SUB, RUN_SLOTS), 1).astype(F32)
    for g in range(x2_ref.shape[0] // MOE_SUB):
        p = slice(g * MOE_SUB, (g + 1) * MOE_SUB)
        route = route_ref[p, :]
        pick = (jnp.where(slots == route[:, 0:1], route[:, 2:3], 0.0)
                + jnp.where(slots == route[:, 1:2], route[:, 3:4], 0.0))
        y = _dot(pick.astype(BF16), yl_ref[g * RUN_SLOTS:(g + 1) * RUN_SLOTS, :])
        o_ref[p, :] = _rms(x2_ref[p, :] + y, gf_ref[...])


def _full(shape):
    return pl.BlockSpec(shape, lambda *_: (0,) * len(shape))


def _params(sem):
    return pltpu.CompilerParams(dimension_semantics=sem, vmem_limit_bytes=VMEM_LIMIT)


def _mixer_a_layout(t):
    s_rkv = 3 * A_WIDTH
    s_w = s_rkv + D_DECAY_LORA
    s_a = s_w + D_AAA_LORA
    gap = lambda n: jnp.zeros((t.shape[0], n), t.dtype)
    return jnp.concatenate(
        [t[:, :s_w], gap(XA_OFF - D_DECAY_LORA), t[:, s_w:s_a], gap(XG_OFF - XA_OFF - D_AAA_LORA),
         t[:, s_a:], gap(LORA_PAD - XG_OFF - D_GATE_LORA)], axis=1)


def _place(cols, parts):
    out = jnp.zeros((parts[0][1].shape[0], cols), parts[0][1].dtype)
    for off, arr in parts:
        out = lax.dynamic_update_slice(out, arr, (0, off))
    return out


def kernel(x, norm1_g, w_in, b_gate, tmix_mu, w0, w2, a0, a2, g2, k_k, k_a, r_k, lnx_g, lnx_b,
           w_oA, lnv_g, lnv_b, w_s, b_s, w_oB, w_out, norm2_g, w_rg, b_rg, w_re, b_re,
           w_e_gate, w_e_up, w_e_down, final_g):
    bsz, seq, d = x.shape
    n_tok = bsz * seq
    depth = norm1_g.shape[0]
    assert depth == 1, "the moe kernel fuses the final norm, so it must be the last layer"
    assert bsz % WKV_SEQS == 0 and seq % WKV_TOKENS == 0
    xf = x.reshape(n_tok, d)

    s_rkv = 3 * A_WIDTH
    s_w = s_rkv + D_DECAY_LORA
    s_a = s_w + D_AAA_LORA
    a_cols = s_a + D_GATE_LORA
    b_cols = 2 * B_WIDTH

    ones_bd = (jnp.arange(A_WIDTH)[:, None] // A_HEAD
               == jnp.arange(A_WIDTH)[None, :] // A_HEAD).astype(BF16)

    tm_a = 512
    tm_p = 512

    for l in range(depth):
        wl = w_in[l]
        rows_w = 256
        w_a, w_b, w_g = pl.pallas_call(
            _w_in_kernel,
            grid=(d // rows_w,),
            in_specs=[pl.BlockSpec((wl.shape[1], rows_w), lambda i: (0, i))],
            out_specs=[pl.BlockSpec((A_PROJ, rows_w), lambda i: (0, i)),
                       pl.BlockSpec((b_cols, rows_w), lambda i: (0, i)),
                       pl.BlockSpec((2 * d, rows_w), lambda i: (0, i))],
            out_shape=[jax.ShapeDtypeStruct((A_PROJ, d), BF16),
                       jax.ShapeDtypeStruct((b_cols, d), BF16),
                       jax.ShapeDtypeStruct((2 * d, d), BF16)],
            compiler_params=_params(("parallel",)),
            name="w_in_layout",
        )(wl.T)
        mu_a = _mixer_a_layout(tmix_mu[l][None, :])
        w2p = jnp.pad(w2[l], ((0, LANES - D_DECAY_LORA), (0, 0))).astype(BF16)
        a2p = jnp.pad(a2[l], ((0, LANES - D_AAA_LORA), (0, 0))).astype(BF16)
        g2p = jnp.pad(g2[l], ((0, LORA_PAD - XG_OFF - D_GATE_LORA), (0, 0))).astype(BF16)
        g1 = norm1_g[l][None, :]

        row512 = lambda i: (i, 0)
        tok_a = pl.BlockSpec((tm_a, A_WIDTH), row512)
        tok_d = pl.BlockSpec((tm_a, d), row512)
        vec_a = _full((1, A_WIDTH))
        bs_full = jnp.repeat(b_s[l].T, B_GROUP_CH, axis=1)
        outs = pl.pallas_call(
            functools.partial(_inproj_kernel, seq // tm_a),
            grid=(n_tok // tm_a,),
            in_specs=[tok_d,
                      _full((1, d)), _full((A_PROJ, d)), _full((1, A_PROJ)), vec_a,
                      _full((LANES, A_WIDTH)), vec_a, _full((LANES, A_WIDTH)),
                      _full((LORA_PAD - XG_OFF, A_WIDTH)), vec_a, vec_a,
                      _full((A_WIDTH, A_WIDTH)),
                      _full((b_cols, d)), _full((2 * d, d)), _full((1, 2 * d)),
                      _full((1, B_WIDTH)), _full((1, B_WIDTH)),
                      _full((B_GROUPS, GMLP_BLOCK, GMLP_BLOCK)), _full((GMLP_BLOCK, B_WIDTH)),
                      _full((B_WIDTH, d))],
            out_specs=[tok_a] * 7 + [tok_d, tok_d],
            out_shape=[jax.ShapeDtypeStruct((n_tok, A_WIDTH), F32)] * 7
            + [jax.ShapeDtypeStruct((n_tok, d), BF16)] * 2,
            scratch_shapes=[pltpu.VMEM((8, A_PROJ), F32)],
            compiler_params=_params(("arbitrary",)),
            name="inproj",
        )(xf, g1, w_a, mu_a, w0[l][None, :], w2p, a0[l][None, :], a2p, g2p,
          k_k[l][None, :], k_a[l][None, :], ones_bd, w_b, w_g, b_gate[l][None, :],
          lnv_g[l][None, :], lnv_b[l][None, :], w_s[l], bs_full, w_oB[l].astype(BF16))
        r_, k_, v_, a_, b_, ld_, gg_, ybg, ga = outs

        n_pairs = A_WIDTH // HEAD_PAIR
        n_chunks = WKV_TOKENS // WKV_CHUNK
        tok_w = pl.BlockSpec((WKV_SEQS, WKV_TOKENS, A_WIDTH), lambda bi, ti: (bi, ti, 0))
        vec_w = _full((1, A_WIDTH))
        seq_major = lambda t: t.reshape(bsz, seq, A_WIDTH)
        tok_scratch = pltpu.VMEM((WKV_SEQS, WKV_TOKENS, A_WIDTH), BF16)
        ya_in = pl.pallas_call(
            _wkv_kernel,
            grid=(bsz // WKV_SEQS, seq // WKV_TOKENS),
            in_specs=[tok_w] * 7 + [vec_w, vec_w, vec_w, _full((HEAD_PAIR, HEAD_PAIR))],
            out_specs=tok_w,
            out_shape=jax.ShapeDtypeStruct((bsz, seq, A_WIDTH), BF16),
            scratch_shapes=[pltpu.VMEM((WKV_SEQS, n_pairs, HEAD_PAIR, HEAD_PAIR), F32)]
            + [tok_scratch] * 5
            + [pltpu.VMEM((WKV_SEQS, n_chunks, n_pairs, HEAD_PAIR, HEAD_PAIR), BF16),
               pltpu.VMEM((WKV_SEQS, n_chunks, n_pairs, HEAD_PAIR, HEAD_PAIR), F32),
               pltpu.VMEM((WKV_SEQS, WKV_TOKENS, A_WIDTH), F32)],
            compiler_params=_params(("parallel", "arbitrary")),
            name="wkv",
        )(*[seq_major(t) for t in (r_, k_, v_, a_, b_, ld_, gg_)], r_k[l].reshape(1, A_WIDTH),
          lnx_g[l][None, :], lnx_b[l][None, :],
          ones_bd[:HEAD_PAIR, :HEAD_PAIR]).reshape(n_tok, A_WIDTH)

        w_r = _place(LANES, [(0, jnp.transpose(w_re[l], (1, 0, 2)).reshape(d, N_EXPERTS)),
                             (N_EXPERTS, w_rg[l])])
        b_r = _place(LANES, [(0, b_re[l].reshape(1, N_EXPERTS)), (N_EXPERTS, b_rg[l][None, :])])
        wr_hi = w_r.astype(BF16)
        wr_lo = (w_r - wr_hi.astype(F32)).astype(BF16)
        wr_t = jnp.concatenate([wr_hi.T, wr_lo.T], axis=0)
        tok_p = pl.BlockSpec((tm_p, d), row512)
        lane_p = pl.BlockSpec((tm_p, LANES), row512)
        n_sub = n_tok // MOE_SUB
        sorted_rows = tm_p // MOE_SUB * RUN_SLOTS
        x2, h_sorted, route, cnt = pl.pallas_call(
            _post_kernel,
            grid=(n_tok // tm_p,),
            in_specs=[tok_p, pl.BlockSpec((tm_p, A_WIDTH), row512), tok_p, tok_p,
                      _full((A_WIDTH, d)), _full((d, d)), _full((1, d)), _full((2 * LANES, d)),
                      _full((LANES, 1))],
            out_specs=[tok_p, pl.BlockSpec((sorted_rows, d), row512), lane_p,
                       pl.BlockSpec((8, LANES), row512)],
            out_shape=[jax.ShapeDtypeStruct((n_tok, d), F32),
                       jax.ShapeDtypeStruct((n_sub * RUN_SLOTS, d), BF16),
                       jax.ShapeDtypeStruct((n_tok, LANES), F32),
                       jax.ShapeDtypeStruct((n_tok // tm_p * 8, LANES), F32)],
            compiler_params=_params(("parallel",)),
            name="post",
        )(xf, ya_in, ga, ybg, w_oA[l].astype(BF16), w_out[l].astype(BF16),
          norm2_g[l][None, :], wr_t, b_r.T)

        run_chunks = cnt.reshape(n_tok // tm_p, 8, LANES)[:, :tm_p // MOE_SUB, :N_EXPERTS]
        run_chunks = run_chunks.reshape(n_sub, N_EXPERTS).astype(jnp.int32)
        run_start = (jnp.cumsum(run_chunks, axis=1) - run_chunks) * RUN_ALIGN
        run_row = (jnp.arange(n_sub, dtype=jnp.int32)[:, None] * RUN_SLOTS + run_start).T.reshape(-1)
        run_n = run_chunks.T.reshape(-1)
        run_end = jnp.cumsum(run_n)
        first_chunk = jnp.concatenate([jnp.zeros((1,), jnp.int32), run_end[n_sub - 1::n_sub]])
        pos = jnp.arange(n_sub * RUN_SLOTS // RUN_ALIGN, dtype=jnp.int32)
        run_of = jnp.sum(pos[:, None] >= run_end[None, :], axis=1)
        in_run = run_of[:, None] == jnp.arange(run_n.shape[0], dtype=jnp.int32)[None, :]
        pick = lambda t: jnp.sum(jnp.where(in_run, t[None, :], 0), axis=1)
        chunk_rows = pick(run_row) + (pos - pick(run_end - run_n)) * RUN_ALIGN

        per_expert = lambda e, *_: (e, 0, 0)
        y_sorted = pl.pallas_call(
            _expert_kernel,
            grid_spec=pltpu.PrefetchScalarGridSpec(
                num_scalar_prefetch=2, grid=(N_EXPERTS,),
                in_specs=[pl.BlockSpec(memory_space=pl.ANY),
                          pl.BlockSpec((1, d, D_EXPERT), per_expert),
                          pl.BlockSpec((1, d, D_EXPERT), per_expert),
                          pl.BlockSpec((1, D_EXPERT, d), per_expert)],
                out_specs=pl.BlockSpec(memory_space=pl.ANY),
                scratch_shapes=[pltpu.VMEM((2, EXPERT_ROWS, d), BF16),
                                pltpu.VMEM((d, 2 * D_EXPERT), BF16),
                                pltpu.VMEM((D_EXPERT, d), BF16),
                                pltpu.SemaphoreType.DMA((4,)),
                                pltpu.SMEM((1,), jnp.int32)]),
            out_shape=jax.ShapeDtypeStruct((n_sub * RUN_SLOTS, d), BF16),
            input_output_aliases={2: 0},
            compiler_params=_params(("arbitrary",)),
            name="moe_experts",
        )(chunk_rows, first_chunk, h_sorted, w_e_gate[l], w_e_up[l], w_e_down[l])

        xf = pl.pallas_call(
            _combine_kernel,
            grid=(n_tok // tm_p,),
            in_specs=[tok_p, lane_p, _full((1, d)), pl.BlockSpec((sorted_rows, d), row512)],
            out_specs=tok_p,
            out_shape=jax.ShapeDtypeStruct((n_tok, d), F32),
            compiler_params=_params(("parallel",)),
            name="moe_combine",
        )(x2, route, final_g[None, :], y_sorted)

    return xf.reshape(bsz, seq, d)
```

```python
import functools

import jax
import jax.numpy as jnp
from jax import lax
from jax.experimental import pallas as pl
from jax.experimental.pallas import tpu as pltpu

F32 = jnp.float32
BF16 = jnp.bfloat16

D_MODEL = 1024
A_WIDTH = 512
A_HEAD = 64
D_DECAY_LORA = 64
D_AAA_LORA = 64
D_GATE_LORA = 160
B_WIDTH = 512
B_GROUPS = 4
B_GROUP_CH = 128
GMLP_BLOCK = 128
N_GROUPS = 4
EXPERTS_PER_GROUP = 8
N_EXPERTS = 32
D_EXPERT = 256
NORM_EPS = 1e-6
LN_EPS = 1e-5
LNX_EPS = 64e-5

LANES = 128
LORA_PAD = 512
XW_OFF, XA_OFF, XG_OFF = 0, 128, 256
A_PROJ = 3 * A_WIDTH + LORA_PAD
WKV_CHUNK = 64
HEAD_PAIR = 2 * A_HEAD
WKV_SEQS = 8
WKV_TOKENS = 64
WKV_PREP_GROUP = 4
WKV_NORM_GROUP = 1
MOE_SUB = 256
RUN_ALIGN = 16
RUN_SLOTS = 2 * MOE_SUB + N_EXPERTS * RUN_ALIGN
EXPERT_ROWS = 2048
EXPERT_TILE = 512
INPROJ_TILE = 512
POST_TILE = 512
COMBINE_TILE = 1024
DENSE_SPLIT = 256
CHUNK_UNROLL = 4
VMEM_LIMIT = 48 * 1024 * 1024


def _rms(x, g):
    return x * lax.rsqrt(jnp.mean(x * x, axis=-1, keepdims=True) + NORM_EPS) * g


def _dot(a, b):
    return jnp.dot(a, b, preferred_element_type=F32)


def _dot_nt(a, b):
    return lax.dot_general(a, b, (((1,), (1,)), ((), ())), preferred_element_type=F32)


def _split2(x):
    hi = x.astype(BF16)
    lo = (x - hi.astype(F32)).astype(BF16)
    return hi, lo


def _split3(x):
    hi = x.astype(BF16)
    r1 = x - hi.astype(F32)
    mid = r1.astype(BF16)
    lo = (r1 - mid.astype(F32)).astype(BF16)
    return hi, mid, lo


def _pair_head_sums(xs, ones_pair):
    n_tiles = A_WIDTH // HEAD_PAIR
    rows = xs[0].shape[0]
    tiles = [x[:, t * HEAD_PAIR:(t + 1) * HEAD_PAIR].astype(BF16) for x in xs for t in range(n_tiles)]
    sums = _dot(jnp.concatenate(tiles, axis=0), ones_pair)
    return [jnp.concatenate([sums[(i * n_tiles + t) * rows:(i * n_tiles + t + 1) * rows]
                             for t in range(n_tiles)], axis=1) for i in range(len(xs))]


def _w_in_kernel(w_ref, wa_out, wb_out, wg_out):
    s_rkv = 3 * A_WIDTH
    s_w = s_rkv + D_DECAY_LORA
    s_a = s_w + D_AAA_LORA
    a_cols = s_a + D_GATE_LORA
    b_end = a_cols + 2 * B_WIDTH
    wa_out[...] = jnp.zeros_like(wa_out)
    wa_out[:s_w, :] = w_ref[:s_w, :].astype(BF16)
    wa_out[s_rkv + XA_OFF:s_rkv + XA_OFF + D_AAA_LORA, :] = w_ref[s_w:s_a, :].astype(BF16)
    wa_out[s_rkv + XG_OFF:s_rkv + XG_OFF + D_GATE_LORA, :] = w_ref[s_a:a_cols, :].astype(BF16)
    wb_out[...] = w_ref[a_cols:b_end, :].astype(BF16)
    wg_out[...] = w_ref[b_end:, :].astype(BF16)


def _inproj_kernel(tiles_per_seq, x_ref, g1_ref, wa_ref, mu_ref, w0_ref, w2_ref, a0_ref, a2_ref,
                   g2_ref, kk_ref, ka_ref, ones_ref, wb_ref, wg_ref, bg_ref, lng_ref, lnb_ref,
                   ws_ref, bs_ref, wo_ref,
                   r_out, k_out, v_out, a_out, b_out, ld_out, g_out, ybg_out, ga_out, tail_ref):
    i = pl.program_id(0)
    tm = x_ref.shape[0]
    parts = [slice(j * DENSE_SPLIT, (j + 1) * DENSE_SPLIT) for j in range(tm // DENSE_SPLIT)]
    g1 = g1_ref[...]
    h = [_rms(x_ref[p, :], g1).astype(BF16) for p in parts]
    proj = [_dot_nt(hh, wa_ref[...]) for hh in h]
    pb = [_dot_nt(hh, wb_ref[...]) for hh in h]

    prev = [jnp.where(i % tiles_per_seq == 0, 0.0, tail_ref[7:8, :])]
    prev += [p[DENSE_SPLIT - 1:, :] for p in proj[:-1]]
    tail_ref[...] = proj[-1][DENSE_SPLIT - 8:, :]
    row = lax.broadcasted_iota(jnp.int32, proj[0].shape, 0)
    pm = []
    for p, pv in zip(proj, prev):
        shifted = jnp.where(row == 0, pv, pltpu.roll(p, 1, axis=0))
        pm.append(p + mu_ref[...] * (shifted - p))
    gates = [jax.nn.sigmoid(_dot_nt(hh, wg_ref[...]) + bg_ref[...]) for hh in h]

    tri = (lax.broadcasted_iota(jnp.int32, (GMLP_BLOCK, GMLP_BLOCK), 0)
           >= lax.broadcasted_iota(jnp.int32, (GMLP_BLOCK, GMLP_BLOCK), 1))
    ws = [jnp.where(tri, ws_ref[grp], 0.0).astype(BF16) for grp in range(B_GROUPS)]
    bs = bs_ref[...]
    us, vns = [], []
    for x in pb:
        z = 0.5 * x * (1.0 + lax.erf(x * (2.0 ** -0.5)))
        us.append(z[:, :B_WIDTH])
        v = z[:, B_WIDTH:]
        mean = jnp.mean(v, axis=-1, keepdims=True)
        vc = v - mean
        var = jnp.mean(vc * vc, axis=-1, keepdims=True)
        vns.append((vc * lax.rsqrt(var + LN_EPS) * lng_ref[...] + lnb_ref[...]).astype(BF16))

    lora = [x[:, 3 * A_WIDTH:] for x in pm]
    decay = [_dot(jnp.tanh(x[:, XW_OFF:XW_OFF + LANES]).astype(BF16), w2_ref[...]) for x in lora]
    rate = [_dot(x[:, XA_OFF:XA_OFF + LANES].astype(BF16), a2_ref[...]) for x in lora]
    gate = [_dot(jax.nn.sigmoid(x[:, XG_OFF:]).astype(BF16), g2_ref[...]) for x in lora]
    kks = [x[:, A_WIDTH:2 * A_WIDTH] * kk_ref[...] for x in pm]
    sq = [_dot((kk * kk).astype(BF16), ones_ref[...]) for kk in kks]

    svs = []
    for vn in vns:
        rows = []
        for blk in range(DENSE_SPLIT // GMLP_BLOCK):
            cols = [_dot(ws[grp], vn[blk * GMLP_BLOCK:(blk + 1) * GMLP_BLOCK,
                                     grp * B_GROUP_CH:(grp + 1) * B_GROUP_CH])
                    for grp in range(B_GROUPS)]
            rows.append(jnp.concatenate(cols, axis=1) + bs)
        svs.append(jnp.concatenate(rows, axis=0))
    yb = [_dot((u * sv).astype(BF16), wo_ref[...]) for u, sv in zip(us, svs)]

    for j, p in enumerate(parts):
        k = pm[j][:, A_WIDTH:2 * A_WIDTH]
        z = -(w0_ref[...] + decay[j])
        softplus = jnp.maximum(z, 0.0) + jnp.log(1.0 + jnp.exp(-jnp.abs(z)))
        w = -softplus - 0.5
        a_lr = jax.nn.sigmoid(a0_ref[...] + rate[j])
        kk = kks[j] / jnp.maximum(jnp.sqrt(sq[j]), 1e-12)
        r_out[p, :] = pm[j][:, 0:A_WIDTH]
        k_out[p, :] = k * (1.0 + (a_lr - 1.0) * ka_ref[...])
        v_out[p, :] = pm[j][:, 2 * A_WIDTH:3 * A_WIDTH]
        a_out[p, :] = -kk
        b_out[p, :] = kk * a_lr
        ld_out[p, :] = -jnp.exp(w)
        g_out[p, :] = gate[j]
        ga_out[p, :] = gates[j][:, :D_MODEL].astype(BF16)
        ybg_out[p, :] = (gates[j][:, D_MODEL:] * yb[j]).astype(BF16)


def _wkv_kernel(r_ref, k_ref, v_ref, a_ref, b_ref, ld_ref, g_ref, rk_ref, lng_ref, lnb_ref,
                ones_ref, o_ref, st_ref, ta_s, tl_s, arb_s, ark_s, rt_s, bkt_s, dcol_s, y_s):
    C = WKV_CHUNK
    bb, tb, _ = r_ref.shape
    n_chunks = tb // C
    n_pairs = A_WIDTH // HEAD_PAIR

    @pl.when(pl.program_id(1) == 0)
    def _():
        st_ref[...] = jnp.zeros_like(st_ref)

    row = lax.broadcasted_iota(jnp.int32, (C, HEAD_PAIR), 0)
    src = lax.broadcasted_iota(jnp.int32, (C, HEAD_PAIR), 1) & (C - 1)
    incl = src <= row
    strict = src < row
    eye_pair = jnp.where(src == row, 1.0, 0.0)
    bd_mask = ((lax.broadcasted_iota(jnp.int32, (HEAD_PAIR, HEAD_PAIR), 0) >= A_HEAD)
               == (lax.broadcasted_iota(jnp.int32, (HEAD_PAIR, HEAD_PAIR), 1) >= A_HEAD))
    tri_c = (lax.broadcasted_iota(jnp.int32, (C, C), 0)
             >= lax.broadcasted_iota(jnp.int32, (C, C), 1)).astype(BF16)
    ones_pair = ones_ref[...]
    pair_cols = [slice(p * HEAD_PAIR, (p + 1) * HEAD_PAIR) for p in range(n_pairs)]

    def bd(x):
        xb = x.astype(BF16)
        return jnp.where(bd_mask, jnp.concatenate([xb, xb], axis=0), jnp.zeros((), BF16))

    def prep(it, carry):
        where, lhs, rhs, ats = [], [], [], []
        for j in range(WKV_PREP_GROUP):
            flat = it * WKV_PREP_GROUP + j
            b = flat // n_chunks
            c = flat % n_chunks
            sl = pl.ds(pl.multiple_of(c * C, C), C)
            ld = ld_ref[b, sl, :]
            hi, mid, lo = _split3(ld)
            cl = _dot(tri_c, hi) + _dot(tri_c, mid) + _dot(tri_c, lo)
            cl_end = cl[C - 1:C, :]
            k = k_ref[b, sl, :]
            bv = b_ref[b, sl, :]
            d_inv = jnp.exp(-cl)
            d_tail = jnp.exp(cl_end - cl)
            rt = (r_ref[b, sl, :] * jnp.exp(cl)).astype(BF16)
            rt_s[b, sl, :] = rt
            kt = k * d_inv
            bt = bv * d_inv
            at = a_ref[b, sl, :] * jnp.exp(cl - ld)
            kd = k * d_tail
            bdk = bv * d_tail
            d_end = jnp.exp(cl_end)
            for p, cs in enumerate(pair_cols):
                bkt_s[b, c, p] = jnp.concatenate([bdk[:, cs], kd[:, cs]], axis=0).T.astype(BF16)
                dcol_s[b, c, p] = jnp.broadcast_to(d_end[:, cs], (HEAD_PAIR, HEAD_PAIR)).T
                where.append((b, sl, cs))
                ats.append(at[:, cs])
                lhs.append(jnp.concatenate([rt[:, cs], at[:, cs].astype(BF16)], axis=0))
                rhs.append(jnp.concatenate([bd(kt[:, cs]), bd(bt[:, cs])], axis=0))
        n = len(where)
        amat = [_dot_nt(lhs[i], rhs[i]) for i in range(n)]
        l_ak, l_ab = [], []
        for i, (b, sl, cs) in enumerate(where):
            ark_s[b, sl, cs] = jnp.where(incl, amat[i][:C, :HEAD_PAIR], 0.0).astype(BF16)
            arb_s[b, sl, cs] = jnp.where(incl, amat[i][:C, HEAD_PAIR:], 0.0).astype(BF16)
            l_ak.append(jnp.where(strict, amat[i][C:, :HEAD_PAIR], 0.0))
            l_ab.append(jnp.where(strict, amat[i][C:, HEAD_PAIR:], 0.0))

        t_mat = [eye_pair + l for l in l_ab]
        q = [_dot(l.astype(BF16), bd(l)) for l in l_ab]
        n_sq = 1
        while 2 * n_sq < C // 2:
            both = [_dot(q[i].astype(BF16), jnp.concatenate([bd(q[i]), bd(t_mat[i])], axis=1))
                    for i in range(n)]
            q = [x[:, :HEAD_PAIR] for x in both]
            t_mat = [t_mat[i] + both[i][:, HEAD_PAIR:] for i in range(n)]
            n_sq *= 2
        t_mat = [t_mat[i] + _dot(q[i].astype(BF16), bd(t_mat[i])) for i in range(n)]
        tal = [_dot(t_mat[i].astype(BF16), jnp.concatenate([bd(ats[i]), bd(l_ak[i])], axis=1))
               for i in range(n)]
        for i, (b, sl, cs) in enumerate(where):
            ta_s[b, sl, cs] = tal[i][:, :HEAD_PAIR].astype(BF16)
            tl_s[b, sl, cs] = tal[i][:, HEAD_PAIR:].astype(BF16)
        return carry

    lax.fori_loop(0, bb * n_chunks // WKV_PREP_GROUP, prep, 0)

    def step(c, carry):
        sl = pl.ds(pl.multiple_of(c * C, C), C)
        chains = [(b, p) for b in range(bb) for p in range(n_pairs)]
        v = [v_ref[b, sl, :] for b in range(bb)]
        st = [st_ref[b, p] for b, p in chains]
        st_b = [x.astype(BF16) for x in st]
        bd_v = [bd(v[b][:, pair_cols[p]]) for b, p in chains]
        on_st = [_dot(jnp.concatenate([ta_s[b, sl, pair_cols[p]], rt_s[b, sl, pair_cols[p]]],
                                      axis=0), st_b[i]) for i, (b, p) in enumerate(chains)]
        on_v = [_dot(jnp.concatenate([tl_s[b, sl, pair_cols[p]], ark_s[b, sl, pair_cols[p]]],
                                     axis=0), bd_v[i]) for i, (b, p) in enumerate(chains)]
        u = [on_st[i][:C] + on_v[i][:C] for i in range(len(chains))]
        for i, (b, p) in enumerate(chains):
            uv = jnp.concatenate([u[i], v[b][:, pair_cols[p]]], axis=0).astype(BF16)
            st_ref[b, p] = (dcol_s[b, c, p] * st[i]
                            + jnp.where(bd_mask, _dot(bkt_s[b, c, p], uv), 0.0))
        ys = [on_st[i][C:] + on_v[i][C:] + _dot(arb_s[b, sl, pair_cols[p]], bd(u[i]))
              for i, (b, p) in enumerate(chains)]
        for b in range(bb):
            y_s[b, sl, :] = jnp.concatenate(ys[b * n_pairs:(b + 1) * n_pairs], axis=1)
        return carry

    lax.fori_loop(0, n_chunks, step, 0)

    def finish(it, carry):
        items = [(b, pl.ds(pl.multiple_of((it * WKV_NORM_GROUP + j) * C, C), C))
                 for j in range(WKV_NORM_GROUP) for b in range(bb)]
        y = [y_s[b, sl, :] for b, sl in items]
        rkr = [r_ref[b, sl, :] * k_ref[b, sl, :] * rk_ref[...] for b, sl in items]
        sums = [_pair_head_sums([y[i], rkr[i]], ones_pair) for i in range(len(items))]
        yc = [y[i] - sums[i][0] * (1.0 / A_HEAD) for i in range(len(items))]
        var = [_pair_head_sums([x * x], ones_pair)[0] * (1.0 / A_HEAD) for x in yc]
        for i, (b, sl) in enumerate(items):
            yn = yc[i] * lax.rsqrt(var[i] + LNX_EPS) * lng_ref[...] + lnb_ref[...]
            out = (yn + sums[i][1] * v_ref[b, sl, :]) * g_ref[b, sl, :]
            o_ref[b, sl, :] = out.astype(o_ref.dtype)
        return carry

    lax.fori_loop(0, n_chunks // WKV_NORM_GROUP, finish, 0)


def _post_kernel(x_ref, ya_ref, ga_ref, ybg_ref, woa_ref, wout_ref, g2_ref, wrt_ref, brt_ref,
                 x2_out, hl_out, route_out, cnt_out):
    n_sub = x_ref.shape[0] // MOE_SUB
    groups = [slice(g * MOE_SUB, (g + 1) * MOE_SUB) for g in range(n_sub)]
    y_a = [_dot(ya_ref[p, :].astype(BF16), woa_ref[...]) for p in groups]
    x2 = [x_ref[p, :] + _dot((ga_ref[p, :] * y + ybg_ref[p, :]).astype(BF16), wout_ref[...])
          for p, y in zip(groups, y_a)]
    h2 = [_rms(x, g2_ref[...]) for x in x2]
    split = [_split2(h) for h in h2]
    on_hi = [_dot_nt(wrt_ref[...], hi) for hi, _ in split]
    on_lo = [_dot_nt(wrt_ref[0:LANES, :], lo) for _, lo in split]
    row = lax.broadcasted_iota(jnp.int32, (LANES, MOE_SUB), 0)
    neg = jnp.float32(-jnp.inf)
    big = jnp.int32(LANES)
    is_grp = (row >= N_EXPERTS) & (row < N_EXPERTS + N_GROUPS)
    top = lambda v: jnp.max(v, axis=0, keepdims=True)
    first = lambda hit: jnp.min(jnp.where(hit, row, big), axis=0, keepdims=True)
    picks, onehots, weights = [], [], []
    for g, p in enumerate(groups):
        x2_out[p, :] = x2[g].astype(x2_out.dtype)
        logits = on_hi[g][:LANES, :] + on_hi[g][LANES:, :] + on_lo[g] + brt_ref[...]
        gl = jnp.where(is_grp, logits, neg)
        gmax = top(gl)
        g_p = 1.0 / jnp.sum(jnp.exp(gl - gmax), axis=0, keepdims=True)
        lo_row = (first(gl == gmax) - N_EXPERTS) * EXPERTS_PER_GROUP
        el = jnp.where((row >= lo_row) & (row < lo_row + EXPERTS_PER_GROUP), logits, neg)
        e1 = top(el)
        i1 = first(el == e1)
        el2 = jnp.where(row == i1, neg, el)
        e2 = top(el2)
        i2 = first(el2 == e2)
        t = jnp.exp(e2 - e1)
        weights.append((g_p / (1.0 + t), g_p * t / (1.0 + t)))
        picks.append((i1, i2))
        onehots.append(jnp.where((row == i1) | (row == i2), 1.0, 0.0))

    earlier = (lax.broadcasted_iota(jnp.int32, (MOE_SUB, MOE_SUB), 0)
               <= lax.broadcasted_iota(jnp.int32, (MOE_SUB, MOE_SUB), 1)).astype(BF16)
    incl = [_dot(oh.astype(BF16), earlier) for oh in onehots]
    chunks = [jnp.ceil(x[:, MOE_SUB - 1:] * (1.0 / RUN_ALIGN)) for x in incl]
    below = (lax.broadcasted_iota(jnp.int32, (LANES, LANES), 1)
             < lax.broadcasted_iota(jnp.int32, (LANES, LANES), 0)).astype(BF16)
    wide = [jnp.broadcast_to(c, (LANES, LANES)) for c in chunks]
    run_start = [_dot(below, w.astype(BF16))[:, 0:1] * RUN_ALIGN for w in wide]
    row8 = lax.broadcasted_iota(jnp.int32, (8, LANES), 0)
    counts = jnp.zeros((8, LANES), F32)
    slot_ids = lax.broadcasted_iota(jnp.int32, (RUN_SLOTS, MOE_SUB), 0)
    for g, p in enumerate(groups):
        counts = jnp.where(row8 == g, wide[g].T[0:8, :], counts)
        slot_of = run_start[g] + incl[g] - onehots[g]
        i1, i2 = picks[g]
        slot1 = jnp.sum(jnp.where(row == i1, slot_of, 0.0), axis=0, keepdims=True)
        slot2 = jnp.sum(jnp.where(row == i2, slot_of, 0.0), axis=0, keepdims=True)
        w1, w2 = weights[g]
        packed = jnp.where(row == 0, slot1, jnp.where(row == 1, slot2, jnp.where(
            row == 2, w1, jnp.where(row == 3, w2, 0.0))))
        route_out[p, :] = packed.T
        pick = jnp.where((slot_ids == slot1.astype(jnp.int32))
                         | (slot_ids == slot2.astype(jnp.int32)), 1.0, 0.0).astype(BF16)
        hl_out[g * RUN_SLOTS:(g + 1) * RUN_SLOTS, :] = _dot(pick, h2[g].astype(BF16)).astype(BF16)
    cnt_out[...] = counts


def _expert_kernel(chunk_s, first_s, hl_in_ref, wg_ref, wu_ref, wd_ref, hl_ref, xbuf, wgu_s, wd_s,
                   sem, drain_s):
    del hl_in_ref
    e = pl.program_id(0)
    pass_chunks = EXPERT_ROWS // RUN_ALIGN
    mine = e % 2
    other = 1 - mine
    gather_sem = lambda buf: sem.at[buf]
    scatter_sem = lambda buf: sem.at[2 + buf]

    def first_pass(ex):
        lo = first_s[ex]
        return lo, jnp.minimum(lo + pass_chunks, first_s[ex + 1])

    def for_chunks(buf, c_lo, c_hi, fn):
        def one(c):
            fn(hl_ref.at[pl.ds(pl.multiple_of(chunk_s[c], RUN_ALIGN), RUN_ALIGN)],
               xbuf.at[buf, pl.ds(pl.multiple_of((c - c_lo) * RUN_ALIGN, RUN_ALIGN), RUN_ALIGN)])

        def block(i, carry):
            for u in range(CHUNK_UNROLL):
                one(c_lo + i * CHUNK_UNROLL + u)
            return carry

        def tail(c, carry):
            one(c)
            return carry

        n_blocks = (c_hi - c_lo) // CHUNK_UNROLL
        lax.fori_loop(0, n_blocks, block, 0)
        lax.fori_loop(c_lo + n_blocks * CHUNK_UNROLL, c_hi, tail, 0)

    def gather(buf, c_lo, c_hi):
        for_chunks(buf, c_lo, c_hi, lambda hbm, vm: pltpu.make_async_copy(
            hbm, vm, gather_sem(buf)).start())

    def scatter(buf, c_lo, c_hi):
        for_chunks(buf, c_lo, c_hi, lambda hbm, vm: pltpu.make_async_copy(
            vm, hbm, scatter_sem(buf)).start())

    def wait_chunks(which_sem, n):
        def wait_rows(rows):
            def body(c, carry):
                pltpu.make_async_copy(hl_ref.at[pl.ds(0, rows)], xbuf.at[0, pl.ds(0, rows)],
                                      which_sem).wait()
                return carry
            return body

        lax.fori_loop(0, n // CHUNK_UNROLL, wait_rows(CHUNK_UNROLL * RUN_ALIGN), 0)
        lax.fori_loop(0, n % CHUNK_UNROLL, wait_rows(RUN_ALIGN), 0)

    def evaluate(buf, n_chunks):
        def tile(k, carry):
            sl = pl.ds(pl.multiple_of(k * EXPERT_TILE, EXPERT_TILE), EXPERT_TILE)
            gu = _dot(xbuf[buf, sl, :], wgu_s[...])
            gate = gu[:, :D_EXPERT]
            act = gate * jax.nn.sigmoid(gate) * gu[:, D_EXPERT:]
            xbuf[buf, sl, :] = _dot(act.astype(BF16), wd_s[...]).astype(BF16)
            return carry

        lax.fori_loop(0, (n_chunks * RUN_ALIGN + EXPERT_TILE - 1) // EXPERT_TILE, tile, 0)

    @pl.when(e == 0)
    def _():
        xbuf[...] = jnp.zeros_like(xbuf)
        gather(0, *first_pass(0))
        drain_s[0] = 0

    wgu_s[:, :D_EXPERT] = wg_ref[0].astype(BF16)
    wgu_s[:, D_EXPERT:] = wu_ref[0].astype(BF16)
    wd_s[...] = wd_ref[0].astype(BF16)

    c_lo, c_hi = first_pass(e)
    wait_chunks(gather_sem(mine), c_hi - c_lo)
    wait_chunks(scatter_sem(other), drain_s[0])

    @pl.when(e + 1 < pl.num_programs(0))
    def _():
        gather(other, *first_pass(e + 1))

    evaluate(mine, c_hi - c_lo)
    scatter(mine, c_lo, c_hi)

    def later_pass(state):
        lo, pending = state
        hi = jnp.minimum(lo + pass_chunks, first_s[e + 1])
        wait_chunks(scatter_sem(mine), pending)
        gather(mine, lo, hi)
        wait_chunks(gather_sem(mine), hi - lo)
        evaluate(mine, hi - lo)
        scatter(mine, lo, hi)
        return hi, hi - lo

    _, pending = lax.while_loop(lambda st: st[0] < first_s[e + 1], later_pass,
                                (c_hi, c_hi - c_lo))
    drain_s[0] = pending

    @pl.when(e + 1 == pl.num_programs(0))
    def _():
        wait_chunks(scatter_sem(mine), pending)


def _combine_kernel(x2_ref, route_ref, gf_ref, yl_ref, o_ref):
    slots = lax.broadcasted_iota(jnp.int32, (MOE_SUB, RUN_SLOTS), 1).astype(F32)
    for g in range(x2_ref.shape[0] // MOE_SUB):
        p = slice(g * MOE_SUB, (g + 1) * MOE_SUB)
        route = route_ref[p, :]
        pick = (jnp.where(slots == route[:, 0:1], route[:, 2:3], 0.0)
                + jnp.where(slots == route[:, 1:2], route[:, 3:4], 0.0))
        y = _dot(pick.astype(BF16), yl_ref[g * RUN_SLOTS:(g + 1) * RUN_SLOTS, :])
        o_ref[p, :] = _rms(x2_ref[p, :] + y, gf_ref[...])


def _full(shape):
    return pl.BlockSpec(shape, lambda *_: (0,) * len(shape))


def _params(sem):
    return pltpu.CompilerParams(dimension_semantics=sem, vmem_limit_bytes=VMEM_LIMIT)


def _mixer_a_layout(t):
    s_rkv = 3 * A_WIDTH
    s_w = s_rkv + D_DECAY_LORA
    s_a = s_w + D_AAA_LORA
    gap = lambda n: jnp.zeros((t.shape[0], n), t.dtype)
    return jnp.concatenate(
        [t[:, :s_w], gap(XA_OFF - D_DECAY_LORA), t[:, s_w:s_a], gap(XG_OFF - XA_OFF - D_AAA_LORA),
         t[:, s_a:], gap(LORA_PAD - XG_OFF - D_GATE_LORA)], axis=1)


def _place(cols, parts):
    out = jnp.zeros((parts[0][1].shape[0], cols), parts[0][1].dtype)
    for off, arr in parts:
        out = lax.dynamic_update_slice(out, arr, (0, off))
    return out


def kernel(x, norm1_g, w_in, b_gate, tmix_mu, w0, w2, a0, a2, g2, k_k, k_a, r_k, lnx_g, lnx_b,
           w_oA, lnv_g, lnv_b, w_s, b_s, w_oB, w_out, norm2_g, w_rg, b_rg, w_re, b_re,
           w_e_gate, w_e_up, w_e_down, final_g):
    bsz, seq, d = x.shape
    n_tok = bsz * seq
    depth = norm1_g.shape[0]
    assert depth == 1, "the moe kernel fuses the final norm, so it must be the last layer"
    assert bsz % WKV_SEQS == 0 and seq % WKV_TOKENS == 0
    xf = x.reshape(n_tok, d)

    s_rkv = 3 * A_WIDTH
    s_w = s_rkv + D_DECAY_LORA
    s_a = s_w + D_AAA_LORA
    a_cols = s_a + D_GATE_LORA
    b_cols = 2 * B_WIDTH

    ones_bd = (jnp.arange(A_WIDTH)[:, None] // A_HEAD
               == jnp.arange(A_WIDTH)[None, :] // A_HEAD).astype(BF16)

    tm_a = INPROJ_TILE
    tm_p = POST_TILE
    tm_c = COMBINE_TILE
    assert tm_p % MOE_SUB == 0 and tm_c % MOE_SUB == 0 and tm_a % DENSE_SPLIT == 0

    for l in range(depth):
        wl = w_in[l]
        rows_w = 256
        w_a, w_b, w_g = pl.pallas_call(
            _w_in_kernel,
            grid=(d // rows_w,),
            in_specs=[pl.BlockSpec((wl.shape[1], rows_w), lambda i: (0, i))],
            out_specs=[pl.BlockSpec((A_PROJ, rows_w), lambda i: (0, i)),
                       pl.BlockSpec((b_cols, rows_w), lambda i: (0, i)),
                       pl.BlockSpec((2 * d, rows_w), lambda i: (0, i))],
            out_shape=[jax.ShapeDtypeStruct((A_PROJ, d), BF16),
                       jax.ShapeDtypeStruct((b_cols, d), BF16),
                       jax.ShapeDtypeStruct((2 * d, d), BF16)],
            compiler_params=_params(("parallel",)),
            name="w_in_layout",
        )(wl.T)
        mu_a = _mixer_a_layout(tmix_mu[l][None, :])
        w2p = jnp.pad(w2[l], ((0, LANES - D_DECAY_LORA), (0, 0))).astype(BF16)
        a2p = jnp.pad(a2[l], ((0, LANES - D_AAA_LORA), (0, 0))).astype(BF16)
        g2p = jnp.pad(g2[l], ((0, LORA_PAD - XG_OFF - D_GATE_LORA), (0, 0))).astype(BF16)
        g1 = norm1_g[l][None, :]

        row512 = lambda i: (i, 0)
        tok_a = pl.BlockSpec((tm_a, A_WIDTH), row512)
        tok_d = pl.BlockSpec((tm_a, d), row512)
        vec_a = _full((1, A_WIDTH))
        bs_full = jnp.repeat(b_s[l].T, B_GROUP_CH, axis=1)
        outs = pl.pallas_call(
            functools.partial(_inproj_kernel, seq // tm_a),
            grid=(n_tok // tm_a,),
            in_specs=[tok_d,
                      _full((1, d)), _full((A_PROJ, d)), _full((1, A_PROJ)), vec_a,
                      _full((LANES, A_WIDTH)), vec_a, _full((LANES, A_WIDTH)),
                      _full((LORA_PAD - XG_OFF, A_WIDTH)), vec_a, vec_a,
                      _full((A_WIDTH, A_WIDTH)),
                      _full((b_cols, d)), _full((2 * d, d)), _full((1, 2 * d)),
                      _full((1, B_WIDTH)), _full((1, B_WIDTH)),
                      _full((B_GROUPS, GMLP_BLOCK, GMLP_BLOCK)), _full((GMLP_BLOCK, B_WIDTH)),
                      _full((B_WIDTH, d))],
            out_specs=[tok_a] * 7 + [tok_d, tok_d],
            out_shape=[jax.ShapeDtypeStruct((n_tok, A_WIDTH), F32)] * 7
            + [jax.ShapeDtypeStruct((n_tok, d), BF16)] * 2,
            scratch_shapes=[pltpu.VMEM((8, A_PROJ), F32)],
            compiler_params=_params(("arbitrary",)),
            name="inproj",
        )(xf, g1, w_a, mu_a, w0[l][None, :], w2p, a0[l][None, :], a2p, g2p,
          k_k[l][None, :], k_a[l][None, :], ones_bd, w_b, w_g, b_gate[l][None, :],
          lnv_g[l][None, :], lnv_b[l][None, :], w_s[l], bs_full, w_oB[l].astype(BF16))
        r_, k_, v_, a_, b_, ld_, gg_, ybg, ga = outs

        n_pairs = A_WIDTH // HEAD_PAIR
        n_chunks = WKV_TOKENS // WKV_CHUNK
        tok_w = pl.BlockSpec((WKV_SEQS, WKV_TOKENS, A_WIDTH), lambda bi, ti: (bi, ti, 0))
        vec_w = _full((1, A_WIDTH))
        seq_major = lambda t: t.reshape(bsz, seq, A_WIDTH)
        tok_scratch = pltpu.VMEM((WKV_SEQS, WKV_TOKENS, A_WIDTH), BF16)
        ya_in = pl.pallas_call(
            _wkv_kernel,
            grid=(bsz // WKV_SEQS, seq // WKV_TOKENS),
            in_specs=[tok_w] * 7 + [vec_w, vec_w, vec_w, _full((HEAD_PAIR, HEAD_PAIR))],
            out_specs=tok_w,
            out_shape=jax.ShapeDtypeStruct((bsz, seq, A_WIDTH), BF16),
            scratch_shapes=[pltpu.VMEM((WKV_SEQS, n_pairs, HEAD_PAIR, HEAD_PAIR), F32)]
            + [tok_scratch] * 5
            + [pltpu.VMEM((WKV_SEQS, n_chunks, n_pairs, HEAD_PAIR, HEAD_PAIR), BF16),
               pltpu.VMEM((WKV_SEQS, n_chunks, n_pairs, HEAD_PAIR, HEAD_PAIR), F32),
               pltpu.VMEM((WKV_SEQS, WKV_TOKENS, A_WIDTH), F32)],
            compiler_params=_params(("parallel", "arbitrary")),
            name="wkv",
        )(*[seq_major(t) for t in (r_, k_, v_, a_, b_, ld_, gg_)], r_k[l].reshape(1, A_WIDTH),
          lnx_g[l][None, :], lnx_b[l][None, :],
          ones_bd[:HEAD_PAIR, :HEAD_PAIR]).reshape(n_tok, A_WIDTH)

        w_r = _place(LANES, [(0, jnp.transpose(w_re[l], (1, 0, 2)).reshape(d, N_EXPERTS)),
                             (N_EXPERTS, w_rg[l])])
        b_r = _place(LANES, [(0, b_re[l].reshape(1, N_EXPERTS)), (N_EXPERTS, b_rg[l][None, :])])
        wr_hi = w_r.astype(BF16)
        wr_lo = (w_r - wr_hi.astype(F32)).astype(BF16)
        wr_t = jnp.concatenate([wr_hi.T, wr_lo.T], axis=0)
        tok_p = pl.BlockSpec((tm_p, d), row512)
        lane_p = pl.BlockSpec((tm_p, LANES), row512)
        n_sub = n_tok // MOE_SUB
        sorted_rows = tm_p // MOE_SUB * RUN_SLOTS
        x2, h_sorted, route, cnt = pl.pallas_call(
            _post_kernel,
            grid=(n_tok // tm_p,),
            in_specs=[tok_p, pl.BlockSpec((tm_p, A_WIDTH), row512), tok_p, tok_p,
                      _full((A_WIDTH, d)), _full((d, d)), _full((1, d)), _full((2 * LANES, d)),
                      _full((LANES, 1))],
            out_specs=[tok_p, pl.BlockSpec((sorted_rows, d), row512), lane_p,
                       pl.BlockSpec((8, LANES), row512)],
            out_shape=[jax.ShapeDtypeStruct((n_tok, d), F32),
                       jax.ShapeDtypeStruct((n_sub * RUN_SLOTS, d), BF16),
                       jax.ShapeDtypeStruct((n_tok, LANES), F32),
                       jax.ShapeDtypeStruct((n_tok // tm_p * 8, LANES), F32)],
            compiler_params=_params(("parallel",)),
            name="post",
        )(xf, ya_in, ga, ybg, w_oA[l].astype(BF16), w_out[l].astype(BF16),
          norm2_g[l][None, :], wr_t, b_r.T)

        run_chunks = cnt.reshape(n_tok // tm_p, 8, LANES)[:, :tm_p // MOE_SUB, :N_EXPERTS]
        run_chunks = run_chunks.reshape(n_sub, N_EXPERTS).astype(jnp.int32)
        run_start = (jnp.cumsum(run_chunks, axis=1) - run_chunks) * RUN_ALIGN
        run_row = (jnp.arange(n_sub, dtype=jnp.int32)[:, None] * RUN_SLOTS + run_start).T.reshape(-1)
        run_n = run_chunks.T.reshape(-1)
        run_end = jnp.cumsum(run_n)
        first_chunk = jnp.concatenate([jnp.zeros((1,), jnp.int32), run_end[n_sub - 1::n_sub]])
        pos = jnp.arange(n_sub * RUN_SLOTS // RUN_ALIGN, dtype=jnp.int32)
        run_of = jnp.sum(pos[:, None] >= run_end[None, :], axis=1)
        in_run = run_of[:, None] == jnp.arange(run_n.shape[0], dtype=jnp.int32)[None, :]
        pick = lambda t: jnp.sum(jnp.where(in_run, t[None, :], 0), axis=1)
        chunk_rows = pick(run_row) + (pos - pick(run_end - run_n)) * RUN_ALIGN

        per_expert = lambda e, *_: (e, 0, 0)
        y_sorted = pl.pallas_call(
            _expert_kernel,
            grid_spec=pltpu.PrefetchScalarGridSpec(
                num_scalar_prefetch=2, grid=(N_EXPERTS,),
                in_specs=[pl.BlockSpec(memory_space=pl.ANY),
                          pl.BlockSpec((1, d, D_EXPERT), per_expert),
                          pl.BlockSpec((1, d, D_EXPERT), per_expert),
                          pl.BlockSpec((1, D_EXPERT, d), per_expert)],
                out_specs=pl.BlockSpec(memory_space=pl.ANY),
                scratch_shapes=[pltpu.VMEM((2, EXPERT_ROWS, d), BF16),
                                pltpu.VMEM((d, 2 * D_EXPERT), BF16),
                                pltpu.VMEM((D_EXPERT, d), BF16),
                                pltpu.SemaphoreType.DMA((4,)),
                                pltpu.SMEM((1,), jnp.int32)]),
            out_shape=jax.ShapeDtypeStruct((n_sub * RUN_SLOTS, d), BF16),
            input_output_aliases={2: 0},
            compiler_params=_params(("arbitrary",)),
            name="moe_experts",
        )(chunk_rows, first_chunk, h_sorted, w_e_gate[l], w_e_up[l], w_e_down[l])

        xf = pl.pallas_call(
            _combine_kernel,
            grid=(n_tok // tm_c,),
            in_specs=[pl.BlockSpec((tm_c, d), row512), pl.BlockSpec((tm_c, LANES), row512),
                      _full((1, d)), pl.BlockSpec((tm_c // MOE_SUB * RUN_SLOTS, d), row512)],
            out_specs=pl.BlockSpec((tm_c, d), row512),
            out_shape=jax.ShapeDtypeStruct((n_tok, d), F32),
            compiler_params=_params(("parallel",)),
            name="moe_combine",
        )(x2, route, final_g[None, :], y_sorted)

    return xf.reshape(bsz, seq, d)
```

```python
import functools

import jax
import jax.numpy as jnp
from jax import lax
from jax.experimental import pallas as pl
from jax.experimental.pallas import tpu as pltpu

F32 = jnp.float32
BF16 = jnp.bfloat16

D_MODEL = 1024
A_WIDTH = 512
A_HEAD = 64
D_DECAY_LORA = 64
D_AAA_LORA = 64
D_GATE_LORA = 160
B_WIDTH = 512
B_GROUPS = 4
B_GROUP_CH = 128
GMLP_BLOCK = 128
N_GROUPS = 4
EXPERTS_PER_GROUP = 8
N_EXPERTS = 32
D_EXPERT = 256
NORM_EPS = 1e-6
LN_EPS = 1e-5
LNX_EPS = 64e-5

LANES = 128
LORA_PAD = 512
XW_OFF, XA_OFF, XG_OFF = 0, 128, 256
A_PROJ = 3 * A_WIDTH + LORA_PAD
WKV_CHUNK = 64
HEAD_PAIR = 2 * A_HEAD
WKV_SEQS = 8
WKV_TOKENS = 64
WKV_PREP_GROUP = 4
WKV_NORM_GROUP = 1
MOE_SUB = 256
RUN_ALIGN = 16
RUN_SLOTS = 2 * MOE_SUB + N_EXPERTS * RUN_ALIGN
EXPERT_ROWS = 2048
EXPERT_TILE = 512
INPROJ_TILE = 512
POST_TILE = 1024
COMBINE_TILE = 1024
DENSE_SPLIT = 256
CHUNK_UNROLL = 4
VMEM_LIMIT = 56 * 1024 * 1024


def _rms(x, g):
    return x * lax.rsqrt(jnp.mean(x * x, axis=-1, keepdims=True) + NORM_EPS) * g


def _dot(a, b):
    return jnp.dot(a, b, preferred_element_type=F32)


def _dot_nt(a, b):
    return lax.dot_general(a, b, (((1,), (1,)), ((), ())), preferred_element_type=F32)


def _split2(x):
    hi = x.astype(BF16)
    lo = (x - hi.astype(F32)).astype(BF16)
    return hi, lo


def _split3(x):
    hi = x.astype(BF16)
    r1 = x - hi.astype(F32)
    mid = r1.astype(BF16)
    lo = (r1 - mid.astype(F32)).astype(BF16)
    return hi, mid, lo


def _pair_head_sums(xs, ones_pair):
    n_tiles = A_WIDTH // HEAD_PAIR
    rows = xs[0].shape[0]
    tiles = [x[:, t * HEAD_PAIR:(t + 1) * HEAD_PAIR].astype(BF16) for x in xs for t in range(n_tiles)]
    sums = _dot(jnp.concatenate(tiles, axis=0), ones_pair)
    return [jnp.concatenate([sums[(i * n_tiles + t) * rows:(i * n_tiles + t + 1) * rows]
                             for t in range(n_tiles)], axis=1) for i in range(len(xs))]


def _w_in_kernel(w_ref, wa_out, wb_out, wg_out):
    s_rkv = 3 * A_WIDTH
    s_w = s_rkv + D_DECAY_LORA
    s_a = s_w + D_AAA_LORA
    a_cols = s_a + D_GATE_LORA
    b_end = a_cols + 2 * B_WIDTH
    wa_out[...] = jnp.zeros_like(wa_out)
    wa_out[:s_w, :] = w_ref[:s_w, :].astype(BF16)
    wa_out[s_rkv + XA_OFF:s_rkv + XA_OFF + D_AAA_LORA, :] = w_ref[s_w:s_a, :].astype(BF16)
    wa_out[s_rkv + XG_OFF:s_rkv + XG_OFF + D_GATE_LORA, :] = w_ref[s_a:a_cols, :].astype(BF16)
    wb_out[...] = w_ref[a_cols:b_end, :].astype(BF16)
    wg_out[...] = w_ref[b_end:, :].astype(BF16)


def _inproj_kernel(tiles_per_seq, x_ref, g1_ref, wa_ref, mu_ref, w0_ref, w2_ref, a0_ref, a2_ref,
                   g2_ref, kk_ref, ka_ref, ones_ref, wb_ref, wg_ref, bg_ref, lng_ref, lnb_ref,
                   ws_ref, bs_ref, wo_ref,
                   r_out, k_out, v_out, a_out, b_out, ld_out, g_out, ybg_out, ga_out, tail_ref):
    i = pl.program_id(0)
    tm = x_ref.shape[0]
    parts = [slice(j * DENSE_SPLIT, (j + 1) * DENSE_SPLIT) for j in range(tm // DENSE_SPLIT)]
    g1 = g1_ref[...]
    h = [_rms(x_ref[p, :], g1).astype(BF16) for p in parts]
    proj = [_dot_nt(hh, wa_ref[...]) for hh in h]
    pb = [_dot_nt(hh, wb_ref[...]) for hh in h]

    prev = [jnp.where(i % tiles_per_seq == 0, 0.0, tail_ref[7:8, :])]
    prev += [p[DENSE_SPLIT - 1:, :] for p in proj[:-1]]
    tail_ref[...] = proj[-1][DENSE_SPLIT - 8:, :]
    row = lax.broadcasted_iota(jnp.int32, proj[0].shape, 0)
    pm = []
    for p, pv in zip(proj, prev):
        shifted = jnp.where(row == 0, pv, pltpu.roll(p, 1, axis=0))
        pm.append(p + mu_ref[...] * (shifted - p))
    gates = [jax.nn.sigmoid(_dot_nt(hh, wg_ref[...]) + bg_ref[...]) for hh in h]

    tri = (lax.broadcasted_iota(jnp.int32, (GMLP_BLOCK, GMLP_BLOCK), 0)
           >= lax.broadcasted_iota(jnp.int32, (GMLP_BLOCK, GMLP_BLOCK), 1))
    ws = [jnp.where(tri, ws_ref[grp], 0.0).astype(BF16) for grp in range(B_GROUPS)]
    bs = bs_ref[...]
    us, vns = [], []
    for x in pb:
        z = 0.5 * x * (1.0 + lax.erf(x * (2.0 ** -0.5)))
        us.append(z[:, :B_WIDTH])
        v = z[:, B_WIDTH:]
        mean = jnp.mean(v, axis=-1, keepdims=True)
        vc = v - mean
        var = jnp.mean(vc * vc, axis=-1, keepdims=True)
        vns.append((vc * lax.rsqrt(var + LN_EPS) * lng_ref[...] + lnb_ref[...]).astype(BF16))

    lora = [x[:, 3 * A_WIDTH:] for x in pm]
    decay = [_dot(jnp.tanh(x[:, XW_OFF:XW_OFF + LANES]).astype(BF16), w2_ref[...]) for x in lora]
    rate = [_dot(x[:, XA_OFF:XA_OFF + LANES].astype(BF16), a2_ref[...]) for x in lora]
    gate = [_dot(jax.nn.sigmoid(x[:, XG_OFF:]).astype(BF16), g2_ref[...]) for x in lora]
    kks = [x[:, A_WIDTH:2 * A_WIDTH] * kk_ref[...] for x in pm]
    sq = [_dot((kk * kk).astype(BF16), ones_ref[...]) for kk in kks]

    svs = []
    for vn in vns:
        rows = []
        for blk in range(DENSE_SPLIT // GMLP_BLOCK):
            cols = [_dot(ws[grp], vn[blk * GMLP_BLOCK:(blk + 1) * GMLP_BLOCK,
                                     grp * B_GROUP_CH:(grp + 1) * B_GROUP_CH])
                    for grp in range(B_GROUPS)]
            rows.append(jnp.concatenate(cols, axis=1) + bs)
        svs.append(jnp.concatenate(rows, axis=0))
    yb = [_dot((u * sv).astype(BF16), wo_ref[...]) for u, sv in zip(us, svs)]

    for j, p in enumerate(parts):
        k = pm[j][:, A_WIDTH:2 * A_WIDTH]
        z = -(w0_ref[...] + decay[j])
        softplus = jnp.maximum(z, 0.0) + jnp.log(1.0 + jnp.exp(-jnp.abs(z)))
        w = -softplus - 0.5
        a_lr = jax.nn.sigmoid(a0_ref[...] + rate[j])
        kk = kks[j] / jnp.maximum(jnp.sqrt(sq[j]), 1e-12)
        r_out[p, :] = pm[j][:, 0:A_WIDTH]
        k_out[p, :] = k * (1.0 + (a_lr - 1.0) * ka_ref[...])
        v_out[p, :] = pm[j][:, 2 * A_WIDTH:3 * A_WIDTH]
        a_out[p, :] = -kk
        b_out[p, :] = kk * a_lr
        ld_out[p, :] = -jnp.exp(w)
        g_out[p, :] = gate[j]
        ga_out[p, :] = gates[j][:, :D_MODEL].astype(BF16)
        ybg_out[p, :] = (gates[j][:, D_MODEL:] * yb[j]).astype(BF16)


def _wkv_kernel(r_ref, k_ref, v_ref, a_ref, b_ref, ld_ref, g_ref, rk_ref, lng_ref, lnb_ref,
                ones_ref, o_ref, st_ref, ta_s, tl_s, arb_s, ark_s, rt_s, bkt_s, dcol_s, y_s):
    C = WKV_CHUNK
    bb, tb, _ = r_ref.shape
    n_chunks = tb // C
    n_pairs = A_WIDTH // HEAD_PAIR

    @pl.when(pl.program_id(1) == 0)
    def _():
        st_ref[...] = jnp.zeros_like(st_ref)

    row = lax.broadcasted_iota(jnp.int32, (C, HEAD_PAIR), 0)
    src = lax.broadcasted_iota(jnp.int32, (C, HEAD_PAIR), 1) & (C - 1)
    incl = src <= row
    strict = src < row
    eye_pair = jnp.where(src == row, 1.0, 0.0)
    bd_mask = ((lax.broadcasted_iota(jnp.int32, (HEAD_PAIR, HEAD_PAIR), 0) >= A_HEAD)
               == (lax.broadcasted_iota(jnp.int32, (HEAD_PAIR, HEAD_PAIR), 1) >= A_HEAD))
    tri_c = (lax.broadcasted_iota(jnp.int32, (C, C), 0)
             >= lax.broadcasted_iota(jnp.int32, (C, C), 1)).astype(BF16)
    ones_pair = ones_ref[...]
    pair_cols = [slice(p * HEAD_PAIR, (p + 1) * HEAD_PAIR) for p in range(n_pairs)]

    def bd(x):
        xb = x.astype(BF16)
        return jnp.where(bd_mask, jnp.concatenate([xb, xb], axis=0), jnp.zeros((), BF16))

    def prep(it, carry):
        where, lhs, rhs, ats = [], [], [], []
        for j in range(WKV_PREP_GROUP):
            flat = it * WKV_PREP_GROUP + j
            b = flat // n_chunks
            c = flat % n_chunks
            sl = pl.ds(pl.multiple_of(c * C, C), C)
            ld = ld_ref[b, sl, :]
            hi, mid, lo = _split3(ld)
            cl = _dot(tri_c, hi) + _dot(tri_c, mid) + _dot(tri_c, lo)
            cl_end = cl[C - 1:C, :]
            k = k_ref[b, sl, :]
            bv = b_ref[b, sl, :]
            d_inv = jnp.exp(-cl)
            d_tail = jnp.exp(cl_end - cl)
            rt = (r_ref[b, sl, :] * jnp.exp(cl)).astype(BF16)
            rt_s[b, sl, :] = rt
            kt = k * d_inv
            bt = bv * d_inv
            at = a_ref[b, sl, :] * jnp.exp(cl - ld)
            kd = k * d_tail
            bdk = bv * d_tail
            d_end = jnp.exp(cl_end)
            for p, cs in enumerate(pair_cols):
                bkt_s[b, c, p] = jnp.concatenate([bdk[:, cs], kd[:, cs]], axis=0).T.astype(BF16)
                dcol_s[b, c, p] = jnp.broadcast_to(d_end[:, cs], (HEAD_PAIR, HEAD_PAIR)).T
                where.append((b, sl, cs))
                ats.append(at[:, cs])
                lhs.append(jnp.concatenate([rt[:, cs], at[:, cs].astype(BF16)], axis=0))
                rhs.append(jnp.concatenate([bd(kt[:, cs]), bd(bt[:, cs])], axis=0))
        n = len(where)
        amat = [_dot_nt(lhs[i], rhs[i]) for i in range(n)]
        l_ak, l_ab = [], []
        for i, (b, sl, cs) in enumerate(where):
            ark_s[b, sl, cs] = jnp.where(incl, amat[i][:C, :HEAD_PAIR], 0.0).astype(BF16)
            arb_s[b, sl, cs] = jnp.where(incl, amat[i][:C, HEAD_PAIR:], 0.0).astype(BF16)
            l_ak.append(jnp.where(strict, amat[i][C:, :HEAD_PAIR], 0.0))
            l_ab.append(jnp.where(strict, amat[i][C:, HEAD_PAIR:], 0.0))

        t_mat = [eye_pair + l for l in l_ab]
        q = [_dot(l.astype(BF16), bd(l)) for l in l_ab]
        n_sq = 1
        while 2 * n_sq < C // 2:
            both = [_dot(q[i].astype(BF16), jnp.concatenate([bd(q[i]), bd(t_mat[i])], axis=1))
                    for i in range(n)]
            q = [x[:, :HEAD_PAIR] for x in both]
            t_mat = [t_mat[i] + both[i][:, HEAD_PAIR:] for i in range(n)]
            n_sq *= 2
        t_mat = [t_mat[i] + _dot(q[i].astype(BF16), bd(t_mat[i])) for i in range(n)]
        tal = [_dot(t_mat[i].astype(BF16), jnp.concatenate([bd(ats[i]), bd(l_ak[i])], axis=1))
               for i in range(n)]
        for i, (b, sl, cs) in enumerate(where):
            ta_s[b, sl, cs] = tal[i][:, :HEAD_PAIR].astype(BF16)
            tl_s[b, sl, cs] = tal[i][:, HEAD_PAIR:].astype(BF16)
        return carry

    lax.fori_loop(0, bb * n_chunks // WKV_PREP_GROUP, prep, 0)

    def step(c, carry):
        sl = pl.ds(pl.multiple_of(c * C, C), C)
        chains = [(b, p) for b in range(bb) for p in range(n_pairs)]
        v = [v_ref[b, sl, :] for b in range(bb)]
        st = [st_ref[b, p] for b, p in chains]
        st_b = [x.astype(BF16) for x in st]
        bd_v = [bd(v[b][:, pair_cols[p]]) for b, p in chains]
        on_st = [_dot(jnp.concatenate([ta_s[b, sl, pair_cols[p]], rt_s[b, sl, pair_cols[p]]],
                                      axis=0), st_b[i]) for i, (b, p) in enumerate(chains)]
        on_v = [_dot(jnp.concatenate([tl_s[b, sl, pair_cols[p]], ark_s[b, sl, pair_cols[p]]],
                                     axis=0), bd_v[i]) for i, (b, p) in enumerate(chains)]
        u = [on_st[i][:C] + on_v[i][:C] for i in range(len(chains))]
        for i, (b, p) in enumerate(chains):
            uv = jnp.concatenate([u[i], v[b][:, pair_cols[p]]], axis=0).astype(BF16)
            st_ref[b, p] = (dcol_s[b, c, p] * st[i]
                            + jnp.where(bd_mask, _dot(bkt_s[b, c, p], uv), 0.0))
        ys = [on_st[i][C:] + on_v[i][C:] + _dot(arb_s[b, sl, pair_cols[p]], bd(u[i]))
              for i, (b, p) in enumerate(chains)]
        for b in range(bb):
            y_s[b, sl, :] = jnp.concatenate(ys[b * n_pairs:(b + 1) * n_pairs], axis=1)
        return carry

    lax.fori_loop(0, n_chunks, step, 0)

    def finish(it, carry):
        items = [(b, pl.ds(pl.multiple_of((it * WKV_NORM_GROUP + j) * C, C), C))
                 for j in range(WKV_NORM_GROUP) for b in range(bb)]
        y = [y_s[b, sl, :] for b, sl in items]
        rkr = [r_ref[b, sl, :] * k_ref[b, sl, :] * rk_ref[...] for b, sl in items]
        sums = [_pair_head_sums([y[i], rkr[i]], ones_pair) for i in range(len(items))]
        yc = [y[i] - sums[i][0] * (1.0 / A_HEAD) for i in range(len(items))]
        var = [_pair_head_sums([x * x], ones_pair)[0] * (1.0 / A_HEAD) for x in yc]
        for i, (b, sl) in enumerate(items):
            yn = yc[i] * lax.rsqrt(var[i] + LNX_EPS) * lng_ref[...] + lnb_ref[...]
            out = (yn + sums[i][1] * v_ref[b, sl, :]) * g_ref[b, sl, :]
            o_ref[b, sl, :] = out.astype(o_ref.dtype)
        return carry

    lax.fori_loop(0, n_chunks // WKV_NORM_GROUP, finish, 0)


def _post_kernel(x_ref, ya_ref, ga_ref, ybg_ref, woa_ref, wout_ref, g2_ref, wrt_ref, brt_ref,
                 x2_out, hl_out, route_out, cnt_out):
    n_sub = x_ref.shape[0] // MOE_SUB
    groups = [slice(g * MOE_SUB, (g + 1) * MOE_SUB) for g in range(n_sub)]
    y_a = [_dot(ya_ref[p, :].astype(BF16), woa_ref[...]) for p in groups]
    x2 = [x_ref[p, :] + _dot((ga_ref[p, :] * y + ybg_ref[p, :]).astype(BF16), wout_ref[...])
          for p, y in zip(groups, y_a)]
    h2 = [_rms(x, g2_ref[...]) for x in x2]
    split = [_split2(h) for h in h2]
    on_hi = [_dot_nt(wrt_ref[...], hi) for hi, _ in split]
    on_lo = [_dot_nt(wrt_ref[0:LANES, :], lo) for _, lo in split]
    row = lax.broadcasted_iota(jnp.int32, (LANES, MOE_SUB), 0)
    neg = jnp.float32(-jnp.inf)
    big = jnp.int32(LANES)
    is_grp = (row >= N_EXPERTS) & (row < N_EXPERTS + N_GROUPS)
    top = lambda v: jnp.max(v, axis=0, keepdims=True)
    first = lambda hit: jnp.min(jnp.where(hit, row, big), axis=0, keepdims=True)
    picks, onehots, weights = [], [], []
    for g, p in enumerate(groups):
        x2_out[p, :] = x2[g].astype(x2_out.dtype)
        logits = on_hi[g][:LANES, :] + on_hi[g][LANES:, :] + on_lo[g] + brt_ref[...]
        gl = jnp.where(is_grp, logits, neg)
        gmax = top(gl)
        g_p = 1.0 / jnp.sum(jnp.exp(gl - gmax), axis=0, keepdims=True)
        lo_row = (first(gl == gmax) - N_EXPERTS) * EXPERTS_PER_GROUP
        el = jnp.where((row >= lo_row) & (row < lo_row + EXPERTS_PER_GROUP), logits, neg)
        e1 = top(el)
        i1 = first(el == e1)
        el2 = jnp.where(row == i1, neg, el)
        e2 = top(el2)
        i2 = first(el2 == e2)
        t = jnp.exp(e2 - e1)
        weights.append((g_p / (1.0 + t), g_p * t / (1.0 + t)))
        picks.append((i1, i2))
        onehots.append(jnp.where((row == i1) | (row == i2), 1.0, 0.0))

    earlier = (lax.broadcasted_iota(jnp.int32, (MOE_SUB, MOE_SUB), 0)
               <= lax.broadcasted_iota(jnp.int32, (MOE_SUB, MOE_SUB), 1)).astype(BF16)
    incl = [_dot(oh.astype(BF16), earlier) for oh in onehots]
    chunks = [jnp.ceil(x[:, MOE_SUB - 1:] * (1.0 / RUN_ALIGN)) for x in incl]
    below = (lax.broadcasted_iota(jnp.int32, (LANES, LANES), 1)
             < lax.broadcasted_iota(jnp.int32, (LANES, LANES), 0)).astype(BF16)
    wide = [jnp.broadcast_to(c, (LANES, LANES)) for c in chunks]
    run_start = [_dot(below, w.astype(BF16))[:, 0:1] * RUN_ALIGN for w in wide]
    row8 = lax.broadcasted_iota(jnp.int32, (8, LANES), 0)
    counts = jnp.zeros((8, LANES), F32)
    slot_ids = lax.broadcasted_iota(jnp.int32, (RUN_SLOTS, MOE_SUB), 0)
    for g, p in enumerate(groups):
        counts = jnp.where(row8 == g, wide[g].T[0:8, :], counts)
        slot_of = run_start[g] + incl[g] - onehots[g]
        i1, i2 = picks[g]
        slot1 = jnp.sum(jnp.where(row == i1, slot_of, 0.0), axis=0, keepdims=True)
        slot2 = jnp.sum(jnp.where(row == i2, slot_of, 0.0), axis=0, keepdims=True)
        w1, w2 = weights[g]
        packed = jnp.where(row == 0, slot1, jnp.where(row == 1, slot2, jnp.where(
            row == 2, w1, jnp.where(row == 3, w2, 0.0))))
        route_out[p, :] = packed.T
        pick = jnp.where((slot_ids == slot1.astype(jnp.int32))
                         | (slot_ids == slot2.astype(jnp.int32)), 1.0, 0.0).astype(BF16)
        hl_out[g * RUN_SLOTS:(g + 1) * RUN_SLOTS, :] = _dot(pick, h2[g].astype(BF16)).astype(BF16)
    cnt_out[...] = counts


def _expert_kernel(chunk_s, first_s, hl_in_ref, wg_ref, wu_ref, wd_ref, hl_ref, xbuf, wgu_s, wd_s,
                   sem, drain_s):
    del hl_in_ref
    e = pl.program_id(0)
    pass_chunks = EXPERT_ROWS // RUN_ALIGN
    mine = e % 2
    other = 1 - mine
    gather_sem = lambda buf: sem.at[buf]
    scatter_sem = lambda buf: sem.at[2 + buf]

    def first_pass(ex):
        lo = first_s[ex]
        return lo, jnp.minimum(lo + pass_chunks, first_s[ex + 1])

    def for_chunks(buf, c_lo, c_hi, fn):
        def one(c):
            fn(hl_ref.at[pl.ds(pl.multiple_of(chunk_s[c], RUN_ALIGN), RUN_ALIGN)],
               xbuf.at[buf, pl.ds(pl.multiple_of((c - c_lo) * RUN_ALIGN, RUN_ALIGN), RUN_ALIGN)])

        def block(i, carry):
            for u in range(CHUNK_UNROLL):
                one(c_lo + i * CHUNK_UNROLL + u)
            return carry

        def tail(c, carry):
            one(c)
            return carry

        n_blocks = (c_hi - c_lo) // CHUNK_UNROLL
        lax.fori_loop(0, n_blocks, block, 0)
        lax.fori_loop(c_lo + n_blocks * CHUNK_UNROLL, c_hi, tail, 0)

    def gather(buf, c_lo, c_hi):
        for_chunks(buf, c_lo, c_hi, lambda hbm, vm: pltpu.make_async_copy(
            hbm, vm, gather_sem(buf)).start())

    def scatter(buf, c_lo, c_hi):
        for_chunks(buf, c_lo, c_hi, lambda hbm, vm: pltpu.make_async_copy(
            vm, hbm, scatter_sem(buf)).start())

    def wait_chunks(which_sem, n):
        def wait_rows(rows):
            def body(c, carry):
                pltpu.make_async_copy(hl_ref.at[pl.ds(0, rows)], xbuf.at[0, pl.ds(0, rows)],
                                      which_sem).wait()
                return carry
            return body

        lax.fori_loop(0, n // CHUNK_UNROLL, wait_rows(CHUNK_UNROLL * RUN_ALIGN), 0)
        lax.fori_loop(0, n % CHUNK_UNROLL, wait_rows(RUN_ALIGN), 0)

    def evaluate(buf, n_chunks):
        def tile(k, carry):
            sl = pl.ds(pl.multiple_of(k * EXPERT_TILE, EXPERT_TILE), EXPERT_TILE)
            gu = _dot(xbuf[buf, sl, :], wgu_s[...])
            gate = gu[:, :D_EXPERT]
            act = gate * jax.nn.sigmoid(gate) * gu[:, D_EXPERT:]
            xbuf[buf, sl, :] = _dot(act.astype(BF16), wd_s[...]).astype(BF16)
            return carry

        lax.fori_loop(0, (n_chunks * RUN_ALIGN + EXPERT_TILE - 1) // EXPERT_TILE, tile, 0)

    @pl.when(e == 0)
    def _():
        xbuf[...] = jnp.zeros_like(xbuf)
        gather(0, *first_pass(0))
        drain_s[0] = 0

    wgu_s[:, :D_EXPERT] = wg_ref[0].astype(BF16)
    wgu_s[:, D_EXPERT:] = wu_ref[0].astype(BF16)
    wd_s[...] = wd_ref[0].astype(BF16)

    c_lo, c_hi = first_pass(e)
    wait_chunks(gather_sem(mine), c_hi - c_lo)
    wait_chunks(scatter_sem(other), drain_s[0])

    @pl.when(e + 1 < pl.num_programs(0))
    def _():
        gather(other, *first_pass(e + 1))

    evaluate(mine, c_hi - c_lo)
    scatter(mine, c_lo, c_hi)

    def later_pass(state):
        lo, pending = state
        hi = jnp.minimum(lo + pass_chunks, first_s[e + 1])
        wait_chunks(scatter_sem(mine), pending)
        gather(mine, lo, hi)
        wait_chunks(gather_sem(mine), hi - lo)
        evaluate(mine, hi - lo)
        scatter(mine, lo, hi)
        return hi, hi - lo

    _, pending = lax.while_loop(lambda st: st[0] < first_s[e + 1], later_pass,
                                (c_hi, c_hi - c_lo))
    drain_s[0] = pending

    @pl.when(e + 1 == pl.num_programs(0))
    def _():
        wait_chunks(scatter_sem(mine), pending)


def _combine_kernel(x2_ref, route_ref, gf_ref, yl_ref, o_ref):
    slots = lax.broadcasted_iota(jnp.int32, (MOE_SUB, RUN_SLOTS), 1).astype(F32)
    for g in range(x2_ref.shape[0] // MOE_SUB):
        p = slice(g * MOE_SUB, (g + 1) * MOE_SUB)
        route = route_ref[p, :]
        pick = (jnp.where(slots == route[:, 0:1], route[:, 2:3], 0.0)
                + jnp.where(slots == route[:, 1:2], route[:, 3:4], 0.0))
        y = _dot(pick.astype(BF16), yl_ref[g * RUN_SLOTS:(g + 1) * RUN_SLOTS, :])
        o_ref[p, :] = _rms(x2_ref[p, :] + y, gf_ref[...])


def _full(shape):
    return pl.BlockSpec(shape, lambda *_: (0,) * len(shape))


def _params(sem):
    return pltpu.CompilerParams(dimension_semantics=sem, vmem_limit_bytes=VMEM_LIMIT)


def _mixer_a_layout(t):
    s_rkv = 3 * A_WIDTH
    s_w = s_rkv + D_DECAY_LORA
    s_a = s_w + D_AAA_LORA
    gap = lambda n: jnp.zeros((t.shape[0], n), t.dtype)
    return jnp.concatenate(
        [t[:, :s_w], gap(XA_OFF - D_DECAY_LORA), t[:, s_w:s_a], gap(XG_OFF - XA_OFF - D_AAA_LORA),
         t[:, s_a:], gap(LORA_PAD - XG_OFF - D_GATE_LORA)], axis=1)


def _place(cols, parts):
    out = jnp.zeros((parts[0][1].shape[0], cols), parts[0][1].dtype)
    for off, arr in parts:
        out = lax.dynamic_update_slice(out, arr, (0, off))
    return out


def kernel(x, norm1_g, w_in, b_gate, tmix_mu, w0, w2, a0, a2, g2, k_k, k_a, r_k, lnx_g, lnx_b,
           w_oA, lnv_g, lnv_b, w_s, b_s, w_oB, w_out, norm2_g, w_rg, b_rg, w_re, b_re,
           w_e_gate, w_e_up, w_e_down, final_g):
    bsz, seq, d = x.shape
    n_tok = bsz * seq
    depth = norm1_g.shape[0]
    assert depth == 1, "the moe kernel fuses the final norm, so it must be the last layer"
    assert bsz % WKV_SEQS == 0 and seq % WKV_TOKENS == 0
    xf = x.reshape(n_tok, d)

    s_rkv = 3 * A_WIDTH
    s_w = s_rkv + D_DECAY_LORA
    s_a = s_w + D_AAA_LORA
    a_cols = s_a + D_GATE_LORA
    b_cols = 2 * B_WIDTH

    ones_bd = (jnp.arange(A_WIDTH)[:, None] // A_HEAD
               == jnp.arange(A_WIDTH)[None, :] // A_HEAD).astype(BF16)

    tm_a = INPROJ_TILE
    tm_p = POST_TILE
    tm_c = COMBINE_TILE
    assert tm_p % MOE_SUB == 0 and tm_c % MOE_SUB == 0 and tm_a % DENSE_SPLIT == 0

    for l in range(depth):
        wl = w_in[l]
        rows_w = 256
        w_a, w_b, w_g = pl.pallas_call(
            _w_in_kernel,
            grid=(d // rows_w,),
            in_specs=[pl.BlockSpec((wl.shape[1], rows_w), lambda i: (0, i))],
            out_specs=[pl.BlockSpec((A_PROJ, rows_w), lambda i: (0, i)),
                       pl.BlockSpec((b_cols, rows_w), lambda i: (0, i)),
                       pl.BlockSpec((2 * d, rows_w), lambda i: (0, i))],
            out_shape=[jax.ShapeDtypeStruct((A_PROJ, d), BF16),
                       jax.ShapeDtypeStruct((b_cols, d), BF16),
                       jax.ShapeDtypeStruct((2 * d, d), BF16)],
            compiler_params=_params(("parallel",)),
            name="w_in_layout",
        )(wl.T)
        mu_a = _mixer_a_layout(tmix_mu[l][None, :])
        w2p = jnp.pad(w2[l], ((0, LANES - D_DECAY_LORA), (0, 0))).astype(BF16)
        a2p = jnp.pad(a2[l], ((0, LANES - D_AAA_LORA), (0, 0))).astype(BF16)
        g2p = jnp.pad(g2[l], ((0, LORA_PAD - XG_OFF - D_GATE_LORA), (0, 0))).astype(BF16)
        g1 = norm1_g[l][None, :]

        row512 = lambda i: (i, 0)
        tok_a = pl.BlockSpec((tm_a, A_WIDTH), row512)
        tok_d = pl.BlockSpec((tm_a, d), row512)
        vec_a = _full((1, A_WIDTH))
        bs_full = jnp.repeat(b_s[l].T, B_GROUP_CH, axis=1)
        outs = pl.pallas_call(
            functools.partial(_inproj_kernel, seq // tm_a),
            grid=(n_tok // tm_a,),
            in_specs=[tok_d,
                      _full((1, d)), _full((A_PROJ, d)), _full((1, A_PROJ)), vec_a,
                      _full((LANES, A_WIDTH)), vec_a, _full((LANES, A_WIDTH)),
                      _full((LORA_PAD - XG_OFF, A_WIDTH)), vec_a, vec_a,
                      _full((A_WIDTH, A_WIDTH)),
                      _full((b_cols, d)), _full((2 * d, d)), _full((1, 2 * d)),
                      _full((1, B_WIDTH)), _full((1, B_WIDTH)),
                      _full((B_GROUPS, GMLP_BLOCK, GMLP_BLOCK)), _full((GMLP_BLOCK, B_WIDTH)),
                      _full((B_WIDTH, d))],
            out_specs=[tok_a] * 7 + [tok_d, tok_d],
            out_shape=[jax.ShapeDtypeStruct((n_tok, A_WIDTH), F32)] * 7
            + [jax.ShapeDtypeStruct((n_tok, d), BF16)] * 2,
            scratch_shapes=[pltpu.VMEM((8, A_PROJ), F32)],
            compiler_params=_params(("arbitrary",)),
            name="inproj",
        )(xf, g1, w_a, mu_a, w0[l][None, :], w2p, a0[l][None, :], a2p, g2p,
          k_k[l][None, :], k_a[l][None, :], ones_bd, w_b, w_g, b_gate[l][None, :],
          lnv_g[l][None, :], lnv_b[l][None, :], w_s[l], bs_full, w_oB[l].astype(BF16))
        r_, k_, v_, a_, b_, ld_, gg_, ybg, ga = outs

        n_pairs = A_WIDTH // HEAD_PAIR
        n_chunks = WKV_TOKENS // WKV_CHUNK
        tok_w = pl.BlockSpec((WKV_SEQS, WKV_TOKENS, A_WIDTH), lambda bi, ti: (bi, ti, 0))
        vec_w = _full((1, A_WIDTH))
        seq_major = lambda t: t.reshape(bsz, seq, A_WIDTH)
        tok_scratch = pltpu.VMEM((WKV_SEQS, WKV_TOKENS, A_WIDTH), BF16)
        ya_in = pl.pallas_call(
            _wkv_kernel,
            grid=(bsz // WKV_SEQS, seq // WKV_TOKENS),
            in_specs=[tok_w] * 7 + [vec_w, vec_w, vec_w, _full((HEAD_PAIR, HEAD_PAIR))],
            out_specs=tok_w,
            out_shape=jax.ShapeDtypeStruct((bsz, seq, A_WIDTH), BF16),
            scratch_shapes=[pltpu.VMEM((WKV_SEQS, n_pairs, HEAD_PAIR, HEAD_PAIR), F32)]
            + [tok_scratch] * 5
            + [pltpu.VMEM((WKV_SEQS, n_chunks, n_pairs, HEAD_PAIR, HEAD_PAIR), BF16),
               pltpu.VMEM((WKV_SEQS, n_chunks, n_pairs, HEAD_PAIR, HEAD_PAIR), F32),
               pltpu.VMEM((WKV_SEQS, WKV_TOKENS, A_WIDTH), F32)],
            compiler_params=_params(("parallel", "arbitrary")),
            name="wkv",
        )(*[seq_major(t) for t in (r_, k_, v_, a_, b_, ld_, gg_)], r_k[l].reshape(1, A_WIDTH),
          lnx_g[l][None, :], lnx_b[l][None, :],
          ones_bd[:HEAD_PAIR, :HEAD_PAIR]).reshape(n_tok, A_WIDTH)

        w_r = _place(LANES, [(0, jnp.transpose(w_re[l], (1, 0, 2)).reshape(d, N_EXPERTS)),
                             (N_EXPERTS, w_rg[l])])
        b_r = _place(LANES, [(0, b_re[l].reshape(1, N_EXPERTS)), (N_EXPERTS, b_rg[l][None, :])])
        wr_hi = w_r.astype(BF16)
        wr_lo = (w_r - wr_hi.astype(F32)).astype(BF16)
        wr_t = jnp.concatenate([wr_hi.T, wr_lo.T], axis=0)
        tok_p = pl.BlockSpec((tm_p, d), row512)
        lane_p = pl.BlockSpec((tm_p, LANES), row512)
        n_sub = n_tok // MOE_SUB
        sorted_rows = tm_p // MOE_SUB * RUN_SLOTS
        x2, h_sorted, route, cnt = pl.pallas_call(
            _post_kernel,
            grid=(n_tok // tm_p,),
            in_specs=[tok_p, pl.BlockSpec((tm_p, A_WIDTH), row512), tok_p, tok_p,
                      _full((A_WIDTH, d)), _full((d, d)), _full((1, d)), _full((2 * LANES, d)),
                      _full((LANES, 1))],
            out_specs=[tok_p, pl.BlockSpec((sorted_rows, d), row512), lane_p,
                       pl.BlockSpec((8, LANES), row512)],
            out_shape=[jax.ShapeDtypeStruct((n_tok, d), F32),
                       jax.ShapeDtypeStruct((n_sub * RUN_SLOTS, d), BF16),
                       jax.ShapeDtypeStruct((n_tok, LANES), F32),
                       jax.ShapeDtypeStruct((n_tok // tm_p * 8, LANES), F32)],
            compiler_params=_params(("parallel",)),
            name="post",
        )(xf, ya_in, ga, ybg, w_oA[l].astype(BF16), w_out[l].astype(BF16),
          norm2_g[l][None, :], wr_t, b_r.T)

        run_chunks = cnt.reshape(n_tok // tm_p, 8, LANES)[:, :tm_p // MOE_SUB, :N_EXPERTS]
        run_chunks = run_chunks.reshape(n_sub, N_EXPERTS).astype(jnp.int32)
        run_start = (jnp.cumsum(run_chunks, axis=1) - run_chunks) * RUN_ALIGN
        run_row = (jnp.arange(n_sub, dtype=jnp.int32)[:, None] * RUN_SLOTS + run_start).T.reshape(-1)
        run_n = run_chunks.T.reshape(-1)
        run_end = jnp.cumsum(run_n)
        first_chunk = jnp.concatenate([jnp.zeros((1,), jnp.int32), run_end[n_sub - 1::n_sub]])
        pos = jnp.arange(n_sub * RUN_SLOTS // RUN_ALIGN, dtype=jnp.int32)
        run_of = jnp.sum(pos[:, None] >= run_end[None, :], axis=1)
        in_run = run_of[:, None] == jnp.arange(run_n.shape[0], dtype=jnp.int32)[None, :]
        pick = lambda t: jnp.sum(jnp.where(in_run, t[None, :], 0), axis=1)
        chunk_rows = pick(run_row) + (pos - pick(run_end - run_n)) * RUN_ALIGN

        per_expert = lambda e, *_: (e, 0, 0)
        y_sorted = pl.pallas_call(
            _expert_kernel,
            grid_spec=pltpu.PrefetchScalarGridSpec(
                num_scalar_prefetch=2, grid=(N_EXPERTS,),
                in_specs=[pl.BlockSpec(memory_space=pl.ANY),
                          pl.BlockSpec((1, d, D_EXPERT), per_expert),
                          pl.BlockSpec((1, d, D_EXPERT), per_expert),
                          pl.BlockSpec((1, D_EXPERT, d), per_expert)],
                out_specs=pl.BlockSpec(memory_space=pl.ANY),
                scratch_shapes=[pltpu.VMEM((2, EXPERT_ROWS, d), BF16),
                                pltpu.VMEM((d, 2 * D_EXPERT), BF16),
                                pltpu.VMEM((D_EXPERT, d), BF16),
                                pltpu.SemaphoreType.DMA((4,)),
                                pltpu.SMEM((1,), jnp.int32)]),
            out_shape=jax.ShapeDtypeStruct((n_sub * RUN_SLOTS, d), BF16),
            input_output_aliases={2: 0},
            compiler_params=_params(("arbitrary",)),
            name="moe_experts",
        )(chunk_rows, first_chunk, h_sorted, w_e_gate[l], w_e_up[l], w_e_down[l])

        xf = pl.pallas_call(
            _combine_kernel,
            grid=(n_tok // tm_c,),
            in_specs=[pl.BlockSpec((tm_c, d), row512), pl.BlockSpec((tm_c, LANES), row512),
                      _full((1, d)), pl.BlockSpec((tm_c // MOE_SUB * RUN_SLOTS, d), row512)],
            out_specs=pl.BlockSpec((tm_c, d), row512),
            out_shape=jax.ShapeDtypeStruct((n_tok, d), F32),
            compiler_params=_params(("parallel",)),
            name="moe_combine",
        )(x2, route, final_g[None, :], y_sorted)

    return xf.reshape(bsz, seq, d)
```

```python
import functools

import jax
import jax.numpy as jnp
from jax import lax
from jax.experimental import pallas as pl
from jax.experimental.pallas import tpu as pltpu

F32 = jnp.float32
BF16 = jnp.bfloat16

D_MODEL = 1024
A_WIDTH = 512
A_HEAD = 64
D_DECAY_LORA = 64
D_AAA_LORA = 64
D_GATE_LORA = 160
B_WIDTH = 512
B_GROUPS = 4
B_GROUP_CH = 128
GMLP_BLOCK = 128
N_GROUPS = 4
EXPERTS_PER_GROUP = 8
N_EXPERTS = 32
D_EXPERT = 256
NORM_EPS = 1e-6
LN_EPS = 1e-5
LNX_EPS = 64e-5

LANES = 128
LORA_PAD = 512
XW_OFF, XA_OFF, XG_OFF = 0, 128, 256
A_PROJ = 3 * A_WIDTH + LORA_PAD
WKV_CHUNK = 64
HEAD_PAIR = 2 * A_HEAD
WKV_SEQS = 8
WKV_TOKENS = 64
WKV_PREP_GROUP = 4
WKV_NORM_GROUP = 1
MOE_SUB = 256
RUN_ALIGN = 16
RUN_SLOTS = 2 * MOE_SUB + N_EXPERTS * RUN_ALIGN
EXPERT_ROWS = 2048
EXPERT_TILE = 512
INPROJ_TILE = 512
POST_TILE = 1024
COMBINE_TILE = 1024
DENSE_SPLIT = 256
CHUNK_UNROLL = 4
VMEM_LIMIT = 48 * 1024 * 1024
POST_VMEM_LIMIT = 56 * 1024 * 1024


def _rms(x, g):
    return x * lax.rsqrt(jnp.mean(x * x, axis=-1, keepdims=True) + NORM_EPS) * g


def _dot(a, b):
    return jnp.dot(a, b, preferred_element_type=F32)


def _dot_nt(a, b):
    return lax.dot_general(a, b, (((1,), (1,)), ((), ())), preferred_element_type=F32)


def _split2(x):
    hi = x.astype(BF16)
    lo = (x - hi.astype(F32)).astype(BF16)
    return hi, lo


def _split3(x):
    hi = x.astype(BF16)
    r1 = x - hi.astype(F32)
    mid = r1.astype(BF16)
    lo = (r1 - mid.astype(F32)).astype(BF16)
    return hi, mid, lo


def _pair_head_sums(xs, ones_pair):
    n_tiles = A_WIDTH // HEAD_PAIR
    rows = xs[0].shape[0]
    tiles = [x[:, t * HEAD_PAIR:(t + 1) * HEAD_PAIR].astype(BF16) for x in xs for t in range(n_tiles)]
    sums = _dot(jnp.concatenate(tiles, axis=0), ones_pair)
    return [jnp.concatenate([sums[(i * n_tiles + t) * rows:(i * n_tiles + t + 1) * rows]
                             for t in range(n_tiles)], axis=1) for i in range(len(xs))]


def _w_in_kernel(w_ref, wa_out, wb_out, wg_out):
    s_rkv = 3 * A_WIDTH
    s_w = s_rkv + D_DECAY_LORA
    s_a = s_w + D_AAA_LORA
    a_cols = s_a + D_GATE_LORA
    b_end = a_cols + 2 * B_WIDTH
    wa_out[...] = jnp.zeros_like(wa_out)
    wa_out[:s_w, :] = w_ref[:s_w, :].astype(BF16)
    wa_out[s_rkv + XA_OFF:s_rkv + XA_OFF + D_AAA_LORA, :] = w_ref[s_w:s_a, :].astype(BF16)
    wa_out[s_rkv + XG_OFF:s_rkv + XG_OFF + D_GATE_LORA, :] = w_ref[s_a:a_cols, :].astype(BF16)
    wb_out[...] = w_ref[a_cols:b_end, :].astype(BF16)
    wg_out[...] = w_ref[b_end:, :].astype(BF16)


def _inproj_kernel(tiles_per_seq, x_ref, g1_ref, wa_ref, mu_ref, w0_ref, w2_ref, a0_ref, a2_ref,
                   g2_ref, kk_ref, ka_ref, ones_ref, wb_ref, wg_ref, bg_ref, lng_ref, lnb_ref,
                   ws_ref, bs_ref, wo_ref,
                   r_out, k_out, v_out, a_out, b_out, ld_out, g_out, ybg_out, ga_out, tail_ref):
    i = pl.program_id(0)
    tm = x_ref.shape[0]
    parts = [slice(j * DENSE_SPLIT, (j + 1) * DENSE_SPLIT) for j in range(tm // DENSE_SPLIT)]
    g1 = g1_ref[...]
    h = [_rms(x_ref[p, :], g1).astype(BF16) for p in parts]
    proj = [_dot_nt(hh, wa_ref[...]) for hh in h]
    pb = [_dot_nt(hh, wb_ref[...]) for hh in h]

    prev = [jnp.where(i % tiles_per_seq == 0, 0.0, tail_ref[7:8, :])]
    prev += [p[DENSE_SPLIT - 1:, :] for p in proj[:-1]]
    tail_ref[...] = proj[-1][DENSE_SPLIT - 8:, :]
    row = lax.broadcasted_iota(jnp.int32, proj[0].shape, 0)
    pm = []
    for p, pv in zip(proj, prev):
        shifted = jnp.where(row == 0, pv, pltpu.roll(p, 1, axis=0))
        pm.append(p + mu_ref[...] * (shifted - p))
    gates = [jax.nn.sigmoid(_dot_nt(hh, wg_ref[...]) + bg_ref[...]) for hh in h]

    tri = (lax.broadcasted_iota(jnp.int32, (GMLP_BLOCK, GMLP_BLOCK), 0)
           >= lax.broadcasted_iota(jnp.int32, (GMLP_BLOCK, GMLP_BLOCK), 1))
    ws = [jnp.where(tri, ws_ref[grp], 0.0).astype(BF16) for grp in range(B_GROUPS)]
    bs = bs_ref[...]
    us, vns = [], []
    for x in pb:
        z = 0.5 * x * (1.0 + lax.erf(x * (2.0 ** -0.5)))
        us.append(z[:, :B_WIDTH])
        v = z[:, B_WIDTH:]
        mean = jnp.mean(v, axis=-1, keepdims=True)
        vc = v - mean
        var = jnp.mean(vc * vc, axis=-1, keepdims=True)
        vns.append((vc * lax.rsqrt(var + LN_EPS) * lng_ref[...] + lnb_ref[...]).astype(BF16))

    lora = [x[:, 3 * A_WIDTH:] for x in pm]
    decay = [_dot(jnp.tanh(x[:, XW_OFF:XW_OFF + LANES]).astype(BF16), w2_ref[...]) for x in lora]
    rate = [_dot(x[:, XA_OFF:XA_OFF + LANES].astype(BF16), a2_ref[...]) for x in lora]
    gate = [_dot(jax.nn.sigmoid(x[:, XG_OFF:]).astype(BF16), g2_ref[...]) for x in lora]
    kks = [x[:, A_WIDTH:2 * A_WIDTH] * kk_ref[...] for x in pm]
    sq = [_dot((kk * kk).astype(BF16), ones_ref[...]) for kk in kks]

    svs = []
    for vn in vns:
        rows = []
        for blk in range(DENSE_SPLIT // GMLP_BLOCK):
            cols = [_dot(ws[grp], vn[blk * GMLP_BLOCK:(blk + 1) * GMLP_BLOCK,
                                     grp * B_GROUP_CH:(grp + 1) * B_GROUP_CH])
                    for grp in range(B_GROUPS)]
            rows.append(jnp.concatenate(cols, axis=1) + bs)
        svs.append(jnp.concatenate(rows, axis=0))
    yb = [_dot((u * sv).astype(BF16), wo_ref[...]) for u, sv in zip(us, svs)]

    for j, p in enumerate(parts):
        k = pm[j][:, A_WIDTH:2 * A_WIDTH]
        z = -(w0_ref[...] + decay[j])
        softplus = jnp.maximum(z, 0.0) + jnp.log(1.0 + jnp.exp(-jnp.abs(z)))
        w = -softplus - 0.5
        a_lr = jax.nn.sigmoid(a0_ref[...] + rate[j])
        kk = kks[j] / jnp.maximum(jnp.sqrt(sq[j]), 1e-12)
        r_out[p, :] = pm[j][:, 0:A_WIDTH]
        k_out[p, :] = k * (1.0 + (a_lr - 1.0) * ka_ref[...])
        v_out[p, :] = pm[j][:, 2 * A_WIDTH:3 * A_WIDTH]
        a_out[p, :] = -kk
        b_out[p, :] = kk * a_lr
        ld_out[p, :] = -jnp.exp(w)
        g_out[p, :] = gate[j]
        ga_out[p, :] = gates[j][:, :D_MODEL].astype(BF16)
        ybg_out[p, :] = (gates[j][:, D_MODEL:] * yb[j]).astype(BF16)


def _wkv_kernel(r_ref, k_ref, v_ref, a_ref, b_ref, ld_ref, g_ref, rk_ref, lng_ref, lnb_ref,
                ones_ref, o_ref, st_ref, ta_s, tl_s, arb_s, ark_s, rt_s, bkt_s, dcol_s, y_s):
    C = WKV_CHUNK
    bb, tb, _ = r_ref.shape
    n_chunks = tb // C
    n_pairs = A_WIDTH // HEAD_PAIR

    @pl.when(pl.program_id(1) == 0)
    def _():
        st_ref[...] = jnp.zeros_like(st_ref)

    row = lax.broadcasted_iota(jnp.int32, (C, HEAD_PAIR), 0)
    src = lax.broadcasted_iota(jnp.int32, (C, HEAD_PAIR), 1) & (C - 1)
    incl = src <= row
    strict = src < row
    eye_pair = jnp.where(src == row, 1.0, 0.0)
    bd_mask = ((lax.broadcasted_iota(jnp.int32, (HEAD_PAIR, HEAD_PAIR), 0) >= A_HEAD)
               == (lax.broadcasted_iota(jnp.int32, (HEAD_PAIR, HEAD_PAIR), 1) >= A_HEAD))
    tri_c = (lax.broadcasted_iota(jnp.int32, (C, C), 0)
             >= lax.broadcasted_iota(jnp.int32, (C, C), 1)).astype(BF16)
    ones_pair = ones_ref[...]
    pair_cols = [slice(p * HEAD_PAIR, (p + 1) * HEAD_PAIR) for p in range(n_pairs)]

    def bd(x):
        xb = x.astype(BF16)
        return jnp.where(bd_mask, jnp.concatenate([xb, xb], axis=0), jnp.zeros((), BF16))

    def prep(it, carry):
        where, lhs, rhs, ats = [], [], [], []
        for j in range(WKV_PREP_GROUP):
            flat = it * WKV_PREP_GROUP + j
            b = flat // n_chunks
            c = flat % n_chunks
            sl = pl.ds(pl.multiple_of(c * C, C), C)
            ld = ld_ref[b, sl, :]
            hi, mid, lo = _split3(ld)
            cl = _dot(tri_c, hi) + _dot(tri_c, mid) + _dot(tri_c, lo)
            cl_end = cl[C - 1:C, :]
            k = k_ref[b, sl, :]
            bv = b_ref[b, sl, :]
            d_inv = jnp.exp(-cl)
            d_tail = jnp.exp(cl_end - cl)
            rt = (r_ref[b, sl, :] * jnp.exp(cl)).astype(BF16)
            rt_s[b, sl, :] = rt
            kt = k * d_inv
            bt = bv * d_inv
            at = a_ref[b, sl, :] * jnp.exp(cl - ld)
            kd = k * d_tail
            bdk = bv * d_tail
            d_end = jnp.exp(cl_end)
            for p, cs in enumerate(pair_cols):
                bkt_s[b, c, p] = jnp.concatenate([bdk[:, cs], kd[:, cs]], axis=0).T.astype(BF16)
                dcol_s[b, c, p] = jnp.broadcast_to(d_end[:, cs], (HEAD_PAIR, HEAD_PAIR)).T
                where.append((b, sl, cs))
                ats.append(at[:, cs])
                lhs.append(jnp.concatenate([rt[:, cs], at[:, cs].astype(BF16)], axis=0))
                rhs.append(jnp.concatenate([bd(kt[:, cs]), bd(bt[:, cs])], axis=0))
        n = len(where)
        amat = [_dot_nt(lhs[i], rhs[i]) for i in range(n)]
        l_ak, l_ab = [], []
        for i, (b, sl, cs) in enumerate(where):
            ark_s[b, sl, cs] = jnp.where(incl, amat[i][:C, :HEAD_PAIR], 0.0).astype(BF16)
            arb_s[b, sl, cs] = jnp.where(incl, amat[i][:C, HEAD_PAIR:], 0.0).astype(BF16)
            l_ak.append(jnp.where(strict, amat[i][C:, :HEAD_PAIR], 0.0))
            l_ab.append(jnp.where(strict, amat[i][C:, HEAD_PAIR:], 0.0))

        t_mat = [eye_pair + l for l in l_ab]
        q = [_dot(l.astype(BF16), bd(l)) for l in l_ab]
        n_sq = 1
        while 2 * n_sq < C // 2:
            both = [_dot(q[i].astype(BF16), jnp.concatenate([bd(q[i]), bd(t_mat[i])], axis=1))
                    for i in range(n)]
            q = [x[:, :HEAD_PAIR] for x in both]
            t_mat = [t_mat[i] + both[i][:, HEAD_PAIR:] for i in range(n)]
            n_sq *= 2
        t_mat = [t_mat[i] + _dot(q[i].astype(BF16), bd(t_mat[i])) for i in range(n)]
        tal = [_dot(t_mat[i].astype(BF16), jnp.concatenate([bd(ats[i]), bd(l_ak[i])], axis=1))
               for i in range(n)]
        for i, (b, sl, cs) in enumerate(where):
            ta_s[b, sl, cs] = tal[i][:, :HEAD_PAIR].astype(BF16)
            tl_s[b, sl, cs] = tal[i][:, HEAD_PAIR:].astype(BF16)
        return carry

    lax.fori_loop(0, bb * n_chunks // WKV_PREP_GROUP, prep, 0)

    def step(c, carry):
        sl = pl.ds(pl.multiple_of(c * C, C), C)
        chains = [(b, p) for b in range(bb) for p in range(n_pairs)]
        v = [v_ref[b, sl, :] for b in range(bb)]
        st = [st_ref[b, p] for b, p in chains]
        st_b = [x.astype(BF16) for x in st]
        bd_v = [bd(v[b][:, pair_cols[p]]) for b, p in chains]
        on_st = [_dot(jnp.concatenate([ta_s[b, sl, pair_cols[p]], rt_s[b, sl, pair_cols[p]]],
                                      axis=0), st_b[i]) for i, (b, p) in enumerate(chains)]
        on_v = [_dot(jnp.concatenate([tl_s[b, sl, pair_cols[p]], ark_s[b, sl, pair_cols[p]]],
                                     axis=0), bd_v[i]) for i, (b, p) in enumerate(chains)]
        u = [on_st[i][:C] + on_v[i][:C] for i in range(len(chains))]
        for i, (b, p) in enumerate(chains):
            uv = jnp.concatenate([u[i], v[b][:, pair_cols[p]]], axis=0).astype(BF16)
            st_ref[b, p] = (dcol_s[b, c, p] * st[i]
                            + jnp.where(bd_mask, _dot(bkt_s[b, c, p], uv), 0.0))
        ys = [on_st[i][C:] + on_v[i][C:] + _dot(arb_s[b, sl, pair_cols[p]], bd(u[i]))
              for i, (b, p) in enumerate(chains)]
        for b in range(bb):
            y_s[b, sl, :] = jnp.concatenate(ys[b * n_pairs:(b + 1) * n_pairs], axis=1)
        return carry

    lax.fori_loop(0, n_chunks, step, 0)

    def finish(it, carry):
        items = [(b, pl.ds(pl.multiple_of((it * WKV_NORM_GROUP + j) * C, C), C))
                 for j in range(WKV_NORM_GROUP) for b in range(bb)]
        y = [y_s[b, sl, :] for b, sl in items]
        rkr = [r_ref[b, sl, :] * k_ref[b, sl, :] * rk_ref[...] for b, sl in items]
        sums = [_pair_head_sums([y[i], rkr[i]], ones_pair) for i in range(len(items))]
        yc = [y[i] - sums[i][0] * (1.0 / A_HEAD) for i in range(len(items))]
        var = [_pair_head_sums([x * x], ones_pair)[0] * (1.0 / A_HEAD) for x in yc]
        for i, (b, sl) in enumerate(items):
            yn = yc[i] * lax.rsqrt(var[i] + LNX_EPS) * lng_ref[...] + lnb_ref[...]
            out = (yn + sums[i][1] * v_ref[b, sl, :]) * g_ref[b, sl, :]
            o_ref[b, sl, :] = out.astype(o_ref.dtype)
        return carry

    lax.fori_loop(0, n_chunks // WKV_NORM_GROUP, finish, 0)


def _post_kernel(x_ref, ya_ref, ga_ref, ybg_ref, woa_ref, wout_ref, g2_ref, wrt_ref, brt_ref,
                 x2_out, hl_out, route_out, cnt_out):
    n_sub = x_ref.shape[0] // MOE_SUB
    groups = [slice(g * MOE_SUB, (g + 1) * MOE_SUB) for g in range(n_sub)]
    y_a = [_dot(ya_ref[p, :].astype(BF16), woa_ref[...]) for p in groups]
    x2 = [x_ref[p, :] + _dot((ga_ref[p, :] * y + ybg_ref[p, :]).astype(BF16), wout_ref[...])
          for p, y in zip(groups, y_a)]
    h2 = [_rms(x, g2_ref[...]) for x in x2]
    split = [_split2(h) for h in h2]
    on_hi = [_dot_nt(wrt_ref[...], hi) for hi, _ in split]
    on_lo = [_dot_nt(wrt_ref[0:LANES, :], lo) for _, lo in split]
    row = lax.broadcasted_iota(jnp.int32, (LANES, MOE_SUB), 0)
    neg = jnp.float32(-jnp.inf)
    big = jnp.int32(LANES)
    is_grp = (row >= N_EXPERTS) & (row < N_EXPERTS + N_GROUPS)
    top = lambda v: jnp.max(v, axis=0, keepdims=True)
    first = lambda hit: jnp.min(jnp.where(hit, row, big), axis=0, keepdims=True)
    picks, onehots, weights = [], [], []
    for g, p in enumerate(groups):
        x2_out[p, :] = x2[g].astype(x2_out.dtype)
        logits = on_hi[g][:LANES, :] + on_hi[g][LANES:, :] + on_lo[g] + brt_ref[...]
        gl = jnp.where(is_grp, logits, neg)
        gmax = top(gl)
        g_p = 1.0 / jnp.sum(jnp.exp(gl - gmax), axis=0, keepdims=True)
        lo_row = (first(gl == gmax) - N_EXPERTS) * EXPERTS_PER_GROUP
        el = jnp.where((row >= lo_row) & (row < lo_row + EXPERTS_PER_GROUP), logits, neg)
        e1 = top(el)
        i1 = first(el == e1)
        el2 = jnp.where(row == i1, neg, el)
        e2 = top(el2)
        i2 = first(el2 == e2)
        t = jnp.exp(e2 - e1)
        weights.append((g_p / (1.0 + t), g_p * t / (1.0 + t)))
        picks.append((i1, i2))
        onehots.append(jnp.where((row == i1) | (row == i2), 1.0, 0.0))

    earlier = (lax.broadcasted_iota(jnp.int32, (MOE_SUB, MOE_SUB), 0)
               <= lax.broadcasted_iota(jnp.int32, (MOE_SUB, MOE_SUB), 1)).astype(BF16)
    incl = [_dot(oh.astype(BF16), earlier) for oh in onehots]
    chunks = [jnp.ceil(x[:, MOE_SUB - 1:] * (1.0 / RUN_ALIGN)) for x in incl]
    below = (lax.broadcasted_iota(jnp.int32, (LANES, LANES), 1)
             < lax.broadcasted_iota(jnp.int32, (LANES, LANES), 0)).astype(BF16)
    wide = [jnp.broadcast_to(c, (LANES, LANES)) for c in chunks]
    run_start = [_dot(below, w.astype(BF16))[:, 0:1] * RUN_ALIGN for w in wide]
    row8 = lax.broadcasted_iota(jnp.int32, (8, LANES), 0)
    counts = jnp.zeros((8, LANES), F32)
    slot_ids = lax.broadcasted_iota(jnp.int32, (RUN_SLOTS, MOE_SUB), 0)
    for g, p in enumerate(groups):
        counts = jnp.where(row8 == g, wide[g].T[0:8, :], counts)
        slot_of = run_start[g] + incl[g] - onehots[g]
        i1, i2 = picks[g]
        slot1 = jnp.sum(jnp.where(row == i1, slot_of, 0.0), axis=0, keepdims=True)
        slot2 = jnp.sum(jnp.where(row == i2, slot_of, 0.0), axis=0, keepdims=True)
        w1, w2 = weights[g]
        packed = jnp.where(row == 0, slot1, jnp.where(row == 1, slot2, jnp.where(
            row == 2, w1, jnp.where(row == 3, w2, 0.0))))
        route_out[p, :] = packed.T
        pick = jnp.where((slot_ids == slot1.astype(jnp.int32))
                         | (slot_ids == slot2.astype(jnp.int32)), 1.0, 0.0).astype(BF16)
        hl_out[g * RUN_SLOTS:(g + 1) * RUN_SLOTS, :] = _dot(pick, h2[g].astype(BF16)).astype(BF16)
    cnt_out[...] = counts


def _expert_kernel(chunk_s, first_s, hl_in_ref, wg_ref, wu_ref, wd_ref, hl_ref, xbuf, wgu_s, wd_s,
                   sem, drain_s):
    del hl_in_ref
    e = pl.program_id(0)
    pass_chunks = EXPERT_ROWS // RUN_ALIGN
    mine = e % 2
    other = 1 - mine
    gather_sem = lambda buf: sem.at[buf]
    scatter_sem = lambda buf: sem.at[2 + buf]

    def first_pass(ex):
        lo = first_s[ex]
        return lo, jnp.minimum(lo + pass_chunks, first_s[ex + 1])

    def for_chunks(buf, c_lo, c_hi, fn):
        def one(c):
            fn(hl_ref.at[pl.ds(pl.multiple_of(chunk_s[c], RUN_ALIGN), RUN_ALIGN)],
               xbuf.at[buf, pl.ds(pl.multiple_of((c - c_lo) * RUN_ALIGN, RUN_ALIGN), RUN_ALIGN)])

        def block(i, carry):
            for u in range(CHUNK_UNROLL):
                one(c_lo + i * CHUNK_UNROLL + u)
            return carry

        def tail(c, carry):
            one(c)
            return carry

        n_blocks = (c_hi - c_lo) // CHUNK_UNROLL
        lax.fori_loop(0, n_blocks, block, 0)
        lax.fori_loop(c_lo + n_blocks * CHUNK_UNROLL, c_hi, tail, 0)

    def gather(buf, c_lo, c_hi):
        for_chunks(buf, c_lo, c_hi, lambda hbm, vm: pltpu.make_async_copy(
            hbm, vm, gather_sem(buf)).start())

    def scatter(buf, c_lo, c_hi):
        for_chunks(buf, c_lo, c_hi, lambda hbm, vm: pltpu.make_async_copy(
            vm, hbm, scatter_sem(buf)).start())

    def wait_chunks(which_sem, n):
        def wait_rows(rows):
            def body(c, carry):
                pltpu.make_async_copy(hl_ref.at[pl.ds(0, rows)], xbuf.at[0, pl.ds(0, rows)],
                                      which_sem).wait()
                return carry
            return body

        lax.fori_loop(0, n // CHUNK_UNROLL, wait_rows(CHUNK_UNROLL * RUN_ALIGN), 0)
        lax.fori_loop(0, n % CHUNK_UNROLL, wait_rows(RUN_ALIGN), 0)

    def evaluate(buf, n_chunks):
        def tile(k, carry):
            sl = pl.ds(pl.multiple_of(k * EXPERT_TILE, EXPERT_TILE), EXPERT_TILE)
            gu = _dot(xbuf[buf, sl, :], wgu_s[...])
            gate = gu[:, :D_EXPERT]
            act = gate * jax.nn.sigmoid(gate) * gu[:, D_EXPERT:]
            xbuf[buf, sl, :] = _dot(act.astype(BF16), wd_s[...]).astype(BF16)
            return carry

        lax.fori_loop(0, (n_chunks * RUN_ALIGN + EXPERT_TILE - 1) // EXPERT_TILE, tile, 0)

    @pl.when(e == 0)
    def _():
        xbuf[...] = jnp.zeros_like(xbuf)
        gather(0, *first_pass(0))
        drain_s[0] = 0

    wgu_s[:, :D_EXPERT] = wg_ref[0].astype(BF16)
    wgu_s[:, D_EXPERT:] = wu_ref[0].astype(BF16)
    wd_s[...] = wd_ref[0].astype(BF16)

    c_lo, c_hi = first_pass(e)
    wait_chunks(gather_sem(mine), c_hi - c_lo)
    wait_chunks(scatter_sem(other), drain_s[0])

    @pl.when(e + 1 < pl.num_programs(0))
    def _():
        gather(other, *first_pass(e + 1))

    evaluate(mine, c_hi - c_lo)
    scatter(mine, c_lo, c_hi)

    def later_pass(state):
        lo, pending = state
        hi = jnp.minimum(lo + pass_chunks, first_s[e + 1])
        wait_chunks(scatter_sem(mine), pending)
        gather(mine, lo, hi)
        wait_chunks(gather_sem(mine), hi - lo)
        evaluate(mine, hi - lo)
        scatter(mine, lo, hi)
        return hi, hi - lo

    _, pending = lax.while_loop(lambda st: st[0] < first_s[e + 1], later_pass,
                                (c_hi, c_hi - c_lo))
    drain_s[0] = pending

    @pl.when(e + 1 == pl.num_programs(0))
    def _():
        wait_chunks(scatter_sem(mine), pending)


def _combine_kernel(x2_ref, route_ref, gf_ref, yl_ref, o_ref):
    slots = lax.broadcasted_iota(jnp.int32, (MOE_SUB, RUN_SLOTS), 1).astype(F32)
    for g in range(x2_ref.shape[0] // MOE_SUB):
        p = slice(g * MOE_SUB, (g + 1) * MOE_SUB)
        route = route_ref[p, :]
        pick = (jnp.where(slots == route[:, 0:1], route[:, 2:3], 0.0)
                + jnp.where(slots == route[:, 1:2], route[:, 3:4], 0.0))
        y = _dot(pick.astype(BF16), yl_ref[g * RUN_SLOTS:(g + 1) * RUN_SLOTS, :])
        o_ref[p, :] = _rms(x2_ref[p, :] + y, gf_ref[...])


def _full(shape):
    return pl.BlockSpec(shape, lambda *_: (0,) * len(shape))


def _params(sem, vmem_limit=VMEM_LIMIT):
    return pltpu.CompilerParams(dimension_semantics=sem, vmem_limit_bytes=vmem_limit)


def _mixer_a_layout(t):
    s_rkv = 3 * A_WIDTH
    s_w = s_rkv + D_DECAY_LORA
    s_a = s_w + D_AAA_LORA
    gap = lambda n: jnp.zeros((t.shape[0], n), t.dtype)
    return jnp.concatenate(
        [t[:, :s_w], gap(XA_OFF - D_DECAY_LORA), t[:, s_w:s_a], gap(XG_OFF - XA_OFF - D_AAA_LORA),
         t[:, s_a:], gap(LORA_PAD - XG_OFF - D_GATE_LORA)], axis=1)


def _place(cols, parts):
    out = jnp.zeros((parts[0][1].shape[0], cols), parts[0][1].dtype)
    for off, arr in parts:
        out = lax.dynamic_update_slice(out, arr, (0, off))
    return out


def kernel(x, norm1_g, w_in, b_gate, tmix_mu, w0, w2, a0, a2, g2, k_k, k_a, r_k, lnx_g, lnx_b,
           w_oA, lnv_g, lnv_b, w_s, b_s, w_oB, w_out, norm2_g, w_rg, b_rg, w_re, b_re,
           w_e_gate, w_e_up, w_e_down, final_g):
    bsz, seq, d = x.shape
    n_tok = bsz * seq
    depth = norm1_g.shape[0]
    assert depth == 1, "the moe kernel fuses the final norm, so it must be the last layer"
    assert bsz % WKV_SEQS == 0 and seq % WKV_TOKENS == 0
    xf = x.reshape(n_tok, d)

    s_rkv = 3 * A_WIDTH
    s_w = s_rkv + D_DECAY_LORA
    s_a = s_w + D_AAA_LORA
    a_cols = s_a + D_GATE_LORA
    b_cols = 2 * B_WIDTH

    ones_bd = (jnp.arange(A_WIDTH)[:, None] // A_HEAD
               == jnp.arange(A_WIDTH)[None, :] // A_HEAD).astype(BF16)

    tm_a = INPROJ_TILE
    tm_p = POST_TILE
    tm_c = COMBINE_TILE
    assert tm_p % MOE_SUB == 0 and tm_c % MOE_SUB == 0 and tm_a % DENSE_SPLIT == 0

    for l in range(depth):
        wl = w_in[l]
        rows_w = 256
        w_a, w_b, w_g = pl.pallas_call(
            _w_in_kernel,
            grid=(d // rows_w,),
            in_specs=[pl.BlockSpec((wl.shape[1], rows_w), lambda i: (0, i))],
            out_specs=[pl.BlockSpec((A_PROJ, rows_w), lambda i: (0, i)),
                       pl.BlockSpec((b_cols, rows_w), lambda i: (0, i)),
                       pl.BlockSpec((2 * d, rows_w), lambda i: (0, i))],
            out_shape=[jax.ShapeDtypeStruct((A_PROJ, d), BF16),
                       jax.ShapeDtypeStruct((b_cols, d), BF16),
                       jax.ShapeDtypeStruct((2 * d, d), BF16)],
            compiler_params=_params(("parallel",)),
            name="w_in_layout",
        )(wl.T)
        mu_a = _mixer_a_layout(tmix_mu[l][None, :])
        w2p = jnp.pad(w2[l], ((0, LANES - D_DECAY_LORA), (0, 0))).astype(BF16)
        a2p = jnp.pad(a2[l], ((0, LANES - D_AAA_LORA), (0, 0))).astype(BF16)
        g2p = jnp.pad(g2[l], ((0, LORA_PAD - XG_OFF - D_GATE_LORA), (0, 0))).astype(BF16)
        g1 = norm1_g[l][None, :]

        row512 = lambda i: (i, 0)
        tok_a = pl.BlockSpec((tm_a, A_WIDTH), row512)
        tok_d = pl.BlockSpec((tm_a, d), row512)
        vec_a = _full((1, A_WIDTH))
        bs_full = jnp.repeat(b_s[l].T, B_GROUP_CH, axis=1)
        outs = pl.pallas_call(
            functools.partial(_inproj_kernel, seq // tm_a),
            grid=(n_tok // tm_a,),
            in_specs=[tok_d,
                      _full((1, d)), _full((A_PROJ, d)), _full((1, A_PROJ)), vec_a,
                      _full((LANES, A_WIDTH)), vec_a, _full((LANES, A_WIDTH)),
                      _full((LORA_PAD - XG_OFF, A_WIDTH)), vec_a, vec_a,
                      _full((A_WIDTH, A_WIDTH)),
                      _full((b_cols, d)), _full((2 * d, d)), _full((1, 2 * d)),
                      _full((1, B_WIDTH)), _full((1, B_WIDTH)),
                      _full((B_GROUPS, GMLP_BLOCK, GMLP_BLOCK)), _full((GMLP_BLOCK, B_WIDTH)),
                      _full((B_WIDTH, d))],
            out_specs=[tok_a] * 7 + [tok_d, tok_d],
            out_shape=[jax.ShapeDtypeStruct((n_tok, A_WIDTH), F32)] * 7
            + [jax.ShapeDtypeStruct((n_tok, d), BF16)] * 2,
            scratch_shapes=[pltpu.VMEM((8, A_PROJ), F32)],
            compiler_params=_params(("arbitrary",)),
            name="inproj",
        )(xf, g1, w_a, mu_a, w0[l][None, :], w2p, a0[l][None, :], a2p, g2p,
          k_k[l][None, :], k_a[l][None, :], ones_bd, w_b, w_g, b_gate[l][None, :],
          lnv_g[l][None, :], lnv_b[l][None, :], w_s[l], bs_full, w_oB[l].astype(BF16))
        r_, k_, v_, a_, b_, ld_, gg_, ybg, ga = outs

        n_pairs = A_WIDTH // HEAD_PAIR
        n_chunks = WKV_TOKENS // WKV_CHUNK
        tok_w = pl.BlockSpec((WKV_SEQS, WKV_TOKENS, A_WIDTH), lambda bi, ti: (bi, ti, 0))
        vec_w = _full((1, A_WIDTH))
        seq_major = lambda t: t.reshape(bsz, seq, A_WIDTH)
        tok_scratch = pltpu.VMEM((WKV_SEQS, WKV_TOKENS, A_WIDTH), BF16)
        ya_in = pl.pallas_call(
            _wkv_kernel,
            grid=(bsz // WKV_SEQS, seq // WKV_TOKENS),
            in_specs=[tok_w] * 7 + [vec_w, vec_w, vec_w, _full((HEAD_PAIR, HEAD_PAIR))],
            out_specs=tok_w,
            out_shape=jax.ShapeDtypeStruct((bsz, seq, A_WIDTH), BF16),
            scratch_shapes=[pltpu.VMEM((WKV_SEQS, n_pairs, HEAD_PAIR, HEAD_PAIR), F32)]
            + [tok_scratch] * 5
            + [pltpu.VMEM((WKV_SEQS, n_chunks, n_pairs, HEAD_PAIR, HEAD_PAIR), BF16),
               pltpu.VMEM((WKV_SEQS, n_chunks, n_pairs, HEAD_PAIR, HEAD_PAIR), F32),
               pltpu.VMEM((WKV_SEQS, WKV_TOKENS, A_WIDTH), F32)],
            compiler_params=_params(("parallel", "arbitrary")),
            name="wkv",
        )(*[seq_major(t) for t in (r_, k_, v_, a_, b_, ld_, gg_)], r_k[l].reshape(1, A_WIDTH),
          lnx_g[l][None, :], lnx_b[l][None, :],
          ones_bd[:HEAD_PAIR, :HEAD_PAIR]).reshape(n_tok, A_WIDTH)

        w_r = _place(LANES, [(0, jnp.transpose(w_re[l], (1, 0, 2)).reshape(d, N_EXPERTS)),
                             (N_EXPERTS, w_rg[l])])
        b_r = _place(LANES, [(0, b_re[l].reshape(1, N_EXPERTS)), (N_EXPERTS, b_rg[l][None, :])])
        wr_hi = w_r.astype(BF16)
        wr_lo = (w_r - wr_hi.astype(F32)).astype(BF16)
        wr_t = jnp.concatenate([wr_hi.T, wr_lo.T], axis=0)
        tok_p = pl.BlockSpec((tm_p, d), row512)
        lane_p = pl.BlockSpec((tm_p, LANES), row512)
        n_sub = n_tok // MOE_SUB
        sorted_rows = tm_p // MOE_SUB * RUN_SLOTS
        x2, h_sorted, route, cnt = pl.pallas_call(
            _post_kernel,
            grid=(n_tok // tm_p,),
            in_specs=[tok_p, pl.BlockSpec((tm_p, A_WIDTH), row512), tok_p, tok_p,
                      _full((A_WIDTH, d)), _full((d, d)), _full((1, d)), _full((2 * LANES, d)),
                      _full((LANES, 1))],
            out_specs=[tok_p, pl.BlockSpec((sorted_rows, d), row512), lane_p,
                       pl.BlockSpec((8, LANES), row512)],
            out_shape=[jax.ShapeDtypeStruct((n_tok, d), F32),
                       jax.ShapeDtypeStruct((n_sub * RUN_SLOTS, d), BF16),
                       jax.ShapeDtypeStruct((n_tok, LANES), F32),
                       jax.ShapeDtypeStruct((n_tok // tm_p * 8, LANES), F32)],
            compiler_params=_params(("parallel",), POST_VMEM_LIMIT),
            name="post",
        )(xf, ya_in, ga, ybg, w_oA[l].astype(BF16), w_out[l].astype(BF16),
          norm2_g[l][None, :], wr_t, b_r.T)

        run_chunks = cnt.reshape(n_tok // tm_p, 8, LANES)[:, :tm_p // MOE_SUB, :N_EXPERTS]
        run_chunks = run_chunks.reshape(n_sub, N_EXPERTS).astype(jnp.int32)
        run_start = (jnp.cumsum(run_chunks, axis=1) - run_chunks) * RUN_ALIGN
        run_row = (jnp.arange(n_sub, dtype=jnp.int32)[:, None] * RUN_SLOTS + run_start).T.reshape(-1)
        run_n = run_chunks.T.reshape(-1)
        run_end = jnp.cumsum(run_n)
        first_chunk = jnp.concatenate([jnp.zeros((1,), jnp.int32), run_end[n_sub - 1::n_sub]])
        pos = jnp.arange(n_sub * RUN_SLOTS // RUN_ALIGN, dtype=jnp.int32)
        run_of = jnp.sum(pos[:, None] >= run_end[None, :], axis=1)
        in_run = run_of[:, None] == jnp.arange(run_n.shape[0], dtype=jnp.int32)[None, :]
        pick = lambda t: jnp.sum(jnp.where(in_run, t[None, :], 0), axis=1)
        chunk_rows = pick(run_row) + (pos - pick(run_end - run_n)) * RUN_ALIGN

        per_expert = lambda e, *_: (e, 0, 0)
        y_sorted = pl.pallas_call(
            _expert_kernel,
            grid_spec=pltpu.PrefetchScalarGridSpec(
                num_scalar_prefetch=2, grid=(N_EXPERTS,),
                in_specs=[pl.BlockSpec(memory_space=pl.ANY),
                          pl.BlockSpec((1, d, D_EXPERT), per_expert),
                          pl.BlockSpec((1, d, D_EXPERT), per_expert),
                          pl.BlockSpec((1, D_EXPERT, d), per_expert)],
                out_specs=pl.BlockSpec(memory_space=pl.ANY),
                scratch_shapes=[pltpu.VMEM((2, EXPERT_ROWS, d), BF16),
                                pltpu.VMEM((d, 2 * D_EXPERT), BF16),
                                pltpu.VMEM((D_EXPERT, d), BF16),
                                pltpu.SemaphoreType.DMA((4,)),
                                pltpu.SMEM((1,), jnp.int32)]),
            out_shape=jax.ShapeDtypeStruct((n_sub * RUN_SLOTS, d), BF16),
            input_output_aliases={2: 0},
            compiler_params=_params(("arbitrary",)),
            name="moe_experts",
        )(chunk_rows, first_chunk, h_sorted, w_e_gate[l], w_e_up[l], w_e_down[l])

        xf = pl.pallas_call(
            _combine_kernel,
            grid=(n_tok // tm_c,),
            in_specs=[pl.BlockSpec((tm_c, d), row512), pl.BlockSpec((tm_c, LANES), row512),
                      _full((1, d)), pl.BlockSpec((tm_c // MOE_SUB * RUN_SLOTS, d), row512)],
            out_specs=pl.BlockSpec((tm_c, d), row512),
            out_shape=jax.ShapeDtypeStruct((n_tok, d), F32),
            compiler_params=_params(("parallel",)),
            name="moe_combine",
        )(x2, route, final_g[None, :], y_sorted)

    return xf.reshape(bsz, seq, d)
```

```python
import functools

import jax
import jax.numpy as jnp
from jax import lax
from jax.experimental import pallas as pl
from jax.experimental.pallas import tpu as pltpu

F32 = jnp.float32
BF16 = jnp.bfloat16

D_MODEL = 1024
A_WIDTH = 512
A_HEAD = 64
D_DECAY_LORA = 64
D_AAA_LORA = 64
D_GATE_LORA = 160
B_WIDTH = 512
B_GROUPS = 4
B_GROUP_CH = 128
GMLP_BLOCK = 128
N_GROUPS = 4
EXPERTS_PER_GROUP = 8
N_EXPERTS = 32
D_EXPERT = 256
NORM_EPS = 1e-6
LN_EPS = 1e-5
LNX_EPS = 64e-5

LANES = 128
LORA_PAD = 512
XW_OFF, XA_OFF, XG_OFF = 0, 128, 256
A_PROJ = 3 * A_WIDTH + LORA_PAD
WKV_CHUNK = 64
HEAD_PAIR = 2 * A_HEAD
WKV_SEQS = 8
WKV_TOKENS = 64
WKV_PREP_GROUP = 4
WKV_NORM_GROUP = 1
MOE_SUB = 256
RUN_ALIGN = 16
RUN_SLOTS = 2 * MOE_SUB + N_EXPERTS * RUN_ALIGN
EXPERT_ROWS = 2048
EXPERT_TILE = 512
INPROJ_TILE = 512
POST_TILE = 1024
COMBINE_TILE = 1024
DENSE_SPLIT = 256
CHUNK_UNROLL = 4
VMEM_LIMIT = 48 * 1024 * 1024
POST_VMEM_LIMIT = 56 * 1024 * 1024


def _rms(x, g):
    return x * lax.rsqrt(jnp.mean(x * x, axis=-1, keepdims=True) + NORM_EPS) * g


def _dot(a, b):
    return jnp.dot(a, b, preferred_element_type=F32)


def _dot_nt(a, b):
    return lax.dot_general(a, b, (((1,), (1,)), ((), ())), preferred_element_type=F32)


def _split2(x):
    hi = x.astype(BF16)
    lo = (x - hi.astype(F32)).astype(BF16)
    return hi, lo


def _split3(x):
    hi = x.astype(BF16)
    r1 = x - hi.astype(F32)
    mid = r1.astype(BF16)
    lo = (r1 - mid.astype(F32)).astype(BF16)
    return hi, mid, lo


def _pair_head_sums(xs, ones_pair):
    n_tiles = A_WIDTH // HEAD_PAIR
    rows = xs[0].shape[0]
    tiles = [x[:, t * HEAD_PAIR:(t + 1) * HEAD_PAIR].astype(BF16) for x in xs for t in range(n_tiles)]
    sums = _dot(jnp.concatenate(tiles, axis=0), ones_pair)
    return [jnp.concatenate([sums[(i * n_tiles + t) * rows:(i * n_tiles + t + 1) * rows]
                             for t in range(n_tiles)], axis=1) for i in range(len(xs))]


def _w_in_kernel(w_ref, wa_out, wb_out, wg_out):
    s_rkv = 3 * A_WIDTH
    s_w = s_rkv + D_DECAY_LORA
    s_a = s_w + D_AAA_LORA
    a_cols = s_a + D_GATE_LORA
    b_end = a_cols + 2 * B_WIDTH
    wa_out[...] = jnp.zeros_like(wa_out)
    wa_out[:s_w, :] = w_ref[:s_w, :].astype(BF16)
    wa_out[s_rkv + XA_OFF:s_rkv + XA_OFF + D_AAA_LORA, :] = w_ref[s_w:s_a, :].astype(BF16)
    wa_out[s_rkv + XG_OFF:s_rkv + XG_OFF + D_GATE_LORA, :] = w_ref[s_a:a_cols, :].astype(BF16)
    wb_out[...] = w_ref[a_cols:b_end, :].astype(BF16)
    wg_out[...] = w_ref[b_end:, :].astype(BF16)


def _inproj_kernel(tiles_per_seq, x_ref, g1_ref, wa_ref, mu_ref, w0_ref, w2_ref, a0_ref, a2_ref,
                   g2_ref, kk_ref, ka_ref, ones_ref, wb_ref, wg_ref, bg_ref, lng_ref, lnb_ref,
                   ws_ref, bs_ref, wo_ref,
                   r_out, k_out, v_out, a_out, b_out, ld_out, g_out, ybg_out, ga_out, tail_ref):
    i = pl.program_id(0)
    tm = x_ref.shape[0]
    parts = [slice(j * DENSE_SPLIT, (j + 1) * DENSE_SPLIT) for j in range(tm // DENSE_SPLIT)]
    g1 = g1_ref[...]
    h = [_rms(x_ref[p, :], g1).astype(BF16) for p in parts]
    proj = [_dot_nt(hh, wa_ref[...]) for hh in h]
    pb = [_dot_nt(hh, wb_ref[...]) for hh in h]

    prev = [jnp.where(i % tiles_per_seq == 0, 0.0, tail_ref[7:8, :])]
    prev += [p[DENSE_SPLIT - 1:, :] for p in proj[:-1]]
    tail_ref[...] = proj[-1][DENSE_SPLIT - 8:, :]
    row = lax.broadcasted_iota(jnp.int32, proj[0].shape, 0)
    pm = []
    for p, pv in zip(proj, prev):
        shifted = jnp.where(row == 0, pv, pltpu.roll(p, 1, axis=0))
        pm.append(p + mu_ref[...] * (shifted - p))
    gates = [jax.nn.sigmoid(_dot_nt(hh, wg_ref[...]) + bg_ref[...]) for hh in h]

    tri = (lax.broadcasted_iota(jnp.int32, (GMLP_BLOCK, GMLP_BLOCK), 0)
           >= lax.broadcasted_iota(jnp.int32, (GMLP_BLOCK, GMLP_BLOCK), 1))
    ws = [jnp.where(tri, ws_ref[grp], 0.0).astype(BF16) for grp in range(B_GROUPS)]
    bs = bs_ref[...]
    us, vns = [], []
    for x in pb:
        z = 0.5 * x * (1.0 + lax.erf(x * (2.0 ** -0.5)))
        us.append(z[:, :B_WIDTH])
        v = z[:, B_WIDTH:]
        mean = jnp.mean(v, axis=-1, keepdims=True)
        vc = v - mean
        var = jnp.mean(vc * vc, axis=-1, keepdims=True)
        vns.append((vc * lax.rsqrt(var + LN_EPS) * lng_ref[...] + lnb_ref[...]).astype(BF16))

    lora = [x[:, 3 * A_WIDTH:] for x in pm]
    decay = [_dot(jnp.tanh(x[:, XW_OFF:XW_OFF + LANES]).astype(BF16), w2_ref[...]) for x in lora]
    rate = [_dot(x[:, XA_OFF:XA_OFF + LANES].astype(BF16), a2_ref[...]) for x in lora]
    gate = [_dot(jax.nn.sigmoid(x[:, XG_OFF:]).astype(BF16), g2_ref[...]) for x in lora]
    kks = [x[:, A_WIDTH:2 * A_WIDTH] * kk_ref[...] for x in pm]
    sq = [_dot((kk * kk).astype(BF16), ones_ref[...]) for kk in kks]

    svs = []
    for vn in vns:
        rows = []
        for blk in range(DENSE_SPLIT // GMLP_BLOCK):
            cols = [_dot(ws[grp], vn[blk * GMLP_BLOCK:(blk + 1) * GMLP_BLOCK,
                                     grp * B_GROUP_CH:(grp + 1) * B_GROUP_CH])
                    for grp in range(B_GROUPS)]
            rows.append(jnp.concatenate(cols, axis=1) + bs)
        svs.append(jnp.concatenate(rows, axis=0))
    yb = [_dot((u * sv).astype(BF16), wo_ref[...]) for u, sv in zip(us, svs)]

    for j, p in enumerate(parts):
        k = pm[j][:, A_WIDTH:2 * A_WIDTH]
        z = -(w0_ref[...] + decay[j])
        softplus = jnp.maximum(z, 0.0) + jnp.log(1.0 + jnp.exp(-jnp.abs(z)))
        w = -softplus - 0.5
        a_lr = jax.nn.sigmoid(a0_ref[...] + rate[j])
        kk = kks[j] / jnp.maximum(jnp.sqrt(sq[j]), 1e-12)
        r_out[p, :] = pm[j][:, 0:A_WIDTH]
        k_out[p, :] = k * (1.0 + (a_lr - 1.0) * ka_ref[...])
        v_out[p, :] = pm[j][:, 2 * A_WIDTH:3 * A_WIDTH]
        a_out[p, :] = -kk
        b_out[p, :] = kk * a_lr
        ld_out[p, :] = -jnp.exp(w)
        g_out[p, :] = gate[j]
        ga_out[p, :] = gates[j][:, :D_MODEL].astype(BF16)
        ybg_out[p, :] = (gates[j][:, D_MODEL:] * yb[j]).astype(BF16)


def _wkv_kernel(r_ref, k_ref, v_ref, a_ref, b_ref, ld_ref, g_ref, rk_ref, lng_ref, lnb_ref,
                ones_ref, o_ref, st_ref, ta_s, tl_s, arb_s, ark_s, rt_s, bkt_s, dcol_s, y_s):
    C = WKV_CHUNK
    bb, tb, _ = r_ref.shape
    n_chunks = tb // C
    n_pairs = A_WIDTH // HEAD_PAIR

    @pl.when(pl.program_id(1) == 0)
    def _():
        st_ref[...] = jnp.zeros_like(st_ref)

    row = lax.broadcasted_iota(jnp.int32, (C, HEAD_PAIR), 0)
    src = lax.broadcasted_iota(jnp.int32, (C, HEAD_PAIR), 1) & (C - 1)
    incl = src <= row
    strict = src < row
    eye_pair = jnp.where(src == row, 1.0, 0.0)
    bd_mask = ((lax.broadcasted_iota(jnp.int32, (HEAD_PAIR, HEAD_PAIR), 0) >= A_HEAD)
               == (lax.broadcasted_iota(jnp.int32, (HEAD_PAIR, HEAD_PAIR), 1) >= A_HEAD))
    tri_c = (lax.broadcasted_iota(jnp.int32, (C, C), 0)
             >= lax.broadcasted_iota(jnp.int32, (C, C), 1)).astype(BF16)
    ones_pair = ones_ref[...]
    pair_cols = [slice(p * HEAD_PAIR, (p + 1) * HEAD_PAIR) for p in range(n_pairs)]

    def bd(x):
        xb = x.astype(BF16)
        return jnp.where(bd_mask, jnp.concatenate([xb, xb], axis=0), jnp.zeros((), BF16))

    def prep(it, carry):
        where, lhs, rhs, ats = [], [], [], []
        for j in range(WKV_PREP_GROUP):
            flat = it * WKV_PREP_GROUP + j
            b = flat // n_chunks
            c = flat % n_chunks
            sl = pl.ds(pl.multiple_of(c * C, C), C)
            ld = ld_ref[b, sl, :]
            hi, mid, lo = _split3(ld)
            cl = _dot(tri_c, hi) + _dot(tri_c, mid) + _dot(tri_c, lo)
            cl_end = cl[C - 1:C, :]
            k = k_ref[b, sl, :]
            bv = b_ref[b, sl, :]
            d_inv = jnp.exp(-cl)
            d_tail = jnp.exp(cl_end - cl)
            rt = (r_ref[b, sl, :] * jnp.exp(cl)).astype(BF16)
            rt_s[b, sl, :] = rt
            kt = k * d_inv
            bt = bv * d_inv
            at = a_ref[b, sl, :] * jnp.exp(cl - ld)
            kd = k * d_tail
            bdk = bv * d_tail
            d_end = jnp.exp(cl_end)
            for p, cs in enumerate(pair_cols):
                bkt_s[b, c, p] = jnp.concatenate([bdk[:, cs], kd[:, cs]], axis=0).T.astype(BF16)
                dcol_s[b, c, p] = jnp.broadcast_to(d_end[:, cs], (HEAD_PAIR, HEAD_PAIR)).T
                where.append((b, sl, cs))
                ats.append(at[:, cs])
                lhs.append(jnp.concatenate([rt[:, cs], at[:, cs].astype(BF16)], axis=0))
                rhs.append(jnp.concatenate([bd(kt[:, cs]), bd(bt[:, cs])], axis=0))
        n = len(where)
        amat = [_dot_nt(lhs[i], rhs[i]) for i in range(n)]
        l_ak, l_ab = [], []
        for i, (b, sl, cs) in enumerate(where):
            ark_s[b, sl, cs] = jnp.where(incl, amat[i][:C, :HEAD_PAIR], 0.0).astype(BF16)
            arb_s[b, sl, cs] = jnp.where(incl, amat[i][:C, HEAD_PAIR:], 0.0).astype(BF16)
            l_ak.append(jnp.where(strict, amat[i][C:, :HEAD_PAIR], 0.0))
            l_ab.append(jnp.where(strict, amat[i][C:, HEAD_PAIR:], 0.0))

        t_mat = [eye_pair + l for l in l_ab]
        q = [_dot(l.astype(BF16), bd(l)) for l in l_ab]
        n_sq = 1
        while 2 * n_sq < C // 2:
            both = [_dot(q[i].astype(BF16), jnp.concatenate([bd(q[i]), bd(t_mat[i])], axis=1))
                    for i in range(n)]
            q = [x[:, :HEAD_PAIR] for x in both]
            t_mat = [t_mat[i] + both[i][:, HEAD_PAIR:] for i in range(n)]
            n_sq *= 2
        t_mat = [t_mat[i] + _dot(q[i].astype(BF16), bd(t_mat[i])) for i in range(n)]
        tal = [_dot(t_mat[i].astype(BF16), jnp.concatenate([bd(ats[i]), bd(l_ak[i])], axis=1))
               for i in range(n)]
        for i, (b, sl, cs) in enumerate(where):
            ta_s[b, sl, cs] = tal[i][:, :HEAD_PAIR].astype(BF16)
            tl_s[b, sl, cs] = tal[i][:, HEAD_PAIR:].astype(BF16)
        return carry

    lax.fori_loop(0, bb * n_chunks // WKV_PREP_GROUP, prep, 0)

    def step(c, carry):
        sl = pl.ds(pl.multiple_of(c * C, C), C)
        chains = [(b, p) for b in range(bb) for p in range(n_pairs)]
        v = [v_ref[b, sl, :] for b in range(bb)]
        st = [st_ref[b, p] for b, p in chains]
        st_b = [x.astype(BF16) for x in st]
        bd_v = [bd(v[b][:, pair_cols[p]]) for b, p in chains]
        on_st = [_dot(jnp.concatenate([ta_s[b, sl, pair_cols[p]], rt_s[b, sl, pair_cols[p]]],
                                      axis=0), st_b[i]) for i, (b, p) in enumerate(chains)]
        on_v = [_dot(jnp.concatenate([tl_s[b, sl, pair_cols[p]], ark_s[b, sl, pair_cols[p]]],
                                     axis=0), bd_v[i]) for i, (b, p) in enumerate(chains)]
        u = [on_st[i][:C] + on_v[i][:C] for i in range(len(chains))]
        for i, (b, p) in enumerate(chains):
            uv = jnp.concatenate([u[i], v[b][:, pair_cols[p]]], axis=0).astype(BF16)
            st_ref[b, p] = (dcol_s[b, c, p] * st[i]
                            + jnp.where(bd_mask, _dot(bkt_s[b, c, p], uv), 0.0))
        ys = [on_st[i][C:] + on_v[i][C:] + _dot(arb_s[b, sl, pair_cols[p]], bd(u[i]))
              for i, (b, p) in enumerate(chains)]
        for b in range(bb):
            y_s[b, sl, :] = jnp.concatenate(ys[b * n_pairs:(b + 1) * n_pairs], axis=1)
        return carry

    lax.fori_loop(0, n_chunks, step, 0)

    def finish(it, carry):
        items = [(b, pl.ds(pl.multiple_of((it * WKV_NORM_GROUP + j) * C, C), C))
                 for j in range(WKV_NORM_GROUP) for b in range(bb)]
        y = [y_s[b, sl, :] for b, sl in items]
        rkr = [r_ref[b, sl, :] * k_ref[b, sl, :] * rk_ref[...] for b, sl in items]
        sums = [_pair_head_sums([y[i], rkr[i]], ones_pair) for i in range(len(items))]
        yc = [y[i] - sums[i][0] * (1.0 / A_HEAD) for i in range(len(items))]
        var = [_pair_head_sums([x * x], ones_pair)[0] * (1.0 / A_HEAD) for x in yc]
        for i, (b, sl) in enumerate(items):
            yn = yc[i] * lax.rsqrt(var[i] + LNX_EPS) * lng_ref[...] + lnb_ref[...]
            out = (yn + sums[i][1] * v_ref[b, sl, :]) * g_ref[b, sl, :]
            o_ref[b, sl, :] = out.astype(o_ref.dtype)
        return carry

    lax.fori_loop(0, n_chunks // WKV_NORM_GROUP, finish, 0)


def _post_kernel(x_ref, ya_ref, ga_ref, ybg_ref, woa_ref, wout_ref, g2_ref, wrt_ref, brt_ref,
                 x2_out, hl_out, route_out, cnt_out):
    n_sub = x_ref.shape[0] // MOE_SUB
    groups = [slice(g * MOE_SUB, (g + 1) * MOE_SUB) for g in range(n_sub)]
    y_a = [_dot(ya_ref[p, :].astype(BF16), woa_ref[...]) for p in groups]
    x2 = [x_ref[p, :] + _dot((ga_ref[p, :] * y + ybg_ref[p, :]).astype(BF16), wout_ref[...])
          for p, y in zip(groups, y_a)]
    h2 = [_rms(x, g2_ref[...]) for x in x2]
    split = [_split2(h) for h in h2]
    on_hi = [_dot_nt(wrt_ref[...], hi) for hi, _ in split]
    on_lo = [_dot_nt(wrt_ref[0:LANES, :], lo) for _, lo in split]
    row = lax.broadcasted_iota(jnp.int32, (LANES, MOE_SUB), 0)
    neg = jnp.float32(-jnp.inf)
    big = jnp.int32(LANES)
    is_grp = (row >= N_EXPERTS) & (row < N_EXPERTS + N_GROUPS)
    top = lambda v: jnp.max(v, axis=0, keepdims=True)
    first = lambda hit: jnp.min(jnp.where(hit, row, big), axis=0, keepdims=True)
    picks, onehots, weights = [], [], []
    for g, p in enumerate(groups):
        x2_out[p, :] = x2[g].astype(x2_out.dtype)
        logits = on_hi[g][:LANES, :] + on_hi[g][LANES:, :] + on_lo[g] + brt_ref[...]
        gl = jnp.where(is_grp, logits, neg)
        gmax = top(gl)
        g_p = 1.0 / jnp.sum(jnp.exp(gl - gmax), axis=0, keepdims=True)
        lo_row = (first(gl == gmax) - N_EXPERTS) * EXPERTS_PER_GROUP
        el = jnp.where((row >= lo_row) & (row < lo_row + EXPERTS_PER_GROUP), logits, neg)
        e1 = top(el)
        i1 = first(el == e1)
        el2 = jnp.where(row == i1, neg, el)
        e2 = top(el2)
        i2 = first(el2 == e2)
        t = jnp.exp(e2 - e1)
        weights.append((g_p / (1.0 + t), g_p * t / (1.0 + t)))
        picks.append((i1, i2))
        onehots.append(jnp.where((row == i1) | (row == i2), 1.0, 0.0))

    earlier = (lax.broadcasted_iota(jnp.int32, (MOE_SUB, MOE_SUB), 0)
               <= lax.broadcasted_iota(jnp.int32, (MOE_SUB, MOE_SUB), 1)).astype(BF16)
    incl = [_dot(oh.astype(BF16), earlier) for oh in onehots]
    chunks = [jnp.ceil(x[:, MOE_SUB - 1:] * (1.0 / RUN_ALIGN)) for x in incl]
    below = (lax.broadcasted_iota(jnp.int32, (LANES, LANES), 1)
             < lax.broadcasted_iota(jnp.int32, (LANES, LANES), 0)).astype(BF16)
    wide = [jnp.broadcast_to(c, (LANES, LANES)) for c in chunks]
    run_start = [_dot(below, w.astype(BF16))[:, 0:1] * RUN_ALIGN for w in wide]
    row8 = lax.broadcasted_iota(jnp.int32, (8, LANES), 0)
    counts = jnp.zeros((8, LANES), F32)
    slot_ids = lax.broadcasted_iota(jnp.int32, (RUN_SLOTS, MOE_SUB), 0)
    for g, p in enumerate(groups):
        counts = jnp.where(row8 == g, wide[g].T[0:8, :], counts)
        slot_of = run_start[g] + incl[g] - onehots[g]
        i1, i2 = picks[g]
        slot1 = jnp.sum(jnp.where(row == i1, slot_of, 0.0), axis=0, keepdims=True)
        slot2 = jnp.sum(jnp.where(row == i2, slot_of, 0.0), axis=0, keepdims=True)
        w1, w2 = weights[g]
        packed = jnp.where(row == 0, slot1, jnp.where(row == 1, slot2, jnp.where(
            row == 2, w1, jnp.where(row == 3, w2, 0.0))))
        route_out[p, :] = packed.T
        pick = jnp.where((slot_ids == slot1.astype(jnp.int32))
                         | (slot_ids == slot2.astype(jnp.int32)), 1.0, 0.0).astype(BF16)
        hl_out[g * RUN_SLOTS:(g + 1) * RUN_SLOTS, :] = _dot(pick, h2[g].astype(BF16)).astype(BF16)
    cnt_out[...] = counts


def _expert_kernel(chunk_s, first_s, hl_in_ref, wg_ref, wu_ref, wd_ref, hl_ref, xbuf, wgu_s, wd_s,
                   sem, drain_s):
    del hl_in_ref
    e = pl.program_id(0)
    pass_chunks = EXPERT_ROWS // RUN_ALIGN
    mine = e % 2
    other = 1 - mine
    gather_sem = lambda buf: sem.at[buf]
    scatter_sem = lambda buf: sem.at[2 + buf]

    def first_pass(ex):
        lo = first_s[ex]
        return lo, jnp.minimum(lo + pass_chunks, first_s[ex + 1])

    def for_chunks(buf, c_lo, c_hi, fn):
        def one(c):
            fn(hl_ref.at[pl.ds(pl.multiple_of(chunk_s[c], RUN_ALIGN), RUN_ALIGN)],
               xbuf.at[buf, pl.ds(pl.multiple_of((c - c_lo) * RUN_ALIGN, RUN_ALIGN), RUN_ALIGN)])

        def block(i, carry):
            for u in range(CHUNK_UNROLL):
                one(c_lo + i * CHUNK_UNROLL + u)
            return carry

        def tail(c, carry):
            one(c)
            return carry

        n_blocks = (c_hi - c_lo) // CHUNK_UNROLL
        lax.fori_loop(0, n_blocks, block, 0)
        lax.fori_loop(c_lo + n_blocks * CHUNK_UNROLL, c_hi, tail, 0)

    def gather(buf, c_lo, c_hi):
        for_chunks(buf, c_lo, c_hi, lambda hbm, vm: pltpu.make_async_copy(
            hbm, vm, gather_sem(buf)).start())

    def scatter(buf, c_lo, c_hi):
        for_chunks(buf, c_lo, c_hi, lambda hbm, vm: pltpu.make_async_copy(
            vm, hbm, scatter_sem(buf)).start())

    def wait_chunks(which_sem, n):
        def wait_rows(rows):
            def body(c, carry):
                pltpu.make_async_copy(hl_ref.at[pl.ds(0, rows)], xbuf.at[0, pl.ds(0, rows)],
                                      which_sem).wait()
                return carry
            return body

        lax.fori_loop(0, n // CHUNK_UNROLL, wait_rows(CHUNK_UNROLL * RUN_ALIGN), 0)
        lax.fori_loop(0, n % CHUNK_UNROLL, wait_rows(RUN_ALIGN), 0)

    def evaluate(buf, n_chunks):
        def tile(k, carry):
            sl = pl.ds(pl.multiple_of(k * EXPERT_TILE, EXPERT_TILE), EXPERT_TILE)
            gu = _dot(xbuf[buf, sl, :], wgu_s[...])
            gate = gu[:, :D_EXPERT]
            act = gate * jax.nn.sigmoid(gate) * gu[:, D_EXPERT:]
            xbuf[buf, sl, :] = _dot(act.astype(BF16), wd_s[...]).astype(BF16)
            return carry

        lax.fori_loop(0, (n_chunks * RUN_ALIGN + EXPERT_TILE - 1) // EXPERT_TILE, tile, 0)

    @pl.when(e == 0)
    def _():
        xbuf[...] = jnp.zeros_like(xbuf)
        gather(0, *first_pass(0))
        drain_s[0] = 0

    wgu_s[:, :D_EXPERT] = wg_ref[0].astype(BF16)
    wgu_s[:, D_EXPERT:] = wu_ref[0].astype(BF16)
    wd_s[...] = wd_ref[0].astype(BF16)

    c_lo, c_hi = first_pass(e)
    wait_chunks(gather_sem(mine), c_hi - c_lo)
    wait_chunks(scatter_sem(other), drain_s[0])

    @pl.when(e + 1 < pl.num_programs(0))
    def _():
        gather(other, *first_pass(e + 1))

    evaluate(mine, c_hi - c_lo)
    scatter(mine, c_lo, c_hi)

    def later_pass(state):
        lo, pending = state
        hi = jnp.minimum(lo + pass_chunks, first_s[e + 1])
        wait_chunks(scatter_sem(mine), pending)
        gather(mine, lo, hi)
        wait_chunks(gather_sem(mine), hi - lo)
        evaluate(mine, hi - lo)
        scatter(mine, lo, hi)
        return hi, hi - lo

    _, pending = lax.while_loop(lambda st: st[0] < first_s[e + 1], later_pass,
                                (c_hi, c_hi - c_lo))
    drain_s[0] = pending

    @pl.when(e + 1 == pl.num_programs(0))
    def _():
        wait_chunks(scatter_sem(mine), pending)


def _combine_kernel(x2_ref, route_ref, gf_ref, yl_ref, o_ref):
    slots = lax.broadcasted_iota(jnp.int32, (MOE_SUB, RUN_SLOTS), 1).astype(F32)
    for g in range(x2_ref.shape[0] // MOE_SUB):
        p = slice(g * MOE_SUB, (g + 1) * MOE_SUB)
        route = route_ref[p, :]
        pick = (jnp.where(slots == route[:, 0:1], route[:, 2:3], 0.0)
                + jnp.where(slots == route[:, 1:2], route[:, 3:4], 0.0))
        y = _dot(pick.astype(BF16), yl_ref[g * RUN_SLOTS:(g + 1) * RUN_SLOTS, :])
        o_ref[p, :] = _rms(x2_ref[p, :] + y, gf_ref[...])


def _full(shape):
    return pl.BlockSpec(shape, lambda *_: (0,) * len(shape))


def _params(sem, vmem_limit=VMEM_LIMIT):
    return pltpu.CompilerParams(dimension_semantics=sem, vmem_limit_bytes=vmem_limit)


def _mixer_a_layout(t):
    s_rkv = 3 * A_WIDTH
    s_w = s_rkv + D_DECAY_LORA
    s_a = s_w + D_AAA_LORA
    gap = lambda n: jnp.zeros((t.shape[0], n), t.dtype)
    return jnp.concatenate(
        [t[:, :s_w], gap(XA_OFF - D_DECAY_LORA), t[:, s_w:s_a], gap(XG_OFF - XA_OFF - D_AAA_LORA),
         t[:, s_a:], gap(LORA_PAD - XG_OFF - D_GATE_LORA)], axis=1)


def _place(cols, parts):
    out = jnp.zeros((parts[0][1].shape[0], cols), parts[0][1].dtype)
    for off, arr in parts:
        out = lax.dynamic_update_slice(out, arr, (0, off))
    return out


def kernel(x, norm1_g, w_in, b_gate, tmix_mu, w0, w2, a0, a2, g2, k_k, k_a, r_k, lnx_g, lnx_b,
           w_oA, lnv_g, lnv_b, w_s, b_s, w_oB, w_out, norm2_g, w_rg, b_rg, w_re, b_re,
           w_e_gate, w_e_up, w_e_down, final_g):
    bsz, seq, d = x.shape
    n_tok = bsz * seq
    depth = norm1_g.shape[0]
    assert depth == 1, "the moe kernel fuses the final norm, so it must be the last layer"
    assert bsz % WKV_SEQS == 0 and seq % WKV_TOKENS == 0
    xf = x.reshape(n_tok, d)

    s_rkv = 3 * A_WIDTH
    s_w = s_rkv + D_DECAY_LORA
    s_a = s_w + D_AAA_LORA
    a_cols = s_a + D_GATE_LORA
    b_cols = 2 * B_WIDTH

    ones_bd = (jnp.arange(A_WIDTH)[:, None] // A_HEAD
               == jnp.arange(A_WIDTH)[None, :] // A_HEAD).astype(BF16)

    tm_a = INPROJ_TILE
    tm_p = POST_TILE
    tm_c = COMBINE_TILE
    assert tm_p % MOE_SUB == 0 and tm_c % MOE_SUB == 0 and tm_a % DENSE_SPLIT == 0
    assert seq % tm_a == 0, "an inproj tile must not straddle two sequences (token shift)"

    for l in range(depth):
        wl = w_in[l]
        rows_w = 256
        w_a, w_b, w_g = pl.pallas_call(
            _w_in_kernel,
            grid=(d // rows_w,),
            in_specs=[pl.BlockSpec((wl.shape[1], rows_w), lambda i: (0, i))],
            out_specs=[pl.BlockSpec((A_PROJ, rows_w), lambda i: (0, i)),
                       pl.BlockSpec((b_cols, rows_w), lambda i: (0, i)),
                       pl.BlockSpec((2 * d, rows_w), lambda i: (0, i))],
            out_shape=[jax.ShapeDtypeStruct((A_PROJ, d), BF16),
                       jax.ShapeDtypeStruct((b_cols, d), BF16),
                       jax.ShapeDtypeStruct((2 * d, d), BF16)],
            compiler_params=_params(("parallel",)),
            name="w_in_layout",
        )(wl.T)
        mu_a = _mixer_a_layout(tmix_mu[l][None, :])
        w2p = jnp.pad(w2[l], ((0, LANES - D_DECAY_LORA), (0, 0))).astype(BF16)
        a2p = jnp.pad(a2[l], ((0, LANES - D_AAA_LORA), (0, 0))).astype(BF16)
        g2p = jnp.pad(g2[l], ((0, LORA_PAD - XG_OFF - D_GATE_LORA), (0, 0))).astype(BF16)
        g1 = norm1_g[l][None, :]

        row512 = lambda i: (i, 0)
        tok_a = pl.BlockSpec((tm_a, A_WIDTH), row512)
        tok_d = pl.BlockSpec((tm_a, d), row512)
        vec_a = _full((1, A_WIDTH))
        bs_full = jnp.repeat(b_s[l].T, B_GROUP_CH, axis=1)
        outs = pl.pallas_call(
            functools.partial(_inproj_kernel, seq // tm_a),
            grid=(n_tok // tm_a,),
            in_specs=[tok_d,
                      _full((1, d)), _full((A_PROJ, d)), _full((1, A_PROJ)), vec_a,
                      _full((LANES, A_WIDTH)), vec_a, _full((LANES, A_WIDTH)),
                      _full((LORA_PAD - XG_OFF, A_WIDTH)), vec_a, vec_a,
                      _full((A_WIDTH, A_WIDTH)),
                      _full((b_cols, d)), _full((2 * d, d)), _full((1, 2 * d)),
                      _full((1, B_WIDTH)), _full((1, B_WIDTH)),
                      _full((B_GROUPS, GMLP_BLOCK, GMLP_BLOCK)), _full((GMLP_BLOCK, B_WIDTH)),
                      _full((B_WIDTH, d))],
            out_specs=[tok_a] * 7 + [tok_d, tok_d],
            out_shape=[jax.ShapeDtypeStruct((n_tok, A_WIDTH), F32)] * 7
            + [jax.ShapeDtypeStruct((n_tok, d), BF16)] * 2,
            scratch_shapes=[pltpu.VMEM((8, A_PROJ), F32)],
            compiler_params=_params(("arbitrary",)),
            name="inproj",
        )(xf, g1, w_a, mu_a, w0[l][None, :], w2p, a0[l][None, :], a2p, g2p,
          k_k[l][None, :], k_a[l][None, :], ones_bd, w_b, w_g, b_gate[l][None, :],
          lnv_g[l][None, :], lnv_b[l][None, :], w_s[l], bs_full, w_oB[l].astype(BF16))
        r_, k_, v_, a_, b_, ld_, gg_, ybg, ga = outs

        n_pairs = A_WIDTH // HEAD_PAIR
        n_chunks = WKV_TOKENS // WKV_CHUNK
        tok_w = pl.BlockSpec((WKV_SEQS, WKV_TOKENS, A_WIDTH), lambda bi, ti: (bi, ti, 0))
        vec_w = _full((1, A_WIDTH))
        seq_major = lambda t: t.reshape(bsz, seq, A_WIDTH)
        tok_scratch = pltpu.VMEM((WKV_SEQS, WKV_TOKENS, A_WIDTH), BF16)
        ya_in = pl.pallas_call(
            _wkv_kernel,
            grid=(bsz // WKV_SEQS, seq // WKV_TOKENS),
            in_specs=[tok_w] * 7 + [vec_w, vec_w, vec_w, _full((HEAD_PAIR, HEAD_PAIR))],
            out_specs=tok_w,
            out_shape=jax.ShapeDtypeStruct((bsz, seq, A_WIDTH), BF16),
            scratch_shapes=[pltpu.VMEM((WKV_SEQS, n_pairs, HEAD_PAIR, HEAD_PAIR), F32)]
            + [tok_scratch] * 5
            + [pltpu.VMEM((WKV_SEQS, n_chunks, n_pairs, HEAD_PAIR, HEAD_PAIR), BF16),
               pltpu.VMEM((WKV_SEQS, n_chunks, n_pairs, HEAD_PAIR, HEAD_PAIR), F32),
               pltpu.VMEM((WKV_SEQS, WKV_TOKENS, A_WIDTH), F32)],
            compiler_params=_params(("parallel", "arbitrary")),
            name="wkv",
        )(*[seq_major(t) for t in (r_, k_, v_, a_, b_, ld_, gg_)], r_k[l].reshape(1, A_WIDTH),
          lnx_g[l][None, :], lnx_b[l][None, :],
          ones_bd[:HEAD_PAIR, :HEAD_PAIR]).reshape(n_tok, A_WIDTH)

        w_r = _place(LANES, [(0, jnp.transpose(w_re[l], (1, 0, 2)).reshape(d, N_EXPERTS)),
                             (N_EXPERTS, w_rg[l])])
        b_r = _place(LANES, [(0, b_re[l].reshape(1, N_EXPERTS)), (N_EXPERTS, b_rg[l][None, :])])
        wr_hi = w_r.astype(BF16)
        wr_lo = (w_r - wr_hi.astype(F32)).astype(BF16)
        wr_t = jnp.concatenate([wr_hi.T, wr_lo.T], axis=0)
        tok_p = pl.BlockSpec((tm_p, d), row512)
        lane_p = pl.BlockSpec((tm_p, LANES), row512)
        n_sub = n_tok // MOE_SUB
        sorted_rows = tm_p // MOE_SUB * RUN_SLOTS
        x2, h_sorted, route, cnt = pl.pallas_call(
            _post_kernel,
            grid=(n_tok // tm_p,),
            in_specs=[tok_p, pl.BlockSpec((tm_p, A_WIDTH), row512), tok_p, tok_p,
                      _full((A_WIDTH, d)), _full((d, d)), _full((1, d)), _full((2 * LANES, d)),
                      _full((LANES, 1))],
            out_specs=[tok_p, pl.BlockSpec((sorted_rows, d), row512), lane_p,
                       pl.BlockSpec((8, LANES), row512)],
            out_shape=[jax.ShapeDtypeStruct((n_tok, d), F32),
                       jax.ShapeDtypeStruct((n_sub * RUN_SLOTS, d), BF16),
                       jax.ShapeDtypeStruct((n_tok, LANES), F32),
                       jax.ShapeDtypeStruct((n_tok // tm_p * 8, LANES), F32)],
            compiler_params=_params(("parallel",), POST_VMEM_LIMIT),
            name="post",
        )(xf, ya_in, ga, ybg, w_oA[l].astype(BF16), w_out[l].astype(BF16),
          norm2_g[l][None, :], wr_t, b_r.T)

        run_chunks = cnt.reshape(n_tok // tm_p, 8, LANES)[:, :tm_p // MOE_SUB, :N_EXPERTS]
        run_chunks = run_chunks.reshape(n_sub, N_EXPERTS).astype(jnp.int32)
        run_start = (jnp.cumsum(run_chunks, axis=1) - run_chunks) * RUN_ALIGN
        group_end = jnp.cumsum(run_chunks, axis=0)
        first_chunk = jnp.concatenate([jnp.zeros((1,), jnp.int32), jnp.cumsum(group_end[-1])])
        pos = jnp.arange(n_sub * RUN_SLOTS // RUN_ALIGN, dtype=jnp.int32)
        e_of = jnp.minimum(jnp.sum(pos[:, None] >= first_chunk[None, 1:], axis=1), N_EXPERTS - 1)
        of_e = (e_of[:, None] == jnp.arange(N_EXPERTS, dtype=jnp.int32)[None, :])
        column = lambda t: jnp.dot(of_e.astype(F32), t.T.astype(F32),
                                   precision=lax.Precision.HIGHEST).astype(jnp.int32)
        in_expert = pos - jnp.sum(jnp.where(of_e, first_chunk[None, :-1], 0), axis=1)
        ends = column(group_end)
        g_of = jnp.minimum(jnp.sum(in_expert[:, None] >= ends, axis=1), n_sub - 1)
        of_g = g_of[:, None] == jnp.arange(n_sub, dtype=jnp.int32)[None, :]
        at_g = lambda t: jnp.sum(jnp.where(of_g, t, 0), axis=1)
        in_run = in_expert - at_g(ends - column(run_chunks))
        chunk_rows = g_of * RUN_SLOTS + at_g(column(run_start)) + in_run * RUN_ALIGN

        per_expert = lambda e, *_: (e, 0, 0)
        y_sorted = pl.pallas_call(
            _expert_kernel,
            grid_spec=pltpu.PrefetchScalarGridSpec(
                num_scalar_prefetch=2, grid=(N_EXPERTS,),
                in_specs=[pl.BlockSpec(memory_space=pl.ANY),
                          pl.BlockSpec((1, d, D_EXPERT), per_expert),
                          pl.BlockSpec((1, d, D_EXPERT), per_expert),
                          pl.BlockSpec((1, D_EXPERT, d), per_expert)],
                out_specs=pl.BlockSpec(memory_space=pl.ANY),
                scratch_shapes=[pltpu.VMEM((2, EXPERT_ROWS, d), BF16),
                                pltpu.VMEM((d, 2 * D_EXPERT), BF16),
                                pltpu.VMEM((D_EXPERT, d), BF16),
                                pltpu.SemaphoreType.DMA((4,)),
                                pltpu.SMEM((1,), jnp.int32)]),
            out_shape=jax.ShapeDtypeStruct((n_sub * RUN_SLOTS, d), BF16),
            input_output_aliases={2: 0},
            compiler_params=_params(("arbitrary",)),
            name="moe_experts",
        )(chunk_rows, first_chunk, h_sorted, w_e_gate[l], w_e_up[l], w_e_down[l])

        xf = pl.pallas_call(
            _combine_kernel,
            grid=(n_tok // tm_c,),
            in_specs=[pl.BlockSpec((tm_c, d), row512), pl.BlockSpec((tm_c, LANES), row512),
                      _full((1, d)), pl.BlockSpec((tm_c // MOE_SUB * RUN_SLOTS, d), row512)],
            out_specs=pl.BlockSpec((tm_c, d), row512),
            out_shape=jax.ShapeDtypeStruct((n_tok, d), F32),
            compiler_params=_params(("parallel",)),
            name="moe_combine",
        )(x2, route, final_g[None, :], y_sorted)

    return xf.reshape(bsz, seq, d)
```

```python
import functools

import jax
import jax.numpy as jnp
from jax import lax
from jax.experimental import pallas as pl
from jax.experimental.pallas import tpu as pltpu

F32 = jnp.float32
BF16 = jnp.bfloat16

D_MODEL = 1024
A_WIDTH = 512
A_HEAD = 64
D_DECAY_LORA = 64
D_AAA_LORA = 64
D_GATE_LORA = 160
B_WIDTH = 512
B_GROUPS = 4
B_GROUP_CH = 128
GMLP_BLOCK = 128
N_GROUPS = 4
EXPERTS_PER_GROUP = 8
N_EXPERTS = 32
D_EXPERT = 256
NORM_EPS = 1e-6
LN_EPS = 1e-5
LNX_EPS = 64e-5

LANES = 128
LORA_PAD = 512
XW_OFF, XA_OFF, XG_OFF = 0, 128, 256
A_PROJ = 3 * A_WIDTH + LORA_PAD
WKV_CHUNK = 64
HEAD_PAIR = 2 * A_HEAD
WKV_SEQS = 8
WKV_TOKENS = 64
WKV_PREP_GROUP = 4
WKV_NORM_GROUP = 1
MOE_SUB = 256
RUN_ALIGN = 16
RUN_SLOTS = 2 * MOE_SUB + N_EXPERTS * RUN_ALIGN
EXPERT_ROWS = 2048
EXPERT_TILE = 512
INPROJ_TILE = 512
POST_TILE = 1024
COMBINE_TILE = 1024
DENSE_SPLIT = 256
CHUNK_UNROLL = 8
VMEM_LIMIT = 48 * 1024 * 1024
POST_VMEM_LIMIT = 56 * 1024 * 1024


def _rms(x, g):
    return x * lax.rsqrt(jnp.mean(x * x, axis=-1, keepdims=True) + NORM_EPS) * g


def _dot(a, b):
    return jnp.dot(a, b, preferred_element_type=F32)


def _dot_nt(a, b):
    return lax.dot_general(a, b, (((1,), (1,)), ((), ())), preferred_element_type=F32)


def _split2(x):
    hi = x.astype(BF16)
    lo = (x - hi.astype(F32)).astype(BF16)
    return hi, lo


def _split3(x):
    hi = x.astype(BF16)
    r1 = x - hi.astype(F32)
    mid = r1.astype(BF16)
    lo = (r1 - mid.astype(F32)).astype(BF16)
    return hi, mid, lo


def _pair_head_sums(xs, ones_pair):
    n_tiles = A_WIDTH // HEAD_PAIR
    rows = xs[0].shape[0]
    tiles = [x[:, t * HEAD_PAIR:(t + 1) * HEAD_PAIR].astype(BF16) for x in xs for t in range(n_tiles)]
    sums = _dot(jnp.concatenate(tiles, axis=0), ones_pair)
    return [jnp.concatenate([sums[(i * n_tiles + t) * rows:(i * n_tiles + t + 1) * rows]
                             for t in range(n_tiles)], axis=1) for i in range(len(xs))]


def _w_in_kernel(w_ref, woa_ref, wob_ref, wout_ref, wa_out, wb_out, wg_out, woa_out, wob_out,
                 wout_out):
    woa_out[...] = woa_ref[...].astype(BF16)
    wob_out[...] = wob_ref[...].astype(BF16)
    wout_out[...] = wout_ref[...].astype(BF16)
    s_rkv = 3 * A_WIDTH
    s_w = s_rkv + D_DECAY_LORA
    s_a = s_w + D_AAA_LORA
    a_cols = s_a + D_GATE_LORA
    b_end = a_cols + 2 * B_WIDTH
    wa_out[...] = jnp.zeros_like(wa_out)
    wa_out[:s_w, :] = w_ref[:s_w, :].astype(BF16)
    wa_out[s_rkv + XA_OFF:s_rkv + XA_OFF + D_AAA_LORA, :] = w_ref[s_w:s_a, :].astype(BF16)
    wa_out[s_rkv + XG_OFF:s_rkv + XG_OFF + D_GATE_LORA, :] = w_ref[s_a:a_cols, :].astype(BF16)
    wb_out[...] = w_ref[a_cols:b_end, :].astype(BF16)
    wg_out[...] = w_ref[b_end:, :].astype(BF16)


def _inproj_kernel(tiles_per_seq, x_ref, g1_ref, wa_ref, mu_ref, w0_ref, w2_ref, a0_ref, a2_ref,
                   g2_ref, kk_ref, ka_ref, ones_ref, wb_ref, wg_ref, bg_ref, lng_ref, lnb_ref,
                   ws_ref, bs_ref, wo_ref,
                   r_out, k_out, v_out, a_out, b_out, ld_out, g_out, ybg_out, ga_out, tail_ref):
    i = pl.program_id(0)
    tm = x_ref.shape[0]
    parts = [slice(j * DENSE_SPLIT, (j + 1) * DENSE_SPLIT) for j in range(tm // DENSE_SPLIT)]
    g1 = g1_ref[...]
    h = [_rms(x_ref[p, :], g1).astype(BF16) for p in parts]
    proj = [_dot_nt(hh, wa_ref[...]) for hh in h]
    pb = [_dot_nt(hh, wb_ref[...]) for hh in h]

    prev = [jnp.where(i % tiles_per_seq == 0, 0.0, tail_ref[7:8, :])]
    prev += [p[DENSE_SPLIT - 1:, :] for p in proj[:-1]]
    tail_ref[...] = proj[-1][DENSE_SPLIT - 8:, :]
    row = lax.broadcasted_iota(jnp.int32, proj[0].shape, 0)
    pm = []
    for p, pv in zip(proj, prev):
        shifted = jnp.where(row == 0, pv, pltpu.roll(p, 1, axis=0))
        pm.append(p + mu_ref[...] * (shifted - p))
    gates = [jax.nn.sigmoid(_dot_nt(hh, wg_ref[...]) + bg_ref[...]) for hh in h]

    tri = (lax.broadcasted_iota(jnp.int32, (GMLP_BLOCK, GMLP_BLOCK), 0)
           >= lax.broadcasted_iota(jnp.int32, (GMLP_BLOCK, GMLP_BLOCK), 1))
    ws = [jnp.where(tri, ws_ref[grp], 0.0).astype(BF16) for grp in range(B_GROUPS)]
    bs = bs_ref[...]
    us, vns = [], []
    for x in pb:
        z = 0.5 * x * (1.0 + lax.erf(x * (2.0 ** -0.5)))
        us.append(z[:, :B_WIDTH])
        v = z[:, B_WIDTH:]
        mean = jnp.mean(v, axis=-1, keepdims=True)
        vc = v - mean
        var = jnp.mean(vc * vc, axis=-1, keepdims=True)
        vns.append((vc * lax.rsqrt(var + LN_EPS) * lng_ref[...] + lnb_ref[...]).astype(BF16))

    lora = [x[:, 3 * A_WIDTH:] for x in pm]
    decay = [_dot(jnp.tanh(x[:, XW_OFF:XW_OFF + LANES]).astype(BF16), w2_ref[...]) for x in lora]
    rate = [_dot(x[:, XA_OFF:XA_OFF + LANES].astype(BF16), a2_ref[...]) for x in lora]
    gate = [_dot(jax.nn.sigmoid(x[:, XG_OFF:]).astype(BF16), g2_ref[...]) for x in lora]
    kks = [x[:, A_WIDTH:2 * A_WIDTH] * kk_ref[...] for x in pm]
    sq = [_dot((kk * kk).astype(BF16), ones_ref[...]) for kk in kks]

    svs = []
    for vn in vns:
        rows = []
        for blk in range(DENSE_SPLIT // GMLP_BLOCK):
            cols = [_dot(ws[grp], vn[blk * GMLP_BLOCK:(blk + 1) * GMLP_BLOCK,
                                     grp * B_GROUP_CH:(grp + 1) * B_GROUP_CH])
                    for grp in range(B_GROUPS)]
            rows.append(jnp.concatenate(cols, axis=1) + bs)
        svs.append(jnp.concatenate(rows, axis=0))
    yb = [_dot((u * sv).astype(BF16), wo_ref[...]) for u, sv in zip(us, svs)]

    for j, p in enumerate(parts):
        k = pm[j][:, A_WIDTH:2 * A_WIDTH]
        z = -(w0_ref[...] + decay[j])
        softplus = jnp.maximum(z, 0.0) + jnp.log(1.0 + jnp.exp(-jnp.abs(z)))
        w = -softplus - 0.5
        a_lr = jax.nn.sigmoid(a0_ref[...] + rate[j])
        kk = kks[j] / jnp.maximum(jnp.sqrt(sq[j]), 1e-12)
        r_out[p, :] = pm[j][:, 0:A_WIDTH]
        k_out[p, :] = k * (1.0 + (a_lr - 1.0) * ka_ref[...])
        v_out[p, :] = pm[j][:, 2 * A_WIDTH:3 * A_WIDTH]
        a_out[p, :] = -kk
        b_out[p, :] = kk * a_lr
        ld_out[p, :] = -jnp.exp(w)
        g_out[p, :] = gate[j]
        ga_out[p, :] = gates[j][:, :D_MODEL].astype(BF16)
        ybg_out[p, :] = (gates[j][:, D_MODEL:] * yb[j]).astype(BF16)


def _wkv_kernel(r_ref, k_ref, v_ref, a_ref, b_ref, ld_ref, g_ref, rk_ref, lng_ref, lnb_ref,
                ones_ref, o_ref, st_ref, ta_s, tl_s, arb_s, ark_s, rt_s, bkt_s, dcol_s, y_s):
    C = WKV_CHUNK
    bb, tb, _ = r_ref.shape
    n_chunks = tb // C
    n_pairs = A_WIDTH // HEAD_PAIR

    @pl.when(pl.program_id(1) == 0)
    def _():
        st_ref[...] = jnp.zeros_like(st_ref)

    row = lax.broadcasted_iota(jnp.int32, (C, HEAD_PAIR), 0)
    src = lax.broadcasted_iota(jnp.int32, (C, HEAD_PAIR), 1) & (C - 1)
    incl = src <= row
    strict = src < row
    eye_pair = jnp.where(src == row, 1.0, 0.0)
    bd_mask = ((lax.broadcasted_iota(jnp.int32, (HEAD_PAIR, HEAD_PAIR), 0) >= A_HEAD)
               == (lax.broadcasted_iota(jnp.int32, (HEAD_PAIR, HEAD_PAIR), 1) >= A_HEAD))
    tri_c = (lax.broadcasted_iota(jnp.int32, (C, C), 0)
             >= lax.broadcasted_iota(jnp.int32, (C, C), 1)).astype(BF16)
    ones_pair = ones_ref[...]
    pair_cols = [slice(p * HEAD_PAIR, (p + 1) * HEAD_PAIR) for p in range(n_pairs)]

    def bd(x):
        xb = x.astype(BF16)
        return jnp.where(bd_mask, jnp.concatenate([xb, xb], axis=0), jnp.zeros((), BF16))

    def prep(it, carry):
        where, lhs, rhs, ats = [], [], [], []
        for j in range(WKV_PREP_GROUP):
            flat = it * WKV_PREP_GROUP + j
            b = flat // n_chunks
            c = flat % n_chunks
            sl = pl.ds(pl.multiple_of(c * C, C), C)
            ld = ld_ref[b, sl, :]
            hi, mid, lo = _split3(ld)
            cl = _dot(tri_c, hi) + _dot(tri_c, mid) + _dot(tri_c, lo)
            cl_end = cl[C - 1:C, :]
            k = k_ref[b, sl, :]
            bv = b_ref[b, sl, :]
            d_inv = jnp.exp(-cl)
            d_tail = jnp.exp(cl_end - cl)
            rt = (r_ref[b, sl, :] * jnp.exp(cl)).astype(BF16)
            rt_s[b, sl, :] = rt
            kt = k * d_inv
            bt = bv * d_inv
            at = a_ref[b, sl, :] * jnp.exp(cl - ld)
            kd = k * d_tail
            bdk = bv * d_tail
            d_end = jnp.exp(cl_end)
            for p, cs in enumerate(pair_cols):
                bkt_s[b, c, p] = jnp.concatenate([bdk[:, cs], kd[:, cs]], axis=0).T.astype(BF16)
                dcol_s[b, c, p] = jnp.broadcast_to(d_end[:, cs], (HEAD_PAIR, HEAD_PAIR)).T
                where.append((b, sl, cs))
                ats.append(at[:, cs])
                lhs.append(jnp.concatenate([rt[:, cs], at[:, cs].astype(BF16)], axis=0))
                rhs.append(jnp.concatenate([bd(kt[:, cs]), bd(bt[:, cs])], axis=0))
        n = len(where)
        amat = [_dot_nt(lhs[i], rhs[i]) for i in range(n)]
        l_ak, l_ab = [], []
        for i, (b, sl, cs) in enumerate(where):
            ark_s[b, sl, cs] = jnp.where(incl, amat[i][:C, :HEAD_PAIR], 0.0).astype(BF16)
            arb_s[b, sl, cs] = jnp.where(incl, amat[i][:C, HEAD_PAIR:], 0.0).astype(BF16)
            l_ak.append(jnp.where(strict, amat[i][C:, :HEAD_PAIR], 0.0))
            l_ab.append(jnp.where(strict, amat[i][C:, HEAD_PAIR:], 0.0))

        t_mat = [eye_pair + l for l in l_ab]
        q = [_dot(l.astype(BF16), bd(l)) for l in l_ab]
        n_sq = 1
        while 2 * n_sq < C // 2:
            both = [_dot(q[i].astype(BF16), jnp.concatenate([bd(q[i]), bd(t_mat[i])], axis=1))
                    for i in range(n)]
            q = [x[:, :HEAD_PAIR] for x in both]
            t_mat = [t_mat[i] + both[i][:, HEAD_PAIR:] for i in range(n)]
            n_sq *= 2
        t_mat = [t_mat[i] + _dot(q[i].astype(BF16), bd(t_mat[i])) for i in range(n)]
        tal = [_dot(t_mat[i].astype(BF16), jnp.concatenate([bd(ats[i]), bd(l_ak[i])], axis=1))
               for i in range(n)]
        for i, (b, sl, cs) in enumerate(where):
            ta_s[b, sl, cs] = tal[i][:, :HEAD_PAIR].astype(BF16)
            tl_s[b, sl, cs] = tal[i][:, HEAD_PAIR:].astype(BF16)
        return carry

    lax.fori_loop(0, bb * n_chunks // WKV_PREP_GROUP, prep, 0)

    def step(c, carry):
        sl = pl.ds(pl.multiple_of(c * C, C), C)
        chains = [(b, p) for b in range(bb) for p in range(n_pairs)]
        v = [v_ref[b, sl, :] for b in range(bb)]
        st = [st_ref[b, p] for b, p in chains]
        st_b = [x.astype(BF16) for x in st]
        bd_v = [bd(v[b][:, pair_cols[p]]) for b, p in chains]
        on_st = [_dot(jnp.concatenate([ta_s[b, sl, pair_cols[p]], rt_s[b, sl, pair_cols[p]]],
                                      axis=0), st_b[i]) for i, (b, p) in enumerate(chains)]
        on_v = [_dot(jnp.concatenate([tl_s[b, sl, pair_cols[p]], ark_s[b, sl, pair_cols[p]]],
                                     axis=0), bd_v[i]) for i, (b, p) in enumerate(chains)]
        u = [on_st[i][:C] + on_v[i][:C] for i in range(len(chains))]
        for i, (b, p) in enumerate(chains):
            uv = jnp.concatenate([u[i], v[b][:, pair_cols[p]]], axis=0).astype(BF16)
            st_ref[b, p] = (dcol_s[b, c, p] * st[i]
                            + jnp.where(bd_mask, _dot(bkt_s[b, c, p], uv), 0.0))
        ys = [on_st[i][C:] + on_v[i][C:] + _dot(arb_s[b, sl, pair_cols[p]], bd(u[i]))
              for i, (b, p) in enumerate(chains)]
        for b in range(bb):
            y_s[b, sl, :] = jnp.concatenate(ys[b * n_pairs:(b + 1) * n_pairs], axis=1)
        return carry

    lax.fori_loop(0, n_chunks, step, 0)

    def finish(it, carry):
        items = [(b, pl.ds(pl.multiple_of((it * WKV_NORM_GROUP + j) * C, C), C))
                 for j in range(WKV_NORM_GROUP) for b in range(bb)]
        y = [y_s[b, sl, :] for b, sl in items]
        rkr = [r_ref[b, sl, :] * k_ref[b, sl, :] * rk_ref[...] for b, sl in items]
        sums = [_pair_head_sums([y[i], rkr[i]], ones_pair) for i in range(len(items))]
        yc = [y[i] - sums[i][0] * (1.0 / A_HEAD) for i in range(len(items))]
        var = [_pair_head_sums([x * x], ones_pair)[0] * (1.0 / A_HEAD) for x in yc]
        for i, (b, sl) in enumerate(items):
            yn = yc[i] * lax.rsqrt(var[i] + LNX_EPS) * lng_ref[...] + lnb_ref[...]
            out = (yn + sums[i][1] * v_ref[b, sl, :]) * g_ref[b, sl, :]
            o_ref[b, sl, :] = out.astype(o_ref.dtype)
        return carry

    lax.fori_loop(0, n_chunks // WKV_NORM_GROUP, finish, 0)


def _post_kernel(x_ref, ya_ref, ga_ref, ybg_ref, woa_ref, wout_ref, g2_ref, wrt_ref, brt_ref,
                 x2_out, hl_out, route_out, cnt_out):
    n_sub = x_ref.shape[0] // MOE_SUB
    groups = [slice(g * MOE_SUB, (g + 1) * MOE_SUB) for g in range(n_sub)]
    y_a = [_dot(ya_ref[p, :].astype(BF16), woa_ref[...]) for p in groups]
    x2 = [x_ref[p, :] + _dot((ga_ref[p, :] * y + ybg_ref[p, :]).astype(BF16), wout_ref[...])
          for p, y in zip(groups, y_a)]
    h2 = [_rms(x, g2_ref[...]) for x in x2]
    split = [_split2(h) for h in h2]
    on_hi = [_dot_nt(wrt_ref[...], hi) for hi, _ in split]
    on_lo = [_dot_nt(wrt_ref[0:LANES, :], lo) for _, lo in split]
    row = lax.broadcasted_iota(jnp.int32, (LANES, MOE_SUB), 0)
    neg = jnp.float32(-jnp.inf)
    big = jnp.int32(LANES)
    is_grp = (row >= N_EXPERTS) & (row < N_EXPERTS + N_GROUPS)
    top = lambda v: jnp.max(v, axis=0, keepdims=True)
    first = lambda hit: jnp.min(jnp.where(hit, row, big), axis=0, keepdims=True)
    picks, onehots, weights = [], [], []
    for g, p in enumerate(groups):
        x2_out[p, :] = x2[g].astype(x2_out.dtype)
        logits = on_hi[g][:LANES, :] + on_hi[g][LANES:, :] + on_lo[g] + brt_ref[...]
        gl = jnp.where(is_grp, logits, neg)
        gmax = top(gl)
        g_p = 1.0 / jnp.sum(jnp.exp(gl - gmax), axis=0, keepdims=True)
        lo_row = (first(gl == gmax) - N_EXPERTS) * EXPERTS_PER_GROUP
        el = jnp.where((row >= lo_row) & (row < lo_row + EXPERTS_PER_GROUP), logits, neg)
        e1 = top(el)
        i1 = first(el == e1)
        el2 = jnp.where(row == i1, neg, el)
        e2 = top(el2)
        i2 = first(el2 == e2)
        t = jnp.exp(e2 - e1)
        weights.append((g_p / (1.0 + t), g_p * t / (1.0 + t)))
        picks.append((i1, i2))
        onehots.append(jnp.where((row == i1) | (row == i2), 1.0, 0.0))

    earlier = (lax.broadcasted_iota(jnp.int32, (MOE_SUB, MOE_SUB), 0)
               <= lax.broadcasted_iota(jnp.int32, (MOE_SUB, MOE_SUB), 1)).astype(BF16)
    incl = [_dot(oh.astype(BF16), earlier) for oh in onehots]
    chunks = [jnp.ceil(x[:, MOE_SUB - 1:] * (1.0 / RUN_ALIGN)) for x in incl]
    below = (lax.broadcasted_iota(jnp.int32, (LANES, LANES), 1)
             < lax.broadcasted_iota(jnp.int32, (LANES, LANES), 0)).astype(BF16)
    wide = [jnp.broadcast_to(c, (LANES, LANES)) for c in chunks]
    run_start = [_dot(below, w.astype(BF16))[:, 0:1] * RUN_ALIGN for w in wide]
    row8 = lax.broadcasted_iota(jnp.int32, (8, LANES), 0)
    counts = jnp.zeros((8, LANES), F32)
    slot_ids = lax.broadcasted_iota(jnp.int32, (RUN_SLOTS, MOE_SUB), 0)
    for g, p in enumerate(groups):
        counts = jnp.where(row8 == g, wide[g].T[0:8, :], counts)
        slot_of = run_start[g] + incl[g] - onehots[g]
        i1, i2 = picks[g]
        slot1 = jnp.sum(jnp.where(row == i1, slot_of, 0.0), axis=0, keepdims=True)
        slot2 = jnp.sum(jnp.where(row == i2, slot_of, 0.0), axis=0, keepdims=True)
        w1, w2 = weights[g]
        packed = jnp.where(row == 0, slot1, jnp.where(row == 1, slot2, jnp.where(
            row == 2, w1, jnp.where(row == 3, w2, 0.0))))
        route_out[p, :] = packed.T
        pick = jnp.where((slot_ids == slot1.astype(jnp.int32))
                         | (slot_ids == slot2.astype(jnp.int32)), 1.0, 0.0).astype(BF16)
        hl_out[g * RUN_SLOTS:(g + 1) * RUN_SLOTS, :] = _dot(pick, h2[g].astype(BF16)).astype(BF16)
    cnt_out[...] = counts


def _expert_kernel(chunk_s, first_s, hl_in_ref, wg_ref, wu_ref, wd_ref, hl_ref, xbuf, wgu_s, wd_s,
                   sem, drain_s):
    del hl_in_ref
    e = pl.program_id(0)
    pass_chunks = EXPERT_ROWS // RUN_ALIGN
    mine = e % 2
    other = 1 - mine
    gather_sem = lambda buf: sem.at[buf]
    scatter_sem = lambda buf: sem.at[2 + buf]

    def first_pass(ex):
        lo = first_s[ex]
        return lo, jnp.minimum(lo + pass_chunks, first_s[ex + 1])

    def for_chunks(buf, c_lo, c_hi, fn):
        def one(c):
            fn(hl_ref.at[pl.ds(pl.multiple_of(chunk_s[c], RUN_ALIGN), RUN_ALIGN)],
               xbuf.at[buf, pl.ds(pl.multiple_of((c - c_lo) * RUN_ALIGN, RUN_ALIGN), RUN_ALIGN)])

        def block(i, carry):
            for u in range(CHUNK_UNROLL):
                one(c_lo + i * CHUNK_UNROLL + u)
            return carry

        def tail(c, carry):
            one(c)
            return carry

        n_blocks = (c_hi - c_lo) // CHUNK_UNROLL
        lax.fori_loop(0, n_blocks, block, 0)
        lax.fori_loop(c_lo + n_blocks * CHUNK_UNROLL, c_hi, tail, 0)

    def gather(buf, c_lo, c_hi):
        for_chunks(buf, c_lo, c_hi, lambda hbm, vm: pltpu.make_async_copy(
            hbm, vm, gather_sem(buf)).start())

    def scatter(buf, c_lo, c_hi):
        for_chunks(buf, c_lo, c_hi, lambda hbm, vm: pltpu.make_async_copy(
            vm, hbm, scatter_sem(buf)).start())

    def wait_chunks(which_sem, n):
        def wait_rows(rows):
            def body(c, carry):
                pltpu.make_async_copy(hl_ref.at[pl.ds(0, rows)], xbuf.at[0, pl.ds(0, rows)],
                                      which_sem).wait()
                return carry
            return body

        lax.fori_loop(0, n // CHUNK_UNROLL, wait_rows(CHUNK_UNROLL * RUN_ALIGN), 0)
        lax.fori_loop(0, n % CHUNK_UNROLL, wait_rows(RUN_ALIGN), 0)

    def evaluate(buf, n_chunks):
        def tile(k, carry):
            sl = pl.ds(pl.multiple_of(k * EXPERT_TILE, EXPERT_TILE), EXPERT_TILE)
            gu = _dot(xbuf[buf, sl, :], wgu_s[...])
            gate = gu[:, :D_EXPERT]
            act = gate * jax.nn.sigmoid(gate) * gu[:, D_EXPERT:]
            xbuf[buf, sl, :] = _dot(act.astype(BF16), wd_s[...]).astype(BF16)
            return carry

        lax.fori_loop(0, (n_chunks * RUN_ALIGN + EXPERT_TILE - 1) // EXPERT_TILE, tile, 0)

    @pl.when(e == 0)
    def _():
        xbuf[...] = jnp.zeros_like(xbuf)
        gather(0, *first_pass(0))
        drain_s[0] = 0

    wgu_s[:, :D_EXPERT] = wg_ref[0].astype(BF16)
    wgu_s[:, D_EXPERT:] = wu_ref[0].astype(BF16)
    wd_s[...] = wd_ref[0].astype(BF16)

    c_lo, c_hi = first_pass(e)
    wait_chunks(gather_sem(mine), c_hi - c_lo)
    wait_chunks(scatter_sem(other), drain_s[0])

    @pl.when(e + 1 < pl.num_programs(0))
    def _():
        gather(other, *first_pass(e + 1))

    evaluate(mine, c_hi - c_lo)
    scatter(mine, c_lo, c_hi)

    def later_pass(state):
        lo, pending = state
        hi = jnp.minimum(lo + pass_chunks, first_s[e + 1])
        wait_chunks(scatter_sem(mine), pending)
        gather(mine, lo, hi)
        wait_chunks(gather_sem(mine), hi - lo)
        evaluate(mine, hi - lo)
        scatter(mine, lo, hi)
        return hi, hi - lo

    _, pending = lax.while_loop(lambda st: st[0] < first_s[e + 1], later_pass,
                                (c_hi, c_hi - c_lo))
    drain_s[0] = pending

    @pl.when(e + 1 == pl.num_programs(0))
    def _():
        wait_chunks(scatter_sem(mine), pending)


def _combine_kernel(x2_ref, route_ref, gf_ref, yl_ref, o_ref):
    slots = lax.broadcasted_iota(jnp.int32, (MOE_SUB, RUN_SLOTS), 1).astype(F32)
    for g in range(x2_ref.shape[0] // MOE_SUB):
        p = slice(g * MOE_SUB, (g + 1) * MOE_SUB)
        route = route_ref[p, :]
        pick = (jnp.where(slots == route[:, 0:1], route[:, 2:3], 0.0)
                + jnp.where(slots == route[:, 1:2], route[:, 3:4], 0.0))
        y = _dot(pick.astype(BF16), yl_ref[g * RUN_SLOTS:(g + 1) * RUN_SLOTS, :])
        o_ref[p, :] = _rms(x2_ref[p, :] + y, gf_ref[...])


def _full(shape):
    return pl.BlockSpec(shape, lambda *_: (0,) * len(shape))


def _params(sem, vmem_limit=VMEM_LIMIT):
    return pltpu.CompilerParams(dimension_semantics=sem, vmem_limit_bytes=vmem_limit)


def _mixer_a_layout(t):
    s_rkv = 3 * A_WIDTH
    s_w = s_rkv + D_DECAY_LORA
    s_a = s_w + D_AAA_LORA
    gap = lambda n: jnp.zeros((t.shape[0], n), t.dtype)
    return jnp.concatenate(
        [t[:, :s_w], gap(XA_OFF - D_DECAY_LORA), t[:, s_w:s_a], gap(XG_OFF - XA_OFF - D_AAA_LORA),
         t[:, s_a:], gap(LORA_PAD - XG_OFF - D_GATE_LORA)], axis=1)


def _place(cols, parts):
    out = jnp.zeros((parts[0][1].shape[0], cols), parts[0][1].dtype)
    for off, arr in parts:
        out = lax.dynamic_update_slice(out, arr, (0, off))
    return out


def kernel(x, norm1_g, w_in, b_gate, tmix_mu, w0, w2, a0, a2, g2, k_k, k_a, r_k, lnx_g, lnx_b,
           w_oA, lnv_g, lnv_b, w_s, b_s, w_oB, w_out, norm2_g, w_rg, b_rg, w_re, b_re,
           w_e_gate, w_e_up, w_e_down, final_g):
    bsz, seq, d = x.shape
    n_tok = bsz * seq
    depth = norm1_g.shape[0]
    assert depth == 1, "the moe kernel fuses the final norm, so it must be the last layer"
    assert bsz % WKV_SEQS == 0 and seq % WKV_TOKENS == 0
    xf = x.reshape(n_tok, d)

    s_rkv = 3 * A_WIDTH
    s_w = s_rkv + D_DECAY_LORA
    s_a = s_w + D_AAA_LORA
    a_cols = s_a + D_GATE_LORA
    b_cols = 2 * B_WIDTH

    ones_bd = (jnp.arange(A_WIDTH)[:, None] // A_HEAD
               == jnp.arange(A_WIDTH)[None, :] // A_HEAD).astype(BF16)

    tm_a = INPROJ_TILE
    tm_p = POST_TILE
    tm_c = COMBINE_TILE
    assert tm_p % MOE_SUB == 0 and tm_c % MOE_SUB == 0 and tm_a % DENSE_SPLIT == 0
    assert seq % tm_a == 0, "an inproj tile must not straddle two sequences (token shift)"

    for l in range(depth):
        wl = w_in[l]
        n_w = 4
        col_blk = lambda rows: pl.BlockSpec((rows, d // n_w), lambda i: (0, i))
        row_blk = lambda rows: pl.BlockSpec((rows // n_w, d), lambda i: (i, 0))
        w_a, w_b, w_g, w_oa, w_ob, w_o = pl.pallas_call(
            _w_in_kernel,
            grid=(n_w,),
            in_specs=[col_blk(wl.shape[1]), row_blk(A_WIDTH), row_blk(B_WIDTH), row_blk(d)],
            out_specs=[col_blk(A_PROJ), col_blk(b_cols), col_blk(2 * d),
                       row_blk(A_WIDTH), row_blk(B_WIDTH), row_blk(d)],
            out_shape=[jax.ShapeDtypeStruct((A_PROJ, d), BF16),
                       jax.ShapeDtypeStruct((b_cols, d), BF16),
                       jax.ShapeDtypeStruct((2 * d, d), BF16),
                       jax.ShapeDtypeStruct((A_WIDTH, d), BF16),
                       jax.ShapeDtypeStruct((B_WIDTH, d), BF16),
                       jax.ShapeDtypeStruct((d, d), BF16)],
            compiler_params=_params(("parallel",)),
            name="w_in_layout",
        )(wl.T, w_oA[l], w_oB[l], w_out[l])
        mu_a = _mixer_a_layout(tmix_mu[l][None, :])
        w2p = jnp.pad(w2[l], ((0, LANES - D_DECAY_LORA), (0, 0))).astype(BF16)
        a2p = jnp.pad(a2[l], ((0, LANES - D_AAA_LORA), (0, 0))).astype(BF16)
        g2p = jnp.pad(g2[l], ((0, LORA_PAD - XG_OFF - D_GATE_LORA), (0, 0))).astype(BF16)
        g1 = norm1_g[l][None, :]

        row512 = lambda i: (i, 0)
        tok_a = pl.BlockSpec((tm_a, A_WIDTH), row512)
        tok_d = pl.BlockSpec((tm_a, d), row512)
        vec_a = _full((1, A_WIDTH))
        bs_full = jnp.repeat(b_s[l].T, B_GROUP_CH, axis=1)
        outs = pl.pallas_call(
            functools.partial(_inproj_kernel, seq // tm_a),
            grid=(n_tok // tm_a,),
            in_specs=[tok_d,
                      _full((1, d)), _full((A_PROJ, d)), _full((1, A_PROJ)), vec_a,
                      _full((LANES, A_WIDTH)), vec_a, _full((LANES, A_WIDTH)),
                      _full((LORA_PAD - XG_OFF, A_WIDTH)), vec_a, vec_a,
                      _full((A_WIDTH, A_WIDTH)),
                      _full((b_cols, d)), _full((2 * d, d)), _full((1, 2 * d)),
                      _full((1, B_WIDTH)), _full((1, B_WIDTH)),
                      _full((B_GROUPS, GMLP_BLOCK, GMLP_BLOCK)), _full((GMLP_BLOCK, B_WIDTH)),
                      _full((B_WIDTH, d))],
            out_specs=[tok_a] * 7 + [tok_d, tok_d],
            out_shape=[jax.ShapeDtypeStruct((n_tok, A_WIDTH), F32)] * 7
            + [jax.ShapeDtypeStruct((n_tok, d), BF16)] * 2,
            scratch_shapes=[pltpu.VMEM((8, A_PROJ), F32)],
            compiler_params=_params(("arbitrary",)),
            name="inproj",
        )(xf, g1, w_a, mu_a, w0[l][None, :], w2p, a0[l][None, :], a2p, g2p,
          k_k[l][None, :], k_a[l][None, :], ones_bd, w_b, w_g, b_gate[l][None, :],
          lnv_g[l][None, :], lnv_b[l][None, :], w_s[l], bs_full, w_ob)
        r_, k_, v_, a_, b_, ld_, gg_, ybg, ga = outs

        n_pairs = A_WIDTH // HEAD_PAIR
        n_chunks = WKV_TOKENS // WKV_CHUNK
        tok_w = pl.BlockSpec((WKV_SEQS, WKV_TOKENS, A_WIDTH), lambda bi, ti: (bi, ti, 0))
        vec_w = _full((1, A_WIDTH))
        seq_major = lambda t: t.reshape(bsz, seq, A_WIDTH)
        tok_scratch = pltpu.VMEM((WKV_SEQS, WKV_TOKENS, A_WIDTH), BF16)
        ya_in = pl.pallas_call(
            _wkv_kernel,
            grid=(bsz // WKV_SEQS, seq // WKV_TOKENS),
            in_specs=[tok_w] * 7 + [vec_w, vec_w, vec_w, _full((HEAD_PAIR, HEAD_PAIR))],
            out_specs=tok_w,
            out_shape=jax.ShapeDtypeStruct((bsz, seq, A_WIDTH), BF16),
            scratch_shapes=[pltpu.VMEM((WKV_SEQS, n_pairs, HEAD_PAIR, HEAD_PAIR), F32)]
            + [tok_scratch] * 5
            + [pltpu.VMEM((WKV_SEQS, n_chunks, n_pairs, HEAD_PAIR, HEAD_PAIR), BF16),
               pltpu.VMEM((WKV_SEQS, n_chunks, n_pairs, HEAD_PAIR, HEAD_PAIR), F32),
               pltpu.VMEM((WKV_SEQS, WKV_TOKENS, A_WIDTH), F32)],
            compiler_params=_params(("parallel", "arbitrary")),
            name="wkv",
        )(*[seq_major(t) for t in (r_, k_, v_, a_, b_, ld_, gg_)], r_k[l].reshape(1, A_WIDTH),
          lnx_g[l][None, :], lnx_b[l][None, :],
          ones_bd[:HEAD_PAIR, :HEAD_PAIR]).reshape(n_tok, A_WIDTH)

        w_r = _place(LANES, [(0, jnp.transpose(w_re[l], (1, 0, 2)).reshape(d, N_EXPERTS)),
                             (N_EXPERTS, w_rg[l])])
        b_r = _place(LANES, [(0, b_re[l].reshape(1, N_EXPERTS)), (N_EXPERTS, b_rg[l][None, :])])
        wr_hi = w_r.astype(BF16)
        wr_lo = (w_r - wr_hi.astype(F32)).astype(BF16)
        wr_t = jnp.concatenate([wr_hi.T, wr_lo.T], axis=0)
        tok_p = pl.BlockSpec((tm_p, d), row512)
        lane_p = pl.BlockSpec((tm_p, LANES), row512)
        n_sub = n_tok // MOE_SUB
        sorted_rows = tm_p // MOE_SUB * RUN_SLOTS
        x2, h_sorted, route, cnt = pl.pallas_call(
            _post_kernel,
            grid=(n_tok // tm_p,),
            in_specs=[tok_p, pl.BlockSpec((tm_p, A_WIDTH), row512), tok_p, tok_p,
                      _full((A_WIDTH, d)), _full((d, d)), _full((1, d)), _full((2 * LANES, d)),
                      _full((LANES, 1))],
            out_specs=[tok_p, pl.BlockSpec((sorted_rows, d), row512), lane_p,
                       pl.BlockSpec((8, LANES), row512)],
            out_shape=[jax.ShapeDtypeStruct((n_tok, d), F32),
                       jax.ShapeDtypeStruct((n_sub * RUN_SLOTS, d), BF16),
                       jax.ShapeDtypeStruct((n_tok, LANES), F32),
                       jax.ShapeDtypeStruct((n_tok // tm_p * 8, LANES), F32)],
            compiler_params=_params(("parallel",), POST_VMEM_LIMIT),
            name="post",
        )(xf, ya_in, ga, ybg, w_oa, w_o,
          norm2_g[l][None, :], wr_t, b_r.T)

        run_chunks = cnt.reshape(n_tok // tm_p, 8, LANES)[:, :tm_p // MOE_SUB, :N_EXPERTS]
        run_chunks = run_chunks.reshape(n_sub, N_EXPERTS).astype(jnp.int32)
        run_start = (jnp.cumsum(run_chunks, axis=1) - run_chunks) * RUN_ALIGN
        group_end = jnp.cumsum(run_chunks, axis=0)
        first_chunk = jnp.concatenate([jnp.zeros((1,), jnp.int32), jnp.cumsum(group_end[-1])])
        pos = jnp.arange(n_sub * RUN_SLOTS // RUN_ALIGN, dtype=jnp.int32)
        e_of = jnp.minimum(jnp.sum(pos[:, None] >= first_chunk[None, 1:], axis=1), N_EXPERTS - 1)
        of_e = (e_of[:, None] == jnp.arange(N_EXPERTS, dtype=jnp.int32)[None, :])
        column = lambda t: jnp.dot(of_e.astype(F32), t.T.astype(F32),
                                   precision=lax.Precision.HIGHEST).astype(jnp.int32)
        in_expert = pos - jnp.sum(jnp.where(of_e, first_chunk[None, :-1], 0), axis=1)
        ends = column(group_end)
        g_of = jnp.minimum(jnp.sum(in_expert[:, None] >= ends, axis=1), n_sub - 1)
        of_g = g_of[:, None] == jnp.arange(n_sub, dtype=jnp.int32)[None, :]
        at_g = lambda t: jnp.sum(jnp.where(of_g, t, 0), axis=1)
        in_run = in_expert - at_g(ends - column(run_chunks))
        chunk_rows = g_of * RUN_SLOTS + at_g(column(run_start)) + in_run * RUN_ALIGN

        per_expert = lambda e, *_: (e, 0, 0)
        y_sorted = pl.pallas_call(
            _expert_kernel,
            grid_spec=pltpu.PrefetchScalarGridSpec(
                num_scalar_prefetch=2, grid=(N_EXPERTS,),
                in_specs=[pl.BlockSpec(memory_space=pl.ANY),
                          pl.BlockSpec((1, d, D_EXPERT), per_expert),
                          pl.BlockSpec((1, d, D_EXPERT), per_expert),
                          pl.BlockSpec((1, D_EXPERT, d), per_expert)],
                out_specs=pl.BlockSpec(memory_space=pl.ANY),
                scratch_shapes=[pltpu.VMEM((2, EXPERT_ROWS, d), BF16),
                                pltpu.VMEM((d, 2 * D_EXPERT), BF16),
                                pltpu.VMEM((D_EXPERT, d), BF16),
                                pltpu.SemaphoreType.DMA((4,)),
                                pltpu.SMEM((1,), jnp.int32)]),
            out_shape=jax.ShapeDtypeStruct((n_sub * RUN_SLOTS, d), BF16),
            input_output_aliases={2: 0},
            compiler_params=_params(("arbitrary",)),
            name="moe_experts",
        )(chunk_rows, first_chunk, h_sorted, w_e_gate[l], w_e_up[l], w_e_down[l])

        xf = pl.pallas_call(
            _combine_kernel,
            grid=(n_tok // tm_c,),
            in_specs=[pl.BlockSpec((tm_c, d), row512), pl.BlockSpec((tm_c, LANES), row512),
                      _full((1, d)), pl.BlockSpec((tm_c // MOE_SUB * RUN_SLOTS, d), row512)],
            out_specs=pl.BlockSpec((tm_c, d), row512),
            out_shape=jax.ShapeDtypeStruct((n_tok, d), F32),
            compiler_params=_params(("parallel",)),
            name="moe_combine",
        )(x2, route, final_g[None, :], y_sorted)

    return xf.reshape(bsz, seq, d)
```

```python
import functools

import jax
import jax.numpy as jnp
from jax import lax
from jax.experimental import pallas as pl
from jax.experimental.pallas import tpu as pltpu

F32 = jnp.float32
BF16 = jnp.bfloat16

D_MODEL = 1024
A_WIDTH = 512
A_HEAD = 64
D_DECAY_LORA = 64
D_AAA_LORA = 64
D_GATE_LORA = 160
B_WIDTH = 512
B_GROUPS = 4
B_GROUP_CH = 128
GMLP_BLOCK = 128
N_GROUPS = 4
EXPERTS_PER_GROUP = 8
N_EXPERTS = 32
D_EXPERT = 256
NORM_EPS = 1e-6
LN_EPS = 1e-5
LNX_EPS = 64e-5

LANES = 128
LORA_PAD = 512
XW_OFF, XA_OFF, XG_OFF = 0, 128, 256
A_PROJ = 3 * A_WIDTH + LORA_PAD
WKV_CHUNK = 64
HEAD_PAIR = 2 * A_HEAD
WKV_SEQS = 8
WKV_TOKENS = 64
WKV_PREP_GROUP = 8
WKV_NORM_GROUP = 1
MOE_SUB = 256
RUN_ALIGN = 16
RUN_SLOTS = 2 * MOE_SUB + N_EXPERTS * RUN_ALIGN
EXPERT_ROWS = 2048
EXPERT_TILE = 512
INPROJ_TILE = 512
POST_TILE = 1024
COMBINE_TILE = 1024
DENSE_SPLIT = 256
CHUNK_UNROLL = 8
VMEM_LIMIT = 48 * 1024 * 1024
POST_VMEM_LIMIT = 56 * 1024 * 1024


def _rms(x, g):
    return x * lax.rsqrt(jnp.mean(x * x, axis=-1, keepdims=True) + NORM_EPS) * g


def _dot(a, b):
    return jnp.dot(a, b, preferred_element_type=F32)


def _dot_nt(a, b):
    return lax.dot_general(a, b, (((1,), (1,)), ((), ())), preferred_element_type=F32)


def _split2(x):
    hi = x.astype(BF16)
    lo = (x - hi.astype(F32)).astype(BF16)
    return hi, lo


def _split3(x):
    hi = x.astype(BF16)
    r1 = x - hi.astype(F32)
    mid = r1.astype(BF16)
    lo = (r1 - mid.astype(F32)).astype(BF16)
    return hi, mid, lo


def _pair_head_sums(xs, ones_pair):
    n_tiles = A_WIDTH // HEAD_PAIR
    rows = xs[0].shape[0]
    tiles = [x[:, t * HEAD_PAIR:(t + 1) * HEAD_PAIR].astype(BF16) for x in xs for t in range(n_tiles)]
    sums = _dot(jnp.concatenate(tiles, axis=0), ones_pair)
    return [jnp.concatenate([sums[(i * n_tiles + t) * rows:(i * n_tiles + t + 1) * rows]
                             for t in range(n_tiles)], axis=1) for i in range(len(xs))]


def _w_in_kernel(w_ref, woa_ref, wob_ref, wout_ref, wa_out, wb_out, wg_out, woa_out, wob_out,
                 wout_out):
    woa_out[...] = woa_ref[...].astype(BF16)
    wob_out[...] = wob_ref[...].astype(BF16)
    wout_out[...] = wout_ref[...].astype(BF16)
    s_rkv = 3 * A_WIDTH
    s_w = s_rkv + D_DECAY_LORA
    s_a = s_w + D_AAA_LORA
    a_cols = s_a + D_GATE_LORA
    b_end = a_cols + 2 * B_WIDTH
    wa_out[...] = jnp.zeros_like(wa_out)
    wa_out[:s_w, :] = w_ref[:s_w, :].astype(BF16)
    wa_out[s_rkv + XA_OFF:s_rkv + XA_OFF + D_AAA_LORA, :] = w_ref[s_w:s_a, :].astype(BF16)
    wa_out[s_rkv + XG_OFF:s_rkv + XG_OFF + D_GATE_LORA, :] = w_ref[s_a:a_cols, :].astype(BF16)
    wb_out[...] = w_ref[a_cols:b_end, :].astype(BF16)
    wg_out[...] = w_ref[b_end:, :].astype(BF16)


def _inproj_kernel(tiles_per_seq, x_ref, g1_ref, wa_ref, mu_ref, w0_ref, w2_ref, a0_ref, a2_ref,
                   g2_ref, kk_ref, ka_ref, ones_ref, wb_ref, wg_ref, bg_ref, lng_ref, lnb_ref,
                   ws_ref, bs_ref, wo_ref,
                   r_out, k_out, v_out, a_out, b_out, ld_out, g_out, ybg_out, ga_out, tail_ref):
    i = pl.program_id(0)
    tm = x_ref.shape[0]
    parts = [slice(j * DENSE_SPLIT, (j + 1) * DENSE_SPLIT) for j in range(tm // DENSE_SPLIT)]
    g1 = g1_ref[...]
    h = [_rms(x_ref[p, :], g1).astype(BF16) for p in parts]
    proj = [_dot_nt(hh, wa_ref[...]) for hh in h]
    pb = [_dot_nt(hh, wb_ref[...]) for hh in h]

    prev = [jnp.where(i % tiles_per_seq == 0, 0.0, tail_ref[7:8, :])]
    prev += [p[DENSE_SPLIT - 1:, :] for p in proj[:-1]]
    tail_ref[...] = proj[-1][DENSE_SPLIT - 8:, :]
    row = lax.broadcasted_iota(jnp.int32, proj[0].shape, 0)
    pm = []
    for p, pv in zip(proj, prev):
        shifted = jnp.where(row == 0, pv, pltpu.roll(p, 1, axis=0))
        pm.append(p + mu_ref[...] * (shifted - p))
    gates = [jax.nn.sigmoid(_dot_nt(hh, wg_ref[...]) + bg_ref[...]) for hh in h]

    tri = (lax.broadcasted_iota(jnp.int32, (GMLP_BLOCK, GMLP_BLOCK), 0)
           >= lax.broadcasted_iota(jnp.int32, (GMLP_BLOCK, GMLP_BLOCK), 1))
    ws = [jnp.where(tri, ws_ref[grp], 0.0).astype(BF16) for grp in range(B_GROUPS)]
    bs = bs_ref[...]
    us, vns = [], []
    for x in pb:
        z = 0.5 * x * (1.0 + lax.erf(x * (2.0 ** -0.5)))
        us.append(z[:, :B_WIDTH])
        v = z[:, B_WIDTH:]
        mean = jnp.mean(v, axis=-1, keepdims=True)
        vc = v - mean
        var = jnp.mean(vc * vc, axis=-1, keepdims=True)
        vns.append((vc * lax.rsqrt(var + LN_EPS) * lng_ref[...] + lnb_ref[...]).astype(BF16))

    lora = [x[:, 3 * A_WIDTH:] for x in pm]
    decay = [_dot(jnp.tanh(x[:, XW_OFF:XW_OFF + LANES]).astype(BF16), w2_ref[...]) for x in lora]
    rate = [_dot(x[:, XA_OFF:XA_OFF + LANES].astype(BF16), a2_ref[...]) for x in lora]
    gate = [_dot(jax.nn.sigmoid(x[:, XG_OFF:]).astype(BF16), g2_ref[...]) for x in lora]
    kks = [x[:, A_WIDTH:2 * A_WIDTH] * kk_ref[...] for x in pm]
    sq = [_dot((kk * kk).astype(BF16), ones_ref[...]) for kk in kks]

    svs = []
    for vn in vns:
        rows = []
        for blk in range(DENSE_SPLIT // GMLP_BLOCK):
            cols = [_dot(ws[grp], vn[blk * GMLP_BLOCK:(blk + 1) * GMLP_BLOCK,
                                     grp * B_GROUP_CH:(grp + 1) * B_GROUP_CH])
                    for grp in range(B_GROUPS)]
            rows.append(jnp.concatenate(cols, axis=1) + bs)
        svs.append(jnp.concatenate(rows, axis=0))
    yb = [_dot((u * sv).astype(BF16), wo_ref[...]) for u, sv in zip(us, svs)]

    for j, p in enumerate(parts):
        k = pm[j][:, A_WIDTH:2 * A_WIDTH]
        z = -(w0_ref[...] + decay[j])
        softplus = jnp.maximum(z, 0.0) + jnp.log(1.0 + jnp.exp(-jnp.abs(z)))
        w = -softplus - 0.5
        a_lr = jax.nn.sigmoid(a0_ref[...] + rate[j])
        kk = kks[j] / jnp.maximum(jnp.sqrt(sq[j]), 1e-12)
        r_out[p, :] = pm[j][:, 0:A_WIDTH]
        k_out[p, :] = k * (1.0 + (a_lr - 1.0) * ka_ref[...])
        v_out[p, :] = pm[j][:, 2 * A_WIDTH:3 * A_WIDTH]
        a_out[p, :] = -kk
        b_out[p, :] = kk * a_lr
        ld_out[p, :] = -jnp.exp(w)
        g_out[p, :] = gate[j]
        ga_out[p, :] = gates[j][:, :D_MODEL].astype(BF16)
        ybg_out[p, :] = (gates[j][:, D_MODEL:] * yb[j]).astype(BF16)


def _wkv_kernel(r_ref, k_ref, v_ref, a_ref, b_ref, ld_ref, g_ref, rk_ref, lng_ref, lnb_ref,
                ones_ref, o_ref, st_ref, ta_s, tl_s, arb_s, ark_s, rt_s, bkt_s, dcol_s, y_s):
    C = WKV_CHUNK
    bb, tb, _ = r_ref.shape
    n_chunks = tb // C
    n_pairs = A_WIDTH // HEAD_PAIR

    @pl.when(pl.program_id(1) == 0)
    def _():
        st_ref[...] = jnp.zeros_like(st_ref)

    row = lax.broadcasted_iota(jnp.int32, (C, HEAD_PAIR), 0)
    src = lax.broadcasted_iota(jnp.int32, (C, HEAD_PAIR), 1) & (C - 1)
    incl = src <= row
    strict = src < row
    eye_pair = jnp.where(src == row, 1.0, 0.0)
    bd_mask = ((lax.broadcasted_iota(jnp.int32, (HEAD_PAIR, HEAD_PAIR), 0) >= A_HEAD)
               == (lax.broadcasted_iota(jnp.int32, (HEAD_PAIR, HEAD_PAIR), 1) >= A_HEAD))
    tri_c = (lax.broadcasted_iota(jnp.int32, (C, C), 0)
             >= lax.broadcasted_iota(jnp.int32, (C, C), 1)).astype(BF16)
    ones_pair = ones_ref[...]
    pair_cols = [slice(p * HEAD_PAIR, (p + 1) * HEAD_PAIR) for p in range(n_pairs)]

    def bd(x):
        xb = x.astype(BF16)
        return jnp.where(bd_mask, jnp.concatenate([xb, xb], axis=0), jnp.zeros((), BF16))

    def prep(it, carry):
        where, lhs, rhs, ats = [], [], [], []
        for j in range(WKV_PREP_GROUP):
            flat = it * WKV_PREP_GROUP + j
            b = flat // n_chunks
            c = flat % n_chunks
            sl = pl.ds(pl.multiple_of(c * C, C), C)
            ld = ld_ref[b, sl, :]
            hi, mid, lo = _split3(ld)
            cl = _dot(tri_c, hi) + _dot(tri_c, mid) + _dot(tri_c, lo)
            cl_end = cl[C - 1:C, :]
            k = k_ref[b, sl, :]
            bv = b_ref[b, sl, :]
            d_inv = jnp.exp(-cl)
            d_tail = jnp.exp(cl_end - cl)
            rt = (r_ref[b, sl, :] * jnp.exp(cl)).astype(BF16)
            rt_s[b, sl, :] = rt
            kt = k * d_inv
            bt = bv * d_inv
            at = a_ref[b, sl, :] * jnp.exp(cl - ld)
            kd = k * d_tail
            bdk = bv * d_tail
            d_end = jnp.exp(cl_end)
            for p, cs in enumerate(pair_cols):
                bkt_s[b, c, p] = jnp.concatenate([bdk[:, cs], kd[:, cs]], axis=0).T.astype(BF16)
                dcol_s[b, c, p] = jnp.broadcast_to(d_end[:, cs], (HEAD_PAIR, HEAD_PAIR)).T
                where.append((b, sl, cs))
                ats.append(at[:, cs])
                lhs.append(jnp.concatenate([rt[:, cs], at[:, cs].astype(BF16)], axis=0))
                rhs.append(jnp.concatenate([bd(kt[:, cs]), bd(bt[:, cs])], axis=0))
        n = len(where)
        amat = [_dot_nt(lhs[i], rhs[i]) for i in range(n)]
        l_ak, l_ab = [], []
        for i, (b, sl, cs) in enumerate(where):
            ark_s[b, sl, cs] = jnp.where(incl, amat[i][:C, :HEAD_PAIR], 0.0).astype(BF16)
            arb_s[b, sl, cs] = jnp.where(incl, amat[i][:C, HEAD_PAIR:], 0.0).astype(BF16)
            l_ak.append(jnp.where(strict, amat[i][C:, :HEAD_PAIR], 0.0))
            l_ab.append(jnp.where(strict, amat[i][C:, HEAD_PAIR:], 0.0))

        t_mat = [eye_pair + l for l in l_ab]
        q = [_dot(l.astype(BF16), bd(l)) for l in l_ab]
        n_sq = 1
        while 2 * n_sq < C // 2:
            both = [_dot(q[i].astype(BF16), jnp.concatenate([bd(q[i]), bd(t_mat[i])], axis=1))
                    for i in range(n)]
            q = [x[:, :HEAD_PAIR] for x in both]
            t_mat = [t_mat[i] + both[i][:, HEAD_PAIR:] for i in range(n)]
            n_sq *= 2
        t_mat = [t_mat[i] + _dot(q[i].astype(BF16), bd(t_mat[i])) for i in range(n)]
        tal = [_dot(t_mat[i].astype(BF16), jnp.concatenate([bd(ats[i]), bd(l_ak[i])], axis=1))
               for i in range(n)]
        for i, (b, sl, cs) in enumerate(where):
            ta_s[b, sl, cs] = tal[i][:, :HEAD_PAIR].astype(BF16)
            tl_s[b, sl, cs] = tal[i][:, HEAD_PAIR:].astype(BF16)
        return carry

    lax.fori_loop(0, bb * n_chunks // WKV_PREP_GROUP, prep, 0)

    def step(c, carry):
        sl = pl.ds(pl.multiple_of(c * C, C), C)
        chains = [(b, p) for b in range(bb) for p in range(n_pairs)]
        v = [v_ref[b, sl, :] for b in range(bb)]
        st = [st_ref[b, p] for b, p in chains]
        st_b = [x.astype(BF16) for x in st]
        bd_v = [bd(v[b][:, pair_cols[p]]) for b, p in chains]
        on_st = [_dot(jnp.concatenate([ta_s[b, sl, pair_cols[p]], rt_s[b, sl, pair_cols[p]]],
                                      axis=0), st_b[i]) for i, (b, p) in enumerate(chains)]
        on_v = [_dot(jnp.concatenate([tl_s[b, sl, pair_cols[p]], ark_s[b, sl, pair_cols[p]]],
                                     axis=0), bd_v[i]) for i, (b, p) in enumerate(chains)]
        u = [on_st[i][:C] + on_v[i][:C] for i in range(len(chains))]
        for i, (b, p) in enumerate(chains):
            uv = jnp.concatenate([u[i], v[b][:, pair_cols[p]]], axis=0).astype(BF16)
            st_ref[b, p] = (dcol_s[b, c, p] * st[i]
                            + jnp.where(bd_mask, _dot(bkt_s[b, c, p], uv), 0.0))
        ys = [on_st[i][C:] + on_v[i][C:] + _dot(arb_s[b, sl, pair_cols[p]], bd(u[i]))
              for i, (b, p) in enumerate(chains)]
        for b in range(bb):
            y_s[b, sl, :] = jnp.concatenate(ys[b * n_pairs:(b + 1) * n_pairs], axis=1)
        return carry

    lax.fori_loop(0, n_chunks, step, 0)

    def finish(it, carry):
        items = [(b, pl.ds(pl.multiple_of((it * WKV_NORM_GROUP + j) * C, C), C))
                 for j in range(WKV_NORM_GROUP) for b in range(bb)]
        y = [y_s[b, sl, :] for b, sl in items]
        rkr = [r_ref[b, sl, :] * k_ref[b, sl, :] * rk_ref[...] for b, sl in items]
        sums = [_pair_head_sums([y[i], rkr[i]], ones_pair) for i in range(len(items))]
        yc = [y[i] - sums[i][0] * (1.0 / A_HEAD) for i in range(len(items))]
        var = [_pair_head_sums([x * x], ones_pair)[0] * (1.0 / A_HEAD) for x in yc]
        for i, (b, sl) in enumerate(items):
            yn = yc[i] * lax.rsqrt(var[i] + LNX_EPS) * lng_ref[...] + lnb_ref[...]
            out = (yn + sums[i][1] * v_ref[b, sl, :]) * g_ref[b, sl, :]
            o_ref[b, sl, :] = out.astype(o_ref.dtype)
        return carry

    lax.fori_loop(0, n_chunks // WKV_NORM_GROUP, finish, 0)


def _post_kernel(x_ref, ya_ref, ga_ref, ybg_ref, woa_ref, wout_ref, g2_ref, wrt_ref, brt_ref,
                 x2_out, hl_out, route_out, cnt_out):
    n_sub = x_ref.shape[0] // MOE_SUB
    groups = [slice(g * MOE_SUB, (g + 1) * MOE_SUB) for g in range(n_sub)]
    y_a = [_dot(ya_ref[p, :].astype(BF16), woa_ref[...]) for p in groups]
    x2 = [x_ref[p, :] + _dot((ga_ref[p, :] * y + ybg_ref[p, :]).astype(BF16), wout_ref[...])
          for p, y in zip(groups, y_a)]
    h2 = [_rms(x, g2_ref[...]) for x in x2]
    split = [_split2(h) for h in h2]
    on_hi = [_dot_nt(wrt_ref[...], hi) for hi, _ in split]
    on_lo = [_dot_nt(wrt_ref[0:LANES, :], lo) for _, lo in split]
    row = lax.broadcasted_iota(jnp.int32, (LANES, MOE_SUB), 0)
    neg = jnp.float32(-jnp.inf)
    big = jnp.int32(LANES)
    is_grp = (row >= N_EXPERTS) & (row < N_EXPERTS + N_GROUPS)
    top = lambda v: jnp.max(v, axis=0, keepdims=True)
    first = lambda hit: jnp.min(jnp.where(hit, row, big), axis=0, keepdims=True)
    picks, onehots, weights = [], [], []
    for g, p in enumerate(groups):
        x2_out[p, :] = x2[g].astype(x2_out.dtype)
        logits = on_hi[g][:LANES, :] + on_hi[g][LANES:, :] + on_lo[g] + brt_ref[...]
        gl = jnp.where(is_grp, logits, neg)
        gmax = top(gl)
        g_p = 1.0 / jnp.sum(jnp.exp(gl - gmax), axis=0, keepdims=True)
        lo_row = (first(gl == gmax) - N_EXPERTS) * EXPERTS_PER_GROUP
        el = jnp.where((row >= lo_row) & (row < lo_row + EXPERTS_PER_GROUP), logits, neg)
        e1 = top(el)
        i1 = first(el == e1)
        el2 = jnp.where(row == i1, neg, el)
        e2 = top(el2)
        i2 = first(el2 == e2)
        t = jnp.exp(e2 - e1)
        weights.append((g_p / (1.0 + t), g_p * t / (1.0 + t)))
        picks.append((i1, i2))
        onehots.append(jnp.where((row == i1) | (row == i2), 1.0, 0.0))

    earlier = (lax.broadcasted_iota(jnp.int32, (MOE_SUB, MOE_SUB), 0)
               <= lax.broadcasted_iota(jnp.int32, (MOE_SUB, MOE_SUB), 1)).astype(BF16)
    incl = [_dot(oh.astype(BF16), earlier) for oh in onehots]
    chunks = [jnp.ceil(x[:, MOE_SUB - 1:] * (1.0 / RUN_ALIGN)) for x in incl]
    below = (lax.broadcasted_iota(jnp.int32, (LANES, LANES), 1)
             < lax.broadcasted_iota(jnp.int32, (LANES, LANES), 0)).astype(BF16)
    wide = [jnp.broadcast_to(c, (LANES, LANES)) for c in chunks]
    run_start = [_dot(below, w.astype(BF16))[:, 0:1] * RUN_ALIGN for w in wide]
    row8 = lax.broadcasted_iota(jnp.int32, (8, LANES), 0)
    counts = jnp.zeros((8, LANES), F32)
    slot_ids = lax.broadcasted_iota(jnp.int32, (RUN_SLOTS, MOE_SUB), 0)
    for g, p in enumerate(groups):
        counts = jnp.where(row8 == g, wide[g].T[0:8, :], counts)
        slot_of = run_start[g] + incl[g] - onehots[g]
        i1, i2 = picks[g]
        slot1 = jnp.sum(jnp.where(row == i1, slot_of, 0.0), axis=0, keepdims=True)
        slot2 = jnp.sum(jnp.where(row == i2, slot_of, 0.0), axis=0, keepdims=True)
        w1, w2 = weights[g]
        packed = jnp.where(row == 0, slot1, jnp.where(row == 1, slot2, jnp.where(
            row == 2, w1, jnp.where(row == 3, w2, 0.0))))
        route_out[p, :] = packed.T
        pick = jnp.where((slot_ids == slot1.astype(jnp.int32))
                         | (slot_ids == slot2.astype(jnp.int32)), 1.0, 0.0).astype(BF16)
        hl_out[g * RUN_SLOTS:(g + 1) * RUN_SLOTS, :] = _dot(pick, h2[g].astype(BF16)).astype(BF16)
    cnt_out[...] = counts


def _expert_kernel(chunk_s, first_s, hl_in_ref, wg_ref, wu_ref, wd_ref, hl_ref, xbuf, wgu_s, wd_s,
                   sem, drain_s):
    del hl_in_ref
    e = pl.program_id(0)
    pass_chunks = EXPERT_ROWS // RUN_ALIGN
    mine = e % 2
    other = 1 - mine
    gather_sem = lambda buf: sem.at[buf]
    scatter_sem = lambda buf: sem.at[2 + buf]

    def first_pass(ex):
        lo = first_s[ex]
        return lo, jnp.minimum(lo + pass_chunks, first_s[ex + 1])

    def for_chunks(buf, c_lo, c_hi, fn):
        def one(c):
            fn(hl_ref.at[pl.ds(pl.multiple_of(chunk_s[c], RUN_ALIGN), RUN_ALIGN)],
               xbuf.at[buf, pl.ds(pl.multiple_of((c - c_lo) * RUN_ALIGN, RUN_ALIGN), RUN_ALIGN)])

        def block(i, carry):
            for u in range(CHUNK_UNROLL):
                one(c_lo + i * CHUNK_UNROLL + u)
            return carry

        def tail(c, carry):
            one(c)
            return carry

        n_blocks = (c_hi - c_lo) // CHUNK_UNROLL
        lax.fori_loop(0, n_blocks, block, 0)
        lax.fori_loop(c_lo + n_blocks * CHUNK_UNROLL, c_hi, tail, 0)

    def gather(buf, c_lo, c_hi):
        for_chunks(buf, c_lo, c_hi, lambda hbm, vm: pltpu.make_async_copy(
            hbm, vm, gather_sem(buf)).start())

    def scatter(buf, c_lo, c_hi):
        for_chunks(buf, c_lo, c_hi, lambda hbm, vm: pltpu.make_async_copy(
            vm, hbm, scatter_sem(buf)).start())

    def wait_chunks(which_sem, n):
        def wait_rows(rows):
            def body(c, carry):
                pltpu.make_async_copy(hl_ref.at[pl.ds(0, rows)], xbuf.at[0, pl.ds(0, rows)],
                                      which_sem).wait()
                return carry
            return body

        lax.fori_loop(0, n // CHUNK_UNROLL, wait_rows(CHUNK_UNROLL * RUN_ALIGN), 0)
        lax.fori_loop(0, n % CHUNK_UNROLL, wait_rows(RUN_ALIGN), 0)

    def evaluate(buf, n_chunks):
        def tile(k, carry):
            sl = pl.ds(pl.multiple_of(k * EXPERT_TILE, EXPERT_TILE), EXPERT_TILE)
            gu = _dot(xbuf[buf, sl, :], wgu_s[...])
            gate = gu[:, :D_EXPERT]
            act = gate * jax.nn.sigmoid(gate) * gu[:, D_EXPERT:]
            xbuf[buf, sl, :] = _dot(act.astype(BF16), wd_s[...]).astype(BF16)
            return carry

        lax.fori_loop(0, (n_chunks * RUN_ALIGN + EXPERT_TILE - 1) // EXPERT_TILE, tile, 0)

    @pl.when(e == 0)
    def _():
        xbuf[...] = jnp.zeros_like(xbuf)
        gather(0, *first_pass(0))
        drain_s[0] = 0

    wgu_s[:, :D_EXPERT] = wg_ref[0].astype(BF16)
    wgu_s[:, D_EXPERT:] = wu_ref[0].astype(BF16)
    wd_s[...] = wd_ref[0].astype(BF16)

    c_lo, c_hi = first_pass(e)
    wait_chunks(gather_sem(mine), c_hi - c_lo)
    wait_chunks(scatter_sem(other), drain_s[0])

    @pl.when(e + 1 < pl.num_programs(0))
    def _():
        gather(other, *first_pass(e + 1))

    evaluate(mine, c_hi - c_lo)
    scatter(mine, c_lo, c_hi)

    def later_pass(state):
        lo, pending = state
        hi = jnp.minimum(lo + pass_chunks, first_s[e + 1])
        wait_chunks(scatter_sem(mine), pending)
        gather(mine, lo, hi)
        wait_chunks(gather_sem(mine), hi - lo)
        evaluate(mine, hi - lo)
        scatter(mine, lo, hi)
        return hi, hi - lo

    _, pending = lax.while_loop(lambda st: st[0] < first_s[e + 1], later_pass,
                                (c_hi, c_hi - c_lo))
    drain_s[0] = pending

    @pl.when(e + 1 == pl.num_programs(0))
    def _():
        wait_chunks(scatter_sem(mine), pending)


def _combine_kernel(x2_ref, route_ref, gf_ref, yl_ref, o_ref):
    slots = lax.broadcasted_iota(jnp.int32, (MOE_SUB, RUN_SLOTS), 1).astype(F32)
    for g in range(x2_ref.shape[0] // MOE_SUB):
        p = slice(g * MOE_SUB, (g + 1) * MOE_SUB)
        route = route_ref[p, :]
        pick = (jnp.where(slots == route[:, 0:1], route[:, 2:3], 0.0)
                + jnp.where(slots == route[:, 1:2], route[:, 3:4], 0.0))
        y = _dot(pick.astype(BF16), yl_ref[g * RUN_SLOTS:(g + 1) * RUN_SLOTS, :])
        o_ref[p, :] = _rms(x2_ref[p, :] + y, gf_ref[...])


def _full(shape):
    return pl.BlockSpec(shape, lambda *_: (0,) * len(shape))


def _params(sem, vmem_limit=VMEM_LIMIT):
    return pltpu.CompilerParams(dimension_semantics=sem, vmem_limit_bytes=vmem_limit)


def _mixer_a_layout(t):
    s_rkv = 3 * A_WIDTH
    s_w = s_rkv + D_DECAY_LORA
    s_a = s_w + D_AAA_LORA
    gap = lambda n: jnp.zeros((t.shape[0], n), t.dtype)
    return jnp.concatenate(
        [t[:, :s_w], gap(XA_OFF - D_DECAY_LORA), t[:, s_w:s_a], gap(XG_OFF - XA_OFF - D_AAA_LORA),
         t[:, s_a:], gap(LORA_PAD - XG_OFF - D_GATE_LORA)], axis=1)


def _place(cols, parts):
    out = jnp.zeros((parts[0][1].shape[0], cols), parts[0][1].dtype)
    for off, arr in parts:
        out = lax.dynamic_update_slice(out, arr, (0, off))
    return out


def kernel(x, norm1_g, w_in, b_gate, tmix_mu, w0, w2, a0, a2, g2, k_k, k_a, r_k, lnx_g, lnx_b,
           w_oA, lnv_g, lnv_b, w_s, b_s, w_oB, w_out, norm2_g, w_rg, b_rg, w_re, b_re,
           w_e_gate, w_e_up, w_e_down, final_g):
    bsz, seq, d = x.shape
    n_tok = bsz * seq
    depth = norm1_g.shape[0]
    assert depth == 1, "the moe kernel fuses the final norm, so it must be the last layer"
    assert bsz % WKV_SEQS == 0 and seq % WKV_TOKENS == 0
    xf = x.reshape(n_tok, d)

    s_rkv = 3 * A_WIDTH
    s_w = s_rkv + D_DECAY_LORA
    s_a = s_w + D_AAA_LORA
    a_cols = s_a + D_GATE_LORA
    b_cols = 2 * B_WIDTH

    ones_bd = (jnp.arange(A_WIDTH)[:, None] // A_HEAD
               == jnp.arange(A_WIDTH)[None, :] // A_HEAD).astype(BF16)

    tm_a = INPROJ_TILE
    tm_p = POST_TILE
    tm_c = COMBINE_TILE
    assert tm_p % MOE_SUB == 0 and tm_c % MOE_SUB == 0 and tm_a % DENSE_SPLIT == 0
    assert seq % tm_a == 0, "an inproj tile must not straddle two sequences (token shift)"

    for l in range(depth):
        wl = w_in[l]
        n_w = 4
        col_blk = lambda rows: pl.BlockSpec((rows, d // n_w), lambda i: (0, i))
        row_blk = lambda rows: pl.BlockSpec((rows // n_w, d), lambda i: (i, 0))
        w_a, w_b, w_g, w_oa, w_ob, w_o = pl.pallas_call(
            _w_in_kernel,
            grid=(n_w,),
            in_specs=[col_blk(wl.shape[1]), row_blk(A_WIDTH), row_blk(B_WIDTH), row_blk(d)],
            out_specs=[col_blk(A_PROJ), col_blk(b_cols), col_blk(2 * d),
                       row_blk(A_WIDTH), row_blk(B_WIDTH), row_blk(d)],
            out_shape=[jax.ShapeDtypeStruct((A_PROJ, d), BF16),
                       jax.ShapeDtypeStruct((b_cols, d), BF16),
                       jax.ShapeDtypeStruct((2 * d, d), BF16),
                       jax.ShapeDtypeStruct((A_WIDTH, d), BF16),
                       jax.ShapeDtypeStruct((B_WIDTH, d), BF16),
                       jax.ShapeDtypeStruct((d, d), BF16)],
            compiler_params=_params(("parallel",)),
            name="w_in_layout",
        )(wl.T, w_oA[l], w_oB[l], w_out[l])
        mu_a = _mixer_a_layout(tmix_mu[l][None, :])
        w2p = jnp.pad(w2[l], ((0, LANES - D_DECAY_LORA), (0, 0))).astype(BF16)
        a2p = jnp.pad(a2[l], ((0, LANES - D_AAA_LORA), (0, 0))).astype(BF16)
        g2p = jnp.pad(g2[l], ((0, LORA_PAD - XG_OFF - D_GATE_LORA), (0, 0))).astype(BF16)
        g1 = norm1_g[l][None, :]

        row512 = lambda i: (i, 0)
        tok_a = pl.BlockSpec((tm_a, A_WIDTH), row512)
        tok_d = pl.BlockSpec((tm_a, d), row512)
        vec_a = _full((1, A_WIDTH))
        bs_full = jnp.repeat(b_s[l].T, B_GROUP_CH, axis=1)
        outs = pl.pallas_call(
            functools.partial(_inproj_kernel, seq // tm_a),
            grid=(n_tok // tm_a,),
            in_specs=[tok_d,
                      _full((1, d)), _full((A_PROJ, d)), _full((1, A_PROJ)), vec_a,
                      _full((LANES, A_WIDTH)), vec_a, _full((LANES, A_WIDTH)),
                      _full((LORA_PAD - XG_OFF, A_WIDTH)), vec_a, vec_a,
                      _full((A_WIDTH, A_WIDTH)),
                      _full((b_cols, d)), _full((2 * d, d)), _full((1, 2 * d)),
                      _full((1, B_WIDTH)), _full((1, B_WIDTH)),
                      _full((B_GROUPS, GMLP_BLOCK, GMLP_BLOCK)), _full((GMLP_BLOCK, B_WIDTH)),
                      _full((B_WIDTH, d))],
            out_specs=[tok_a] * 7 + [tok_d, tok_d],
            out_shape=[jax.ShapeDtypeStruct((n_tok, A_WIDTH), F32)] * 7
            + [jax.ShapeDtypeStruct((n_tok, d), BF16)] * 2,
            scratch_shapes=[pltpu.VMEM((8, A_PROJ), F32)],
            compiler_params=_params(("arbitrary",)),
            name="inproj",
        )(xf, g1, w_a, mu_a, w0[l][None, :], w2p, a0[l][None, :], a2p, g2p,
          k_k[l][None, :], k_a[l][None, :], ones_bd, w_b, w_g, b_gate[l][None, :],
          lnv_g[l][None, :], lnv_b[l][None, :], w_s[l], bs_full, w_ob)
        r_, k_, v_, a_, b_, ld_, gg_, ybg, ga = outs

        n_pairs = A_WIDTH // HEAD_PAIR
        n_chunks = WKV_TOKENS // WKV_CHUNK
        tok_w = pl.BlockSpec((WKV_SEQS, WKV_TOKENS, A_WIDTH), lambda bi, ti: (bi, ti, 0))
        vec_w = _full((1, A_WIDTH))
        seq_major = lambda t: t.reshape(bsz, seq, A_WIDTH)
        tok_scratch = pltpu.VMEM((WKV_SEQS, WKV_TOKENS, A_WIDTH), BF16)
        ya_in = pl.pallas_call(
            _wkv_kernel,
            grid=(bsz // WKV_SEQS, seq // WKV_TOKENS),
            in_specs=[tok_w] * 7 + [vec_w, vec_w, vec_w, _full((HEAD_PAIR, HEAD_PAIR))],
            out_specs=tok_w,
            out_shape=jax.ShapeDtypeStruct((bsz, seq, A_WIDTH), BF16),
            scratch_shapes=[pltpu.VMEM((WKV_SEQS, n_pairs, HEAD_PAIR, HEAD_PAIR), F32)]
            + [tok_scratch] * 5
            + [pltpu.VMEM((WKV_SEQS, n_chunks, n_pairs, HEAD_PAIR, HEAD_PAIR), BF16),
               pltpu.VMEM((WKV_SEQS, n_chunks, n_pairs, HEAD_PAIR, HEAD_PAIR), F32),
               pltpu.VMEM((WKV_SEQS, WKV_TOKENS, A_WIDTH), F32)],
            compiler_params=_params(("parallel", "arbitrary")),
            name="wkv",
        )(*[seq_major(t) for t in (r_, k_, v_, a_, b_, ld_, gg_)], r_k[l].reshape(1, A_WIDTH),
          lnx_g[l][None, :], lnx_b[l][None, :],
          ones_bd[:HEAD_PAIR, :HEAD_PAIR]).reshape(n_tok, A_WIDTH)

        w_r = _place(LANES, [(0, jnp.transpose(w_re[l], (1, 0, 2)).reshape(d, N_EXPERTS)),
                             (N_EXPERTS, w_rg[l])])
        b_r = _place(LANES, [(0, b_re[l].reshape(1, N_EXPERTS)), (N_EXPERTS, b_rg[l][None, :])])
        wr_hi = w_r.astype(BF16)
        wr_lo = (w_r - wr_hi.astype(F32)).astype(BF16)
        wr_t = jnp.concatenate([wr_hi.T, wr_lo.T], axis=0)
        tok_p = pl.BlockSpec((tm_p, d), row512)
        lane_p = pl.BlockSpec((tm_p, LANES), row512)
        n_sub = n_tok // MOE_SUB
        sorted_rows = tm_p // MOE_SUB * RUN_SLOTS
        x2, h_sorted, route, cnt = pl.pallas_call(
            _post_kernel,
            grid=(n_tok // tm_p,),
            in_specs=[tok_p, pl.BlockSpec((tm_p, A_WIDTH), row512), tok_p, tok_p,
                      _full((A_WIDTH, d)), _full((d, d)), _full((1, d)), _full((2 * LANES, d)),
                      _full((LANES, 1))],
            out_specs=[tok_p, pl.BlockSpec((sorted_rows, d), row512), lane_p,
                       pl.BlockSpec((8, LANES), row512)],
            out_shape=[jax.ShapeDtypeStruct((n_tok, d), F32),
                       jax.ShapeDtypeStruct((n_sub * RUN_SLOTS, d), BF16),
                       jax.ShapeDtypeStruct((n_tok, LANES), F32),
                       jax.ShapeDtypeStruct((n_tok // tm_p * 8, LANES), F32)],
            compiler_params=_params(("parallel",), POST_VMEM_LIMIT),
            name="post",
        )(xf, ya_in, ga, ybg, w_oa, w_o,
          norm2_g[l][None, :], wr_t, b_r.T)

        run_chunks = cnt.reshape(n_tok // tm_p, 8, LANES)[:, :tm_p // MOE_SUB, :N_EXPERTS]
        run_chunks = run_chunks.reshape(n_sub, N_EXPERTS).astype(jnp.int32)
        run_start = (jnp.cumsum(run_chunks, axis=1) - run_chunks) * RUN_ALIGN
        group_end = jnp.cumsum(run_chunks, axis=0)
        first_chunk = jnp.concatenate([jnp.zeros((1,), jnp.int32), jnp.cumsum(group_end[-1])])
        pos = jnp.arange(n_sub * RUN_SLOTS // RUN_ALIGN, dtype=jnp.int32)
        e_of = jnp.minimum(jnp.sum(pos[:, None] >= first_chunk[None, 1:], axis=1), N_EXPERTS - 1)
        of_e = (e_of[:, None] == jnp.arange(N_EXPERTS, dtype=jnp.int32)[None, :])
        column = lambda t: jnp.dot(of_e.astype(F32), t.T.astype(F32),
                                   precision=lax.Precision.HIGHEST).astype(jnp.int32)
        in_expert = pos - jnp.sum(jnp.where(of_e, first_chunk[None, :-1], 0), axis=1)
        ends = column(group_end)
        g_of = jnp.minimum(jnp.sum(in_expert[:, None] >= ends, axis=1), n_sub - 1)
        of_g = g_of[:, None] == jnp.arange(n_sub, dtype=jnp.int32)[None, :]
        at_g = lambda t: jnp.sum(jnp.where(of_g, t, 0), axis=1)
        in_run = in_expert - at_g(ends - column(run_chunks))
        chunk_rows = g_of * RUN_SLOTS + at_g(column(run_start)) + in_run * RUN_ALIGN

        per_expert = lambda e, *_: (e, 0, 0)
        y_sorted = pl.pallas_call(
            _expert_kernel,
            grid_spec=pltpu.PrefetchScalarGridSpec(
                num_scalar_prefetch=2, grid=(N_EXPERTS,),
                in_specs=[pl.BlockSpec(memory_space=pl.ANY),
                          pl.BlockSpec((1, d, D_EXPERT), per_expert),
                          pl.BlockSpec((1, d, D_EXPERT), per_expert),
                          pl.BlockSpec((1, D_EXPERT, d), per_expert)],
                out_specs=pl.BlockSpec(memory_space=pl.ANY),
                scratch_shapes=[pltpu.VMEM((2, EXPERT_ROWS, d), BF16),
                                pltpu.VMEM((d, 2 * D_EXPERT), BF16),
                                pltpu.VMEM((D_EXPERT, d), BF16),
                                pltpu.SemaphoreType.DMA((4,)),
                                pltpu.SMEM((1,), jnp.int32)]),
            out_shape=jax.ShapeDtypeStruct((n_sub * RUN_SLOTS, d), BF16),
            input_output_aliases={2: 0},
            compiler_params=_params(("arbitrary",)),
            name="moe_experts",
        )(chunk_rows, first_chunk, h_sorted, w_e_gate[l], w_e_up[l], w_e_down[l])

        xf = pl.pallas_call(
            _combine_kernel,
            grid=(n_tok // tm_c,),
            in_specs=[pl.BlockSpec((tm_c, d), row512), pl.BlockSpec((tm_c, LANES), row512),
                      _full((1, d)), pl.BlockSpec((tm_c // MOE_SUB * RUN_SLOTS, d), row512)],
            out_specs=pl.BlockSpec((tm_c, d), row512),
            out_shape=jax.ShapeDtypeStruct((n_tok, d), F32),
            compiler_params=_params(("parallel",)),
            name="moe_combine",
        )(x2, route, final_g[None, :], y_sorted)

    return xf.reshape(bsz, seq, d)
```

```python
import functools

import jax
import jax.numpy as jnp
from jax import lax
from jax.experimental import pallas as pl
from jax.experimental.pallas import tpu as pltpu

F32 = jnp.float32
BF16 = jnp.bfloat16

D_MODEL = 1024
A_WIDTH = 512
A_HEAD = 64
D_DECAY_LORA = 64
D_AAA_LORA = 64
D_GATE_LORA = 160
B_WIDTH = 512
B_GROUPS = 4
B_GROUP_CH = 128
GMLP_BLOCK = 128
N_GROUPS = 4
EXPERTS_PER_GROUP = 8
N_EXPERTS = 32
D_EXPERT = 256
NORM_EPS = 1e-6
LN_EPS = 1e-5
LNX_EPS = 64e-5

LANES = 128
LORA_PAD = 512
XW_OFF, XA_OFF, XG_OFF = 0, 128, 256
A_PROJ = 3 * A_WIDTH + LORA_PAD
WKV_CHUNK = 64
HEAD_PAIR = 2 * A_HEAD
WKV_SEQS = 8
WKV_TOKENS = 64
WKV_PREP_GROUP = 8
WKV_NORM_GROUP = 1
MOE_SUB = 256
RUN_ALIGN = 16
RUN_SLOTS = 2 * MOE_SUB + N_EXPERTS * RUN_ALIGN
EXPERT_ROWS = 2048
EXPERT_TILE = 512
INPROJ_TILE = 512
POST_TILE = 1024
COMBINE_TILE = 1024
COMBINE_MAIN = 768
DENSE_SPLIT = 256
CHUNK_UNROLL = 8
VMEM_LIMIT = 48 * 1024 * 1024
POST_VMEM_LIMIT = 56 * 1024 * 1024


def _rms(x, g):
    return x * lax.rsqrt(jnp.mean(x * x, axis=-1, keepdims=True) + NORM_EPS) * g


def _dot(a, b):
    return jnp.dot(a, b, preferred_element_type=F32)


def _dot_nt(a, b):
    return lax.dot_general(a, b, (((1,), (1,)), ((), ())), preferred_element_type=F32)


def _split2(x):
    hi = x.astype(BF16)
    lo = (x - hi.astype(F32)).astype(BF16)
    return hi, lo


def _split3(x):
    hi = x.astype(BF16)
    r1 = x - hi.astype(F32)
    mid = r1.astype(BF16)
    lo = (r1 - mid.astype(F32)).astype(BF16)
    return hi, mid, lo


def _pair_head_sums(xs, ones_pair):
    n_tiles = A_WIDTH // HEAD_PAIR
    rows = xs[0].shape[0]
    tiles = [x[:, t * HEAD_PAIR:(t + 1) * HEAD_PAIR].astype(BF16) for x in xs for t in range(n_tiles)]
    sums = _dot(jnp.concatenate(tiles, axis=0), ones_pair)
    return [jnp.concatenate([sums[(i * n_tiles + t) * rows:(i * n_tiles + t + 1) * rows]
                             for t in range(n_tiles)], axis=1) for i in range(len(xs))]


def _w_in_kernel(w_ref, woa_ref, wob_ref, wout_ref, wa_out, wb_out, wg_out, woa_out, wob_out,
                 wout_out):
    woa_out[...] = woa_ref[...].astype(BF16)
    wob_out[...] = wob_ref[...].astype(BF16)
    wout_out[...] = wout_ref[...].astype(BF16)
    s_rkv = 3 * A_WIDTH
    s_w = s_rkv + D_DECAY_LORA
    s_a = s_w + D_AAA_LORA
    a_cols = s_a + D_GATE_LORA
    b_end = a_cols + 2 * B_WIDTH
    wa_out[...] = jnp.zeros_like(wa_out)
    wa_out[:s_w, :] = w_ref[:s_w, :].astype(BF16)
    wa_out[s_rkv + XA_OFF:s_rkv + XA_OFF + D_AAA_LORA, :] = w_ref[s_w:s_a, :].astype(BF16)
    wa_out[s_rkv + XG_OFF:s_rkv + XG_OFF + D_GATE_LORA, :] = w_ref[s_a:a_cols, :].astype(BF16)
    wb_out[...] = w_ref[a_cols:b_end, :].astype(BF16)
    wg_out[...] = w_ref[b_end:, :].astype(BF16)


def _inproj_kernel(tiles_per_seq, x_ref, g1_ref, wa_ref, mu_ref, w0_ref, w2_ref, a0_ref, a2_ref,
                   g2_ref, kk_ref, ka_ref, ones_ref, wb_ref, wg_ref, bg_ref, lng_ref, lnb_ref,
                   ws_ref, bs_ref, wo_ref,
                   r_out, k_out, v_out, a_out, b_out, ld_out, g_out, ybg_out, ga_out, tail_ref):
    i = pl.program_id(0)
    tm = x_ref.shape[0]
    parts = [slice(j * DENSE_SPLIT, (j + 1) * DENSE_SPLIT) for j in range(tm // DENSE_SPLIT)]
    g1 = g1_ref[...]
    h = [_rms(x_ref[p, :], g1).astype(BF16) for p in parts]
    proj = [_dot_nt(hh, wa_ref[...]) for hh in h]
    pb = [_dot_nt(hh, wb_ref[...]) for hh in h]

    prev = [jnp.where(i % tiles_per_seq == 0, 0.0, tail_ref[7:8, :])]
    prev += [p[DENSE_SPLIT - 1:, :] for p in proj[:-1]]
    tail_ref[...] = proj[-1][DENSE_SPLIT - 8:, :]
    row = lax.broadcasted_iota(jnp.int32, proj[0].shape, 0)
    pm = []
    for p, pv in zip(proj, prev):
        shifted = jnp.where(row == 0, pv, pltpu.roll(p, 1, axis=0))
        pm.append(p + mu_ref[...] * (shifted - p))
    gates = [jax.nn.sigmoid(_dot_nt(hh, wg_ref[...]) + bg_ref[...]) for hh in h]

    tri = (lax.broadcasted_iota(jnp.int32, (GMLP_BLOCK, GMLP_BLOCK), 0)
           >= lax.broadcasted_iota(jnp.int32, (GMLP_BLOCK, GMLP_BLOCK), 1))
    ws = [jnp.where(tri, ws_ref[grp], 0.0).astype(BF16) for grp in range(B_GROUPS)]
    bs = bs_ref[...]
    us, vns = [], []
    for x in pb:
        z = 0.5 * x * (1.0 + lax.erf(x * (2.0 ** -0.5)))
        us.append(z[:, :B_WIDTH])
        v = z[:, B_WIDTH:]
        mean = jnp.mean(v, axis=-1, keepdims=True)
        vc = v - mean
        var = jnp.mean(vc * vc, axis=-1, keepdims=True)
        vns.append((vc * lax.rsqrt(var + LN_EPS) * lng_ref[...] + lnb_ref[...]).astype(BF16))

    lora = [x[:, 3 * A_WIDTH:] for x in pm]
    decay = [_dot(jnp.tanh(x[:, XW_OFF:XW_OFF + LANES]).astype(BF16), w2_ref[...]) for x in lora]
    rate = [_dot(x[:, XA_OFF:XA_OFF + LANES].astype(BF16), a2_ref[...]) for x in lora]
    gate = [_dot(jax.nn.sigmoid(x[:, XG_OFF:]).astype(BF16), g2_ref[...]) for x in lora]
    kks = [x[:, A_WIDTH:2 * A_WIDTH] * kk_ref[...] for x in pm]
    sq = [_dot((kk * kk).astype(BF16), ones_ref[...]) for kk in kks]

    svs = []
    for vn in vns:
        rows = []
        for blk in range(DENSE_SPLIT // GMLP_BLOCK):
            cols = [_dot(ws[grp], vn[blk * GMLP_BLOCK:(blk + 1) * GMLP_BLOCK,
                                     grp * B_GROUP_CH:(grp + 1) * B_GROUP_CH])
                    for grp in range(B_GROUPS)]
            rows.append(jnp.concatenate(cols, axis=1) + bs)
        svs.append(jnp.concatenate(rows, axis=0))
    yb = [_dot((u * sv).astype(BF16), wo_ref[...]) for u, sv in zip(us, svs)]

    for j, p in enumerate(parts):
        k = pm[j][:, A_WIDTH:2 * A_WIDTH]
        z = -(w0_ref[...] + decay[j])
        softplus = jnp.maximum(z, 0.0) + jnp.log(1.0 + jnp.exp(-jnp.abs(z)))
        w = -softplus - 0.5
        a_lr = jax.nn.sigmoid(a0_ref[...] + rate[j])
        kk = kks[j] / jnp.maximum(jnp.sqrt(sq[j]), 1e-12)
        r_out[p, :] = pm[j][:, 0:A_WIDTH]
        k_out[p, :] = k * (1.0 + (a_lr - 1.0) * ka_ref[...])
        v_out[p, :] = pm[j][:, 2 * A_WIDTH:3 * A_WIDTH]
        a_out[p, :] = -kk
        b_out[p, :] = kk * a_lr
        ld_out[p, :] = -jnp.exp(w)
        g_out[p, :] = gate[j]
        ga_out[p, :] = gates[j][:, :D_MODEL].astype(BF16)
        ybg_out[p, :] = (gates[j][:, D_MODEL:] * yb[j]).astype(BF16)


def _wkv_kernel(r_ref, k_ref, v_ref, a_ref, b_ref, ld_ref, g_ref, rk_ref, lng_ref, lnb_ref,
                ones_ref, o_ref, st_ref, ta_s, tl_s, arb_s, ark_s, rt_s, bkt_s, dcol_s, y_s):
    C = WKV_CHUNK
    bb, tb, _ = r_ref.shape
    n_chunks = tb // C
    n_pairs = A_WIDTH // HEAD_PAIR

    @pl.when(pl.program_id(1) == 0)
    def _():
        st_ref[...] = jnp.zeros_like(st_ref)

    row = lax.broadcasted_iota(jnp.int32, (C, HEAD_PAIR), 0)
    src = lax.broadcasted_iota(jnp.int32, (C, HEAD_PAIR), 1) & (C - 1)
    incl = src <= row
    strict = src < row
    eye_pair = jnp.where(src == row, 1.0, 0.0)
    bd_mask = ((lax.broadcasted_iota(jnp.int32, (HEAD_PAIR, HEAD_PAIR), 0) >= A_HEAD)
               == (lax.broadcasted_iota(jnp.int32, (HEAD_PAIR, HEAD_PAIR), 1) >= A_HEAD))
    tri_c = (lax.broadcasted_iota(jnp.int32, (C, C), 0)
             >= lax.broadcasted_iota(jnp.int32, (C, C), 1)).astype(BF16)
    ones_pair = ones_ref[...]
    pair_cols = [slice(p * HEAD_PAIR, (p + 1) * HEAD_PAIR) for p in range(n_pairs)]

    def bd(x):
        xb = x.astype(BF16)
        return jnp.where(bd_mask, jnp.concatenate([xb, xb], axis=0), jnp.zeros((), BF16))

    def prep(it, carry):
        where, lhs, rhs, ats = [], [], [], []
        for j in range(WKV_PREP_GROUP):
            flat = it * WKV_PREP_GROUP + j
            b = flat // n_chunks
            c = flat % n_chunks
            sl = pl.ds(pl.multiple_of(c * C, C), C)
            ld = ld_ref[b, sl, :]
            hi, mid, lo = _split3(ld)
            cl = _dot(tri_c, hi) + _dot(tri_c, mid) + _dot(tri_c, lo)
            cl_end = cl[C - 1:C, :]
            k = k_ref[b, sl, :]
            bv = b_ref[b, sl, :]
            d_inv = jnp.exp(-cl)
            d_tail = jnp.exp(cl_end - cl)
            rt = (r_ref[b, sl, :] * jnp.exp(cl)).astype(BF16)
            rt_s[b, sl, :] = rt
            kt = k * d_inv
            bt = bv * d_inv
            at = a_ref[b, sl, :] * jnp.exp(cl - ld)
            kd = k * d_tail
            bdk = bv * d_tail
            d_end = jnp.exp(cl_end)
            for p, cs in enumerate(pair_cols):
                bkt_s[b, c, p] = jnp.concatenate([bdk[:, cs], kd[:, cs]], axis=0).T.astype(BF16)
                dcol_s[b, c, p] = jnp.broadcast_to(d_end[:, cs], (HEAD_PAIR, HEAD_PAIR)).T
                where.append((b, sl, cs))
                ats.append(at[:, cs])
                lhs.append(jnp.concatenate([rt[:, cs], at[:, cs].astype(BF16)], axis=0))
                rhs.append(jnp.concatenate([bd(kt[:, cs]), bd(bt[:, cs])], axis=0))
        n = len(where)
        amat = [_dot_nt(lhs[i], rhs[i]) for i in range(n)]
        l_ak, l_ab = [], []
        for i, (b, sl, cs) in enumerate(where):
            ark_s[b, sl, cs] = jnp.where(incl, amat[i][:C, :HEAD_PAIR], 0.0).astype(BF16)
            arb_s[b, sl, cs] = jnp.where(incl, amat[i][:C, HEAD_PAIR:], 0.0).astype(BF16)
            l_ak.append(jnp.where(strict, amat[i][C:, :HEAD_PAIR], 0.0))
            l_ab.append(jnp.where(strict, amat[i][C:, HEAD_PAIR:], 0.0))

        t_mat = [eye_pair + l for l in l_ab]
        q = [_dot(l.astype(BF16), bd(l)) for l in l_ab]
        n_sq = 1
        while 2 * n_sq < C // 2:
            both = [_dot(q[i].astype(BF16), jnp.concatenate([bd(q[i]), bd(t_mat[i])], axis=1))
                    for i in range(n)]
            q = [x[:, :HEAD_PAIR] for x in both]
            t_mat = [t_mat[i] + both[i][:, HEAD_PAIR:] for i in range(n)]
            n_sq *= 2
        t_mat = [t_mat[i] + _dot(q[i].astype(BF16), bd(t_mat[i])) for i in range(n)]
        tal = [_dot(t_mat[i].astype(BF16), jnp.concatenate([bd(ats[i]), bd(l_ak[i])], axis=1))
               for i in range(n)]
        for i, (b, sl, cs) in enumerate(where):
            ta_s[b, sl, cs] = tal[i][:, :HEAD_PAIR].astype(BF16)
            tl_s[b, sl, cs] = tal[i][:, HEAD_PAIR:].astype(BF16)
        return carry

    lax.fori_loop(0, bb * n_chunks // WKV_PREP_GROUP, prep, 0)

    def step(c, carry):
        sl = pl.ds(pl.multiple_of(c * C, C), C)
        chains = [(b, p) for b in range(bb) for p in range(n_pairs)]
        v = [v_ref[b, sl, :] for b in range(bb)]
        st = [st_ref[b, p] for b, p in chains]
        st_b = [x.astype(BF16) for x in st]
        bd_v = [bd(v[b][:, pair_cols[p]]) for b, p in chains]
        on_st = [_dot(jnp.concatenate([ta_s[b, sl, pair_cols[p]], rt_s[b, sl, pair_cols[p]]],
                                      axis=0), st_b[i]) for i, (b, p) in enumerate(chains)]
        on_v = [_dot(jnp.concatenate([tl_s[b, sl, pair_cols[p]], ark_s[b, sl, pair_cols[p]]],
                                     axis=0), bd_v[i]) for i, (b, p) in enumerate(chains)]
        u = [on_st[i][:C] + on_v[i][:C] for i in range(len(chains))]
        for i, (b, p) in enumerate(chains):
            uv = jnp.concatenate([u[i], v[b][:, pair_cols[p]]], axis=0).astype(BF16)
            st_ref[b, p] = (dcol_s[b, c, p] * st[i]
                            + jnp.where(bd_mask, _dot(bkt_s[b, c, p], uv), 0.0))
        ys = [on_st[i][C:] + on_v[i][C:] + _dot(arb_s[b, sl, pair_cols[p]], bd(u[i]))
              for i, (b, p) in enumerate(chains)]
        for b in range(bb):
            y_s[b, sl, :] = jnp.concatenate(ys[b * n_pairs:(b + 1) * n_pairs], axis=1)
        return carry

    lax.fori_loop(0, n_chunks, step, 0)

    def finish(it, carry):
        items = [(b, pl.ds(pl.multiple_of((it * WKV_NORM_GROUP + j) * C, C), C))
                 for j in range(WKV_NORM_GROUP) for b in range(bb)]
        y = [y_s[b, sl, :] for b, sl in items]
        rkr = [r_ref[b, sl, :] * k_ref[b, sl, :] * rk_ref[...] for b, sl in items]
        sums = [_pair_head_sums([y[i], rkr[i]], ones_pair) for i in range(len(items))]
        yc = [y[i] - sums[i][0] * (1.0 / A_HEAD) for i in range(len(items))]
        var = [_pair_head_sums([x * x], ones_pair)[0] * (1.0 / A_HEAD) for x in yc]
        for i, (b, sl) in enumerate(items):
            yn = yc[i] * lax.rsqrt(var[i] + LNX_EPS) * lng_ref[...] + lnb_ref[...]
            out = (yn + sums[i][1] * v_ref[b, sl, :]) * g_ref[b, sl, :]
            o_ref[b, sl, :] = out.astype(o_ref.dtype)
        return carry

    lax.fori_loop(0, n_chunks // WKV_NORM_GROUP, finish, 0)


def _post_kernel(x_ref, ya_ref, ga_ref, ybg_ref, woa_ref, wout_ref, g2_ref, wrt_ref, brt_ref,
                 x2_out, hl_out, route_out, cnt_out):
    n_sub = x_ref.shape[0] // MOE_SUB
    groups = [slice(g * MOE_SUB, (g + 1) * MOE_SUB) for g in range(n_sub)]
    y_a = [_dot(ya_ref[p, :].astype(BF16), woa_ref[...]) for p in groups]
    x2 = [x_ref[p, :] + _dot((ga_ref[p, :] * y + ybg_ref[p, :]).astype(BF16), wout_ref[...])
          for p, y in zip(groups, y_a)]
    h2 = [_rms(x, g2_ref[...]) for x in x2]
    split = [_split2(h) for h in h2]
    on_hi = [_dot_nt(wrt_ref[...], hi) for hi, _ in split]
    on_lo = [_dot_nt(wrt_ref[0:LANES, :], lo) for _, lo in split]
    row = lax.broadcasted_iota(jnp.int32, (LANES, MOE_SUB), 0)
    neg = jnp.float32(-jnp.inf)
    big = jnp.int32(LANES)
    is_grp = (row >= N_EXPERTS) & (row < N_EXPERTS + N_GROUPS)
    top = lambda v: jnp.max(v, axis=0, keepdims=True)
    first = lambda hit: jnp.min(jnp.where(hit, row, big), axis=0, keepdims=True)
    picks, onehots, weights = [], [], []
    for g, p in enumerate(groups):
        x2_out[p, :] = x2[g].astype(x2_out.dtype)
        logits = on_hi[g][:LANES, :] + on_hi[g][LANES:, :] + on_lo[g] + brt_ref[...]
        gl = jnp.where(is_grp, logits, neg)
        gmax = top(gl)
        g_p = 1.0 / jnp.sum(jnp.exp(gl - gmax), axis=0, keepdims=True)
        lo_row = (first(gl == gmax) - N_EXPERTS) * EXPERTS_PER_GROUP
        el = jnp.where((row >= lo_row) & (row < lo_row + EXPERTS_PER_GROUP), logits, neg)
        e1 = top(el)
        i1 = first(el == e1)
        el2 = jnp.where(row == i1, neg, el)
        e2 = top(el2)
        i2 = first(el2 == e2)
        t = jnp.exp(e2 - e1)
        weights.append((g_p / (1.0 + t), g_p * t / (1.0 + t)))
        picks.append((i1, i2))
        onehots.append(jnp.where((row == i1) | (row == i2), 1.0, 0.0))

    earlier = (lax.broadcasted_iota(jnp.int32, (MOE_SUB, MOE_SUB), 0)
               <= lax.broadcasted_iota(jnp.int32, (MOE_SUB, MOE_SUB), 1)).astype(BF16)
    incl = [_dot(oh.astype(BF16), earlier) for oh in onehots]
    chunks = [jnp.ceil(x[:, MOE_SUB - 1:] * (1.0 / RUN_ALIGN)) for x in incl]
    below = (lax.broadcasted_iota(jnp.int32, (LANES, LANES), 1)
             < lax.broadcasted_iota(jnp.int32, (LANES, LANES), 0)).astype(BF16)
    wide = [jnp.broadcast_to(c, (LANES, LANES)) for c in chunks]
    run_start = [_dot(below, w.astype(BF16))[:, 0:1] * RUN_ALIGN for w in wide]
    row8 = lax.broadcasted_iota(jnp.int32, (8, LANES), 0)
    counts = jnp.zeros((8, LANES), F32)
    slot_ids = lax.broadcasted_iota(jnp.int32, (RUN_SLOTS, MOE_SUB), 0)
    for g, p in enumerate(groups):
        counts = jnp.where(row8 == g, wide[g].T[0:8, :], counts)
        slot_of = run_start[g] + incl[g] - onehots[g]
        i1, i2 = picks[g]
        slot1 = jnp.sum(jnp.where(row == i1, slot_of, 0.0), axis=0, keepdims=True)
        slot2 = jnp.sum(jnp.where(row == i2, slot_of, 0.0), axis=0, keepdims=True)
        w1, w2 = weights[g]
        packed = jnp.where(row == 0, slot1, jnp.where(row == 1, slot2, jnp.where(
            row == 2, w1, jnp.where(row == 3, w2, 0.0))))
        route_out[p, :] = packed.T
        pick = jnp.where((slot_ids == slot1.astype(jnp.int32))
                         | (slot_ids == slot2.astype(jnp.int32)), 1.0, 0.0).astype(BF16)
        hl_out[g * RUN_SLOTS:(g + 1) * RUN_SLOTS, :] = _dot(pick, h2[g].astype(BF16)).astype(BF16)
    cnt_out[...] = counts


def _expert_kernel(chunk_s, first_s, hl_in_ref, wg_ref, wu_ref, wd_ref, hl_ref, xbuf, wgu_s, wd_s,
                   sem, drain_s):
    del hl_in_ref
    e = pl.program_id(0)
    pass_chunks = EXPERT_ROWS // RUN_ALIGN
    mine = e % 2
    other = 1 - mine
    gather_sem = lambda buf: sem.at[buf]
    scatter_sem = lambda buf: sem.at[2 + buf]

    def first_pass(ex):
        lo = first_s[ex]
        return lo, jnp.minimum(lo + pass_chunks, first_s[ex + 1])

    def for_chunks(buf, c_lo, c_hi, fn):
        def one(c):
            fn(hl_ref.at[pl.ds(pl.multiple_of(chunk_s[c], RUN_ALIGN), RUN_ALIGN)],
               xbuf.at[buf, pl.ds(pl.multiple_of((c - c_lo) * RUN_ALIGN, RUN_ALIGN), RUN_ALIGN)])

        def block(i, carry):
            for u in range(CHUNK_UNROLL):
                one(c_lo + i * CHUNK_UNROLL + u)
            return carry

        def tail(c, carry):
            one(c)
            return carry

        n_blocks = (c_hi - c_lo) // CHUNK_UNROLL
        lax.fori_loop(0, n_blocks, block, 0)
        lax.fori_loop(c_lo + n_blocks * CHUNK_UNROLL, c_hi, tail, 0)

    def gather(buf, c_lo, c_hi):
        for_chunks(buf, c_lo, c_hi, lambda hbm, vm: pltpu.make_async_copy(
            hbm, vm, gather_sem(buf)).start())

    def scatter(buf, c_lo, c_hi):
        for_chunks(buf, c_lo, c_hi, lambda hbm, vm: pltpu.make_async_copy(
            vm, hbm, scatter_sem(buf)).start())

    def wait_chunks(which_sem, n):
        def wait_rows(rows):
            def body(c, carry):
                pltpu.make_async_copy(hl_ref.at[pl.ds(0, rows)], xbuf.at[0, pl.ds(0, rows)],
                                      which_sem).wait()
                return carry
            return body

        lax.fori_loop(0, n // CHUNK_UNROLL, wait_rows(CHUNK_UNROLL * RUN_ALIGN), 0)
        lax.fori_loop(0, n % CHUNK_UNROLL, wait_rows(RUN_ALIGN), 0)

    def evaluate(buf, n_chunks):
        def tile(k, carry):
            sl = pl.ds(pl.multiple_of(k * EXPERT_TILE, EXPERT_TILE), EXPERT_TILE)
            gu = _dot(xbuf[buf, sl, :], wgu_s[...])
            gate = gu[:, :D_EXPERT]
            act = gate * jax.nn.sigmoid(gate) * gu[:, D_EXPERT:]
            xbuf[buf, sl, :] = _dot(act.astype(BF16), wd_s[...]).astype(BF16)
            return carry

        lax.fori_loop(0, (n_chunks * RUN_ALIGN + EXPERT_TILE - 1) // EXPERT_TILE, tile, 0)

    @pl.when(e == 0)
    def _():
        xbuf[...] = jnp.zeros_like(xbuf)
        gather(0, *first_pass(0))
        drain_s[0] = 0

    wgu_s[:, :D_EXPERT] = wg_ref[0].astype(BF16)
    wgu_s[:, D_EXPERT:] = wu_ref[0].astype(BF16)
    wd_s[...] = wd_ref[0].astype(BF16)

    c_lo, c_hi = first_pass(e)
    wait_chunks(gather_sem(mine), c_hi - c_lo)
    wait_chunks(scatter_sem(other), drain_s[0])

    @pl.when(e + 1 < pl.num_programs(0))
    def _():
        gather(other, *first_pass(e + 1))

    evaluate(mine, c_hi - c_lo)
    scatter(mine, c_lo, c_hi)

    def later_pass(state):
        lo, pending = state
        hi = jnp.minimum(lo + pass_chunks, first_s[e + 1])
        wait_chunks(scatter_sem(mine), pending)
        gather(mine, lo, hi)
        wait_chunks(gather_sem(mine), hi - lo)
        evaluate(mine, hi - lo)
        scatter(mine, lo, hi)
        return hi, hi - lo

    _, pending = lax.while_loop(lambda st: st[0] < first_s[e + 1], later_pass,
                                (c_hi, c_hi - c_lo))
    drain_s[0] = pending

    @pl.when(e + 1 == pl.num_programs(0))
    def _():
        wait_chunks(scatter_sem(mine), pending)


def _combine_kernel(used_s, x2_ref, route_ref, gf_ref, yl_ref, yl_hbm, o_ref, tail_s, sem):
    n_g = x2_ref.shape[0] // MOE_SUB
    base = pl.program_id(0) * n_g

    def tail_copy(g):
        return pltpu.make_async_copy(
            yl_hbm.at[base + g, pl.ds(COMBINE_MAIN, RUN_SLOTS - COMBINE_MAIN), :],
            tail_s.at[g], sem.at[g])

    def gather(g, lo, hi, rows):
        p = slice(g * MOE_SUB, (g + 1) * MOE_SUB)
        slots = (lax.broadcasted_iota(jnp.int32, (MOE_SUB, hi - lo), 1) + lo).astype(F32)
        route = route_ref[p, :]
        pick = (jnp.where(slots == route[:, 0:1], route[:, 2:3], 0.0)
                + jnp.where(slots == route[:, 1:2], route[:, 3:4], 0.0))
        return _dot(pick.astype(BF16), rows)

    for g in range(n_g):
        @pl.when(used_s[base + g] > COMBINE_MAIN)
        def _(g=g):
            tail_copy(g).start()

    for g in range(n_g):
        p = slice(g * MOE_SUB, (g + 1) * MOE_SUB)
        o_ref[p, :] = x2_ref[p, :] + gather(g, 0, COMBINE_MAIN, yl_ref[g])

        @pl.when(used_s[base + g] > COMBINE_MAIN)
        def _(g=g, p=p):
            tail_copy(g).wait()
            o_ref[p, :] += gather(g, COMBINE_MAIN, RUN_SLOTS, tail_s[g])

        o_ref[p, :] = _rms(o_ref[p, :], gf_ref[...])


def _full(shape):
    return pl.BlockSpec(shape, lambda *_: (0,) * len(shape))


def _params(sem, vmem_limit=VMEM_LIMIT):
    return pltpu.CompilerParams(dimension_semantics=sem, vmem_limit_bytes=vmem_limit)


def _mixer_a_layout(t):
    s_rkv = 3 * A_WIDTH
    s_w = s_rkv + D_DECAY_LORA
    s_a = s_w + D_AAA_LORA
    gap = lambda n: jnp.zeros((t.shape[0], n), t.dtype)
    return jnp.concatenate(
        [t[:, :s_w], gap(XA_OFF - D_DECAY_LORA), t[:, s_w:s_a], gap(XG_OFF - XA_OFF - D_AAA_LORA),
         t[:, s_a:], gap(LORA_PAD - XG_OFF - D_GATE_LORA)], axis=1)


def _place(cols, parts):
    out = jnp.zeros((parts[0][1].shape[0], cols), parts[0][1].dtype)
    for off, arr in parts:
        out = lax.dynamic_update_slice(out, arr, (0, off))
    return out


def kernel(x, norm1_g, w_in, b_gate, tmix_mu, w0, w2, a0, a2, g2, k_k, k_a, r_k, lnx_g, lnx_b,
           w_oA, lnv_g, lnv_b, w_s, b_s, w_oB, w_out, norm2_g, w_rg, b_rg, w_re, b_re,
           w_e_gate, w_e_up, w_e_down, final_g):
    bsz, seq, d = x.shape
    n_tok = bsz * seq
    depth = norm1_g.shape[0]
    assert depth == 1, "the moe kernel fuses the final norm, so it must be the last layer"
    assert bsz % WKV_SEQS == 0 and seq % WKV_TOKENS == 0
    xf = x.reshape(n_tok, d)

    s_rkv = 3 * A_WIDTH
    s_w = s_rkv + D_DECAY_LORA
    s_a = s_w + D_AAA_LORA
    a_cols = s_a + D_GATE_LORA
    b_cols = 2 * B_WIDTH

    ones_bd = (jnp.arange(A_WIDTH)[:, None] // A_HEAD
               == jnp.arange(A_WIDTH)[None, :] // A_HEAD).astype(BF16)

    tm_a = INPROJ_TILE
    tm_p = POST_TILE
    tm_c = COMBINE_TILE
    assert tm_p % MOE_SUB == 0 and tm_c % MOE_SUB == 0 and tm_a % DENSE_SPLIT == 0
    assert seq % tm_a == 0, "an inproj tile must not straddle two sequences (token shift)"

    for l in range(depth):
        wl = w_in[l]
        n_w = 4
        col_blk = lambda rows: pl.BlockSpec((rows, d // n_w), lambda i: (0, i))
        row_blk = lambda rows: pl.BlockSpec((rows // n_w, d), lambda i: (i, 0))
        w_a, w_b, w_g, w_oa, w_ob, w_o = pl.pallas_call(
            _w_in_kernel,
            grid=(n_w,),
            in_specs=[col_blk(wl.shape[1]), row_blk(A_WIDTH), row_blk(B_WIDTH), row_blk(d)],
            out_specs=[col_blk(A_PROJ), col_blk(b_cols), col_blk(2 * d),
                       row_blk(A_WIDTH), row_blk(B_WIDTH), row_blk(d)],
            out_shape=[jax.ShapeDtypeStruct((A_PROJ, d), BF16),
                       jax.ShapeDtypeStruct((b_cols, d), BF16),
                       jax.ShapeDtypeStruct((2 * d, d), BF16),
                       jax.ShapeDtypeStruct((A_WIDTH, d), BF16),
                       jax.ShapeDtypeStruct((B_WIDTH, d), BF16),
                       jax.ShapeDtypeStruct((d, d), BF16)],
            compiler_params=_params(("parallel",)),
            name="w_in_layout",
        )(wl.T, w_oA[l], w_oB[l], w_out[l])
        mu_a = _mixer_a_layout(tmix_mu[l][None, :])
        w2p = jnp.pad(w2[l], ((0, LANES - D_DECAY_LORA), (0, 0))).astype(BF16)
        a2p = jnp.pad(a2[l], ((0, LANES - D_AAA_LORA), (0, 0))).astype(BF16)
        g2p = jnp.pad(g2[l], ((0, LORA_PAD - XG_OFF - D_GATE_LORA), (0, 0))).astype(BF16)
        g1 = norm1_g[l][None, :]

        row512 = lambda i: (i, 0)
        row_s = lambda i, *_: (i, 0)
        tok_a = pl.BlockSpec((tm_a, A_WIDTH), row512)
        tok_d = pl.BlockSpec((tm_a, d), row512)
        vec_a = _full((1, A_WIDTH))
        bs_full = jnp.repeat(b_s[l].T, B_GROUP_CH, axis=1)
        outs = pl.pallas_call(
            functools.partial(_inproj_kernel, seq // tm_a),
            grid=(n_tok // tm_a,),
            in_specs=[tok_d,
                      _full((1, d)), _full((A_PROJ, d)), _full((1, A_PROJ)), vec_a,
                      _full((LANES, A_WIDTH)), vec_a, _full((LANES, A_WIDTH)),
                      _full((LORA_PAD - XG_OFF, A_WIDTH)), vec_a, vec_a,
                      _full((A_WIDTH, A_WIDTH)),
                      _full((b_cols, d)), _full((2 * d, d)), _full((1, 2 * d)),
                      _full((1, B_WIDTH)), _full((1, B_WIDTH)),
                      _full((B_GROUPS, GMLP_BLOCK, GMLP_BLOCK)), _full((GMLP_BLOCK, B_WIDTH)),
                      _full((B_WIDTH, d))],
            out_specs=[tok_a] * 7 + [tok_d, tok_d],
            out_shape=[jax.ShapeDtypeStruct((n_tok, A_WIDTH), F32)] * 7
            + [jax.ShapeDtypeStruct((n_tok, d), BF16)] * 2,
            scratch_shapes=[pltpu.VMEM((8, A_PROJ), F32)],
            compiler_params=_params(("arbitrary",)),
            name="inproj",
        )(xf, g1, w_a, mu_a, w0[l][None, :], w2p, a0[l][None, :], a2p, g2p,
          k_k[l][None, :], k_a[l][None, :], ones_bd, w_b, w_g, b_gate[l][None, :],
          lnv_g[l][None, :], lnv_b[l][None, :], w_s[l], bs_full, w_ob)
        r_, k_, v_, a_, b_, ld_, gg_, ybg, ga = outs

        n_pairs = A_WIDTH // HEAD_PAIR
        n_chunks = WKV_TOKENS // WKV_CHUNK
        tok_w = pl.BlockSpec((WKV_SEQS, WKV_TOKENS, A_WIDTH), lambda bi, ti: (bi, ti, 0))
        vec_w = _full((1, A_WIDTH))
        seq_major = lambda t: t.reshape(bsz, seq, A_WIDTH)
        tok_scratch = pltpu.VMEM((WKV_SEQS, WKV_TOKENS, A_WIDTH), BF16)
        ya_in = pl.pallas_call(
            _wkv_kernel,
            grid=(bsz // WKV_SEQS, seq // WKV_TOKENS),
            in_specs=[tok_w] * 7 + [vec_w, vec_w, vec_w, _full((HEAD_PAIR, HEAD_PAIR))],
            out_specs=tok_w,
            out_shape=jax.ShapeDtypeStruct((bsz, seq, A_WIDTH), BF16),
            scratch_shapes=[pltpu.VMEM((WKV_SEQS, n_pairs, HEAD_PAIR, HEAD_PAIR), F32)]
            + [tok_scratch] * 5
            + [pltpu.VMEM((WKV_SEQS, n_chunks, n_pairs, HEAD_PAIR, HEAD_PAIR), BF16),
               pltpu.VMEM((WKV_SEQS, n_chunks, n_pairs, HEAD_PAIR, HEAD_PAIR), F32),
               pltpu.VMEM((WKV_SEQS, WKV_TOKENS, A_WIDTH), F32)],
            compiler_params=_params(("parallel", "arbitrary")),
            name="wkv",
        )(*[seq_major(t) for t in (r_, k_, v_, a_, b_, ld_, gg_)], r_k[l].reshape(1, A_WIDTH),
          lnx_g[l][None, :], lnx_b[l][None, :],
          ones_bd[:HEAD_PAIR, :HEAD_PAIR]).reshape(n_tok, A_WIDTH)

        w_r = _place(LANES, [(0, jnp.transpose(w_re[l], (1, 0, 2)).reshape(d, N_EXPERTS)),
                             (N_EXPERTS, w_rg[l])])
        b_r = _place(LANES, [(0, b_re[l].reshape(1, N_EXPERTS)), (N_EXPERTS, b_rg[l][None, :])])
        wr_hi = w_r.astype(BF16)
        wr_lo = (w_r - wr_hi.astype(F32)).astype(BF16)
        wr_t = jnp.concatenate([wr_hi.T, wr_lo.T], axis=0)
        tok_p = pl.BlockSpec((tm_p, d), row512)
        lane_p = pl.BlockSpec((tm_p, LANES), row512)
        n_sub = n_tok // MOE_SUB
        sorted_rows = tm_p // MOE_SUB * RUN_SLOTS
        x2, h_sorted, route, cnt = pl.pallas_call(
            _post_kernel,
            grid=(n_tok // tm_p,),
            in_specs=[tok_p, pl.BlockSpec((tm_p, A_WIDTH), row512), tok_p, tok_p,
                      _full((A_WIDTH, d)), _full((d, d)), _full((1, d)), _full((2 * LANES, d)),
                      _full((LANES, 1))],
            out_specs=[tok_p, pl.BlockSpec((sorted_rows, d), row512), lane_p,
                       pl.BlockSpec((8, LANES), row512)],
            out_shape=[jax.ShapeDtypeStruct((n_tok, d), F32),
                       jax.ShapeDtypeStruct((n_sub * RUN_SLOTS, d), BF16),
                       jax.ShapeDtypeStruct((n_tok, LANES), F32),
                       jax.ShapeDtypeStruct((n_tok // tm_p * 8, LANES), F32)],
            compiler_params=_params(("parallel",), POST_VMEM_LIMIT),
            name="post",
        )(xf, ya_in, ga, ybg, w_oa, w_o,
          norm2_g[l][None, :], wr_t, b_r.T)

        run_chunks = cnt.reshape(n_tok // tm_p, 8, LANES)[:, :tm_p // MOE_SUB, :N_EXPERTS]
        run_chunks = run_chunks.reshape(n_sub, N_EXPERTS).astype(jnp.int32)
        run_start = (jnp.cumsum(run_chunks, axis=1) - run_chunks) * RUN_ALIGN
        used = jnp.sum(run_chunks, axis=1) * RUN_ALIGN
        group_end = jnp.cumsum(run_chunks, axis=0)
        first_chunk = jnp.concatenate([jnp.zeros((1,), jnp.int32), jnp.cumsum(group_end[-1])])
        pos = jnp.arange(n_sub * RUN_SLOTS // RUN_ALIGN, dtype=jnp.int32)
        e_of = jnp.minimum(jnp.sum(pos[:, None] >= first_chunk[None, 1:], axis=1), N_EXPERTS - 1)
        of_e = (e_of[:, None] == jnp.arange(N_EXPERTS, dtype=jnp.int32)[None, :])
        column = lambda t: jnp.dot(of_e.astype(F32), t.T.astype(F32),
                                   precision=lax.Precision.HIGHEST).astype(jnp.int32)
        in_expert = pos - jnp.sum(jnp.where(of_e, first_chunk[None, :-1], 0), axis=1)
        ends = column(group_end)
        g_of = jnp.minimum(jnp.sum(in_expert[:, None] >= ends, axis=1), n_sub - 1)
        of_g = g_of[:, None] == jnp.arange(n_sub, dtype=jnp.int32)[None, :]
        at_g = lambda t: jnp.sum(jnp.where(of_g, t, 0), axis=1)
        in_run = in_expert - at_g(ends - column(run_chunks))
        chunk_rows = g_of * RUN_SLOTS + at_g(column(run_start)) + in_run * RUN_ALIGN

        per_expert = lambda e, *_: (e, 0, 0)
        y_sorted = pl.pallas_call(
            _expert_kernel,
            grid_spec=pltpu.PrefetchScalarGridSpec(
                num_scalar_prefetch=2, grid=(N_EXPERTS,),
                in_specs=[pl.BlockSpec(memory_space=pl.ANY),
                          pl.BlockSpec((1, d, D_EXPERT), per_expert),
                          pl.BlockSpec((1, d, D_EXPERT), per_expert),
                          pl.BlockSpec((1, D_EXPERT, d), per_expert)],
                out_specs=pl.BlockSpec(memory_space=pl.ANY),
                scratch_shapes=[pltpu.VMEM((2, EXPERT_ROWS, d), BF16),
                                pltpu.VMEM((d, 2 * D_EXPERT), BF16),
                                pltpu.VMEM((D_EXPERT, d), BF16),
                                pltpu.SemaphoreType.DMA((4,)),
                                pltpu.SMEM((1,), jnp.int32)]),
            out_shape=jax.ShapeDtypeStruct((n_sub * RUN_SLOTS, d), BF16),
            input_output_aliases={2: 0},
            compiler_params=_params(("arbitrary",)),
            name="moe_experts",
        )(chunk_rows, first_chunk, h_sorted, w_e_gate[l], w_e_up[l], w_e_down[l])

        y_blocks = y_sorted.reshape(n_sub, RUN_SLOTS, d)
        xf = pl.pallas_call(
            _combine_kernel,
            grid_spec=pltpu.PrefetchScalarGridSpec(
                num_scalar_prefetch=1, grid=(n_tok // tm_c,),
                in_specs=[pl.BlockSpec((tm_c, d), row_s), pl.BlockSpec((tm_c, LANES), row_s),
                          _full((1, d)),
                          pl.BlockSpec((tm_c // MOE_SUB, COMBINE_MAIN, d),
                                       lambda i, *_: (i, 0, 0)),
                          pl.BlockSpec(memory_space=pl.ANY)],
                out_specs=pl.BlockSpec((tm_c, d), row_s),
                scratch_shapes=[
                    pltpu.VMEM((tm_c // MOE_SUB, RUN_SLOTS - COMBINE_MAIN, d), BF16),
                    pltpu.SemaphoreType.DMA((tm_c // MOE_SUB,))]),
            out_shape=jax.ShapeDtypeStruct((n_tok, d), F32),
            compiler_params=_params(("parallel",)),
            name="moe_combine",
        )(used, x2, route, final_g[None, :], y_blocks, y_blocks)

    return xf.reshape(bsz, seq, d)
```

```python
import functools

import jax
import jax.numpy as jnp
from jax import lax
from jax.experimental import pallas as pl
from jax.experimental.pallas import tpu as pltpu

F32 = jnp.float32
BF16 = jnp.bfloat16

D_MODEL = 1024
A_WIDTH = 512
A_HEAD = 64
D_DECAY_LORA = 64
D_AAA_LORA = 64
D_GATE_LORA = 160
B_WIDTH = 512
B_GROUPS = 4
B_GROUP_CH = 128
GMLP_BLOCK = 128
N_GROUPS = 4
EXPERTS_PER_GROUP = 8
N_EXPERTS = 32
D_EXPERT = 256
NORM_EPS = 1e-6
LN_EPS = 1e-5
LNX_EPS = 64e-5

LANES = 128
LORA_PAD = 512
XW_OFF, XA_OFF, XG_OFF = 0, 128, 256
A_PROJ = 3 * A_WIDTH + LORA_PAD
WKV_CHUNK = 64
HEAD_PAIR = 2 * A_HEAD
WKV_SEQS = 8
WKV_TOKENS = 64
WKV_PREP_GROUP = 8
WKV_NORM_GROUP = 1
MOE_SUB = 256
RUN_ALIGN = 16
RUN_SLOTS = 2 * MOE_SUB + N_EXPERTS * RUN_ALIGN
EXPERT_ROWS = 2048
EXPERT_TILE = 512
INPROJ_TILE = 512
POST_TILE = 1024
COMBINE_TILE = 512
COMBINE_MAIN = 768
DENSE_SPLIT = 256
CHUNK_UNROLL = 8
VMEM_LIMIT = 48 * 1024 * 1024
POST_VMEM_LIMIT = 56 * 1024 * 1024


def _rms(x, g):
    return x * lax.rsqrt(jnp.mean(x * x, axis=-1, keepdims=True) + NORM_EPS) * g


def _dot(a, b):
    return jnp.dot(a, b, preferred_element_type=F32)


def _dot_nt(a, b):
    return lax.dot_general(a, b, (((1,), (1,)), ((), ())), preferred_element_type=F32)


def _split2(x):
    hi = x.astype(BF16)
    lo = (x - hi.astype(F32)).astype(BF16)
    return hi, lo


def _split3(x):
    hi = x.astype(BF16)
    r1 = x - hi.astype(F32)
    mid = r1.astype(BF16)
    lo = (r1 - mid.astype(F32)).astype(BF16)
    return hi, mid, lo


def _pair_head_sums(xs, ones_pair):
    n_tiles = A_WIDTH // HEAD_PAIR
    rows = xs[0].shape[0]
    tiles = [x[:, t * HEAD_PAIR:(t + 1) * HEAD_PAIR].astype(BF16) for x in xs for t in range(n_tiles)]
    sums = _dot(jnp.concatenate(tiles, axis=0), ones_pair)
    return [jnp.concatenate([sums[(i * n_tiles + t) * rows:(i * n_tiles + t + 1) * rows]
                             for t in range(n_tiles)], axis=1) for i in range(len(xs))]


def _w_in_kernel(w_ref, woa_ref, wob_ref, wout_ref, wa_out, wb_out, wg_out, woa_out, wob_out,
                 wout_out):
    woa_out[...] = woa_ref[...].astype(BF16)
    wob_out[...] = wob_ref[...].astype(BF16)
    wout_out[...] = wout_ref[...].astype(BF16)
    s_rkv = 3 * A_WIDTH
    s_w = s_rkv + D_DECAY_LORA
    s_a = s_w + D_AAA_LORA
    a_cols = s_a + D_GATE_LORA
    b_end = a_cols + 2 * B_WIDTH
    wa_out[...] = jnp.zeros_like(wa_out)
    wa_out[:s_w, :] = w_ref[:s_w, :].astype(BF16)
    wa_out[s_rkv + XA_OFF:s_rkv + XA_OFF + D_AAA_LORA, :] = w_ref[s_w:s_a, :].astype(BF16)
    wa_out[s_rkv + XG_OFF:s_rkv + XG_OFF + D_GATE_LORA, :] = w_ref[s_a:a_cols, :].astype(BF16)
    wb_out[...] = w_ref[a_cols:b_end, :].astype(BF16)
    wg_out[...] = w_ref[b_end:, :].astype(BF16)


def _inproj_kernel(tiles_per_seq, x_ref, g1_ref, wa_ref, mu_ref, w0_ref, w2_ref, a0_ref, a2_ref,
                   g2_ref, kk_ref, ka_ref, ones_ref, wb_ref, wg_ref, bg_ref, lng_ref, lnb_ref,
                   ws_ref, bs_ref, wo_ref,
                   r_out, k_out, v_out, a_out, b_out, ld_out, g_out, ybg_out, ga_out, tail_ref):
    i = pl.program_id(0)
    tm = x_ref.shape[0]
    parts = [slice(j * DENSE_SPLIT, (j + 1) * DENSE_SPLIT) for j in range(tm // DENSE_SPLIT)]
    g1 = g1_ref[...]
    h = [_rms(x_ref[p, :], g1).astype(BF16) for p in parts]
    proj = [_dot_nt(hh, wa_ref[...]) for hh in h]
    pb = [_dot_nt(hh, wb_ref[...]) for hh in h]

    prev = [jnp.where(i % tiles_per_seq == 0, 0.0, tail_ref[7:8, :])]
    prev += [p[DENSE_SPLIT - 1:, :] for p in proj[:-1]]
    tail_ref[...] = proj[-1][DENSE_SPLIT - 8:, :]
    row = lax.broadcasted_iota(jnp.int32, proj[0].shape, 0)
    pm = []
    for p, pv in zip(proj, prev):
        shifted = jnp.where(row == 0, pv, pltpu.roll(p, 1, axis=0))
        pm.append(p + mu_ref[...] * (shifted - p))
    gates = [jax.nn.sigmoid(_dot_nt(hh, wg_ref[...]) + bg_ref[...]) for hh in h]

    tri = (lax.broadcasted_iota(jnp.int32, (GMLP_BLOCK, GMLP_BLOCK), 0)
           >= lax.broadcasted_iota(jnp.int32, (GMLP_BLOCK, GMLP_BLOCK), 1))
    ws = [jnp.where(tri, ws_ref[grp], 0.0).astype(BF16) for grp in range(B_GROUPS)]
    bs = bs_ref[...]
    us, vns = [], []
    for x in pb:
        z = 0.5 * x * (1.0 + lax.erf(x * (2.0 ** -0.5)))
        us.append(z[:, :B_WIDTH])
        v = z[:, B_WIDTH:]
        mean = jnp.mean(v, axis=-1, keepdims=True)
        vc = v - mean
        var = jnp.mean(vc * vc, axis=-1, keepdims=True)
        vns.append((vc * lax.rsqrt(var + LN_EPS) * lng_ref[...] + lnb_ref[...]).astype(BF16))

    lora = [x[:, 3 * A_WIDTH:] for x in pm]
    decay = [_dot(jnp.tanh(x[:, XW_OFF:XW_OFF + LANES]).astype(BF16), w2_ref[...]) for x in lora]
    rate = [_dot(x[:, XA_OFF:XA_OFF + LANES].astype(BF16), a2_ref[...]) for x in lora]
    gate = [_dot(jax.nn.sigmoid(x[:, XG_OFF:]).astype(BF16), g2_ref[...]) for x in lora]
    kks = [x[:, A_WIDTH:2 * A_WIDTH] * kk_ref[...] for x in pm]
    sq = [_dot((kk * kk).astype(BF16), ones_ref[...]) for kk in kks]

    svs = []
    for vn in vns:
        rows = []
        for blk in range(DENSE_SPLIT // GMLP_BLOCK):
            cols = [_dot(ws[grp], vn[blk * GMLP_BLOCK:(blk + 1) * GMLP_BLOCK,
                                     grp * B_GROUP_CH:(grp + 1) * B_GROUP_CH])
                    for grp in range(B_GROUPS)]
            rows.append(jnp.concatenate(cols, axis=1) + bs)
        svs.append(jnp.concatenate(rows, axis=0))
    yb = [_dot((u * sv).astype(BF16), wo_ref[...]) for u, sv in zip(us, svs)]

    for j, p in enumerate(parts):
        k = pm[j][:, A_WIDTH:2 * A_WIDTH]
        z = -(w0_ref[...] + decay[j])
        softplus = jnp.maximum(z, 0.0) + jnp.log(1.0 + jnp.exp(-jnp.abs(z)))
        w = -softplus - 0.5
        a_lr = jax.nn.sigmoid(a0_ref[...] + rate[j])
        kk = kks[j] / jnp.maximum(jnp.sqrt(sq[j]), 1e-12)
        r_out[p, :] = pm[j][:, 0:A_WIDTH]
        k_out[p, :] = k * (1.0 + (a_lr - 1.0) * ka_ref[...])
        v_out[p, :] = pm[j][:, 2 * A_WIDTH:3 * A_WIDTH]
        a_out[p, :] = -kk
        b_out[p, :] = kk * a_lr
        ld_out[p, :] = -jnp.exp(w)
        g_out[p, :] = gate[j]
        ga_out[p, :] = gates[j][:, :D_MODEL].astype(BF16)
        ybg_out[p, :] = (gates[j][:, D_MODEL:] * yb[j]).astype(BF16)


def _wkv_kernel(r_ref, k_ref, v_ref, a_ref, b_ref, ld_ref, g_ref, rk_ref, lng_ref, lnb_ref,
                ones_ref, o_ref, st_ref, ta_s, tl_s, arb_s, ark_s, rt_s, bkt_s, dcol_s, y_s):
    C = WKV_CHUNK
    bb, tb, _ = r_ref.shape
    n_chunks = tb // C
    n_pairs = A_WIDTH // HEAD_PAIR

    @pl.when(pl.program_id(1) == 0)
    def _():
        st_ref[...] = jnp.zeros_like(st_ref)

    row = lax.broadcasted_iota(jnp.int32, (C, HEAD_PAIR), 0)
    src = lax.broadcasted_iota(jnp.int32, (C, HEAD_PAIR), 1) & (C - 1)
    incl = src <= row
    strict = src < row
    eye_pair = jnp.where(src == row, 1.0, 0.0)
    bd_mask = ((lax.broadcasted_iota(jnp.int32, (HEAD_PAIR, HEAD_PAIR), 0) >= A_HEAD)
               == (lax.broadcasted_iota(jnp.int32, (HEAD_PAIR, HEAD_PAIR), 1) >= A_HEAD))
    tri_c = (lax.broadcasted_iota(jnp.int32, (C, C), 0)
             >= lax.broadcasted_iota(jnp.int32, (C, C), 1)).astype(BF16)
    ones_pair = ones_ref[...]
    pair_cols = [slice(p * HEAD_PAIR, (p + 1) * HEAD_PAIR) for p in range(n_pairs)]

    def bd(x):
        xb = x.astype(BF16)
        return jnp.where(bd_mask, jnp.concatenate([xb, xb], axis=0), jnp.zeros((), BF16))

    def prep(it, carry):
        where, lhs, rhs, ats = [], [], [], []
        for j in range(WKV_PREP_GROUP):
            flat = it * WKV_PREP_GROUP + j
            b = flat // n_chunks
            c = flat % n_chunks
            sl = pl.ds(pl.multiple_of(c * C, C), C)
            ld = ld_ref[b, sl, :]
            hi, mid, lo = _split3(ld)
            cl = _dot(tri_c, hi) + _dot(tri_c, mid) + _dot(tri_c, lo)
            cl_end = cl[C - 1:C, :]
            k = k_ref[b, sl, :]
            bv = b_ref[b, sl, :]
            d_inv = jnp.exp(-cl)
            d_tail = jnp.exp(cl_end - cl)
            rt = (r_ref[b, sl, :] * jnp.exp(cl)).astype(BF16)
            rt_s[b, sl, :] = rt
            kt = k * d_inv
            bt = bv * d_inv
            at = a_ref[b, sl, :] * jnp.exp(cl - ld)
            kd = k * d_tail
            bdk = bv * d_tail
            d_end = jnp.exp(cl_end)
            for p, cs in enumerate(pair_cols):
                bkt_s[b, c, p] = jnp.concatenate([bdk[:, cs], kd[:, cs]], axis=0).T.astype(BF16)
                dcol_s[b, c, p] = jnp.broadcast_to(d_end[:, cs], (HEAD_PAIR, HEAD_PAIR)).T
                where.append((b, sl, cs))
                ats.append(at[:, cs])
                lhs.append(jnp.concatenate([rt[:, cs], at[:, cs].astype(BF16)], axis=0))
                rhs.append(jnp.concatenate([bd(kt[:, cs]), bd(bt[:, cs])], axis=0))
        n = len(where)
        amat = [_dot_nt(lhs[i], rhs[i]) for i in range(n)]
        l_ak, l_ab = [], []
        for i, (b, sl, cs) in enumerate(where):
            ark_s[b, sl, cs] = jnp.where(incl, amat[i][:C, :HEAD_PAIR], 0.0).astype(BF16)
            arb_s[b, sl, cs] = jnp.where(incl, amat[i][:C, HEAD_PAIR:], 0.0).astype(BF16)
            l_ak.append(jnp.where(strict, amat[i][C:, :HEAD_PAIR], 0.0))
            l_ab.append(jnp.where(strict, amat[i][C:, HEAD_PAIR:], 0.0))

        t_mat = [eye_pair + l for l in l_ab]
        q = [_dot(l.astype(BF16), bd(l)) for l in l_ab]
        n_sq = 1
        while 2 * n_sq < C // 2:
            both = [_dot(q[i].astype(BF16), jnp.concatenate([bd(q[i]), bd(t_mat[i])], axis=1))
                    for i in range(n)]
            q = [x[:, :HEAD_PAIR] for x in both]
            t_mat = [t_mat[i] + both[i][:, HEAD_PAIR:] for i in range(n)]
            n_sq *= 2
        t_mat = [t_mat[i] + _dot(q[i].astype(BF16), bd(t_mat[i])) for i in range(n)]
        tal = [_dot(t_mat[i].astype(BF16), jnp.concatenate([bd(ats[i]), bd(l_ak[i])], axis=1))
               for i in range(n)]
        for i, (b, sl, cs) in enumerate(where):
            ta_s[b, sl, cs] = tal[i][:, :HEAD_PAIR].astype(BF16)
            tl_s[b, sl, cs] = tal[i][:, HEAD_PAIR:].astype(BF16)
        return carry

    lax.fori_loop(0, bb * n_chunks // WKV_PREP_GROUP, prep, 0)

    def step(c, carry):
        sl = pl.ds(pl.multiple_of(c * C, C), C)
        chains = [(b, p) for b in range(bb) for p in range(n_pairs)]
        v = [v_ref[b, sl, :] for b in range(bb)]
        st = [st_ref[b, p] for b, p in chains]
        st_b = [x.astype(BF16) for x in st]
        bd_v = [bd(v[b][:, pair_cols[p]]) for b, p in chains]
        on_st = [_dot(jnp.concatenate([ta_s[b, sl, pair_cols[p]], rt_s[b, sl, pair_cols[p]]],
                                      axis=0), st_b[i]) for i, (b, p) in enumerate(chains)]
        on_v = [_dot(jnp.concatenate([tl_s[b, sl, pair_cols[p]], ark_s[b, sl, pair_cols[p]]],
                                     axis=0), bd_v[i]) for i, (b, p) in enumerate(chains)]
        u = [on_st[i][:C] + on_v[i][:C] for i in range(len(chains))]
        for i, (b, p) in enumerate(chains):
            uv = jnp.concatenate([u[i], v[b][:, pair_cols[p]]], axis=0).astype(BF16)
            st_ref[b, p] = (dcol_s[b, c, p] * st[i]
                            + jnp.where(bd_mask, _dot(bkt_s[b, c, p], uv), 0.0))
        ys = [on_st[i][C:] + on_v[i][C:] + _dot(arb_s[b, sl, pair_cols[p]], bd(u[i]))
              for i, (b, p) in enumerate(chains)]
        for b in range(bb):
            y_s[b, sl, :] = jnp.concatenate(ys[b * n_pairs:(b + 1) * n_pairs], axis=1)
        return carry

    lax.fori_loop(0, n_chunks, step, 0)

    def finish(it, carry):
        items = [(b, pl.ds(pl.multiple_of((it * WKV_NORM_GROUP + j) * C, C), C))
                 for j in range(WKV_NORM_GROUP) for b in range(bb)]
        y = [y_s[b, sl, :] for b, sl in items]
        rkr = [r_ref[b, sl, :] * k_ref[b, sl, :] * rk_ref[...] for b, sl in items]
        sums = [_pair_head_sums([y[i], rkr[i]], ones_pair) for i in range(len(items))]
        yc = [y[i] - sums[i][0] * (1.0 / A_HEAD) for i in range(len(items))]
        var = [_pair_head_sums([x * x], ones_pair)[0] * (1.0 / A_HEAD) for x in yc]
        for i, (b, sl) in enumerate(items):
            yn = yc[i] * lax.rsqrt(var[i] + LNX_EPS) * lng_ref[...] + lnb_ref[...]
            out = (yn + sums[i][1] * v_ref[b, sl, :]) * g_ref[b, sl, :]
            o_ref[b, sl, :] = out.astype(o_ref.dtype)
        return carry

    lax.fori_loop(0, n_chunks // WKV_NORM_GROUP, finish, 0)


def _post_kernel(x_ref, ya_ref, ga_ref, ybg_ref, woa_ref, wout_ref, g2_ref, wrt_ref, brt_ref,
                 x2_out, hl_out, route_out, cnt_out):
    n_sub = x_ref.shape[0] // MOE_SUB
    groups = [slice(g * MOE_SUB, (g + 1) * MOE_SUB) for g in range(n_sub)]
    y_a = [_dot(ya_ref[p, :].astype(BF16), woa_ref[...]) for p in groups]
    x2 = [x_ref[p, :] + _dot((ga_ref[p, :] * y + ybg_ref[p, :]).astype(BF16), wout_ref[...])
          for p, y in zip(groups, y_a)]
    h2 = [_rms(x, g2_ref[...]) for x in x2]
    split = [_split2(h) for h in h2]
    on_hi = [_dot_nt(wrt_ref[...], hi) for hi, _ in split]
    on_lo = [_dot_nt(wrt_ref[0:LANES, :], lo) for _, lo in split]
    row = lax.broadcasted_iota(jnp.int32, (LANES, MOE_SUB), 0)
    neg = jnp.float32(-jnp.inf)
    big = jnp.int32(LANES)
    is_grp = (row >= N_EXPERTS) & (row < N_EXPERTS + N_GROUPS)
    top = lambda v: jnp.max(v, axis=0, keepdims=True)
    first = lambda hit: jnp.min(jnp.where(hit, row, big), axis=0, keepdims=True)
    picks, onehots, weights = [], [], []
    for g, p in enumerate(groups):
        x2_out[p, :] = x2[g].astype(x2_out.dtype)
        logits = on_hi[g][:LANES, :] + on_hi[g][LANES:, :] + on_lo[g] + brt_ref[...]
        gl = jnp.where(is_grp, logits, neg)
        gmax = top(gl)
        g_p = 1.0 / jnp.sum(jnp.exp(gl - gmax), axis=0, keepdims=True)
        lo_row = (first(gl == gmax) - N_EXPERTS) * EXPERTS_PER_GROUP
        el = jnp.where((row >= lo_row) & (row < lo_row + EXPERTS_PER_GROUP), logits, neg)
        e1 = top(el)
        i1 = first(el == e1)
        el2 = jnp.where(row == i1, neg, el)
        e2 = top(el2)
        i2 = first(el2 == e2)
        t = jnp.exp(e2 - e1)
        weights.append((g_p / (1.0 + t), g_p * t / (1.0 + t)))
        picks.append((i1, i2))
        onehots.append(jnp.where((row == i1) | (row == i2), 1.0, 0.0))

    earlier = (lax.broadcasted_iota(jnp.int32, (MOE_SUB, MOE_SUB), 0)
               <= lax.broadcasted_iota(jnp.int32, (MOE_SUB, MOE_SUB), 1)).astype(BF16)
    incl = [_dot(oh.astype(BF16), earlier) for oh in onehots]
    chunks = [jnp.ceil(x[:, MOE_SUB - 1:] * (1.0 / RUN_ALIGN)) for x in incl]
    below = (lax.broadcasted_iota(jnp.int32, (LANES, LANES), 1)
             < lax.broadcasted_iota(jnp.int32, (LANES, LANES), 0)).astype(BF16)
    wide = [jnp.broadcast_to(c, (LANES, LANES)) for c in chunks]
    run_start = [_dot(below, w.astype(BF16))[:, 0:1] * RUN_ALIGN for w in wide]
    row8 = lax.broadcasted_iota(jnp.int32, (8, LANES), 0)
    counts = jnp.zeros((8, LANES), F32)
    slot_ids = lax.broadcasted_iota(jnp.int32, (RUN_SLOTS, MOE_SUB), 0)
    for g, p in enumerate(groups):
        counts = jnp.where(row8 == g, wide[g].T[0:8, :], counts)
        slot_of = run_start[g] + incl[g] - onehots[g]
        i1, i2 = picks[g]
        slot1 = jnp.sum(jnp.where(row == i1, slot_of, 0.0), axis=0, keepdims=True)
        slot2 = jnp.sum(jnp.where(row == i2, slot_of, 0.0), axis=0, keepdims=True)
        w1, w2 = weights[g]
        packed = jnp.where(row == 0, slot1, jnp.where(row == 1, slot2, jnp.where(
            row == 2, w1, jnp.where(row == 3, w2, 0.0))))
        route_out[p, :] = packed.T
        pick = jnp.where((slot_ids == slot1.astype(jnp.int32))
                         | (slot_ids == slot2.astype(jnp.int32)), 1.0, 0.0).astype(BF16)
        hl_out[g * RUN_SLOTS:(g + 1) * RUN_SLOTS, :] = _dot(pick, h2[g].astype(BF16)).astype(BF16)
    cnt_out[...] = counts


def _expert_kernel(chunk_s, first_s, hl_in_ref, wg_ref, wu_ref, wd_ref, hl_ref, xbuf, wgu_s, wd_s,
                   sem, drain_s):
    del hl_in_ref
    e = pl.program_id(0)
    pass_chunks = EXPERT_ROWS // RUN_ALIGN
    mine = e % 2
    other = 1 - mine
    gather_sem = lambda buf: sem.at[buf]
    scatter_sem = lambda buf: sem.at[2 + buf]

    def first_pass(ex):
        lo = first_s[ex]
        return lo, jnp.minimum(lo + pass_chunks, first_s[ex + 1])

    def for_chunks(buf, c_lo, c_hi, fn):
        def one(c):
            fn(hl_ref.at[pl.ds(pl.multiple_of(chunk_s[c], RUN_ALIGN), RUN_ALIGN)],
               xbuf.at[buf, pl.ds(pl.multiple_of((c - c_lo) * RUN_ALIGN, RUN_ALIGN), RUN_ALIGN)])

        def block(i, carry):
            for u in range(CHUNK_UNROLL):
                one(c_lo + i * CHUNK_UNROLL + u)
            return carry

        def tail(c, carry):
            one(c)
            return carry

        n_blocks = (c_hi - c_lo) // CHUNK_UNROLL
        lax.fori_loop(0, n_blocks, block, 0)
        lax.fori_loop(c_lo + n_blocks * CHUNK_UNROLL, c_hi, tail, 0)

    def gather(buf, c_lo, c_hi):
        for_chunks(buf, c_lo, c_hi, lambda hbm, vm: pltpu.make_async_copy(
            hbm, vm, gather_sem(buf)).start())

    def scatter(buf, c_lo, c_hi):
        for_chunks(buf, c_lo, c_hi, lambda hbm, vm: pltpu.make_async_copy(
            vm, hbm, scatter_sem(buf)).start())

    def wait_chunks(which_sem, n):
        def wait_rows(rows):
            def body(c, carry):
                pltpu.make_async_copy(hl_ref.at[pl.ds(0, rows)], xbuf.at[0, pl.ds(0, rows)],
                                      which_sem).wait()
                return carry
            return body

        lax.fori_loop(0, n // CHUNK_UNROLL, wait_rows(CHUNK_UNROLL * RUN_ALIGN), 0)
        lax.fori_loop(0, n % CHUNK_UNROLL, wait_rows(RUN_ALIGN), 0)

    def evaluate(buf, n_chunks):
        def tile(k, carry):
            sl = pl.ds(pl.multiple_of(k * EXPERT_TILE, EXPERT_TILE), EXPERT_TILE)
            gu = _dot(xbuf[buf, sl, :], wgu_s[...])
            gate = gu[:, :D_EXPERT]
            act = gate * jax.nn.sigmoid(gate) * gu[:, D_EXPERT:]
            xbuf[buf, sl, :] = _dot(act.astype(BF16), wd_s[...]).astype(BF16)
            return carry

        lax.fori_loop(0, (n_chunks * RUN_ALIGN + EXPERT_TILE - 1) // EXPERT_TILE, tile, 0)

    @pl.when(e == 0)
    def _():
        xbuf[...] = jnp.zeros_like(xbuf)
        gather(0, *first_pass(0))
        drain_s[0] = 0

    wgu_s[:, :D_EXPERT] = wg_ref[0].astype(BF16)
    wgu_s[:, D_EXPERT:] = wu_ref[0].astype(BF16)
    wd_s[...] = wd_ref[0].astype(BF16)

    c_lo, c_hi = first_pass(e)
    wait_chunks(gather_sem(mine), c_hi - c_lo)
    wait_chunks(scatter_sem(other), drain_s[0])

    @pl.when(e + 1 < pl.num_programs(0))
    def _():
        gather(other, *first_pass(e + 1))

    evaluate(mine, c_hi - c_lo)
    scatter(mine, c_lo, c_hi)

    def later_pass(state):
        lo, pending = state
        hi = jnp.minimum(lo + pass_chunks, first_s[e + 1])
        wait_chunks(scatter_sem(mine), pending)
        gather(mine, lo, hi)
        wait_chunks(gather_sem(mine), hi - lo)
        evaluate(mine, hi - lo)
        scatter(mine, lo, hi)
        return hi, hi - lo

    _, pending = lax.while_loop(lambda st: st[0] < first_s[e + 1], later_pass,
                                (c_hi, c_hi - c_lo))
    drain_s[0] = pending

    @pl.when(e + 1 == pl.num_programs(0))
    def _():
        wait_chunks(scatter_sem(mine), pending)


def _combine_kernel(used_s, x2_ref, route_ref, gf_ref, yl_ref, yl_hbm, o_ref, tail_s, sem):
    n_g = x2_ref.shape[0] // MOE_SUB
    base = pl.program_id(0) * n_g

    def tail_copy(g):
        return pltpu.make_async_copy(
            yl_hbm.at[base + g, pl.ds(COMBINE_MAIN, RUN_SLOTS - COMBINE_MAIN), :],
            tail_s.at[g], sem.at[g])

    def gather(g, lo, hi, rows):
        p = slice(g * MOE_SUB, (g + 1) * MOE_SUB)
        slots = (lax.broadcasted_iota(jnp.int32, (MOE_SUB, hi - lo), 1) + lo).astype(F32)
        route = route_ref[p, :]
        pick = (jnp.where(slots == route[:, 0:1], route[:, 2:3], 0.0)
                + jnp.where(slots == route[:, 1:2], route[:, 3:4], 0.0))
        return _dot(pick.astype(BF16), rows)

    for g in range(n_g):
        @pl.when(used_s[base + g] > COMBINE_MAIN)
        def _(g=g):
            tail_copy(g).start()

    for g in range(n_g):
        p = slice(g * MOE_SUB, (g + 1) * MOE_SUB)
        o_ref[p, :] = x2_ref[p, :] + gather(g, 0, COMBINE_MAIN, yl_ref[g])

        @pl.when(used_s[base + g] > COMBINE_MAIN)
        def _(g=g, p=p):
            tail_copy(g).wait()
            o_ref[p, :] += gather(g, COMBINE_MAIN, RUN_SLOTS, tail_s[g])

        o_ref[p, :] = _rms(o_ref[p, :], gf_ref[...])


def _full(shape):
    return pl.BlockSpec(shape, lambda *_: (0,) * len(shape))


def _params(sem, vmem_limit=VMEM_LIMIT):
    return pltpu.CompilerParams(dimension_semantics=sem, vmem_limit_bytes=vmem_limit)


def _mixer_a_layout(t):
    s_rkv = 3 * A_WIDTH
    s_w = s_rkv + D_DECAY_LORA
    s_a = s_w + D_AAA_LORA
    gap = lambda n: jnp.zeros((t.shape[0], n), t.dtype)
    return jnp.concatenate(
        [t[:, :s_w], gap(XA_OFF - D_DECAY_LORA), t[:, s_w:s_a], gap(XG_OFF - XA_OFF - D_AAA_LORA),
         t[:, s_a:], gap(LORA_PAD - XG_OFF - D_GATE_LORA)], axis=1)


def _place(cols, parts):
    out = jnp.zeros((parts[0][1].shape[0], cols), parts[0][1].dtype)
    for off, arr in parts:
        out = lax.dynamic_update_slice(out, arr, (0, off))
    return out


def kernel(x, norm1_g, w_in, b_gate, tmix_mu, w0, w2, a0, a2, g2, k_k, k_a, r_k, lnx_g, lnx_b,
           w_oA, lnv_g, lnv_b, w_s, b_s, w_oB, w_out, norm2_g, w_rg, b_rg, w_re, b_re,
           w_e_gate, w_e_up, w_e_down, final_g):
    bsz, seq, d = x.shape
    n_tok = bsz * seq
    depth = norm1_g.shape[0]
    assert depth == 1, "the moe kernel fuses the final norm, so it must be the last layer"
    assert bsz % WKV_SEQS == 0 and seq % WKV_TOKENS == 0
    xf = x.reshape(n_tok, d)

    s_rkv = 3 * A_WIDTH
    s_w = s_rkv + D_DECAY_LORA
    s_a = s_w + D_AAA_LORA
    a_cols = s_a + D_GATE_LORA
    b_cols = 2 * B_WIDTH

    ones_bd = (jnp.arange(A_WIDTH)[:, None] // A_HEAD
               == jnp.arange(A_WIDTH)[None, :] // A_HEAD).astype(BF16)

    tm_a = INPROJ_TILE
    tm_p = POST_TILE
    tm_c = COMBINE_TILE
    assert tm_p % MOE_SUB == 0 and tm_c % MOE_SUB == 0 and tm_a % DENSE_SPLIT == 0
    assert seq % tm_a == 0, "an inproj tile must not straddle two sequences (token shift)"

    for l in range(depth):
        wl = w_in[l]
        n_w = 4
        col_blk = lambda rows: pl.BlockSpec((rows, d // n_w), lambda i: (0, i))
        row_blk = lambda rows: pl.BlockSpec((rows // n_w, d), lambda i: (i, 0))
        w_a, w_b, w_g, w_oa, w_ob, w_o = pl.pallas_call(
            _w_in_kernel,
            grid=(n_w,),
            in_specs=[col_blk(wl.shape[1]), row_blk(A_WIDTH), row_blk(B_WIDTH), row_blk(d)],
            out_specs=[col_blk(A_PROJ), col_blk(b_cols), col_blk(2 * d),
                       row_blk(A_WIDTH), row_blk(B_WIDTH), row_blk(d)],
            out_shape=[jax.ShapeDtypeStruct((A_PROJ, d), BF16),
                       jax.ShapeDtypeStruct((b_cols, d), BF16),
                       jax.ShapeDtypeStruct((2 * d, d), BF16),
                       jax.ShapeDtypeStruct((A_WIDTH, d), BF16),
                       jax.ShapeDtypeStruct((B_WIDTH, d), BF16),
                       jax.ShapeDtypeStruct((d, d), BF16)],
            compiler_params=_params(("parallel",)),
            name="w_in_layout",
        )(wl.T, w_oA[l], w_oB[l], w_out[l])
        mu_a = _mixer_a_layout(tmix_mu[l][None, :])
        w2p = jnp.pad(w2[l], ((0, LANES - D_DECAY_LORA), (0, 0))).astype(BF16)
        a2p = jnp.pad(a2[l], ((0, LANES - D_AAA_LORA), (0, 0))).astype(BF16)
        g2p = jnp.pad(g2[l], ((0, LORA_PAD - XG_OFF - D_GATE_LORA), (0, 0))).astype(BF16)
        g1 = norm1_g[l][None, :]

        row512 = lambda i: (i, 0)
        row_s = lambda i, *_: (i, 0)
        tok_a = pl.BlockSpec((tm_a, A_WIDTH), row512)
        tok_d = pl.BlockSpec((tm_a, d), row512)
        vec_a = _full((1, A_WIDTH))
        bs_full = jnp.repeat(b_s[l].T, B_GROUP_CH, axis=1)
        outs = pl.pallas_call(
            functools.partial(_inproj_kernel, seq // tm_a),
            grid=(n_tok // tm_a,),
            in_specs=[tok_d,
                      _full((1, d)), _full((A_PROJ, d)), _full((1, A_PROJ)), vec_a,
                      _full((LANES, A_WIDTH)), vec_a, _full((LANES, A_WIDTH)),
                      _full((LORA_PAD - XG_OFF, A_WIDTH)), vec_a, vec_a,
                      _full((A_WIDTH, A_WIDTH)),
                      _full((b_cols, d)), _full((2 * d, d)), _full((1, 2 * d)),
                      _full((1, B_WIDTH)), _full((1, B_WIDTH)),
                      _full((B_GROUPS, GMLP_BLOCK, GMLP_BLOCK)), _full((GMLP_BLOCK, B_WIDTH)),
                      _full((B_WIDTH, d))],
            out_specs=[tok_a] * 7 + [tok_d, tok_d],
            out_shape=[jax.ShapeDtypeStruct((n_tok, A_WIDTH), F32)] * 7
            + [jax.ShapeDtypeStruct((n_tok, d), BF16)] * 2,
            scratch_shapes=[pltpu.VMEM((8, A_PROJ), F32)],
            compiler_params=_params(("arbitrary",)),
            name="inproj",
        )(xf, g1, w_a, mu_a, w0[l][None, :], w2p, a0[l][None, :], a2p, g2p,
          k_k[l][None, :], k_a[l][None, :], ones_bd, w_b, w_g, b_gate[l][None, :],
          lnv_g[l][None, :], lnv_b[l][None, :], w_s[l], bs_full, w_ob)
        r_, k_, v_, a_, b_, ld_, gg_, ybg, ga = outs

        n_pairs = A_WIDTH // HEAD_PAIR
        n_chunks = WKV_TOKENS // WKV_CHUNK
        tok_w = pl.BlockSpec((WKV_SEQS, WKV_TOKENS, A_WIDTH), lambda bi, ti: (bi, ti, 0))
        vec_w = _full((1, A_WIDTH))
        seq_major = lambda t: t.reshape(bsz, seq, A_WIDTH)
        tok_scratch = pltpu.VMEM((WKV_SEQS, WKV_TOKENS, A_WIDTH), BF16)
        ya_in = pl.pallas_call(
            _wkv_kernel,
            grid=(bsz // WKV_SEQS, seq // WKV_TOKENS),
            in_specs=[tok_w] * 7 + [vec_w, vec_w, vec_w, _full((HEAD_PAIR, HEAD_PAIR))],
            out_specs=tok_w,
            out_shape=jax.ShapeDtypeStruct((bsz, seq, A_WIDTH), BF16),
            scratch_shapes=[pltpu.VMEM((WKV_SEQS, n_pairs, HEAD_PAIR, HEAD_PAIR), F32)]
            + [tok_scratch] * 5
            + [pltpu.VMEM((WKV_SEQS, n_chunks, n_pairs, HEAD_PAIR, HEAD_PAIR), BF16),
               pltpu.VMEM((WKV_SEQS, n_chunks, n_pairs, HEAD_PAIR, HEAD_PAIR), F32),
               pltpu.VMEM((WKV_SEQS, WKV_TOKENS, A_WIDTH), F32)],
            compiler_params=_params(("parallel", "arbitrary")),
            name="wkv",
        )(*[seq_major(t) for t in (r_, k_, v_, a_, b_, ld_, gg_)], r_k[l].reshape(1, A_WIDTH),
          lnx_g[l][None, :], lnx_b[l][None, :],
          ones_bd[:HEAD_PAIR, :HEAD_PAIR]).reshape(n_tok, A_WIDTH)

        w_r = _place(LANES, [(0, jnp.transpose(w_re[l], (1, 0, 2)).reshape(d, N_EXPERTS)),
                             (N_EXPERTS, w_rg[l])])
        b_r = _place(LANES, [(0, b_re[l].reshape(1, N_EXPERTS)), (N_EXPERTS, b_rg[l][None, :])])
        wr_hi = w_r.astype(BF16)
        wr_lo = (w_r - wr_hi.astype(F32)).astype(BF16)
        wr_t = jnp.concatenate([wr_hi.T, wr_lo.T], axis=0)
        tok_p = pl.BlockSpec((tm_p, d), row512)
        lane_p = pl.BlockSpec((tm_p, LANES), row512)
        n_sub = n_tok // MOE_SUB
        sorted_rows = tm_p // MOE_SUB * RUN_SLOTS
        x2, h_sorted, route, cnt = pl.pallas_call(
            _post_kernel,
            grid=(n_tok // tm_p,),
            in_specs=[tok_p, pl.BlockSpec((tm_p, A_WIDTH), row512), tok_p, tok_p,
                      _full((A_WIDTH, d)), _full((d, d)), _full((1, d)), _full((2 * LANES, d)),
                      _full((LANES, 1))],
            out_specs=[tok_p, pl.BlockSpec((sorted_rows, d), row512), lane_p,
                       pl.BlockSpec((8, LANES), row512)],
            out_shape=[jax.ShapeDtypeStruct((n_tok, d), F32),
                       jax.ShapeDtypeStruct((n_sub * RUN_SLOTS, d), BF16),
                       jax.ShapeDtypeStruct((n_tok, LANES), F32),
                       jax.ShapeDtypeStruct((n_tok // tm_p * 8, LANES), F32)],
            compiler_params=_params(("parallel",), POST_VMEM_LIMIT),
            name="post",
        )(xf, ya_in, ga, ybg, w_oa, w_o,
          norm2_g[l][None, :], wr_t, b_r.T)

        run_chunks = cnt.reshape(n_tok // tm_p, 8, LANES)[:, :tm_p // MOE_SUB, :N_EXPERTS]
        run_chunks = run_chunks.reshape(n_sub, N_EXPERTS).astype(jnp.int32)
        run_start = (jnp.cumsum(run_chunks, axis=1) - run_chunks) * RUN_ALIGN
        used = jnp.sum(run_chunks, axis=1) * RUN_ALIGN
        group_end = jnp.cumsum(run_chunks, axis=0)
        first_chunk = jnp.concatenate([jnp.zeros((1,), jnp.int32), jnp.cumsum(group_end[-1])])
        pos = jnp.arange(n_sub * RUN_SLOTS // RUN_ALIGN, dtype=jnp.int32)
        e_of = jnp.minimum(jnp.sum(pos[:, None] >= first_chunk[None, 1:], axis=1), N_EXPERTS - 1)
        of_e = (e_of[:, None] == jnp.arange(N_EXPERTS, dtype=jnp.int32)[None, :])
        column = lambda t: jnp.dot(of_e.astype(F32), t.T.astype(F32),
                                   precision=lax.Precision.HIGHEST).astype(jnp.int32)
        in_expert = pos - jnp.sum(jnp.where(of_e, first_chunk[None, :-1], 0), axis=1)
        ends = column(group_end)
        g_of = jnp.minimum(jnp.sum(in_expert[:, None] >= ends, axis=1), n_sub - 1)
        of_g = g_of[:, None] == jnp.arange(n_sub, dtype=jnp.int32)[None, :]
        at_g = lambda t: jnp.sum(jnp.where(of_g, t, 0), axis=1)
        in_run = in_expert - at_g(ends - column(run_chunks))
        chunk_rows = g_of * RUN_SLOTS + at_g(column(run_start)) + in_run * RUN_ALIGN

        per_expert = lambda e, *_: (e, 0, 0)
        y_sorted = pl.pallas_call(
            _expert_kernel,
            grid_spec=pltpu.PrefetchScalarGridSpec(
                num_scalar_prefetch=2, grid=(N_EXPERTS,),
                in_specs=[pl.BlockSpec(memory_space=pl.ANY),
                          pl.BlockSpec((1, d, D_EXPERT), per_expert),
                          pl.BlockSpec((1, d, D_EXPERT), per_expert),
                          pl.BlockSpec((1, D_EXPERT, d), per_expert)],
                out_specs=pl.BlockSpec(memory_space=pl.ANY),
                scratch_shapes=[pltpu.VMEM((2, EXPERT_ROWS, d), BF16),
                                pltpu.VMEM((d, 2 * D_EXPERT), BF16),
                                pltpu.VMEM((D_EXPERT, d), BF16),
                                pltpu.SemaphoreType.DMA((4,)),
                                pltpu.SMEM((1,), jnp.int32)]),
            out_shape=jax.ShapeDtypeStruct((n_sub * RUN_SLOTS, d), BF16),
            input_output_aliases={2: 0},
            compiler_params=_params(("arbitrary",)),
            name="moe_experts",
        )(chunk_rows, first_chunk, h_sorted, w_e_gate[l], w_e_up[l], w_e_down[l])

        y_blocks = y_sorted.reshape(n_sub, RUN_SLOTS, d)
        xf = pl.pallas_call(
            _combine_kernel,
            grid_spec=pltpu.PrefetchScalarGridSpec(
                num_scalar_prefetch=1, grid=(n_tok // tm_c,),
                in_specs=[pl.BlockSpec((tm_c, d), row_s), pl.BlockSpec((tm_c, LANES), row_s),
                          _full((1, d)),
                          pl.BlockSpec((tm_c // MOE_SUB, COMBINE_MAIN, d),
                                       lambda i, *_: (i, 0, 0)),
                          pl.BlockSpec(memory_space=pl.ANY)],
                out_specs=pl.BlockSpec((tm_c, d), row_s),
                scratch_shapes=[
                    pltpu.VMEM((tm_c // MOE_SUB, RUN_SLOTS - COMBINE_MAIN, d), BF16),
                    pltpu.SemaphoreType.DMA((tm_c // MOE_SUB,))]),
            out_shape=jax.ShapeDtypeStruct((n_tok, d), F32),
            compiler_params=_params(("parallel",)),
            name="moe_combine",
        )(used, x2, route, final_g[None, :], y_blocks, y_blocks)

    return xf.reshape(bsz, seq, d)
```

```python
import functools

import jax
import jax.numpy as jnp
from jax import lax
from jax.experimental import pallas as pl
from jax.experimental.pallas import tpu as pltpu

F32 = jnp.float32
BF16 = jnp.bfloat16

D_MODEL = 1024
A_WIDTH = 512
A_HEAD = 64
D_DECAY_LORA = 64
D_AAA_LORA = 64
D_GATE_LORA = 160
B_WIDTH = 512
B_GROUPS = 4
B_GROUP_CH = 128
GMLP_BLOCK = 128
N_GROUPS = 4
EXPERTS_PER_GROUP = 8
N_EXPERTS = 32
D_EXPERT = 256
NORM_EPS = 1e-6
LN_EPS = 1e-5
LNX_EPS = 64e-5

LANES = 128
LORA_PAD = 512
XW_OFF, XA_OFF, XG_OFF = 0, 128, 256
A_PROJ = 3 * A_WIDTH + LORA_PAD
WKV_CHUNK = 64
HEAD_PAIR = 2 * A_HEAD
WKV_SEQS = 8
WKV_TOKENS = 64
WKV_PREP_GROUP = 8
WKV_NORM_GROUP = 1
MOE_SUB = 256
RUN_ALIGN = 16
RUN_SLOTS = 2 * MOE_SUB + N_EXPERTS * RUN_ALIGN
EXPERT_ROWS = 2048
EXPERT_TILE = 512
INPROJ_TILE = 512
POST_TILE = 1024
COMBINE_TILE = 1024
DENSE_SPLIT = 256
CHUNK_UNROLL = 8
VMEM_LIMIT = 48 * 1024 * 1024
POST_VMEM_LIMIT = 56 * 1024 * 1024


def _rms(x, g):
    return x * lax.rsqrt(jnp.mean(x * x, axis=-1, keepdims=True) + NORM_EPS) * g


def _dot(a, b):
    return jnp.dot(a, b, preferred_element_type=F32)


def _dot_nt(a, b):
    return lax.dot_general(a, b, (((1,), (1,)), ((), ())), preferred_element_type=F32)


def _split2(x):
    hi = x.astype(BF16)
    lo = (x - hi.astype(F32)).astype(BF16)
    return hi, lo


def _split3(x):
    hi = x.astype(BF16)
    r1 = x - hi.astype(F32)
    mid = r1.astype(BF16)
    lo = (r1 - mid.astype(F32)).astype(BF16)
    return hi, mid, lo


def _pair_head_sums(xs, ones_pair):
    n_tiles = A_WIDTH // HEAD_PAIR
    rows = xs[0].shape[0]
    tiles = [x[:, t * HEAD_PAIR:(t + 1) * HEAD_PAIR].astype(BF16) for x in xs for t in range(n_tiles)]
    sums = _dot(jnp.concatenate(tiles, axis=0), ones_pair)
    return [jnp.concatenate([sums[(i * n_tiles + t) * rows:(i * n_tiles + t + 1) * rows]
                             for t in range(n_tiles)], axis=1) for i in range(len(xs))]


def _w_in_kernel(w_ref, woa_ref, wob_ref, wout_ref, wa_out, wb_out, wg_out, woa_out, wob_out,
                 wout_out):
    woa_out[...] = woa_ref[...].astype(BF16)
    wob_out[...] = wob_ref[...].astype(BF16)
    wout_out[...] = wout_ref[...].astype(BF16)
    s_rkv = 3 * A_WIDTH
    s_w = s_rkv + D_DECAY_LORA
    s_a = s_w + D_AAA_LORA
    a_cols = s_a + D_GATE_LORA
    b_end = a_cols + 2 * B_WIDTH
    wa_out[...] = jnp.zeros_like(wa_out)
    wa_out[:s_w, :] = w_ref[:s_w, :].astype(BF16)
    wa_out[s_rkv + XA_OFF:s_rkv + XA_OFF + D_AAA_LORA, :] = w_ref[s_w:s_a, :].astype(BF16)
    wa_out[s_rkv + XG_OFF:s_rkv + XG_OFF + D_GATE_LORA, :] = w_ref[s_a:a_cols, :].astype(BF16)
    wb_out[...] = w_ref[a_cols:b_end, :].astype(BF16)
    wg_out[...] = w_ref[b_end:, :].astype(BF16)


def _inproj_kernel(tiles_per_seq, x_ref, g1_ref, wa_ref, mu_ref, w0_ref, w2_ref, a0_ref, a2_ref,
                   g2_ref, kk_ref, ka_ref, ones_ref, wb_ref, wg_ref, bg_ref, lng_ref, lnb_ref,
                   ws_ref, bs_ref, wo_ref,
                   r_out, k_out, v_out, a_out, b_out, ld_out, g_out, ybg_out, ga_out, tail_ref):
    i = pl.program_id(0)
    tm = x_ref.shape[0]
    parts = [slice(j * DENSE_SPLIT, (j + 1) * DENSE_SPLIT) for j in range(tm // DENSE_SPLIT)]
    g1 = g1_ref[...]
    h = [_rms(x_ref[p, :], g1).astype(BF16) for p in parts]
    proj = [_dot_nt(hh, wa_ref[...]) for hh in h]
    pb = [_dot_nt(hh, wb_ref[...]) for hh in h]

    prev = [jnp.where(i % tiles_per_seq == 0, 0.0, tail_ref[7:8, :])]
    prev += [p[DENSE_SPLIT - 1:, :] for p in proj[:-1]]
    tail_ref[...] = proj[-1][DENSE_SPLIT - 8:, :]
    row = lax.broadcasted_iota(jnp.int32, proj[0].shape, 0)
    pm = []
    for p, pv in zip(proj, prev):
        shifted = jnp.where(row == 0, pv, pltpu.roll(p, 1, axis=0))
        pm.append(p + mu_ref[...] * (shifted - p))
    gates = [jax.nn.sigmoid(_dot_nt(hh, wg_ref[...]) + bg_ref[...]) for hh in h]

    tri = (lax.broadcasted_iota(jnp.int32, (GMLP_BLOCK, GMLP_BLOCK), 0)
           >= lax.broadcasted_iota(jnp.int32, (GMLP_BLOCK, GMLP_BLOCK), 1))
    ws = [jnp.where(tri, ws_ref[grp], 0.0).astype(BF16) for grp in range(B_GROUPS)]
    bs = bs_ref[...]
    us, vns = [], []
    for x in pb:
        z = 0.5 * x * (1.0 + lax.erf(x * (2.0 ** -0.5)))
        us.append(z[:, :B_WIDTH])
        v = z[:, B_WIDTH:]
        mean = jnp.mean(v, axis=-1, keepdims=True)
        vc = v - mean
        var = jnp.mean(vc * vc, axis=-1, keepdims=True)
        vns.append((vc * lax.rsqrt(var + LN_EPS) * lng_ref[...] + lnb_ref[...]).astype(BF16))

    lora = [x[:, 3 * A_WIDTH:] for x in pm]
    decay = [_dot(jnp.tanh(x[:, XW_OFF:XW_OFF + LANES]).astype(BF16), w2_ref[...]) for x in lora]
    rate = [_dot(x[:, XA_OFF:XA_OFF + LANES].astype(BF16), a2_ref[...]) for x in lora]
    gate = [_dot(jax.nn.sigmoid(x[:, XG_OFF:]).astype(BF16), g2_ref[...]) for x in lora]
    kks = [x[:, A_WIDTH:2 * A_WIDTH] * kk_ref[...] for x in pm]
    sq = [_dot((kk * kk).astype(BF16), ones_ref[...]) for kk in kks]

    svs = []
    for vn in vns:
        rows = []
        for blk in range(DENSE_SPLIT // GMLP_BLOCK):
            cols = [_dot(ws[grp], vn[blk * GMLP_BLOCK:(blk + 1) * GMLP_BLOCK,
                                     grp * B_GROUP_CH:(grp + 1) * B_GROUP_CH])
                    for grp in range(B_GROUPS)]
            rows.append(jnp.concatenate(cols, axis=1) + bs)
        svs.append(jnp.concatenate(rows, axis=0))
    yb = [_dot((u * sv).astype(BF16), wo_ref[...]) for u, sv in zip(us, svs)]

    for j, p in enumerate(parts):
        k = pm[j][:, A_WIDTH:2 * A_WIDTH]
        z = -(w0_ref[...] + decay[j])
        softplus = jnp.maximum(z, 0.0) + jnp.log(1.0 + jnp.exp(-jnp.abs(z)))
        w = -softplus - 0.5
        a_lr = jax.nn.sigmoid(a0_ref[...] + rate[j])
        kk = kks[j] / jnp.maximum(jnp.sqrt(sq[j]), 1e-12)
        r_out[p, :] = pm[j][:, 0:A_WIDTH]
        k_out[p, :] = k * (1.0 + (a_lr - 1.0) * ka_ref[...])
        v_out[p, :] = pm[j][:, 2 * A_WIDTH:3 * A_WIDTH]
        a_out[p, :] = -kk
        b_out[p, :] = kk * a_lr
        ld_out[p, :] = -jnp.exp(w)
        g_out[p, :] = gate[j]
        ga_out[p, :] = gates[j][:, :D_MODEL].astype(BF16)
        ybg_out[p, :] = (gates[j][:, D_MODEL:] * yb[j]).astype(BF16)


def _wkv_kernel(r_ref, k_ref, v_ref, a_ref, b_ref, ld_ref, g_ref, rk_ref, lng_ref, lnb_ref,
                ones_ref, o_ref, st_ref, ta_s, tl_s, arb_s, ark_s, rt_s, bkt_s, dcol_s, y_s):
    C = WKV_CHUNK
    bb, tb, _ = r_ref.shape
    n_chunks = tb // C
    n_pairs = A_WIDTH // HEAD_PAIR

    @pl.when(pl.program_id(1) == 0)
    def _():
        st_ref[...] = jnp.zeros_like(st_ref)

    row = lax.broadcasted_iota(jnp.int32, (C, HEAD_PAIR), 0)
    src = lax.broadcasted_iota(jnp.int32, (C, HEAD_PAIR), 1) & (C - 1)
    incl = src <= row
    strict = src < row
    eye_pair = jnp.where(src == row, 1.0, 0.0)
    bd_mask = ((lax.broadcasted_iota(jnp.int32, (HEAD_PAIR, HEAD_PAIR), 0) >= A_HEAD)
               == (lax.broadcasted_iota(jnp.int32, (HEAD_PAIR, HEAD_PAIR), 1) >= A_HEAD))
    tri_c = (lax.broadcasted_iota(jnp.int32, (C, C), 0)
             >= lax.broadcasted_iota(jnp.int32, (C, C), 1)).astype(BF16)
    ones_pair = ones_ref[...]
    pair_cols = [slice(p * HEAD_PAIR, (p + 1) * HEAD_PAIR) for p in range(n_pairs)]

    def bd(x):
        xb = x.astype(BF16)
        return jnp.where(bd_mask, jnp.concatenate([xb, xb], axis=0), jnp.zeros((), BF16))

    def prep(it, carry):
        where, lhs, rhs, ats = [], [], [], []
        for j in range(WKV_PREP_GROUP):
            flat = it * WKV_PREP_GROUP + j
            b = flat // n_chunks
            c = flat % n_chunks
            sl = pl.ds(pl.multiple_of(c * C, C), C)
            ld = ld_ref[b, sl, :]
            hi, lo = _split2(ld)
            cl = _dot(tri_c, hi) + _dot(tri_c, lo)
            cl_end = cl[C - 1:C, :]
            k = k_ref[b, sl, :]
            bv = b_ref[b, sl, :]
            d_inv = jnp.exp(-cl)
            d_tail = jnp.exp(cl_end - cl)
            rt = (r_ref[b, sl, :] * jnp.exp(cl)).astype(BF16)
            rt_s[b, sl, :] = rt
            kt = k * d_inv
            bt = bv * d_inv
            at = a_ref[b, sl, :] * jnp.exp(cl - ld)
            kd = k * d_tail
            bdk = bv * d_tail
            d_end = jnp.exp(cl_end)
            for p, cs in enumerate(pair_cols):
                bkt_s[b, c, p] = jnp.concatenate([bdk[:, cs], kd[:, cs]], axis=0).T.astype(BF16)
                dcol_s[b, c, p] = jnp.broadcast_to(d_end[:, cs], (HEAD_PAIR, HEAD_PAIR)).T
                where.append((b, sl, cs))
                ats.append(at[:, cs])
                lhs.append(jnp.concatenate([rt[:, cs], at[:, cs].astype(BF16)], axis=0))
                rhs.append(jnp.concatenate([bd(kt[:, cs]), bd(bt[:, cs])], axis=0))
        n = len(where)
        amat = [_dot_nt(lhs[i], rhs[i]) for i in range(n)]
        l_ak, l_ab = [], []
        for i, (b, sl, cs) in enumerate(where):
            ark_s[b, sl, cs] = jnp.where(incl, amat[i][:C, :HEAD_PAIR], 0.0).astype(BF16)
            arb_s[b, sl, cs] = jnp.where(incl, amat[i][:C, HEAD_PAIR:], 0.0).astype(BF16)
            l_ak.append(jnp.where(strict, amat[i][C:, :HEAD_PAIR], 0.0))
            l_ab.append(jnp.where(strict, amat[i][C:, HEAD_PAIR:], 0.0))

        t_mat = [eye_pair + l for l in l_ab]
        q = [_dot(l.astype(BF16), bd(l)) for l in l_ab]
        n_sq = 1
        while 2 * n_sq < C // 2:
            both = [_dot(q[i].astype(BF16), jnp.concatenate([bd(q[i]), bd(t_mat[i])], axis=1))
                    for i in range(n)]
            q = [x[:, :HEAD_PAIR] for x in both]
            t_mat = [t_mat[i] + both[i][:, HEAD_PAIR:] for i in range(n)]
            n_sq *= 2
        t_mat = [t_mat[i] + _dot(q[i].astype(BF16), bd(t_mat[i])) for i in range(n)]
        tal = [_dot(t_mat[i].astype(BF16), jnp.concatenate([bd(ats[i]), bd(l_ak[i])], axis=1))
               for i in range(n)]
        for i, (b, sl, cs) in enumerate(where):
            ta_s[b, sl, cs] = tal[i][:, :HEAD_PAIR].astype(BF16)
            tl_s[b, sl, cs] = tal[i][:, HEAD_PAIR:].astype(BF16)
        return carry

    lax.fori_loop(0, bb * n_chunks // WKV_PREP_GROUP, prep, 0)

    def step(c, carry):
        sl = pl.ds(pl.multiple_of(c * C, C), C)
        chains = [(b, p) for b in range(bb) for p in range(n_pairs)]
        v = [v_ref[b, sl, :] for b in range(bb)]
        st = [st_ref[b, p] for b, p in chains]
        st_b = [x.astype(BF16) for x in st]
        bd_v = [bd(v[b][:, pair_cols[p]]) for b, p in chains]
        on_st = [_dot(jnp.concatenate([ta_s[b, sl, pair_cols[p]], rt_s[b, sl, pair_cols[p]]],
                                      axis=0), st_b[i]) for i, (b, p) in enumerate(chains)]
        on_v = [_dot(jnp.concatenate([tl_s[b, sl, pair_cols[p]], ark_s[b, sl, pair_cols[p]]],
                                     axis=0), bd_v[i]) for i, (b, p) in enumerate(chains)]
        u = [on_st[i][:C] + on_v[i][:C] for i in range(len(chains))]
        for i, (b, p) in enumerate(chains):
            uv = jnp.concatenate([u[i], v[b][:, pair_cols[p]]], axis=0).astype(BF16)
            st_ref[b, p] = (dcol_s[b, c, p] * st[i]
                            + jnp.where(bd_mask, _dot(bkt_s[b, c, p], uv), 0.0))
        ys = [on_st[i][C:] + on_v[i][C:] + _dot(arb_s[b, sl, pair_cols[p]], bd(u[i]))
              for i, (b, p) in enumerate(chains)]
        for b in range(bb):
            y_s[b, sl, :] = jnp.concatenate(ys[b * n_pairs:(b + 1) * n_pairs], axis=1)
        return carry

    lax.fori_loop(0, n_chunks, step, 0)

    def finish(it, carry):
        items = [(b, pl.ds(pl.multiple_of((it * WKV_NORM_GROUP + j) * C, C), C))
                 for j in range(WKV_NORM_GROUP) for b in range(bb)]
        y = [y_s[b, sl, :] for b, sl in items]
        rkr = [r_ref[b, sl, :] * k_ref[b, sl, :] * rk_ref[...] for b, sl in items]
        sums = [_pair_head_sums([y[i], rkr[i]], ones_pair) for i in range(len(items))]
        yc = [y[i] - sums[i][0] * (1.0 / A_HEAD) for i in range(len(items))]
        var = [_pair_head_sums([x * x], ones_pair)[0] * (1.0 / A_HEAD) for x in yc]
        for i, (b, sl) in enumerate(items):
            yn = yc[i] * lax.rsqrt(var[i] + LNX_EPS) * lng_ref[...] + lnb_ref[...]
            out = (yn + sums[i][1] * v_ref[b, sl, :]) * g_ref[b, sl, :]
            o_ref[b, sl, :] = out.astype(o_ref.dtype)
        return carry

    lax.fori_loop(0, n_chunks // WKV_NORM_GROUP, finish, 0)


def _post_kernel(x_ref, ya_ref, ga_ref, ybg_ref, woa_ref, wout_ref, g2_ref, wrt_ref, brt_ref,
                 x2_out, hl_out, route_out, cnt_out):
    n_sub = x_ref.shape[0] // MOE_SUB
    groups = [slice(g * MOE_SUB, (g + 1) * MOE_SUB) for g in range(n_sub)]
    y_a = [_dot(ya_ref[p, :].astype(BF16), woa_ref[...]) for p in groups]
    x2 = [x_ref[p, :] + _dot((ga_ref[p, :] * y + ybg_ref[p, :]).astype(BF16), wout_ref[...])
          for p, y in zip(groups, y_a)]
    h2 = [_rms(x, g2_ref[...]) for x in x2]
    split = [_split2(h) for h in h2]
    on_hi = [_dot_nt(wrt_ref[...], hi) for hi, _ in split]
    on_lo = [_dot_nt(wrt_ref[0:LANES, :], lo) for _, lo in split]
    row = lax.broadcasted_iota(jnp.int32, (LANES, MOE_SUB), 0)
    neg = jnp.float32(-jnp.inf)
    big = jnp.int32(LANES)
    is_grp = (row >= N_EXPERTS) & (row < N_EXPERTS + N_GROUPS)
    top = lambda v: jnp.max(v, axis=0, keepdims=True)
    first = lambda hit: jnp.min(jnp.where(hit, row, big), axis=0, keepdims=True)
    picks, onehots, weights = [], [], []
    for g, p in enumerate(groups):
        x2_out[p, :] = x2[g].astype(x2_out.dtype)
        logits = on_hi[g][:LANES, :] + on_hi[g][LANES:, :] + on_lo[g] + brt_ref[...]
        gl = jnp.where(is_grp, logits, neg)
        gmax = top(gl)
        g_p = 1.0 / jnp.sum(jnp.exp(gl - gmax), axis=0, keepdims=True)
        lo_row = (first(gl == gmax) - N_EXPERTS) * EXPERTS_PER_GROUP
        el = jnp.where((row >= lo_row) & (row < lo_row + EXPERTS_PER_GROUP), logits, neg)
        e1 = top(el)
        i1 = first(el == e1)
        el2 = jnp.where(row == i1, neg, el)
        e2 = top(el2)
        i2 = first(el2 == e2)
        t = jnp.exp(e2 - e1)
        weights.append((g_p / (1.0 + t), g_p * t / (1.0 + t)))
        picks.append((i1, i2))
        onehots.append(jnp.where((row == i1) | (row == i2), 1.0, 0.0))

    earlier = (lax.broadcasted_iota(jnp.int32, (MOE_SUB, MOE_SUB), 0)
               <= lax.broadcasted_iota(jnp.int32, (MOE_SUB, MOE_SUB), 1)).astype(BF16)
    incl = [_dot(oh.astype(BF16), earlier) for oh in onehots]
    chunks = [jnp.ceil(x[:, MOE_SUB - 1:] * (1.0 / RUN_ALIGN)) for x in incl]
    below = (lax.broadcasted_iota(jnp.int32, (LANES, LANES), 1)
             < lax.broadcasted_iota(jnp.int32, (LANES, LANES), 0)).astype(BF16)
    wide = [jnp.broadcast_to(c, (LANES, LANES)) for c in chunks]
    run_start = [_dot(below, w.astype(BF16))[:, 0:1] * RUN_ALIGN for w in wide]
    row8 = lax.broadcasted_iota(jnp.int32, (8, LANES), 0)
    counts = jnp.zeros((8, LANES), F32)
    slot_ids = lax.broadcasted_iota(jnp.int32, (RUN_SLOTS, MOE_SUB), 0)
    for g, p in enumerate(groups):
        counts = jnp.where(row8 == g, wide[g].T[0:8, :], counts)
        slot_of = run_start[g] + incl[g] - onehots[g]
        i1, i2 = picks[g]
        slot1 = jnp.sum(jnp.where(row == i1, slot_of, 0.0), axis=0, keepdims=True)
        slot2 = jnp.sum(jnp.where(row == i2, slot_of, 0.0), axis=0, keepdims=True)
        w1, w2 = weights[g]
        packed = jnp.where(row == 0, slot1, jnp.where(row == 1, slot2, jnp.where(
            row == 2, w1, jnp.where(row == 3, w2, 0.0))))
        route_out[p, :] = packed.T
        pick = jnp.where((slot_ids == slot1.astype(jnp.int32))
                         | (slot_ids == slot2.astype(jnp.int32)), 1.0, 0.0).astype(BF16)
        hl_out[g * RUN_SLOTS:(g + 1) * RUN_SLOTS, :] = _dot(pick, h2[g].astype(BF16)).astype(BF16)
    cnt_out[...] = counts


def _expert_kernel(chunk_s, first_s, hl_in_ref, wg_ref, wu_ref, wd_ref, hl_ref, xbuf, wgu_s, wd_s,
                   sem, drain_s):
    del hl_in_ref
    e = pl.program_id(0)
    pass_chunks = EXPERT_ROWS // RUN_ALIGN
    mine = e % 2
    other = 1 - mine
    gather_sem = lambda buf: sem.at[buf]
    scatter_sem = lambda buf: sem.at[2 + buf]

    def first_pass(ex):
        lo = first_s[ex]
        return lo, jnp.minimum(lo + pass_chunks, first_s[ex + 1])

    def for_chunks(buf, c_lo, c_hi, fn):
        def one(c):
            fn(hl_ref.at[pl.ds(pl.multiple_of(chunk_s[c], RUN_ALIGN), RUN_ALIGN)],
               xbuf.at[buf, pl.ds(pl.multiple_of((c - c_lo) * RUN_ALIGN, RUN_ALIGN), RUN_ALIGN)])

        def block(i, carry):
            for u in range(CHUNK_UNROLL):
                one(c_lo + i * CHUNK_UNROLL + u)
            return carry

        def tail(c, carry):
            one(c)
            return carry

        n_blocks = (c_hi - c_lo) // CHUNK_UNROLL
        lax.fori_loop(0, n_blocks, block, 0)
        lax.fori_loop(c_lo + n_blocks * CHUNK_UNROLL, c_hi, tail, 0)

    def gather(buf, c_lo, c_hi):
        for_chunks(buf, c_lo, c_hi, lambda hbm, vm: pltpu.make_async_copy(
            hbm, vm, gather_sem(buf)).start())

    def scatter(buf, c_lo, c_hi):
        for_chunks(buf, c_lo, c_hi, lambda hbm, vm: pltpu.make_async_copy(
            vm, hbm, scatter_sem(buf)).start())

    def wait_chunks(which_sem, n):
        def wait_rows(rows):
            def body(c, carry):
                pltpu.make_async_copy(hl_ref.at[pl.ds(0, rows)], xbuf.at[0, pl.ds(0, rows)],
                                      which_sem).wait()
                return carry
            return body

        lax.fori_loop(0, n // CHUNK_UNROLL, wait_rows(CHUNK_UNROLL * RUN_ALIGN), 0)
        lax.fori_loop(0, n % CHUNK_UNROLL, wait_rows(RUN_ALIGN), 0)

    def evaluate(buf, n_chunks):
        def tile(k, carry):
            sl = pl.ds(pl.multiple_of(k * EXPERT_TILE, EXPERT_TILE), EXPERT_TILE)
            gu = _dot(xbuf[buf, sl, :], wgu_s[...])
            gate = gu[:, :D_EXPERT]
            act = gate * jax.nn.sigmoid(gate) * gu[:, D_EXPERT:]
            xbuf[buf, sl, :] = _dot(act.astype(BF16), wd_s[...]).astype(BF16)
            return carry

        lax.fori_loop(0, (n_chunks * RUN_ALIGN + EXPERT_TILE - 1) // EXPERT_TILE, tile, 0)

    @pl.when(e == 0)
    def _():
        xbuf[...] = jnp.zeros_like(xbuf)
        gather(0, *first_pass(0))
        drain_s[0] = 0

    wgu_s[:, :D_EXPERT] = wg_ref[0].astype(BF16)
    wgu_s[:, D_EXPERT:] = wu_ref[0].astype(BF16)
    wd_s[...] = wd_ref[0].astype(BF16)

    c_lo, c_hi = first_pass(e)
    wait_chunks(gather_sem(mine), c_hi - c_lo)
    wait_chunks(scatter_sem(other), drain_s[0])

    @pl.when(e + 1 < pl.num_programs(0))
    def _():
        gather(other, *first_pass(e + 1))

    evaluate(mine, c_hi - c_lo)
    scatter(mine, c_lo, c_hi)

    def later_pass(state):
        lo, pending = state
        hi = jnp.minimum(lo + pass_chunks, first_s[e + 1])
        wait_chunks(scatter_sem(mine), pending)
        gather(mine, lo, hi)
        wait_chunks(gather_sem(mine), hi - lo)
        evaluate(mine, hi - lo)
        scatter(mine, lo, hi)
        return hi, hi - lo

    _, pending = lax.while_loop(lambda st: st[0] < first_s[e + 1], later_pass,
                                (c_hi, c_hi - c_lo))
    drain_s[0] = pending

    @pl.when(e + 1 == pl.num_programs(0))
    def _():
        wait_chunks(scatter_sem(mine), pending)


def _combine_kernel(x2_ref, route_ref, gf_ref, yl_ref, o_ref):
    slots = lax.broadcasted_iota(jnp.int32, (MOE_SUB, RUN_SLOTS), 1).astype(F32)
    for g in range(x2_ref.shape[0] // MOE_SUB):
        p = slice(g * MOE_SUB, (g + 1) * MOE_SUB)
        route = route_ref[p, :]
        pick = (jnp.where(slots == route[:, 0:1], route[:, 2:3], 0.0)
                + jnp.where(slots == route[:, 1:2], route[:, 3:4], 0.0))
        y = _dot(pick.astype(BF16), yl_ref[g * RUN_SLOTS:(g + 1) * RUN_SLOTS, :])
        o_ref[p, :] = _rms(x2_ref[p, :] + y, gf_ref[...])


def _full(shape):
    return pl.BlockSpec(shape, lambda *_: (0,) * len(shape))


def _params(sem, vmem_limit=VMEM_LIMIT):
    return pltpu.CompilerParams(dimension_semantics=sem, vmem_limit_bytes=vmem_limit)


def _mixer_a_layout(t):
    s_rkv = 3 * A_WIDTH
    s_w = s_rkv + D_DECAY_LORA
    s_a = s_w + D_AAA_LORA
    gap = lambda n: jnp.zeros((t.shape[0], n), t.dtype)
    return jnp.concatenate(
        [t[:, :s_w], gap(XA_OFF - D_DECAY_LORA), t[:, s_w:s_a], gap(XG_OFF - XA_OFF - D_AAA_LORA),
         t[:, s_a:], gap(LORA_PAD - XG_OFF - D_GATE_LORA)], axis=1)


def _place(cols, parts):
    out = jnp.zeros((parts[0][1].shape[0], cols), parts[0][1].dtype)
    for off, arr in parts:
        out = lax.dynamic_update_slice(out, arr, (0, off))
    return out


def kernel(x, norm1_g, w_in, b_gate, tmix_mu, w0, w2, a0, a2, g2, k_k, k_a, r_k, lnx_g, lnx_b,
           w_oA, lnv_g, lnv_b, w_s, b_s, w_oB, w_out, norm2_g, w_rg, b_rg, w_re, b_re,
           w_e_gate, w_e_up, w_e_down, final_g):
    bsz, seq, d = x.shape
    n_tok = bsz * seq
    depth = norm1_g.shape[0]
    assert depth == 1, "the moe kernel fuses the final norm, so it must be the last layer"
    assert bsz % WKV_SEQS == 0 and seq % WKV_TOKENS == 0
    xf = x.reshape(n_tok, d)

    s_rkv = 3 * A_WIDTH
    s_w = s_rkv + D_DECAY_LORA
    s_a = s_w + D_AAA_LORA
    a_cols = s_a + D_GATE_LORA
    b_cols = 2 * B_WIDTH

    ones_bd = (jnp.arange(A_WIDTH)[:, None] // A_HEAD
               == jnp.arange(A_WIDTH)[None, :] // A_HEAD).astype(BF16)

    tm_a = INPROJ_TILE
    tm_p = POST_TILE
    tm_c = COMBINE_TILE
    assert tm_p % MOE_SUB == 0 and tm_c % MOE_SUB == 0 and tm_a % DENSE_SPLIT == 0
    assert seq % tm_a == 0, "an inproj tile must not straddle two sequences (token shift)"

    for l in range(depth):
        wl = w_in[l]
        n_w = 4
        col_blk = lambda rows: pl.BlockSpec((rows, d // n_w), lambda i: (0, i))
        row_blk = lambda rows: pl.BlockSpec((rows // n_w, d), lambda i: (i, 0))
        w_a, w_b, w_g, w_oa, w_ob, w_o = pl.pallas_call(
            _w_in_kernel,
            grid=(n_w,),
            in_specs=[col_blk(wl.shape[1]), row_blk(A_WIDTH), row_blk(B_WIDTH), row_blk(d)],
            out_specs=[col_blk(A_PROJ), col_blk(b_cols), col_blk(2 * d),
                       row_blk(A_WIDTH), row_blk(B_WIDTH), row_blk(d)],
            out_shape=[jax.ShapeDtypeStruct((A_PROJ, d), BF16),
                       jax.ShapeDtypeStruct((b_cols, d), BF16),
                       jax.ShapeDtypeStruct((2 * d, d), BF16),
                       jax.ShapeDtypeStruct((A_WIDTH, d), BF16),
                       jax.ShapeDtypeStruct((B_WIDTH, d), BF16),
                       jax.ShapeDtypeStruct((d, d), BF16)],
            compiler_params=_params(("parallel",)),
            name="w_in_layout",
        )(wl.T, w_oA[l], w_oB[l], w_out[l])
        mu_a = _mixer_a_layout(tmix_mu[l][None, :])
        w2p = jnp.pad(w2[l], ((0, LANES - D_DECAY_LORA), (0, 0))).astype(BF16)
        a2p = jnp.pad(a2[l], ((0, LANES - D_AAA_LORA), (0, 0))).astype(BF16)
        g2p = jnp.pad(g2[l], ((0, LORA_PAD - XG_OFF - D_GATE_LORA), (0, 0))).astype(BF16)
        g1 = norm1_g[l][None, :]

        row512 = lambda i: (i, 0)
        tok_a = pl.BlockSpec((tm_a, A_WIDTH), row512)
        tok_d = pl.BlockSpec((tm_a, d), row512)
        vec_a = _full((1, A_WIDTH))
        bs_full = jnp.repeat(b_s[l].T, B_GROUP_CH, axis=1)
        outs = pl.pallas_call(
            functools.partial(_inproj_kernel, seq // tm_a),
            grid=(n_tok // tm_a,),
            in_specs=[tok_d,
                      _full((1, d)), _full((A_PROJ, d)), _full((1, A_PROJ)), vec_a,
                      _full((LANES, A_WIDTH)), vec_a, _full((LANES, A_WIDTH)),
                      _full((LORA_PAD - XG_OFF, A_WIDTH)), vec_a, vec_a,
                      _full((A_WIDTH, A_WIDTH)),
                      _full((b_cols, d)), _full((2 * d, d)), _full((1, 2 * d)),
                      _full((1, B_WIDTH)), _full((1, B_WIDTH)),
                      _full((B_GROUPS, GMLP_BLOCK, GMLP_BLOCK)), _full((GMLP_BLOCK, B_WIDTH)),
                      _full((B_WIDTH, d))],
            out_specs=[tok_a] * 7 + [tok_d, tok_d],
            out_shape=[jax.ShapeDtypeStruct((n_tok, A_WIDTH), F32)] * 7
            + [jax.ShapeDtypeStruct((n_tok, d), BF16)] * 2,
            scratch_shapes=[pltpu.VMEM((8, A_PROJ), F32)],
            compiler_params=_params(("arbitrary",)),
            name="inproj",
        )(xf, g1, w_a, mu_a, w0[l][None, :], w2p, a0[l][None, :], a2p, g2p,
          k_k[l][None, :], k_a[l][None, :], ones_bd, w_b, w_g, b_gate[l][None, :],
          lnv_g[l][None, :], lnv_b[l][None, :], w_s[l], bs_full, w_ob)
        r_, k_, v_, a_, b_, ld_, gg_, ybg, ga = outs

        n_pairs = A_WIDTH // HEAD_PAIR
        n_chunks = WKV_TOKENS // WKV_CHUNK
        tok_w = pl.BlockSpec((WKV_SEQS, WKV_TOKENS, A_WIDTH), lambda bi, ti: (bi, ti, 0))
        vec_w = _full((1, A_WIDTH))
        seq_major = lambda t: t.reshape(bsz, seq, A_WIDTH)
        tok_scratch = pltpu.VMEM((WKV_SEQS, WKV_TOKENS, A_WIDTH), BF16)
        ya_in = pl.pallas_call(
            _wkv_kernel,
            grid=(bsz // WKV_SEQS, seq // WKV_TOKENS),
            in_specs=[tok_w] * 7 + [vec_w, vec_w, vec_w, _full((HEAD_PAIR, HEAD_PAIR))],
            out_specs=tok_w,
            out_shape=jax.ShapeDtypeStruct((bsz, seq, A_WIDTH), BF16),
            scratch_shapes=[pltpu.VMEM((WKV_SEQS, n_pairs, HEAD_PAIR, HEAD_PAIR), F32)]
            + [tok_scratch] * 5
            + [pltpu.VMEM((WKV_SEQS, n_chunks, n_pairs, HEAD_PAIR, HEAD_PAIR), BF16),
               pltpu.VMEM((WKV_SEQS, n_chunks, n_pairs, HEAD_PAIR, HEAD_PAIR), F32),
               pltpu.VMEM((WKV_SEQS, WKV_TOKENS, A_WIDTH), F32)],
            compiler_params=_params(("parallel", "arbitrary")),
            name="wkv",
        )(*[seq_major(t) for t in (r_, k_, v_, a_, b_, ld_, gg_)], r_k[l].reshape(1, A_WIDTH),
          lnx_g[l][None, :], lnx_b[l][None, :],
          ones_bd[:HEAD_PAIR, :HEAD_PAIR]).reshape(n_tok, A_WIDTH)

        w_r = _place(LANES, [(0, jnp.transpose(w_re[l], (1, 0, 2)).reshape(d, N_EXPERTS)),
                             (N_EXPERTS, w_rg[l])])
        b_r = _place(LANES, [(0, b_re[l].reshape(1, N_EXPERTS)), (N_EXPERTS, b_rg[l][None, :])])
        wr_hi = w_r.astype(BF16)
        wr_lo = (w_r - wr_hi.astype(F32)).astype(BF16)
        wr_t = jnp.concatenate([wr_hi.T, wr_lo.T], axis=0)
        tok_p = pl.BlockSpec((tm_p, d), row512)
        lane_p = pl.BlockSpec((tm_p, LANES), row512)
        n_sub = n_tok // MOE_SUB
        sorted_rows = tm_p // MOE_SUB * RUN_SLOTS
        x2, h_sorted, route, cnt = pl.pallas_call(
            _post_kernel,
            grid=(n_tok // tm_p,),
            in_specs=[tok_p, pl.BlockSpec((tm_p, A_WIDTH), row512), tok_p, tok_p,
                      _full((A_WIDTH, d)), _full((d, d)), _full((1, d)), _full((2 * LANES, d)),
                      _full((LANES, 1))],
            out_specs=[tok_p, pl.BlockSpec((sorted_rows, d), row512), lane_p,
                       pl.BlockSpec((8, LANES), row512)],
            out_shape=[jax.ShapeDtypeStruct((n_tok, d), F32),
                       jax.ShapeDtypeStruct((n_sub * RUN_SLOTS, d), BF16),
                       jax.ShapeDtypeStruct((n_tok, LANES), F32),
                       jax.ShapeDtypeStruct((n_tok // tm_p * 8, LANES), F32)],
            compiler_params=_params(("parallel",), POST_VMEM_LIMIT),
            name="post",
        )(xf, ya_in, ga, ybg, w_oa, w_o,
          norm2_g[l][None, :], wr_t, b_r.T)

        run_chunks = cnt.reshape(n_tok // tm_p, 8, LANES)[:, :tm_p // MOE_SUB, :N_EXPERTS]
        run_chunks = run_chunks.reshape(n_sub, N_EXPERTS).astype(jnp.int32)
        run_start = (jnp.cumsum(run_chunks, axis=1) - run_chunks) * RUN_ALIGN
        group_end = jnp.cumsum(run_chunks, axis=0)
        first_chunk = jnp.concatenate([jnp.zeros((1,), jnp.int32), jnp.cumsum(group_end[-1])])
        pos = jnp.arange(n_sub * RUN_SLOTS // RUN_ALIGN, dtype=jnp.int32)
        e_of = jnp.minimum(jnp.sum(pos[:, None] >= first_chunk[None, 1:], axis=1), N_EXPERTS - 1)
        of_e = (e_of[:, None] == jnp.arange(N_EXPERTS, dtype=jnp.int32)[None, :])
        column = lambda t: jnp.dot(of_e.astype(F32), t.T.astype(F32),
                                   precision=lax.Precision.HIGHEST).astype(jnp.int32)
        in_expert = pos - jnp.sum(jnp.where(of_e, first_chunk[None, :-1], 0), axis=1)
        ends = column(group_end)
        g_of = jnp.minimum(jnp.sum(in_expert[:, None] >= ends, axis=1), n_sub - 1)
        of_g = g_of[:, None] == jnp.arange(n_sub, dtype=jnp.int32)[None, :]
        at_g = lambda t: jnp.sum(jnp.where(of_g, t, 0), axis=1)
        in_run = in_expert - at_g(ends - column(run_chunks))
        chunk_rows = g_of * RUN_SLOTS + at_g(column(run_start)) + in_run * RUN_ALIGN

        per_expert = lambda e, *_: (e, 0, 0)
        y_sorted = pl.pallas_call(
            _expert_kernel,
            grid_spec=pltpu.PrefetchScalarGridSpec(
                num_scalar_prefetch=2, grid=(N_EXPERTS,),
                in_specs=[pl.BlockSpec(memory_space=pl.ANY),
                          pl.BlockSpec((1, d, D_EXPERT), per_expert),
                          pl.BlockSpec((1, d, D_EXPERT), per_expert),
                          pl.BlockSpec((1, D_EXPERT, d), per_expert)],
                out_specs=pl.BlockSpec(memory_space=pl.ANY),
                scratch_shapes=[pltpu.VMEM((2, EXPERT_ROWS, d), BF16),
                                pltpu.VMEM((d, 2 * D_EXPERT), BF16),
                                pltpu.VMEM((D_EXPERT, d), BF16),
                                pltpu.SemaphoreType.DMA((4,)),
                                pltpu.SMEM((1,), jnp.int32)]),
            out_shape=jax.ShapeDtypeStruct((n_sub * RUN_SLOTS, d), BF16),
            input_output_aliases={2: 0},
            compiler_params=_params(("arbitrary",)),
            name="moe_experts",
        )(chunk_rows, first_chunk, h_sorted, w_e_gate[l], w_e_up[l], w_e_down[l])

        xf = pl.pallas_call(
            _combine_kernel,
            grid=(n_tok // tm_c,),
            in_specs=[pl.BlockSpec((tm_c, d), row512), pl.BlockSpec((tm_c, LANES), row512),
                      _full((1, d)), pl.BlockSpec((tm_c // MOE_SUB * RUN_SLOTS, d), row512)],
            out_specs=pl.BlockSpec((tm_c, d), row512),
            out_shape=jax.ShapeDtypeStruct((n_tok, d), F32),
            compiler_params=_params(("parallel",)),
            name="moe_combine",
        )(x2, route, final_g[None, :], y_sorted)

    return xf.reshape(bsz, seq, d)
```

```python
import functools

import jax
import jax.numpy as jnp
from jax import lax
from jax.experimental import pallas as pl
from jax.experimental.pallas import tpu as pltpu

F32 = jnp.float32
BF16 = jnp.bfloat16

D_MODEL = 1024
A_WIDTH = 512
A_HEAD = 64
D_DECAY_LORA = 64
D_AAA_LORA = 64
D_GATE_LORA = 160
B_WIDTH = 512
B_GROUPS = 4
B_GROUP_CH = 128
GMLP_BLOCK = 128
N_GROUPS = 4
EXPERTS_PER_GROUP = 8
N_EXPERTS = 32
D_EXPERT = 256
NORM_EPS = 1e-6
LN_EPS = 1e-5
LNX_EPS = 64e-5

LANES = 128
LORA_PAD = 512
XW_OFF, XA_OFF, XG_OFF = 0, 128, 256
A_PROJ = 3 * A_WIDTH + LORA_PAD
WKV_CHUNK = 64
HEAD_PAIR = 2 * A_HEAD
WKV_SEQS = 8
WKV_TOKENS = 64
WKV_PREP_GROUP = 8
WKV_NORM_GROUP = 1
MOE_SUB = 256
RUN_ALIGN = 16
RUN_SLOTS = 2 * MOE_SUB + N_EXPERTS * RUN_ALIGN
EXPERT_ROWS = 2048
EXPERT_TILE = 512
INPROJ_TILE = 512
POST_TILE = 1024
COMBINE_TILE = 1024
DENSE_SPLIT = 256
CHUNK_UNROLL = 8
VMEM_LIMIT = 48 * 1024 * 1024
POST_VMEM_LIMIT = 56 * 1024 * 1024


def _rms(x, g):
    return x * lax.rsqrt(jnp.mean(x * x, axis=-1, keepdims=True) + NORM_EPS) * g


def _dot(a, b):
    return jnp.dot(a, b, preferred_element_type=F32)


def _dot_nt(a, b):
    return lax.dot_general(a, b, (((1,), (1,)), ((), ())), preferred_element_type=F32)


def _split2(x):
    hi = x.astype(BF16)
    lo = (x - hi.astype(F32)).astype(BF16)
    return hi, lo


def _split3(x):
    hi = x.astype(BF16)
    r1 = x - hi.astype(F32)
    mid = r1.astype(BF16)
    lo = (r1 - mid.astype(F32)).astype(BF16)
    return hi, mid, lo


def _pair_head_sums(xs, ones_pair):
    n_tiles = A_WIDTH // HEAD_PAIR
    rows = xs[0].shape[0]
    tiles = [x[:, t * HEAD_PAIR:(t + 1) * HEAD_PAIR].astype(BF16) for x in xs for t in range(n_tiles)]
    sums = _dot(jnp.concatenate(tiles, axis=0), ones_pair)
    return [jnp.concatenate([sums[(i * n_tiles + t) * rows:(i * n_tiles + t + 1) * rows]
                             for t in range(n_tiles)], axis=1) for i in range(len(xs))]


def _w_in_kernel(w_ref, woa_ref, wob_ref, wout_ref, wa_out, wb_out, wg_out, woa_out, wob_out,
                 wout_out):
    woa_out[...] = woa_ref[...].astype(BF16)
    wob_out[...] = wob_ref[...].astype(BF16)
    wout_out[...] = wout_ref[...].astype(BF16)
    s_rkv = 3 * A_WIDTH
    s_w = s_rkv + D_DECAY_LORA
    s_a = s_w + D_AAA_LORA
    a_cols = s_a + D_GATE_LORA
    b_end = a_cols + 2 * B_WIDTH
    wa_out[...] = jnp.zeros_like(wa_out)
    wa_out[:s_w, :] = w_ref[:s_w, :].astype(BF16)
    wa_out[s_rkv + XA_OFF:s_rkv + XA_OFF + D_AAA_LORA, :] = w_ref[s_w:s_a, :].astype(BF16)
    wa_out[s_rkv + XG_OFF:s_rkv + XG_OFF + D_GATE_LORA, :] = w_ref[s_a:a_cols, :].astype(BF16)
    wb_out[...] = w_ref[a_cols:b_end, :].astype(BF16)
    wg_out[...] = w_ref[b_end:, :].astype(BF16)


def _inproj_kernel(tiles_per_seq, x_ref, g1_ref, wa_ref, mu_ref, w0_ref, w2_ref, a0_ref, a2_ref,
                   g2_ref, kk_ref, ka_ref, ones_ref, wb_ref, wg_ref, bg_ref, lng_ref, lnb_ref,
                   ws_ref, bs_ref, wo_ref,
                   r_out, k_out, v_out, a_out, b_out, ld_out, g_out, ybg_out, ga_out, tail_ref):
    i = pl.program_id(0)
    tm = x_ref.shape[0]
    parts = [slice(j * DENSE_SPLIT, (j + 1) * DENSE_SPLIT) for j in range(tm // DENSE_SPLIT)]
    g1 = g1_ref[...]
    h = [_rms(x_ref[p, :], g1).astype(BF16) for p in parts]
    proj = [_dot_nt(hh, wa_ref[...]) for hh in h]
    pb = [_dot_nt(hh, wb_ref[...]) for hh in h]

    prev = [jnp.where(i % tiles_per_seq == 0, 0.0, tail_ref[7:8, :])]
    prev += [p[DENSE_SPLIT - 1:, :] for p in proj[:-1]]
    tail_ref[...] = proj[-1][DENSE_SPLIT - 8:, :]
    row = lax.broadcasted_iota(jnp.int32, proj[0].shape, 0)
    pm = []
    for p, pv in zip(proj, prev):
        shifted = jnp.where(row == 0, pv, pltpu.roll(p, 1, axis=0))
        pm.append(p + mu_ref[...] * (shifted - p))
    gates = [jax.nn.sigmoid(_dot_nt(hh, wg_ref[...]) + bg_ref[...]) for hh in h]

    tri = (lax.broadcasted_iota(jnp.int32, (GMLP_BLOCK, GMLP_BLOCK), 0)
           >= lax.broadcasted_iota(jnp.int32, (GMLP_BLOCK, GMLP_BLOCK), 1))
    ws = [jnp.where(tri, ws_ref[grp], 0.0).astype(BF16) for grp in range(B_GROUPS)]
    bs = bs_ref[...]
    us, vns = [], []
    for x in pb:
        z = 0.5 * x * (1.0 + lax.erf(x * (2.0 ** -0.5)))
        us.append(z[:, :B_WIDTH])
        v = z[:, B_WIDTH:]
        mean = jnp.mean(v, axis=-1, keepdims=True)
        vc = v - mean
        var = jnp.mean(vc * vc, axis=-1, keepdims=True)
        vns.append((vc * lax.rsqrt(var + LN_EPS) * lng_ref[...] + lnb_ref[...]).astype(BF16))

    lora = [x[:, 3 * A_WIDTH:] for x in pm]
    decay = [_dot(jnp.tanh(x[:, XW_OFF:XW_OFF + LANES]).astype(BF16), w2_ref[...]) for x in lora]
    rate = [_dot(x[:, XA_OFF:XA_OFF + LANES].astype(BF16), a2_ref[...]) for x in lora]
    gate = [_dot(jax.nn.sigmoid(x[:, XG_OFF:]).astype(BF16), g2_ref[...]) for x in lora]
    kks = [x[:, A_WIDTH:2 * A_WIDTH] * kk_ref[...] for x in pm]
    sq = [_dot((kk * kk).astype(BF16), ones_ref[...]) for kk in kks]

    svs = []
    for vn in vns:
        rows = []
        for blk in range(DENSE_SPLIT // GMLP_BLOCK):
            cols = [_dot(ws[grp], vn[blk * GMLP_BLOCK:(blk + 1) * GMLP_BLOCK,
                                     grp * B_GROUP_CH:(grp + 1) * B_GROUP_CH])
                    for grp in range(B_GROUPS)]
            rows.append(jnp.concatenate(cols, axis=1) + bs)
        svs.append(jnp.concatenate(rows, axis=0))
    yb = [_dot((u * sv).astype(BF16), wo_ref[...]) for u, sv in zip(us, svs)]

    for j, p in enumerate(parts):
        k = pm[j][:, A_WIDTH:2 * A_WIDTH]
        z = -(w0_ref[...] + decay[j])
        softplus = jnp.maximum(z, 0.0) + jnp.log(1.0 + jnp.exp(-jnp.abs(z)))
        w = -softplus - 0.5
        a_lr = jax.nn.sigmoid(a0_ref[...] + rate[j])
        kk = kks[j] / jnp.maximum(jnp.sqrt(sq[j]), 1e-12)
        r_out[p, :] = pm[j][:, 0:A_WIDTH]
        k_out[p, :] = k * (1.0 + (a_lr - 1.0) * ka_ref[...])
        v_out[p, :] = pm[j][:, 2 * A_WIDTH:3 * A_WIDTH]
        a_out[p, :] = -kk
        b_out[p, :] = kk * a_lr
        ld_out[p, :] = -jnp.exp(w)
        g_out[p, :] = gate[j]
        ga_out[p, :] = gates[j][:, :D_MODEL].astype(BF16)
        ybg_out[p, :] = (gates[j][:, D_MODEL:] * yb[j]).astype(BF16)


def _wkv_kernel(r_ref, k_ref, v_ref, a_ref, b_ref, ld_ref, g_ref, rk_ref, lng_ref, lnb_ref,
                ones_ref, o_ref, st_ref, ta_s, tl_s, arb_s, ark_s, rt_s, bkt_s, dcol_s, y_s):
    C = WKV_CHUNK
    bb, tb, _ = r_ref.shape
    n_chunks = tb // C
    n_pairs = A_WIDTH // HEAD_PAIR

    @pl.when(pl.program_id(1) == 0)
    def _():
        st_ref[...] = jnp.zeros_like(st_ref)

    row = lax.broadcasted_iota(jnp.int32, (C, HEAD_PAIR), 0)
    src = lax.broadcasted_iota(jnp.int32, (C, HEAD_PAIR), 1) & (C - 1)
    incl = src <= row
    strict = src < row
    eye_pair = jnp.where(src == row, 1.0, 0.0)
    bd_mask = ((lax.broadcasted_iota(jnp.int32, (HEAD_PAIR, HEAD_PAIR), 0) >= A_HEAD)
               == (lax.broadcasted_iota(jnp.int32, (HEAD_PAIR, HEAD_PAIR), 1) >= A_HEAD))
    tri_c = (lax.broadcasted_iota(jnp.int32, (C, C), 0)
             >= lax.broadcasted_iota(jnp.int32, (C, C), 1)).astype(BF16)
    ones_pair = ones_ref[...]
    pair_cols = [slice(p * HEAD_PAIR, (p + 1) * HEAD_PAIR) for p in range(n_pairs)]

    def bd(x):
        xb = x.astype(BF16)
        return jnp.where(bd_mask, jnp.concatenate([xb, xb], axis=0), jnp.zeros((), BF16))

    def prep(it, carry):
        where, lhs, rhs, ats = [], [], [], []
        for j in range(WKV_PREP_GROUP):
            flat = it * WKV_PREP_GROUP + j
            b = flat // n_chunks
            c = flat % n_chunks
            sl = pl.ds(pl.multiple_of(c * C, C), C)
            ld = ld_ref[b, sl, :]
            hi, lo = _split2(ld)
            cl = _dot(tri_c, hi) + _dot(tri_c, lo)
            cl_end = cl[C - 1:C, :]
            k = k_ref[b, sl, :]
            bv = b_ref[b, sl, :]
            d_inv = jnp.exp(-cl)
            d_end = jnp.exp(cl_end)
            d_tail = d_end * d_inv
            rt = (r_ref[b, sl, :] * jnp.exp(cl)).astype(BF16)
            rt_s[b, sl, :] = rt
            kt = k * d_inv
            bt = bv * d_inv
            at = a_ref[b, sl, :] * jnp.exp(cl - ld)
            kd = k * d_tail
            bdk = bv * d_tail
            for p, cs in enumerate(pair_cols):
                bkt_s[b, c, p] = jnp.concatenate([bdk[:, cs], kd[:, cs]], axis=0).T.astype(BF16)
                dcol_s[b, c, p] = jnp.broadcast_to(d_end[:, cs], (HEAD_PAIR, HEAD_PAIR)).T
                where.append((b, sl, cs))
                ats.append(at[:, cs])
                lhs.append(jnp.concatenate([rt[:, cs], at[:, cs].astype(BF16)], axis=0))
                rhs.append(jnp.concatenate([bd(kt[:, cs]), bd(bt[:, cs])], axis=0))
        n = len(where)
        amat = [_dot_nt(lhs[i], rhs[i]) for i in range(n)]
        l_ak, l_ab = [], []
        for i, (b, sl, cs) in enumerate(where):
            ark_s[b, sl, cs] = jnp.where(incl, amat[i][:C, :HEAD_PAIR], 0.0).astype(BF16)
            arb_s[b, sl, cs] = jnp.where(incl, amat[i][:C, HEAD_PAIR:], 0.0).astype(BF16)
            l_ak.append(jnp.where(strict, amat[i][C:, :HEAD_PAIR], 0.0))
            l_ab.append(jnp.where(strict, amat[i][C:, HEAD_PAIR:], 0.0))

        t_mat = [eye_pair + l for l in l_ab]
        q = [_dot(l.astype(BF16), bd(l)) for l in l_ab]
        n_sq = 1
        while 2 * n_sq < C // 2:
            both = [_dot(q[i].astype(BF16), jnp.concatenate([bd(q[i]), bd(t_mat[i])], axis=1))
                    for i in range(n)]
            q = [x[:, :HEAD_PAIR] for x in both]
            t_mat = [t_mat[i] + both[i][:, HEAD_PAIR:] for i in range(n)]
            n_sq *= 2
        t_mat = [t_mat[i] + _dot(q[i].astype(BF16), bd(t_mat[i])) for i in range(n)]
        tal = [_dot(t_mat[i].astype(BF16), jnp.concatenate([bd(ats[i]), bd(l_ak[i])], axis=1))
               for i in range(n)]
        for i, (b, sl, cs) in enumerate(where):
            ta_s[b, sl, cs] = tal[i][:, :HEAD_PAIR].astype(BF16)
            tl_s[b, sl, cs] = tal[i][:, HEAD_PAIR:].astype(BF16)
        return carry

    lax.fori_loop(0, bb * n_chunks // WKV_PREP_GROUP, prep, 0)

    def step(c, carry):
        sl = pl.ds(pl.multiple_of(c * C, C), C)
        chains = [(b, p) for b in range(bb) for p in range(n_pairs)]
        v = [v_ref[b, sl, :] for b in range(bb)]
        st = [st_ref[b, p] for b, p in chains]
        st_b = [x.astype(BF16) for x in st]
        bd_v = [bd(v[b][:, pair_cols[p]]) for b, p in chains]
        on_st = [_dot(jnp.concatenate([ta_s[b, sl, pair_cols[p]], rt_s[b, sl, pair_cols[p]]],
                                      axis=0), st_b[i]) for i, (b, p) in enumerate(chains)]
        on_v = [_dot(jnp.concatenate([tl_s[b, sl, pair_cols[p]], ark_s[b, sl, pair_cols[p]]],
                                     axis=0), bd_v[i]) for i, (b, p) in enumerate(chains)]
        u = [on_st[i][:C] + on_v[i][:C] for i in range(len(chains))]
        for i, (b, p) in enumerate(chains):
            uv = jnp.concatenate([u[i], v[b][:, pair_cols[p]]], axis=0).astype(BF16)
            st_ref[b, p] = (dcol_s[b, c, p] * st[i]
                            + jnp.where(bd_mask, _dot(bkt_s[b, c, p], uv), 0.0))
        ys = [on_st[i][C:] + on_v[i][C:] + _dot(arb_s[b, sl, pair_cols[p]], bd(u[i]))
              for i, (b, p) in enumerate(chains)]
        for b in range(bb):
            y_s[b, sl, :] = jnp.concatenate(ys[b * n_pairs:(b + 1) * n_pairs], axis=1)
        return carry

    lax.fori_loop(0, n_chunks, step, 0)

    def finish(it, carry):
        items = [(b, pl.ds(pl.multiple_of((it * WKV_NORM_GROUP + j) * C, C), C))
                 for j in range(WKV_NORM_GROUP) for b in range(bb)]
        y = [y_s[b, sl, :] for b, sl in items]
        rkr = [r_ref[b, sl, :] * k_ref[b, sl, :] * rk_ref[...] for b, sl in items]
        sums = [_pair_head_sums([y[i], rkr[i]], ones_pair) for i in range(len(items))]
        yc = [y[i] - sums[i][0] * (1.0 / A_HEAD) for i in range(len(items))]
        var = [_pair_head_sums([x * x], ones_pair)[0] * (1.0 / A_HEAD) for x in yc]
        for i, (b, sl) in enumerate(items):
            yn = yc[i] * lax.rsqrt(var[i] + LNX_EPS) * lng_ref[...] + lnb_ref[...]
            out = (yn + sums[i][1] * v_ref[b, sl, :]) * g_ref[b, sl, :]
            o_ref[b, sl, :] = out.astype(o_ref.dtype)
        return carry

    lax.fori_loop(0, n_chunks // WKV_NORM_GROUP, finish, 0)


def _post_kernel(x_ref, ya_ref, ga_ref, ybg_ref, woa_ref, wout_ref, g2_ref, wrt_ref, brt_ref,
                 x2_out, hl_out, route_out, cnt_out):
    n_sub = x_ref.shape[0] // MOE_SUB
    groups = [slice(g * MOE_SUB, (g + 1) * MOE_SUB) for g in range(n_sub)]
    y_a = [_dot(ya_ref[p, :].astype(BF16), woa_ref[...]) for p in groups]
    x2 = [x_ref[p, :] + _dot((ga_ref[p, :] * y + ybg_ref[p, :]).astype(BF16), wout_ref[...])
          for p, y in zip(groups, y_a)]
    h2 = [_rms(x, g2_ref[...]) for x in x2]
    split = [_split2(h) for h in h2]
    on_hi = [_dot_nt(wrt_ref[...], hi) for hi, _ in split]
    on_lo = [_dot_nt(wrt_ref[0:LANES, :], lo) for _, lo in split]
    row = lax.broadcasted_iota(jnp.int32, (LANES, MOE_SUB), 0)
    neg = jnp.float32(-jnp.inf)
    big = jnp.int32(LANES)
    is_grp = (row >= N_EXPERTS) & (row < N_EXPERTS + N_GROUPS)
    top = lambda v: jnp.max(v, axis=0, keepdims=True)
    first = lambda hit: jnp.min(jnp.where(hit, row, big), axis=0, keepdims=True)
    picks, onehots, weights = [], [], []
    for g, p in enumerate(groups):
        x2_out[p, :] = x2[g].astype(x2_out.dtype)
        logits = on_hi[g][:LANES, :] + on_hi[g][LANES:, :] + on_lo[g] + brt_ref[...]
        gl = jnp.where(is_grp, logits, neg)
        gmax = top(gl)
        g_p = 1.0 / jnp.sum(jnp.exp(gl - gmax), axis=0, keepdims=True)
        lo_row = (first(gl == gmax) - N_EXPERTS) * EXPERTS_PER_GROUP
        el = jnp.where((row >= lo_row) & (row < lo_row + EXPERTS_PER_GROUP), logits, neg)
        e1 = top(el)
        i1 = first(el == e1)
        el2 = jnp.where(row == i1, neg, el)
        e2 = top(el2)
        i2 = first(el2 == e2)
        t = jnp.exp(e2 - e1)
        weights.append((g_p / (1.0 + t), g_p * t / (1.0 + t)))
        picks.append((i1, i2))
        onehots.append(jnp.where((row == i1) | (row == i2), 1.0, 0.0))

    earlier = (lax.broadcasted_iota(jnp.int32, (MOE_SUB, MOE_SUB), 0)
               <= lax.broadcasted_iota(jnp.int32, (MOE_SUB, MOE_SUB), 1)).astype(BF16)
    incl = [_dot(oh.astype(BF16), earlier) for oh in onehots]
    chunks = [jnp.ceil(x[:, MOE_SUB - 1:] * (1.0 / RUN_ALIGN)) for x in incl]
    below = (lax.broadcasted_iota(jnp.int32, (LANES, LANES), 1)
             < lax.broadcasted_iota(jnp.int32, (LANES, LANES), 0)).astype(BF16)
    wide = [jnp.broadcast_to(c, (LANES, LANES)) for c in chunks]
    run_start = [_dot(below, w.astype(BF16))[:, 0:1] * RUN_ALIGN for w in wide]
    row8 = lax.broadcasted_iota(jnp.int32, (8, LANES), 0)
    counts = jnp.zeros((8, LANES), F32)
    slot_ids = lax.broadcasted_iota(jnp.int32, (RUN_SLOTS, MOE_SUB), 0)
    for g, p in enumerate(groups):
        counts = jnp.where(row8 == g, wide[g].T[0:8, :], counts)
        slot_of = run_start[g] + incl[g] - onehots[g]
        i1, i2 = picks[g]
        slot1 = jnp.sum(jnp.where(row == i1, slot_of, 0.0), axis=0, keepdims=True)
        slot2 = jnp.sum(jnp.where(row == i2, slot_of, 0.0), axis=0, keepdims=True)
        w1, w2 = weights[g]
        packed = jnp.where(row == 0, slot1, jnp.where(row == 1, slot2, jnp.where(
            row == 2, w1, jnp.where(row == 3, w2, 0.0))))
        route_out[p, :] = packed.T
        pick = jnp.where((slot_ids == slot1.astype(jnp.int32))
                         | (slot_ids == slot2.astype(jnp.int32)), 1.0, 0.0).astype(BF16)
        hl_out[g * RUN_SLOTS:(g + 1) * RUN_SLOTS, :] = _dot(pick, h2[g].astype(BF16)).astype(BF16)
    cnt_out[...] = counts


def _expert_kernel(chunk_s, first_s, hl_in_ref, wg_ref, wu_ref, wd_ref, hl_ref, xbuf, wgu_s, wd_s,
                   sem, drain_s):
    del hl_in_ref
    e = pl.program_id(0)
    pass_chunks = EXPERT_ROWS // RUN_ALIGN
    mine = e % 2
    other = 1 - mine
    gather_sem = lambda buf: sem.at[buf]
    scatter_sem = lambda buf: sem.at[2 + buf]

    def first_pass(ex):
        lo = first_s[ex]
        return lo, jnp.minimum(lo + pass_chunks, first_s[ex + 1])

    def for_chunks(buf, c_lo, c_hi, fn):
        def one(c):
            fn(hl_ref.at[pl.ds(pl.multiple_of(chunk_s[c], RUN_ALIGN), RUN_ALIGN)],
               xbuf.at[buf, pl.ds(pl.multiple_of((c - c_lo) * RUN_ALIGN, RUN_ALIGN), RUN_ALIGN)])

        def block(i, carry):
            for u in range(CHUNK_UNROLL):
                one(c_lo + i * CHUNK_UNROLL + u)
            return carry

        def tail(c, carry):
            one(c)
            return carry

        n_blocks = (c_hi - c_lo) // CHUNK_UNROLL
        lax.fori_loop(0, n_blocks, block, 0)
        lax.fori_loop(c_lo + n_blocks * CHUNK_UNROLL, c_hi, tail, 0)

    def gather(buf, c_lo, c_hi):
        for_chunks(buf, c_lo, c_hi, lambda hbm, vm: pltpu.make_async_copy(
            hbm, vm, gather_sem(buf)).start())

    def scatter(buf, c_lo, c_hi):
        for_chunks(buf, c_lo, c_hi, lambda hbm, vm: pltpu.make_async_copy(
            vm, hbm, scatter_sem(buf)).start())

    def wait_chunks(which_sem, n):
        def wait_rows(rows):
            def body(c, carry):
                pltpu.make_async_copy(hl_ref.at[pl.ds(0, rows)], xbuf.at[0, pl.ds(0, rows)],
                                      which_sem).wait()
                return carry
            return body

        lax.fori_loop(0, n // CHUNK_UNROLL, wait_rows(CHUNK_UNROLL * RUN_ALIGN), 0)
        lax.fori_loop(0, n % CHUNK_UNROLL, wait_rows(RUN_ALIGN), 0)

    def evaluate(buf, n_chunks):
        def tile(k, carry):
            sl = pl.ds(pl.multiple_of(k * EXPERT_TILE, EXPERT_TILE), EXPERT_TILE)
            gu = _dot(xbuf[buf, sl, :], wgu_s[...])
            gate = gu[:, :D_EXPERT]
            act = gate * jax.nn.sigmoid(gate) * gu[:, D_EXPERT:]
            xbuf[buf, sl, :] = _dot(act.astype(BF16), wd_s[...]).astype(BF16)
            return carry

        lax.fori_loop(0, (n_chunks * RUN_ALIGN + EXPERT_TILE - 1) // EXPERT_TILE, tile, 0)

    @pl.when(e == 0)
    def _():
        xbuf[...] = jnp.zeros_like(xbuf)
        gather(0, *first_pass(0))
        drain_s[0] = 0

    wgu_s[:, :D_EXPERT] = wg_ref[0].astype(BF16)
    wgu_s[:, D_EXPERT:] = wu_ref[0].astype(BF16)
    wd_s[...] = wd_ref[0].astype(BF16)

    c_lo, c_hi = first_pass(e)
    wait_chunks(gather_sem(mine), c_hi - c_lo)
    wait_chunks(scatter_sem(other), drain_s[0])

    @pl.when(e + 1 < pl.num_programs(0))
    def _():
        gather(other, *first_pass(e + 1))

    evaluate(mine, c_hi - c_lo)
    scatter(mine, c_lo, c_hi)

    def later_pass(state):
        lo, pending = state
        hi = jnp.minimum(lo + pass_chunks, first_s[e + 1])
        wait_chunks(scatter_sem(mine), pending)
        gather(mine, lo, hi)
        wait_chunks(gather_sem(mine), hi - lo)
        evaluate(mine, hi - lo)
        scatter(mine, lo, hi)
        return hi, hi - lo

    _, pending = lax.while_loop(lambda st: st[0] < first_s[e + 1], later_pass,
                                (c_hi, c_hi - c_lo))
    drain_s[0] = pending

    @pl.when(e + 1 == pl.num_programs(0))
    def _():
        wait_chunks(scatter_sem(mine), pending)


def _combine_kernel(x2_ref, route_ref, gf_ref, yl_ref, o_ref):
    slots = lax.broadcasted_iota(jnp.int32, (MOE_SUB, RUN_SLOTS), 1).astype(F32)
    for g in range(x2_ref.shape[0] // MOE_SUB):
        p = slice(g * MOE_SUB, (g + 1) * MOE_SUB)
        route = route_ref[p, :]
        pick = (jnp.where(slots == route[:, 0:1], route[:, 2:3], 0.0)
                + jnp.where(slots == route[:, 1:2], route[:, 3:4], 0.0))
        y = _dot(pick.astype(BF16), yl_ref[g * RUN_SLOTS:(g + 1) * RUN_SLOTS, :])
        o_ref[p, :] = _rms(x2_ref[p, :] + y, gf_ref[...])


def _full(shape):
    return pl.BlockSpec(shape, lambda *_: (0,) * len(shape))


def _params(sem, vmem_limit=VMEM_LIMIT):
    return pltpu.CompilerParams(dimension_semantics=sem, vmem_limit_bytes=vmem_limit)


def _mixer_a_layout(t):
    s_rkv = 3 * A_WIDTH
    s_w = s_rkv + D_DECAY_LORA
    s_a = s_w + D_AAA_LORA
    gap = lambda n: jnp.zeros((t.shape[0], n), t.dtype)
    return jnp.concatenate(
        [t[:, :s_w], gap(XA_OFF - D_DECAY_LORA), t[:, s_w:s_a], gap(XG_OFF - XA_OFF - D_AAA_LORA),
         t[:, s_a:], gap(LORA_PAD - XG_OFF - D_GATE_LORA)], axis=1)


def _place(cols, parts):
    out = jnp.zeros((parts[0][1].shape[0], cols), parts[0][1].dtype)
    for off, arr in parts:
        out = lax.dynamic_update_slice(out, arr, (0, off))
    return out


def kernel(x, norm1_g, w_in, b_gate, tmix_mu, w0, w2, a0, a2, g2, k_k, k_a, r_k, lnx_g, lnx_b,
           w_oA, lnv_g, lnv_b, w_s, b_s, w_oB, w_out, norm2_g, w_rg, b_rg, w_re, b_re,
           w_e_gate, w_e_up, w_e_down, final_g):
    bsz, seq, d = x.shape
    n_tok = bsz * seq
    depth = norm1_g.shape[0]
    assert depth == 1, "the moe kernel fuses the final norm, so it must be the last layer"
    assert bsz % WKV_SEQS == 0 and seq % WKV_TOKENS == 0
    xf = x.reshape(n_tok, d)

    s_rkv = 3 * A_WIDTH
    s_w = s_rkv + D_DECAY_LORA
    s_a = s_w + D_AAA_LORA
    a_cols = s_a + D_GATE_LORA
    b_cols = 2 * B_WIDTH

    ones_bd = (jnp.arange(A_WIDTH)[:, None] // A_HEAD
               == jnp.arange(A_WIDTH)[None, :] // A_HEAD).astype(BF16)

    tm_a = INPROJ_TILE
    tm_p = POST_TILE
    tm_c = COMBINE_TILE
    assert tm_p % MOE_SUB == 0 and tm_c % MOE_SUB == 0 and tm_a % DENSE_SPLIT == 0
    assert seq % tm_a == 0, "an inproj tile must not straddle two sequences (token shift)"

    for l in range(depth):
        wl = w_in[l]
        n_w = 4
        col_blk = lambda rows: pl.BlockSpec((rows, d // n_w), lambda i: (0, i))
        row_blk = lambda rows: pl.BlockSpec((rows // n_w, d), lambda i: (i, 0))
        w_a, w_b, w_g, w_oa, w_ob, w_o = pl.pallas_call(
            _w_in_kernel,
            grid=(n_w,),
            in_specs=[col_blk(wl.shape[1]), row_blk(A_WIDTH), row_blk(B_WIDTH), row_blk(d)],
            out_specs=[col_blk(A_PROJ), col_blk(b_cols), col_blk(2 * d),
                       row_blk(A_WIDTH), row_blk(B_WIDTH), row_blk(d)],
            out_shape=[jax.ShapeDtypeStruct((A_PROJ, d), BF16),
                       jax.ShapeDtypeStruct((b_cols, d), BF16),
                       jax.ShapeDtypeStruct((2 * d, d), BF16),
                       jax.ShapeDtypeStruct((A_WIDTH, d), BF16),
                       jax.ShapeDtypeStruct((B_WIDTH, d), BF16),
                       jax.ShapeDtypeStruct((d, d), BF16)],
            compiler_params=_params(("parallel",)),
            name="w_in_layout",
        )(wl.T, w_oA[l], w_oB[l], w_out[l])
        mu_a = _mixer_a_layout(tmix_mu[l][None, :])
        w2p = jnp.pad(w2[l], ((0, LANES - D_DECAY_LORA), (0, 0))).astype(BF16)
        a2p = jnp.pad(a2[l], ((0, LANES - D_AAA_LORA), (0, 0))).astype(BF16)
        g2p = jnp.pad(g2[l], ((0, LORA_PAD - XG_OFF - D_GATE_LORA), (0, 0))).astype(BF16)
        g1 = norm1_g[l][None, :]

        row512 = lambda i: (i, 0)
        tok_a = pl.BlockSpec((tm_a, A_WIDTH), row512)
        tok_d = pl.BlockSpec((tm_a, d), row512)
        vec_a = _full((1, A_WIDTH))
        bs_full = jnp.repeat(b_s[l].T, B_GROUP_CH, axis=1)
        outs = pl.pallas_call(
            functools.partial(_inproj_kernel, seq // tm_a),
            grid=(n_tok // tm_a,),
            in_specs=[tok_d,
                      _full((1, d)), _full((A_PROJ, d)), _full((1, A_PROJ)), vec_a,
                      _full((LANES, A_WIDTH)), vec_a, _full((LANES, A_WIDTH)),
                      _full((LORA_PAD - XG_OFF, A_WIDTH)), vec_a, vec_a,
                      _full((A_WIDTH, A_WIDTH)),
                      _full((b_cols, d)), _full((2 * d, d)), _full((1, 2 * d)),
                      _full((1, B_WIDTH)), _full((1, B_WIDTH)),
                      _full((B_GROUPS, GMLP_BLOCK, GMLP_BLOCK)), _full((GMLP_BLOCK, B_WIDTH)),
                      _full((B_WIDTH, d))],
            out_specs=[tok_a] * 7 + [tok_d, tok_d],
            out_shape=[jax.ShapeDtypeStruct((n_tok, A_WIDTH), F32)] * 7
            + [jax.ShapeDtypeStruct((n_tok, d), BF16)] * 2,
            scratch_shapes=[pltpu.VMEM((8, A_PROJ), F32)],
            compiler_params=_params(("arbitrary",)),
            name="inproj",
        )(xf, g1, w_a, mu_a, w0[l][None, :], w2p, a0[l][None, :], a2p, g2p,
          k_k[l][None, :], k_a[l][None, :], ones_bd, w_b, w_g, b_gate[l][None, :],
          lnv_g[l][None, :], lnv_b[l][None, :], w_s[l], bs_full, w_ob)
        r_, k_, v_, a_, b_, ld_, gg_, ybg, ga = outs

        n_pairs = A_WIDTH // HEAD_PAIR
        n_chunks = WKV_TOKENS // WKV_CHUNK
        tok_w = pl.BlockSpec((WKV_SEQS, WKV_TOKENS, A_WIDTH), lambda bi, ti: (bi, ti, 0))
        vec_w = _full((1, A_WIDTH))
        seq_major = lambda t: t.reshape(bsz, seq, A_WIDTH)
        tok_scratch = pltpu.VMEM((WKV_SEQS, WKV_TOKENS, A_WIDTH), BF16)
        ya_in = pl.pallas_call(
            _wkv_kernel,
            grid=(bsz // WKV_SEQS, seq // WKV_TOKENS),
            in_specs=[tok_w] * 7 + [vec_w, vec_w, vec_w, _full((HEAD_PAIR, HEAD_PAIR))],
            out_specs=tok_w,
            out_shape=jax.ShapeDtypeStruct((bsz, seq, A_WIDTH), BF16),
            scratch_shapes=[pltpu.VMEM((WKV_SEQS, n_pairs, HEAD_PAIR, HEAD_PAIR), F32)]
            + [tok_scratch] * 5
            + [pltpu.VMEM((WKV_SEQS, n_chunks, n_pairs, HEAD_PAIR, HEAD_PAIR), BF16),
               pltpu.VMEM((WKV_SEQS, n_chunks, n_pairs, HEAD_PAIR, HEAD_PAIR), F32),
               pltpu.VMEM((WKV_SEQS, WKV_TOKENS, A_WIDTH), F32)],
            compiler_params=_params(("parallel", "arbitrary")),
            name="wkv",
        )(*[seq_major(t) for t in (r_, k_, v_, a_, b_, ld_, gg_)], r_k[l].reshape(1, A_WIDTH),
          lnx_g[l][None, :], lnx_b[l][None, :],
          ones_bd[:HEAD_PAIR, :HEAD_PAIR]).reshape(n_tok, A_WIDTH)

        w_r = _place(LANES, [(0, jnp.transpose(w_re[l], (1, 0, 2)).reshape(d, N_EXPERTS)),
                             (N_EXPERTS, w_rg[l])])
        b_r = _place(LANES, [(0, b_re[l].reshape(1, N_EXPERTS)), (N_EXPERTS, b_rg[l][None, :])])
        wr_hi = w_r.astype(BF16)
        wr_lo = (w_r - wr_hi.astype(F32)).astype(BF16)
        wr_t = jnp.concatenate([wr_hi.T, wr_lo.T], axis=0)
        tok_p = pl.BlockSpec((tm_p, d), row512)
        lane_p = pl.BlockSpec((tm_p, LANES), row512)
        n_sub = n_tok // MOE_SUB
        sorted_rows = tm_p // MOE_SUB * RUN_SLOTS
        x2, h_sorted, route, cnt = pl.pallas_call(
            _post_kernel,
            grid=(n_tok // tm_p,),
            in_specs=[tok_p, pl.BlockSpec((tm_p, A_WIDTH), row512), tok_p, tok_p,
                      _full((A_WIDTH, d)), _full((d, d)), _full((1, d)), _full((2 * LANES, d)),
                      _full((LANES, 1))],
            out_specs=[tok_p, pl.BlockSpec((sorted_rows, d), row512), lane_p,
                       pl.BlockSpec((8, LANES), row512)],
            out_shape=[jax.ShapeDtypeStruct((n_tok, d), F32),
                       jax.ShapeDtypeStruct((n_sub * RUN_SLOTS, d), BF16),
                       jax.ShapeDtypeStruct((n_tok, LANES), F32),
                       jax.ShapeDtypeStruct((n_tok // tm_p * 8, LANES), F32)],
            compiler_params=_params(("parallel",), POST_VMEM_LIMIT),
            name="post",
        )(xf, ya_in, ga, ybg, w_oa, w_o,
          norm2_g[l][None, :], wr_t, b_r.T)

        run_chunks = cnt.reshape(n_tok // tm_p, 8, LANES)[:, :tm_p // MOE_SUB, :N_EXPERTS]
        run_chunks = run_chunks.reshape(n_sub, N_EXPERTS).astype(jnp.int32)
        run_start = (jnp.cumsum(run_chunks, axis=1) - run_chunks) * RUN_ALIGN
        group_end = jnp.cumsum(run_chunks, axis=0)
        first_chunk = jnp.concatenate([jnp.zeros((1,), jnp.int32), jnp.cumsum(group_end[-1])])
        pos = jnp.arange(n_sub * RUN_SLOTS // RUN_ALIGN, dtype=jnp.int32)
        e_of = jnp.minimum(jnp.sum(pos[:, None] >= first_chunk[None, 1:], axis=1), N_EXPERTS - 1)
        of_e = (e_of[:, None] == jnp.arange(N_EXPERTS, dtype=jnp.int32)[None, :])
        column = lambda t: jnp.dot(of_e.astype(F32), t.T.astype(F32),
                                   precision=lax.Precision.HIGHEST).astype(jnp.int32)
        in_expert = pos - jnp.sum(jnp.where(of_e, first_chunk[None, :-1], 0), axis=1)
        ends = column(group_end)
        g_of = jnp.minimum(jnp.sum(in_expert[:, None] >= ends, axis=1), n_sub - 1)
        of_g = g_of[:, None] == jnp.arange(n_sub, dtype=jnp.int32)[None, :]
        at_g = lambda t: jnp.sum(jnp.where(of_g, t, 0), axis=1)
        in_run = in_expert - at_g(ends - column(run_chunks))
        chunk_rows = g_of * RUN_SLOTS + at_g(column(run_start)) + in_run * RUN_ALIGN

        per_expert = lambda e, *_: (e, 0, 0)
        y_sorted = pl.pallas_call(
            _expert_kernel,
            grid_spec=pltpu.PrefetchScalarGridSpec(
                num_scalar_prefetch=2, grid=(N_EXPERTS,),
                in_specs=[pl.BlockSpec(memory_space=pl.ANY),
                          pl.BlockSpec((1, d, D_EXPERT), per_expert),
                          pl.BlockSpec((1, d, D_EXPERT), per_expert),
                          pl.BlockSpec((1, D_EXPERT, d), per_expert)],
                out_specs=pl.BlockSpec(memory_space=pl.ANY),
                scratch_shapes=[pltpu.VMEM((2, EXPERT_ROWS, d), BF16),
                                pltpu.VMEM((d, 2 * D_EXPERT), BF16),
                                pltpu.VMEM((D_EXPERT, d), BF16),
                                pltpu.SemaphoreType.DMA((4,)),
                                pltpu.SMEM((1,), jnp.int32)]),
            out_shape=jax.ShapeDtypeStruct((n_sub * RUN_SLOTS, d), BF16),
            input_output_aliases={2: 0},
            compiler_params=_params(("arbitrary",)),
            name="moe_experts",
        )(chunk_rows, first_chunk, h_sorted, w_e_gate[l], w_e_up[l], w_e_down[l])

        xf = pl.pallas_call(
            _combine_kernel,
            grid=(n_tok // tm_c,),
            in_specs=[pl.BlockSpec((tm_c, d), row512), pl.BlockSpec((tm_c, LANES), row512),
                      _full((1, d)), pl.BlockSpec((tm_c // MOE_SUB * RUN_SLOTS, d), row512)],
            out_specs=pl.BlockSpec((tm_c, d), row512),
            out_shape=jax.ShapeDtypeStruct((n_tok, d), F32),
            compiler_params=_params(("parallel",)),
            name="moe_combine",
        )(x2, route, final_g[None, :], y_sorted)

    return xf.reshape(bsz, seq, d)
```

```python
import functools

import jax
import jax.numpy as jnp
from jax import lax
from jax.experimental import pallas as pl
from jax.experimental.pallas import tpu as pltpu

F32 = jnp.float32
BF16 = jnp.bfloat16

D_MODEL = 1024
A_WIDTH = 512
A_HEAD = 64
D_DECAY_LORA = 64
D_AAA_LORA = 64
D_GATE_LORA = 160
B_WIDTH = 512
B_GROUPS = 4
B_GROUP_CH = 128
GMLP_BLOCK = 128
N_GROUPS = 4
EXPERTS_PER_GROUP = 8
N_EXPERTS = 32
D_EXPERT = 256
NORM_EPS = 1e-6
LN_EPS = 1e-5
LNX_EPS = 64e-5

LANES = 128
LORA_PAD = 512
XW_OFF, XA_OFF, XG_OFF = 0, 128, 256
A_PROJ = 3 * A_WIDTH + LORA_PAD
WKV_CHUNK = 64
HEAD_PAIR = 2 * A_HEAD
WKV_SEQS = 8
WKV_TOKENS = 64
WKV_PREP_GROUP = 8
WKV_NORM_GROUP = 1
MOE_SUB = 256
RUN_ALIGN = 16
RUN_SLOTS = 2 * MOE_SUB + N_EXPERTS * RUN_ALIGN
EXPERT_ROWS = 2048
EXPERT_TILE = 512
INPROJ_TILE = 512
POST_TILE = 1024
COMBINE_TILE = 1024
DENSE_SPLIT = 256
CHUNK_UNROLL = 8
VMEM_LIMIT = 48 * 1024 * 1024
POST_VMEM_LIMIT = 56 * 1024 * 1024


def _rms(x, g):
    return x * lax.rsqrt(jnp.mean(x * x, axis=-1, keepdims=True) + NORM_EPS) * g


def _dot(a, b):
    return jnp.dot(a, b, preferred_element_type=F32)


def _dot_nt(a, b):
    return lax.dot_general(a, b, (((1,), (1,)), ((), ())), preferred_element_type=F32)


def _split2(x):
    hi = x.astype(BF16)
    lo = (x - hi.astype(F32)).astype(BF16)
    return hi, lo


def _split3(x):
    hi = x.astype(BF16)
    r1 = x - hi.astype(F32)
    mid = r1.astype(BF16)
    lo = (r1 - mid.astype(F32)).astype(BF16)
    return hi, mid, lo


def _pair_head_sums(xs, ones_pair):
    n_tiles = A_WIDTH // HEAD_PAIR
    rows = xs[0].shape[0]
    tiles = [x[:, t * HEAD_PAIR:(t + 1) * HEAD_PAIR].astype(BF16) for x in xs for t in range(n_tiles)]
    sums = _dot(jnp.concatenate(tiles, axis=0), ones_pair)
    return [jnp.concatenate([sums[(i * n_tiles + t) * rows:(i * n_tiles + t + 1) * rows]
                             for t in range(n_tiles)], axis=1) for i in range(len(xs))]


def _w_in_kernel(w_ref, woa_ref, wob_ref, wout_ref, wa_out, wb_out, wg_out, woa_out, wob_out,
                 wout_out):
    woa_out[...] = woa_ref[...].astype(BF16)
    wob_out[...] = wob_ref[...].astype(BF16)
    wout_out[...] = wout_ref[...].astype(BF16)
    s_rkv = 3 * A_WIDTH
    s_w = s_rkv + D_DECAY_LORA
    s_a = s_w + D_AAA_LORA
    a_cols = s_a + D_GATE_LORA
    b_end = a_cols + 2 * B_WIDTH
    wa_out[...] = jnp.zeros_like(wa_out)
    wa_out[:s_w, :] = w_ref[:s_w, :].astype(BF16)
    wa_out[s_rkv + XA_OFF:s_rkv + XA_OFF + D_AAA_LORA, :] = w_ref[s_w:s_a, :].astype(BF16)
    wa_out[s_rkv + XG_OFF:s_rkv + XG_OFF + D_GATE_LORA, :] = w_ref[s_a:a_cols, :].astype(BF16)
    wb_out[...] = w_ref[a_cols:b_end, :].astype(BF16)
    wg_out[...] = w_ref[b_end:, :].astype(BF16)


def _inproj_kernel(tiles_per_seq, x_ref, g1_ref, wa_ref, mu_ref, w0_ref, w2_ref, a0_ref, a2_ref,
                   g2_ref, kk_ref, ka_ref, ones_ref, wb_ref, wg_ref, bg_ref, lng_ref, lnb_ref,
                   ws_ref, bs_ref, wo_ref,
                   r_out, k_out, v_out, a_out, b_out, ld_out, g_out, ybg_out, ga_out, tail_ref):
    i = pl.program_id(0)
    tm = x_ref.shape[0]
    parts = [slice(j * DENSE_SPLIT, (j + 1) * DENSE_SPLIT) for j in range(tm // DENSE_SPLIT)]
    g1 = g1_ref[...]
    h = [_rms(x_ref[p, :], g1).astype(BF16) for p in parts]
    proj = [_dot_nt(hh, wa_ref[...]) for hh in h]
    pb = [_dot_nt(hh, wb_ref[...]) for hh in h]

    prev = [jnp.where(i % tiles_per_seq == 0, 0.0, tail_ref[7:8, :])]
    prev += [p[DENSE_SPLIT - 1:, :] for p in proj[:-1]]
    tail_ref[...] = proj[-1][DENSE_SPLIT - 8:, :]
    row = lax.broadcasted_iota(jnp.int32, proj[0].shape, 0)
    pm = []
    for p, pv in zip(proj, prev):
        shifted = jnp.where(row == 0, pv, pltpu.roll(p, 1, axis=0))
        pm.append(p + mu_ref[...] * (shifted - p))
    gates = [jax.nn.sigmoid(_dot_nt(hh, wg_ref[...]) + bg_ref[...]) for hh in h]

    tri = (lax.broadcasted_iota(jnp.int32, (GMLP_BLOCK, GMLP_BLOCK), 0)
           >= lax.broadcasted_iota(jnp.int32, (GMLP_BLOCK, GMLP_BLOCK), 1))
    ws = [jnp.where(tri, ws_ref[grp], 0.0).astype(BF16) for grp in range(B_GROUPS)]
    bs = bs_ref[...]
    us, vns = [], []
    for x in pb:
        z = 0.5 * x * (1.0 + lax.erf(x * (2.0 ** -0.5)))
        us.append(z[:, :B_WIDTH])
        v = z[:, B_WIDTH:]
        mean = jnp.mean(v, axis=-1, keepdims=True)
        vc = v - mean
        var = jnp.mean(vc * vc, axis=-1, keepdims=True)
        vns.append((vc * lax.rsqrt(var + LN_EPS) * lng_ref[...] + lnb_ref[...]).astype(BF16))

    lora = [x[:, 3 * A_WIDTH:] for x in pm]
    decay = [_dot(jnp.tanh(x[:, XW_OFF:XW_OFF + LANES]).astype(BF16), w2_ref[...]) for x in lora]
    rate = [_dot(x[:, XA_OFF:XA_OFF + LANES].astype(BF16), a2_ref[...]) for x in lora]
    gate = [_dot(jax.nn.sigmoid(x[:, XG_OFF:]).astype(BF16), g2_ref[...]) for x in lora]
    kks = [x[:, A_WIDTH:2 * A_WIDTH] * kk_ref[...] for x in pm]
    sq = [_dot((kk * kk).astype(BF16), ones_ref[...]) for kk in kks]

    svs = []
    for vn in vns:
        rows = []
        for blk in range(DENSE_SPLIT // GMLP_BLOCK):
            cols = [_dot(ws[grp], vn[blk * GMLP_BLOCK:(blk + 1) * GMLP_BLOCK,
                                     grp * B_GROUP_CH:(grp + 1) * B_GROUP_CH])
                    for grp in range(B_GROUPS)]
            rows.append(jnp.concatenate(cols, axis=1) + bs)
        svs.append(jnp.concatenate(rows, axis=0))
    yb = [_dot((u * sv).astype(BF16), wo_ref[...]) for u, sv in zip(us, svs)]

    for j, p in enumerate(parts):
        k = pm[j][:, A_WIDTH:2 * A_WIDTH]
        z = -(w0_ref[...] + decay[j])
        softplus = jnp.maximum(z, 0.0) + jnp.log(1.0 + jnp.exp(-jnp.abs(z)))
        w = -softplus - 0.5
        a_lr = jax.nn.sigmoid(a0_ref[...] + rate[j])
        kk = kks[j] / jnp.maximum(jnp.sqrt(sq[j]), 1e-12)
        r_out[p, :] = pm[j][:, 0:A_WIDTH]
        k_out[p, :] = k * (1.0 + (a_lr - 1.0) * ka_ref[...])
        v_out[p, :] = pm[j][:, 2 * A_WIDTH:3 * A_WIDTH]
        a_out[p, :] = -kk
        b_out[p, :] = kk * a_lr
        ld_out[p, :] = -jnp.exp(w)
        g_out[p, :] = gate[j]
        ga_out[p, :] = gates[j][:, :D_MODEL].astype(BF16)
        ybg_out[p, :] = (gates[j][:, D_MODEL:] * yb[j]).astype(BF16)


def _wkv_kernel(r_ref, k_ref, v_ref, a_ref, b_ref, ld_ref, g_ref, rk_ref, lng_ref, lnb_ref,
                ones_ref, o_ref, st_ref, ta_s, tl_s, arb_s, ark_s, rt_s, bkt_s, dcol_s, y_s):
    C = WKV_CHUNK
    bb, tb, _ = r_ref.shape
    n_chunks = tb // C
    n_pairs = A_WIDTH // HEAD_PAIR

    @pl.when(pl.program_id(1) == 0)
    def _():
        st_ref[...] = jnp.zeros_like(st_ref)

    row = lax.broadcasted_iota(jnp.int32, (C, HEAD_PAIR), 0)
    src = lax.broadcasted_iota(jnp.int32, (C, HEAD_PAIR), 1) & (C - 1)
    incl = src <= row
    strict = src < row
    eye_pair = jnp.where(src == row, 1.0, 0.0)
    bd_mask = ((lax.broadcasted_iota(jnp.int32, (HEAD_PAIR, HEAD_PAIR), 0) >= A_HEAD)
               == (lax.broadcasted_iota(jnp.int32, (HEAD_PAIR, HEAD_PAIR), 1) >= A_HEAD))
    tri_c = (lax.broadcasted_iota(jnp.int32, (C, C), 0)
             >= lax.broadcasted_iota(jnp.int32, (C, C), 1)).astype(BF16)
    ones_pair = ones_ref[...]
    pair_cols = [slice(p * HEAD_PAIR, (p + 1) * HEAD_PAIR) for p in range(n_pairs)]

    def bd(x):
        xb = x.astype(BF16)
        return jnp.where(bd_mask, jnp.concatenate([xb, xb], axis=0), jnp.zeros((), BF16))

    first_row = lax.broadcasted_iota(jnp.int32, (C, 1), 0) == 0

    def prep(it, carry):
        where, lhs, rhs, ats = [], [], [], []
        for j in range(WKV_PREP_GROUP):
            flat = it * WKV_PREP_GROUP + j
            b = flat // n_chunks
            c = flat % n_chunks
            sl = pl.ds(pl.multiple_of(c * C, C), C)
            ld = ld_ref[b, sl, :]
            hi, lo = _split2(ld)
            cl = _dot(tri_c, hi) + _dot(tri_c, lo)
            cl_end = cl[C - 1:C, :]
            k = k_ref[b, sl, :]
            bv = b_ref[b, sl, :]
            d_inv = jnp.exp(-cl)
            d_end = jnp.exp(cl_end)
            d_tail = d_end * d_inv
            d_cum = jnp.exp(cl)
            rt = (r_ref[b, sl, :] * d_cum).astype(BF16)
            rt_s[b, sl, :] = rt
            kt = k * d_inv
            bt = bv * d_inv
            d_prev = jnp.where(first_row, 1.0, pltpu.roll(d_cum, 1, 0))
            at = a_ref[b, sl, :] * d_prev
            kd = k * d_tail
            bdk = bv * d_tail
            for p, cs in enumerate(pair_cols):
                bkt_s[b, c, p] = jnp.concatenate([bdk[:, cs], kd[:, cs]], axis=0).T.astype(BF16)
                dcol_s[b, c, p] = jnp.broadcast_to(d_end[:, cs], (HEAD_PAIR, HEAD_PAIR)).T
                where.append((b, sl, cs))
                ats.append(at[:, cs])
                lhs.append(jnp.concatenate([rt[:, cs], at[:, cs].astype(BF16)], axis=0))
                rhs.append(jnp.concatenate([bd(kt[:, cs]), bd(bt[:, cs])], axis=0))
        n = len(where)
        amat = [_dot_nt(lhs[i], rhs[i]) for i in range(n)]
        l_ak, l_ab = [], []
        for i, (b, sl, cs) in enumerate(where):
            ark_s[b, sl, cs] = jnp.where(incl, amat[i][:C, :HEAD_PAIR], 0.0).astype(BF16)
            arb_s[b, sl, cs] = jnp.where(incl, amat[i][:C, HEAD_PAIR:], 0.0).astype(BF16)
            l_ak.append(jnp.where(strict, amat[i][C:, :HEAD_PAIR], 0.0))
            l_ab.append(jnp.where(strict, amat[i][C:, HEAD_PAIR:], 0.0))

        t_mat = [eye_pair + l for l in l_ab]
        q = [_dot(l.astype(BF16), bd(l)) for l in l_ab]
        n_sq = 1
        while 2 * n_sq < C // 2:
            both = [_dot(q[i].astype(BF16), jnp.concatenate([bd(q[i]), bd(t_mat[i])], axis=1))
                    for i in range(n)]
            q = [x[:, :HEAD_PAIR] for x in both]
            t_mat = [t_mat[i] + both[i][:, HEAD_PAIR:] for i in range(n)]
            n_sq *= 2
        t_mat = [t_mat[i] + _dot(q[i].astype(BF16), bd(t_mat[i])) for i in range(n)]
        tal = [_dot(t_mat[i].astype(BF16), jnp.concatenate([bd(ats[i]), bd(l_ak[i])], axis=1))
               for i in range(n)]
        for i, (b, sl, cs) in enumerate(where):
            ta_s[b, sl, cs] = tal[i][:, :HEAD_PAIR].astype(BF16)
            tl_s[b, sl, cs] = tal[i][:, HEAD_PAIR:].astype(BF16)
        return carry

    lax.fori_loop(0, bb * n_chunks // WKV_PREP_GROUP, prep, 0)

    def step(c, carry):
        sl = pl.ds(pl.multiple_of(c * C, C), C)
        chains = [(b, p) for b in range(bb) for p in range(n_pairs)]
        v = [v_ref[b, sl, :] for b in range(bb)]
        st = [st_ref[b, p] for b, p in chains]
        st_b = [x.astype(BF16) for x in st]
        bd_v = [bd(v[b][:, pair_cols[p]]) for b, p in chains]
        on_st = [_dot(jnp.concatenate([ta_s[b, sl, pair_cols[p]], rt_s[b, sl, pair_cols[p]]],
                                      axis=0), st_b[i]) for i, (b, p) in enumerate(chains)]
        on_v = [_dot(jnp.concatenate([tl_s[b, sl, pair_cols[p]], ark_s[b, sl, pair_cols[p]]],
                                     axis=0), bd_v[i]) for i, (b, p) in enumerate(chains)]
        u = [on_st[i][:C] + on_v[i][:C] for i in range(len(chains))]
        for i, (b, p) in enumerate(chains):
            uv = jnp.concatenate([u[i], v[b][:, pair_cols[p]]], axis=0).astype(BF16)
            st_ref[b, p] = (dcol_s[b, c, p] * st[i]
                            + jnp.where(bd_mask, _dot(bkt_s[b, c, p], uv), 0.0))
        ys = [on_st[i][C:] + on_v[i][C:] + _dot(arb_s[b, sl, pair_cols[p]], bd(u[i]))
              for i, (b, p) in enumerate(chains)]
        for b in range(bb):
            y_s[b, sl, :] = jnp.concatenate(ys[b * n_pairs:(b + 1) * n_pairs], axis=1)
        return carry

    lax.fori_loop(0, n_chunks, step, 0)

    def finish(it, carry):
        items = [(b, pl.ds(pl.multiple_of((it * WKV_NORM_GROUP + j) * C, C), C))
                 for j in range(WKV_NORM_GROUP) for b in range(bb)]
        y = [y_s[b, sl, :] for b, sl in items]
        rkr = [r_ref[b, sl, :] * k_ref[b, sl, :] * rk_ref[...] for b, sl in items]
        sums = [_pair_head_sums([y[i], rkr[i]], ones_pair) for i in range(len(items))]
        yc = [y[i] - sums[i][0] * (1.0 / A_HEAD) for i in range(len(items))]
        var = [_pair_head_sums([x * x], ones_pair)[0] * (1.0 / A_HEAD) for x in yc]
        for i, (b, sl) in enumerate(items):
            yn = yc[i] * lax.rsqrt(var[i] + LNX_EPS) * lng_ref[...] + lnb_ref[...]
            out = (yn + sums[i][1] * v_ref[b, sl, :]) * g_ref[b, sl, :]
            o_ref[b, sl, :] = out.astype(o_ref.dtype)
        return carry

    lax.fori_loop(0, n_chunks // WKV_NORM_GROUP, finish, 0)


def _post_kernel(x_ref, ya_ref, ga_ref, ybg_ref, woa_ref, wout_ref, g2_ref, wrt_ref, brt_ref,
                 x2_out, hl_out, route_out, cnt_out):
    n_sub = x_ref.shape[0] // MOE_SUB
    groups = [slice(g * MOE_SUB, (g + 1) * MOE_SUB) for g in range(n_sub)]
    y_a = [_dot(ya_ref[p, :].astype(BF16), woa_ref[...]) for p in groups]
    x2 = [x_ref[p, :] + _dot((ga_ref[p, :] * y + ybg_ref[p, :]).astype(BF16), wout_ref[...])
          for p, y in zip(groups, y_a)]
    h2 = [_rms(x, g2_ref[...]) for x in x2]
    split = [_split2(h) for h in h2]
    on_hi = [_dot_nt(wrt_ref[...], hi) for hi, _ in split]
    on_lo = [_dot_nt(wrt_ref[0:LANES, :], lo) for _, lo in split]
    row = lax.broadcasted_iota(jnp.int32, (LANES, MOE_SUB), 0)
    neg = jnp.float32(-jnp.inf)
    big = jnp.int32(LANES)
    is_grp = (row >= N_EXPERTS) & (row < N_EXPERTS + N_GROUPS)
    top = lambda v: jnp.max(v, axis=0, keepdims=True)
    first = lambda hit: jnp.min(jnp.where(hit, row, big), axis=0, keepdims=True)
    picks, onehots, weights = [], [], []
    for g, p in enumerate(groups):
        x2_out[p, :] = x2[g].astype(x2_out.dtype)
        logits = on_hi[g][:LANES, :] + on_hi[g][LANES:, :] + on_lo[g] + brt_ref[...]
        gl = jnp.where(is_grp, logits, neg)
        gmax = top(gl)
        g_p = 1.0 / jnp.sum(jnp.exp(gl - gmax), axis=0, keepdims=True)
        lo_row = (first(gl == gmax) - N_EXPERTS) * EXPERTS_PER_GROUP
        el = jnp.where((row >= lo_row) & (row < lo_row + EXPERTS_PER_GROUP), logits, neg)
        e1 = top(el)
        i1 = first(el == e1)
        el2 = jnp.where(row == i1, neg, el)
        e2 = top(el2)
        i2 = first(el2 == e2)
        t = jnp.exp(e2 - e1)
        weights.append((g_p / (1.0 + t), g_p * t / (1.0 + t)))
        picks.append((i1, i2))
        onehots.append(jnp.where((row == i1) | (row == i2), 1.0, 0.0))

    earlier = (lax.broadcasted_iota(jnp.int32, (MOE_SUB, MOE_SUB), 0)
               <= lax.broadcasted_iota(jnp.int32, (MOE_SUB, MOE_SUB), 1)).astype(BF16)
    incl = [_dot(oh.astype(BF16), earlier) for oh in onehots]
    chunks = [jnp.ceil(x[:, MOE_SUB - 1:] * (1.0 / RUN_ALIGN)) for x in incl]
    below = (lax.broadcasted_iota(jnp.int32, (LANES, LANES), 1)
             < lax.broadcasted_iota(jnp.int32, (LANES, LANES), 0)).astype(BF16)
    wide = [jnp.broadcast_to(c, (LANES, LANES)) for c in chunks]
    run_start = [_dot(below, w.astype(BF16))[:, 0:1] * RUN_ALIGN for w in wide]
    row8 = lax.broadcasted_iota(jnp.int32, (8, LANES), 0)
    counts = jnp.zeros((8, LANES), F32)
    slot_ids = lax.broadcasted_iota(jnp.int32, (RUN_SLOTS, MOE_SUB), 0)
    for g, p in enumerate(groups):
        counts = jnp.where(row8 == g, wide[g].T[0:8, :], counts)
        slot_of = run_start[g] + incl[g] - onehots[g]
        i1, i2 = picks[g]
        slot1 = jnp.sum(jnp.where(row == i1, slot_of, 0.0), axis=0, keepdims=True)
        slot2 = jnp.sum(jnp.where(row == i2, slot_of, 0.0), axis=0, keepdims=True)
        w1, w2 = weights[g]
        packed = jnp.where(row == 0, slot1, jnp.where(row == 1, slot2, jnp.where(
            row == 2, w1, jnp.where(row == 3, w2, 0.0))))
        route_out[p, :] = packed.T
        pick = jnp.where((slot_ids == slot1.astype(jnp.int32))
                         | (slot_ids == slot2.astype(jnp.int32)), 1.0, 0.0).astype(BF16)
        hl_out[g * RUN_SLOTS:(g + 1) * RUN_SLOTS, :] = _dot(pick, h2[g].astype(BF16)).astype(BF16)
    cnt_out[...] = counts


def _expert_kernel(chunk_s, first_s, hl_in_ref, wg_ref, wu_ref, wd_ref, hl_ref, xbuf, wgu_s, wd_s,
                   sem, drain_s):
    del hl_in_ref
    e = pl.program_id(0)
    pass_chunks = EXPERT_ROWS // RUN_ALIGN
    mine = e % 2
    other = 1 - mine
    gather_sem = lambda buf: sem.at[buf]
    scatter_sem = lambda buf: sem.at[2 + buf]

    def first_pass(ex):
        lo = first_s[ex]
        return lo, jnp.minimum(lo + pass_chunks, first_s[ex + 1])

    def for_chunks(buf, c_lo, c_hi, fn):
        def one(c):
            fn(hl_ref.at[pl.ds(pl.multiple_of(chunk_s[c], RUN_ALIGN), RUN_ALIGN)],
               xbuf.at[buf, pl.ds(pl.multiple_of((c - c_lo) * RUN_ALIGN, RUN_ALIGN), RUN_ALIGN)])

        def block(i, carry):
            for u in range(CHUNK_UNROLL):
                one(c_lo + i * CHUNK_UNROLL + u)
            return carry

        def tail(c, carry):
            one(c)
            return carry

        n_blocks = (c_hi - c_lo) // CHUNK_UNROLL
        lax.fori_loop(0, n_blocks, block, 0)
        lax.fori_loop(c_lo + n_blocks * CHUNK_UNROLL, c_hi, tail, 0)

    def gather(buf, c_lo, c_hi):
        for_chunks(buf, c_lo, c_hi, lambda hbm, vm: pltpu.make_async_copy(
            hbm, vm, gather_sem(buf)).start())

    def scatter(buf, c_lo, c_hi):
        for_chunks(buf, c_lo, c_hi, lambda hbm, vm: pltpu.make_async_copy(
            vm, hbm, scatter_sem(buf)).start())

    def wait_chunks(which_sem, n):
        def wait_rows(rows):
            def body(c, carry):
                pltpu.make_async_copy(hl_ref.at[pl.ds(0, rows)], xbuf.at[0, pl.ds(0, rows)],
                                      which_sem).wait()
                return carry
            return body

        lax.fori_loop(0, n // CHUNK_UNROLL, wait_rows(CHUNK_UNROLL * RUN_ALIGN), 0)
        lax.fori_loop(0, n % CHUNK_UNROLL, wait_rows(RUN_ALIGN), 0)

    def evaluate(buf, n_chunks):
        def tile(k, carry):
            sl = pl.ds(pl.multiple_of(k * EXPERT_TILE, EXPERT_TILE), EXPERT_TILE)
            gu = _dot(xbuf[buf, sl, :], wgu_s[...])
            gate = gu[:, :D_EXPERT]
            act = gate * jax.nn.sigmoid(gate) * gu[:, D_EXPERT:]
            xbuf[buf, sl, :] = _dot(act.astype(BF16), wd_s[...]).astype(BF16)
            return carry

        lax.fori_loop(0, (n_chunks * RUN_ALIGN + EXPERT_TILE - 1) // EXPERT_TILE, tile, 0)

    @pl.when(e == 0)
    def _():
        xbuf[...] = jnp.zeros_like(xbuf)
        gather(0, *first_pass(0))
        drain_s[0] = 0

    wgu_s[:, :D_EXPERT] = wg_ref[0].astype(BF16)
    wgu_s[:, D_EXPERT:] = wu_ref[0].astype(BF16)
    wd_s[...] = wd_ref[0].astype(BF16)

    c_lo, c_hi = first_pass(e)
    wait_chunks(gather_sem(mine), c_hi - c_lo)
    wait_chunks(scatter_sem(other), drain_s[0])

    @pl.when(e + 1 < pl.num_programs(0))
    def _():
        gather(other, *first_pass(e + 1))

    evaluate(mine, c_hi - c_lo)
    scatter(mine, c_lo, c_hi)

    def later_pass(state):
        lo, pending = state
        hi = jnp.minimum(lo + pass_chunks, first_s[e + 1])
        wait_chunks(scatter_sem(mine), pending)
        gather(mine, lo, hi)
        wait_chunks(gather_sem(mine), hi - lo)
        evaluate(mine, hi - lo)
        scatter(mine, lo, hi)
        return hi, hi - lo

    _, pending = lax.while_loop(lambda st: st[0] < first_s[e + 1], later_pass,
                                (c_hi, c_hi - c_lo))
    drain_s[0] = pending

    @pl.when(e + 1 == pl.num_programs(0))
    def _():
        wait_chunks(scatter_sem(mine), pending)


def _combine_kernel(x2_ref, route_ref, gf_ref, yl_ref, o_ref):
    slots = lax.broadcasted_iota(jnp.int32, (MOE_SUB, RUN_SLOTS), 1).astype(F32)
    for g in range(x2_ref.shape[0] // MOE_SUB):
        p = slice(g * MOE_SUB, (g + 1) * MOE_SUB)
        route = route_ref[p, :]
        pick = (jnp.where(slots == route[:, 0:1], route[:, 2:3], 0.0)
                + jnp.where(slots == route[:, 1:2], route[:, 3:4], 0.0))
        y = _dot(pick.astype(BF16), yl_ref[g * RUN_SLOTS:(g + 1) * RUN_SLOTS, :])
        o_ref[p, :] = _rms(x2_ref[p, :] + y, gf_ref[...])


def _full(shape):
    return pl.BlockSpec(shape, lambda *_: (0,) * len(shape))


def _params(sem, vmem_limit=VMEM_LIMIT):
    return pltpu.CompilerParams(dimension_semantics=sem, vmem_limit_bytes=vmem_limit)


def _mixer_a_layout(t):
    s_rkv = 3 * A_WIDTH
    s_w = s_rkv + D_DECAY_LORA
    s_a = s_w + D_AAA_LORA
    gap = lambda n: jnp.zeros((t.shape[0], n), t.dtype)
    return jnp.concatenate(
        [t[:, :s_w], gap(XA_OFF - D_DECAY_LORA), t[:, s_w:s_a], gap(XG_OFF - XA_OFF - D_AAA_LORA),
         t[:, s_a:], gap(LORA_PAD - XG_OFF - D_GATE_LORA)], axis=1)


def _place(cols, parts):
    out = jnp.zeros((parts[0][1].shape[0], cols), parts[0][1].dtype)
    for off, arr in parts:
        out = lax.dynamic_update_slice(out, arr, (0, off))
    return out


def kernel(x, norm1_g, w_in, b_gate, tmix_mu, w0, w2, a0, a2, g2, k_k, k_a, r_k, lnx_g, lnx_b,
           w_oA, lnv_g, lnv_b, w_s, b_s, w_oB, w_out, norm2_g, w_rg, b_rg, w_re, b_re,
           w_e_gate, w_e_up, w_e_down, final_g):
    bsz, seq, d = x.shape
    n_tok = bsz * seq
    depth = norm1_g.shape[0]
    assert depth == 1, "the moe kernel fuses the final norm, so it must be the last layer"
    assert bsz % WKV_SEQS == 0 and seq % WKV_TOKENS == 0
    xf = x.reshape(n_tok, d)

    s_rkv = 3 * A_WIDTH
    s_w = s_rkv + D_DECAY_LORA
    s_a = s_w + D_AAA_LORA
    a_cols = s_a + D_GATE_LORA
    b_cols = 2 * B_WIDTH

    ones_bd = (jnp.arange(A_WIDTH)[:, None] // A_HEAD
               == jnp.arange(A_WIDTH)[None, :] // A_HEAD).astype(BF16)

    tm_a = INPROJ_TILE
    tm_p = POST_TILE
    tm_c = COMBINE_TILE
    assert tm_p % MOE_SUB == 0 and tm_c % MOE_SUB == 0 and tm_a % DENSE_SPLIT == 0
    assert seq % tm_a == 0, "an inproj tile must not straddle two sequences (token shift)"

    for l in range(depth):
        wl = w_in[l]
        n_w = 4
        col_blk = lambda rows: pl.BlockSpec((rows, d // n_w), lambda i: (0, i))
        row_blk = lambda rows: pl.BlockSpec((rows // n_w, d), lambda i: (i, 0))
        w_a, w_b, w_g, w_oa, w_ob, w_o = pl.pallas_call(
            _w_in_kernel,
            grid=(n_w,),
            in_specs=[col_blk(wl.shape[1]), row_blk(A_WIDTH), row_blk(B_WIDTH), row_blk(d)],
            out_specs=[col_blk(A_PROJ), col_blk(b_cols), col_blk(2 * d),
                       row_blk(A_WIDTH), row_blk(B_WIDTH), row_blk(d)],
            out_shape=[jax.ShapeDtypeStruct((A_PROJ, d), BF16),
                       jax.ShapeDtypeStruct((b_cols, d), BF16),
                       jax.ShapeDtypeStruct((2 * d, d), BF16),
                       jax.ShapeDtypeStruct((A_WIDTH, d), BF16),
                       jax.ShapeDtypeStruct((B_WIDTH, d), BF16),
                       jax.ShapeDtypeStruct((d, d), BF16)],
            compiler_params=_params(("parallel",)),
            name="w_in_layout",
        )(wl.T, w_oA[l], w_oB[l], w_out[l])
        mu_a = _mixer_a_layout(tmix_mu[l][None, :])
        w2p = jnp.pad(w2[l], ((0, LANES - D_DECAY_LORA), (0, 0))).astype(BF16)
        a2p = jnp.pad(a2[l], ((0, LANES - D_AAA_LORA), (0, 0))).astype(BF16)
        g2p = jnp.pad(g2[l], ((0, LORA_PAD - XG_OFF - D_GATE_LORA), (0, 0))).astype(BF16)
        g1 = norm1_g[l][None, :]

        row512 = lambda i: (i, 0)
        tok_a = pl.BlockSpec((tm_a, A_WIDTH), row512)
        tok_d = pl.BlockSpec((tm_a, d), row512)
        vec_a = _full((1, A_WIDTH))
        bs_full = jnp.repeat(b_s[l].T, B_GROUP_CH, axis=1)
        outs = pl.pallas_call(
            functools.partial(_inproj_kernel, seq // tm_a),
            grid=(n_tok // tm_a,),
            in_specs=[tok_d,
                      _full((1, d)), _full((A_PROJ, d)), _full((1, A_PROJ)), vec_a,
                      _full((LANES, A_WIDTH)), vec_a, _full((LANES, A_WIDTH)),
                      _full((LORA_PAD - XG_OFF, A_WIDTH)), vec_a, vec_a,
                      _full((A_WIDTH, A_WIDTH)),
                      _full((b_cols, d)), _full((2 * d, d)), _full((1, 2 * d)),
                      _full((1, B_WIDTH)), _full((1, B_WIDTH)),
                      _full((B_GROUPS, GMLP_BLOCK, GMLP_BLOCK)), _full((GMLP_BLOCK, B_WIDTH)),
                      _full((B_WIDTH, d))],
            out_specs=[tok_a] * 7 + [tok_d, tok_d],
            out_shape=[jax.ShapeDtypeStruct((n_tok, A_WIDTH), F32)] * 7
            + [jax.ShapeDtypeStruct((n_tok, d), BF16)] * 2,
            scratch_shapes=[pltpu.VMEM((8, A_PROJ), F32)],
            compiler_params=_params(("arbitrary",)),
            name="inproj",
        )(xf, g1, w_a, mu_a, w0[l][None, :], w2p, a0[l][None, :], a2p, g2p,
          k_k[l][None, :], k_a[l][None, :], ones_bd, w_b, w_g, b_gate[l][None, :],
          lnv_g[l][None, :], lnv_b[l][None, :], w_s[l], bs_full, w_ob)
        r_, k_, v_, a_, b_, ld_, gg_, ybg, ga = outs

        n_pairs = A_WIDTH // HEAD_PAIR
        n_chunks = WKV_TOKENS // WKV_CHUNK
        tok_w = pl.BlockSpec((WKV_SEQS, WKV_TOKENS, A_WIDTH), lambda bi, ti: (bi, ti, 0))
        vec_w = _full((1, A_WIDTH))
        seq_major = lambda t: t.reshape(bsz, seq, A_WIDTH)
        tok_scratch = pltpu.VMEM((WKV_SEQS, WKV_TOKENS, A_WIDTH), BF16)
        ya_in = pl.pallas_call(
            _wkv_kernel,
            grid=(bsz // WKV_SEQS, seq // WKV_TOKENS),
            in_specs=[tok_w] * 7 + [vec_w, vec_w, vec_w, _full((HEAD_PAIR, HEAD_PAIR))],
            out_specs=tok_w,
            out_shape=jax.ShapeDtypeStruct((bsz, seq, A_WIDTH), BF16),
            scratch_shapes=[pltpu.VMEM((WKV_SEQS, n_pairs, HEAD_PAIR, HEAD_PAIR), F32)]
            + [tok_scratch] * 5
            + [pltpu.VMEM((WKV_SEQS, n_chunks, n_pairs, HEAD_PAIR, HEAD_PAIR), BF16),
               pltpu.VMEM((WKV_SEQS, n_chunks, n_pairs, HEAD_PAIR, HEAD_PAIR), F32),
               pltpu.VMEM((WKV_SEQS, WKV_TOKENS, A_WIDTH), F32)],
            compiler_params=_params(("parallel", "arbitrary")),
            name="wkv",
        )(*[seq_major(t) for t in (r_, k_, v_, a_, b_, ld_, gg_)], r_k[l].reshape(1, A_WIDTH),
          lnx_g[l][None, :], lnx_b[l][None, :],
          ones_bd[:HEAD_PAIR, :HEAD_PAIR]).reshape(n_tok, A_WIDTH)

        w_r = _place(LANES, [(0, jnp.transpose(w_re[l], (1, 0, 2)).reshape(d, N_EXPERTS)),
                             (N_EXPERTS, w_rg[l])])
        b_r = _place(LANES, [(0, b_re[l].reshape(1, N_EXPERTS)), (N_EXPERTS, b_rg[l][None, :])])
        wr_hi = w_r.astype(BF16)
        wr_lo = (w_r - wr_hi.astype(F32)).astype(BF16)
        wr_t = jnp.concatenate([wr_hi.T, wr_lo.T], axis=0)
        tok_p = pl.BlockSpec((tm_p, d), row512)
        lane_p = pl.BlockSpec((tm_p, LANES), row512)
        n_sub = n_tok // MOE_SUB
        sorted_rows = tm_p // MOE_SUB * RUN_SLOTS
        x2, h_sorted, route, cnt = pl.pallas_call(
            _post_kernel,
            grid=(n_tok // tm_p,),
            in_specs=[tok_p, pl.BlockSpec((tm_p, A_WIDTH), row512), tok_p, tok_p,
                      _full((A_WIDTH, d)), _full((d, d)), _full((1, d)), _full((2 * LANES, d)),
                      _full((LANES, 1))],
            out_specs=[tok_p, pl.BlockSpec((sorted_rows, d), row512), lane_p,
                       pl.BlockSpec((8, LANES), row512)],
            out_shape=[jax.ShapeDtypeStruct((n_tok, d), F32),
                       jax.ShapeDtypeStruct((n_sub * RUN_SLOTS, d), BF16),
                       jax.ShapeDtypeStruct((n_tok, LANES), F32),
                       jax.ShapeDtypeStruct((n_tok // tm_p * 8, LANES), F32)],
            compiler_params=_params(("parallel",), POST_VMEM_LIMIT),
            name="post",
        )(xf, ya_in, ga, ybg, w_oa, w_o,
          norm2_g[l][None, :], wr_t, b_r.T)

        run_chunks = cnt.reshape(n_tok // tm_p, 8, LANES)[:, :tm_p // MOE_SUB, :N_EXPERTS]
        run_chunks = run_chunks.reshape(n_sub, N_EXPERTS).astype(jnp.int32)
        run_start = (jnp.cumsum(run_chunks, axis=1) - run_chunks) * RUN_ALIGN
        group_end = jnp.cumsum(run_chunks, axis=0)
        first_chunk = jnp.concatenate([jnp.zeros((1,), jnp.int32), jnp.cumsum(group_end[-1])])
        pos = jnp.arange(n_sub * RUN_SLOTS // RUN_ALIGN, dtype=jnp.int32)
        e_of = jnp.minimum(jnp.sum(pos[:, None] >= first_chunk[None, 1:], axis=1), N_EXPERTS - 1)
        of_e = (e_of[:, None] == jnp.arange(N_EXPERTS, dtype=jnp.int32)[None, :])
        column = lambda t: jnp.dot(of_e.astype(F32), t.T.astype(F32),
                                   precision=lax.Precision.HIGHEST).astype(jnp.int32)
        in_expert = pos - jnp.sum(jnp.where(of_e, first_chunk[None, :-1], 0), axis=1)
        ends = column(group_end)
        g_of = jnp.minimum(jnp.sum(in_expert[:, None] >= ends, axis=1), n_sub - 1)
        of_g = g_of[:, None] == jnp.arange(n_sub, dtype=jnp.int32)[None, :]
        at_g = lambda t: jnp.sum(jnp.where(of_g, t, 0), axis=1)
        in_run = in_expert - at_g(ends - column(run_chunks))
        chunk_rows = g_of * RUN_SLOTS + at_g(column(run_start)) + in_run * RUN_ALIGN

        per_expert = lambda e, *_: (e, 0, 0)
        y_sorted = pl.pallas_call(
            _expert_kernel,
            grid_spec=pltpu.PrefetchScalarGridSpec(
                num_scalar_prefetch=2, grid=(N_EXPERTS,),
                in_specs=[pl.BlockSpec(memory_space=pl.ANY),
                          pl.BlockSpec((1, d, D_EXPERT), per_expert),
                          pl.BlockSpec((1, d, D_EXPERT), per_expert),
                          pl.BlockSpec((1, D_EXPERT, d), per_expert)],
                out_specs=pl.BlockSpec(memory_space=pl.ANY),
                scratch_shapes=[pltpu.VMEM((2, EXPERT_ROWS, d), BF16),
                                pltpu.VMEM((d, 2 * D_EXPERT), BF16),
                                pltpu.VMEM((D_EXPERT, d), BF16),
                                pltpu.SemaphoreType.DMA((4,)),
                                pltpu.SMEM((1,), jnp.int32)]),
            out_shape=jax.ShapeDtypeStruct((n_sub * RUN_SLOTS, d), BF16),
            input_output_aliases={2: 0},
            compiler_params=_params(("arbitrary",)),
            name="moe_experts",
        )(chunk_rows, first_chunk, h_sorted, w_e_gate[l], w_e_up[l], w_e_down[l])

        xf = pl.pallas_call(
            _combine_kernel,
            grid=(n_tok // tm_c,),
            in_specs=[pl.BlockSpec((tm_c, d), row512), pl.BlockSpec((tm_c, LANES), row512),
                      _full((1, d)), pl.BlockSpec((tm_c // MOE_SUB * RUN_SLOTS, d), row512)],
            out_specs=pl.BlockSpec((tm_c, d), row512),
            out_shape=jax.ShapeDtypeStruct((n_tok, d), F32),
            compiler_params=_params(("parallel",)),
            name="moe_combine",
        )(x2, route, final_g[None, :], y_sorted)

    return xf.reshape(bsz, seq, d)
```

```python
import functools

import jax
import jax.numpy as jnp
from jax import lax
from jax.experimental import pallas as pl
from jax.experimental.pallas import tpu as pltpu

F32 = jnp.float32
BF16 = jnp.bfloat16

D_MODEL = 1024
A_WIDTH = 512
A_HEAD = 64
D_DECAY_LORA = 64
D_AAA_LORA = 64
D_GATE_LORA = 160
B_WIDTH = 512
B_GROUPS = 4
B_GROUP_CH = 128
GMLP_BLOCK = 128
N_GROUPS = 4
EXPERTS_PER_GROUP = 8
N_EXPERTS = 32
D_EXPERT = 256
NORM_EPS = 1e-6
LN_EPS = 1e-5
LNX_EPS = 64e-5

LANES = 128
LORA_PAD = 512
XW_OFF, XA_OFF, XG_OFF = 0, 128, 256
A_PROJ = 3 * A_WIDTH + LORA_PAD
WKV_CHUNK = 64
HEAD_PAIR = 2 * A_HEAD
WKV_SEQS = 8
WKV_TOKENS = 64
WKV_PREP_GROUP = 8
WKV_NORM_GROUP = 1
MOE_SUB = 256
RUN_ALIGN = 16
RUN_SLOTS = 2 * MOE_SUB + N_EXPERTS * RUN_ALIGN
EXPERT_ROWS = 2048
EXPERT_TILE = 512
INPROJ_TILE = 512
POST_TILE = 1024
COMBINE_TILE = 1024
DENSE_SPLIT = 256
CHUNK_UNROLL = 8
VMEM_LIMIT = 48 * 1024 * 1024
POST_VMEM_LIMIT = 56 * 1024 * 1024


def _rms(x, g):
    return x * lax.rsqrt(jnp.mean(x * x, axis=-1, keepdims=True) + NORM_EPS) * g


def _dot(a, b):
    return jnp.dot(a, b, preferred_element_type=F32)


def _dot_nt(a, b):
    return lax.dot_general(a, b, (((1,), (1,)), ((), ())), preferred_element_type=F32)


def _split2(x):
    hi = x.astype(BF16)
    lo = (x - hi.astype(F32)).astype(BF16)
    return hi, lo


def _split3(x):
    hi = x.astype(BF16)
    r1 = x - hi.astype(F32)
    mid = r1.astype(BF16)
    lo = (r1 - mid.astype(F32)).astype(BF16)
    return hi, mid, lo


def _pair_head_sums(xs, ones_pair):
    n_tiles = A_WIDTH // HEAD_PAIR
    rows = xs[0].shape[0]
    tiles = [x[:, t * HEAD_PAIR:(t + 1) * HEAD_PAIR].astype(BF16) for x in xs for t in range(n_tiles)]
    sums = _dot(jnp.concatenate(tiles, axis=0), ones_pair)
    return [jnp.concatenate([sums[(i * n_tiles + t) * rows:(i * n_tiles + t + 1) * rows]
                             for t in range(n_tiles)], axis=1) for i in range(len(xs))]


def _w_in_kernel(w_ref, woa_ref, wob_ref, wout_ref, wa_out, wb_out, wg_out, woa_out, wob_out,
                 wout_out):
    woa_out[...] = woa_ref[...].astype(BF16)
    wob_out[...] = wob_ref[...].astype(BF16)
    wout_out[...] = wout_ref[...].astype(BF16)
    s_rkv = 3 * A_WIDTH
    s_w = s_rkv + D_DECAY_LORA
    s_a = s_w + D_AAA_LORA
    a_cols = s_a + D_GATE_LORA
    b_end = a_cols + 2 * B_WIDTH
    wa_out[...] = jnp.zeros_like(wa_out)
    wa_out[:s_w, :] = w_ref[:s_w, :].astype(BF16)
    wa_out[s_rkv + XA_OFF:s_rkv + XA_OFF + D_AAA_LORA, :] = w_ref[s_w:s_a, :].astype(BF16)
    wa_out[s_rkv + XG_OFF:s_rkv + XG_OFF + D_GATE_LORA, :] = w_ref[s_a:a_cols, :].astype(BF16)
    wb_out[...] = w_ref[a_cols:b_end, :].astype(BF16)
    wg_out[...] = w_ref[b_end:, :].astype(BF16)


def _inproj_kernel(tiles_per_seq, x_ref, g1_ref, wa_ref, mu_ref, w0_ref, w2_ref, a0_ref, a2_ref,
                   g2_ref, kk_ref, ka_ref, ones_ref, wb_ref, wg_ref, bg_ref, lng_ref, lnb_ref,
                   ws_ref, bs_ref, wo_ref,
                   r_out, k_out, v_out, a_out, b_out, ld_out, g_out, ybg_out, ga_out, tail_ref):
    i = pl.program_id(0)
    tm = x_ref.shape[0]
    parts = [slice(j * DENSE_SPLIT, (j + 1) * DENSE_SPLIT) for j in range(tm // DENSE_SPLIT)]
    g1 = g1_ref[...]
    h = [_rms(x_ref[p, :], g1).astype(BF16) for p in parts]
    proj = [_dot_nt(hh, wa_ref[...]) for hh in h]
    pb = [_dot_nt(hh, wb_ref[...]) for hh in h]

    prev = [jnp.where(i % tiles_per_seq == 0, 0.0, tail_ref[7:8, :])]
    prev += [p[DENSE_SPLIT - 1:, :] for p in proj[:-1]]
    tail_ref[...] = proj[-1][DENSE_SPLIT - 8:, :]
    row = lax.broadcasted_iota(jnp.int32, proj[0].shape, 0)
    pm = []
    for p, pv in zip(proj, prev):
        shifted = jnp.where(row == 0, pv, pltpu.roll(p, 1, axis=0))
        pm.append(p + mu_ref[...] * (shifted - p))
    gates = [jax.nn.sigmoid(_dot_nt(hh, wg_ref[...]) + bg_ref[...]) for hh in h]

    tri = (lax.broadcasted_iota(jnp.int32, (GMLP_BLOCK, GMLP_BLOCK), 0)
           >= lax.broadcasted_iota(jnp.int32, (GMLP_BLOCK, GMLP_BLOCK), 1))
    ws = [jnp.where(tri, ws_ref[grp], 0.0).astype(BF16) for grp in range(B_GROUPS)]
    bs = bs_ref[...]
    us, vns = [], []
    for x in pb:
        z = 0.5 * x * (1.0 + lax.erf(x * (2.0 ** -0.5)))
        us.append(z[:, :B_WIDTH])
        v = z[:, B_WIDTH:]
        mean = jnp.mean(v, axis=-1, keepdims=True)
        vc = v - mean
        var = jnp.mean(vc * vc, axis=-1, keepdims=True)
        vns.append((vc * lax.rsqrt(var + LN_EPS) * lng_ref[...] + lnb_ref[...]).astype(BF16))

    lora = [x[:, 3 * A_WIDTH:] for x in pm]
    decay = [_dot(jnp.tanh(x[:, XW_OFF:XW_OFF + LANES]).astype(BF16), w2_ref[...]) for x in lora]
    rate = [_dot(x[:, XA_OFF:XA_OFF + LANES].astype(BF16), a2_ref[...]) for x in lora]
    gate = [_dot(jax.nn.sigmoid(x[:, XG_OFF:]).astype(BF16), g2_ref[...]) for x in lora]
    kks = [x[:, A_WIDTH:2 * A_WIDTH] * kk_ref[...] for x in pm]
    sq = [_dot((kk * kk).astype(BF16), ones_ref[...]) for kk in kks]

    svs = []
    for vn in vns:
        rows = []
        for blk in range(DENSE_SPLIT // GMLP_BLOCK):
            cols = [_dot(ws[grp], vn[blk * GMLP_BLOCK:(blk + 1) * GMLP_BLOCK,
                                     grp * B_GROUP_CH:(grp + 1) * B_GROUP_CH])
                    for grp in range(B_GROUPS)]
            rows.append(jnp.concatenate(cols, axis=1) + bs)
        svs.append(jnp.concatenate(rows, axis=0))
    yb = [_dot((u * sv).astype(BF16), wo_ref[...]) for u, sv in zip(us, svs)]

    for j, p in enumerate(parts):
        k = pm[j][:, A_WIDTH:2 * A_WIDTH]
        z = -(w0_ref[...] + decay[j])
        softplus = jnp.maximum(z, 0.0) + jnp.log(1.0 + jnp.exp(-jnp.abs(z)))
        w = -softplus - 0.5
        a_lr = jax.nn.sigmoid(a0_ref[...] + rate[j])
        kk = kks[j] / jnp.maximum(jnp.sqrt(sq[j]), 1e-12)
        r_out[p, :] = pm[j][:, 0:A_WIDTH]
        k_out[p, :] = k * (1.0 + (a_lr - 1.0) * ka_ref[...])
        v_out[p, :] = pm[j][:, 2 * A_WIDTH:3 * A_WIDTH]
        a_out[p, :] = -kk
        b_out[p, :] = kk * a_lr
        ld_out[p, :] = -jnp.exp(w)
        g_out[p, :] = gate[j]
        ga_out[p, :] = gates[j][:, :D_MODEL].astype(BF16)
        ybg_out[p, :] = (gates[j][:, D_MODEL:] * yb[j]).astype(BF16)


def _wkv_kernel(r_ref, k_ref, v_ref, a_ref, b_ref, ld_ref, g_ref, rk_ref, lng_ref, lnb_ref,
                ones_ref, o_ref, st_ref, ta_s, tl_s, arb_s, ark_s, rt_s, bkt_s, dcol_s, y_s):
    C = WKV_CHUNK
    bb, tb, _ = r_ref.shape
    n_chunks = tb // C
    n_pairs = A_WIDTH // HEAD_PAIR

    @pl.when(pl.program_id(1) == 0)
    def _():
        st_ref[...] = jnp.zeros_like(st_ref)

    row = lax.broadcasted_iota(jnp.int32, (C, HEAD_PAIR), 0)
    src = lax.broadcasted_iota(jnp.int32, (C, HEAD_PAIR), 1) & (C - 1)
    incl = src <= row
    strict = src < row
    eye_pair = jnp.where(src == row, 1.0, 0.0)
    bd_mask = ((lax.broadcasted_iota(jnp.int32, (HEAD_PAIR, HEAD_PAIR), 0) >= A_HEAD)
               == (lax.broadcasted_iota(jnp.int32, (HEAD_PAIR, HEAD_PAIR), 1) >= A_HEAD))
    tri_c = (lax.broadcasted_iota(jnp.int32, (C, C), 0)
             >= lax.broadcasted_iota(jnp.int32, (C, C), 1)).astype(BF16)
    ones_pair = ones_ref[...]
    pair_cols = [slice(p * HEAD_PAIR, (p + 1) * HEAD_PAIR) for p in range(n_pairs)]

    def bd(x):
        xb = x.astype(BF16)
        return jnp.where(bd_mask, jnp.concatenate([xb, xb], axis=0), jnp.zeros((), BF16))

    first_row = lax.broadcasted_iota(jnp.int32, (C, 1), 0) == 0
    tri_cc = jnp.concatenate([tri_c, tri_c], axis=1)

    def prep(it, carry):
        where, lhs, rhs, ats = [], [], [], []
        for j in range(WKV_PREP_GROUP):
            flat = it * WKV_PREP_GROUP + j
            b = flat // n_chunks
            c = flat % n_chunks
            sl = pl.ds(pl.multiple_of(c * C, C), C)
            ld = ld_ref[b, sl, :]
            hi, lo = _split2(ld)
            cl = _dot(tri_cc, jnp.concatenate([hi, lo], axis=0))
            cl_end = cl[C - 1:C, :]
            k = k_ref[b, sl, :]
            bv = b_ref[b, sl, :]
            d_inv = jnp.exp(-cl)
            d_end = jnp.exp(cl_end)
            d_tail = d_end * d_inv
            d_cum = jnp.exp(cl)
            rt = (r_ref[b, sl, :] * d_cum).astype(BF16)
            rt_s[b, sl, :] = rt
            kt = k * d_inv
            bt = bv * d_inv
            d_prev = jnp.where(first_row, 1.0, pltpu.roll(d_cum, 1, 0))
            at = a_ref[b, sl, :] * d_prev
            kd = k * d_tail
            bdk = bv * d_tail
            for p, cs in enumerate(pair_cols):
                bkt_s[b, c, p] = jnp.concatenate([bdk[:, cs], kd[:, cs]], axis=0).T.astype(BF16)
                dcol_s[b, c, p] = jnp.broadcast_to(d_end[:, cs], (HEAD_PAIR, HEAD_PAIR)).T
                where.append((b, sl, cs))
                ats.append(at[:, cs])
                lhs.append(jnp.concatenate([rt[:, cs], at[:, cs].astype(BF16)], axis=0))
                rhs.append(jnp.concatenate([bd(kt[:, cs]), bd(bt[:, cs])], axis=0))
        n = len(where)
        amat = [_dot_nt(lhs[i], rhs[i]) for i in range(n)]
        l_ak, l_ab = [], []
        for i, (b, sl, cs) in enumerate(where):
            ark_s[b, sl, cs] = jnp.where(incl, amat[i][:C, :HEAD_PAIR], 0.0).astype(BF16)
            arb_s[b, sl, cs] = jnp.where(incl, amat[i][:C, HEAD_PAIR:], 0.0).astype(BF16)
            l_ak.append(jnp.where(strict, amat[i][C:, :HEAD_PAIR], 0.0))
            l_ab.append(jnp.where(strict, amat[i][C:, HEAD_PAIR:], 0.0))

        t_mat = [eye_pair + l for l in l_ab]
        q = [_dot(l.astype(BF16), bd(l)) for l in l_ab]
        n_sq = 1
        while 2 * n_sq < C // 2:
            both = [_dot(q[i].astype(BF16), jnp.concatenate([bd(q[i]), bd(t_mat[i])], axis=1))
                    for i in range(n)]
            q = [x[:, :HEAD_PAIR] for x in both]
            t_mat = [t_mat[i] + both[i][:, HEAD_PAIR:] for i in range(n)]
            n_sq *= 2
        t_mat = [t_mat[i] + _dot(q[i].astype(BF16), bd(t_mat[i])) for i in range(n)]
        tal = [_dot(t_mat[i].astype(BF16), jnp.concatenate([bd(ats[i]), bd(l_ak[i])], axis=1))
               for i in range(n)]
        for i, (b, sl, cs) in enumerate(where):
            ta_s[b, sl, cs] = tal[i][:, :HEAD_PAIR].astype(BF16)
            tl_s[b, sl, cs] = tal[i][:, HEAD_PAIR:].astype(BF16)
        return carry

    lax.fori_loop(0, bb * n_chunks // WKV_PREP_GROUP, prep, 0)

    def step(c, carry):
        sl = pl.ds(pl.multiple_of(c * C, C), C)
        chains = [(b, p) for b in range(bb) for p in range(n_pairs)]
        v = [v_ref[b, sl, :] for b in range(bb)]
        st = [st_ref[b, p] for b, p in chains]
        st_b = [x.astype(BF16) for x in st]
        bd_v = [bd(v[b][:, pair_cols[p]]) for b, p in chains]
        on_st = [_dot(jnp.concatenate([ta_s[b, sl, pair_cols[p]], rt_s[b, sl, pair_cols[p]]],
                                      axis=0), st_b[i]) for i, (b, p) in enumerate(chains)]
        on_v = [_dot(jnp.concatenate([tl_s[b, sl, pair_cols[p]], ark_s[b, sl, pair_cols[p]]],
                                     axis=0), bd_v[i]) for i, (b, p) in enumerate(chains)]
        u = [on_st[i][:C] + on_v[i][:C] for i in range(len(chains))]
        for i, (b, p) in enumerate(chains):
            uv = jnp.concatenate([u[i], v[b][:, pair_cols[p]]], axis=0).astype(BF16)
            st_ref[b, p] = (dcol_s[b, c, p] * st[i]
                            + jnp.where(bd_mask, _dot(bkt_s[b, c, p], uv), 0.0))
        ys = [on_st[i][C:] + on_v[i][C:] + _dot(arb_s[b, sl, pair_cols[p]], bd(u[i]))
              for i, (b, p) in enumerate(chains)]
        for b in range(bb):
            y_s[b, sl, :] = jnp.concatenate(ys[b * n_pairs:(b + 1) * n_pairs], axis=1)
        return carry

    lax.fori_loop(0, n_chunks, step, 0)

    def finish(it, carry):
        items = [(b, pl.ds(pl.multiple_of((it * WKV_NORM_GROUP + j) * C, C), C))
                 for j in range(WKV_NORM_GROUP) for b in range(bb)]
        y = [y_s[b, sl, :] for b, sl in items]
        rkr = [r_ref[b, sl, :] * k_ref[b, sl, :] * rk_ref[...] for b, sl in items]
        sums = [_pair_head_sums([y[i], rkr[i]], ones_pair) for i in range(len(items))]
        yc = [y[i] - sums[i][0] * (1.0 / A_HEAD) for i in range(len(items))]
        var = [_pair_head_sums([x * x], ones_pair)[0] * (1.0 / A_HEAD) for x in yc]
        for i, (b, sl) in enumerate(items):
            yn = yc[i] * lax.rsqrt(var[i] + LNX_EPS) * lng_ref[...] + lnb_ref[...]
            out = (yn + sums[i][1] * v_ref[b, sl, :]) * g_ref[b, sl, :]
            o_ref[b, sl, :] = out.astype(o_ref.dtype)
        return carry

    lax.fori_loop(0, n_chunks // WKV_NORM_GROUP, finish, 0)


def _post_kernel(x_ref, ya_ref, ga_ref, ybg_ref, woa_ref, wout_ref, g2_ref, wrt_ref, brt_ref,
                 x2_out, hl_out, route_out, cnt_out):
    n_sub = x_ref.shape[0] // MOE_SUB
    groups = [slice(g * MOE_SUB, (g + 1) * MOE_SUB) for g in range(n_sub)]
    y_a = [_dot(ya_ref[p, :].astype(BF16), woa_ref[...]) for p in groups]
    x2 = [x_ref[p, :] + _dot((ga_ref[p, :] * y + ybg_ref[p, :]).astype(BF16), wout_ref[...])
          for p, y in zip(groups, y_a)]
    h2 = [_rms(x, g2_ref[...]) for x in x2]
    split = [_split2(h) for h in h2]
    on_hi = [_dot_nt(wrt_ref[...], hi) for hi, _ in split]
    on_lo = [_dot_nt(wrt_ref[0:LANES, :], lo) for _, lo in split]
    row = lax.broadcasted_iota(jnp.int32, (LANES, MOE_SUB), 0)
    neg = jnp.float32(-jnp.inf)
    big = jnp.int32(LANES)
    is_grp = (row >= N_EXPERTS) & (row < N_EXPERTS + N_GROUPS)
    top = lambda v: jnp.max(v, axis=0, keepdims=True)
    first = lambda hit: jnp.min(jnp.where(hit, row, big), axis=0, keepdims=True)
    picks, onehots, weights = [], [], []
    for g, p in enumerate(groups):
        x2_out[p, :] = x2[g].astype(x2_out.dtype)
        logits = on_hi[g][:LANES, :] + on_hi[g][LANES:, :] + on_lo[g] + brt_ref[...]
        gl = jnp.where(is_grp, logits, neg)
        gmax = top(gl)
        g_p = 1.0 / jnp.sum(jnp.exp(gl - gmax), axis=0, keepdims=True)
        lo_row = (first(gl == gmax) - N_EXPERTS) * EXPERTS_PER_GROUP
        el = jnp.where((row >= lo_row) & (row < lo_row + EXPERTS_PER_GROUP), logits, neg)
        e1 = top(el)
        i1 = first(el == e1)
        el2 = jnp.where(row == i1, neg, el)
        e2 = top(el2)
        i2 = first(el2 == e2)
        t = jnp.exp(e2 - e1)
        weights.append((g_p / (1.0 + t), g_p * t / (1.0 + t)))
        picks.append((i1, i2))
        onehots.append(jnp.where((row == i1) | (row == i2), 1.0, 0.0))

    earlier = (lax.broadcasted_iota(jnp.int32, (MOE_SUB, MOE_SUB), 0)
               <= lax.broadcasted_iota(jnp.int32, (MOE_SUB, MOE_SUB), 1)).astype(BF16)
    incl = [_dot(oh.astype(BF16), earlier) for oh in onehots]
    chunks = [jnp.ceil(x[:, MOE_SUB - 1:] * (1.0 / RUN_ALIGN)) for x in incl]
    below = (lax.broadcasted_iota(jnp.int32, (LANES, LANES), 1)
             < lax.broadcasted_iota(jnp.int32, (LANES, LANES), 0)).astype(BF16)
    wide = [jnp.broadcast_to(c, (LANES, LANES)) for c in chunks]
    run_start = [_dot(below, w.astype(BF16))[:, 0:1] * RUN_ALIGN for w in wide]
    row8 = lax.broadcasted_iota(jnp.int32, (8, LANES), 0)
    counts = jnp.zeros((8, LANES), F32)
    slot_ids = lax.broadcasted_iota(jnp.int32, (RUN_SLOTS, MOE_SUB), 0)
    for g, p in enumerate(groups):
        counts = jnp.where(row8 == g, wide[g].T[0:8, :], counts)
        slot_of = run_start[g] + incl[g] - onehots[g]
        i1, i2 = picks[g]
        slot1 = jnp.sum(jnp.where(row == i1, slot_of, 0.0), axis=0, keepdims=True)
        slot2 = jnp.sum(jnp.where(row == i2, slot_of, 0.0), axis=0, keepdims=True)
        w1, w2 = weights[g]
        packed = jnp.where(row == 0, slot1, jnp.where(row == 1, slot2, jnp.where(
            row == 2, w1, jnp.where(row == 3, w2, 0.0))))
        route_out[p, :] = packed.T
        pick = jnp.where((slot_ids == slot1.astype(jnp.int32))
                         | (slot_ids == slot2.astype(jnp.int32)), 1.0, 0.0).astype(BF16)
        hl_out[g * RUN_SLOTS:(g + 1) * RUN_SLOTS, :] = _dot(pick, h2[g].astype(BF16)).astype(BF16)
    cnt_out[...] = counts


def _expert_kernel(chunk_s, first_s, hl_in_ref, wg_ref, wu_ref, wd_ref, hl_ref, xbuf, wgu_s, wd_s,
                   sem, drain_s):
    del hl_in_ref
    e = pl.program_id(0)
    pass_chunks = EXPERT_ROWS // RUN_ALIGN
    mine = e % 2
    other = 1 - mine
    gather_sem = lambda buf: sem.at[buf]
    scatter_sem = lambda buf: sem.at[2 + buf]

    def first_pass(ex):
        lo = first_s[ex]
        return lo, jnp.minimum(lo + pass_chunks, first_s[ex + 1])

    def for_chunks(buf, c_lo, c_hi, fn):
        def one(c):
            fn(hl_ref.at[pl.ds(pl.multiple_of(chunk_s[c], RUN_ALIGN), RUN_ALIGN)],
               xbuf.at[buf, pl.ds(pl.multiple_of((c - c_lo) * RUN_ALIGN, RUN_ALIGN), RUN_ALIGN)])

        def block(i, carry):
            for u in range(CHUNK_UNROLL):
                one(c_lo + i * CHUNK_UNROLL + u)
            return carry

        def tail(c, carry):
            one(c)
            return carry

        n_blocks = (c_hi - c_lo) // CHUNK_UNROLL
        lax.fori_loop(0, n_blocks, block, 0)
        lax.fori_loop(c_lo + n_blocks * CHUNK_UNROLL, c_hi, tail, 0)

    def gather(buf, c_lo, c_hi):
        for_chunks(buf, c_lo, c_hi, lambda hbm, vm: pltpu.make_async_copy(
            hbm, vm, gather_sem(buf)).start())

    def scatter(buf, c_lo, c_hi):
        for_chunks(buf, c_lo, c_hi, lambda hbm, vm: pltpu.make_async_copy(
            vm, hbm, scatter_sem(buf)).start())

    def wait_chunks(which_sem, n):
        def wait_rows(rows):
            def body(c, carry):
                pltpu.make_async_copy(hl_ref.at[pl.ds(0, rows)], xbuf.at[0, pl.ds(0, rows)],
                                      which_sem).wait()
                return carry
            return body

        lax.fori_loop(0, n // CHUNK_UNROLL, wait_rows(CHUNK_UNROLL * RUN_ALIGN), 0)
        lax.fori_loop(0, n % CHUNK_UNROLL, wait_rows(RUN_ALIGN), 0)

    def evaluate(buf, n_chunks):
        def tile(k, carry):
            sl = pl.ds(pl.multiple_of(k * EXPERT_TILE, EXPERT_TILE), EXPERT_TILE)
            gu = _dot(xbuf[buf, sl, :], wgu_s[...])
            gate = gu[:, :D_EXPERT]
            act = gate * jax.nn.sigmoid(gate) * gu[:, D_EXPERT:]
            xbuf[buf, sl, :] = _dot(act.astype(BF16), wd_s[...]).astype(BF16)
            return carry

        lax.fori_loop(0, (n_chunks * RUN_ALIGN + EXPERT_TILE - 1) // EXPERT_TILE, tile, 0)

    @pl.when(e == 0)
    def _():
        xbuf[...] = jnp.zeros_like(xbuf)
        gather(0, *first_pass(0))
        drain_s[0] = 0

    wgu_s[:, :D_EXPERT] = wg_ref[0].astype(BF16)
    wgu_s[:, D_EXPERT:] = wu_ref[0].astype(BF16)
    wd_s[...] = wd_ref[0].astype(BF16)

    c_lo, c_hi = first_pass(e)
    wait_chunks(gather_sem(mine), c_hi - c_lo)
    wait_chunks(scatter_sem(other), drain_s[0])

    @pl.when(e + 1 < pl.num_programs(0))
    def _():
        gather(other, *first_pass(e + 1))

    evaluate(mine, c_hi - c_lo)
    scatter(mine, c_lo, c_hi)

    def later_pass(state):
        lo, pending = state
        hi = jnp.minimum(lo + pass_chunks, first_s[e + 1])
        wait_chunks(scatter_sem(mine), pending)
        gather(mine, lo, hi)
        wait_chunks(gather_sem(mine), hi - lo)
        evaluate(mine, hi - lo)
        scatter(mine, lo, hi)
        return hi, hi - lo

    _, pending = lax.while_loop(lambda st: st[0] < first_s[e + 1], later_pass,
                                (c_hi, c_hi - c_lo))
    drain_s[0] = pending

    @pl.when(e + 1 == pl.num_programs(0))
    def _():
        wait_chunks(scatter_sem(mine), pending)


def _combine_kernel(x2_ref, route_ref, gf_ref, yl_ref, o_ref):
    slots = lax.broadcasted_iota(jnp.int32, (MOE_SUB, RUN_SLOTS), 1).astype(F32)
    for g in range(x2_ref.shape[0] // MOE_SUB):
        p = slice(g * MOE_SUB, (g + 1) * MOE_SUB)
        route = route_ref[p, :]
        pick = (jnp.where(slots == route[:, 0:1], route[:, 2:3], 0.0)
                + jnp.where(slots == route[:, 1:2], route[:, 3:4], 0.0))
        y = _dot(pick.astype(BF16), yl_ref[g * RUN_SLOTS:(g + 1) * RUN_SLOTS, :])
        o_ref[p, :] = _rms(x2_ref[p, :] + y, gf_ref[...])


def _full(shape):
    return pl.BlockSpec(shape, lambda *_: (0,) * len(shape))


def _params(sem, vmem_limit=VMEM_LIMIT):
    return pltpu.CompilerParams(dimension_semantics=sem, vmem_limit_bytes=vmem_limit)


def _mixer_a_layout(t):
    s_rkv = 3 * A_WIDTH
    s_w = s_rkv + D_DECAY_LORA
    s_a = s_w + D_AAA_LORA
    gap = lambda n: jnp.zeros((t.shape[0], n), t.dtype)
    return jnp.concatenate(
        [t[:, :s_w], gap(XA_OFF - D_DECAY_LORA), t[:, s_w:s_a], gap(XG_OFF - XA_OFF - D_AAA_LORA),
         t[:, s_a:], gap(LORA_PAD - XG_OFF - D_GATE_LORA)], axis=1)


def _place(cols, parts):
    out = jnp.zeros((parts[0][1].shape[0], cols), parts[0][1].dtype)
    for off, arr in parts:
        out = lax.dynamic_update_slice(out, arr, (0, off))
    return out


def kernel(x, norm1_g, w_in, b_gate, tmix_mu, w0, w2, a0, a2, g2, k_k, k_a, r_k, lnx_g, lnx_b,
           w_oA, lnv_g, lnv_b, w_s, b_s, w_oB, w_out, norm2_g, w_rg, b_rg, w_re, b_re,
           w_e_gate, w_e_up, w_e_down, final_g):
    bsz, seq, d = x.shape
    n_tok = bsz * seq
    depth = norm1_g.shape[0]
    assert depth == 1, "the moe kernel fuses the final norm, so it must be the last layer"
    assert bsz % WKV_SEQS == 0 and seq % WKV_TOKENS == 0
    xf = x.reshape(n_tok, d)

    s_rkv = 3 * A_WIDTH
    s_w = s_rkv + D_DECAY_LORA
    s_a = s_w + D_AAA_LORA
    a_cols = s_a + D_GATE_LORA
    b_cols = 2 * B_WIDTH

    ones_bd = (jnp.arange(A_WIDTH)[:, None] // A_HEAD
               == jnp.arange(A_WIDTH)[None, :] // A_HEAD).astype(BF16)

    tm_a = INPROJ_TILE
    tm_p = POST_TILE
    tm_c = COMBINE_TILE
    assert tm_p % MOE_SUB == 0 and tm_c % MOE_SUB == 0 and tm_a % DENSE_SPLIT == 0
    assert seq % tm_a == 0, "an inproj tile must not straddle two sequences (token shift)"

    for l in range(depth):
        wl = w_in[l]
        n_w = 4
        col_blk = lambda rows: pl.BlockSpec((rows, d // n_w), lambda i: (0, i))
        row_blk = lambda rows: pl.BlockSpec((rows // n_w, d), lambda i: (i, 0))
        w_a, w_b, w_g, w_oa, w_ob, w_o = pl.pallas_call(
            _w_in_kernel,
            grid=(n_w,),
            in_specs=[col_blk(wl.shape[1]), row_blk(A_WIDTH), row_blk(B_WIDTH), row_blk(d)],
            out_specs=[col_blk(A_PROJ), col_blk(b_cols), col_blk(2 * d),
                       row_blk(A_WIDTH), row_blk(B_WIDTH), row_blk(d)],
            out_shape=[jax.ShapeDtypeStruct((A_PROJ, d), BF16),
                       jax.ShapeDtypeStruct((b_cols, d), BF16),
                       jax.ShapeDtypeStruct((2 * d, d), BF16),
                       jax.ShapeDtypeStruct((A_WIDTH, d), BF16),
                       jax.ShapeDtypeStruct((B_WIDTH, d), BF16),
                       jax.ShapeDtypeStruct((d, d), BF16)],
            compiler_params=_params(("parallel",)),
            name="w_in_layout",
        )(wl.T, w_oA[l], w_oB[l], w_out[l])
        mu_a = _mixer_a_layout(tmix_mu[l][None, :])
        w2p = jnp.pad(w2[l], ((0, LANES - D_DECAY_LORA), (0, 0))).astype(BF16)
        a2p = jnp.pad(a2[l], ((0, LANES - D_AAA_LORA), (0, 0))).astype(BF16)
        g2p = jnp.pad(g2[l], ((0, LORA_PAD - XG_OFF - D_GATE_LORA), (0, 0))).astype(BF16)
        g1 = norm1_g[l][None, :]

        row512 = lambda i: (i, 0)
        tok_a = pl.BlockSpec((tm_a, A_WIDTH), row512)
        tok_d = pl.BlockSpec((tm_a, d), row512)
        vec_a = _full((1, A_WIDTH))
        bs_full = jnp.repeat(b_s[l].T, B_GROUP_CH, axis=1)
        outs = pl.pallas_call(
            functools.partial(_inproj_kernel, seq // tm_a),
            grid=(n_tok // tm_a,),
            in_specs=[tok_d,
                      _full((1, d)), _full((A_PROJ, d)), _full((1, A_PROJ)), vec_a,
                      _full((LANES, A_WIDTH)), vec_a, _full((LANES, A_WIDTH)),
                      _full((LORA_PAD - XG_OFF, A_WIDTH)), vec_a, vec_a,
                      _full((A_WIDTH, A_WIDTH)),
                      _full((b_cols, d)), _full((2 * d, d)), _full((1, 2 * d)),
                      _full((1, B_WIDTH)), _full((1, B_WIDTH)),
                      _full((B_GROUPS, GMLP_BLOCK, GMLP_BLOCK)), _full((GMLP_BLOCK, B_WIDTH)),
                      _full((B_WIDTH, d))],
            out_specs=[tok_a] * 7 + [tok_d, tok_d],
            out_shape=[jax.ShapeDtypeStruct((n_tok, A_WIDTH), F32)] * 7
            + [jax.ShapeDtypeStruct((n_tok, d), BF16)] * 2,
            scratch_shapes=[pltpu.VMEM((8, A_PROJ), F32)],
            compiler_params=_params(("arbitrary",)),
            name="inproj",
        )(xf, g1, w_a, mu_a, w0[l][None, :], w2p, a0[l][None, :], a2p, g2p,
          k_k[l][None, :], k_a[l][None, :], ones_bd, w_b, w_g, b_gate[l][None, :],
          lnv_g[l][None, :], lnv_b[l][None, :], w_s[l], bs_full, w_ob)
        r_, k_, v_, a_, b_, ld_, gg_, ybg, ga = outs

        n_pairs = A_WIDTH // HEAD_PAIR
        n_chunks = WKV_TOKENS // WKV_CHUNK
        tok_w = pl.BlockSpec((WKV_SEQS, WKV_TOKENS, A_WIDTH), lambda bi, ti: (bi, ti, 0))
        vec_w = _full((1, A_WIDTH))
        seq_major = lambda t: t.reshape(bsz, seq, A_WIDTH)
        tok_scratch = pltpu.VMEM((WKV_SEQS, WKV_TOKENS, A_WIDTH), BF16)
        ya_in = pl.pallas_call(
            _wkv_kernel,
            grid=(bsz // WKV_SEQS, seq // WKV_TOKENS),
            in_specs=[tok_w] * 7 + [vec_w, vec_w, vec_w, _full((HEAD_PAIR, HEAD_PAIR))],
            out_specs=tok_w,
            out_shape=jax.ShapeDtypeStruct((bsz, seq, A_WIDTH), BF16),
            scratch_shapes=[pltpu.VMEM((WKV_SEQS, n_pairs, HEAD_PAIR, HEAD_PAIR), F32)]
            + [tok_scratch] * 5
            + [pltpu.VMEM((WKV_SEQS, n_chunks, n_pairs, HEAD_PAIR, HEAD_PAIR), BF16),
               pltpu.VMEM((WKV_SEQS, n_chunks, n_pairs, HEAD_PAIR, HEAD_PAIR), F32),
               pltpu.VMEM((WKV_SEQS, WKV_TOKENS, A_WIDTH), F32)],
            compiler_params=_params(("parallel", "arbitrary")),
            name="wkv",
        )(*[seq_major(t) for t in (r_, k_, v_, a_, b_, ld_, gg_)], r_k[l].reshape(1, A_WIDTH),
          lnx_g[l][None, :], lnx_b[l][None, :],
          ones_bd[:HEAD_PAIR, :HEAD_PAIR]).reshape(n_tok, A_WIDTH)

        w_r = _place(LANES, [(0, jnp.transpose(w_re[l], (1, 0, 2)).reshape(d, N_EXPERTS)),
                             (N_EXPERTS, w_rg[l])])
        b_r = _place(LANES, [(0, b_re[l].reshape(1, N_EXPERTS)), (N_EXPERTS, b_rg[l][None, :])])
        wr_hi = w_r.astype(BF16)
        wr_lo = (w_r - wr_hi.astype(F32)).astype(BF16)
        wr_t = jnp.concatenate([wr_hi.T, wr_lo.T], axis=0)
        tok_p = pl.BlockSpec((tm_p, d), row512)
        lane_p = pl.BlockSpec((tm_p, LANES), row512)
        n_sub = n_tok // MOE_SUB
        sorted_rows = tm_p // MOE_SUB * RUN_SLOTS
        x2, h_sorted, route, cnt = pl.pallas_call(
            _post_kernel,
            grid=(n_tok // tm_p,),
            in_specs=[tok_p, pl.BlockSpec((tm_p, A_WIDTH), row512), tok_p, tok_p,
                      _full((A_WIDTH, d)), _full((d, d)), _full((1, d)), _full((2 * LANES, d)),
                      _full((LANES, 1))],
            out_specs=[tok_p, pl.BlockSpec((sorted_rows, d), row512), lane_p,
                       pl.BlockSpec((8, LANES), row512)],
            out_shape=[jax.ShapeDtypeStruct((n_tok, d), F32),
                       jax.ShapeDtypeStruct((n_sub * RUN_SLOTS, d), BF16),
                       jax.ShapeDtypeStruct((n_tok, LANES), F32),
                       jax.ShapeDtypeStruct((n_tok // tm_p * 8, LANES), F32)],
            compiler_params=_params(("parallel",), POST_VMEM_LIMIT),
            name="post",
        )(xf, ya_in, ga, ybg, w_oa, w_o,
          norm2_g[l][None, :], wr_t, b_r.T)

        run_chunks = cnt.reshape(n_tok // tm_p, 8, LANES)[:, :tm_p // MOE_SUB, :N_EXPERTS]
        run_chunks = run_chunks.reshape(n_sub, N_EXPERTS).astype(jnp.int32)
        run_start = (jnp.cumsum(run_chunks, axis=1) - run_chunks) * RUN_ALIGN
        group_end = jnp.cumsum(run_chunks, axis=0)
        first_chunk = jnp.concatenate([jnp.zeros((1,), jnp.int32), jnp.cumsum(group_end[-1])])
        pos = jnp.arange(n_sub * RUN_SLOTS // RUN_ALIGN, dtype=jnp.int32)
        e_of = jnp.minimum(jnp.sum(pos[:, None] >= first_chunk[None, 1:], axis=1), N_EXPERTS - 1)
        of_e = (e_of[:, None] == jnp.arange(N_EXPERTS, dtype=jnp.int32)[None, :])
        column = lambda t: jnp.dot(of_e.astype(F32), t.T.astype(F32),
                                   precision=lax.Precision.HIGHEST).astype(jnp.int32)
        in_expert = pos - jnp.sum(jnp.where(of_e, first_chunk[None, :-1], 0), axis=1)
        ends = column(group_end)
        g_of = jnp.minimum(jnp.sum(in_expert[:, None] >= ends, axis=1), n_sub - 1)
        of_g = g_of[:, None] == jnp.arange(n_sub, dtype=jnp.int32)[None, :]
        at_g = lambda t: jnp.sum(jnp.where(of_g, t, 0), axis=1)
        in_run = in_expert - at_g(ends - column(run_chunks))
        chunk_rows = g_of * RUN_SLOTS + at_g(column(run_start)) + in_run * RUN_ALIGN

        per_expert = lambda e, *_: (e, 0, 0)
        y_sorted = pl.pallas_call(
            _expert_kernel,
            grid_spec=pltpu.PrefetchScalarGridSpec(
                num_scalar_prefetch=2, grid=(N_EXPERTS,),
                in_specs=[pl.BlockSpec(memory_space=pl.ANY),
                          pl.BlockSpec((1, d, D_EXPERT), per_expert),
                          pl.BlockSpec((1, d, D_EXPERT), per_expert),
                          pl.BlockSpec((1, D_EXPERT, d), per_expert)],
                out_specs=pl.BlockSpec(memory_space=pl.ANY),
                scratch_shapes=[pltpu.VMEM((2, EXPERT_ROWS, d), BF16),
                                pltpu.VMEM((d, 2 * D_EXPERT), BF16),
                                pltpu.VMEM((D_EXPERT, d), BF16),
                                pltpu.SemaphoreType.DMA((4,)),
                                pltpu.SMEM((1,), jnp.int32)]),
            out_shape=jax.ShapeDtypeStruct((n_sub * RUN_SLOTS, d), BF16),
            input_output_aliases={2: 0},
            compiler_params=_params(("arbitrary",)),
            name="moe_experts",
        )(chunk_rows, first_chunk, h_sorted, w_e_gate[l], w_e_up[l], w_e_down[l])

        xf = pl.pallas_call(
            _combine_kernel,
            grid=(n_tok // tm_c,),
            in_specs=[pl.BlockSpec((tm_c, d), row512), pl.BlockSpec((tm_c, LANES), row512),
                      _full((1, d)), pl.BlockSpec((tm_c // MOE_SUB * RUN_SLOTS, d), row512)],
            out_specs=pl.BlockSpec((tm_c, d), row512),
            out_shape=jax.ShapeDtypeStruct((n_tok, d), F32),
            compiler_params=_params(("parallel",)),
            name="moe_combine",
        )(x2, route, final_g[None, :], y_sorted)

    return xf.reshape(bsz, seq, d)
```
